```python
import jax, jax.numpy as jnp
from jax import lax
import numpy as np

D_MODEL = 1024
BATCH = 8
SEQ = 4096
DEPTH = 2

N_HEADS = 16
HEAD_DIM = 64
ATTN_WIDTH = N_HEADS * HEAD_DIM
DILATED_PATTERNS = ((128, 1), (512, 4), (2048, 16))
N_GROUPS = len(DILATED_PATTERNS)
FOX_Q_BLOCK = 128
ROT_DIM = HEAD_DIM // 4
ROPE_THETA = 500000.0
D_FF = -(-8 * D_MODEL // (3 * 256)) * 256
RMS_EPS = 1e-6
NEG_INF = -1e30
N_MIXERS = 2
N_A_LAYERS = (DEPTH + 1) // 2
N_B_LAYERS = DEPTH // 2

kernel_name = "hybrid_dilated_fox_swiglu"


def rmsnorm(x, g):
    xf = x.astype(jnp.float32)
    y = xf * lax.rsqrt(jnp.mean(xf * xf, axis=-1, keepdims=True) + RMS_EPS) * g.astype(jnp.float32)
    return y.astype(x.dtype)


def partial_rotary(t, pos):
    half = ROT_DIM // 2
    inv_freq = ROPE_THETA ** (-jnp.arange(half, dtype=jnp.float32) * 2.0 / ROT_DIM)
    ang = pos[:, None] * inv_freq[None, :]
    cos = jnp.cos(ang)[None, :, None, :]
    sin = jnp.sin(ang)[None, :, None, :]
    t1 = t[..., :half].astype(jnp.float32)
    t2 = t[..., half:ROT_DIM].astype(jnp.float32)
    rot = jnp.concatenate([t1 * cos - t2 * sin, t2 * cos + t1 * sin], axis=-1).astype(t.dtype)
    return jnp.concatenate([rot, t[..., ROT_DIM:]], axis=-1)


def dilated_band_attention(q, k, v, dil, steps):
    B, S, H, D = q.shape
    L = S // dil
    blk = steps
    nb = -(-L // blk)
    Lp = nb * blk

    def by_stride(t):
        t = t.reshape(B, L, dil, H, D).transpose(0, 2, 1, 3, 4)
        return jnp.pad(t, ((0, 0), (0, 0), (0, Lp - L), (0, 0), (0, 0)))

    def band(t):
        tp = jnp.pad(t, ((0, 0), (0, 0), (blk, 0), (0, 0), (0, 0))).reshape(B, dil, nb + 1, blk, H, D)
        return jnp.concatenate([tp[:, :, :-1], tp[:, :, 1:]], axis=3)

    qb = by_stride(q).reshape(B, dil, nb, blk, H, D)
    kb = band(by_stride(k))
    vb = band(by_stride(v))

    s = jnp.einsum('bcnihd,bcnjhd->bcnhij', qb, kb).astype(jnp.float32) * (D ** -0.5)
    i = jnp.arange(blk)[:, None]
    j = jnp.arange(2 * blk)[None, :]
    diff = i + blk - j
    key_step = (jnp.arange(nb)[:, None] - 1) * blk + jnp.arange(2 * blk)[None, :]
    valid = ((diff >= 0) & (diff <= steps))[None, :, :] & (key_step >= 0)[:, None, :]
    s = jnp.where(valid[None, None, :, None, :, :], s, NEG_INF)

    m = jnp.max(s, axis=-1, keepdims=True)
    p = jnp.exp(s - m)
    den = jnp.sum(p, axis=-1, keepdims=True)
    o = jnp.einsum('bcnhij,bcnjhd->bcnihd', p / den, vb.astype(jnp.float32))
    lse = (m + jnp.log(den))[..., 0]

    o = o.reshape(B, dil, Lp, H, D)[:, :, :L].transpose(0, 2, 1, 3, 4).reshape(B, S, H, D)
    lse = lse.transpose(0, 1, 2, 4, 3).reshape(B, dil, Lp, H)[:, :, :L].transpose(0, 2, 1, 3).reshape(B, S, H)
    return o, lse


def dilated_mixer(h, w_in, w_out):
    B, S, _ = h.shape
    proj = (h @ w_in).reshape(B, S, N_GROUPS, 3, N_HEADS, HEAD_DIM)
    pos = jnp.arange(S, dtype=jnp.float32)
    outs, lses = [], []
    for g, (window, dil) in enumerate(DILATED_PATTERNS):
        q = partial_rotary(proj[:, :, g, 0], pos)
        k = partial_rotary(proj[:, :, g, 1], pos)
        v = proj[:, :, g, 2]
        o, lse = dilated_band_attention(q, k, v, dil, window // dil)
        outs.append(o)
        lses.append(lse)
    wts = jax.nn.softmax(jnp.stack(lses, axis=0), axis=0)
    o = jnp.einsum('gbsh,gbshd->bshd', wts, jnp.stack(outs, axis=0))
    return o.reshape(B, S, ATTN_WIDTH).astype(h.dtype) @ w_out


def forgetting_mixer(h, w_in, b_f, w_out):
    B, S, _ = h.shape
    proj = h @ w_in
    qkv = proj[..., :3 * ATTN_WIDTH].reshape(B, S, 3, N_HEADS, HEAD_DIM)
    q, k, v = qkv[:, :, 0], qkv[:, :, 1], qkv[:, :, 2]
    log_f = jax.nn.log_sigmoid(proj[..., 3 * ATTN_WIDTH:].astype(jnp.float32) + b_f.astype(jnp.float32))
    c = lax.cumsum(log_f, axis=1)
    nq = S // FOX_Q_BLOCK
    qb = q.reshape(B, nq, FOX_Q_BLOCK, N_HEADS, HEAD_DIM).transpose(1, 0, 2, 3, 4)
    cqb = c.reshape(B, nq, FOX_Q_BLOCK, N_HEADS).transpose(1, 0, 3, 2)
    ck = c.transpose(0, 2, 1)
    key_pos = jnp.arange(S)
    vf = v.astype(jnp.float32)
    scale = HEAD_DIM ** -0.5

    def block(args):
        qi, ci, n = args
        s = jnp.einsum('bihd,bjhd->bhij', qi, k).astype(jnp.float32) * scale
        s = s + ci[..., None] - ck[:, :, None, :]
        qpos = n * FOX_Q_BLOCK + jnp.arange(FOX_Q_BLOCK)
        s = jnp.where(key_pos[None, :] <= qpos[:, None], s, NEG_INF)
        p = jax.nn.softmax(s, axis=-1)
        return jnp.einsum('bhij,bjhd->bihd', p, vf)

    o = lax.map(block, (qb, cqb, jnp.arange(nq)))
    o = o.transpose(1, 0, 2, 3, 4).reshape(B, S, ATTN_WIDTH).astype(h.dtype)
    return o @ w_out


def swiglu(h, w_gu, w_down):
    gu = h @ w_gu
    g, u = gu[..., :D_FF], gu[..., D_FF:]
    return (jax.nn.silu(g) * u) @ w_down


def _fwd_setup_inputs(seed: int = 0) -> dict:
    key = jax.random.key(seed)
    ks = jax.random.split(key, 13)
    f32 = jnp.float32
    x = jax.random.normal(ks[0], (BATCH, SEQ, D_MODEL), f32)
    a_norm = 1.0 + 0.02 * jax.random.normal(ks[1], (N_A_LAYERS, D_MODEL), f32)
    a_w_in = jax.random.normal(ks[2], (N_A_LAYERS, D_MODEL, N_GROUPS * 3 * ATTN_WIDTH), f32) * D_MODEL ** -0.5
    a_w_out = jax.random.normal(ks[3], (N_A_LAYERS, ATTN_WIDTH, D_MODEL), f32) * ATTN_WIDTH ** -0.5
    b_norm = 1.0 + 0.02 * jax.random.normal(ks[4], (N_B_LAYERS, D_MODEL), f32)
    b_w_in = jax.random.normal(ks[5], (N_B_LAYERS, D_MODEL, 3 * ATTN_WIDTH + N_HEADS), f32) * D_MODEL ** -0.5
    b_f = jnp.linspace(1.0, 6.0, N_HEADS, dtype=f32)[None, :] + 0.1 * jax.random.normal(ks[6], (N_B_LAYERS, N_HEADS), f32)
    b_w_out = jax.random.normal(ks[7], (N_B_LAYERS, ATTN_WIDTH, D_MODEL), f32) * ATTN_WIDTH ** -0.5
    ffn_norm = 1.0 + 0.02 * jax.random.normal(ks[8], (DEPTH, D_MODEL), f32)
    ffn_w_gu = jax.random.normal(ks[9], (DEPTH, D_MODEL, 2 * D_FF), f32) * D_MODEL ** -0.5
    ffn_w_down = jax.random.normal(ks[10], (DEPTH, D_FF, D_MODEL), f32) * D_FF ** -0.5
    final_norm = 1.0 + 0.02 * jax.random.normal(ks[11], (D_MODEL,), f32)
    return {"x": x, "a_norm": a_norm, "a_w_in": a_w_in, "a_w_out": a_w_out,
            "b_norm": b_norm, "b_w_in": b_w_in, "b_f": b_f, "b_w_out": b_w_out,
            "ffn_norm": ffn_norm, "ffn_w_gu": ffn_w_gu, "ffn_w_down": ffn_w_down,
            "final_norm": final_norm}


def _fwd_reference(x, a_norm, a_w_in, a_w_out, b_norm, b_w_in, b_f, b_w_out,
              ffn_norm, ffn_w_gu, ffn_w_down, final_norm):
    h = x
    for i in range(DEPTH):
        j = i // N_MIXERS
        if i % N_MIXERS == 0:
            h = h + dilated_mixer(rmsnorm(h, a_norm[j]), a_w_in[j], a_w_out[j])
        else:
            h = h + forgetting_mixer(rmsnorm(h, b_norm[j]), b_w_in[j], b_f[j], b_w_out[j])
        h = h + swiglu(rmsnorm(h, ffn_norm[i]), ffn_w_gu[i], ffn_w_down[i])
    return rmsnorm(h, final_norm)


import jax as _jax
import jax.numpy as _jnp

TWIN_FORMAT = 'train_step'
FWD_PARAMS = ['x', 'a_norm', 'a_w_in', 'a_w_out', 'b_norm', 'b_w_in', 'b_f', 'b_w_out', 'ffn_norm', 'ffn_w_gu', 'ffn_w_down', 'final_norm']
TWIN_WEIGHTS = ['a_norm', 'a_w_in', 'a_w_out', 'b_norm', 'b_w_in', 'b_f', 'b_w_out', 'ffn_norm', 'ffn_w_gu', 'ffn_w_down', 'final_norm']
TWIN_DIFF_INPUT = 'x'
TWIN_INPUTS = ['x', 'a_norm', 'a_w_in', 'a_w_out', 'b_norm', 'b_w_in', 'b_f', 'b_w_out', 'ffn_norm', 'ffn_w_gu', 'ffn_w_down', 'final_norm', 'loss_target', 'm_a_norm', 'm_a_w_in', 'm_a_w_out', 'm_b_norm', 'm_b_w_in', 'm_b_f', 'm_b_w_out', 'm_ffn_norm', 'm_ffn_w_gu', 'm_ffn_w_down', 'm_final_norm', 'v_a_norm', 'v_a_w_in', 'v_a_w_out', 'v_b_norm', 'v_b_w_in', 'v_b_f', 'v_b_w_out', 'v_ffn_norm', 'v_ffn_w_gu', 'v_ffn_w_down', 'v_final_norm']
TWIN_OUTPUTS = ['loss', 'grad_x', 'grad_a_norm', 'grad_a_w_in', 'grad_a_w_out', 'grad_b_norm', 'grad_b_w_in', 'grad_b_f', 'grad_b_w_out', 'grad_ffn_norm', 'grad_ffn_w_gu', 'grad_ffn_w_down', 'grad_final_norm', 'delta_a_norm', 'delta_a_w_in', 'delta_a_w_out', 'delta_b_norm', 'delta_b_w_in', 'delta_b_f', 'delta_b_w_out', 'delta_ffn_norm', 'delta_ffn_w_gu', 'delta_ffn_w_down', 'delta_final_norm', 'new_m_a_norm', 'new_m_a_w_in', 'new_m_a_w_out', 'new_m_b_norm', 'new_m_b_w_in', 'new_m_b_f', 'new_m_b_w_out', 'new_m_ffn_norm', 'new_m_ffn_w_gu', 'new_m_ffn_w_down', 'new_m_final_norm', 'new_v_a_norm', 'new_v_a_w_in', 'new_v_a_w_out', 'new_v_b_norm', 'new_v_b_w_in', 'new_v_b_f', 'new_v_b_w_out', 'new_v_ffn_norm', 'new_v_ffn_w_gu', 'new_v_ffn_w_down', 'new_v_final_norm']
TWIN_LEAF_KINDS = {'loss': 'loss', 'grad_x': 'grad_x', 'grad_a_norm': 'grad_w', 'grad_a_w_in': 'grad_w', 'grad_a_w_out': 'grad_w', 'grad_b_norm': 'grad_w', 'grad_b_w_in': 'grad_w', 'grad_b_f': 'grad_w', 'grad_b_w_out': 'grad_w', 'grad_ffn_norm': 'grad_w', 'grad_ffn_w_gu': 'grad_w', 'grad_ffn_w_down': 'grad_w', 'grad_final_norm': 'grad_w', 'delta_a_norm': 'delta_w', 'delta_a_w_in': 'delta_w', 'delta_a_w_out': 'delta_w', 'delta_b_norm': 'delta_w', 'delta_b_w_in': 'delta_w', 'delta_b_f': 'delta_w', 'delta_b_w_out': 'delta_w', 'delta_ffn_norm': 'delta_w', 'delta_ffn_w_gu': 'delta_w', 'delta_ffn_w_down': 'delta_w', 'delta_final_norm': 'delta_w', 'new_m_a_norm': 'new_m', 'new_m_a_w_in': 'new_m', 'new_m_a_w_out': 'new_m', 'new_m_b_norm': 'new_m', 'new_m_b_w_in': 'new_m', 'new_m_b_f': 'new_m', 'new_m_b_w_out': 'new_m', 'new_m_ffn_norm': 'new_m', 'new_m_ffn_w_gu': 'new_m', 'new_m_ffn_w_down': 'new_m', 'new_m_final_norm': 'new_m', 'new_v_a_norm': 'new_v', 'new_v_a_w_in': 'new_v', 'new_v_a_w_out': 'new_v', 'new_v_b_norm': 'new_v', 'new_v_b_w_in': 'new_v', 'new_v_b_f': 'new_v', 'new_v_b_w_out': 'new_v', 'new_v_ffn_norm': 'new_v', 'new_v_ffn_w_gu': 'new_v', 'new_v_ffn_w_down': 'new_v', 'new_v_final_norm': 'new_v'}


def _forward(args):
    return _fwd_reference(*[args[k] for k in FWD_PARAMS])


def _output_shape():
    out = _jax.eval_shape(lambda: _forward(_fwd_setup_inputs(0)))
    return out.shape, out.dtype

N_MICROBATCH = 1
ADAM_LR = 0.001
ADAM_B1 = 0.9
ADAM_B2 = 0.999
ADAM_EPS = 1e-08
ADAM_WD = 0.01
ADAM_STEP = 10
PER_EXAMPLE_BATCH_AXIS = {'x': 0, 'loss_target': 0}
SHARED_INPUTS = []
_WEIGHT_DTYPES = {'a_norm': _jnp.float32, 'a_w_in': _jnp.float32, 'a_w_out': _jnp.float32, 'b_norm': _jnp.float32, 'b_w_in': _jnp.float32, 'b_f': _jnp.float32, 'b_w_out': _jnp.float32, 'ffn_norm': _jnp.float32, 'ffn_w_gu': _jnp.float32, 'ffn_w_down': _jnp.float32, 'final_norm': _jnp.float32}
MOMENT_SCALE = {'a_norm': 6.897195e-02, 'a_w_in': 2.254001e-02, 'a_w_out': 4.316106e-02, 'b_norm': 7.765384e-02, 'b_w_in': 4.512618e-02, 'b_f': 5.401738e-01, 'b_w_out': 5.372569e-02, 'ffn_norm': 1.419234e-01, 'ffn_w_gu': 5.653475e-02, 'ffn_w_down': 9.244710e-02, 'final_norm': 3.199976e+01}


def _to_microbatches(a, axis):
    t = _jnp.moveaxis(a, axis, 0)
    t = t.reshape((N_MICROBATCH, t.shape[0] // N_MICROBATCH) + t.shape[1:])
    return _jnp.moveaxis(t, 1, axis + 1)


def setup_inputs(seed: int = 0) -> dict:
    inp = _fwd_setup_inputs(seed)
    key = _jax.random.fold_in(_jax.random.key(seed), 7919)
    shape, _ = _output_shape()
    out = dict(inp)
    out["loss_target"] = _jax.random.normal(_jax.random.fold_in(key, 0), shape, _jnp.float32)
    for i, name in enumerate(TWIN_WEIGHTS):
        w = inp[name].astype(_jnp.float32)
        if MOMENT_SCALE is None:
            s = _jnp.sqrt(_jnp.mean(_jnp.square(w)) + 1e-30)
        else:
            s = MOMENT_SCALE[name]
        km, kv = _jax.random.split(_jax.random.fold_in(key, i + 1))
        out[name] = w
        out["m_" + name] = s * _jax.random.normal(km, w.shape, _jnp.float32)
        out["v_" + name] = (s * s) * _jax.random.uniform(kv, w.shape, _jnp.float32, 0.5, 1.5)
    if N_MICROBATCH > 1:
        for name, axis in PER_EXAMPLE_BATCH_AXIS.items():
            out[name] = _to_microbatches(out[name], axis)
    return {'x': out['x'], 'a_norm': out['a_norm'], 'a_w_in': out['a_w_in'], 'a_w_out': out['a_w_out'], 'b_norm': out['b_norm'], 'b_w_in': out['b_w_in'], 'b_f': out['b_f'], 'b_w_out': out['b_w_out'], 'ffn_norm': out['ffn_norm'], 'ffn_w_gu': out['ffn_w_gu'], 'ffn_w_down': out['ffn_w_down'], 'final_norm': out['final_norm'], 'loss_target': out['loss_target'], 'm_a_norm': out['m_a_norm'], 'm_a_w_in': out['m_a_w_in'], 'm_a_w_out': out['m_a_w_out'], 'm_b_norm': out['m_b_norm'], 'm_b_w_in': out['m_b_w_in'], 'm_b_f': out['m_b_f'], 'm_b_w_out': out['m_b_w_out'], 'm_ffn_norm': out['m_ffn_norm'], 'm_ffn_w_gu': out['m_ffn_w_gu'], 'm_ffn_w_down': out['m_ffn_w_down'], 'm_final_norm': out['m_final_norm'], 'v_a_norm': out['v_a_norm'], 'v_a_w_in': out['v_a_w_in'], 'v_a_w_out': out['v_a_w_out'], 'v_b_norm': out['v_b_norm'], 'v_b_w_in': out['v_b_w_in'], 'v_b_f': out['v_b_f'], 'v_b_w_out': out['v_b_w_out'], 'v_ffn_norm': out['v_ffn_norm'], 'v_ffn_w_gu': out['v_ffn_w_gu'], 'v_ffn_w_down': out['v_ffn_w_down'], 'v_final_norm': out['v_final_norm']}


def _loss(weights, diff, rest, loss_target):
    with _jax.named_scope("forward"):
        args = {**rest, TWIN_DIFF_INPUT: diff, **{k: w.astype(_WEIGHT_DTYPES[k]) for k, w in weights.items()}}
        y = _forward(args)
    with _jax.named_scope("loss_head"):
        err = _jnp.square(y.astype(_jnp.float32) - loss_target)
        return 0.5 * _jnp.sum(_jnp.mean(err, axis=-1)) if err.ndim else 0.5 * err


def _adamw(w, g, m, v):
    m = ADAM_B1 * m + (1.0 - ADAM_B1) * g
    v = ADAM_B2 * v + (1.0 - ADAM_B2) * _jnp.square(g)
    m_hat = m / (1.0 - ADAM_B1 ** ADAM_STEP)
    v_hat = v / (1.0 - ADAM_B2 ** ADAM_STEP)
    delta = -ADAM_LR * (m_hat / (_jnp.sqrt(v_hat) + ADAM_EPS) + ADAM_WD * w)
    return delta, m, v


def reference(x, a_norm, a_w_in, a_w_out, b_norm, b_w_in, b_f, b_w_out, ffn_norm, ffn_w_gu, ffn_w_down, final_norm, loss_target, m_a_norm, m_a_w_in, m_a_w_out, m_b_norm, m_b_w_in, m_b_f, m_b_w_out, m_ffn_norm, m_ffn_w_gu, m_ffn_w_down, m_final_norm, v_a_norm, v_a_w_in, v_a_w_out, v_b_norm, v_b_w_in, v_b_f, v_b_w_out, v_ffn_norm, v_ffn_w_gu, v_ffn_w_down, v_final_norm):
    given = dict(x=x, a_norm=a_norm, a_w_in=a_w_in, a_w_out=a_w_out, b_norm=b_norm, b_w_in=b_w_in, b_f=b_f, b_w_out=b_w_out, ffn_norm=ffn_norm, ffn_w_gu=ffn_w_gu, ffn_w_down=ffn_w_down, final_norm=final_norm, loss_target=loss_target, m_a_norm=m_a_norm, m_a_w_in=m_a_w_in, m_a_w_out=m_a_w_out, m_b_norm=m_b_norm, m_b_w_in=m_b_w_in, m_b_f=m_b_f, m_b_w_out=m_b_w_out, m_ffn_norm=m_ffn_norm, m_ffn_w_gu=m_ffn_w_gu, m_ffn_w_down=m_ffn_w_down, m_final_norm=m_final_norm, v_a_norm=v_a_norm, v_a_w_in=v_a_w_in, v_a_w_out=v_a_w_out, v_b_norm=v_b_norm, v_b_w_in=v_b_w_in, v_b_f=v_b_f, v_b_w_out=v_b_w_out, v_ffn_norm=v_ffn_norm, v_ffn_w_gu=v_ffn_w_gu, v_ffn_w_down=v_ffn_w_down, v_final_norm=v_final_norm)
    weights = {n: given[n] for n in TWIN_WEIGHTS}
    shared = {n: given[n] for n in SHARED_INPUTS}
    per_example = {n: given[n] for n in ['x']}
    grad_fn = _jax.value_and_grad(_loss, argnums=(0, 1))

    def one_microbatch(ex, loss_target):
        ex = dict(ex)
        diff = ex.pop(TWIN_DIFF_INPUT)
        return grad_fn(weights, diff, {**shared, **ex}, loss_target)

    if N_MICROBATCH == 1:
        loss, (grad_w, grad_x) = one_microbatch(per_example, given["loss_target"])
    else:
        def body(carry, xs):
            loss_sum, grad_sum = carry
            l_k, (gw_k, gx_k) = one_microbatch(xs[0], xs[1])
            with _jax.named_scope("update"):
                return (loss_sum + l_k, _jax.tree.map(_jnp.add, grad_sum, gw_k)), gx_k

        init = (_jnp.zeros((), _jnp.float32), _jax.tree.map(_jnp.zeros_like, weights))
        (loss, grad_w), grad_x = _jax.lax.scan(body, init, (per_example, given["loss_target"]))
    with _jax.named_scope("update"):
        delta_w, new_m, new_v = {}, {}, {}
        for n in TWIN_WEIGHTS:
            delta_w[n], new_m[n], new_v[n] = _adamw(weights[n], grad_w[n], given["m_" + n], given["v_" + n])
    return (loss, grad_x, *[grad_w[n] for n in TWIN_WEIGHTS], *[delta_w[n] for n in TWIN_WEIGHTS],
            *[new_m[n] for n in TWIN_WEIGHTS], *[new_v[n] for n in TWIN_WEIGHTS])
```

```python
import functools

import jax
import jax.numpy as jnp
from jax import lax
from jax.experimental import pallas as pl
from jax.experimental.pallas import tpu as pltpu

F32 = jnp.float32
BF = jnp.bfloat16

D_MODEL = 1024
N_HEADS = 16
HEAD_DIM = 64
D_FF = 2816
DILATED_PATTERNS = ((128, 1), (512, 4), (2048, 16))
ROT_DIM = 16
ROPE_THETA = 500000.0
RMS_EPS = 1e-6
NEG_INF = -1e30
ATTN_SCALE = HEAD_DIM ** -0.5
GATE_LANES = 128
N_CHIPS = 4
MESH_AXES = ("x", "y", "c")
MESH = pl.DeviceIdType.MESH

ADAM_LR = 0.001
ADAM_B1 = 0.9
ADAM_B2 = 0.999
ADAM_EPS = 1e-08
ADAM_WD = 0.01
ADAM_STEP = 10

VMEM_LIMIT_BYTES = 56 * 1024 * 1024


def _params(*sem):
    return pltpu.CompilerParams(dimension_semantics=sem, vmem_limit_bytes=VMEM_LIMIT_BYTES)


def _rope_rotate(t, cos, sin_a, sin_b):
    outs = []
    for cidx in range(t.shape[1] // 128):
        tc = t[:, cidx * 128:(cidx + 1) * 128]
        outs.append(tc * cos + pltpu.roll(tc, 120, 1) * sin_a + pltpu.roll(tc, 8, 1) * sin_b)
    return jnp.concatenate(outs, axis=1)


def _mm_nn(a, b, *, tm, tn, out_dtype, name, resid=None, rope=None):
    M, K = a.shape
    N = b.shape[1]
    assert M % tm == 0 and N % tn == 0 and b.shape[0] == K
    n_in = 2 + (resid is not None) + (3 if rope is not None else 0)

    def body(*refs):
        a_ref, b_ref = refs[0], refs[1]
        o_ref = refs[n_in]
        acc = jnp.dot(a_ref[...], b_ref[...], preferred_element_type=F32)
        if resid is not None:
            acc = acc + refs[2][...]
        if rope is not None:
            cos_ref, sa_ref, sb_ref = refs[n_in - 3:n_in]
            j = pl.program_id(1)

            @pl.when(j % 3 != 2)
            def _():
                o_ref[...] = _rope_rotate(acc, cos_ref[...], sa_ref[...], sb_ref[...]).astype(out_dtype)

            @pl.when(j % 3 == 2)
            def _():
                o_ref[...] = acc.astype(out_dtype)
        else:
            o_ref[...] = acc.astype(out_dtype)

    in_specs = [pl.BlockSpec((tm, K), lambda i, j: (i, 0)), pl.BlockSpec((K, tn), lambda i, j: (0, j))]
    args = [a, b]
    if resid is not None:
        in_specs.append(pl.BlockSpec((tm, tn), lambda i, j: (i, j)))
        args.append(resid)
    if rope is not None:
        assert tn == 1024
        for t in rope:
            in_specs.append(pl.BlockSpec((tm, 128), lambda i, j: (i, 0)))
            args.append(t)
    return pl.pallas_call(
        body, grid=(M // tm, N // tn), in_specs=in_specs,
        out_specs=pl.BlockSpec((tm, tn), lambda i, j: (i, j)),
        out_shape=jax.ShapeDtypeStruct((M, N), out_dtype), name=name,
        compiler_params=_params("parallel", "arbitrary"),
    )(*args)


def _mm_nt(a, b, *, tm, to, tn, out_dtype, name, add=None):
    M, N = a.shape
    O = b.shape[0]
    assert M % tm == 0 and O % to == 0 and N % tn == 0 and b.shape[1] == N
    nk = N // tn

    def body(*refs):
        a_ref, b_ref = refs[0], refs[1]
        o_ref, acc_ref = refs[-2], refs[-1]
        k = pl.program_id(2)

        @pl.when(k == 0)
        def _():
            if add is not None:
                acc_ref[...] = refs[2][...]
            else:
                acc_ref[...] = jnp.zeros_like(acc_ref)

        acc_ref[...] += lax.dot_general(a_ref[...], b_ref[...], (((1,), (1,)), ((), ())),
                                        preferred_element_type=F32)

        @pl.when(k == nk - 1)
        def _():
            o_ref[...] = acc_ref[...].astype(out_dtype)

    in_specs = [pl.BlockSpec((tm, tn), lambda i, j, k: (i, k)), pl.BlockSpec((to, tn), lambda i, j, k: (j, k))]
    args = [a, b]
    if add is not None:
        in_specs.append(pl.BlockSpec((tm, to), lambda i, j, k: (i, j)))
        args.append(add)
    return pl.pallas_call(
        body, grid=(M // tm, O // to, nk), in_specs=in_specs,
        out_specs=pl.BlockSpec((tm, to), lambda i, j, k: (i, j)),
        out_shape=jax.ShapeDtypeStruct((M, O), out_dtype),
        scratch_shapes=[pltpu.VMEM((tm, to), F32)], name=name,
        compiler_params=_params("parallel", "parallel", "arbitrary"),
    )(*args)


def _mm_tn(a, b, *, tk, tn, tm, out_dtype, name):
    M, K = a.shape
    N = b.shape[1]
    assert M % tm == 0 and K % tk == 0 and N % tn == 0 and b.shape[0] == M
    nm = M // tm

    def body(a_ref, b_ref, o_ref, acc_ref):
        m = pl.program_id(2)

        @pl.when(m == 0)
        def _():
            acc_ref[...] = jnp.zeros_like(acc_ref)

        acc_ref[...] += lax.dot_general(a_ref[...], b_ref[...], (((0,), (0,)), ((), ())),
                                        preferred_element_type=F32)

        @pl.when(m == nm - 1)
        def _():
            o_ref[...] = acc_ref[...].astype(out_dtype)

    return pl.pallas_call(
        body, grid=(K // tk, N // tn, nm),
        in_specs=[pl.BlockSpec((tm, tk), lambda i, j, m: (m, i)), pl.BlockSpec((tm, tn), lambda i, j, m: (m, j))],
        out_specs=pl.BlockSpec((tk, tn), lambda i, j, m: (i, j)),
        out_shape=jax.ShapeDtypeStruct((K, N), out_dtype),
        scratch_shapes=[pltpu.VMEM((tk, tn), F32)], name=name,
        compiler_params=_params("parallel", "parallel", "arbitrary"),
    )(a, b)


ROW_TILE = 512


def _rms_fwd(x, g, *, name):
    S, Dm = x.shape

    def body(x_ref, g_ref, o_ref):
        xf = x_ref[...]
        r = lax.rsqrt(jnp.mean(xf * xf, axis=-1, keepdims=True) + RMS_EPS)
        o_ref[...] = (xf * r * g_ref[...]).astype(BF)

    return pl.pallas_call(
        body, grid=(S // ROW_TILE,),
        in_specs=[pl.BlockSpec((ROW_TILE, Dm), lambda i: (i, 0)), pl.BlockSpec((1, Dm), lambda i: (0, 0))],
        out_specs=pl.BlockSpec((ROW_TILE, Dm), lambda i: (i, 0)),
        out_shape=jax.ShapeDtypeStruct((S, Dm), BF), name=name, compiler_params=_params("parallel"),
    )(x, g)


def _rms_bwd(x, g, dn, dres, *, name):
    S, Dm = x.shape

    def body(x_ref, g_ref, dn_ref, dres_ref, dx_ref, dxb_ref, dg_ref):
        i = pl.program_id(0)
        xf = x_ref[...]
        r = lax.rsqrt(jnp.mean(xf * xf, axis=-1, keepdims=True) + RMS_EPS)
        xh = xf * r
        dnf = dn_ref[...]
        dyg = dnf * g_ref[...]
        dx = dres_ref[...] + r * (dyg - xh * jnp.mean(dyg * xh, axis=-1, keepdims=True))
        dx_ref[...] = dx
        dxb_ref[...] = dx.astype(BF)

        @pl.when(i == 0)
        def _():
            dg_ref[...] = jnp.zeros_like(dg_ref)

        dg_ref[...] += jnp.sum(dnf * xh, axis=0, keepdims=True)

    row = pl.BlockSpec((ROW_TILE, Dm), lambda i: (i, 0))
    vec = pl.BlockSpec((1, Dm), lambda i: (0, 0))
    return pl.pallas_call(
        body, grid=(S // ROW_TILE,), in_specs=[row, vec, row, row], out_specs=[row, row, vec],
        out_shape=[jax.ShapeDtypeStruct((S, Dm), F32), jax.ShapeDtypeStruct((S, Dm), BF),
                   jax.ShapeDtypeStruct((1, Dm), F32)],
        name=name, compiler_params=_params("arbitrary"),
    )(x, g, dn, dres)


def _loss_head(h, g, tgt, *, name):
    S, Dm = h.shape

    def body(h_ref, g_ref, t_ref, loss_ref, dh_ref, dhb_ref, dg_ref):
        i = pl.program_id(0)
        xf = h_ref[...]
        r = lax.rsqrt(jnp.mean(xf * xf, axis=-1, keepdims=True) + RMS_EPS)
        xh = xf * r
        gv = g_ref[...]
        err = xh * gv - t_ref[...]
        dy = err * (1.0 / Dm)
        dyg = dy * gv
        dh = r * (dyg - xh * jnp.mean(dyg * xh, axis=-1, keepdims=True))
        dh_ref[...] = dh
        dhb_ref[...] = dh.astype(BF)

        @pl.when(i == 0)
        def _():
            dg_ref[...] = jnp.zeros_like(dg_ref)
            loss_ref[...] = jnp.zeros_like(loss_ref)

        dg_ref[...] += jnp.sum(dy * xh, axis=0, keepdims=True)
        part = 0.5 * jnp.sum(jnp.mean(err * err, axis=-1, keepdims=True), axis=0, keepdims=True)
        loss_ref[...] += jnp.broadcast_to(part, loss_ref.shape)

    row = pl.BlockSpec((ROW_TILE, Dm), lambda i: (i, 0))
    vec = pl.BlockSpec((1, Dm), lambda i: (0, 0))
    return pl.pallas_call(
        body, grid=(S // ROW_TILE,), in_specs=[row, vec, row],
        out_specs=[pl.BlockSpec((1, 128), lambda i: (0, 0)), row, row, vec],
        out_shape=[jax.ShapeDtypeStruct((1, 128), F32), jax.ShapeDtypeStruct((S, Dm), F32),
                   jax.ShapeDtypeStruct((S, Dm), BF), jax.ShapeDtypeStruct((1, Dm), F32)],
        name=name, compiler_params=_params("arbitrary"),
    )(h, g, tgt)


SWIGLU_ROWS = 256


def _swiglu_fwd(gu, *, name):
    S = gu.shape[0]

    def body(g_ref, u_ref, o_ref):
        g = g_ref[...].astype(F32)
        sig = 1.0 / (1.0 + jnp.exp(-g))
        o_ref[...] = (g * sig * u_ref[...].astype(F32)).astype(BF)

    return pl.pallas_call(
        body, grid=(S // SWIGLU_ROWS,),
        in_specs=[pl.BlockSpec((SWIGLU_ROWS, D_FF), lambda i: (i, 0)), pl.BlockSpec((SWIGLU_ROWS, D_FF), lambda i: (i, 1))],
        out_specs=pl.BlockSpec((SWIGLU_ROWS, D_FF), lambda i: (i, 0)),
        out_shape=jax.ShapeDtypeStruct((S, D_FF), BF), name=name, compiler_params=_params("parallel"),
    )(gu, gu)


def _swiglu_bwd(gu, dact, *, name):
    S = gu.shape[0]

    def body(g_ref, u_ref, d_ref, o_ref):
        g = g_ref[...].astype(F32)
        u = u_ref[...].astype(F32)
        d = d_ref[...].astype(F32)
        sig = 1.0 / (1.0 + jnp.exp(-g))
        o_ref[:, :D_FF] = (d * u * sig * (1.0 + g * (1.0 - sig))).astype(BF)
        o_ref[:, D_FF:] = (d * g * sig).astype(BF)

    return pl.pallas_call(
        body, grid=(S // SWIGLU_ROWS,),
        in_specs=[pl.BlockSpec((SWIGLU_ROWS, D_FF), lambda i: (i, 0)), pl.BlockSpec((SWIGLU_ROWS, D_FF), lambda i: (i, 1)),
                  pl.BlockSpec((SWIGLU_ROWS, D_FF), lambda i: (i, 0))],
        out_specs=pl.BlockSpec((SWIGLU_ROWS, 2 * D_FF), lambda i: (i, 0)),
        out_shape=jax.ShapeDtypeStruct((S, 2 * D_FF), BF), name=name, compiler_params=_params("parallel"),
    )(gu, gu, dact)


def _attn_fwd(qa, ka, va, qcb, kcb, vcb, *, dil, T, nkv, window, name, c=None, cT=None, o_dtype=F32):
    L = qa.shape[0]
    nq = L // T
    fox = c is not None

    def kv_block(n, j):
        return n - (nkv - 1) + j

    def body(*refs):
        if fox:
            q_ref, k_ref, v_ref, c_ref, ct_ref, o_ref, lse_ref, m_sc, l_sc, acc_sc = refs
        else:
            q_ref, k_ref, v_ref, o_ref, lse_ref, m_sc, l_sc, acc_sc = refs
        n = pl.program_id(1)
        j = pl.program_id(2)
        kb = kv_block(n, j)

        @pl.when(j == 0)
        def _():
            m_sc[...] = jnp.full(m_sc.shape, NEG_INF, F32)
            l_sc[...] = jnp.zeros_like(l_sc)
            acc_sc[...] = jnp.zeros_like(acc_sc)

        @pl.when(kb >= 0)
        def _():
            diff = (n * T + lax.broadcasted_iota(jnp.int32, (T, T), 0)) - (kb * T + lax.broadcasted_iota(jnp.int32, (T, T), 1))
            valid = diff >= 0
            if window is not None:
                valid = jnp.logical_and(valid, diff <= window)
            for h in range(N_HEADS):
                hs = slice(h * HEAD_DIM, (h + 1) * HEAD_DIM)
                qh = q_ref[:, hs] * jnp.asarray(ATTN_SCALE, BF)
                s = lax.dot_general(qh, k_ref[:, hs], (((1,), (1,)), ((), ())), preferred_element_type=F32)
                if fox:
                    s = s + c_ref[:, h:h + 1] - ct_ref[h:h + 1, :]
                s = jnp.where(valid, s, NEG_INF)
                m_prev = m_sc[:, h:h + 1]
                m_new = jnp.maximum(m_prev, jnp.max(s, axis=1, keepdims=True))
                alpha = jnp.exp(m_prev - m_new)
                p = jnp.exp(s - m_new)
                l_sc[:, h:h + 1] = alpha * l_sc[:, h:h + 1] + jnp.sum(p, axis=1, keepdims=True)
                acc_sc[:, hs] = alpha * acc_sc[:, hs] + jnp.dot(p.astype(BF), v_ref[:, hs], preferred_element_type=F32)
                m_sc[:, h:h + 1] = m_new

        @pl.when(j == nkv - 1)
        def _():
            lane = lax.broadcasted_iota(jnp.int32, (T, 128), 1)
            lse = jnp.zeros((T, 128), F32)
            for h in range(N_HEADS):
                hs = slice(h * HEAD_DIM, (h + 1) * HEAD_DIM)
                l = l_sc[:, h:h + 1]
                o_ref[:, hs] = (acc_sc[:, hs] / l).astype(o_dtype)
                lse = jnp.where(lane == h, m_sc[:, h:h + 1] + jnp.log(l), lse)
            lse_ref[...] = lse

    def kvi(n, j):
        return jnp.maximum(kv_block(n, j), 0)

    in_specs = [pl.BlockSpec((T, 1024), lambda r, n, j: (n, qcb(r))),
                pl.BlockSpec((T, 1024), lambda r, n, j: (kvi(n, j), kcb(r))),
                pl.BlockSpec((T, 1024), lambda r, n, j: (kvi(n, j), vcb(r)))]
    args = [qa, ka, va]
    if fox:
        in_specs += [pl.BlockSpec((T, GATE_LANES), lambda r, n, j: (n, 0)),
                     pl.BlockSpec((GATE_LANES, T), lambda r, n, j: (0, kvi(n, j)))]
        args += [c, cT]
    return pl.pallas_call(
        body, grid=(dil, nq, nkv), in_specs=in_specs,
        out_specs=[pl.BlockSpec((T, 1024), lambda r, n, j: (n, r)), pl.BlockSpec((T, 128), lambda r, n, j: (n, r))],
        out_shape=[jax.ShapeDtypeStruct((L, dil * 1024), o_dtype), jax.ShapeDtypeStruct((L, dil * 128), F32)],
        scratch_shapes=[pltpu.VMEM((T, 128), F32), pltpu.VMEM((T, 128), F32), pltpu.VMEM((T, 1024), F32)],
        name=name, compiler_params=_params("parallel", "parallel", "arbitrary"),
    )(*args)


def _attn_bwd(qa, ka, va, qcb, kcb, vcb, doa, oa, lsea, *, dil, T, nqs, window, name, c=None, cT=None):
    L = qa.shape[0]
    nq = L // T
    fox = c is not None

    def body(*refs):
        if fox:
            (q_ref, k_ref, v_ref, do_ref, o_ref, lse_ref, c_ref, ct_ref,
             dq_ref, dk_ref, dv_ref, dct_ref, dcq_ref, dq_sc, dk_sc, dv_sc, dc_sc, dcq_sc) = refs
        else:
            (q_ref, k_ref, v_ref, do_ref, o_ref, lse_ref,
             dq_ref, dk_ref, dv_ref, dq_sc, dk_sc, dv_sc) = refs
        kb = pl.program_id(1)
        jq = pl.program_id(2)
        qb = kb + jq

        @pl.when(jnp.logical_and(kb == 0, jq == 0))
        def _():
            dq_sc[...] = jnp.zeros_like(dq_sc)
            if fox:
                dcq_sc[...] = jnp.zeros_like(dcq_sc)

        @pl.when(jq == 0)
        def _():
            dk_sc[...] = jnp.zeros_like(dk_sc)
            dv_sc[...] = jnp.zeros_like(dv_sc)
            if fox:
                dc_sc[...] = jnp.zeros_like(dc_sc)

        @pl.when(qb < nq)
        def _():
            diff = (qb * T + lax.broadcasted_iota(jnp.int32, (T, T), 0)) - (kb * T + lax.broadcasted_iota(jnp.int32, (T, T), 1))
            valid = diff >= 0
            if window is not None:
                valid = jnp.logical_and(valid, diff <= window)
            qrows = pl.ds(pl.multiple_of(qb * T, T), T)
            for h in range(N_HEADS):
                hs = slice(h * HEAD_DIM, (h + 1) * HEAD_DIM)
                qh = q_ref[:, hs] * jnp.asarray(ATTN_SCALE, BF)
                kh = k_ref[:, hs]
                doh = do_ref[:, hs]
                s = lax.dot_general(qh, kh, (((1,), (1,)), ((), ())), preferred_element_type=F32)
                if fox:
                    s = s + c_ref[:, h:h + 1] - ct_ref[h:h + 1, :]
                s = jnp.where(valid, s, NEG_INF)
                p = jnp.exp(s - lse_ref[:, h:h + 1])
                dp = lax.dot_general(doh, v_ref[:, hs], (((1,), (1,)), ((), ())), preferred_element_type=F32)
                delta = jnp.sum(doh.astype(F32) * o_ref[:, hs].astype(F32), axis=1, keepdims=True)
                ds = p * (dp - delta)
                dsb = ds.astype(BF)
                dv_sc[:, hs] += lax.dot_general(p.astype(BF), doh, (((0,), (0,)), ((), ())), preferred_element_type=F32)
                dk_sc[:, hs] += lax.dot_general(dsb, qh, (((0,), (0,)), ((), ())), preferred_element_type=F32)
                dq_sc[qrows, hs] += jnp.dot(dsb, kh, preferred_element_type=F32) * ATTN_SCALE
                if fox:
                    dc_sc[h:h + 1, :] -= jnp.sum(ds, axis=0, keepdims=True)
                    dcq_sc[qrows, h:h + 1] += jnp.sum(ds, axis=1, keepdims=True)

        @pl.when(jq == nqs - 1)
        def _():
            dk_ref[...] = dk_sc[...].astype(BF)
            dv_ref[...] = dv_sc[...].astype(BF)
            if fox:
                dct_ref[...] = dc_sc[...]

        @pl.when(jnp.logical_and(kb == nq - 1, jq == nqs - 1))
        def _():
            def put(i, carry):
                rows = pl.ds(pl.multiple_of(i * T, T), T)
                dq_ref[rows, :] = dq_sc[rows, :].astype(BF)
                return carry
            lax.fori_loop(0, nq, put, 0)
            if fox:
                dcq_ref[...] = dcq_sc[...]

    def qi(kb, jq):
        return jnp.minimum(kb + jq, nq - 1)

    in_specs = [pl.BlockSpec((T, 1024), lambda r, kb, jq: (qi(kb, jq), qcb(r))),
                pl.BlockSpec((T, 1024), lambda r, kb, jq: (kb, kcb(r))),
                pl.BlockSpec((T, 1024), lambda r, kb, jq: (kb, vcb(r))),
                pl.BlockSpec((T, 1024), lambda r, kb, jq: (qi(kb, jq), r)),
                pl.BlockSpec((T, 1024), lambda r, kb, jq: (qi(kb, jq), r)),
                pl.BlockSpec((T, 128), lambda r, kb, jq: (qi(kb, jq), r))]
    args = [qa, ka, va, doa, oa, lsea]
    out_specs = [pl.BlockSpec((L, 1024), lambda r, kb, jq: (0, r)),
                 pl.BlockSpec((T, 1024), lambda r, kb, jq: (kb, r)),
                 pl.BlockSpec((T, 1024), lambda r, kb, jq: (kb, r))]
    out_shape = [jax.ShapeDtypeStruct((L, dil * 1024), BF)] * 3
    scratch = [pltpu.VMEM((L, 1024), F32), pltpu.VMEM((T, 1024), F32), pltpu.VMEM((T, 1024), F32)]
    if fox:
        in_specs += [pl.BlockSpec((T, GATE_LANES), lambda r, kb, jq: (qi(kb, jq), 0)),
                     pl.BlockSpec((GATE_LANES, T), lambda r, kb, jq: (0, kb))]
        args += [c, cT]
        out_specs.append(pl.BlockSpec((GATE_LANES, T), lambda r, kb, jq: (0, kb)))
        out_shape.append(jax.ShapeDtypeStruct((GATE_LANES, L), F32))
        scratch.append(pltpu.VMEM((GATE_LANES, T), F32))
        out_specs.append(pl.BlockSpec((L, GATE_LANES), lambda r, kb, jq: (0, 0)))
        out_shape.append(jax.ShapeDtypeStruct((L, GATE_LANES), F32))
        scratch.append(pltpu.VMEM((L, GATE_LANES), F32))
    return pl.pallas_call(
        body, grid=(dil, nq, nqs), in_specs=in_specs, out_specs=out_specs, out_shape=out_shape,
        scratch_shapes=scratch, name=name, compiler_params=_params("arbitrary", "arbitrary", "arbitrary"),
    )(*args)


def _combine_groups(os, lses, *, name):
    S = os[0].shape[0]
    ng = len(os)

    def body(*refs):
        o_refs, l_refs = refs[:ng], refs[ng:2 * ng]
        out_ref, lse_ref = refs[2 * ng], refs[2 * ng + 1]
        ls = [r[...] for r in l_refs]
        m = functools.reduce(jnp.maximum, ls)
        es = [jnp.exp(l - m) for l in ls]
        den = functools.reduce(jnp.add, es)
        ws = [e / den for e in es]
        lse_ref[...] = m + jnp.log(den)
        for h in range(N_HEADS):
            hs = slice(h * HEAD_DIM, (h + 1) * HEAD_DIM)
            acc = ws[0][:, h:h + 1] * o_refs[0][:, hs]
            for g in range(1, ng):
                acc = acc + ws[g][:, h:h + 1] * o_refs[g][:, hs]
            out_ref[:, hs] = acc.astype(BF)

    row = pl.BlockSpec((ROW_TILE, 1024), lambda i: (i, 0))
    stat = pl.BlockSpec((ROW_TILE, 128), lambda i: (i, 0))
    return pl.pallas_call(
        body, grid=(S // ROW_TILE,), in_specs=[row] * ng + [stat] * ng, out_specs=[row, stat],
        out_shape=[jax.ShapeDtypeStruct((S, 1024), BF), jax.ShapeDtypeStruct((S, 128), F32)],
        name=name, compiler_params=_params("parallel"),
    )(*os, *lses)


def _assemble(parts, rope_flags, rope, *, name):
    S = parts[0].shape[0]
    n = len(parts)
    use_rope = any(rope_flags)

    def body(*refs):
        out_ref = refs[-1]
        for b in range(n):
            cols = slice(b * 1024, (b + 1) * 1024)
            if rope_flags[b]:
                cos_ref, sa_ref, sb_ref = refs[n:n + 3]
                out_ref[:, cols] = _rope_rotate(refs[b][...].astype(F32), cos_ref[...], sa_ref[...], sb_ref[...]).astype(BF)
            else:
                out_ref[:, cols] = refs[b][...]

    tm = 256
    row = pl.BlockSpec((tm, 1024), lambda i: (i, 0))
    in_specs = [row] * n
    args = list(parts)
    if use_rope:
        in_specs += [pl.BlockSpec((tm, 128), lambda i: (i, 0))] * 3
        args += list(rope)
    return pl.pallas_call(
        body, grid=(S // tm,), in_specs=in_specs, out_specs=pl.BlockSpec((tm, n * 1024), lambda i: (i, 0)),
        out_shape=jax.ShapeDtypeStruct((S, n * 1024), BF), name=name, compiler_params=_params("parallel"),
    )(*args)


GATE_ROWS = 512


def _gate_fwd(z, bf, *, name):
    S = z.shape[0]

    def body(z_ref, b_ref, c_ref, ct_ref, carry):
        i = pl.program_id(0)

        @pl.when(i == 0)
        def _():
            carry[...] = jnp.zeros_like(carry)

        zz = z_ref[...] + b_ref[...]
        logf = jnp.minimum(zz, 0.0) - jnp.log(1.0 + jnp.exp(-jnp.abs(zz)))
        tri = (lax.broadcasted_iota(jnp.int32, (GATE_ROWS, GATE_ROWS), 0)
               >= lax.broadcasted_iota(jnp.int32, (GATE_ROWS, GATE_ROWS), 1)).astype(F32)
        cs = jnp.dot(tri, logf, precision=lax.Precision.HIGHEST, preferred_element_type=F32) + carry[...]
        c_ref[...] = cs
        ct_ref[...] = cs.T
        carry[...] = cs[GATE_ROWS - 1:GATE_ROWS, :]

    return pl.pallas_call(
        body, grid=(S // GATE_ROWS,),
        in_specs=[pl.BlockSpec((GATE_ROWS, GATE_LANES), lambda i: (i, 0)), pl.BlockSpec((1, GATE_LANES), lambda i: (0, 0))],
        out_specs=[pl.BlockSpec((GATE_ROWS, GATE_LANES), lambda i: (i, 0)), pl.BlockSpec((GATE_LANES, GATE_ROWS), lambda i: (0, i))],
        out_shape=[jax.ShapeDtypeStruct((S, GATE_LANES), F32), jax.ShapeDtypeStruct((GATE_LANES, S), F32)],
        scratch_shapes=[pltpu.VMEM((1, GATE_LANES), F32)], name=name, compiler_params=_params("arbitrary"),
    )(z, bf)


def _gate_bwd(z, bf, dcT, dcq, *, name):
    S = z.shape[0]
    nb = S // GATE_ROWS

    def body(z_ref, b_ref, dct_ref, dcq_ref, dz_ref, db_ref, carry):
        i = pl.program_id(0)

        @pl.when(i == 0)
        def _():
            carry[...] = jnp.zeros_like(carry)
            db_ref[...] = jnp.zeros_like(db_ref)

        dc = dct_ref[...].T + dcq_ref[...]
        tri = (lax.broadcasted_iota(jnp.int32, (GATE_ROWS, GATE_ROWS), 0)
               <= lax.broadcasted_iota(jnp.int32, (GATE_ROWS, GATE_ROWS), 1)).astype(F32)
        dl = jnp.dot(tri, dc, precision=lax.Precision.HIGHEST, preferred_element_type=F32) + carry[...]
        carry[...] = dl[0:1, :]
        zz = z_ref[...] + b_ref[...]
        dz = dl * (1.0 / (1.0 + jnp.exp(zz)))
        lane = lax.broadcasted_iota(jnp.int32, dz.shape, 1)
        dz = jnp.where(lane < N_HEADS, dz, 0.0)
        dz_ref[...] = dz.astype(BF)
        db_ref[...] += jnp.sum(dz, axis=0, keepdims=True)

    return pl.pallas_call(
        body, grid=(nb,),
        in_specs=[pl.BlockSpec((GATE_ROWS, GATE_LANES), lambda i: (nb - 1 - i, 0)), pl.BlockSpec((1, GATE_LANES), lambda i: (0, 0)),
                  pl.BlockSpec((GATE_LANES, GATE_ROWS), lambda i: (0, nb - 1 - i)),
                  pl.BlockSpec((GATE_ROWS, GATE_LANES), lambda i: (nb - 1 - i, 0))],
        out_specs=[pl.BlockSpec((GATE_ROWS, GATE_LANES), lambda i: (nb - 1 - i, 0)), pl.BlockSpec((1, GATE_LANES), lambda i: (0, 0))],
        out_shape=[jax.ShapeDtypeStruct((S, GATE_LANES), BF), jax.ShapeDtypeStruct((1, GATE_LANES), F32)],
        scratch_shapes=[pltpu.VMEM((1, GATE_LANES), F32)], name=name, compiler_params=_params("arbitrary"),
    )(z, bf, dcT, dcq)


def _rope_tables(S):
    half = ROT_DIM // 2
    inv_freq = ROPE_THETA ** (-jnp.arange(half, dtype=F32) * 2.0 / ROT_DIM)
    ang = jnp.arange(S, dtype=F32)[:, None] * inv_freq[None, :]
    cos, sin = jnp.cos(ang), jnp.sin(ang)
    zero = jnp.zeros((S, HEAD_DIM - ROT_DIM), F32)
    zh = jnp.zeros((S, half), F32)
    cos_h = jnp.concatenate([cos, cos, jnp.ones_like(zero)], axis=1)
    sa_h = jnp.concatenate([-sin, zh, zero], axis=1)
    sb_h = jnp.concatenate([zh, sin, zero], axis=1)
    two = lambda t: jnp.concatenate([t, t], axis=1)
    return two(cos_h), two(sa_h), two(sb_h)


def _ffn_fwd(h, norm, w_gu, w_down, tag):
    n = _rms_fwd(h, norm, name=f"ffn{tag}_norm")
    gu = _mm_nn(n, w_gu, tm=1024, tn=512, out_dtype=BF, name=f"ffn{tag}_gu")
    act = _swiglu_fwd(gu, name=f"ffn{tag}_act")
    out = _mm_nn(act, w_down, tm=512, tn=1024, out_dtype=F32, name=f"ffn{tag}_down", resid=h)
    return out, (h, n, gu, act)


def _ffn_bwd(dh, dhb, saved, norm, w_gu, w_down, tag):
    h, n, gu, act = saved
    dact = _mm_nt(dhb, w_down, tm=512, to=1408, tn=1024, out_dtype=BF, name=f"ffn{tag}_dact")
    dw_down = _mm_tn(act, dhb, tk=1408, tn=1024, tm=512, out_dtype=BF, name=f"ffn{tag}_dwdown")
    dgu = _swiglu_bwd(gu, dact, name=f"ffn{tag}_dgu")
    dn = _mm_nt(dgu, w_gu, tm=512, to=1024, tn=1408, out_dtype=F32, name=f"ffn{tag}_dn")
    dw_gu = _mm_tn(n, dgu, tk=1024, tn=1408, tm=512, out_dtype=BF, name=f"ffn{tag}_dwgu")
    dx, dxb, dg = _rms_bwd(h, norm, dn, dh, name=f"ffn{tag}_dnorm")
    return dx, dxb, dg, dw_gu, dw_down


def _local_step(x, tgt, w):
    S = x.shape[0]
    rope_f = _rope_tables(S)
    rope_b = (rope_f[0], -rope_f[1], -rope_f[2])
    g = {}

    n0 = _rms_fwd(x, w["a_norm"], name="a_norm")
    proj = _mm_nn(n0, w["a_w_in"], tm=512, tn=1024, out_dtype=BF, name="a_proj", rope=rope_f)
    o_parts, lse_parts = [], []
    for gi, (window, dil) in enumerate(DILATED_PATTERNS):
        L = S // dil
        pv = proj.reshape(L, dil * proj.shape[1])
        cb = lambda t, gi=gi: (lambda r: r * 9 + gi * 3 + t)
        o_g, lse_g = _attn_fwd(pv, pv, pv, cb(0), cb(1), cb(2), dil=dil, T=128, nkv=2, window=window // dil,
                               name=f"a_attn{gi}")
        o_parts.append(o_g.reshape(S, 1024))
        lse_parts.append(lse_g.reshape(S, 128))
    o_a, lse_a = _combine_groups(o_parts, lse_parts, name="a_combine")
    h1 = _mm_nn(o_a, w["a_w_out"], tm=512, tn=1024, out_dtype=F32, name="a_out", resid=x)
    h2, ffn0 = _ffn_fwd(h1, w["ffn_norm"][0:1], w["ffn_w_gu"][0], w["ffn_w_down"][0], 0)

    n2 = _rms_fwd(h2, w["b_norm"], name="b_norm")
    qkv = _mm_nn(n2, w["b_w_qkv"], tm=512, tn=1024, out_dtype=BF, name="b_proj")
    zf = _mm_nn(n2, w["b_w_f"], tm=512, tn=GATE_LANES, out_dtype=F32, name="b_gate_proj")
    c, cT = _gate_fwd(zf, w["b_f"], name="b_gate")
    cbk = lambda t: (lambda r: t)
    o_b, lse_b = _attn_fwd(qkv, qkv, qkv, cbk(0), cbk(1), cbk(2), dil=1, T=512, nkv=S // 512, window=None,
                           name="b_attn", c=c, cT=cT, o_dtype=BF)
    h3 = _mm_nn(o_b, w["b_w_out"], tm=512, tn=1024, out_dtype=F32, name="b_out", resid=h2)
    h4, ffn1 = _ffn_fwd(h3, w["ffn_norm"][1:2], w["ffn_w_gu"][1], w["ffn_w_down"][1], 1)

    loss, dh4, dh4b, g["final_norm"] = _loss_head(h4, w["final_norm"], tgt, name="loss_head")

    dh3, dh3b, dg_f1, g["ffn_w_gu1"], g["ffn_w_down1"] = _ffn_bwd(dh4, dh4b, ffn1, w["ffn_norm"][1:2], w["ffn_w_gu"][1], w["ffn_w_down"][1], 1)

    do_b = _mm_nt(dh3b, w["b_w_out"], tm=512, to=1024, tn=1024, out_dtype=BF, name="b_do")
    g["b_w_out"] = _mm_tn(o_b, dh3b, tk=1024, tn=1024, tm=512, out_dtype=BF, name="b_dwout")
    dq, dk, dv, dcT, dcq = _attn_bwd(qkv, qkv, qkv, cbk(0), cbk(1), cbk(2), do_b, o_b, lse_b, dil=1, T=256, nqs=S // 256,
                                window=None, name="b_attn_bwd", c=c, cT=cT)
    dz, g["b_f"] = _gate_bwd(zf, w["b_f"], dcT, dcq, name="b_gate_bwd")
    dqkv = _assemble([dq, dk, dv], [False] * 3, None, name="b_dproj")
    dn2 = _mm_nt(dz, w["b_w_f"], tm=512, to=1024, tn=GATE_LANES, out_dtype=F32, name="b_dn_gate")
    dn2 = _mm_nt(dqkv, w["b_w_qkv"], tm=512, to=1024, tn=1024, out_dtype=F32, name="b_dn", add=dn2)
    g["b_w_qkv"] = _mm_tn(n2, dqkv, tk=1024, tn=1024, tm=512, out_dtype=BF, name="b_dwqkv")
    g["b_w_f"] = _mm_tn(n2, dz, tk=1024, tn=GATE_LANES, tm=512, out_dtype=BF, name="b_dwf")
    dh2, dh2b, g["b_norm"] = _rms_bwd(h2, w["b_norm"], dn2, dh3, name="b_dnorm")

    dh1, dh1b, dg_f0, g["ffn_w_gu0"], g["ffn_w_down0"] = _ffn_bwd(dh2, dh2b, ffn0, w["ffn_norm"][0:1], w["ffn_w_gu"][0], w["ffn_w_down"][0], 0)
    g["ffn_norm"] = jnp.concatenate([dg_f0, dg_f1], axis=0)

    do_a = _mm_nt(dh1b, w["a_w_out"], tm=512, to=1024, tn=1024, out_dtype=BF, name="a_do")
    g["a_w_out"] = _mm_tn(o_a, dh1b, tk=1024, tn=1024, tm=512, out_dtype=BF, name="a_dwout")
    parts = []
    for gi, (window, dil) in enumerate(DILATED_PATTERNS):
        L = S // dil
        pv = proj.reshape(L, dil * proj.shape[1])
        cb = lambda t, gi=gi: (lambda r: r * 9 + gi * 3 + t)
        dq, dk, dv = _attn_bwd(pv, pv, pv, cb(0), cb(1), cb(2), do_a.reshape(L, dil * 1024), o_a.reshape(L, dil * 1024),
                               lse_a.reshape(L, dil * 128), dil=dil, T=128, nqs=2, window=window // dil,
                               name=f"a_attn_bwd{gi}")
        parts += [t.reshape(S, 1024) for t in (dq, dk, dv)]
    dproj = _assemble(parts, [True, True, False] * 3, rope_b, name="a_dproj")
    dn0 = _mm_nt(dproj, w["a_w_in"], tm=512, to=1024, tn=1024, out_dtype=F32, name="a_dn")
    g["a_w_in"] = _mm_tn(n0, dproj, tk=1024, tn=1024, tm=512, out_dtype=BF, name="a_dwin")
    dx, _, g["a_norm"] = _rms_bwd(x, w["a_norm"], dn0, dh1, name="a_dnorm")
    return loss, dx, g


ANY = pl.BlockSpec(memory_space=pl.ANY)


def _place():
    x, y, c = lax.axis_index("x"), lax.axis_index("y"), lax.axis_index("c")
    chips = [(1 - x, y), (x, 1 - y), (1 - x, 1 - y)]
    return x, y, c, chips


def _shard_slice(ref, kind, rows, cols, s, half):
    hr = rows // 2
    if kind == "col":
        return ref.at[pl.ds(half * hr, hr), pl.ds(pl.multiple_of(s * cols, 128), cols)]
    if kind == "row":
        return ref.at[pl.ds(pl.multiple_of(s * rows + half * hr, 16), hr), :]
    return ref.at[s, pl.ds(half * hr, hr), :]


def _whole_shape(kind, rows, cols):
    return {"col": (rows, N_CHIPS * cols), "row": (N_CHIPS * rows, cols), "stack": (N_CHIPS, rows, cols)}[kind]


def _gather_weights(shards, kinds):
    nw = len(shards)
    dims = [s.shape for s in shards]

    def body(*refs):
        src, dst = refs[:nw], refs[nw:2 * nw]
        send_sems, recv_sems, local_sems = refs[2 * nw:]
        x, y, c, chips = _place()
        me = 2 * x + y
        sibling = (x, y, 1 - c)

        def part(wi, s, half):
            return _shard_slice(dst[wi], kinds[wi], dims[wi][0], dims[wi][1], s, half)

        def copy(wi, k, s, half, to, src_ref=None):
            p = part(wi, s, half)
            return pltpu.make_async_remote_copy(src_ref=p if src_ref is None else src_ref, dst_ref=p,
                                                send_sem=send_sems.at[wi * 6 + k], recv_sem=recv_sems.at[wi * 6 + k],
                                                device_id=to, device_id_type=MESH)

        local, first, passed = [], [], []
        for wi in range(nw):
            hr = dims[wi][0] // 2
            for half in range(2):
                cp = pltpu.make_async_copy(src[wi].at[pl.ds(half * hr, hr), :], part(wi, me, half), local_sems.at[wi * 2 + half])
                cp.start()
                local.append(cp)
            mine = src[wi].at[pl.ds(pl.multiple_of(c * hr, 16), hr), :]
            for j, chip in enumerate(chips):
                cp = copy(wi, j, me, c, (*chip, c), src_ref=mine)
                cp.start()
                first.append(cp)
        for wi in range(nw):
            for j, chip in enumerate(chips):
                s = 2 * chip[0] + chip[1]
                copy(wi, j, s, c, (x, y, c)).wait_recv()
                cp = copy(wi, 3 + j, s, c, sibling)
                cp.start()
                passed.append(cp)
        for wi in range(nw):
            for j, chip in enumerate(chips):
                s = 2 * chip[0] + chip[1]
                copy(wi, 3 + j, s, 1 - c, (x, y, c)).wait_recv()
        for cp in first + passed:
            cp.wait_send()
        for cp in local:
            cp.wait()

    return pl.pallas_call(
        body, in_specs=[ANY] * nw, out_specs=[ANY] * nw,
        out_shape=[jax.ShapeDtypeStruct(_whole_shape(k, *d), BF) for k, d in zip(kinds, dims)],
        scratch_shapes=[pltpu.SemaphoreType.DMA((nw * 6,)), pltpu.SemaphoreType.DMA((nw * 6,)), pltpu.SemaphoreType.DMA((nw * 2,))],
        name="gather_weights",
    )(*shards)


def _scatter_grads(partials, kinds, dims):
    nw = len(partials)

    def body(*refs):
        src, dst = refs[:nw], refs[nw:2 * nw]
        send_sems, recv_sems, local_sems = refs[2 * nw:]
        x, y, c, chips = _place()
        me = 2 * x + y
        local, sent = [], []
        for wi in range(nw):
            rows, cols = dims[wi]

            def part(s, half, wi=wi, rows=rows, cols=cols):
                return _shard_slice(src[wi], kinds[wi], rows, cols, s, half)

            cp = pltpu.make_async_copy(part(me, c), dst[wi].at[7], local_sems.at[wi])
            cp.start()
            local.append(cp)
            for j, chip in enumerate(chips):
                s = 2 * chip[0] + chip[1]
                for half in range(2):
                    slot = 2 * j + (c if half == 0 else 1 - c)
                    cp = pltpu.make_async_remote_copy(
                        src_ref=part(s, half), dst_ref=dst[wi].at[slot],
                        send_sem=send_sems.at[wi * 7 + 2 * j + half], recv_sem=recv_sems.at[wi * 7 + slot],
                        device_id=(*chip, half), device_id_type=MESH)
                    cp.start()
                    sent.append(cp)
            cp = pltpu.make_async_remote_copy(
                src_ref=part(me, 1 - c), dst_ref=dst[wi].at[6],
                send_sem=send_sems.at[wi * 7 + 6], recv_sem=recv_sems.at[wi * 7 + 6],
                device_id=(x, y, 1 - c), device_id_type=MESH)
            cp.start()
            sent.append(cp)
        for wi in range(nw):
            for slot in range(7):
                pltpu.make_async_remote_copy(
                    src_ref=dst[wi].at[slot], dst_ref=dst[wi].at[slot],
                    send_sem=send_sems.at[wi * 7 + slot], recv_sem=recv_sems.at[wi * 7 + slot],
                    device_id=(x, y, c), device_id_type=MESH).wait_recv()
        for cp in sent:
            cp.wait_send()
        for cp in local:
            cp.wait()

    return pl.pallas_call(
        body, in_specs=[ANY] * nw, out_specs=[ANY] * nw,
        out_shape=[jax.ShapeDtypeStruct((8, d[0] // 2, d[1]), BF) for d in dims],
        scratch_shapes=[pltpu.SemaphoreType.DMA((nw * 7,)), pltpu.SemaphoreType.DMA((nw * 7,)), pltpu.SemaphoreType.DMA((nw,))],
        name="scatter_grads",
    )(*partials)


def _sum_slots(buf, *, name):
    _, R, C = buf.shape
    tr = R if 8 * R * C * 2 <= 6 * 1024 * 1024 else 128
    assert R % tr == 0

    def body(b_ref, o_ref):
        acc = b_ref[0].astype(F32)
        for k in range(1, 8):
            acc = acc + b_ref[k].astype(F32)
        o_ref[...] = acc

    return pl.pallas_call(
        body, grid=(R // tr,), in_specs=[pl.BlockSpec((8, tr, C), lambda i: (0, i, 0))],
        out_specs=pl.BlockSpec((tr, C), lambda i: (i, 0)),
        out_shape=jax.ShapeDtypeStruct((R, C), F32), name=name, compiler_params=_params("parallel"),
    )(buf)


def _pair_exchange(halves, groups):
    nw = len(halves)
    dims = [h.shape for h in halves]
    where = {}
    out_shape = []
    for gi, members in enumerate(groups):
        hr, cols = dims[members[0]]
        for l, wi in enumerate(members):
            where[wi] = (gi, l if len(members) > 1 else None)
        shape = (2 * hr, cols) if len(members) == 1 else (len(members), 2 * hr, cols)
        out_shape.append(jax.ShapeDtypeStruct(shape, F32))

    def body(*refs):
        src, dst = refs[:nw], refs[nw:nw + len(groups)]
        send_sems, recv_sems, local_sems = refs[nw + len(groups):]
        x, y, c, _ = _place()

        def rows_of(wi, half):
            gi, l = where[wi]
            hr = dims[wi][0]
            ref = dst[gi] if l is None else dst[gi].at[l]
            return ref.at[pl.ds(pl.multiple_of(half * hr, 8), hr), :]

        local, sent = [], []
        for wi in range(nw):
            cp = pltpu.make_async_copy(src[wi], rows_of(wi, c), local_sems.at[wi])
            cp.start()
            local.append(cp)
            cp = pltpu.make_async_remote_copy(src_ref=src[wi], dst_ref=rows_of(wi, c), send_sem=send_sems.at[wi],
                                              recv_sem=recv_sems.at[wi], device_id=(x, y, 1 - c), device_id_type=MESH)
            cp.start()
            sent.append(cp)
        for wi in range(nw):
            pltpu.make_async_remote_copy(src_ref=src[wi], dst_ref=rows_of(wi, 1 - c), send_sem=send_sems.at[wi],
                                         recv_sem=recv_sems.at[wi], device_id=(x, y, c), device_id_type=MESH).wait_recv()
        for cp in sent:
            cp.wait_send()
        for cp in local:
            cp.wait()

    return pl.pallas_call(
        body, in_specs=[ANY] * nw, out_specs=[ANY] * len(groups), out_shape=out_shape,
        scratch_shapes=[pltpu.SemaphoreType.DMA((nw,)), pltpu.SemaphoreType.DMA((nw,)), pltpu.SemaphoreType.DMA((nw,))],
        name="pair_exchange",
    )(*halves)


SMALL_ROWS = 8


def _allreduce_small(v, *, name):
    assert v.shape == (SMALL_ROWS, D_MODEL)

    def body(v_ref, o_ref, buf, send_sems, recv_sems):
        x, y, c, _ = _place()
        me = 4 * x + 2 * y + c
        buf[me] = v_ref[...]
        sent = []
        for k in range(1, 8):
            bx, by, bc = (k >> 2) & 1, (k >> 1) & 1, k & 1
            peer = (1 - x if bx else x, 1 - y if by else y, 1 - c if bc else c)
            cp = pltpu.make_async_remote_copy(src_ref=v_ref, dst_ref=buf.at[me], send_sem=send_sems.at[k - 1],
                                              recv_sem=recv_sems.at[k - 1], device_id=peer, device_id_type=MESH)
            cp.start()
            sent.append(cp)
        for k in range(1, 8):
            bx, by, bc = (k >> 2) & 1, (k >> 1) & 1, k & 1
            peer = 4 * (1 - x if bx else x) + 2 * (1 - y if by else y) + (1 - c if bc else c)
            pltpu.make_async_remote_copy(src_ref=v_ref, dst_ref=buf.at[peer], send_sem=send_sems.at[k - 1],
                                         recv_sem=recv_sems.at[k - 1], device_id=(x, y, c), device_id_type=MESH).wait_recv()
        for cp in sent:
            cp.wait_send()
        acc = buf[0]
        for d in range(1, 8):
            acc = acc + buf[d]
        o_ref[...] = acc

    vmem = pl.BlockSpec(memory_space=pltpu.VMEM)
    return pl.pallas_call(
        body, in_specs=[vmem], out_specs=vmem, out_shape=jax.ShapeDtypeStruct(v.shape, F32),
        scratch_shapes=[pltpu.VMEM((8,) + v.shape, F32), pltpu.SemaphoreType.DMA((7,)), pltpu.SemaphoreType.DMA((7,))],
        name=name,
    )(v)


def _adamw(w, g, m, v, *, name):
    R, C = w.shape
    tr = R
    if R * C * 4 > 1024 * 1024:
        tr = max(t for t in range(8, R, 8) if R % t == 0 and t * C * 4 <= 1024 * 1024)

    def body(w_ref, g_ref, m_ref, v_ref, d_ref, m2_ref, v2_ref):
        gg = g_ref[...]
        m2 = ADAM_B1 * m_ref[...] + (1.0 - ADAM_B1) * gg
        v2 = ADAM_B2 * v_ref[...] + (1.0 - ADAM_B2) * jnp.square(gg)
        m_hat = m2 / (1.0 - ADAM_B1 ** ADAM_STEP)
        v_hat = v2 / (1.0 - ADAM_B2 ** ADAM_STEP)
        d_ref[...] = -ADAM_LR * (m_hat / (jnp.sqrt(v_hat) + ADAM_EPS) + ADAM_WD * w_ref[...])
        m2_ref[...] = m2
        v2_ref[...] = v2

    blk = pl.BlockSpec((tr, C), lambda i: (i, 0))
    out = jax.ShapeDtypeStruct((R, C), F32)
    return pl.pallas_call(
        body, grid=(R // tr,), in_specs=[blk] * 4, out_specs=[blk] * 3, out_shape=[out] * 3,
        name=name, compiler_params=_params("parallel"),
    )(w, g, m, v)


WEIGHT_ORDER = ("a_norm", "a_w_in", "a_w_out", "b_norm", "b_w_in", "b_f", "b_w_out", "ffn_norm", "ffn_w_gu",
                "ffn_w_down", "final_norm")
MATRICES = (("a_w_in", 0, "col"), ("a_w_out", 0, "row"), ("b_w_in", 0, "stack"), ("b_w_out", 0, "row"),
            ("ffn_w_gu", 0, "col"), ("ffn_w_gu", 1, "col"), ("ffn_w_down", 0, "row"), ("ffn_w_down", 1, "row"))
MATRIX_GROUPS = ([0], [1], [2], [3], [4, 5], [6, 7])
GROUP_NAMES = ("a_w_in", "a_w_out", "b_w_in", "b_w_out", "ffn_w_gu", "ffn_w_down")
QKV_COLS = 3 * N_HEADS * HEAD_DIM


def kernel(x, a_norm, a_w_in, a_w_out, b_norm, b_w_in, b_f, b_w_out, ffn_norm, ffn_w_gu, ffn_w_down, final_norm, loss_target, m_a_norm, m_a_w_in, m_a_w_out, m_b_norm, m_b_w_in, m_b_f, m_b_w_out, m_ffn_norm, m_ffn_w_gu, m_ffn_w_down, m_final_norm, v_a_norm, v_a_w_in, v_a_w_out, v_b_norm, v_b_w_in, v_b_f, v_b_w_out, v_ffn_norm, v_ffn_w_gu, v_ffn_w_down, v_final_norm):
    given = dict(a_norm=a_norm, a_w_in=a_w_in, a_w_out=a_w_out, b_norm=b_norm, b_w_in=b_w_in, b_f=b_f, b_w_out=b_w_out,
                 ffn_norm=ffn_norm, ffn_w_gu=ffn_w_gu, ffn_w_down=ffn_w_down, final_norm=final_norm)
    mom_m = dict(a_norm=m_a_norm, a_w_in=m_a_w_in, a_w_out=m_a_w_out, b_norm=m_b_norm, b_w_in=m_b_w_in, b_f=m_b_f,
                 b_w_out=m_b_w_out, ffn_norm=m_ffn_norm, ffn_w_gu=m_ffn_w_gu, ffn_w_down=m_ffn_w_down, final_norm=m_final_norm)
    mom_v = dict(a_norm=v_a_norm, a_w_in=v_a_w_in, a_w_out=v_a_w_out, b_norm=v_b_norm, b_w_in=v_b_w_in, b_f=v_b_f,
                 b_w_out=v_b_w_out, ffn_norm=v_ffn_norm, ffn_w_gu=v_ffn_w_gu, ffn_w_down=v_ffn_w_down, final_norm=v_final_norm)
    chip = 2 * lax.axis_index("x") + lax.axis_index("y")
    core = lax.axis_index("c")
    bn_cols = b_norm.shape[1]

    placed = lax.dynamic_update_slice(jnp.zeros((SMALL_ROWS, D_MODEL), F32), b_norm, (0, chip * bn_cols))
    placed = placed * (core == 0).astype(F32)
    b_norm_full = _allreduce_small(placed, name="gather_b_norm")[0:1]

    shards = [given[n][l].astype(BF) for n, l, _ in MATRICES]
    kinds = [k for _, _, k in MATRICES]
    dims = [s.shape for s in shards]
    whole = _gather_weights(shards, kinds)
    b_in = whole[2].transpose(1, 0, 2).reshape(D_MODEL, -1)
    gate_cols = b_in.shape[1] - QKV_COLS
    w = dict(a_norm=a_norm, a_w_in=whole[0], a_w_out=whole[1], b_norm=b_norm_full,
             b_w_qkv=b_in[:, :QKV_COLS], b_w_f=jnp.pad(b_in[:, QKV_COLS:], ((0, 0), (0, GATE_LANES - gate_cols))),
             b_f=jnp.pad(b_f, ((0, 0), (0, GATE_LANES - gate_cols))), b_w_out=whole[3],
             ffn_norm=ffn_norm, ffn_w_gu=(whole[4], whole[5]), ffn_w_down=(whole[6], whole[7]),
             final_norm=final_norm.reshape(1, D_MODEL))

    loss, dx, g = _local_step(x[0], loss_target[0], w)

    g_b_in = jnp.concatenate([g["b_w_qkv"], g["b_w_f"][:, :gate_cols]], axis=1)
    g_b_in = g_b_in.reshape(D_MODEL, N_CHIPS, -1).transpose(1, 0, 2)
    partials = [g["a_w_in"], g["a_w_out"], g_b_in, g["b_w_out"], g["ffn_w_gu0"], g["ffn_w_gu1"],
                g["ffn_w_down0"], g["ffn_w_down1"]]
    slots = _scatter_grads(partials, kinds, dims)
    halves = [_sum_slots(b, name=f"sum_{n}{l}") for b, (n, l, _) in zip(slots, MATRICES)]
    reduced = dict(zip(GROUP_NAMES, _pair_exchange(halves, MATRIX_GROUPS)))

    small = jnp.concatenate([g["a_norm"], g["b_norm"], g["ffn_norm"], g["final_norm"],
                             jnp.pad(g["b_f"], ((0, 0), (0, D_MODEL - GATE_LANES))),
                             jnp.zeros((SMALL_ROWS - 6, D_MODEL), F32)], axis=0)
    small = _allreduce_small(small, name="allreduce_small")
    grads = dict(reduced)
    grads["a_norm"] = small[0:1]
    grads["b_norm"] = lax.dynamic_slice(small, (1, chip * bn_cols), (1, bn_cols))
    grads["ffn_norm"] = small[2:4]
    grads["final_norm"] = small[4]
    grads["b_f"] = small[5:6, :gate_cols]

    out_g, out_d, out_m, out_v = [], [], [], []
    for n in WEIGHT_ORDER:
        shape = given[n].shape
        two_d = (1, shape[0]) if len(shape) == 1 else (-1, shape[-1])
        d, m2, v2 = _adamw(given[n].reshape(two_d), grads[n].reshape(two_d), mom_m[n].reshape(two_d),
                           mom_v[n].reshape(two_d), name=f"adamw_{n}")
        out_g.append(grads[n].reshape(shape))
        out_d.append(d.reshape(shape))
        out_m.append(m2.reshape(shape))
        out_v.append(v2.reshape(shape))

    total = lax.psum(loss[0, 0], MESH_AXES)
    return (total, dx[None], *out_g, *out_d, *out_m, *out_v)
```

```python
import functools

import jax
import jax.numpy as jnp
from jax import lax
from jax.experimental import pallas as pl
from jax.experimental.pallas import tpu as pltpu

F32 = jnp.float32
BF = jnp.bfloat16

D_MODEL = 1024
N_HEADS = 16
HEAD_DIM = 64
D_FF = 2816
DILATED_PATTERNS = ((128, 1), (512, 4), (2048, 16))
ROT_DIM = 16
ROPE_THETA = 500000.0
RMS_EPS = 1e-6
NEG_INF = -1e30
ATTN_SCALE = HEAD_DIM ** -0.5
GATE_LANES = 128
N_CHIPS = 4
MESH_AXES = ("x", "y", "c")
MESH = pl.DeviceIdType.MESH

ADAM_LR = 0.001
ADAM_B1 = 0.9
ADAM_B2 = 0.999
ADAM_EPS = 1e-08
ADAM_WD = 0.01
ADAM_STEP = 10

VMEM_LIMIT_BYTES = 56 * 1024 * 1024


def _params(*sem):
    return pltpu.CompilerParams(dimension_semantics=sem, vmem_limit_bytes=VMEM_LIMIT_BYTES)


def _rope_rotate(t, cos, sin_a, sin_b):
    outs = []
    for cidx in range(t.shape[1] // 128):
        tc = t[:, cidx * 128:(cidx + 1) * 128]
        outs.append(tc * cos + pltpu.roll(tc, 120, 1) * sin_a + pltpu.roll(tc, 8, 1) * sin_b)
    return jnp.concatenate(outs, axis=1)


def _mm_nn(a, b, *, tm, tn, out_dtype, name, resid=None, rope=None):
    M, K = a.shape
    N = b.shape[1]
    assert M % tm == 0 and N % tn == 0 and b.shape[0] == K
    n_in = 2 + (resid is not None) + (3 if rope is not None else 0)

    def body(*refs):
        a_ref, b_ref = refs[0], refs[1]
        o_ref = refs[n_in]
        acc = jnp.dot(a_ref[...], b_ref[...], preferred_element_type=F32)
        if resid is not None:
            acc = acc + refs[2][...]
        if rope is not None:
            cos_ref, sa_ref, sb_ref = refs[n_in - 3:n_in]
            j = pl.program_id(1)

            @pl.when(j % 3 != 2)
            def _():
                o_ref[...] = _rope_rotate(acc, cos_ref[...], sa_ref[...], sb_ref[...]).astype(out_dtype)

            @pl.when(j % 3 == 2)
            def _():
                o_ref[...] = acc.astype(out_dtype)
        else:
            o_ref[...] = acc.astype(out_dtype)

    in_specs = [pl.BlockSpec((tm, K), lambda i, j: (i, 0)), pl.BlockSpec((K, tn), lambda i, j: (0, j))]
    args = [a, b]
    if resid is not None:
        in_specs.append(pl.BlockSpec((tm, tn), lambda i, j: (i, j)))
        args.append(resid)
    if rope is not None:
        assert tn == 1024
        for t in rope:
            in_specs.append(pl.BlockSpec((tm, 128), lambda i, j: (i, 0)))
            args.append(t)
    return pl.pallas_call(
        body, grid=(M // tm, N // tn), in_specs=in_specs,
        out_specs=pl.BlockSpec((tm, tn), lambda i, j: (i, j)),
        out_shape=jax.ShapeDtypeStruct((M, N), out_dtype), name=name,
        compiler_params=_params("parallel", "arbitrary"),
    )(*args)


def _mm_nt(a, b, *, tm, to, tn, out_dtype, name, add=None):
    M, N = a.shape
    O = b.shape[0]
    assert M % tm == 0 and O % to == 0 and N % tn == 0 and b.shape[1] == N
    nk = N // tn

    def body(*refs):
        a_ref, b_ref = refs[0], refs[1]
        o_ref, acc_ref = refs[-2], refs[-1]
        k = pl.program_id(2)

        @pl.when(k == 0)
        def _():
            if add is not None:
                acc_ref[...] = refs[2][...]
            else:
                acc_ref[...] = jnp.zeros_like(acc_ref)

        acc_ref[...] += lax.dot_general(a_ref[...], b_ref[...], (((1,), (1,)), ((), ())),
                                        preferred_element_type=F32)

        @pl.when(k == nk - 1)
        def _():
            o_ref[...] = acc_ref[...].astype(out_dtype)

    in_specs = [pl.BlockSpec((tm, tn), lambda i, j, k: (i, k)), pl.BlockSpec((to, tn), lambda i, j, k: (j, k))]
    args = [a, b]
    if add is not None:
        in_specs.append(pl.BlockSpec((tm, to), lambda i, j, k: (i, j)))
        args.append(add)
    return pl.pallas_call(
        body, grid=(M // tm, O // to, nk), in_specs=in_specs,
        out_specs=pl.BlockSpec((tm, to), lambda i, j, k: (i, j)),
        out_shape=jax.ShapeDtypeStruct((M, O), out_dtype),
        scratch_shapes=[pltpu.VMEM((tm, to), F32)], name=name,
        compiler_params=_params("parallel", "parallel", "arbitrary"),
    )(*args)


def _mm_tn(a, b, *, tk, tn, tm, out_dtype, name):
    M, K = a.shape
    N = b.shape[1]
    assert M % tm == 0 and K % tk == 0 and N % tn == 0 and b.shape[0] == M
    nm = M // tm

    def body(a_ref, b_ref, o_ref, acc_ref):
        m = pl.program_id(2)

        @pl.when(m == 0)
        def _():
            acc_ref[...] = jnp.zeros_like(acc_ref)

        acc_ref[...] += lax.dot_general(a_ref[...], b_ref[...], (((0,), (0,)), ((), ())),
                                        preferred_element_type=F32)

        @pl.when(m == nm - 1)
        def _():
            o_ref[...] = acc_ref[...].astype(out_dtype)

    return pl.pallas_call(
        body, grid=(K // tk, N // tn, nm),
        in_specs=[pl.BlockSpec((tm, tk), lambda i, j, m: (m, i)), pl.BlockSpec((tm, tn), lambda i, j, m: (m, j))],
        out_specs=pl.BlockSpec((tk, tn), lambda i, j, m: (i, j)),
        out_shape=jax.ShapeDtypeStruct((K, N), out_dtype),
        scratch_shapes=[pltpu.VMEM((tk, tn), F32)], name=name,
        compiler_params=_params("parallel", "parallel", "arbitrary"),
    )(a, b)


ROW_TILE = 512


def _rms_fwd(x, g, *, name):
    S, Dm = x.shape

    def body(x_ref, g_ref, o_ref):
        xf = x_ref[...]
        r = lax.rsqrt(jnp.mean(xf * xf, axis=-1, keepdims=True) + RMS_EPS)
        o_ref[...] = (xf * r * g_ref[...]).astype(BF)

    return pl.pallas_call(
        body, grid=(S // ROW_TILE,),
        in_specs=[pl.BlockSpec((ROW_TILE, Dm), lambda i: (i, 0)), pl.BlockSpec((1, Dm), lambda i: (0, 0))],
        out_specs=pl.BlockSpec((ROW_TILE, Dm), lambda i: (i, 0)),
        out_shape=jax.ShapeDtypeStruct((S, Dm), BF), name=name, compiler_params=_params("parallel"),
    )(x, g)


def _rms_bwd(x, g, dn, dres, *, name):
    S, Dm = x.shape

    def body(x_ref, g_ref, dn_ref, dres_ref, dx_ref, dxb_ref, dg_ref):
        i = pl.program_id(0)
        xf = x_ref[...]
        r = lax.rsqrt(jnp.mean(xf * xf, axis=-1, keepdims=True) + RMS_EPS)
        xh = xf * r
        dnf = dn_ref[...]
        dyg = dnf * g_ref[...]
        dx = dres_ref[...] + r * (dyg - xh * jnp.mean(dyg * xh, axis=-1, keepdims=True))
        dx_ref[...] = dx
        dxb_ref[...] = dx.astype(BF)

        @pl.when(i == 0)
        def _():
            dg_ref[...] = jnp.zeros_like(dg_ref)

        dg_ref[...] += jnp.sum(dnf * xh, axis=0, keepdims=True)

    row = pl.BlockSpec((ROW_TILE, Dm), lambda i: (i, 0))
    vec = pl.BlockSpec((1, Dm), lambda i: (0, 0))
    return pl.pallas_call(
        body, grid=(S // ROW_TILE,), in_specs=[row, vec, row, row], out_specs=[row, row, vec],
        out_shape=[jax.ShapeDtypeStruct((S, Dm), F32), jax.ShapeDtypeStruct((S, Dm), BF),
                   jax.ShapeDtypeStruct((1, Dm), F32)],
        name=name, compiler_params=_params("arbitrary"),
    )(x, g, dn, dres)


def _loss_head(h, g, tgt, *, name):
    S, Dm = h.shape

    def body(h_ref, g_ref, t_ref, loss_ref, dh_ref, dhb_ref, dg_ref):
        i = pl.program_id(0)
        xf = h_ref[...]
        r = lax.rsqrt(jnp.mean(xf * xf, axis=-1, keepdims=True) + RMS_EPS)
        xh = xf * r
        gv = g_ref[...]
        err = xh * gv - t_ref[...]
        dy = err * (1.0 / Dm)
        dyg = dy * gv
        dh = r * (dyg - xh * jnp.mean(dyg * xh, axis=-1, keepdims=True))
        dh_ref[...] = dh
        dhb_ref[...] = dh.astype(BF)

        @pl.when(i == 0)
        def _():
            dg_ref[...] = jnp.zeros_like(dg_ref)
            loss_ref[...] = jnp.zeros_like(loss_ref)

        dg_ref[...] += jnp.sum(dy * xh, axis=0, keepdims=True)
        part = 0.5 * jnp.sum(jnp.mean(err * err, axis=-1, keepdims=True), axis=0, keepdims=True)
        loss_ref[...] += jnp.broadcast_to(part, loss_ref.shape)

    row = pl.BlockSpec((ROW_TILE, Dm), lambda i: (i, 0))
    vec = pl.BlockSpec((1, Dm), lambda i: (0, 0))
    return pl.pallas_call(
        body, grid=(S // ROW_TILE,), in_specs=[row, vec, row],
        out_specs=[pl.BlockSpec((1, 128), lambda i: (0, 0)), row, row, vec],
        out_shape=[jax.ShapeDtypeStruct((1, 128), F32), jax.ShapeDtypeStruct((S, Dm), F32),
                   jax.ShapeDtypeStruct((S, Dm), BF), jax.ShapeDtypeStruct((1, Dm), F32)],
        name=name, compiler_params=_params("arbitrary"),
    )(h, g, tgt)


SWIGLU_ROWS = 256


def _swiglu_fwd(gu, *, name):
    S = gu.shape[0]

    def body(g_ref, u_ref, o_ref):
        g = g_ref[...].astype(F32)
        sig = 1.0 / (1.0 + jnp.exp(-g))
        o_ref[...] = (g * sig * u_ref[...].astype(F32)).astype(BF)

    return pl.pallas_call(
        body, grid=(S // SWIGLU_ROWS,),
        in_specs=[pl.BlockSpec((SWIGLU_ROWS, D_FF), lambda i: (i, 0)), pl.BlockSpec((SWIGLU_ROWS, D_FF), lambda i: (i, 1))],
        out_specs=pl.BlockSpec((SWIGLU_ROWS, D_FF), lambda i: (i, 0)),
        out_shape=jax.ShapeDtypeStruct((S, D_FF), BF), name=name, compiler_params=_params("parallel"),
    )(gu, gu)


def _swiglu_bwd(gu, dact, *, name):
    S = gu.shape[0]

    def body(g_ref, u_ref, d_ref, o_ref):
        g = g_ref[...].astype(F32)
        u = u_ref[...].astype(F32)
        d = d_ref[...].astype(F32)
        sig = 1.0 / (1.0 + jnp.exp(-g))
        o_ref[:, :D_FF] = (d * u * sig * (1.0 + g * (1.0 - sig))).astype(BF)
        o_ref[:, D_FF:] = (d * g * sig).astype(BF)

    return pl.pallas_call(
        body, grid=(S // SWIGLU_ROWS,),
        in_specs=[pl.BlockSpec((SWIGLU_ROWS, D_FF), lambda i: (i, 0)), pl.BlockSpec((SWIGLU_ROWS, D_FF), lambda i: (i, 1)),
                  pl.BlockSpec((SWIGLU_ROWS, D_FF), lambda i: (i, 0))],
        out_specs=pl.BlockSpec((SWIGLU_ROWS, 2 * D_FF), lambda i: (i, 0)),
        out_shape=jax.ShapeDtypeStruct((S, 2 * D_FF), BF), name=name, compiler_params=_params("parallel"),
    )(gu, gu, dact)


def _attn_fwd(qa, ka, va, qcb, kcb, vcb, *, dil, T, nkv, window, name, c=None, cT=None, o_dtype=F32):
    L = qa.shape[0]
    nq = L // T
    fox = c is not None

    def kv_block(n, j):
        return n - (nkv - 1) + j

    def body(*refs):
        if fox:
            q_ref, k_ref, v_ref, c_ref, ct_ref, o_ref, lse_ref, m_sc, l_sc, acc_sc = refs
        else:
            q_ref, k_ref, v_ref, o_ref, lse_ref, m_sc, l_sc, acc_sc = refs
        n = pl.program_id(1)
        j = pl.program_id(2)
        kb = kv_block(n, j)

        @pl.when(j == 0)
        def _():
            m_sc[...] = jnp.full(m_sc.shape, NEG_INF, F32)
            l_sc[...] = jnp.zeros_like(l_sc)
            acc_sc[...] = jnp.zeros_like(acc_sc)

        @pl.when(kb >= 0)
        def _():
            diff = (n * T + lax.broadcasted_iota(jnp.int32, (T, T), 0)) - (kb * T + lax.broadcasted_iota(jnp.int32, (T, T), 1))
            valid = diff >= 0
            if window is not None:
                valid = jnp.logical_and(valid, diff <= window)
            for h in range(N_HEADS):
                hs = slice(h * HEAD_DIM, (h + 1) * HEAD_DIM)
                qh = q_ref[:, hs] * jnp.asarray(ATTN_SCALE, BF)
                s = lax.dot_general(qh, k_ref[:, hs], (((1,), (1,)), ((), ())), preferred_element_type=F32)
                if fox:
                    s = s + c_ref[:, h:h + 1] - ct_ref[h:h + 1, :]
                s = jnp.where(valid, s, NEG_INF)
                m_prev = m_sc[:, h:h + 1]
                m_new = jnp.maximum(m_prev, jnp.max(s, axis=1, keepdims=True))
                alpha = jnp.exp(m_prev - m_new)
                p = jnp.exp(s - m_new)
                l_sc[:, h:h + 1] = alpha * l_sc[:, h:h + 1] + jnp.sum(p, axis=1, keepdims=True)
                acc_sc[:, hs] = alpha * acc_sc[:, hs] + jnp.dot(p.astype(BF), v_ref[:, hs], preferred_element_type=F32)
                m_sc[:, h:h + 1] = m_new

        @pl.when(j == nkv - 1)
        def _():
            lane = lax.broadcasted_iota(jnp.int32, (T, 128), 1)
            lse = jnp.zeros((T, 128), F32)
            for h in range(N_HEADS):
                hs = slice(h * HEAD_DIM, (h + 1) * HEAD_DIM)
                l = l_sc[:, h:h + 1]
                o_ref[:, hs] = (acc_sc[:, hs] / l).astype(o_dtype)
                lse = jnp.where(lane == h, m_sc[:, h:h + 1] + jnp.log(l), lse)
            lse_ref[...] = lse

    def kvi(n, j):
        return jnp.maximum(kv_block(n, j), 0)

    in_specs = [pl.BlockSpec((T, 1024), lambda r, n, j: (n, qcb(r))),
                pl.BlockSpec((T, 1024), lambda r, n, j: (kvi(n, j), kcb(r))),
                pl.BlockSpec((T, 1024), lambda r, n, j: (kvi(n, j), vcb(r)))]
    args = [qa, ka, va]
    if fox:
        in_specs += [pl.BlockSpec((T, GATE_LANES), lambda r, n, j: (n, 0)),
                     pl.BlockSpec((GATE_LANES, T), lambda r, n, j: (0, kvi(n, j)))]
        args += [c, cT]
    return pl.pallas_call(
        body, grid=(dil, nq, nkv), in_specs=in_specs,
        out_specs=[pl.BlockSpec((T, 1024), lambda r, n, j: (n, r)), pl.BlockSpec((T, 128), lambda r, n, j: (n, r))],
        out_shape=[jax.ShapeDtypeStruct((L, dil * 1024), o_dtype), jax.ShapeDtypeStruct((L, dil * 128), F32)],
        scratch_shapes=[pltpu.VMEM((T, 128), F32), pltpu.VMEM((T, 128), F32), pltpu.VMEM((T, 1024), F32)],
        name=name, compiler_params=_params("parallel", "parallel", "arbitrary"),
    )(*args)


def _attn_bwd(qa, ka, va, qcb, kcb, vcb, doa, oa, lsea, *, dil, T, nqs, window, name, c=None, cT=None):
    L = qa.shape[0]
    nq = L // T
    fox = c is not None

    def body(*refs):
        if fox:
            (q_ref, k_ref, v_ref, do_ref, o_ref, lse_ref, c_ref, ct_ref,
             dq_ref, dk_ref, dv_ref, dct_ref, dcq_ref, dq_sc, dk_sc, dv_sc, dc_sc, dcq_sc) = refs
        else:
            (q_ref, k_ref, v_ref, do_ref, o_ref, lse_ref,
             dq_ref, dk_ref, dv_ref, dq_sc, dk_sc, dv_sc) = refs
        kb = pl.program_id(1)
        jq = pl.program_id(2)
        qb = kb + jq

        @pl.when(jnp.logical_and(kb == 0, jq == 0))
        def _():
            dq_sc[...] = jnp.zeros_like(dq_sc)
            if fox:
                dcq_sc[...] = jnp.zeros_like(dcq_sc)

        @pl.when(jq == 0)
        def _():
            dk_sc[...] = jnp.zeros_like(dk_sc)
            dv_sc[...] = jnp.zeros_like(dv_sc)
            if fox:
                dc_sc[...] = jnp.zeros_like(dc_sc)

        @pl.when(qb < nq)
        def _():
            diff = (qb * T + lax.broadcasted_iota(jnp.int32, (T, T), 0)) - (kb * T + lax.broadcasted_iota(jnp.int32, (T, T), 1))
            valid = diff >= 0
            if window is not None:
                valid = jnp.logical_and(valid, diff <= window)
            qrows = pl.ds(pl.multiple_of(qb * T, T), T)
            for h in range(N_HEADS):
                hs = slice(h * HEAD_DIM, (h + 1) * HEAD_DIM)
                qh = q_ref[:, hs] * jnp.asarray(ATTN_SCALE, BF)
                kh = k_ref[:, hs]
                doh = do_ref[:, hs]
                s = lax.dot_general(qh, kh, (((1,), (1,)), ((), ())), preferred_element_type=F32)
                if fox:
                    s = s + c_ref[:, h:h + 1] - ct_ref[h:h + 1, :]
                s = jnp.where(valid, s, NEG_INF)
                p = jnp.exp(s - lse_ref[:, h:h + 1])
                dp = lax.dot_general(doh, v_ref[:, hs], (((1,), (1,)), ((), ())), preferred_element_type=F32)
                delta = jnp.sum(doh.astype(F32) * o_ref[:, hs].astype(F32), axis=1, keepdims=True)
                ds = p * (dp - delta)
                dsb = ds.astype(BF)
                dv_sc[:, hs] += lax.dot_general(p.astype(BF), doh, (((0,), (0,)), ((), ())), preferred_element_type=F32)
                dk_sc[:, hs] += lax.dot_general(dsb, qh, (((0,), (0,)), ((), ())), preferred_element_type=F32)
                dq_sc[qrows, hs] += jnp.dot(dsb, kh, preferred_element_type=F32) * ATTN_SCALE
                if fox:
                    dc_sc[h:h + 1, :] -= jnp.sum(ds, axis=0, keepdims=True)
                    dcq_sc[qrows, h:h + 1] += jnp.sum(ds, axis=1, keepdims=True)

        @pl.when(jq == nqs - 1)
        def _():
            dk_ref[...] = dk_sc[...].astype(BF)
            dv_ref[...] = dv_sc[...].astype(BF)
            if fox:
                dct_ref[...] = dc_sc[...]

        @pl.when(jnp.logical_and(kb == nq - 1, jq == nqs - 1))
        def _():
            def put(i, carry):
                rows = pl.ds(pl.multiple_of(i * T, T), T)
                dq_ref[rows, :] = dq_sc[rows, :].astype(BF)
                return carry
            lax.fori_loop(0, nq, put, 0)
            if fox:
                dcq_ref[...] = dcq_sc[...]

    def qi(kb, jq):
        return jnp.minimum(kb + jq, nq - 1)

    in_specs = [pl.BlockSpec((T, 1024), lambda r, kb, jq: (qi(kb, jq), qcb(r))),
                pl.BlockSpec((T, 1024), lambda r, kb, jq: (kb, kcb(r))),
                pl.BlockSpec((T, 1024), lambda r, kb, jq: (kb, vcb(r))),
                pl.BlockSpec((T, 1024), lambda r, kb, jq: (qi(kb, jq), r)),
                pl.BlockSpec((T, 1024), lambda r, kb, jq: (qi(kb, jq), r)),
                pl.BlockSpec((T, 128), lambda r, kb, jq: (qi(kb, jq), r))]
    args = [qa, ka, va, doa, oa, lsea]
    out_specs = [pl.BlockSpec((L, 1024), lambda r, kb, jq: (0, r)),
                 pl.BlockSpec((T, 1024), lambda r, kb, jq: (kb, r)),
                 pl.BlockSpec((T, 1024), lambda r, kb, jq: (kb, r))]
    out_shape = [jax.ShapeDtypeStruct((L, dil * 1024), BF)] * 3
    scratch = [pltpu.VMEM((L, 1024), F32), pltpu.VMEM((T, 1024), F32), pltpu.VMEM((T, 1024), F32)]
    if fox:
        in_specs += [pl.BlockSpec((T, GATE_LANES), lambda r, kb, jq: (qi(kb, jq), 0)),
                     pl.BlockSpec((GATE_LANES, T), lambda r, kb, jq: (0, kb))]
        args += [c, cT]
        out_specs.append(pl.BlockSpec((GATE_LANES, T), lambda r, kb, jq: (0, kb)))
        out_shape.append(jax.ShapeDtypeStruct((GATE_LANES, L), F32))
        scratch.append(pltpu.VMEM((GATE_LANES, T), F32))
        out_specs.append(pl.BlockSpec((L, GATE_LANES), lambda r, kb, jq: (0, 0)))
        out_shape.append(jax.ShapeDtypeStruct((L, GATE_LANES), F32))
        scratch.append(pltpu.VMEM((L, GATE_LANES), F32))
    return pl.pallas_call(
        body, grid=(dil, nq, nqs), in_specs=in_specs, out_specs=out_specs, out_shape=out_shape,
        scratch_shapes=scratch, name=name, compiler_params=_params("arbitrary", "arbitrary", "arbitrary"),
    )(*args)


def _band_masks(T, n):
    row = lax.broadcasted_iota(jnp.int32, (T, T), 0)
    col = lax.broadcasted_iota(jnp.int32, (T, T), 1)
    return jnp.logical_and(col >= row, n > 0), col <= row


def _band_fwd(qa, ka, va, qcb, kcb, vcb, *, dil, T, window, name):
    L = qa.shape[0]
    nq = L // T
    assert window == T
    nt = (((1,), (1,)), ((), ()))

    def body(q_ref, kp_ref, kc_ref, vp_ref, vc_ref, o_ref, lse_ref):
        valid_prev, valid_cur = _band_masks(T, pl.program_id(1))
        lane = lax.broadcasted_iota(jnp.int32, (T, 128), 1)
        low = lane < HEAD_DIM
        ones = jnp.ones((T, 128), BF)
        lse = jnp.zeros((T, 128), F32)
        def scores(h):
            ps = slice((h // 2) * 128, (h // 2 + 1) * 128)
            qp = q_ref[:, ps] * jnp.asarray(ATTN_SCALE, BF)
            qm = jnp.where(low if h % 2 == 0 else jnp.logical_not(low), qp, jnp.zeros_like(qp))
            s0 = jnp.where(valid_prev, lax.dot_general(qm, kp_ref[:, ps], nt, preferred_element_type=F32), NEG_INF)
            s1 = jnp.where(valid_cur, lax.dot_general(qm, kc_ref[:, ps], nt, preferred_element_type=F32), NEG_INF)
            return s0, s1

        def softmax(s0, s1):
            m = jnp.maximum(jnp.max(s0, axis=1, keepdims=True), jnp.max(s1, axis=1, keepdims=True))
            return m, jnp.exp(s0 - m).astype(BF), jnp.exp(s1 - m).astype(BF)

        def weighted(h, p0, p1):
            ps = slice((h // 2) * 128, (h // 2 + 1) * 128)
            l = jnp.dot(p0, ones, preferred_element_type=F32) + jnp.dot(p1, ones, preferred_element_type=F32)
            acc = jnp.dot(p0, vp_ref[:, ps], preferred_element_type=F32) + jnp.dot(p1, vc_ref[:, ps], preferred_element_type=F32)
            return l, acc

        sc, pr, even = {}, {}, None
        for t in range(N_HEADS + 2):
            if t < N_HEADS:
                sc[t] = scores(t)
            done = None
            if t >= 2:
                m, p0, p1 = pr.pop(t - 2)
                done = (m,) + weighted(t - 2, p0, p1)
            if 1 <= t <= N_HEADS:
                pr[t - 1] = softmax(*sc.pop(t - 1))
            if done is not None:
                h = t - 2
                m, l, acc = done
                lse = jnp.where(lane == h, m + jnp.log(l), lse)
                if h % 2 == 0:
                    even = acc / l
                else:
                    o_ref[:, (h // 2) * 128:(h // 2 + 1) * 128] = jnp.where(low, even, acc / l)
        lse_ref[...] = lse

    def prev(n):
        return jnp.maximum(n - 1, 0)

    blk = lambda f, cb: pl.BlockSpec((T, 1024), lambda r, n: (f(n), cb(r)))
    same = lambda n: n
    return pl.pallas_call(
        body, grid=(dil, nq),
        in_specs=[blk(same, qcb), blk(prev, kcb), blk(same, kcb), blk(prev, vcb), blk(same, vcb)],
        out_specs=[pl.BlockSpec((T, 1024), lambda r, n: (n, r)), pl.BlockSpec((T, 128), lambda r, n: (n, r))],
        out_shape=[jax.ShapeDtypeStruct((L, dil * 1024), F32), jax.ShapeDtypeStruct((L, dil * 128), F32)],
        name=name, compiler_params=_params("parallel", "parallel"),
    )(qa, ka, ka, va, va)


def _band_bwd(qa, ka, va, qcb, kcb, vcb, doa, oa, lsea, *, dil, T, window, name):
    L = qa.shape[0]
    nq = L // T
    assert window == T
    nt = (((1,), (1,)), ((), ()))
    tn = (((0,), (0,)), ((), ()))

    def body(q_ref, kp_ref, kc_ref, vp_ref, vc_ref, do_ref, o_ref, lse_ref, dq_ref, dk_ref, dv_ref, ck_sc, cv_sc):
        n = pl.program_id(1)

        @pl.when(n == 0)
        def _():
            ck_sc[...] = jnp.zeros_like(ck_sc)
            cv_sc[...] = jnp.zeros_like(cv_sc)

        @pl.when(n < nq)
        def _():
            valid_prev, valid_cur = _band_masks(T, n)
            low = lax.broadcasted_iota(jnp.int32, (T, 128), 1) < HEAD_DIM
            dot = functools.partial(lax.dot_general, preferred_element_type=F32)

            def pair(h):
                return slice((h // 2) * 128, (h // 2 + 1) * 128)

            def products(h):
                ps = pair(h)
                mask = low if h % 2 == 0 else jnp.logical_not(low)
                qp = q_ref[:, ps] * jnp.asarray(ATTN_SCALE, BF)
                dop = do_ref[:, ps]
                qm = jnp.where(mask, qp, jnp.zeros_like(qp))
                dom = jnp.where(mask, dop, jnp.zeros_like(dop))
                s0 = jnp.where(valid_prev, dot(qm, kp_ref[:, ps], nt), NEG_INF)
                s1 = jnp.where(valid_cur, dot(qm, kc_ref[:, ps], nt), NEG_INF)
                return qm, dom, s0, s1, dot(dom, vp_ref[:, ps], nt), dot(dom, vc_ref[:, ps], nt)

            def pointwise(h, qm, dom, s0, s1, dp0, dp1):
                ps = pair(h)
                mask = low if h % 2 == 0 else jnp.logical_not(low)
                prod = do_ref[:, ps].astype(F32) * o_ref[:, ps].astype(F32)
                delta = jnp.sum(jnp.where(mask, prod, 0.0), axis=1, keepdims=True)
                lse = lse_ref[:, h:h + 1]
                p0 = jnp.exp(s0 - lse)
                p1 = jnp.exp(s1 - lse)
                ds0 = (p0 * (dp0 - delta)).astype(BF)
                ds1 = (p1 * (dp1 - delta)).astype(BF)
                return qm, dom, p0.astype(BF), p1.astype(BF), ds0, ds1

            def gradients(h, qm, dom, p0, p1, ds0, ds1):
                ps = pair(h)
                dq = dot(ds0, kp_ref[:, ps], (((1,), (0,)), ((), ()))) + dot(ds1, kc_ref[:, ps], (((1,), (0,)), ((), ())))
                return dq, dot(ds0, qm, tn), dot(p0, dom, tn), dot(ds1, qm, tn), dot(p1, dom, tn)

            st1, st2, even = {}, {}, None
            for t in range(N_HEADS + 2):
                if t < N_HEADS:
                    st1[t] = products(t)
                done = gradients(t - 2, *st2.pop(t - 2)) if t >= 2 else None
                if 1 <= t <= N_HEADS:
                    st2[t - 1] = pointwise(t - 1, *st1.pop(t - 1))
                if done is not None:
                    h = t - 2
                    if h % 2 == 0:
                        even = done
                    else:
                        ps = pair(h)
                        dq_ref[:, ps] = (jnp.where(low, even[0], done[0]) * ATTN_SCALE).astype(BF)
                        dk_ref[:, ps] = (ck_sc[:, ps] + even[1] + done[1]).astype(BF)
                        dv_ref[:, ps] = (cv_sc[:, ps] + even[2] + done[2]).astype(BF)
                        ck_sc[:, ps] = even[3] + done[3]
                        cv_sc[:, ps] = even[4] + done[4]

        @pl.when(n == nq)
        def _():
            dk_ref[...] = ck_sc[...].astype(BF)
            dv_ref[...] = cv_sc[...].astype(BF)

    def cur(n):
        return jnp.minimum(n, nq - 1)

    def prev(n):
        return jnp.maximum(cur(n) - 1, 0)

    blk = lambda f, cb: pl.BlockSpec((T, 1024), lambda r, n: (f(n), cb(r)))
    own = lambda r: r
    return pl.pallas_call(
        body, grid=(dil, nq + 1),
        in_specs=[blk(cur, qcb), blk(prev, kcb), blk(cur, kcb), blk(prev, vcb), blk(cur, vcb), blk(cur, own), blk(cur, own),
                  pl.BlockSpec((T, 128), lambda r, n: (cur(n), r))],
        out_specs=[blk(cur, own), blk(lambda n: jnp.maximum(n - 1, 0), own), blk(lambda n: jnp.maximum(n - 1, 0), own)],
        out_shape=[jax.ShapeDtypeStruct((L, dil * 1024), BF)] * 3,
        scratch_shapes=[pltpu.VMEM((T, 1024), F32), pltpu.VMEM((T, 1024), F32)],
        name=name, compiler_params=_params("arbitrary", "arbitrary"),
    )(qa, ka, ka, va, va, doa, oa, lsea)


FOX_T = 256
FOX_ROWS = 128


def _fox_fwd(qkv, cT, *, name):
    S = qkv.shape[0]
    T, R = FOX_T, FOX_ROWS
    nq = S // T
    nt = (((1,), (1,)), ((), ()))
    chains = [(h, rh) for h in range(N_HEADS) for rh in range(T // R)]

    def body(q_ref, k_ref, v_ref, ct_ref, o_ref, lse_ref, m_sc, l_sc, acc_sc):
        n = pl.program_id(0)
        j = pl.program_id(1)
        lane = lax.broadcasted_iota(jnp.int32, (R, 128), 1)
        low = lane < HEAD_DIM
        ones = jnp.ones((T, 128), BF)

        @pl.when(j == 0)
        def _():
            m_sc[...] = jnp.full(m_sc.shape, NEG_INF, F32)
            l_sc[...] = jnp.zeros_like(l_sc)
            acc_sc[...] = jnp.zeros_like(acc_sc)

        def step(diagonal):
            def pair(h):
                return slice((h // 2) * 128, (h // 2 + 1) * 128)

            def rows(rh):
                return slice(rh * R, (rh + 1) * R)

            def scores(h, rh):
                qp = q_ref[rows(rh), pair(h)] * jnp.asarray(ATTN_SCALE, BF)
                qm = jnp.where(low if h % 2 == 0 else jnp.logical_not(low), qp, jnp.zeros_like(qp))
                s = lax.dot_general(qm, k_ref[:, pair(h)], nt, preferred_element_type=F32) - ct_ref[h:h + 1, :]
                if diagonal:
                    keep = (lax.broadcasted_iota(jnp.int32, (R, T), 1)
                            <= rh * R + lax.broadcasted_iota(jnp.int32, (R, T), 0))
                    s = jnp.where(keep, s, NEG_INF)
                return s

            def softmax(h, rh, s):
                m_prev = m_sc[h, rows(rh), :]
                m_new = jnp.maximum(m_prev, jnp.max(s, axis=1, keepdims=True))
                p = jnp.exp(s - jnp.concatenate([m_new] * (T // 128), axis=1)).astype(BF)
                return m_new, jnp.exp(m_prev - m_new), p

            def weighted(h, p):
                vx = jnp.concatenate([v_ref[:, pair(h)], ones], axis=1)
                return jnp.dot(p, vx, preferred_element_type=F32)

            sc, pr, even = {}, {}, {}
            nch = len(chains)
            for t in range(nch + 2):
                if t < nch:
                    sc[t] = scores(*chains[t])
                done = None
                if t >= 2:
                    m_new, alpha, p = pr.pop(t - 2)
                    done = (m_new, alpha, weighted(chains[t - 2][0], p))
                if 1 <= t <= nch:
                    pr[t - 1] = softmax(*chains[t - 1], sc.pop(t - 1))
                if done is not None:
                    h, rh = chains[t - 2]
                    m_new, alpha, pv = done
                    m_sc[h, rows(rh), :] = m_new
                    l_sc[h, rows(rh), :] = alpha * l_sc[h, rows(rh), :] + pv[:, 128:]
                    if h % 2 == 0:
                        even[rh] = (alpha, pv[:, :128])
                    else:
                        a0, pv0 = even.pop(rh)
                        acc = acc_sc[h // 2, rows(rh), :]
                        acc_sc[h // 2, rows(rh), :] = jnp.where(low, a0 * acc + pv0, alpha * acc + pv[:, :128])

        @pl.when(j < n)
        def _():
            step(False)

        @pl.when(j == n)
        def _():
            step(True)
            lane_t = lax.broadcasted_iota(jnp.int32, (T, 128), 1)
            low_t = lane_t < HEAD_DIM
            lse = jnp.zeros((T, 128), F32)
            for h in range(N_HEADS):
                lse = jnp.where(lane_t == h, m_sc[h] + jnp.log(l_sc[h]), lse)
            lse_ref[...] = lse
            for hp in range(N_HEADS // 2):
                inv = jnp.where(low_t, 1.0 / l_sc[2 * hp], 1.0 / l_sc[2 * hp + 1])
                o_ref[:, hp * 128:(hp + 1) * 128] = (acc_sc[hp] * inv).astype(BF)

    def kv(n, j):
        return jnp.minimum(j, n)

    return pl.pallas_call(
        body, grid=(nq, nq),
        in_specs=[pl.BlockSpec((T, 1024), lambda n, j: (n, 0)), pl.BlockSpec((T, 1024), lambda n, j: (kv(n, j), 1)),
                  pl.BlockSpec((T, 1024), lambda n, j: (kv(n, j), 2)), pl.BlockSpec((GATE_LANES, T), lambda n, j: (0, kv(n, j)))],
        out_specs=[pl.BlockSpec((T, 1024), lambda n, j: (n, 0)), pl.BlockSpec((T, 128), lambda n, j: (n, 0))],
        out_shape=[jax.ShapeDtypeStruct((S, 1024), BF), jax.ShapeDtypeStruct((S, 128), F32)],
        scratch_shapes=[pltpu.VMEM((N_HEADS, T, 128), F32), pltpu.VMEM((N_HEADS, T, 128), F32),
                        pltpu.VMEM((N_HEADS // 2, T, 128), F32)],
        name=name, compiler_params=_params("parallel", "arbitrary"),
    )(qkv, qkv, qkv, cT)


def _fox_bwd(qkv, cT, do, o, lse, *, name):
    S = qkv.shape[0]
    T, R = FOX_T, FOX_ROWS
    nq = S // T
    nt = (((1,), (1,)), ((), ()))
    tn = (((0,), (0,)), ((), ()))
    nn = (((1,), (0,)), ((), ()))
    chains = [(h, rh) for h in range(N_HEADS) for rh in range(T // R)]
    dot = functools.partial(lax.dot_general, preferred_element_type=F32)

    def body(q_ref, k_ref, v_ref, ct_ref, do_ref, o_ref, lse_ref, dq_ref, dk_ref, dv_ref, dct_ref, dcq_ref,
             dq_sc, dk_sc, dv_sc, dc_sc, dcq_sc):
        kb = pl.program_id(0)
        jq = pl.program_id(1)
        qb = kb + jq
        lane = lax.broadcasted_iota(jnp.int32, (R, 128), 1)
        low = lane < HEAD_DIM
        ones_k = jnp.ones((T, 128), BF)
        ones_r = jnp.ones((8, R), BF)

        @pl.when(jnp.logical_and(kb == 0, jq == 0))
        def _():
            dq_sc[...] = jnp.zeros_like(dq_sc)
            dcq_sc[...] = jnp.zeros_like(dcq_sc)

        @pl.when(jq == 0)
        def _():
            dk_sc[...] = jnp.zeros_like(dk_sc)
            dv_sc[...] = jnp.zeros_like(dv_sc)
            dc_sc[...] = jnp.zeros_like(dc_sc)

        def step(diagonal):
            def pair(h):
                return slice((h // 2) * 128, (h // 2 + 1) * 128)

            def rows(rh):
                return slice(rh * R, (rh + 1) * R)

            def qrows(rh):
                return pl.ds(pl.multiple_of(qb * T + rh * R, R), R)

            def products(h, rh):
                mask = low if h % 2 == 0 else jnp.logical_not(low)
                qp = q_ref[rows(rh), pair(h)] * jnp.asarray(ATTN_SCALE, BF)
                dop = do_ref[rows(rh), pair(h)]
                qm = jnp.where(mask, qp, jnp.zeros_like(qp))
                dom = jnp.where(mask, dop, jnp.zeros_like(dop))
                s = dot(qm, k_ref[:, pair(h)], nt) - ct_ref[h:h + 1, :]
                if diagonal:
                    keep = (lax.broadcasted_iota(jnp.int32, (R, T), 1)
                            <= rh * R + lax.broadcasted_iota(jnp.int32, (R, T), 0))
                    s = jnp.where(keep, s, NEG_INF)
                return qm, dom, s, dot(dom, v_ref[:, pair(h)], nt)

            def pointwise(h, rh, qm, dom, s, dp):
                mask = low if h % 2 == 0 else jnp.logical_not(low)
                prod = do_ref[rows(rh), pair(h)].astype(F32) * o_ref[rows(rh), pair(h)].astype(F32)
                delta = jnp.sum(jnp.where(mask, prod, 0.0), axis=1, keepdims=True)
                p = jnp.exp(s - lse_ref[rows(rh), h:h + 1])
                ds = (p * (dp - delta)).astype(BF)
                return qm, dom, p.astype(BF), ds

            def gradients(h, qm, dom, p, ds):
                return (dot(ds, k_ref[:, pair(h)], nn), dot(ds, qm, tn), dot(p, dom, tn),
                        dot(ds, ones_k, nn), dot(ones_r, ds, nn))

            st1, st2, even = {}, {}, {}
            dcq_tiles = [jnp.zeros((R, 128), F32) for _ in range(T // R)]
            nch = len(chains)
            for t in range(nch + 2):
                if t < nch:
                    st1[t] = products(*chains[t])
                done = gradients(chains[t - 2][0], *st2.pop(t - 2)) if t >= 2 else None
                if 1 <= t <= nch:
                    st2[t - 1] = pointwise(*chains[t - 1], *st1.pop(t - 1))
                if done is not None:
                    h, rh = chains[t - 2]
                    dq, dk, dv, rsum, csum = done
                    dcq_tiles[rh] = jnp.where(lane == h, rsum, dcq_tiles[rh])
                    dc_sc[h:h + 1, :] -= csum[0:1, :]
                    if h % 2 == 0:
                        even[rh] = (dq, dk, dv)
                    else:
                        dq0, dk0, dv0 = even.pop(rh)
                        ps = pair(h)
                        dq_sc[qrows(rh), ps] += jnp.where(low, dq0, dq) * ATTN_SCALE
                        dk_sc[:, ps] += dk0 + dk
                        dv_sc[:, ps] += dv0 + dv
            for rh in range(T // R):
                dcq_sc[qrows(rh), :] += dcq_tiles[rh]

        @pl.when(jnp.logical_and(jq > 0, qb < nq))
        def _():
            step(False)

        @pl.when(jq == 0)
        def _():
            step(True)

        @pl.when(jq == nq - 1)
        def _():
            dk_ref[...] = dk_sc[...].astype(BF)
            dv_ref[...] = dv_sc[...].astype(BF)
            dct_ref[...] = dc_sc[...]

        @pl.when(jnp.logical_and(kb == nq - 1, jq == nq - 1))
        def _():
            def put(i, carry):
                r = pl.ds(pl.multiple_of(i * T, T), T)
                dq_ref[r, :] = dq_sc[r, :].astype(BF)
                return carry
            lax.fori_loop(0, nq, put, 0)
            dcq_ref[...] = dcq_sc[...]

    def qi(kb, jq):
        return jnp.minimum(kb + jq, nq - 1)

    qblk = lambda col: pl.BlockSpec((T, 1024), lambda kb, jq: (qi(kb, jq), col))
    kblk = lambda col: pl.BlockSpec((T, 1024), lambda kb, jq: (kb, col))
    whole = pl.BlockSpec((S, 1024), lambda kb, jq: (0, 0))
    return pl.pallas_call(
        body, grid=(nq, nq),
        in_specs=[qblk(0), kblk(1), kblk(2), pl.BlockSpec((GATE_LANES, T), lambda kb, jq: (0, kb)), qblk(0), qblk(0),
                  pl.BlockSpec((T, 128), lambda kb, jq: (qi(kb, jq), 0))],
        out_specs=[whole, kblk(0), kblk(0), pl.BlockSpec((GATE_LANES, T), lambda kb, jq: (0, kb)),
                   pl.BlockSpec((S, GATE_LANES), lambda kb, jq: (0, 0))],
        out_shape=[jax.ShapeDtypeStruct((S, 1024), BF)] * 3 + [jax.ShapeDtypeStruct((GATE_LANES, S), F32),
                                                               jax.ShapeDtypeStruct((S, GATE_LANES), F32)],
        scratch_shapes=[pltpu.VMEM((S, 1024), F32), pltpu.VMEM((T, 1024), F32), pltpu.VMEM((T, 1024), F32),
                        pltpu.VMEM((GATE_LANES, T), F32), pltpu.VMEM((S, GATE_LANES), F32)],
        name=name, compiler_params=_params("arbitrary", "arbitrary"),
    )(qkv, qkv, qkv, cT, do, o, lse)


def _combine_groups(os, lses, *, name):
    S = os[0].shape[0]
    ng = len(os)

    def body(*refs):
        o_refs, l_refs = refs[:ng], refs[ng:2 * ng]
        out_ref, lse_ref = refs[2 * ng], refs[2 * ng + 1]
        ls = [r[...] for r in l_refs]
        m = functools.reduce(jnp.maximum, ls)
        es = [jnp.exp(l - m) for l in ls]
        den = functools.reduce(jnp.add, es)
        ws = [e / den for e in es]
        lse_ref[...] = m + jnp.log(den)
        for h in range(N_HEADS):
            hs = slice(h * HEAD_DIM, (h + 1) * HEAD_DIM)
            acc = ws[0][:, h:h + 1] * o_refs[0][:, hs]
            for g in range(1, ng):
                acc = acc + ws[g][:, h:h + 1] * o_refs[g][:, hs]
            out_ref[:, hs] = acc.astype(BF)

    row = pl.BlockSpec((ROW_TILE, 1024), lambda i: (i, 0))
    stat = pl.BlockSpec((ROW_TILE, 128), lambda i: (i, 0))
    return pl.pallas_call(
        body, grid=(S // ROW_TILE,), in_specs=[row] * ng + [stat] * ng, out_specs=[row, stat],
        out_shape=[jax.ShapeDtypeStruct((S, 1024), BF), jax.ShapeDtypeStruct((S, 128), F32)],
        name=name, compiler_params=_params("parallel"),
    )(*os, *lses)


def _assemble(parts, rope_flags, rope, *, name):
    S = parts[0].shape[0]
    n = len(parts)
    use_rope = any(rope_flags)

    def body(*refs):
        out_ref = refs[-1]
        for b in range(n):
            cols = slice(b * 1024, (b + 1) * 1024)
            if rope_flags[b]:
                cos_ref, sa_ref, sb_ref = refs[n:n + 3]
                out_ref[:, cols] = _rope_rotate(refs[b][...].astype(F32), cos_ref[...], sa_ref[...], sb_ref[...]).astype(BF)
            else:
                out_ref[:, cols] = refs[b][...]

    tm = 256
    row = pl.BlockSpec((tm, 1024), lambda i: (i, 0))
    in_specs = [row] * n
    args = list(parts)
    if use_rope:
        in_specs += [pl.BlockSpec((tm, 128), lambda i: (i, 0))] * 3
        args += list(rope)
    return pl.pallas_call(
        body, grid=(S // tm,), in_specs=in_specs, out_specs=pl.BlockSpec((tm, n * 1024), lambda i: (i, 0)),
        out_shape=jax.ShapeDtypeStruct((S, n * 1024), BF), name=name, compiler_params=_params("parallel"),
    )(*args)


GATE_ROWS = 512


def _gate_fwd(z, bf, *, name):
    S = z.shape[0]

    def body(z_ref, b_ref, c_ref, ct_ref, carry):
        i = pl.program_id(0)

        @pl.when(i == 0)
        def _():
            carry[...] = jnp.zeros_like(carry)

        zz = z_ref[...] + b_ref[...]
        logf = jnp.minimum(zz, 0.0) - jnp.log(1.0 + jnp.exp(-jnp.abs(zz)))
        tri = (lax.broadcasted_iota(jnp.int32, (GATE_ROWS, GATE_ROWS), 0)
               >= lax.broadcasted_iota(jnp.int32, (GATE_ROWS, GATE_ROWS), 1)).astype(F32)
        cs = jnp.dot(tri, logf, precision=lax.Precision.HIGHEST, preferred_element_type=F32) + carry[...]
        c_ref[...] = cs
        ct_ref[...] = cs.T
        carry[...] = cs[GATE_ROWS - 1:GATE_ROWS, :]

    return pl.pallas_call(
        body, grid=(S // GATE_ROWS,),
        in_specs=[pl.BlockSpec((GATE_ROWS, GATE_LANES), lambda i: (i, 0)), pl.BlockSpec((1, GATE_LANES), lambda i: (0, 0))],
        out_specs=[pl.BlockSpec((GATE_ROWS, GATE_LANES), lambda i: (i, 0)), pl.BlockSpec((GATE_LANES, GATE_ROWS), lambda i: (0, i))],
        out_shape=[jax.ShapeDtypeStruct((S, GATE_LANES), F32), jax.ShapeDtypeStruct((GATE_LANES, S), F32)],
        scratch_shapes=[pltpu.VMEM((1, GATE_LANES), F32)], name=name, compiler_params=_params("arbitrary"),
    )(z, bf)


def _gate_bwd(z, bf, dcT, dcq, *, name):
    S = z.shape[0]
    nb = S // GATE_ROWS

    def body(z_ref, b_ref, dct_ref, dcq_ref, dz_ref, db_ref, carry):
        i = pl.program_id(0)

        @pl.when(i == 0)
        def _():
            carry[...] = jnp.zeros_like(carry)
            db_ref[...] = jnp.zeros_like(db_ref)

        dc = dct_ref[...].T + dcq_ref[...]
        tri = (lax.broadcasted_iota(jnp.int32, (GATE_ROWS, GATE_ROWS), 0)
               <= lax.broadcasted_iota(jnp.int32, (GATE_ROWS, GATE_ROWS), 1)).astype(F32)
        dl = jnp.dot(tri, dc, precision=lax.Precision.HIGHEST, preferred_element_type=F32) + carry[...]
        carry[...] = dl[0:1, :]
        zz = z_ref[...] + b_ref[...]
        dz = dl * (1.0 / (1.0 + jnp.exp(zz)))
        lane = lax.broadcasted_iota(jnp.int32, dz.shape, 1)
        dz = jnp.where(lane < N_HEADS, dz, 0.0)
        dz_ref[...] = dz.astype(BF)
        db_ref[...] += jnp.sum(dz, axis=0, keepdims=True)

    return pl.pallas_call(
        body, grid=(nb,),
        in_specs=[pl.BlockSpec((GATE_ROWS, GATE_LANES), lambda i: (nb - 1 - i, 0)), pl.BlockSpec((1, GATE_LANES), lambda i: (0, 0)),
                  pl.BlockSpec((GATE_LANES, GATE_ROWS), lambda i: (0, nb - 1 - i)),
                  pl.BlockSpec((GATE_ROWS, GATE_LANES), lambda i: (nb - 1 - i, 0))],
        out_specs=[pl.BlockSpec((GATE_ROWS, GATE_LANES), lambda i: (nb - 1 - i, 0)), pl.BlockSpec((1, GATE_LANES), lambda i: (0, 0))],
        out_shape=[jax.ShapeDtypeStruct((S, GATE_LANES), BF), jax.ShapeDtypeStruct((1, GATE_LANES), F32)],
        scratch_shapes=[pltpu.VMEM((1, GATE_LANES), F32)], name=name, compiler_params=_params("arbitrary"),
    )(z, bf, dcT, dcq)


def _rope_tables(S):
    half = ROT_DIM // 2
    inv_freq = ROPE_THETA ** (-jnp.arange(half, dtype=F32) * 2.0 / ROT_DIM)
    ang = jnp.arange(S, dtype=F32)[:, None] * inv_freq[None, :]
    cos, sin = jnp.cos(ang), jnp.sin(ang)
    zero = jnp.zeros((S, HEAD_DIM - ROT_DIM), F32)
    zh = jnp.zeros((S, half), F32)
    cos_h = jnp.concatenate([cos, cos, jnp.ones_like(zero)], axis=1)
    sa_h = jnp.concatenate([-sin, zh, zero], axis=1)
    sb_h = jnp.concatenate([zh, sin, zero], axis=1)
    two = lambda t: jnp.concatenate([t, t], axis=1)
    return two(cos_h), two(sa_h), two(sb_h)


def _ffn_fwd(h, norm, w_gu, w_down, tag):
    n = _rms_fwd(h, norm, name=f"ffn{tag}_norm")
    gu = _mm_nn(n, w_gu, tm=1024, tn=512, out_dtype=BF, name=f"ffn{tag}_gu")
    act = _swiglu_fwd(gu, name=f"ffn{tag}_act")
    out = _mm_nn(act, w_down, tm=512, tn=1024, out_dtype=F32, name=f"ffn{tag}_down", resid=h)
    return out, (h, n, gu, act)


def _ffn_bwd(dh, dhb, saved, norm, w_gu, w_down, tag):
    h, n, gu, act = saved
    dact = _mm_nt(dhb, w_down, tm=512, to=1408, tn=1024, out_dtype=BF, name=f"ffn{tag}_dact")
    dw_down = _mm_tn(act, dhb, tk=1408, tn=1024, tm=512, out_dtype=BF, name=f"ffn{tag}_dwdown")
    dgu = _swiglu_bwd(gu, dact, name=f"ffn{tag}_dgu")
    dn = _mm_nt(dgu, w_gu, tm=512, to=1024, tn=1408, out_dtype=F32, name=f"ffn{tag}_dn")
    dw_gu = _mm_tn(n, dgu, tk=1024, tn=1408, tm=512, out_dtype=BF, name=f"ffn{tag}_dwgu")
    dx, dxb, dg = _rms_bwd(h, norm, dn, dh, name=f"ffn{tag}_dnorm")
    return dx, dxb, dg, dw_gu, dw_down


def _local_step(x, tgt, w):
    S = x.shape[0]
    rope_f = _rope_tables(S)
    rope_b = (rope_f[0], -rope_f[1], -rope_f[2])
    g = {}

    n0 = _rms_fwd(x, w["a_norm"], name="a_norm")
    proj = _mm_nn(n0, w["a_w_in"], tm=512, tn=1024, out_dtype=BF, name="a_proj", rope=rope_f)
    o_parts, lse_parts = [], []
    for gi, (window, dil) in enumerate(DILATED_PATTERNS):
        L = S // dil
        pv = proj.reshape(L, dil * proj.shape[1])
        cb = lambda t, gi=gi: (lambda r: r * 9 + gi * 3 + t)
        o_g, lse_g = _band_fwd(pv, pv, pv, cb(0), cb(1), cb(2), dil=dil, T=128, window=window // dil, name=f"a_attn{gi}")
        o_parts.append(o_g.reshape(S, 1024))
        lse_parts.append(lse_g.reshape(S, 128))
    o_a, lse_a = _combine_groups(o_parts, lse_parts, name="a_combine")
    h1 = _mm_nn(o_a, w["a_w_out"], tm=512, tn=1024, out_dtype=F32, name="a_out", resid=x)
    h2, ffn0 = _ffn_fwd(h1, w["ffn_norm"][0:1], w["ffn_w_gu"][0], w["ffn_w_down"][0], 0)

    n2 = _rms_fwd(h2, w["b_norm"], name="b_norm")
    qkv = _mm_nn(n2, w["b_w_qkv"], tm=512, tn=1024, out_dtype=BF, name="b_proj")
    zf = _mm_nn(n2, w["b_w_f"], tm=512, tn=GATE_LANES, out_dtype=F32, name="b_gate_proj")
    c, cT = _gate_fwd(zf, w["b_f"], name="b_gate")
    o_b, lse_b = _fox_fwd(qkv, cT, name="b_attn")
    h3 = _mm_nn(o_b, w["b_w_out"], tm=512, tn=1024, out_dtype=F32, name="b_out", resid=h2)
    h4, ffn1 = _ffn_fwd(h3, w["ffn_norm"][1:2], w["ffn_w_gu"][1], w["ffn_w_down"][1], 1)

    loss, dh4, dh4b, g["final_norm"] = _loss_head(h4, w["final_norm"], tgt, name="loss_head")

    dh3, dh3b, dg_f1, g["ffn_w_gu1"], g["ffn_w_down1"] = _ffn_bwd(dh4, dh4b, ffn1, w["ffn_norm"][1:2], w["ffn_w_gu"][1], w["ffn_w_down"][1], 1)

    do_b = _mm_nt(dh3b, w["b_w_out"], tm=512, to=1024, tn=1024, out_dtype=BF, name="b_do")
    g["b_w_out"] = _mm_tn(o_b, dh3b, tk=1024, tn=1024, tm=512, out_dtype=BF, name="b_dwout")
    dq, dk, dv, dcT, dcq = _fox_bwd(qkv, cT, do_b, o_b, lse_b, name="b_attn_bwd")
    dz, g["b_f"] = _gate_bwd(zf, w["b_f"], dcT, dcq, name="b_gate_bwd")
    dqkv = _assemble([dq, dk, dv], [False] * 3, None, name="b_dproj")
    dn2 = _mm_nt(dz, w["b_w_f"], tm=512, to=1024, tn=GATE_LANES, out_dtype=F32, name="b_dn_gate")
    dn2 = _mm_nt(dqkv, w["b_w_qkv"], tm=512, to=1024, tn=1024, out_dtype=F32, name="b_dn", add=dn2)
    g["b_w_qkv"] = _mm_tn(n2, dqkv, tk=1024, tn=1024, tm=512, out_dtype=BF, name="b_dwqkv")
    g["b_w_f"] = _mm_tn(n2, dz, tk=1024, tn=GATE_LANES, tm=512, out_dtype=BF, name="b_dwf")
    dh2, dh2b, g["b_norm"] = _rms_bwd(h2, w["b_norm"], dn2, dh3, name="b_dnorm")

    dh1, dh1b, dg_f0, g["ffn_w_gu0"], g["ffn_w_down0"] = _ffn_bwd(dh2, dh2b, ffn0, w["ffn_norm"][0:1], w["ffn_w_gu"][0], w["ffn_w_down"][0], 0)
    g["ffn_norm"] = jnp.concatenate([dg_f0, dg_f1], axis=0)

    do_a = _mm_nt(dh1b, w["a_w_out"], tm=512, to=1024, tn=1024, out_dtype=BF, name="a_do")
    g["a_w_out"] = _mm_tn(o_a, dh1b, tk=1024, tn=1024, tm=512, out_dtype=BF, name="a_dwout")
    parts = []
    for gi, (window, dil) in enumerate(DILATED_PATTERNS):
        L = S // dil
        pv = proj.reshape(L, dil * proj.shape[1])
        cb = lambda t, gi=gi: (lambda r: r * 9 + gi * 3 + t)
        dq, dk, dv = _band_bwd(pv, pv, pv, cb(0), cb(1), cb(2), do_a.reshape(L, dil * 1024), o_a.reshape(L, dil * 1024),
                               lse_a.reshape(L, dil * 128), dil=dil, T=128, window=window // dil, name=f"a_attn_bwd{gi}")
        parts += [t.reshape(S, 1024) for t in (dq, dk, dv)]
    dproj = _assemble(parts, [True, True, False] * 3, rope_b, name="a_dproj")
    dn0 = _mm_nt(dproj, w["a_w_in"], tm=512, to=1024, tn=1024, out_dtype=F32, name="a_dn")
    g["a_w_in"] = _mm_tn(n0, dproj, tk=1024, tn=1024, tm=512, out_dtype=BF, name="a_dwin")
    dx, _, g["a_norm"] = _rms_bwd(x, w["a_norm"], dn0, dh1, name="a_dnorm")
    return loss, dx, g


ANY = pl.BlockSpec(memory_space=pl.ANY)


def _place():
    x, y, c = lax.axis_index("x"), lax.axis_index("y"), lax.axis_index("c")
    chips = [(1 - x, y), (x, 1 - y), (1 - x, 1 - y)]
    return x, y, c, chips


def _shard_slice(ref, kind, rows, cols, s, half):
    hr = rows // 2
    if kind == "col":
        return ref.at[pl.ds(half * hr, hr), pl.ds(pl.multiple_of(s * cols, 128), cols)]
    if kind == "row":
        return ref.at[pl.ds(pl.multiple_of(s * rows + half * hr, 16), hr), :]
    return ref.at[s, pl.ds(half * hr, hr), :]


def _whole_shape(kind, rows, cols):
    return {"col": (rows, N_CHIPS * cols), "row": (N_CHIPS * rows, cols), "stack": (N_CHIPS, rows, cols)}[kind]


def _gather_weights(shards, kinds):
    nw = len(shards)
    dims = [s.shape for s in shards]

    def body(*refs):
        src, dst = refs[:nw], refs[nw:2 * nw]
        send_sems, recv_sems, local_sems = refs[2 * nw:]
        x, y, c, chips = _place()
        me = 2 * x + y
        sibling = (x, y, 1 - c)

        def part(wi, s, half):
            return _shard_slice(dst[wi], kinds[wi], dims[wi][0], dims[wi][1], s, half)

        def copy(wi, k, s, half, to, src_ref=None):
            p = part(wi, s, half)
            return pltpu.make_async_remote_copy(src_ref=p if src_ref is None else src_ref, dst_ref=p,
                                                send_sem=send_sems.at[wi * 6 + k], recv_sem=recv_sems.at[wi * 6 + k],
                                                device_id=to, device_id_type=MESH)

        local, first, passed = [], [], []
        for wi in range(nw):
            hr = dims[wi][0] // 2
            for half in range(2):
                cp = pltpu.make_async_copy(src[wi].at[pl.ds(half * hr, hr), :], part(wi, me, half), local_sems.at[wi * 2 + half])
                cp.start()
                local.append(cp)
            mine = src[wi].at[pl.ds(pl.multiple_of(c * hr, 16), hr), :]
            for j, chip in enumerate(chips):
                cp = copy(wi, j, me, c, (*chip, c), src_ref=mine)
                cp.start()
                first.append(cp)
        for wi in range(nw):
            for j, chip in enumerate(chips):
                s = 2 * chip[0] + chip[1]
                copy(wi, j, s, c, (x, y, c)).wait_recv()
                cp = copy(wi, 3 + j, s, c, sibling)
                cp.start()
                passed.append(cp)
        for wi in range(nw):
            for j, chip in enumerate(chips):
                s = 2 * chip[0] + chip[1]
                copy(wi, 3 + j, s, 1 - c, (x, y, c)).wait_recv()
        for cp in first + passed:
            cp.wait_send()
        for cp in local:
            cp.wait()

    return pl.pallas_call(
        body, in_specs=[ANY] * nw, out_specs=[ANY] * nw,
        out_shape=[jax.ShapeDtypeStruct(_whole_shape(k, *d), BF) for k, d in zip(kinds, dims)],
        scratch_shapes=[pltpu.SemaphoreType.DMA((nw * 6,)), pltpu.SemaphoreType.DMA((nw * 6,)), pltpu.SemaphoreType.DMA((nw * 2,))],
        name="gather_weights",
    )(*shards)


def _scatter_grads(partials, kinds, dims):
    nw = len(partials)

    def body(*refs):
        src, dst = refs[:nw], refs[nw:2 * nw]
        send_sems, recv_sems, local_sems = refs[2 * nw:]
        x, y, c, chips = _place()
        me = 2 * x + y
        local, sent = [], []
        for wi in range(nw):
            rows, cols = dims[wi]

            def part(s, half, wi=wi, rows=rows, cols=cols):
                return _shard_slice(src[wi], kinds[wi], rows, cols, s, half)

            cp = pltpu.make_async_copy(part(me, c), dst[wi].at[7], local_sems.at[wi])
            cp.start()
            local.append(cp)
            for j, chip in enumerate(chips):
                s = 2 * chip[0] + chip[1]
                for half in range(2):
                    slot = 2 * j + (c if half == 0 else 1 - c)
                    cp = pltpu.make_async_remote_copy(
                        src_ref=part(s, half), dst_ref=dst[wi].at[slot],
                        send_sem=send_sems.at[wi * 7 + 2 * j + half], recv_sem=recv_sems.at[wi * 7 + slot],
                        device_id=(*chip, half), device_id_type=MESH)
                    cp.start()
                    sent.append(cp)
            cp = pltpu.make_async_remote_copy(
                src_ref=part(me, 1 - c), dst_ref=dst[wi].at[6],
                send_sem=send_sems.at[wi * 7 + 6], recv_sem=recv_sems.at[wi * 7 + 6],
                device_id=(x, y, 1 - c), device_id_type=MESH)
            cp.start()
            sent.append(cp)
        for wi in range(nw):
            for slot in range(7):
                pltpu.make_async_remote_copy(
                    src_ref=dst[wi].at[slot], dst_ref=dst[wi].at[slot],
                    send_sem=send_sems.at[wi * 7 + slot], recv_sem=recv_sems.at[wi * 7 + slot],
                    device_id=(x, y, c), device_id_type=MESH).wait_recv()
        for cp in sent:
            cp.wait_send()
        for cp in local:
            cp.wait()

    return pl.pallas_call(
        body, in_specs=[ANY] * nw, out_specs=[ANY] * nw,
        out_shape=[jax.ShapeDtypeStruct((8, d[0] // 2, d[1]), BF) for d in dims],
        scratch_shapes=[pltpu.SemaphoreType.DMA((nw * 7,)), pltpu.SemaphoreType.DMA((nw * 7,)), pltpu.SemaphoreType.DMA((nw,))],
        name="scatter_grads",
    )(*partials)


def _sum_slots(buf, *, name):
    _, R, C = buf.shape
    tr = R if 8 * R * C * 2 <= 6 * 1024 * 1024 else 128
    assert R % tr == 0

    def body(b_ref, o_ref):
        acc = b_ref[0].astype(F32)
        for k in range(1, 8):
            acc = acc + b_ref[k].astype(F32)
        o_ref[...] = acc

    return pl.pallas_call(
        body, grid=(R // tr,), in_specs=[pl.BlockSpec((8, tr, C), lambda i: (0, i, 0))],
        out_specs=pl.BlockSpec((tr, C), lambda i: (i, 0)),
        out_shape=jax.ShapeDtypeStruct((R, C), F32), name=name, compiler_params=_params("parallel"),
    )(buf)


def _pair_exchange(halves, groups):
    nw = len(halves)
    dims = [h.shape for h in halves]
    where = {}
    out_shape = []
    for gi, members in enumerate(groups):
        hr, cols = dims[members[0]]
        for l, wi in enumerate(members):
            where[wi] = (gi, l if len(members) > 1 else None)
        shape = (2 * hr, cols) if len(members) == 1 else (len(members), 2 * hr, cols)
        out_shape.append(jax.ShapeDtypeStruct(shape, F32))

    def body(*refs):
        src, dst = refs[:nw], refs[nw:nw + len(groups)]
        send_sems, recv_sems, local_sems = refs[nw + len(groups):]
        x, y, c, _ = _place()

        def rows_of(wi, half):
            gi, l = where[wi]
            hr = dims[wi][0]
            ref = dst[gi] if l is None else dst[gi].at[l]
            return ref.at[pl.ds(pl.multiple_of(half * hr, 8), hr), :]

        local, sent = [], []
        for wi in range(nw):
            cp = pltpu.make_async_copy(src[wi], rows_of(wi, c), local_sems.at[wi])
            cp.start()
            local.append(cp)
            cp = pltpu.make_async_remote_copy(src_ref=src[wi], dst_ref=rows_of(wi, c), send_sem=send_sems.at[wi],
                                              recv_sem=recv_sems.at[wi], device_id=(x, y, 1 - c), device_id_type=MESH)
            cp.start()
            sent.append(cp)
        for wi in range(nw):
            pltpu.make_async_remote_copy(src_ref=src[wi], dst_ref=rows_of(wi, 1 - c), send_sem=send_sems.at[wi],
                                         recv_sem=recv_sems.at[wi], device_id=(x, y, c), device_id_type=MESH).wait_recv()
        for cp in sent:
            cp.wait_send()
        for cp in local:
            cp.wait()

    return pl.pallas_call(
        body, in_specs=[ANY] * nw, out_specs=[ANY] * len(groups), out_shape=out_shape,
        scratch_shapes=[pltpu.SemaphoreType.DMA((nw,)), pltpu.SemaphoreType.DMA((nw,)), pltpu.SemaphoreType.DMA((nw,))],
        name="pair_exchange",
    )(*halves)


SMALL_ROWS = 8


def _allreduce_small(v, *, name):
    assert v.shape == (SMALL_ROWS, D_MODEL)

    def body(v_ref, o_ref, buf, send_sems, recv_sems):
        x, y, c, _ = _place()
        me = 4 * x + 2 * y + c
        buf[me] = v_ref[...]
        sent = []
        for k in range(1, 8):
            bx, by, bc = (k >> 2) & 1, (k >> 1) & 1, k & 1
            peer = (1 - x if bx else x, 1 - y if by else y, 1 - c if bc else c)
            cp = pltpu.make_async_remote_copy(src_ref=v_ref, dst_ref=buf.at[me], send_sem=send_sems.at[k - 1],
                                              recv_sem=recv_sems.at[k - 1], device_id=peer, device_id_type=MESH)
            cp.start()
            sent.append(cp)
        for k in range(1, 8):
            bx, by, bc = (k >> 2) & 1, (k >> 1) & 1, k & 1
            peer = 4 * (1 - x if bx else x) + 2 * (1 - y if by else y) + (1 - c if bc else c)
            pltpu.make_async_remote_copy(src_ref=v_ref, dst_ref=buf.at[peer], send_sem=send_sems.at[k - 1],
                                         recv_sem=recv_sems.at[k - 1], device_id=(x, y, c), device_id_type=MESH).wait_recv()
        for cp in sent:
            cp.wait_send()
        acc = buf[0]
        for d in range(1, 8):
            acc = acc + buf[d]
        o_ref[...] = acc

    vmem = pl.BlockSpec(memory_space=pltpu.VMEM)
    return pl.pallas_call(
        body, in_specs=[vmem], out_specs=vmem, out_shape=jax.ShapeDtypeStruct(v.shape, F32),
        scratch_shapes=[pltpu.VMEM((8,) + v.shape, F32), pltpu.SemaphoreType.DMA((7,)), pltpu.SemaphoreType.DMA((7,))],
        name=name,
    )(v)


def _adamw(w, g, m, v, *, name):
    R, C = w.shape
    tr = R
    if R * C * 4 > 1024 * 1024:
        tr = max(t for t in range(8, R, 8) if R % t == 0 and t * C * 4 <= 1024 * 1024)

    def body(w_ref, g_ref, m_ref, v_ref, d_ref, m2_ref, v2_ref):
        gg = g_ref[...]
        m2 = ADAM_B1 * m_ref[...] + (1.0 - ADAM_B1) * gg
        v2 = ADAM_B2 * v_ref[...] + (1.0 - ADAM_B2) * jnp.square(gg)
        m_hat = m2 / (1.0 - ADAM_B1 ** ADAM_STEP)
        v_hat = v2 / (1.0 - ADAM_B2 ** ADAM_STEP)
        d_ref[...] = -ADAM_LR * (m_hat / (jnp.sqrt(v_hat) + ADAM_EPS) + ADAM_WD * w_ref[...])
        m2_ref[...] = m2
        v2_ref[...] = v2

    blk = pl.BlockSpec((tr, C), lambda i: (i, 0))
    out = jax.ShapeDtypeStruct((R, C), F32)
    return pl.pallas_call(
        body, grid=(R // tr,), in_specs=[blk] * 4, out_specs=[blk] * 3, out_shape=[out] * 3,
        name=name, compiler_params=_params("parallel"),
    )(w, g, m, v)


WEIGHT_ORDER = ("a_norm", "a_w_in", "a_w_out", "b_norm", "b_w_in", "b_f", "b_w_out", "ffn_norm", "ffn_w_gu",
                "ffn_w_down", "final_norm")
MATRICES = (("a_w_in", 0, "col"), ("a_w_out", 0, "row"), ("b_w_in", 0, "stack"), ("b_w_out", 0, "row"),
            ("ffn_w_gu", 0, "col"), ("ffn_w_gu", 1, "col"), ("ffn_w_down", 0, "row"), ("ffn_w_down", 1, "row"))
MATRIX_GROUPS = ([0], [1], [2], [3], [4, 5], [6, 7])
GROUP_NAMES = ("a_w_in", "a_w_out", "b_w_in", "b_w_out", "ffn_w_gu", "ffn_w_down")
QKV_COLS = 3 * N_HEADS * HEAD_DIM


def kernel(x, a_norm, a_w_in, a_w_out, b_norm, b_w_in, b_f, b_w_out, ffn_norm, ffn_w_gu, ffn_w_down, final_norm, loss_target, m_a_norm, m_a_w_in, m_a_w_out, m_b_norm, m_b_w_in, m_b_f, m_b_w_out, m_ffn_norm, m_ffn_w_gu, m_ffn_w_down, m_final_norm, v_a_norm, v_a_w_in, v_a_w_out, v_b_norm, v_b_w_in, v_b_f, v_b_w_out, v_ffn_norm, v_ffn_w_gu, v_ffn_w_down, v_final_norm):
    given = dict(a_norm=a_norm, a_w_in=a_w_in, a_w_out=a_w_out, b_norm=b_norm, b_w_in=b_w_in, b_f=b_f, b_w_out=b_w_out,
                 ffn_norm=ffn_norm, ffn_w_gu=ffn_w_gu, ffn_w_down=ffn_w_down, final_norm=final_norm)
    mom_m = dict(a_norm=m_a_norm, a_w_in=m_a_w_in, a_w_out=m_a_w_out, b_norm=m_b_norm, b_w_in=m_b_w_in, b_f=m_b_f,
                 b_w_out=m_b_w_out, ffn_norm=m_ffn_norm, ffn_w_gu=m_ffn_w_gu, ffn_w_down=m_ffn_w_down, final_norm=m_final_norm)
    mom_v = dict(a_norm=v_a_norm, a_w_in=v_a_w_in, a_w_out=v_a_w_out, b_norm=v_b_norm, b_w_in=v_b_w_in, b_f=v_b_f,
                 b_w_out=v_b_w_out, ffn_norm=v_ffn_norm, ffn_w_gu=v_ffn_w_gu, ffn_w_down=v_ffn_w_down, final_norm=v_final_norm)
    chip = 2 * lax.axis_index("x") + lax.axis_index("y")
    core = lax.axis_index("c")
    bn_cols = b_norm.shape[1]

    placed = lax.dynamic_update_slice(jnp.zeros((SMALL_ROWS, D_MODEL), F32), b_norm, (0, chip * bn_cols))
    placed = placed * (core == 0).astype(F32)
    b_norm_full = _allreduce_small(placed, name="gather_b_norm")[0:1]

    shards = [given[n][l].astype(BF) for n, l, _ in MATRICES]
    kinds = [k for _, _, k in MATRICES]
    dims = [s.shape for s in shards]
    whole = _gather_weights(shards, kinds)
    b_in = whole[2].transpose(1, 0, 2).reshape(D_MODEL, -1)
    gate_cols = b_in.shape[1] - QKV_COLS
    w = dict(a_norm=a_norm, a_w_in=whole[0], a_w_out=whole[1], b_norm=b_norm_full,
             b_w_qkv=b_in[:, :QKV_COLS], b_w_f=jnp.pad(b_in[:, QKV_COLS:], ((0, 0), (0, GATE_LANES - gate_cols))),
             b_f=jnp.pad(b_f, ((0, 0), (0, GATE_LANES - gate_cols))), b_w_out=whole[3],
             ffn_norm=ffn_norm, ffn_w_gu=(whole[4], whole[5]), ffn_w_down=(whole[6], whole[7]),
             final_norm=final_norm.reshape(1, D_MODEL))

    loss, dx, g = _local_step(x[0], loss_target[0], w)

    g_b_in = jnp.concatenate([g["b_w_qkv"], g["b_w_f"][:, :gate_cols]], axis=1)
    g_b_in = g_b_in.reshape(D_MODEL, N_CHIPS, -1).transpose(1, 0, 2)
    partials = [g["a_w_in"], g["a_w_out"], g_b_in, g["b_w_out"], g["ffn_w_gu0"], g["ffn_w_gu1"],
                g["ffn_w_down0"], g["ffn_w_down1"]]
    slots = _scatter_grads(partials, kinds, dims)
    halves = [_sum_slots(b, name=f"sum_{n}{l}") for b, (n, l, _) in zip(slots, MATRICES)]
    reduced = dict(zip(GROUP_NAMES, _pair_exchange(halves, MATRIX_GROUPS)))

    small = jnp.concatenate([g["a_norm"], g["b_norm"], g["ffn_norm"], g["final_norm"],
                             jnp.pad(g["b_f"], ((0, 0), (0, D_MODEL - GATE_LANES))),
                             jnp.zeros((SMALL_ROWS - 6, D_MODEL), F32)], axis=0)
    small = _allreduce_small(small, name="allreduce_small")
    grads = dict(reduced)
    grads["a_norm"] = small[0:1]
    grads["b_norm"] = lax.dynamic_slice(small, (1, chip * bn_cols), (1, bn_cols))
    grads["ffn_norm"] = small[2:4]
    grads["final_norm"] = small[4]
    grads["b_f"] = small[5:6, :gate_cols]

    out_g, out_d, out_m, out_v = [], [], [], []
    for n in WEIGHT_ORDER:
        shape = given[n].shape
        two_d = (1, shape[0]) if len(shape) == 1 else (-1, shape[-1])
        d, m2, v2 = _adamw(given[n].reshape(two_d), grads[n].reshape(two_d), mom_m[n].reshape(two_d),
                           mom_v[n].reshape(two_d), name=f"adamw_{n}")
        out_g.append(grads[n].reshape(shape))
        out_d.append(d.reshape(shape))
        out_m.append(m2.reshape(shape))
        out_v.append(v2.reshape(shape))

    total = lax.psum(loss[0, 0], MESH_AXES)
    return (total, dx[None], *out_g, *out_d, *out_m, *out_v)
```

```python
import functools

import jax
import jax.numpy as jnp
from jax import lax
from jax.experimental import pallas as pl
from jax.experimental.pallas import tpu as pltpu

F32 = jnp.float32
BF = jnp.bfloat16

D_MODEL = 1024
N_HEADS = 16
HEAD_DIM = 64
D_FF = 2816
DILATED_PATTERNS = ((128, 1), (512, 4), (2048, 16))
ROT_DIM = 16
ROPE_THETA = 500000.0
RMS_EPS = 1e-6
NEG_INF = -1e30
ATTN_SCALE = HEAD_DIM ** -0.5
GATE_LANES = 128
N_CHIPS = 4
MESH_AXES = ("x", "y", "c")
MESH = pl.DeviceIdType.MESH

ADAM_LR = 0.001
ADAM_B1 = 0.9
ADAM_B2 = 0.999
ADAM_EPS = 1e-08
ADAM_WD = 0.01
ADAM_STEP = 10

VMEM_LIMIT_BYTES = 56 * 1024 * 1024


def _params(*sem):
    return pltpu.CompilerParams(dimension_semantics=sem, vmem_limit_bytes=VMEM_LIMIT_BYTES)


def _rope_rotate(t, cos, sin_a, sin_b):
    outs = []
    for cidx in range(t.shape[1] // 128):
        tc = t[:, cidx * 128:(cidx + 1) * 128]
        outs.append(tc * cos + pltpu.roll(tc, 120, 1) * sin_a + pltpu.roll(tc, 8, 1) * sin_b)
    return jnp.concatenate(outs, axis=1)


def _mm_nn(a, b, *, tm, tn, out_dtype, name, resid=None, rope=None):
    M, K = a.shape
    N = b.shape[1]
    assert M % tm == 0 and N % tn == 0 and b.shape[0] == K
    n_in = 2 + (resid is not None) + (3 if rope is not None else 0)

    def body(*refs):
        a_ref, b_ref = refs[0], refs[1]
        o_ref = refs[n_in]
        acc = jnp.dot(a_ref[...], b_ref[...], preferred_element_type=F32)
        if resid is not None:
            acc = acc + refs[2][...]
        if rope is not None:
            cos_ref, sa_ref, sb_ref = refs[n_in - 3:n_in]
            j = pl.program_id(1)

            @pl.when(j % 3 != 2)
            def _():
                o_ref[...] = _rope_rotate(acc, cos_ref[...], sa_ref[...], sb_ref[...]).astype(out_dtype)

            @pl.when(j % 3 == 2)
            def _():
                o_ref[...] = acc.astype(out_dtype)
        else:
            o_ref[...] = acc.astype(out_dtype)

    in_specs = [pl.BlockSpec((tm, K), lambda i, j: (i, 0)), pl.BlockSpec((K, tn), lambda i, j: (0, j))]
    args = [a, b]
    if resid is not None:
        in_specs.append(pl.BlockSpec((tm, tn), lambda i, j: (i, j)))
        args.append(resid)
    if rope is not None:
        assert tn == 1024
        for t in rope:
            in_specs.append(pl.BlockSpec((tm, 128), lambda i, j: (i, 0)))
            args.append(t)
    return pl.pallas_call(
        body, grid=(M // tm, N // tn), in_specs=in_specs,
        out_specs=pl.BlockSpec((tm, tn), lambda i, j: (i, j)),
        out_shape=jax.ShapeDtypeStruct((M, N), out_dtype), name=name,
        compiler_params=_params("parallel", "arbitrary"),
    )(*args)


def _mm_nt(a, b, *, tm, to, tn, out_dtype, name, add=None):
    M, N = a.shape
    O = b.shape[0]
    assert M % tm == 0 and O % to == 0 and N % tn == 0 and b.shape[1] == N
    nk = N // tn

    def body(*refs):
        a_ref, b_ref = refs[0], refs[1]
        o_ref, acc_ref = refs[-2], refs[-1]
        k = pl.program_id(2)

        @pl.when(k == 0)
        def _():
            if add is not None:
                acc_ref[...] = refs[2][...]
            else:
                acc_ref[...] = jnp.zeros_like(acc_ref)

        acc_ref[...] += lax.dot_general(a_ref[...], b_ref[...], (((1,), (1,)), ((), ())),
                                        preferred_element_type=F32)

        @pl.when(k == nk - 1)
        def _():
            o_ref[...] = acc_ref[...].astype(out_dtype)

    in_specs = [pl.BlockSpec((tm, tn), lambda i, j, k: (i, k)), pl.BlockSpec((to, tn), lambda i, j, k: (j, k))]
    args = [a, b]
    if add is not None:
        in_specs.append(pl.BlockSpec((tm, to), lambda i, j, k: (i, j)))
        args.append(add)
    return pl.pallas_call(
        body, grid=(M // tm, O // to, nk), in_specs=in_specs,
        out_specs=pl.BlockSpec((tm, to), lambda i, j, k: (i, j)),
        out_shape=jax.ShapeDtypeStruct((M, O), out_dtype),
        scratch_shapes=[pltpu.VMEM((tm, to), F32)], name=name,
        compiler_params=_params("parallel", "parallel", "arbitrary"),
    )(*args)


def _mm_tn(a, b, *, tk, tn, tm, out_dtype, name):
    M, K = a.shape
    N = b.shape[1]
    assert M % tm == 0 and K % tk == 0 and N % tn == 0 and b.shape[0] == M
    nm = M // tm

    def body(a_ref, b_ref, o_ref, acc_ref):
        m = pl.program_id(2)

        @pl.when(m == 0)
        def _():
            acc_ref[...] = jnp.zeros_like(acc_ref)

        acc_ref[...] += lax.dot_general(a_ref[...], b_ref[...], (((0,), (0,)), ((), ())),
                                        preferred_element_type=F32)

        @pl.when(m == nm - 1)
        def _():
            o_ref[...] = acc_ref[...].astype(out_dtype)

    return pl.pallas_call(
        body, grid=(K // tk, N // tn, nm),
        in_specs=[pl.BlockSpec((tm, tk), lambda i, j, m: (m, i)), pl.BlockSpec((tm, tn), lambda i, j, m: (m, j))],
        out_specs=pl.BlockSpec((tk, tn), lambda i, j, m: (i, j)),
        out_shape=jax.ShapeDtypeStruct((K, N), out_dtype),
        scratch_shapes=[pltpu.VMEM((tk, tn), F32)], name=name,
        compiler_params=_params("parallel", "parallel", "arbitrary"),
    )(a, b)


ROW_TILE = 512


def _rms_fwd(x, g, *, name):
    S, Dm = x.shape

    def body(x_ref, g_ref, o_ref):
        xf = x_ref[...]
        r = lax.rsqrt(jnp.mean(xf * xf, axis=-1, keepdims=True) + RMS_EPS)
        o_ref[...] = (xf * r * g_ref[...]).astype(BF)

    return pl.pallas_call(
        body, grid=(S // ROW_TILE,),
        in_specs=[pl.BlockSpec((ROW_TILE, Dm), lambda i: (i, 0)), pl.BlockSpec((1, Dm), lambda i: (0, 0))],
        out_specs=pl.BlockSpec((ROW_TILE, Dm), lambda i: (i, 0)),
        out_shape=jax.ShapeDtypeStruct((S, Dm), BF), name=name, compiler_params=_params("parallel"),
    )(x, g)


def _rms_bwd(x, g, dn, dres, *, name):
    S, Dm = x.shape

    def body(x_ref, g_ref, dn_ref, dres_ref, dx_ref, dxb_ref, dg_ref):
        i = pl.program_id(0)
        xf = x_ref[...]
        r = lax.rsqrt(jnp.mean(xf * xf, axis=-1, keepdims=True) + RMS_EPS)
        xh = xf * r
        dnf = dn_ref[...]
        dyg = dnf * g_ref[...]
        dx = dres_ref[...] + r * (dyg - xh * jnp.mean(dyg * xh, axis=-1, keepdims=True))
        dx_ref[...] = dx
        dxb_ref[...] = dx.astype(BF)

        @pl.when(i == 0)
        def _():
            dg_ref[...] = jnp.zeros_like(dg_ref)

        dg_ref[...] += jnp.sum(dnf * xh, axis=0, keepdims=True)

    row = pl.BlockSpec((ROW_TILE, Dm), lambda i: (i, 0))
    vec = pl.BlockSpec((1, Dm), lambda i: (0, 0))
    return pl.pallas_call(
        body, grid=(S // ROW_TILE,), in_specs=[row, vec, row, row], out_specs=[row, row, vec],
        out_shape=[jax.ShapeDtypeStruct((S, Dm), F32), jax.ShapeDtypeStruct((S, Dm), BF),
                   jax.ShapeDtypeStruct((1, Dm), F32)],
        name=name, compiler_params=_params("arbitrary"),
    )(x, g, dn, dres)


def _loss_head(h, g, tgt, *, name):
    S, Dm = h.shape

    def body(h_ref, g_ref, t_ref, loss_ref, dh_ref, dhb_ref, dg_ref):
        i = pl.program_id(0)
        xf = h_ref[...]
        r = lax.rsqrt(jnp.mean(xf * xf, axis=-1, keepdims=True) + RMS_EPS)
        xh = xf * r
        gv = g_ref[...]
        err = xh * gv - t_ref[...]
        dy = err * (1.0 / Dm)
        dyg = dy * gv
        dh = r * (dyg - xh * jnp.mean(dyg * xh, axis=-1, keepdims=True))
        dh_ref[...] = dh
        dhb_ref[...] = dh.astype(BF)

        @pl.when(i == 0)
        def _():
            dg_ref[...] = jnp.zeros_like(dg_ref)
            loss_ref[...] = jnp.zeros_like(loss_ref)

        dg_ref[...] += jnp.sum(dy * xh, axis=0, keepdims=True)
        part = 0.5 * jnp.sum(jnp.mean(err * err, axis=-1, keepdims=True), axis=0, keepdims=True)
        loss_ref[...] += jnp.broadcast_to(part, loss_ref.shape)

    row = pl.BlockSpec((ROW_TILE, Dm), lambda i: (i, 0))
    vec = pl.BlockSpec((1, Dm), lambda i: (0, 0))
    return pl.pallas_call(
        body, grid=(S // ROW_TILE,), in_specs=[row, vec, row],
        out_specs=[pl.BlockSpec((1, 128), lambda i: (0, 0)), row, row, vec],
        out_shape=[jax.ShapeDtypeStruct((1, 128), F32), jax.ShapeDtypeStruct((S, Dm), F32),
                   jax.ShapeDtypeStruct((S, Dm), BF), jax.ShapeDtypeStruct((1, Dm), F32)],
        name=name, compiler_params=_params("arbitrary"),
    )(h, g, tgt)


SWIGLU_ROWS = 256


def _swiglu_fwd(gu, *, name):
    S = gu.shape[0]

    def body(g_ref, u_ref, o_ref):
        g = g_ref[...].astype(F32)
        sig = 1.0 / (1.0 + jnp.exp(-g))
        o_ref[...] = (g * sig * u_ref[...].astype(F32)).astype(BF)

    return pl.pallas_call(
        body, grid=(S // SWIGLU_ROWS,),
        in_specs=[pl.BlockSpec((SWIGLU_ROWS, D_FF), lambda i: (i, 0)), pl.BlockSpec((SWIGLU_ROWS, D_FF), lambda i: (i, 1))],
        out_specs=pl.BlockSpec((SWIGLU_ROWS, D_FF), lambda i: (i, 0)),
        out_shape=jax.ShapeDtypeStruct((S, D_FF), BF), name=name, compiler_params=_params("parallel"),
    )(gu, gu)


def _swiglu_bwd(gu, dact, *, name):
    S = gu.shape[0]

    def body(g_ref, u_ref, d_ref, o_ref):
        g = g_ref[...].astype(F32)
        u = u_ref[...].astype(F32)
        d = d_ref[...].astype(F32)
        sig = 1.0 / (1.0 + jnp.exp(-g))
        o_ref[:, :D_FF] = (d * u * sig * (1.0 + g * (1.0 - sig))).astype(BF)
        o_ref[:, D_FF:] = (d * g * sig).astype(BF)

    return pl.pallas_call(
        body, grid=(S // SWIGLU_ROWS,),
        in_specs=[pl.BlockSpec((SWIGLU_ROWS, D_FF), lambda i: (i, 0)), pl.BlockSpec((SWIGLU_ROWS, D_FF), lambda i: (i, 1)),
                  pl.BlockSpec((SWIGLU_ROWS, D_FF), lambda i: (i, 0))],
        out_specs=pl.BlockSpec((SWIGLU_ROWS, 2 * D_FF), lambda i: (i, 0)),
        out_shape=jax.ShapeDtypeStruct((S, 2 * D_FF), BF), name=name, compiler_params=_params("parallel"),
    )(gu, gu, dact)


def _attn_fwd(qa, ka, va, qcb, kcb, vcb, *, dil, T, nkv, window, name, c=None, cT=None, o_dtype=F32):
    L = qa.shape[0]
    nq = L // T
    fox = c is not None

    def kv_block(n, j):
        return n - (nkv - 1) + j

    def body(*refs):
        if fox:
            q_ref, k_ref, v_ref, c_ref, ct_ref, o_ref, lse_ref, m_sc, l_sc, acc_sc = refs
        else:
            q_ref, k_ref, v_ref, o_ref, lse_ref, m_sc, l_sc, acc_sc = refs
        n = pl.program_id(1)
        j = pl.program_id(2)
        kb = kv_block(n, j)

        @pl.when(j == 0)
        def _():
            m_sc[...] = jnp.full(m_sc.shape, NEG_INF, F32)
            l_sc[...] = jnp.zeros_like(l_sc)
            acc_sc[...] = jnp.zeros_like(acc_sc)

        @pl.when(kb >= 0)
        def _():
            diff = (n * T + lax.broadcasted_iota(jnp.int32, (T, T), 0)) - (kb * T + lax.broadcasted_iota(jnp.int32, (T, T), 1))
            valid = diff >= 0
            if window is not None:
                valid = jnp.logical_and(valid, diff <= window)
            for h in range(N_HEADS):
                hs = slice(h * HEAD_DIM, (h + 1) * HEAD_DIM)
                qh = q_ref[:, hs] * jnp.asarray(ATTN_SCALE, BF)
                s = lax.dot_general(qh, k_ref[:, hs], (((1,), (1,)), ((), ())), preferred_element_type=F32)
                if fox:
                    s = s + c_ref[:, h:h + 1] - ct_ref[h:h + 1, :]
                s = jnp.where(valid, s, NEG_INF)
                m_prev = m_sc[:, h:h + 1]
                m_new = jnp.maximum(m_prev, jnp.max(s, axis=1, keepdims=True))
                alpha = jnp.exp(m_prev - m_new)
                p = jnp.exp(s - m_new)
                l_sc[:, h:h + 1] = alpha * l_sc[:, h:h + 1] + jnp.sum(p, axis=1, keepdims=True)
                acc_sc[:, hs] = alpha * acc_sc[:, hs] + jnp.dot(p.astype(BF), v_ref[:, hs], preferred_element_type=F32)
                m_sc[:, h:h + 1] = m_new

        @pl.when(j == nkv - 1)
        def _():
            lane = lax.broadcasted_iota(jnp.int32, (T, 128), 1)
            lse = jnp.zeros((T, 128), F32)
            for h in range(N_HEADS):
                hs = slice(h * HEAD_DIM, (h + 1) * HEAD_DIM)
                l = l_sc[:, h:h + 1]
                o_ref[:, hs] = (acc_sc[:, hs] / l).astype(o_dtype)
                lse = jnp.where(lane == h, m_sc[:, h:h + 1] + jnp.log(l), lse)
            lse_ref[...] = lse

    def kvi(n, j):
        return jnp.maximum(kv_block(n, j), 0)

    in_specs = [pl.BlockSpec((T, 1024), lambda r, n, j: (n, qcb(r))),
                pl.BlockSpec((T, 1024), lambda r, n, j: (kvi(n, j), kcb(r))),
                pl.BlockSpec((T, 1024), lambda r, n, j: (kvi(n, j), vcb(r)))]
    args = [qa, ka, va]
    if fox:
        in_specs += [pl.BlockSpec((T, GATE_LANES), lambda r, n, j: (n, 0)),
                     pl.BlockSpec((GATE_LANES, T), lambda r, n, j: (0, kvi(n, j)))]
        args += [c, cT]
    return pl.pallas_call(
        body, grid=(dil, nq, nkv), in_specs=in_specs,
        out_specs=[pl.BlockSpec((T, 1024), lambda r, n, j: (n, r)), pl.BlockSpec((T, 128), lambda r, n, j: (n, r))],
        out_shape=[jax.ShapeDtypeStruct((L, dil * 1024), o_dtype), jax.ShapeDtypeStruct((L, dil * 128), F32)],
        scratch_shapes=[pltpu.VMEM((T, 128), F32), pltpu.VMEM((T, 128), F32), pltpu.VMEM((T, 1024), F32)],
        name=name, compiler_params=_params("parallel", "parallel", "arbitrary"),
    )(*args)


def _attn_bwd(qa, ka, va, qcb, kcb, vcb, doa, oa, lsea, *, dil, T, nqs, window, name, c=None, cT=None):
    L = qa.shape[0]
    nq = L // T
    fox = c is not None

    def body(*refs):
        if fox:
            (q_ref, k_ref, v_ref, do_ref, o_ref, lse_ref, c_ref, ct_ref,
             dq_ref, dk_ref, dv_ref, dct_ref, dcq_ref, dq_sc, dk_sc, dv_sc, dc_sc, dcq_sc) = refs
        else:
            (q_ref, k_ref, v_ref, do_ref, o_ref, lse_ref,
             dq_ref, dk_ref, dv_ref, dq_sc, dk_sc, dv_sc) = refs
        kb = pl.program_id(1)
        jq = pl.program_id(2)
        qb = kb + jq

        @pl.when(jnp.logical_and(kb == 0, jq == 0))
        def _():
            dq_sc[...] = jnp.zeros_like(dq_sc)
            if fox:
                dcq_sc[...] = jnp.zeros_like(dcq_sc)

        @pl.when(jq == 0)
        def _():
            dk_sc[...] = jnp.zeros_like(dk_sc)
            dv_sc[...] = jnp.zeros_like(dv_sc)
            if fox:
                dc_sc[...] = jnp.zeros_like(dc_sc)

        @pl.when(qb < nq)
        def _():
            diff = (qb * T + lax.broadcasted_iota(jnp.int32, (T, T), 0)) - (kb * T + lax.broadcasted_iota(jnp.int32, (T, T), 1))
            valid = diff >= 0
            if window is not None:
                valid = jnp.logical_and(valid, diff <= window)
            qrows = pl.ds(pl.multiple_of(qb * T, T), T)
            for h in range(N_HEADS):
                hs = slice(h * HEAD_DIM, (h + 1) * HEAD_DIM)
                qh = q_ref[:, hs] * jnp.asarray(ATTN_SCALE, BF)
                kh = k_ref[:, hs]
                doh = do_ref[:, hs]
                s = lax.dot_general(qh, kh, (((1,), (1,)), ((), ())), preferred_element_type=F32)
                if fox:
                    s = s + c_ref[:, h:h + 1] - ct_ref[h:h + 1, :]
                s = jnp.where(valid, s, NEG_INF)
                p = jnp.exp(s - lse_ref[:, h:h + 1])
                dp = lax.dot_general(doh, v_ref[:, hs], (((1,), (1,)), ((), ())), preferred_element_type=F32)
                delta = jnp.sum(doh.astype(F32) * o_ref[:, hs].astype(F32), axis=1, keepdims=True)
                ds = p * (dp - delta)
                dsb = ds.astype(BF)
                dv_sc[:, hs] += lax.dot_general(p.astype(BF), doh, (((0,), (0,)), ((), ())), preferred_element_type=F32)
                dk_sc[:, hs] += lax.dot_general(dsb, qh, (((0,), (0,)), ((), ())), preferred_element_type=F32)
                dq_sc[qrows, hs] += jnp.dot(dsb, kh, preferred_element_type=F32) * ATTN_SCALE
                if fox:
                    dc_sc[h:h + 1, :] -= jnp.sum(ds, axis=0, keepdims=True)
                    dcq_sc[qrows, h:h + 1] += jnp.sum(ds, axis=1, keepdims=True)

        @pl.when(jq == nqs - 1)
        def _():
            dk_ref[...] = dk_sc[...].astype(BF)
            dv_ref[...] = dv_sc[...].astype(BF)
            if fox:
                dct_ref[...] = dc_sc[...]

        @pl.when(jnp.logical_and(kb == nq - 1, jq == nqs - 1))
        def _():
            def put(i, carry):
                rows = pl.ds(pl.multiple_of(i * T, T), T)
                dq_ref[rows, :] = dq_sc[rows, :].astype(BF)
                return carry
            lax.fori_loop(0, nq, put, 0)
            if fox:
                dcq_ref[...] = dcq_sc[...]

    def qi(kb, jq):
        return jnp.minimum(kb + jq, nq - 1)

    in_specs = [pl.BlockSpec((T, 1024), lambda r, kb, jq: (qi(kb, jq), qcb(r))),
                pl.BlockSpec((T, 1024), lambda r, kb, jq: (kb, kcb(r))),
                pl.BlockSpec((T, 1024), lambda r, kb, jq: (kb, vcb(r))),
                pl.BlockSpec((T, 1024), lambda r, kb, jq: (qi(kb, jq), r)),
                pl.BlockSpec((T, 1024), lambda r, kb, jq: (qi(kb, jq), r)),
                pl.BlockSpec((T, 128), lambda r, kb, jq: (qi(kb, jq), r))]
    args = [qa, ka, va, doa, oa, lsea]
    out_specs = [pl.BlockSpec((L, 1024), lambda r, kb, jq: (0, r)),
                 pl.BlockSpec((T, 1024), lambda r, kb, jq: (kb, r)),
                 pl.BlockSpec((T, 1024), lambda r, kb, jq: (kb, r))]
    out_shape = [jax.ShapeDtypeStruct((L, dil * 1024), BF)] * 3
    scratch = [pltpu.VMEM((L, 1024), F32), pltpu.VMEM((T, 1024), F32), pltpu.VMEM((T, 1024), F32)]
    if fox:
        in_specs += [pl.BlockSpec((T, GATE_LANES), lambda r, kb, jq: (qi(kb, jq), 0)),
                     pl.BlockSpec((GATE_LANES, T), lambda r, kb, jq: (0, kb))]
        args += [c, cT]
        out_specs.append(pl.BlockSpec((GATE_LANES, T), lambda r, kb, jq: (0, kb)))
        out_shape.append(jax.ShapeDtypeStruct((GATE_LANES, L), F32))
        scratch.append(pltpu.VMEM((GATE_LANES, T), F32))
        out_specs.append(pl.BlockSpec((L, GATE_LANES), lambda r, kb, jq: (0, 0)))
        out_shape.append(jax.ShapeDtypeStruct((L, GATE_LANES), F32))
        scratch.append(pltpu.VMEM((L, GATE_LANES), F32))
    return pl.pallas_call(
        body, grid=(dil, nq, nqs), in_specs=in_specs, out_specs=out_specs, out_shape=out_shape,
        scratch_shapes=scratch, name=name, compiler_params=_params("arbitrary", "arbitrary", "arbitrary"),
    )(*args)


def _band_masks(T, n):
    row = lax.broadcasted_iota(jnp.int32, (T, T), 0)
    col = lax.broadcasted_iota(jnp.int32, (T, T), 1)
    return jnp.logical_and(col >= row, n > 0), col <= row


def _band_fwd(qa, ka, va, qcb, kcb, vcb, *, dil, T, window, name):
    L = qa.shape[0]
    nq = L // T
    assert window == T
    nt = (((1,), (1,)), ((), ()))

    def body(q_ref, kp_ref, kc_ref, vp_ref, vc_ref, o_ref, lse_ref):
        valid_prev, valid_cur = _band_masks(T, pl.program_id(1))
        lane = lax.broadcasted_iota(jnp.int32, (T, 128), 1)
        low = lane < HEAD_DIM
        ones = jnp.ones((T, 128), BF)
        lse = jnp.zeros((T, 128), F32)
        def scores(h):
            ps = slice((h // 2) * 128, (h // 2 + 1) * 128)
            qp = q_ref[:, ps] * jnp.asarray(ATTN_SCALE, BF)
            qm = jnp.where(low if h % 2 == 0 else jnp.logical_not(low), qp, jnp.zeros_like(qp))
            s0 = jnp.where(valid_prev, lax.dot_general(qm, kp_ref[:, ps], nt, preferred_element_type=F32), NEG_INF)
            s1 = jnp.where(valid_cur, lax.dot_general(qm, kc_ref[:, ps], nt, preferred_element_type=F32), NEG_INF)
            return s0, s1

        def softmax(s0, s1):
            m = jnp.maximum(jnp.max(s0, axis=1, keepdims=True), jnp.max(s1, axis=1, keepdims=True))
            return m, jnp.exp(s0 - m).astype(BF), jnp.exp(s1 - m).astype(BF)

        def weighted(h, p0, p1):
            ps = slice((h // 2) * 128, (h // 2 + 1) * 128)
            l = jnp.dot(p0, ones, preferred_element_type=F32) + jnp.dot(p1, ones, preferred_element_type=F32)
            acc = jnp.dot(p0, vp_ref[:, ps], preferred_element_type=F32) + jnp.dot(p1, vc_ref[:, ps], preferred_element_type=F32)
            return l, acc

        sc, pr, even = {}, {}, None
        for t in range(N_HEADS + 2):
            if t < N_HEADS:
                sc[t] = scores(t)
            done = None
            if t >= 2:
                m, p0, p1 = pr.pop(t - 2)
                done = (m,) + weighted(t - 2, p0, p1)
            if 1 <= t <= N_HEADS:
                pr[t - 1] = softmax(*sc.pop(t - 1))
            if done is not None:
                h = t - 2
                m, l, acc = done
                lse = jnp.where(lane == h, m + jnp.log(l), lse)
                if h % 2 == 0:
                    even = acc / l
                else:
                    o_ref[:, (h // 2) * 128:(h // 2 + 1) * 128] = jnp.where(low, even, acc / l)
        lse_ref[...] = lse

    def prev(n):
        return jnp.maximum(n - 1, 0)

    blk = lambda f, cb: pl.BlockSpec((T, 1024), lambda r, n: (f(n), cb(r)))
    same = lambda n: n
    return pl.pallas_call(
        body, grid=(dil, nq),
        in_specs=[blk(same, qcb), blk(prev, kcb), blk(same, kcb), blk(prev, vcb), blk(same, vcb)],
        out_specs=[pl.BlockSpec((T, 1024), lambda r, n: (n, r)), pl.BlockSpec((T, 128), lambda r, n: (n, r))],
        out_shape=[jax.ShapeDtypeStruct((L, dil * 1024), F32), jax.ShapeDtypeStruct((L, dil * 128), F32)],
        name=name, compiler_params=_params("parallel", "parallel"),
    )(qa, ka, ka, va, va)


def _band_bwd(qa, ka, va, qcb, kcb, vcb, doa, oa, lsea, *, dil, T, window, name):
    L = qa.shape[0]
    nq = L // T
    assert window == T
    nt = (((1,), (1,)), ((), ()))
    tn = (((0,), (0,)), ((), ()))

    def body(q_ref, kp_ref, kc_ref, vp_ref, vc_ref, do_ref, o_ref, lse_ref, dq_ref, dk_ref, dv_ref, ck_sc, cv_sc):
        n = pl.program_id(1)

        @pl.when(n == 0)
        def _():
            ck_sc[...] = jnp.zeros_like(ck_sc)
            cv_sc[...] = jnp.zeros_like(cv_sc)

        @pl.when(n < nq)
        def _():
            valid_prev, valid_cur = _band_masks(T, n)
            low = lax.broadcasted_iota(jnp.int32, (T, 128), 1) < HEAD_DIM
            dot = functools.partial(lax.dot_general, preferred_element_type=F32)

            def pair(h):
                return slice((h // 2) * 128, (h // 2 + 1) * 128)

            def products(h):
                ps = pair(h)
                mask = low if h % 2 == 0 else jnp.logical_not(low)
                qp = q_ref[:, ps] * jnp.asarray(ATTN_SCALE, BF)
                dop = do_ref[:, ps]
                qm = jnp.where(mask, qp, jnp.zeros_like(qp))
                dom = jnp.where(mask, dop, jnp.zeros_like(dop))
                s0 = jnp.where(valid_prev, dot(qm, kp_ref[:, ps], nt), NEG_INF)
                s1 = jnp.where(valid_cur, dot(qm, kc_ref[:, ps], nt), NEG_INF)
                return qm, dom, s0, s1, dot(dom, vp_ref[:, ps], nt), dot(dom, vc_ref[:, ps], nt)

            def pointwise(h, qm, dom, s0, s1, dp0, dp1):
                ps = pair(h)
                mask = low if h % 2 == 0 else jnp.logical_not(low)
                prod = do_ref[:, ps].astype(F32) * o_ref[:, ps].astype(F32)
                delta = jnp.sum(jnp.where(mask, prod, 0.0), axis=1, keepdims=True)
                lse = lse_ref[:, h:h + 1]
                p0 = jnp.exp(s0 - lse)
                p1 = jnp.exp(s1 - lse)
                ds0 = (p0 * (dp0 - delta)).astype(BF)
                ds1 = (p1 * (dp1 - delta)).astype(BF)
                return qm, dom, p0.astype(BF), p1.astype(BF), ds0, ds1

            def gradients(h, qm, dom, p0, p1, ds0, ds1):
                ps = pair(h)
                dq = dot(ds0, kp_ref[:, ps], (((1,), (0,)), ((), ()))) + dot(ds1, kc_ref[:, ps], (((1,), (0,)), ((), ())))
                return dq, dot(ds0, qm, tn), dot(p0, dom, tn), dot(ds1, qm, tn), dot(p1, dom, tn)

            st1, st2, even = {}, {}, None
            for t in range(N_HEADS + 2):
                if t < N_HEADS:
                    st1[t] = products(t)
                done = gradients(t - 2, *st2.pop(t - 2)) if t >= 2 else None
                if 1 <= t <= N_HEADS:
                    st2[t - 1] = pointwise(t - 1, *st1.pop(t - 1))
                if done is not None:
                    h = t - 2
                    if h % 2 == 0:
                        even = done
                    else:
                        ps = pair(h)
                        dq_ref[:, ps] = (jnp.where(low, even[0], done[0]) * ATTN_SCALE).astype(BF)
                        dk_ref[:, ps] = (ck_sc[:, ps] + even[1] + done[1]).astype(BF)
                        dv_ref[:, ps] = (cv_sc[:, ps] + even[2] + done[2]).astype(BF)
                        ck_sc[:, ps] = even[3] + done[3]
                        cv_sc[:, ps] = even[4] + done[4]

        @pl.when(n == nq)
        def _():
            dk_ref[...] = ck_sc[...].astype(BF)
            dv_ref[...] = cv_sc[...].astype(BF)

    def cur(n):
        return jnp.minimum(n, nq - 1)

    def prev(n):
        return jnp.maximum(cur(n) - 1, 0)

    blk = lambda f, cb: pl.BlockSpec((T, 1024), lambda r, n: (f(n), cb(r)))
    own = lambda r: r
    return pl.pallas_call(
        body, grid=(dil, nq + 1),
        in_specs=[blk(cur, qcb), blk(prev, kcb), blk(cur, kcb), blk(prev, vcb), blk(cur, vcb), blk(cur, own), blk(cur, own),
                  pl.BlockSpec((T, 128), lambda r, n: (cur(n), r))],
        out_specs=[blk(cur, own), blk(lambda n: jnp.maximum(n - 1, 0), own), blk(lambda n: jnp.maximum(n - 1, 0), own)],
        out_shape=[jax.ShapeDtypeStruct((L, dil * 1024), BF)] * 3,
        scratch_shapes=[pltpu.VMEM((T, 1024), F32), pltpu.VMEM((T, 1024), F32)],
        name=name, compiler_params=_params("arbitrary", "arbitrary"),
    )(qa, ka, ka, va, va, doa, oa, lsea)


FOX_T = 256
FOX_ROWS = 128


def _fox_fwd(qkv, cT, *, name):
    S = qkv.shape[0]
    T, R = FOX_T, FOX_ROWS
    nq = S // T
    nt = (((1,), (1,)), ((), ()))
    chains = [(h, rh) for h in range(N_HEADS) for rh in range(T // R)]

    def body(q_ref, k_ref, v_ref, ct_ref, o_ref, lse_ref, m_sc, l_sc, acc_sc):
        n = pl.program_id(0)
        j = pl.program_id(1)
        lane = lax.broadcasted_iota(jnp.int32, (R, 128), 1)
        low = lane < HEAD_DIM
        ones = jnp.ones((T, 128), BF)

        @pl.when(j == 0)
        def _():
            m_sc[...] = jnp.full(m_sc.shape, NEG_INF, F32)
            l_sc[...] = jnp.zeros_like(l_sc)
            acc_sc[...] = jnp.zeros_like(acc_sc)

        def step(diagonal):
            def pair(h):
                return slice((h // 2) * 128, (h // 2 + 1) * 128)

            def rows(rh):
                return slice(rh * R, (rh + 1) * R)

            def scores(h, rh):
                qp = q_ref[rows(rh), pair(h)] * jnp.asarray(ATTN_SCALE, BF)
                qm = jnp.where(low if h % 2 == 0 else jnp.logical_not(low), qp, jnp.zeros_like(qp))
                s = lax.dot_general(qm, k_ref[:, pair(h)], nt, preferred_element_type=F32) - ct_ref[h:h + 1, :]
                if diagonal:
                    keep = (lax.broadcasted_iota(jnp.int32, (R, T), 1)
                            <= rh * R + lax.broadcasted_iota(jnp.int32, (R, T), 0))
                    s = jnp.where(keep, s, NEG_INF)
                return s

            def softmax(h, rh, s):
                m_prev = m_sc[h, rows(rh), :]
                m_new = jnp.maximum(m_prev, jnp.max(s, axis=1, keepdims=True))
                p = jnp.exp(s - jnp.concatenate([m_new] * (T // 128), axis=1)).astype(BF)
                return m_new, jnp.exp(m_prev - m_new), p

            def weighted(h, p):
                vx = jnp.concatenate([v_ref[:, pair(h)], ones], axis=1)
                return jnp.dot(p, vx, preferred_element_type=F32)

            sc, pr, even = {}, {}, {}
            nch = len(chains)
            for t in range(nch + 2):
                if t < nch:
                    sc[t] = scores(*chains[t])
                done = None
                if t >= 2:
                    m_new, alpha, p = pr.pop(t - 2)
                    done = (m_new, alpha, weighted(chains[t - 2][0], p))
                if 1 <= t <= nch:
                    pr[t - 1] = softmax(*chains[t - 1], sc.pop(t - 1))
                if done is not None:
                    h, rh = chains[t - 2]
                    m_new, alpha, pv = done
                    m_sc[h, rows(rh), :] = m_new
                    l_sc[h, rows(rh), :] = alpha * l_sc[h, rows(rh), :] + pv[:, 128:]
                    if h % 2 == 0:
                        even[rh] = (alpha, pv[:, :128])
                    else:
                        a0, pv0 = even.pop(rh)
                        acc = acc_sc[h // 2, rows(rh), :]
                        acc_sc[h // 2, rows(rh), :] = jnp.where(low, a0 * acc + pv0, alpha * acc + pv[:, :128])

        @pl.when(j < n)
        def _():
            step(False)

        @pl.when(j == n)
        def _():
            step(True)
            lane_t = lax.broadcasted_iota(jnp.int32, (T, 128), 1)
            low_t = lane_t < HEAD_DIM
            lse = jnp.zeros((T, 128), F32)
            for h in range(N_HEADS):
                lse = jnp.where(lane_t == h, m_sc[h] + jnp.log(l_sc[h]), lse)
            lse_ref[...] = lse
            for hp in range(N_HEADS // 2):
                inv = jnp.where(low_t, 1.0 / l_sc[2 * hp], 1.0 / l_sc[2 * hp + 1])
                o_ref[:, hp * 128:(hp + 1) * 128] = (acc_sc[hp] * inv).astype(BF)

    def kv(n, j):
        return jnp.minimum(j, n)

    return pl.pallas_call(
        body, grid=(nq, nq),
        in_specs=[pl.BlockSpec((T, 1024), lambda n, j: (n, 0)), pl.BlockSpec((T, 1024), lambda n, j: (kv(n, j), 1)),
                  pl.BlockSpec((T, 1024), lambda n, j: (kv(n, j), 2)), pl.BlockSpec((GATE_LANES, T), lambda n, j: (0, kv(n, j)))],
        out_specs=[pl.BlockSpec((T, 1024), lambda n, j: (n, 0)), pl.BlockSpec((T, 128), lambda n, j: (n, 0))],
        out_shape=[jax.ShapeDtypeStruct((S, 1024), BF), jax.ShapeDtypeStruct((S, 128), F32)],
        scratch_shapes=[pltpu.VMEM((N_HEADS, T, 128), F32), pltpu.VMEM((N_HEADS, T, 128), F32),
                        pltpu.VMEM((N_HEADS // 2, T, 128), F32)],
        name=name, compiler_params=_params("parallel", "arbitrary"),
    )(qkv, qkv, qkv, cT)


def _fox_bwd(qkv, cT, do, o, lse, *, name):
    S = qkv.shape[0]
    T, R = FOX_T, FOX_ROWS
    nq = S // T
    nt = (((1,), (1,)), ((), ()))
    tn = (((0,), (0,)), ((), ()))
    nn = (((1,), (0,)), ((), ()))
    chains = [(h, rh) for h in range(N_HEADS) for rh in range(T // R)]
    dot = functools.partial(lax.dot_general, preferred_element_type=F32)

    def body(q_ref, k_ref, v_ref, ct_ref, do_ref, o_ref, lse_ref, dq_ref, dk_ref, dv_ref, dct_ref, dcq_ref,
             dq_sc, dk_sc, dv_sc, dc_sc, dcq_sc):
        kb = pl.program_id(0)
        jq = pl.program_id(1)
        qb = kb + jq
        lane = lax.broadcasted_iota(jnp.int32, (R, 128), 1)
        low = lane < HEAD_DIM
        ones_k = jnp.ones((T, 128), BF)
        ones_r = jnp.ones((8, R), BF)

        @pl.when(jnp.logical_and(kb == 0, jq == 0))
        def _():
            dq_sc[...] = jnp.zeros_like(dq_sc)
            dcq_sc[...] = jnp.zeros_like(dcq_sc)

        @pl.when(jq == 0)
        def _():
            dk_sc[...] = jnp.zeros_like(dk_sc)
            dv_sc[...] = jnp.zeros_like(dv_sc)
            dc_sc[...] = jnp.zeros_like(dc_sc)

        def step(diagonal):
            def pair(h):
                return slice((h // 2) * 128, (h // 2 + 1) * 128)

            def rows(rh):
                return slice(rh * R, (rh + 1) * R)

            def qrows(rh):
                return pl.ds(pl.multiple_of(qb * T + rh * R, R), R)

            def products(h, rh):
                mask = low if h % 2 == 0 else jnp.logical_not(low)
                qp = q_ref[rows(rh), pair(h)] * jnp.asarray(ATTN_SCALE, BF)
                dop = do_ref[rows(rh), pair(h)]
                qm = jnp.where(mask, qp, jnp.zeros_like(qp))
                dom = jnp.where(mask, dop, jnp.zeros_like(dop))
                s = dot(qm, k_ref[:, pair(h)], nt) - ct_ref[h:h + 1, :]
                if diagonal:
                    keep = (lax.broadcasted_iota(jnp.int32, (R, T), 1)
                            <= rh * R + lax.broadcasted_iota(jnp.int32, (R, T), 0))
                    s = jnp.where(keep, s, NEG_INF)
                return qm, dom, s, dot(dom, v_ref[:, pair(h)], nt)

            def pointwise(h, rh, qm, dom, s, dp):
                mask = low if h % 2 == 0 else jnp.logical_not(low)
                prod = do_ref[rows(rh), pair(h)].astype(F32) * o_ref[rows(rh), pair(h)].astype(F32)
                delta = jnp.sum(jnp.where(mask, prod, 0.0), axis=1, keepdims=True)
                p = jnp.exp(s - lse_ref[rows(rh), h:h + 1])
                ds = (p * (dp - delta)).astype(BF)
                return qm, dom, p.astype(BF), ds

            def gradients(h, qm, dom, p, ds):
                return (dot(ds, k_ref[:, pair(h)], nn), dot(ds, qm, tn), dot(p, dom, tn),
                        dot(ds, ones_k, nn), dot(ones_r, ds, nn))

            st1, st2, even = {}, {}, {}
            dcq_tiles = [jnp.zeros((R, 128), F32) for _ in range(T // R)]
            nch = len(chains)
            for t in range(nch + 2):
                if t < nch:
                    st1[t] = products(*chains[t])
                done = gradients(chains[t - 2][0], *st2.pop(t - 2)) if t >= 2 else None
                if 1 <= t <= nch:
                    st2[t - 1] = pointwise(*chains[t - 1], *st1.pop(t - 1))
                if done is not None:
                    h, rh = chains[t - 2]
                    dq, dk, dv, rsum, csum = done
                    dcq_tiles[rh] = jnp.where(lane == h, rsum, dcq_tiles[rh])
                    dc_sc[h:h + 1, :] -= csum[0:1, :]
                    if h % 2 == 0:
                        even[rh] = (dq, dk, dv)
                    else:
                        dq0, dk0, dv0 = even.pop(rh)
                        ps = pair(h)
                        dq_sc[qrows(rh), ps] += jnp.where(low, dq0, dq) * ATTN_SCALE
                        dk_sc[:, ps] += dk0 + dk
                        dv_sc[:, ps] += dv0 + dv
            for rh in range(T // R):
                dcq_sc[qrows(rh), :] += dcq_tiles[rh]

        @pl.when(jnp.logical_and(jq > 0, qb < nq))
        def _():
            step(False)

        @pl.when(jq == 0)
        def _():
            step(True)

        @pl.when(jq == nq - 1)
        def _():
            dk_ref[...] = dk_sc[...].astype(BF)
            dv_ref[...] = dv_sc[...].astype(BF)
            dct_ref[...] = dc_sc[...]

        @pl.when(jnp.logical_and(kb == nq - 1, jq == nq - 1))
        def _():
            def put(i, carry):
                r = pl.ds(pl.multiple_of(i * T, T), T)
                dq_ref[r, :] = dq_sc[r, :].astype(BF)
                return carry
            lax.fori_loop(0, nq, put, 0)
            dcq_ref[...] = dcq_sc[...]

    def qi(kb, jq):
        return jnp.minimum(kb + jq, nq - 1)

    qblk = lambda col: pl.BlockSpec((T, 1024), lambda kb, jq: (qi(kb, jq), col))
    kblk = lambda col: pl.BlockSpec((T, 1024), lambda kb, jq: (kb, col))
    whole = pl.BlockSpec((S, 1024), lambda kb, jq: (0, 0))
    return pl.pallas_call(
        body, grid=(nq, nq),
        in_specs=[qblk(0), kblk(1), kblk(2), pl.BlockSpec((GATE_LANES, T), lambda kb, jq: (0, kb)), qblk(0), qblk(0),
                  pl.BlockSpec((T, 128), lambda kb, jq: (qi(kb, jq), 0))],
        out_specs=[whole, kblk(0), kblk(0), pl.BlockSpec((GATE_LANES, T), lambda kb, jq: (0, kb)),
                   pl.BlockSpec((S, GATE_LANES), lambda kb, jq: (0, 0))],
        out_shape=[jax.ShapeDtypeStruct((S, 1024), BF)] * 3 + [jax.ShapeDtypeStruct((GATE_LANES, S), F32),
                                                               jax.ShapeDtypeStruct((S, GATE_LANES), F32)],
        scratch_shapes=[pltpu.VMEM((S, 1024), F32), pltpu.VMEM((T, 1024), F32), pltpu.VMEM((T, 1024), F32),
                        pltpu.VMEM((GATE_LANES, T), F32), pltpu.VMEM((S, GATE_LANES), F32)],
        name=name, compiler_params=_params("arbitrary", "arbitrary"),
    )(qkv, qkv, qkv, cT, do, o, lse)


def _combine_groups(os, lses, *, name):
    S = os[0].shape[0]
    ng = len(os)

    def body(*refs):
        o_refs, l_refs = refs[:ng], refs[ng:2 * ng]
        out_ref, lse_ref = refs[2 * ng], refs[2 * ng + 1]
        ls = [r[...] for r in l_refs]
        m = functools.reduce(jnp.maximum, ls)
        es = [jnp.exp(l - m) for l in ls]
        den = functools.reduce(jnp.add, es)
        ws = [e / den for e in es]
        lse_ref[...] = m + jnp.log(den)
        for h in range(N_HEADS):
            hs = slice(h * HEAD_DIM, (h + 1) * HEAD_DIM)
            acc = ws[0][:, h:h + 1] * o_refs[0][:, hs]
            for g in range(1, ng):
                acc = acc + ws[g][:, h:h + 1] * o_refs[g][:, hs]
            out_ref[:, hs] = acc.astype(BF)

    row = pl.BlockSpec((ROW_TILE, 1024), lambda i: (i, 0))
    stat = pl.BlockSpec((ROW_TILE, 128), lambda i: (i, 0))
    return pl.pallas_call(
        body, grid=(S // ROW_TILE,), in_specs=[row] * ng + [stat] * ng, out_specs=[row, stat],
        out_shape=[jax.ShapeDtypeStruct((S, 1024), BF), jax.ShapeDtypeStruct((S, 128), F32)],
        name=name, compiler_params=_params("parallel"),
    )(*os, *lses)


def _assemble(parts, rope_flags, rope, *, name):
    S = parts[0].shape[0]
    n = len(parts)
    use_rope = any(rope_flags)

    def body(*refs):
        out_ref = refs[-1]
        for b in range(n):
            cols = slice(b * 1024, (b + 1) * 1024)
            if rope_flags[b]:
                cos_ref, sa_ref, sb_ref = refs[n:n + 3]
                out_ref[:, cols] = _rope_rotate(refs[b][...].astype(F32), cos_ref[...], sa_ref[...], sb_ref[...]).astype(BF)
            else:
                out_ref[:, cols] = refs[b][...]

    tm = 256
    row = pl.BlockSpec((tm, 1024), lambda i: (i, 0))
    in_specs = [row] * n
    args = list(parts)
    if use_rope:
        in_specs += [pl.BlockSpec((tm, 128), lambda i: (i, 0))] * 3
        args += list(rope)
    return pl.pallas_call(
        body, grid=(S // tm,), in_specs=in_specs, out_specs=pl.BlockSpec((tm, n * 1024), lambda i: (i, 0)),
        out_shape=jax.ShapeDtypeStruct((S, n * 1024), BF), name=name, compiler_params=_params("parallel"),
    )(*args)


GATE_ROWS = 512


def _gate_fwd(z, bf, *, name):
    S = z.shape[0]

    def body(z_ref, b_ref, c_ref, ct_ref, carry):
        i = pl.program_id(0)

        @pl.when(i == 0)
        def _():
            carry[...] = jnp.zeros_like(carry)

        zz = z_ref[...] + b_ref[...]
        logf = jnp.minimum(zz, 0.0) - jnp.log(1.0 + jnp.exp(-jnp.abs(zz)))
        tri = (lax.broadcasted_iota(jnp.int32, (GATE_ROWS, GATE_ROWS), 0)
               >= lax.broadcasted_iota(jnp.int32, (GATE_ROWS, GATE_ROWS), 1)).astype(F32)
        cs = jnp.dot(tri, logf, precision=lax.Precision.HIGHEST, preferred_element_type=F32) + carry[...]
        c_ref[...] = cs
        ct_ref[...] = cs.T
        carry[...] = cs[GATE_ROWS - 1:GATE_ROWS, :]

    return pl.pallas_call(
        body, grid=(S // GATE_ROWS,),
        in_specs=[pl.BlockSpec((GATE_ROWS, GATE_LANES), lambda i: (i, 0)), pl.BlockSpec((1, GATE_LANES), lambda i: (0, 0))],
        out_specs=[pl.BlockSpec((GATE_ROWS, GATE_LANES), lambda i: (i, 0)), pl.BlockSpec((GATE_LANES, GATE_ROWS), lambda i: (0, i))],
        out_shape=[jax.ShapeDtypeStruct((S, GATE_LANES), F32), jax.ShapeDtypeStruct((GATE_LANES, S), F32)],
        scratch_shapes=[pltpu.VMEM((1, GATE_LANES), F32)], name=name, compiler_params=_params("arbitrary"),
    )(z, bf)


def _gate_bwd(z, bf, dcT, dcq, *, name):
    S = z.shape[0]
    nb = S // GATE_ROWS

    def body(z_ref, b_ref, dct_ref, dcq_ref, dz_ref, db_ref, carry):
        i = pl.program_id(0)

        @pl.when(i == 0)
        def _():
            carry[...] = jnp.zeros_like(carry)
            db_ref[...] = jnp.zeros_like(db_ref)

        dc = dct_ref[...].T + dcq_ref[...]
        tri = (lax.broadcasted_iota(jnp.int32, (GATE_ROWS, GATE_ROWS), 0)
               <= lax.broadcasted_iota(jnp.int32, (GATE_ROWS, GATE_ROWS), 1)).astype(F32)
        dl = jnp.dot(tri, dc, precision=lax.Precision.HIGHEST, preferred_element_type=F32) + carry[...]
        carry[...] = dl[0:1, :]
        zz = z_ref[...] + b_ref[...]
        dz = dl * (1.0 / (1.0 + jnp.exp(zz)))
        lane = lax.broadcasted_iota(jnp.int32, dz.shape, 1)
        dz = jnp.where(lane < N_HEADS, dz, 0.0)
        dz_ref[...] = dz.astype(BF)
        db_ref[...] += jnp.sum(dz, axis=0, keepdims=True)

    return pl.pallas_call(
        body, grid=(nb,),
        in_specs=[pl.BlockSpec((GATE_ROWS, GATE_LANES), lambda i: (nb - 1 - i, 0)), pl.BlockSpec((1, GATE_LANES), lambda i: (0, 0)),
                  pl.BlockSpec((GATE_LANES, GATE_ROWS), lambda i: (0, nb - 1 - i)),
                  pl.BlockSpec((GATE_ROWS, GATE_LANES), lambda i: (nb - 1 - i, 0))],
        out_specs=[pl.BlockSpec((GATE_ROWS, GATE_LANES), lambda i: (nb - 1 - i, 0)), pl.BlockSpec((1, GATE_LANES), lambda i: (0, 0))],
        out_shape=[jax.ShapeDtypeStruct((S, GATE_LANES), BF), jax.ShapeDtypeStruct((1, GATE_LANES), F32)],
        scratch_shapes=[pltpu.VMEM((1, GATE_LANES), F32)], name=name, compiler_params=_params("arbitrary"),
    )(z, bf, dcT, dcq)


def _rope_tables(S):
    half = ROT_DIM // 2
    inv_freq = ROPE_THETA ** (-jnp.arange(half, dtype=F32) * 2.0 / ROT_DIM)
    ang = jnp.arange(S, dtype=F32)[:, None] * inv_freq[None, :]
    cos, sin = jnp.cos(ang), jnp.sin(ang)
    zero = jnp.zeros((S, HEAD_DIM - ROT_DIM), F32)
    zh = jnp.zeros((S, half), F32)
    cos_h = jnp.concatenate([cos, cos, jnp.ones_like(zero)], axis=1)
    sa_h = jnp.concatenate([-sin, zh, zero], axis=1)
    sb_h = jnp.concatenate([zh, sin, zero], axis=1)
    two = lambda t: jnp.concatenate([t, t], axis=1)
    return two(cos_h), two(sa_h), two(sb_h)


def _ffn_fwd(h, norm, w_gu, w_down, tag):
    n = _rms_fwd(h, norm, name=f"ffn{tag}_norm")
    gu = _mm_nn(n, w_gu, tm=1024, tn=512, out_dtype=BF, name=f"ffn{tag}_gu")
    act = _swiglu_fwd(gu, name=f"ffn{tag}_act")
    out = _mm_nn(act, w_down, tm=512, tn=1024, out_dtype=F32, name=f"ffn{tag}_down", resid=h)
    return out, (h, n, gu, act)


def _ffn_bwd(dh, dhb, saved, norm, w_gu, w_down, tag):
    h, n, gu, act = saved
    dact = _mm_nt(dhb, w_down, tm=512, to=1408, tn=1024, out_dtype=BF, name=f"ffn{tag}_dact")
    dw_down = _mm_tn(act, dhb, tk=1408, tn=1024, tm=512, out_dtype=BF, name=f"ffn{tag}_dwdown")
    dgu = _swiglu_bwd(gu, dact, name=f"ffn{tag}_dgu")
    dn = _mm_nt(dgu, w_gu, tm=512, to=1024, tn=1408, out_dtype=F32, name=f"ffn{tag}_dn")
    dw_gu = _mm_tn(n, dgu, tk=1024, tn=1408, tm=512, out_dtype=BF, name=f"ffn{tag}_dwgu")
    dx, dxb, dg = _rms_bwd(h, norm, dn, dh, name=f"ffn{tag}_dnorm")
    return dx, dxb, dg, dw_gu, dw_down


def _local_step(x, tgt, w):
    S = x.shape[0]
    rope_f = _rope_tables(S)
    rope_b = (rope_f[0], -rope_f[1], -rope_f[2])
    g = {}

    n0 = _rms_fwd(x, w["a_norm"], name="a_norm")
    proj = _mm_nn(n0, w["a_w_in"], tm=512, tn=1024, out_dtype=BF, name="a_proj", rope=rope_f)
    o_parts, lse_parts = [], []
    for gi, (window, dil) in enumerate(DILATED_PATTERNS):
        L = S // dil
        pv = proj.reshape(L, dil * proj.shape[1])
        cb = lambda t, gi=gi: (lambda r: r * 9 + gi * 3 + t)
        o_g, lse_g = _band_fwd(pv, pv, pv, cb(0), cb(1), cb(2), dil=dil, T=128, window=window // dil, name=f"a_attn{gi}")
        o_parts.append(o_g.reshape(S, 1024))
        lse_parts.append(lse_g.reshape(S, 128))
    o_a, lse_a = _combine_groups(o_parts, lse_parts, name="a_combine")
    h1 = _mm_nn(o_a, w["a_w_out"], tm=512, tn=1024, out_dtype=F32, name="a_out", resid=x)
    h2, ffn0 = _ffn_fwd(h1, w["ffn_norm"][0:1], w["ffn_w_gu"][0], w["ffn_w_down"][0], 0)

    n2 = _rms_fwd(h2, w["b_norm"], name="b_norm")
    qkv = _mm_nn(n2, w["b_w_qkv"], tm=512, tn=1024, out_dtype=BF, name="b_proj")
    zf = _mm_nn(n2, w["b_w_f"], tm=512, tn=GATE_LANES, out_dtype=F32, name="b_gate_proj")
    c, cT = _gate_fwd(zf, w["b_f"], name="b_gate")
    o_b, lse_b = _fox_fwd(qkv, cT, name="b_attn")
    h3 = _mm_nn(o_b, w["b_w_out"], tm=512, tn=1024, out_dtype=F32, name="b_out", resid=h2)
    h4, ffn1 = _ffn_fwd(h3, w["ffn_norm"][1:2], w["ffn_w_gu"][1], w["ffn_w_down"][1], 1)

    loss, dh4, dh4b, g["final_norm"] = _loss_head(h4, w["final_norm"], tgt, name="loss_head")

    dh3, dh3b, dg_f1, g["ffn_w_gu1"], g["ffn_w_down1"] = _ffn_bwd(dh4, dh4b, ffn1, w["ffn_norm"][1:2], w["ffn_w_gu"][1], w["ffn_w_down"][1], 1)

    do_b = _mm_nt(dh3b, w["b_w_out"], tm=512, to=1024, tn=1024, out_dtype=BF, name="b_do")
    g["b_w_out"] = _mm_tn(o_b, dh3b, tk=1024, tn=1024, tm=512, out_dtype=BF, name="b_dwout")
    dq, dk, dv, dcT, dcq = _fox_bwd(qkv, cT, do_b, o_b, lse_b, name="b_attn_bwd")
    dz, g["b_f"] = _gate_bwd(zf, w["b_f"], dcT, dcq, name="b_gate_bwd")
    dqkv = _assemble([dq, dk, dv], [False] * 3, None, name="b_dproj")
    dn2 = _mm_nt(dz, w["b_w_f"], tm=512, to=1024, tn=GATE_LANES, out_dtype=F32, name="b_dn_gate")
    dn2 = _mm_nt(dqkv, w["b_w_qkv"], tm=512, to=1024, tn=1024, out_dtype=F32, name="b_dn", add=dn2)
    g["b_w_qkv"] = _mm_tn(n2, dqkv, tk=1024, tn=1024, tm=512, out_dtype=BF, name="b_dwqkv")
    g["b_w_f"] = _mm_tn(n2, dz, tk=1024, tn=GATE_LANES, tm=512, out_dtype=BF, name="b_dwf")
    dh2, dh2b, g["b_norm"] = _rms_bwd(h2, w["b_norm"], dn2, dh3, name="b_dnorm")

    dh1, dh1b, dg_f0, g["ffn_w_gu0"], g["ffn_w_down0"] = _ffn_bwd(dh2, dh2b, ffn0, w["ffn_norm"][0:1], w["ffn_w_gu"][0], w["ffn_w_down"][0], 0)
    g["ffn_norm"] = jnp.concatenate([dg_f0, dg_f1], axis=0)

    do_a = _mm_nt(dh1b, w["a_w_out"], tm=512, to=1024, tn=1024, out_dtype=BF, name="a_do")
    g["a_w_out"] = _mm_tn(o_a, dh1b, tk=1024, tn=1024, tm=512, out_dtype=BF, name="a_dwout")
    parts = []
    for gi, (window, dil) in enumerate(DILATED_PATTERNS):
        L = S // dil
        pv = proj.reshape(L, dil * proj.shape[1])
        cb = lambda t, gi=gi: (lambda r: r * 9 + gi * 3 + t)
        dq, dk, dv = _band_bwd(pv, pv, pv, cb(0), cb(1), cb(2), do_a.reshape(L, dil * 1024), o_a.reshape(L, dil * 1024),
                               lse_a.reshape(L, dil * 128), dil=dil, T=128, window=window // dil, name=f"a_attn_bwd{gi}")
        parts += [t.reshape(S, 1024) for t in (dq, dk, dv)]
    dproj = _assemble(parts, [True, True, False] * 3, rope_b, name="a_dproj")
    dn0 = _mm_nt(dproj, w["a_w_in"], tm=512, to=1024, tn=1024, out_dtype=F32, name="a_dn")
    g["a_w_in"] = _mm_tn(n0, dproj, tk=1024, tn=1024, tm=512, out_dtype=BF, name="a_dwin")
    dx, _, g["a_norm"] = _rms_bwd(x, w["a_norm"], dn0, dh1, name="a_dnorm")
    return loss, dx, g


ANY = pl.BlockSpec(memory_space=pl.ANY)


def _place():
    x, y, c = lax.axis_index("x"), lax.axis_index("y"), lax.axis_index("c")
    chips = [(1 - x, y), (x, 1 - y), (1 - x, 1 - y)]
    return x, y, c, chips


def _shard_slice(ref, kind, rows, cols, s, half):
    hr = rows // 2
    if kind == "col":
        return ref.at[pl.ds(half * hr, hr), pl.ds(pl.multiple_of(s * cols, 128), cols)]
    if kind == "row":
        return ref.at[pl.ds(pl.multiple_of(s * rows + half * hr, 16), hr), :]
    return ref.at[s, pl.ds(half * hr, hr), :]


def _whole_shape(kind, rows, cols):
    return {"col": (rows, N_CHIPS * cols), "row": (N_CHIPS * rows, cols), "stack": (N_CHIPS, rows, cols)}[kind]


def _own_block(kind, rows, tr, cols):
    per = rows // tr

    def spec(half_rows):
        off = (lambda p: 0) if half_rows is None else (lambda p: p[1] * (half_rows // tr))
        if kind == "col":
            return pl.BlockSpec((tr, cols), lambda i, p: (off(p) + i, p[0]))
        if kind == "row":
            return pl.BlockSpec((tr, cols), lambda i, p: (p[0] * per + off(p) + i, 0))
        return pl.BlockSpec((None, tr, cols), lambda i, p: (p[0], off(p) + i, 0))
    return spec


def _place_shard(shard, kind, place, *, name):
    rows, cols = shard.shape
    tr = 256 if rows % 256 == 0 else rows // 2

    def body(p_ref, s_ref, o_ref):
        o_ref[...] = s_ref[...].astype(BF)

    return pl.pallas_call(
        body,
        grid_spec=pltpu.PrefetchScalarGridSpec(
            num_scalar_prefetch=1, grid=(rows // tr,),
            in_specs=[pl.BlockSpec((tr, cols), lambda i, p: (i, 0))],
            out_specs=_own_block(kind, rows, tr, cols)(None)),
        out_shape=jax.ShapeDtypeStruct(_whole_shape(kind, rows, cols), BF),
        name=name, compiler_params=_params("arbitrary"),
    )(place, shard)


def _gather_weights(placed, kinds, dims):
    nw = len(placed)

    def body(*refs):
        dst = refs[nw:2 * nw]
        send_sems, recv_sems = refs[2 * nw:]
        x, y, c, chips = _place()
        me = 2 * x + y
        sibling = (x, y, 1 - c)

        def copy(wi, k, s, half, to):
            p = _shard_slice(dst[wi], kinds[wi], dims[wi][0], dims[wi][1], s, half)
            return pltpu.make_async_remote_copy(src_ref=p, dst_ref=p, send_sem=send_sems.at[wi * 6 + k],
                                                recv_sem=recv_sems.at[wi * 6 + k], device_id=to, device_id_type=MESH)

        first, passed = [], []
        for wi in range(nw):
            for j, chip in enumerate(chips):
                cp = copy(wi, j, me, c, (*chip, c))
                cp.start()
                first.append(cp)
        for wi in range(nw):
            for j, chip in enumerate(chips):
                s = 2 * chip[0] + chip[1]
                copy(wi, j, s, c, (x, y, c)).wait_recv()
                cp = copy(wi, 3 + j, s, c, sibling)
                cp.start()
                passed.append(cp)
        for wi in range(nw):
            for j, chip in enumerate(chips):
                s = 2 * chip[0] + chip[1]
                copy(wi, 3 + j, s, 1 - c, (x, y, c)).wait_recv()
        for cp in first + passed:
            cp.wait_send()

    return pl.pallas_call(
        body, in_specs=[ANY] * nw, out_specs=[ANY] * nw,
        out_shape=[jax.ShapeDtypeStruct(p.shape, p.dtype) for p in placed],
        input_output_aliases={wi: wi for wi in range(nw)},
        scratch_shapes=[pltpu.SemaphoreType.DMA((nw * 6,)), pltpu.SemaphoreType.DMA((nw * 6,))],
        name="gather_weights",
    )(*placed)


def _scatter_grads(partials, kinds, dims):
    nw = len(partials)

    def body(*refs):
        src, dst = refs[:nw], refs[nw:2 * nw]
        send_sems, recv_sems = refs[2 * nw:]
        x, y, c, chips = _place()
        me = 2 * x + y
        sent = []
        for wi in range(nw):
            rows, cols = dims[wi]

            def part(s, half, wi=wi, rows=rows, cols=cols):
                return _shard_slice(src[wi], kinds[wi], rows, cols, s, half)

            for j, chip in enumerate(chips):
                s = 2 * chip[0] + chip[1]
                for half in range(2):
                    slot = 2 * j + (c if half == 0 else 1 - c)
                    cp = pltpu.make_async_remote_copy(
                        src_ref=part(s, half), dst_ref=dst[wi].at[slot],
                        send_sem=send_sems.at[wi * 7 + 2 * j + half], recv_sem=recv_sems.at[wi * 7 + slot],
                        device_id=(*chip, half), device_id_type=MESH)
                    cp.start()
                    sent.append(cp)
            cp = pltpu.make_async_remote_copy(
                src_ref=part(me, 1 - c), dst_ref=dst[wi].at[6],
                send_sem=send_sems.at[wi * 7 + 6], recv_sem=recv_sems.at[wi * 7 + 6],
                device_id=(x, y, 1 - c), device_id_type=MESH)
            cp.start()
            sent.append(cp)
        for wi in range(nw):
            for slot in range(7):
                pltpu.make_async_remote_copy(
                    src_ref=dst[wi].at[slot], dst_ref=dst[wi].at[slot],
                    send_sem=send_sems.at[wi * 7 + slot], recv_sem=recv_sems.at[wi * 7 + slot],
                    device_id=(x, y, c), device_id_type=MESH).wait_recv()
        for cp in sent:
            cp.wait_send()

    return pl.pallas_call(
        body, in_specs=[ANY] * nw, out_specs=[ANY] * nw,
        out_shape=[jax.ShapeDtypeStruct((7, d[0] // 2, d[1]), BF) for d in dims],
        scratch_shapes=[pltpu.SemaphoreType.DMA((nw * 7,)), pltpu.SemaphoreType.DMA((nw * 7,))],
        name="scatter_grads",
    )(*partials)


def _sum_slots(slots, partial, kind, dims, place, *, name, into=None, layer=None, n_layers=1):
    rows, cols = dims
    hr = rows // 2
    tr = hr if 8 * hr * cols * 2 <= 6 * 1024 * 1024 else 128
    assert hr % tr == 0

    def body(p_ref, b_ref, own_ref, *rest):
        o_ref = rest[-1]
        acc = own_ref[...].astype(F32)
        for k in range(7):
            acc = acc + b_ref[k].astype(F32)
        o_ref[...] = acc

    half = lambda p: p[1] * (hr // tr)
    if n_layers == 1:
        out_spec = pl.BlockSpec((tr, cols), lambda i, p: (half(p) + i, 0))
        out_shape = jax.ShapeDtypeStruct((rows, cols), F32)
    else:
        out_spec = pl.BlockSpec((None, tr, cols), lambda i, p: (layer, half(p) + i, 0))
        out_shape = jax.ShapeDtypeStruct((n_layers, rows, cols), F32)
    in_specs = [pl.BlockSpec((7, tr, cols), lambda i, p: (0, i, 0)), _own_block(kind, rows, tr, cols)(hr)]
    args = [place, slots, partial]
    aliases = {}
    if into is not None:
        in_specs.append(ANY)
        args.append(into)
        aliases = {3: 0}
    return pl.pallas_call(
        body,
        grid_spec=pltpu.PrefetchScalarGridSpec(num_scalar_prefetch=1, grid=(hr // tr,), in_specs=in_specs, out_specs=out_spec),
        out_shape=out_shape, input_output_aliases=aliases, name=name, compiler_params=_params("arbitrary"),
    )(*args)


def _pair_exchange(bufs, members):
    nw = len(members)

    def body(*refs):
        dst = refs[len(bufs):2 * len(bufs)]
        send_sems, recv_sems = refs[2 * len(bufs):]
        x, y, c, _ = _place()

        def rows_of(wi, half):
            bi, l = members[wi]
            ref = dst[bi] if l is None else dst[bi].at[l]
            hr = ref.shape[0] // 2
            return ref.at[pl.ds(pl.multiple_of(half * hr, 8), hr), :]

        def copy(wi, half, to):
            p = rows_of(wi, half)
            return pltpu.make_async_remote_copy(src_ref=p, dst_ref=p, send_sem=send_sems.at[wi], recv_sem=recv_sems.at[wi],
                                                device_id=to, device_id_type=MESH)

        sent = []
        for wi in range(nw):
            cp = copy(wi, c, (x, y, 1 - c))
            cp.start()
            sent.append(cp)
        for wi in range(nw):
            copy(wi, 1 - c, (x, y, c)).wait_recv()
        for cp in sent:
            cp.wait_send()

    return pl.pallas_call(
        body, in_specs=[ANY] * len(bufs), out_specs=[ANY] * len(bufs),
        out_shape=[jax.ShapeDtypeStruct(b.shape, b.dtype) for b in bufs],
        input_output_aliases={i: i for i in range(len(bufs))},
        scratch_shapes=[pltpu.SemaphoreType.DMA((nw,)), pltpu.SemaphoreType.DMA((nw,))],
        name="pair_exchange",
    )(*bufs)


SMALL_ROWS = 8


def _allreduce_small(v, *, name):
    assert v.shape == (SMALL_ROWS, D_MODEL)

    def body(v_ref, o_ref, buf, send_sems, recv_sems):
        x, y, c, _ = _place()
        me = 4 * x + 2 * y + c
        buf[me] = v_ref[...]
        sent = []
        for k in range(1, 8):
            bx, by, bc = (k >> 2) & 1, (k >> 1) & 1, k & 1
            peer = (1 - x if bx else x, 1 - y if by else y, 1 - c if bc else c)
            cp = pltpu.make_async_remote_copy(src_ref=v_ref, dst_ref=buf.at[me], send_sem=send_sems.at[k - 1],
                                              recv_sem=recv_sems.at[k - 1], device_id=peer, device_id_type=MESH)
            cp.start()
            sent.append(cp)
        for k in range(1, 8):
            bx, by, bc = (k >> 2) & 1, (k >> 1) & 1, k & 1
            peer = 4 * (1 - x if bx else x) + 2 * (1 - y if by else y) + (1 - c if bc else c)
            pltpu.make_async_remote_copy(src_ref=v_ref, dst_ref=buf.at[peer], send_sem=send_sems.at[k - 1],
                                         recv_sem=recv_sems.at[k - 1], device_id=(x, y, c), device_id_type=MESH).wait_recv()
        for cp in sent:
            cp.wait_send()
        acc = buf[0]
        for d in range(1, 8):
            acc = acc + buf[d]
        o_ref[...] = acc

    vmem = pl.BlockSpec(memory_space=pltpu.VMEM)
    return pl.pallas_call(
        body, in_specs=[vmem], out_specs=vmem, out_shape=jax.ShapeDtypeStruct(v.shape, F32),
        scratch_shapes=[pltpu.VMEM((8,) + v.shape, F32), pltpu.SemaphoreType.DMA((7,)), pltpu.SemaphoreType.DMA((7,))],
        name=name,
    )(v)


def _adamw(w, g, m, v, *, name):
    R, C = w.shape
    tr = R
    if R * C * 4 > 1024 * 1024:
        tr = max(t for t in range(8, R, 8) if R % t == 0 and t * C * 4 <= 1024 * 1024)

    def body(w_ref, g_ref, m_ref, v_ref, d_ref, m2_ref, v2_ref):
        gg = g_ref[...]
        m2 = ADAM_B1 * m_ref[...] + (1.0 - ADAM_B1) * gg
        v2 = ADAM_B2 * v_ref[...] + (1.0 - ADAM_B2) * jnp.square(gg)
        m_hat = m2 / (1.0 - ADAM_B1 ** ADAM_STEP)
        v_hat = v2 / (1.0 - ADAM_B2 ** ADAM_STEP)
        d_ref[...] = -ADAM_LR * (m_hat / (jnp.sqrt(v_hat) + ADAM_EPS) + ADAM_WD * w_ref[...])
        m2_ref[...] = m2
        v2_ref[...] = v2

    blk = pl.BlockSpec((tr, C), lambda i: (i, 0))
    out = jax.ShapeDtypeStruct((R, C), F32)
    return pl.pallas_call(
        body, grid=(R // tr,), in_specs=[blk] * 4, out_specs=[blk] * 3, out_shape=[out] * 3,
        name=name, compiler_params=_params("parallel"),
    )(w, g, m, v)


WEIGHT_ORDER = ("a_norm", "a_w_in", "a_w_out", "b_norm", "b_w_in", "b_f", "b_w_out", "ffn_norm", "ffn_w_gu",
                "ffn_w_down", "final_norm")
MATRICES = (("a_w_in", 0, "col"), ("a_w_out", 0, "row"), ("b_w_in", 0, "stack"), ("b_w_out", 0, "row"),
            ("ffn_w_gu", 0, "col"), ("ffn_w_gu", 1, "col"), ("ffn_w_down", 0, "row"), ("ffn_w_down", 1, "row"))
MATRIX_GROUPS = ([0], [1], [2], [3], [4, 5], [6, 7])
GROUP_NAMES = ("a_w_in", "a_w_out", "b_w_in", "b_w_out", "ffn_w_gu", "ffn_w_down")
QKV_COLS = 3 * N_HEADS * HEAD_DIM


def kernel(x, a_norm, a_w_in, a_w_out, b_norm, b_w_in, b_f, b_w_out, ffn_norm, ffn_w_gu, ffn_w_down, final_norm, loss_target, m_a_norm, m_a_w_in, m_a_w_out, m_b_norm, m_b_w_in, m_b_f, m_b_w_out, m_ffn_norm, m_ffn_w_gu, m_ffn_w_down, m_final_norm, v_a_norm, v_a_w_in, v_a_w_out, v_b_norm, v_b_w_in, v_b_f, v_b_w_out, v_ffn_norm, v_ffn_w_gu, v_ffn_w_down, v_final_norm):
    given = dict(a_norm=a_norm, a_w_in=a_w_in, a_w_out=a_w_out, b_norm=b_norm, b_w_in=b_w_in, b_f=b_f, b_w_out=b_w_out,
                 ffn_norm=ffn_norm, ffn_w_gu=ffn_w_gu, ffn_w_down=ffn_w_down, final_norm=final_norm)
    mom_m = dict(a_norm=m_a_norm, a_w_in=m_a_w_in, a_w_out=m_a_w_out, b_norm=m_b_norm, b_w_in=m_b_w_in, b_f=m_b_f,
                 b_w_out=m_b_w_out, ffn_norm=m_ffn_norm, ffn_w_gu=m_ffn_w_gu, ffn_w_down=m_ffn_w_down, final_norm=m_final_norm)
    mom_v = dict(a_norm=v_a_norm, a_w_in=v_a_w_in, a_w_out=v_a_w_out, b_norm=v_b_norm, b_w_in=v_b_w_in, b_f=v_b_f,
                 b_w_out=v_b_w_out, ffn_norm=v_ffn_norm, ffn_w_gu=v_ffn_w_gu, ffn_w_down=v_ffn_w_down, final_norm=v_final_norm)
    chip = 2 * lax.axis_index("x") + lax.axis_index("y")
    core = lax.axis_index("c")
    bn_cols = b_norm.shape[1]

    placed = lax.dynamic_update_slice(jnp.zeros((SMALL_ROWS, D_MODEL), F32), b_norm, (0, chip * bn_cols))
    placed = placed * (core == 0).astype(F32)
    b_norm_full = _allreduce_small(placed, name="gather_b_norm")[0:1]

    place = jnp.stack([chip, core]).astype(jnp.int32)
    kinds = [k for _, _, k in MATRICES]
    dims = [given[n][l].shape for n, l, _ in MATRICES]
    placed = [_place_shard(given[n][l], k, place, name=f"place_{n}{l}") for n, l, k in MATRICES]
    whole = _gather_weights(placed, kinds, dims)
    b_in = whole[2].transpose(1, 0, 2).reshape(D_MODEL, -1)
    gate_cols = b_in.shape[1] - QKV_COLS
    w = dict(a_norm=a_norm, a_w_in=whole[0], a_w_out=whole[1], b_norm=b_norm_full,
             b_w_qkv=b_in[:, :QKV_COLS], b_w_f=jnp.pad(b_in[:, QKV_COLS:], ((0, 0), (0, GATE_LANES - gate_cols))),
             b_f=jnp.pad(b_f, ((0, 0), (0, GATE_LANES - gate_cols))), b_w_out=whole[3],
             ffn_norm=ffn_norm, ffn_w_gu=(whole[4], whole[5]), ffn_w_down=(whole[6], whole[7]),
             final_norm=final_norm.reshape(1, D_MODEL))

    loss, dx, g = _local_step(x[0], loss_target[0], w)

    g_b_in = jnp.concatenate([g["b_w_qkv"], g["b_w_f"][:, :gate_cols]], axis=1)
    g_b_in = g_b_in.reshape(D_MODEL, N_CHIPS, -1).transpose(1, 0, 2)
    partials = [g["a_w_in"], g["a_w_out"], g_b_in, g["b_w_out"], g["ffn_w_gu0"], g["ffn_w_gu1"],
                g["ffn_w_down0"], g["ffn_w_down1"]]
    slots = _scatter_grads(partials, kinds, dims)
    bufs, members = [], []
    for group in MATRIX_GROUPS:
        buf = None
        for l, wi in enumerate(group):
            n = MATRICES[wi][0]
            buf = _sum_slots(slots[wi], partials[wi], kinds[wi], dims[wi], place, name=f"sum_{n}{l}", into=buf,
                             layer=l, n_layers=len(group))
            members.append((len(bufs), l if len(group) > 1 else None))
        bufs.append(buf)
    reduced = dict(zip(GROUP_NAMES, _pair_exchange(bufs, members)))

    small = jnp.concatenate([g["a_norm"], g["b_norm"], g["ffn_norm"], g["final_norm"],
                             jnp.pad(g["b_f"], ((0, 0), (0, D_MODEL - GATE_LANES))),
                             jnp.zeros((SMALL_ROWS - 6, D_MODEL), F32)], axis=0)
    small = _allreduce_small(small, name="allreduce_small")
    grads = dict(reduced)
    grads["a_norm"] = small[0:1]
    grads["b_norm"] = lax.dynamic_slice(small, (1, chip * bn_cols), (1, bn_cols))
    grads["ffn_norm"] = small[2:4]
    grads["final_norm"] = small[4]
    grads["b_f"] = small[5:6, :gate_cols]

    out_g, out_d, out_m, out_v = [], [], [], []
    for n in WEIGHT_ORDER:
        shape = given[n].shape
        two_d = (1, shape[0]) if len(shape) == 1 else (-1, shape[-1])
        d, m2, v2 = _adamw(given[n].reshape(two_d), grads[n].reshape(two_d), mom_m[n].reshape(two_d),
                           mom_v[n].reshape(two_d), name=f"adamw_{n}")
        out_g.append(grads[n].reshape(shape))
        out_d.append(d.reshape(shape))
        out_m.append(m2.reshape(shape))
        out_v.append(v2.reshape(shape))

    total = lax.psum(loss[0, 0], MESH_AXES)
    return (total, dx[None], *out_g, *out_d, *out_m, *out_v)
```

```python
import functools

import jax
import jax.numpy as jnp
from jax import lax
from jax.experimental import pallas as pl
from jax.experimental.pallas import tpu as pltpu

F32 = jnp.float32
BF = jnp.bfloat16

D_MODEL = 1024
N_HEADS = 16
HEAD_DIM = 64
D_FF = 2816
DILATED_PATTERNS = ((128, 1), (512, 4), (2048, 16))
ROT_DIM = 16
ROPE_THETA = 500000.0
RMS_EPS = 1e-6
NEG_INF = -1e30
ATTN_SCALE = HEAD_DIM ** -0.5
GATE_LANES = 128
N_CHIPS = 4
MESH_AXES = ("x", "y", "c")
MESH = pl.DeviceIdType.MESH

ADAM_LR = 0.001
ADAM_B1 = 0.9
ADAM_B2 = 0.999
ADAM_EPS = 1e-08
ADAM_WD = 0.01
ADAM_STEP = 10

VMEM_LIMIT_BYTES = 56 * 1024 * 1024


def _params(*sem):
    return pltpu.CompilerParams(dimension_semantics=sem, vmem_limit_bytes=VMEM_LIMIT_BYTES)


def _hosted_call(body, *, grid, in_specs, out_specs, out_shape, scratch_shapes, args, name, guest=None):
    params = _params(*(["arbitrary"] * len(grid)))
    if guest is None:
        return pl.pallas_call(body, grid=grid, in_specs=in_specs, out_specs=out_specs, out_shape=out_shape,
                              scratch_shapes=scratch_shapes, name=name, compiler_params=params)(*args)
    n_in, n_out, n_scr = len(in_specs), len(out_specs), len(scratch_shapes)
    g_in, g_out = len(guest["args"]), len(guest["out_shape"])
    any_spec = pl.BlockSpec(memory_space=pl.ANY)

    def wrapped(*refs):
        i1 = n_in + g_in
        o1 = i1 + n_out
        o2 = o1 + g_out
        s1 = o2 + n_scr
        guest_refs = (refs[n_in:i1], refs[o1:o2], refs[s1:])
        ids = [pl.program_id(d) for d in range(len(grid))]
        first = functools.reduce(jnp.logical_and, [i == 0 for i in ids])
        last = functools.reduce(jnp.logical_and, [i == g - 1 for i, g in zip(ids, grid)])

        @pl.when(first)
        def _():
            guest["start"](*guest_refs)

        body(*refs[:n_in], *refs[i1:o1], *refs[o2:s1])

        @pl.when(last)
        def _():
            guest["finish"](*guest_refs)

    return pl.pallas_call(
        wrapped, grid=grid, in_specs=list(in_specs) + [any_spec] * g_in, out_specs=list(out_specs) + [any_spec] * g_out,
        out_shape=list(out_shape) + list(guest["out_shape"]), scratch_shapes=list(scratch_shapes) + list(guest["scratch"]),
        name=name, compiler_params=params)(*args, *guest["args"])


def _rope_rotate(t, cos, sin_a, sin_b):
    outs = []
    for cidx in range(t.shape[1] // 128):
        tc = t[:, cidx * 128:(cidx + 1) * 128]
        outs.append(tc * cos + pltpu.roll(tc, 120, 1) * sin_a + pltpu.roll(tc, 8, 1) * sin_b)
    return jnp.concatenate(outs, axis=1)


def _mm_nn(a, b, *, tm, tn, out_dtype, name, resid=None, rope=None):
    M, K = a.shape
    N = b.shape[1]
    assert M % tm == 0 and N % tn == 0 and b.shape[0] == K
    n_in = 2 + (resid is not None) + (3 if rope is not None else 0)

    def body(*refs):
        a_ref, b_ref = refs[0], refs[1]
        o_ref = refs[n_in]
        acc = jnp.dot(a_ref[...], b_ref[...], preferred_element_type=F32)
        if resid is not None:
            acc = acc + refs[2][...]
        if rope is not None:
            cos_ref, sa_ref, sb_ref = refs[n_in - 3:n_in]
            j = pl.program_id(1)

            @pl.when(j % 3 != 2)
            def _():
                o_ref[...] = _rope_rotate(acc, cos_ref[...], sa_ref[...], sb_ref[...]).astype(out_dtype)

            @pl.when(j % 3 == 2)
            def _():
                o_ref[...] = acc.astype(out_dtype)
        else:
            o_ref[...] = acc.astype(out_dtype)

    in_specs = [pl.BlockSpec((tm, K), lambda i, j: (i, 0)), pl.BlockSpec((K, tn), lambda i, j: (0, j))]
    args = [a, b]
    if resid is not None:
        in_specs.append(pl.BlockSpec((tm, tn), lambda i, j: (i, j)))
        args.append(resid)
    if rope is not None:
        assert tn == 1024
        for t in rope:
            in_specs.append(pl.BlockSpec((tm, 128), lambda i, j: (i, 0)))
            args.append(t)
    return pl.pallas_call(
        body, grid=(M // tm, N // tn), in_specs=in_specs,
        out_specs=pl.BlockSpec((tm, tn), lambda i, j: (i, j)),
        out_shape=jax.ShapeDtypeStruct((M, N), out_dtype), name=name,
        compiler_params=_params("parallel", "arbitrary"),
    )(*args)


def _mm_nt(a, b, *, tm, to, tn, out_dtype, name, add=None, guest=None):
    M, N = a.shape
    O = b.shape[0]
    assert M % tm == 0 and O % to == 0 and N % tn == 0 and b.shape[1] == N
    nk = N // tn

    def body(*refs):
        a_ref, b_ref = refs[0], refs[1]
        o_ref, acc_ref = refs[-2], refs[-1]
        k = pl.program_id(2)

        @pl.when(k == 0)
        def _():
            if add is not None:
                acc_ref[...] = refs[2][...]
            else:
                acc_ref[...] = jnp.zeros_like(acc_ref)

        acc_ref[...] += lax.dot_general(a_ref[...], b_ref[...], (((1,), (1,)), ((), ())),
                                        preferred_element_type=F32)

        @pl.when(k == nk - 1)
        def _():
            o_ref[...] = acc_ref[...].astype(out_dtype)

    in_specs = [pl.BlockSpec((tm, tn), lambda i, j, k: (i, k)), pl.BlockSpec((to, tn), lambda i, j, k: (j, k))]
    args = [a, b]
    if add is not None:
        in_specs.append(pl.BlockSpec((tm, to), lambda i, j, k: (i, j)))
        args.append(add)
    outs = _hosted_call(
        body, grid=(M // tm, O // to, nk), in_specs=in_specs,
        out_specs=[pl.BlockSpec((tm, to), lambda i, j, k: (i, j))],
        out_shape=[jax.ShapeDtypeStruct((M, O), out_dtype)],
        scratch_shapes=[pltpu.VMEM((tm, to), F32)], args=args, name=name, guest=guest)
    return outs[0] if guest is None else (outs[0], outs[1:])


def _mm_tn(a, b, *, tk, tn, tm, out_dtype, name):
    M, K = a.shape
    N = b.shape[1]
    assert M % tm == 0 and K % tk == 0 and N % tn == 0 and b.shape[0] == M
    nm = M // tm

    def body(a_ref, b_ref, o_ref, acc_ref):
        m = pl.program_id(2)

        @pl.when(m == 0)
        def _():
            acc_ref[...] = jnp.zeros_like(acc_ref)

        acc_ref[...] += lax.dot_general(a_ref[...], b_ref[...], (((0,), (0,)), ((), ())),
                                        preferred_element_type=F32)

        @pl.when(m == nm - 1)
        def _():
            o_ref[...] = acc_ref[...].astype(out_dtype)

    return pl.pallas_call(
        body, grid=(K // tk, N // tn, nm),
        in_specs=[pl.BlockSpec((tm, tk), lambda i, j, m: (m, i)), pl.BlockSpec((tm, tn), lambda i, j, m: (m, j))],
        out_specs=pl.BlockSpec((tk, tn), lambda i, j, m: (i, j)),
        out_shape=jax.ShapeDtypeStruct((K, N), out_dtype),
        scratch_shapes=[pltpu.VMEM((tk, tn), F32)], name=name,
        compiler_params=_params("parallel", "parallel", "arbitrary"),
    )(a, b)


ROW_TILE = 512


def _rms_fwd(x, g, *, name):
    S, Dm = x.shape

    def body(x_ref, g_ref, o_ref):
        xf = x_ref[...]
        r = lax.rsqrt(jnp.mean(xf * xf, axis=-1, keepdims=True) + RMS_EPS)
        o_ref[...] = (xf * r * g_ref[...]).astype(BF)

    return pl.pallas_call(
        body, grid=(S // ROW_TILE,),
        in_specs=[pl.BlockSpec((ROW_TILE, Dm), lambda i: (i, 0)), pl.BlockSpec((1, Dm), lambda i: (0, 0))],
        out_specs=pl.BlockSpec((ROW_TILE, Dm), lambda i: (i, 0)),
        out_shape=jax.ShapeDtypeStruct((S, Dm), BF), name=name, compiler_params=_params("parallel"),
    )(x, g)


def _rms_bwd(x, g, dn, dres, *, name):
    S, Dm = x.shape

    def body(x_ref, g_ref, dn_ref, dres_ref, dx_ref, dxb_ref, dg_ref):
        i = pl.program_id(0)
        xf = x_ref[...]
        r = lax.rsqrt(jnp.mean(xf * xf, axis=-1, keepdims=True) + RMS_EPS)
        xh = xf * r
        dnf = dn_ref[...]
        dyg = dnf * g_ref[...]
        dx = dres_ref[...] + r * (dyg - xh * jnp.mean(dyg * xh, axis=-1, keepdims=True))
        dx_ref[...] = dx
        dxb_ref[...] = dx.astype(BF)

        @pl.when(i == 0)
        def _():
            dg_ref[...] = jnp.zeros_like(dg_ref)

        dg_ref[...] += jnp.sum(dnf * xh, axis=0, keepdims=True)

    row = pl.BlockSpec((ROW_TILE, Dm), lambda i: (i, 0))
    vec = pl.BlockSpec((1, Dm), lambda i: (0, 0))
    return pl.pallas_call(
        body, grid=(S // ROW_TILE,), in_specs=[row, vec, row, row], out_specs=[row, row, vec],
        out_shape=[jax.ShapeDtypeStruct((S, Dm), F32), jax.ShapeDtypeStruct((S, Dm), BF),
                   jax.ShapeDtypeStruct((1, Dm), F32)],
        name=name, compiler_params=_params("arbitrary"),
    )(x, g, dn, dres)


def _loss_head(h, g, tgt, *, name):
    S, Dm = h.shape

    def body(h_ref, g_ref, t_ref, loss_ref, dh_ref, dhb_ref, dg_ref):
        i = pl.program_id(0)
        xf = h_ref[...]
        r = lax.rsqrt(jnp.mean(xf * xf, axis=-1, keepdims=True) + RMS_EPS)
        xh = xf * r
        gv = g_ref[...]
        err = xh * gv - t_ref[...]
        dy = err * (1.0 / Dm)
        dyg = dy * gv
        dh = r * (dyg - xh * jnp.mean(dyg * xh, axis=-1, keepdims=True))
        dh_ref[...] = dh
        dhb_ref[...] = dh.astype(BF)

        @pl.when(i == 0)
        def _():
            dg_ref[...] = jnp.zeros_like(dg_ref)
            loss_ref[...] = jnp.zeros_like(loss_ref)

        dg_ref[...] += jnp.sum(dy * xh, axis=0, keepdims=True)
        part = 0.5 * jnp.sum(jnp.mean(err * err, axis=-1, keepdims=True), axis=0, keepdims=True)
        loss_ref[...] += jnp.broadcast_to(part, loss_ref.shape)

    row = pl.BlockSpec((ROW_TILE, Dm), lambda i: (i, 0))
    vec = pl.BlockSpec((1, Dm), lambda i: (0, 0))
    return pl.pallas_call(
        body, grid=(S // ROW_TILE,), in_specs=[row, vec, row],
        out_specs=[pl.BlockSpec((1, 128), lambda i: (0, 0)), row, row, vec],
        out_shape=[jax.ShapeDtypeStruct((1, 128), F32), jax.ShapeDtypeStruct((S, Dm), F32),
                   jax.ShapeDtypeStruct((S, Dm), BF), jax.ShapeDtypeStruct((1, Dm), F32)],
        name=name, compiler_params=_params("arbitrary"),
    )(h, g, tgt)


SWIGLU_ROWS = 256


def _swiglu_fwd(gu, *, name):
    S = gu.shape[0]

    def body(g_ref, u_ref, o_ref):
        g = g_ref[...].astype(F32)
        sig = 1.0 / (1.0 + jnp.exp(-g))
        o_ref[...] = (g * sig * u_ref[...].astype(F32)).astype(BF)

    return pl.pallas_call(
        body, grid=(S // SWIGLU_ROWS,),
        in_specs=[pl.BlockSpec((SWIGLU_ROWS, D_FF), lambda i: (i, 0)), pl.BlockSpec((SWIGLU_ROWS, D_FF), lambda i: (i, 1))],
        out_specs=pl.BlockSpec((SWIGLU_ROWS, D_FF), lambda i: (i, 0)),
        out_shape=jax.ShapeDtypeStruct((S, D_FF), BF), name=name, compiler_params=_params("parallel"),
    )(gu, gu)


def _swiglu_bwd(gu, dact, *, name):
    S = gu.shape[0]

    def body(g_ref, u_ref, d_ref, o_ref):
        g = g_ref[...].astype(F32)
        u = u_ref[...].astype(F32)
        d = d_ref[...].astype(F32)
        sig = 1.0 / (1.0 + jnp.exp(-g))
        o_ref[:, :D_FF] = (d * u * sig * (1.0 + g * (1.0 - sig))).astype(BF)
        o_ref[:, D_FF:] = (d * g * sig).astype(BF)

    return pl.pallas_call(
        body, grid=(S // SWIGLU_ROWS,),
        in_specs=[pl.BlockSpec((SWIGLU_ROWS, D_FF), lambda i: (i, 0)), pl.BlockSpec((SWIGLU_ROWS, D_FF), lambda i: (i, 1)),
                  pl.BlockSpec((SWIGLU_ROWS, D_FF), lambda i: (i, 0))],
        out_specs=pl.BlockSpec((SWIGLU_ROWS, 2 * D_FF), lambda i: (i, 0)),
        out_shape=jax.ShapeDtypeStruct((S, 2 * D_FF), BF), name=name, compiler_params=_params("parallel"),
    )(gu, gu, dact)


def _attn_fwd(qa, ka, va, qcb, kcb, vcb, *, dil, T, nkv, window, name, c=None, cT=None, o_dtype=F32):
    L = qa.shape[0]
    nq = L // T
    fox = c is not None

    def kv_block(n, j):
        return n - (nkv - 1) + j

    def body(*refs):
        if fox:
            q_ref, k_ref, v_ref, c_ref, ct_ref, o_ref, lse_ref, m_sc, l_sc, acc_sc = refs
        else:
            q_ref, k_ref, v_ref, o_ref, lse_ref, m_sc, l_sc, acc_sc = refs
        n = pl.program_id(1)
        j = pl.program_id(2)
        kb = kv_block(n, j)

        @pl.when(j == 0)
        def _():
            m_sc[...] = jnp.full(m_sc.shape, NEG_INF, F32)
            l_sc[...] = jnp.zeros_like(l_sc)
            acc_sc[...] = jnp.zeros_like(acc_sc)

        @pl.when(kb >= 0)
        def _():
            diff = (n * T + lax.broadcasted_iota(jnp.int32, (T, T), 0)) - (kb * T + lax.broadcasted_iota(jnp.int32, (T, T), 1))
            valid = diff >= 0
            if window is not None:
                valid = jnp.logical_and(valid, diff <= window)
            for h in range(N_HEADS):
                hs = slice(h * HEAD_DIM, (h + 1) * HEAD_DIM)
                qh = q_ref[:, hs] * jnp.asarray(ATTN_SCALE, BF)
                s = lax.dot_general(qh, k_ref[:, hs], (((1,), (1,)), ((), ())), preferred_element_type=F32)
                if fox:
                    s = s + c_ref[:, h:h + 1] - ct_ref[h:h + 1, :]
                s = jnp.where(valid, s, NEG_INF)
                m_prev = m_sc[:, h:h + 1]
                m_new = jnp.maximum(m_prev, jnp.max(s, axis=1, keepdims=True))
                alpha = jnp.exp(m_prev - m_new)
                p = jnp.exp(s - m_new)
                l_sc[:, h:h + 1] = alpha * l_sc[:, h:h + 1] + jnp.sum(p, axis=1, keepdims=True)
                acc_sc[:, hs] = alpha * acc_sc[:, hs] + jnp.dot(p.astype(BF), v_ref[:, hs], preferred_element_type=F32)
                m_sc[:, h:h + 1] = m_new

        @pl.when(j == nkv - 1)
        def _():
            lane = lax.broadcasted_iota(jnp.int32, (T, 128), 1)
            lse = jnp.zeros((T, 128), F32)
            for h in range(N_HEADS):
                hs = slice(h * HEAD_DIM, (h + 1) * HEAD_DIM)
                l = l_sc[:, h:h + 1]
                o_ref[:, hs] = (acc_sc[:, hs] / l).astype(o_dtype)
                lse = jnp.where(lane == h, m_sc[:, h:h + 1] + jnp.log(l), lse)
            lse_ref[...] = lse

    def kvi(n, j):
        return jnp.maximum(kv_block(n, j), 0)

    in_specs = [pl.BlockSpec((T, 1024), lambda r, n, j: (n, qcb(r))),
                pl.BlockSpec((T, 1024), lambda r, n, j: (kvi(n, j), kcb(r))),
                pl.BlockSpec((T, 1024), lambda r, n, j: (kvi(n, j), vcb(r)))]
    args = [qa, ka, va]
    if fox:
        in_specs += [pl.BlockSpec((T, GATE_LANES), lambda r, n, j: (n, 0)),
                     pl.BlockSpec((GATE_LANES, T), lambda r, n, j: (0, kvi(n, j)))]
        args += [c, cT]
    return pl.pallas_call(
        body, grid=(dil, nq, nkv), in_specs=in_specs,
        out_specs=[pl.BlockSpec((T, 1024), lambda r, n, j: (n, r)), pl.BlockSpec((T, 128), lambda r, n, j: (n, r))],
        out_shape=[jax.ShapeDtypeStruct((L, dil * 1024), o_dtype), jax.ShapeDtypeStruct((L, dil * 128), F32)],
        scratch_shapes=[pltpu.VMEM((T, 128), F32), pltpu.VMEM((T, 128), F32), pltpu.VMEM((T, 1024), F32)],
        name=name, compiler_params=_params("parallel", "parallel", "arbitrary"),
    )(*args)


def _attn_bwd(qa, ka, va, qcb, kcb, vcb, doa, oa, lsea, *, dil, T, nqs, window, name, c=None, cT=None):
    L = qa.shape[0]
    nq = L // T
    fox = c is not None

    def body(*refs):
        if fox:
            (q_ref, k_ref, v_ref, do_ref, o_ref, lse_ref, c_ref, ct_ref,
             dq_ref, dk_ref, dv_ref, dct_ref, dcq_ref, dq_sc, dk_sc, dv_sc, dc_sc, dcq_sc) = refs
        else:
            (q_ref, k_ref, v_ref, do_ref, o_ref, lse_ref,
             dq_ref, dk_ref, dv_ref, dq_sc, dk_sc, dv_sc) = refs
        kb = pl.program_id(1)
        jq = pl.program_id(2)
        qb = kb + jq

        @pl.when(jnp.logical_and(kb == 0, jq == 0))
        def _():
            dq_sc[...] = jnp.zeros_like(dq_sc)
            if fox:
                dcq_sc[...] = jnp.zeros_like(dcq_sc)

        @pl.when(jq == 0)
        def _():
            dk_sc[...] = jnp.zeros_like(dk_sc)
            dv_sc[...] = jnp.zeros_like(dv_sc)
            if fox:
                dc_sc[...] = jnp.zeros_like(dc_sc)

        @pl.when(qb < nq)
        def _():
            diff = (qb * T + lax.broadcasted_iota(jnp.int32, (T, T), 0)) - (kb * T + lax.broadcasted_iota(jnp.int32, (T, T), 1))
            valid = diff >= 0
            if window is not None:
                valid = jnp.logical_and(valid, diff <= window)
            qrows = pl.ds(pl.multiple_of(qb * T, T), T)
            for h in range(N_HEADS):
                hs = slice(h * HEAD_DIM, (h + 1) * HEAD_DIM)
                qh = q_ref[:, hs] * jnp.asarray(ATTN_SCALE, BF)
                kh = k_ref[:, hs]
                doh = do_ref[:, hs]
                s = lax.dot_general(qh, kh, (((1,), (1,)), ((), ())), preferred_element_type=F32)
                if fox:
                    s = s + c_ref[:, h:h + 1] - ct_ref[h:h + 1, :]
                s = jnp.where(valid, s, NEG_INF)
                p = jnp.exp(s - lse_ref[:, h:h + 1])
                dp = lax.dot_general(doh, v_ref[:, hs], (((1,), (1,)), ((), ())), preferred_element_type=F32)
                delta = jnp.sum(doh.astype(F32) * o_ref[:, hs].astype(F32), axis=1, keepdims=True)
                ds = p * (dp - delta)
                dsb = ds.astype(BF)
                dv_sc[:, hs] += lax.dot_general(p.astype(BF), doh, (((0,), (0,)), ((), ())), preferred_element_type=F32)
                dk_sc[:, hs] += lax.dot_general(dsb, qh, (((0,), (0,)), ((), ())), preferred_element_type=F32)
                dq_sc[qrows, hs] += jnp.dot(dsb, kh, preferred_element_type=F32) * ATTN_SCALE
                if fox:
                    dc_sc[h:h + 1, :] -= jnp.sum(ds, axis=0, keepdims=True)
                    dcq_sc[qrows, h:h + 1] += jnp.sum(ds, axis=1, keepdims=True)

        @pl.when(jq == nqs - 1)
        def _():
            dk_ref[...] = dk_sc[...].astype(BF)
            dv_ref[...] = dv_sc[...].astype(BF)
            if fox:
                dct_ref[...] = dc_sc[...]

        @pl.when(jnp.logical_and(kb == nq - 1, jq == nqs - 1))
        def _():
            def put(i, carry):
                rows = pl.ds(pl.multiple_of(i * T, T), T)
                dq_ref[rows, :] = dq_sc[rows, :].astype(BF)
                return carry
            lax.fori_loop(0, nq, put, 0)
            if fox:
                dcq_ref[...] = dcq_sc[...]

    def qi(kb, jq):
        return jnp.minimum(kb + jq, nq - 1)

    in_specs = [pl.BlockSpec((T, 1024), lambda r, kb, jq: (qi(kb, jq), qcb(r))),
                pl.BlockSpec((T, 1024), lambda r, kb, jq: (kb, kcb(r))),
                pl.BlockSpec((T, 1024), lambda r, kb, jq: (kb, vcb(r))),
                pl.BlockSpec((T, 1024), lambda r, kb, jq: (qi(kb, jq), r)),
                pl.BlockSpec((T, 1024), lambda r, kb, jq: (qi(kb, jq), r)),
                pl.BlockSpec((T, 128), lambda r, kb, jq: (qi(kb, jq), r))]
    args = [qa, ka, va, doa, oa, lsea]
    out_specs = [pl.BlockSpec((L, 1024), lambda r, kb, jq: (0, r)),
                 pl.BlockSpec((T, 1024), lambda r, kb, jq: (kb, r)),
                 pl.BlockSpec((T, 1024), lambda r, kb, jq: (kb, r))]
    out_shape = [jax.ShapeDtypeStruct((L, dil * 1024), BF)] * 3
    scratch = [pltpu.VMEM((L, 1024), F32), pltpu.VMEM((T, 1024), F32), pltpu.VMEM((T, 1024), F32)]
    if fox:
        in_specs += [pl.BlockSpec((T, GATE_LANES), lambda r, kb, jq: (qi(kb, jq), 0)),
                     pl.BlockSpec((GATE_LANES, T), lambda r, kb, jq: (0, kb))]
        args += [c, cT]
        out_specs.append(pl.BlockSpec((GATE_LANES, T), lambda r, kb, jq: (0, kb)))
        out_shape.append(jax.ShapeDtypeStruct((GATE_LANES, L), F32))
        scratch.append(pltpu.VMEM((GATE_LANES, T), F32))
        out_specs.append(pl.BlockSpec((L, GATE_LANES), lambda r, kb, jq: (0, 0)))
        out_shape.append(jax.ShapeDtypeStruct((L, GATE_LANES), F32))
        scratch.append(pltpu.VMEM((L, GATE_LANES), F32))
    return pl.pallas_call(
        body, grid=(dil, nq, nqs), in_specs=in_specs, out_specs=out_specs, out_shape=out_shape,
        scratch_shapes=scratch, name=name, compiler_params=_params("arbitrary", "arbitrary", "arbitrary"),
    )(*args)


def _band_masks(T, n):
    row = lax.broadcasted_iota(jnp.int32, (T, T), 0)
    col = lax.broadcasted_iota(jnp.int32, (T, T), 1)
    return jnp.logical_and(col >= row, n > 0), col <= row


def _band_fwd(qa, ka, va, qcb, kcb, vcb, *, dil, T, window, name):
    L = qa.shape[0]
    nq = L // T
    assert window == T
    nt = (((1,), (1,)), ((), ()))

    def body(q_ref, kp_ref, kc_ref, vp_ref, vc_ref, o_ref, lse_ref):
        valid_prev, valid_cur = _band_masks(T, pl.program_id(1))
        lane = lax.broadcasted_iota(jnp.int32, (T, 128), 1)
        low = lane < HEAD_DIM
        ones = jnp.ones((T, 128), BF)
        lse = jnp.zeros((T, 128), F32)
        def scores(h):
            ps = slice((h // 2) * 128, (h // 2 + 1) * 128)
            qp = q_ref[:, ps] * jnp.asarray(ATTN_SCALE, BF)
            qm = jnp.where(low if h % 2 == 0 else jnp.logical_not(low), qp, jnp.zeros_like(qp))
            s0 = jnp.where(valid_prev, lax.dot_general(qm, kp_ref[:, ps], nt, preferred_element_type=F32), NEG_INF)
            s1 = jnp.where(valid_cur, lax.dot_general(qm, kc_ref[:, ps], nt, preferred_element_type=F32), NEG_INF)
            return s0, s1

        def softmax(s0, s1):
            m = jnp.maximum(jnp.max(s0, axis=1, keepdims=True), jnp.max(s1, axis=1, keepdims=True))
            return m, jnp.exp(s0 - m).astype(BF), jnp.exp(s1 - m).astype(BF)

        def weighted(h, p0, p1):
            ps = slice((h // 2) * 128, (h // 2 + 1) * 128)
            l = jnp.dot(p0, ones, preferred_element_type=F32) + jnp.dot(p1, ones, preferred_element_type=F32)
            acc = jnp.dot(p0, vp_ref[:, ps], preferred_element_type=F32) + jnp.dot(p1, vc_ref[:, ps], preferred_element_type=F32)
            return l, acc

        sc, pr, even = {}, {}, None
        for t in range(N_HEADS + 2):
            if t < N_HEADS:
                sc[t] = scores(t)
            done = None
            if t >= 2:
                m, p0, p1 = pr.pop(t - 2)
                done = (m,) + weighted(t - 2, p0, p1)
            if 1 <= t <= N_HEADS:
                pr[t - 1] = softmax(*sc.pop(t - 1))
            if done is not None:
                h = t - 2
                m, l, acc = done
                lse = jnp.where(lane == h, m + jnp.log(l), lse)
                if h % 2 == 0:
                    even = acc / l
                else:
                    o_ref[:, (h // 2) * 128:(h // 2 + 1) * 128] = jnp.where(low, even, acc / l)
        lse_ref[...] = lse

    def prev(n):
        return jnp.maximum(n - 1, 0)

    blk = lambda f, cb: pl.BlockSpec((T, 1024), lambda r, n: (f(n), cb(r)))
    same = lambda n: n
    return pl.pallas_call(
        body, grid=(dil, nq),
        in_specs=[blk(same, qcb), blk(prev, kcb), blk(same, kcb), blk(prev, vcb), blk(same, vcb)],
        out_specs=[pl.BlockSpec((T, 1024), lambda r, n: (n, r)), pl.BlockSpec((T, 128), lambda r, n: (n, r))],
        out_shape=[jax.ShapeDtypeStruct((L, dil * 1024), F32), jax.ShapeDtypeStruct((L, dil * 128), F32)],
        name=name, compiler_params=_params("parallel", "parallel"),
    )(qa, ka, ka, va, va)


def _band_bwd(qa, ka, va, qcb, kcb, vcb, doa, oa, lsea, *, dil, T, window, name, guest=None):
    L = qa.shape[0]
    nq = L // T
    assert window == T
    nt = (((1,), (1,)), ((), ()))
    tn = (((0,), (0,)), ((), ()))

    def body(q_ref, kp_ref, kc_ref, vp_ref, vc_ref, do_ref, o_ref, lse_ref, dq_ref, dk_ref, dv_ref, ck_sc, cv_sc):
        n = pl.program_id(1)

        @pl.when(n == 0)
        def _():
            ck_sc[...] = jnp.zeros_like(ck_sc)
            cv_sc[...] = jnp.zeros_like(cv_sc)

        @pl.when(n < nq)
        def _():
            valid_prev, valid_cur = _band_masks(T, n)
            low = lax.broadcasted_iota(jnp.int32, (T, 128), 1) < HEAD_DIM
            dot = functools.partial(lax.dot_general, preferred_element_type=F32)

            def pair(h):
                return slice((h // 2) * 128, (h // 2 + 1) * 128)

            def products(h):
                ps = pair(h)
                mask = low if h % 2 == 0 else jnp.logical_not(low)
                qp = q_ref[:, ps] * jnp.asarray(ATTN_SCALE, BF)
                dop = do_ref[:, ps]
                qm = jnp.where(mask, qp, jnp.zeros_like(qp))
                dom = jnp.where(mask, dop, jnp.zeros_like(dop))
                s0 = jnp.where(valid_prev, dot(qm, kp_ref[:, ps], nt), NEG_INF)
                s1 = jnp.where(valid_cur, dot(qm, kc_ref[:, ps], nt), NEG_INF)
                return qm, dom, s0, s1, dot(dom, vp_ref[:, ps], nt), dot(dom, vc_ref[:, ps], nt)

            def pointwise(h, qm, dom, s0, s1, dp0, dp1):
                ps = pair(h)
                mask = low if h % 2 == 0 else jnp.logical_not(low)
                prod = do_ref[:, ps].astype(F32) * o_ref[:, ps].astype(F32)
                delta = jnp.sum(jnp.where(mask, prod, 0.0), axis=1, keepdims=True)
                lse = lse_ref[:, h:h + 1]
                p0 = jnp.exp(s0 - lse)
                p1 = jnp.exp(s1 - lse)
                ds0 = (p0 * (dp0 - delta)).astype(BF)
                ds1 = (p1 * (dp1 - delta)).astype(BF)
                return qm, dom, p0.astype(BF), p1.astype(BF), ds0, ds1

            def gradients(h, qm, dom, p0, p1, ds0, ds1):
                ps = pair(h)
                dq = dot(ds0, kp_ref[:, ps], (((1,), (0,)), ((), ()))) + dot(ds1, kc_ref[:, ps], (((1,), (0,)), ((), ())))
                return dq, dot(ds0, qm, tn), dot(p0, dom, tn), dot(ds1, qm, tn), dot(p1, dom, tn)

            st1, st2, even = {}, {}, None
            for t in range(N_HEADS + 2):
                if t < N_HEADS:
                    st1[t] = products(t)
                done = gradients(t - 2, *st2.pop(t - 2)) if t >= 2 else None
                if 1 <= t <= N_HEADS:
                    st2[t - 1] = pointwise(t - 1, *st1.pop(t - 1))
                if done is not None:
                    h = t - 2
                    if h % 2 == 0:
                        even = done
                    else:
                        ps = pair(h)
                        dq_ref[:, ps] = (jnp.where(low, even[0], done[0]) * ATTN_SCALE).astype(BF)
                        dk_ref[:, ps] = (ck_sc[:, ps] + even[1] + done[1]).astype(BF)
                        dv_ref[:, ps] = (cv_sc[:, ps] + even[2] + done[2]).astype(BF)
                        ck_sc[:, ps] = even[3] + done[3]
                        cv_sc[:, ps] = even[4] + done[4]

        @pl.when(n == nq)
        def _():
            dk_ref[...] = ck_sc[...].astype(BF)
            dv_ref[...] = cv_sc[...].astype(BF)

    def cur(n):
        return jnp.minimum(n, nq - 1)

    def prev(n):
        return jnp.maximum(cur(n) - 1, 0)

    blk = lambda f, cb: pl.BlockSpec((T, 1024), lambda r, n: (f(n), cb(r)))
    own = lambda r: r
    outs = _hosted_call(
        body, grid=(dil, nq + 1),
        in_specs=[blk(cur, qcb), blk(prev, kcb), blk(cur, kcb), blk(prev, vcb), blk(cur, vcb), blk(cur, own), blk(cur, own),
                  pl.BlockSpec((T, 128), lambda r, n: (cur(n), r))],
        out_specs=[blk(cur, own), blk(lambda n: jnp.maximum(n - 1, 0), own), blk(lambda n: jnp.maximum(n - 1, 0), own)],
        out_shape=[jax.ShapeDtypeStruct((L, dil * 1024), BF)] * 3,
        scratch_shapes=[pltpu.VMEM((T, 1024), F32), pltpu.VMEM((T, 1024), F32)],
        args=(qa, ka, ka, va, va, doa, oa, lsea), name=name, guest=guest)
    return outs if guest is None else (outs[:3], outs[3:])


FOX_T = 256
FOX_ROWS = 128


def _fox_fwd(qkv, cT, *, name):
    S = qkv.shape[0]
    T, R = FOX_T, FOX_ROWS
    nq = S // T
    nt = (((1,), (1,)), ((), ()))
    chains = [(h, rh) for h in range(N_HEADS) for rh in range(T // R)]

    def body(q_ref, k_ref, v_ref, ct_ref, o_ref, lse_ref, m_sc, l_sc, acc_sc):
        n = pl.program_id(0)
        j = pl.program_id(1)
        lane = lax.broadcasted_iota(jnp.int32, (R, 128), 1)
        low = lane < HEAD_DIM
        ones = jnp.ones((T, 128), BF)

        @pl.when(j == 0)
        def _():
            m_sc[...] = jnp.full(m_sc.shape, NEG_INF, F32)
            l_sc[...] = jnp.zeros_like(l_sc)
            acc_sc[...] = jnp.zeros_like(acc_sc)

        def step(diagonal):
            def pair(h):
                return slice((h // 2) * 128, (h // 2 + 1) * 128)

            def rows(rh):
                return slice(rh * R, (rh + 1) * R)

            def scores(h, rh):
                qp = q_ref[rows(rh), pair(h)] * jnp.asarray(ATTN_SCALE, BF)
                qm = jnp.where(low if h % 2 == 0 else jnp.logical_not(low), qp, jnp.zeros_like(qp))
                s = lax.dot_general(qm, k_ref[:, pair(h)], nt, preferred_element_type=F32) - ct_ref[h:h + 1, :]
                if diagonal:
                    keep = (lax.broadcasted_iota(jnp.int32, (R, T), 1)
                            <= rh * R + lax.broadcasted_iota(jnp.int32, (R, T), 0))
                    s = jnp.where(keep, s, NEG_INF)
                return s

            def softmax(h, rh, s):
                m_prev = m_sc[h, rows(rh), :]
                m_new = jnp.maximum(m_prev, jnp.max(s, axis=1, keepdims=True))
                p = jnp.exp(s - jnp.concatenate([m_new] * (T // 128), axis=1)).astype(BF)
                return m_new, jnp.exp(m_prev - m_new), p

            def weighted(h, p):
                vx = jnp.concatenate([v_ref[:, pair(h)], ones], axis=1)
                return jnp.dot(p, vx, preferred_element_type=F32)

            sc, pr, even = {}, {}, {}
            nch = len(chains)
            for t in range(nch + 2):
                if t < nch:
                    sc[t] = scores(*chains[t])
                done = None
                if t >= 2:
                    m_new, alpha, p = pr.pop(t - 2)
                    done = (m_new, alpha, weighted(chains[t - 2][0], p))
                if 1 <= t <= nch:
                    pr[t - 1] = softmax(*chains[t - 1], sc.pop(t - 1))
                if done is not None:
                    h, rh = chains[t - 2]
                    m_new, alpha, pv = done
                    m_sc[h, rows(rh), :] = m_new
                    l_sc[h, rows(rh), :] = alpha * l_sc[h, rows(rh), :] + pv[:, 128:]
                    if h % 2 == 0:
                        even[rh] = (alpha, pv[:, :128])
                    else:
                        a0, pv0 = even.pop(rh)
                        acc = acc_sc[h // 2, rows(rh), :]
                        acc_sc[h // 2, rows(rh), :] = jnp.where(low, a0 * acc + pv0, alpha * acc + pv[:, :128])

        @pl.when(j < n)
        def _():
            step(False)

        @pl.when(j == n)
        def _():
            step(True)
            lane_t = lax.broadcasted_iota(jnp.int32, (T, 128), 1)
            low_t = lane_t < HEAD_DIM
            lse = jnp.zeros((T, 128), F32)
            for h in range(N_HEADS):
                lse = jnp.where(lane_t == h, m_sc[h] + jnp.log(l_sc[h]), lse)
            lse_ref[...] = lse
            for hp in range(N_HEADS // 2):
                inv = jnp.where(low_t, 1.0 / l_sc[2 * hp], 1.0 / l_sc[2 * hp + 1])
                o_ref[:, hp * 128:(hp + 1) * 128] = (acc_sc[hp] * inv).astype(BF)

    def kv(n, j):
        return jnp.minimum(j, n)

    return pl.pallas_call(
        body, grid=(nq, nq),
        in_specs=[pl.BlockSpec((T, 1024), lambda n, j: (n, 0)), pl.BlockSpec((T, 1024), lambda n, j: (kv(n, j), 1)),
                  pl.BlockSpec((T, 1024), lambda n, j: (kv(n, j), 2)), pl.BlockSpec((GATE_LANES, T), lambda n, j: (0, kv(n, j)))],
        out_specs=[pl.BlockSpec((T, 1024), lambda n, j: (n, 0)), pl.BlockSpec((T, 128), lambda n, j: (n, 0))],
        out_shape=[jax.ShapeDtypeStruct((S, 1024), BF), jax.ShapeDtypeStruct((S, 128), F32)],
        scratch_shapes=[pltpu.VMEM((N_HEADS, T, 128), F32), pltpu.VMEM((N_HEADS, T, 128), F32),
                        pltpu.VMEM((N_HEADS // 2, T, 128), F32)],
        name=name, compiler_params=_params("parallel", "arbitrary"),
    )(qkv, qkv, qkv, cT)


def _fox_bwd(qkv, cT, do, o, lse, *, name, guest=None):
    S = qkv.shape[0]
    T, R = FOX_T, FOX_ROWS
    nq = S // T
    nt = (((1,), (1,)), ((), ()))
    tn = (((0,), (0,)), ((), ()))
    nn = (((1,), (0,)), ((), ()))
    chains = [(h, rh) for h in range(N_HEADS) for rh in range(T // R)]
    dot = functools.partial(lax.dot_general, preferred_element_type=F32)

    def body(q_ref, k_ref, v_ref, ct_ref, do_ref, o_ref, lse_ref, dq_ref, dk_ref, dv_ref, dct_ref, dcq_ref,
             dq_sc, dk_sc, dv_sc, dc_sc, dcq_sc):
        kb = pl.program_id(0)
        jq = pl.program_id(1)
        qb = kb + jq
        lane = lax.broadcasted_iota(jnp.int32, (R, 128), 1)
        low = lane < HEAD_DIM
        ones_k = jnp.ones((T, 128), BF)
        ones_r = jnp.ones((8, R), BF)

        @pl.when(jnp.logical_and(kb == 0, jq == 0))
        def _():
            dq_sc[...] = jnp.zeros_like(dq_sc)
            dcq_sc[...] = jnp.zeros_like(dcq_sc)

        @pl.when(jq == 0)
        def _():
            dk_sc[...] = jnp.zeros_like(dk_sc)
            dv_sc[...] = jnp.zeros_like(dv_sc)
            dc_sc[...] = jnp.zeros_like(dc_sc)

        def step(diagonal):
            def pair(h):
                return slice((h // 2) * 128, (h // 2 + 1) * 128)

            def rows(rh):
                return slice(rh * R, (rh + 1) * R)

            def qrows(rh):
                return pl.ds(pl.multiple_of(qb * T + rh * R, R), R)

            def products(h, rh):
                mask = low if h % 2 == 0 else jnp.logical_not(low)
                qp = q_ref[rows(rh), pair(h)] * jnp.asarray(ATTN_SCALE, BF)
                dop = do_ref[rows(rh), pair(h)]
                qm = jnp.where(mask, qp, jnp.zeros_like(qp))
                dom = jnp.where(mask, dop, jnp.zeros_like(dop))
                s = dot(qm, k_ref[:, pair(h)], nt) - ct_ref[h:h + 1, :]
                if diagonal:
                    keep = (lax.broadcasted_iota(jnp.int32, (R, T), 1)
                            <= rh * R + lax.broadcasted_iota(jnp.int32, (R, T), 0))
                    s = jnp.where(keep, s, NEG_INF)
                return qm, dom, s, dot(dom, v_ref[:, pair(h)], nt)

            def pointwise(h, rh, qm, dom, s, dp):
                mask = low if h % 2 == 0 else jnp.logical_not(low)
                prod = do_ref[rows(rh), pair(h)].astype(F32) * o_ref[rows(rh), pair(h)].astype(F32)
                delta = jnp.sum(jnp.where(mask, prod, 0.0), axis=1, keepdims=True)
                p = jnp.exp(s - lse_ref[rows(rh), h:h + 1])
                ds = (p * (dp - delta)).astype(BF)
                return qm, dom, p.astype(BF), ds

            def gradients(h, qm, dom, p, ds):
                return (dot(ds, k_ref[:, pair(h)], nn), dot(ds, qm, tn), dot(p, dom, tn),
                        dot(ds, ones_k, nn), dot(ones_r, ds, nn))

            st1, st2, even = {}, {}, {}
            dcq_tiles = [jnp.zeros((R, 128), F32) for _ in range(T // R)]
            nch = len(chains)
            for t in range(nch + 2):
                if t < nch:
                    st1[t] = products(*chains[t])
                done = gradients(chains[t - 2][0], *st2.pop(t - 2)) if t >= 2 else None
                if 1 <= t <= nch:
                    st2[t - 1] = pointwise(*chains[t - 1], *st1.pop(t - 1))
                if done is not None:
                    h, rh = chains[t - 2]
                    dq, dk, dv, rsum, csum = done
                    dcq_tiles[rh] = jnp.where(lane == h, rsum, dcq_tiles[rh])
                    dc_sc[h:h + 1, :] -= csum[0:1, :]
                    if h % 2 == 0:
                        even[rh] = (dq, dk, dv)
                    else:
                        dq0, dk0, dv0 = even.pop(rh)
                        ps = pair(h)
                        dq_sc[qrows(rh), ps] += jnp.where(low, dq0, dq) * ATTN_SCALE
                        dk_sc[:, ps] += dk0 + dk
                        dv_sc[:, ps] += dv0 + dv
            for rh in range(T // R):
                dcq_sc[qrows(rh), :] += dcq_tiles[rh]

        @pl.when(jnp.logical_and(jq > 0, qb < nq))
        def _():
            step(False)

        @pl.when(jq == 0)
        def _():
            step(True)

        @pl.when(jq == nq - 1)
        def _():
            dk_ref[...] = dk_sc[...].astype(BF)
            dv_ref[...] = dv_sc[...].astype(BF)
            dct_ref[...] = dc_sc[...]

        @pl.when(jnp.logical_and(kb == nq - 1, jq == nq - 1))
        def _():
            def put(i, carry):
                r = pl.ds(pl.multiple_of(i * T, T), T)
                dq_ref[r, :] = dq_sc[r, :].astype(BF)
                return carry
            lax.fori_loop(0, nq, put, 0)
            dcq_ref[...] = dcq_sc[...]

    def qi(kb, jq):
        return jnp.minimum(kb + jq, nq - 1)

    qblk = lambda col: pl.BlockSpec((T, 1024), lambda kb, jq: (qi(kb, jq), col))
    kblk = lambda col: pl.BlockSpec((T, 1024), lambda kb, jq: (kb, col))
    whole = pl.BlockSpec((S, 1024), lambda kb, jq: (0, 0))
    outs = _hosted_call(
        body, grid=(nq, nq),
        in_specs=[qblk(0), kblk(1), kblk(2), pl.BlockSpec((GATE_LANES, T), lambda kb, jq: (0, kb)), qblk(0), qblk(0),
                  pl.BlockSpec((T, 128), lambda kb, jq: (qi(kb, jq), 0))],
        out_specs=[whole, kblk(0), kblk(0), pl.BlockSpec((GATE_LANES, T), lambda kb, jq: (0, kb)),
                   pl.BlockSpec((S, GATE_LANES), lambda kb, jq: (0, 0))],
        out_shape=[jax.ShapeDtypeStruct((S, 1024), BF)] * 3 + [jax.ShapeDtypeStruct((GATE_LANES, S), F32),
                                                               jax.ShapeDtypeStruct((S, GATE_LANES), F32)],
        scratch_shapes=[pltpu.VMEM((S, 1024), F32), pltpu.VMEM((T, 1024), F32), pltpu.VMEM((T, 1024), F32),
                        pltpu.VMEM((GATE_LANES, T), F32), pltpu.VMEM((S, GATE_LANES), F32)],
        args=(qkv, qkv, qkv, cT, do, o, lse), name=name, guest=guest)
    return outs if guest is None else (outs[:5], outs[5:])


def _combine_groups(os, lses, *, name):
    S = os[0].shape[0]
    ng = len(os)

    def body(*refs):
        o_refs, l_refs = refs[:ng], refs[ng:2 * ng]
        out_ref, lse_ref = refs[2 * ng], refs[2 * ng + 1]
        ls = [r[...] for r in l_refs]
        m = functools.reduce(jnp.maximum, ls)
        es = [jnp.exp(l - m) for l in ls]
        den = functools.reduce(jnp.add, es)
        ws = [e / den for e in es]
        lse_ref[...] = m + jnp.log(den)
        for h in range(N_HEADS):
            hs = slice(h * HEAD_DIM, (h + 1) * HEAD_DIM)
            acc = ws[0][:, h:h + 1] * o_refs[0][:, hs]
            for g in range(1, ng):
                acc = acc + ws[g][:, h:h + 1] * o_refs[g][:, hs]
            out_ref[:, hs] = acc.astype(BF)

    row = pl.BlockSpec((ROW_TILE, 1024), lambda i: (i, 0))
    stat = pl.BlockSpec((ROW_TILE, 128), lambda i: (i, 0))
    return pl.pallas_call(
        body, grid=(S // ROW_TILE,), in_specs=[row] * ng + [stat] * ng, out_specs=[row, stat],
        out_shape=[jax.ShapeDtypeStruct((S, 1024), BF), jax.ShapeDtypeStruct((S, 128), F32)],
        name=name, compiler_params=_params("parallel"),
    )(*os, *lses)


def _assemble(parts, rope_flags, rope, *, name):
    S = parts[0].shape[0]
    n = len(parts)
    use_rope = any(rope_flags)

    def body(*refs):
        out_ref = refs[-1]
        for b in range(n):
            cols = slice(b * 1024, (b + 1) * 1024)
            if rope_flags[b]:
                cos_ref, sa_ref, sb_ref = refs[n:n + 3]
                out_ref[:, cols] = _rope_rotate(refs[b][...].astype(F32), cos_ref[...], sa_ref[...], sb_ref[...]).astype(BF)
            else:
                out_ref[:, cols] = refs[b][...]

    tm = 256
    row = pl.BlockSpec((tm, 1024), lambda i: (i, 0))
    in_specs = [row] * n
    args = list(parts)
    if use_rope:
        in_specs += [pl.BlockSpec((tm, 128), lambda i: (i, 0))] * 3
        args += list(rope)
    return pl.pallas_call(
        body, grid=(S // tm,), in_specs=in_specs, out_specs=pl.BlockSpec((tm, n * 1024), lambda i: (i, 0)),
        out_shape=jax.ShapeDtypeStruct((S, n * 1024), BF), name=name, compiler_params=_params("parallel"),
    )(*args)


GATE_ROWS = 512


def _gate_fwd(z, bf, *, name):
    S = z.shape[0]

    def body(z_ref, b_ref, c_ref, ct_ref, carry):
        i = pl.program_id(0)

        @pl.when(i == 0)
        def _():
            carry[...] = jnp.zeros_like(carry)

        zz = z_ref[...] + b_ref[...]
        logf = jnp.minimum(zz, 0.0) - jnp.log(1.0 + jnp.exp(-jnp.abs(zz)))
        tri = (lax.broadcasted_iota(jnp.int32, (GATE_ROWS, GATE_ROWS), 0)
               >= lax.broadcasted_iota(jnp.int32, (GATE_ROWS, GATE_ROWS), 1)).astype(F32)
        cs = jnp.dot(tri, logf, precision=lax.Precision.HIGHEST, preferred_element_type=F32) + carry[...]
        c_ref[...] = cs
        ct_ref[...] = cs.T
        carry[...] = cs[GATE_ROWS - 1:GATE_ROWS, :]

    return pl.pallas_call(
        body, grid=(S // GATE_ROWS,),
        in_specs=[pl.BlockSpec((GATE_ROWS, GATE_LANES), lambda i: (i, 0)), pl.BlockSpec((1, GATE_LANES), lambda i: (0, 0))],
        out_specs=[pl.BlockSpec((GATE_ROWS, GATE_LANES), lambda i: (i, 0)), pl.BlockSpec((GATE_LANES, GATE_ROWS), lambda i: (0, i))],
        out_shape=[jax.ShapeDtypeStruct((S, GATE_LANES), F32), jax.ShapeDtypeStruct((GATE_LANES, S), F32)],
        scratch_shapes=[pltpu.VMEM((1, GATE_LANES), F32)], name=name, compiler_params=_params("arbitrary"),
    )(z, bf)


def _gate_bwd(z, bf, dcT, dcq, *, name):
    S = z.shape[0]
    nb = S // GATE_ROWS

    def body(z_ref, b_ref, dct_ref, dcq_ref, dz_ref, db_ref, carry):
        i = pl.program_id(0)

        @pl.when(i == 0)
        def _():
            carry[...] = jnp.zeros_like(carry)
            db_ref[...] = jnp.zeros_like(db_ref)

        dc = dct_ref[...].T + dcq_ref[...]
        tri = (lax.broadcasted_iota(jnp.int32, (GATE_ROWS, GATE_ROWS), 0)
               <= lax.broadcasted_iota(jnp.int32, (GATE_ROWS, GATE_ROWS), 1)).astype(F32)
        dl = jnp.dot(tri, dc, precision=lax.Precision.HIGHEST, preferred_element_type=F32) + carry[...]
        carry[...] = dl[0:1, :]
        zz = z_ref[...] + b_ref[...]
        dz = dl * (1.0 / (1.0 + jnp.exp(zz)))
        lane = lax.broadcasted_iota(jnp.int32, dz.shape, 1)
        dz = jnp.where(lane < N_HEADS, dz, 0.0)
        dz_ref[...] = dz.astype(BF)
        db_ref[...] += jnp.sum(dz, axis=0, keepdims=True)

    return pl.pallas_call(
        body, grid=(nb,),
        in_specs=[pl.BlockSpec((GATE_ROWS, GATE_LANES), lambda i: (nb - 1 - i, 0)), pl.BlockSpec((1, GATE_LANES), lambda i: (0, 0)),
                  pl.BlockSpec((GATE_LANES, GATE_ROWS), lambda i: (0, nb - 1 - i)),
                  pl.BlockSpec((GATE_ROWS, GATE_LANES), lambda i: (nb - 1 - i, 0))],
        out_specs=[pl.BlockSpec((GATE_ROWS, GATE_LANES), lambda i: (nb - 1 - i, 0)), pl.BlockSpec((1, GATE_LANES), lambda i: (0, 0))],
        out_shape=[jax.ShapeDtypeStruct((S, GATE_LANES), BF), jax.ShapeDtypeStruct((1, GATE_LANES), F32)],
        scratch_shapes=[pltpu.VMEM((1, GATE_LANES), F32)], name=name, compiler_params=_params("arbitrary"),
    )(z, bf, dcT, dcq)


def _rope_tables(S):
    half = ROT_DIM // 2
    inv_freq = ROPE_THETA ** (-jnp.arange(half, dtype=F32) * 2.0 / ROT_DIM)
    ang = jnp.arange(S, dtype=F32)[:, None] * inv_freq[None, :]
    cos, sin = jnp.cos(ang), jnp.sin(ang)
    zero = jnp.zeros((S, HEAD_DIM - ROT_DIM), F32)
    zh = jnp.zeros((S, half), F32)
    cos_h = jnp.concatenate([cos, cos, jnp.ones_like(zero)], axis=1)
    sa_h = jnp.concatenate([-sin, zh, zero], axis=1)
    sb_h = jnp.concatenate([zh, sin, zero], axis=1)
    two = lambda t: jnp.concatenate([t, t], axis=1)
    return two(cos_h), two(sa_h), two(sb_h)


def _ffn_fwd(h, norm, w_gu, w_down, tag):
    n = _rms_fwd(h, norm, name=f"ffn{tag}_norm")
    gu = _mm_nn(n, w_gu, tm=1024, tn=512, out_dtype=BF, name=f"ffn{tag}_gu")
    act = _swiglu_fwd(gu, name=f"ffn{tag}_act")
    out = _mm_nn(act, w_down, tm=512, tn=1024, out_dtype=F32, name=f"ffn{tag}_down", resid=h)
    return out, (h, n, gu, act)


def _ffn_bwd(dh, dhb, saved, norm, w_gu, w_down, tag, ride=None):
    h, n, gu, act = saved
    dact = _mm_nt(dhb, w_down, tm=512, to=1408, tn=1024, out_dtype=BF, name=f"ffn{tag}_dact")
    dw_down = _mm_tn(act, dhb, tk=1408, tn=1024, tm=512, out_dtype=BF, name=f"ffn{tag}_dwdown")
    dgu = _swiglu_bwd(gu, dact, name=f"ffn{tag}_dgu")
    dn_call = lambda guest: _mm_nt(dgu, w_gu, tm=512, to=1024, tn=1408, out_dtype=F32, name=f"ffn{tag}_dn", guest=guest)
    dn = dn_call(None) if ride is None else ride(dn_call)
    dw_gu = _mm_tn(n, dgu, tk=1024, tn=1408, tm=512, out_dtype=BF, name=f"ffn{tag}_dwgu")
    dx, dxb, dg = _rms_bwd(h, norm, dn, dh, name=f"ffn{tag}_dnorm")
    return dx, dxb, dg, dw_gu, dw_down


def _group_view(proj, gi, dil):
    S = proj.shape[0]
    if dil == 1:
        return proj, (lambda t: (lambda r: gi * 3 + t))
    cols = proj[:, gi * 3 * 1024:(gi + 1) * 3 * 1024]
    return cols.reshape(S // dil, dil * 3 * 1024), (lambda t: (lambda r: r * 3 + t))


def _local_step(x, tgt, w, exchange):
    S = x.shape[0]
    rope_f = _rope_tables(S)
    rope_b = (rope_f[0], -rope_f[1], -rope_f[2])
    g, partial, landed = {}, {}, {}

    def ride(call, indices):
        guest = exchange(indices, [partial[wi] for wi in indices]) if indices else None
        res = call(guest)
        if guest is None:
            return res
        res, outs = res
        landed.update(zip(indices, outs))
        return res

    n0 = _rms_fwd(x, w["a_norm"], name="a_norm")
    proj = _mm_nn(n0, w["a_w_in"], tm=512, tn=1024, out_dtype=BF, name="a_proj", rope=rope_f)
    o_parts, lse_parts = [], []
    for gi, (window, dil) in enumerate(DILATED_PATTERNS):
        pv, cb = _group_view(proj, gi, dil)
        o_g, lse_g = _band_fwd(pv, pv, pv, cb(0), cb(1), cb(2), dil=dil, T=128, window=window // dil, name=f"a_attn{gi}")
        o_parts.append(o_g.reshape(S, 1024))
        lse_parts.append(lse_g.reshape(S, 128))
    o_a, lse_a = _combine_groups(o_parts, lse_parts, name="a_combine")
    h1 = _mm_nn(o_a, w["a_w_out"], tm=512, tn=1024, out_dtype=F32, name="a_out", resid=x)
    h2, ffn0 = _ffn_fwd(h1, w["ffn_norm"][0:1], w["ffn_w_gu"][0], w["ffn_w_down"][0], 0)

    n2 = _rms_fwd(h2, w["b_norm"], name="b_norm")
    qkv = _mm_nn(n2, w["b_w_qkv"], tm=512, tn=1024, out_dtype=BF, name="b_proj")
    zf = _mm_nn(n2, w["b_w_f"], tm=512, tn=GATE_LANES, out_dtype=F32, name="b_gate_proj")
    _, cT = _gate_fwd(zf, w["b_f"], name="b_gate")
    o_b, lse_b = _fox_fwd(qkv, cT, name="b_attn")
    h3 = _mm_nn(o_b, w["b_w_out"], tm=512, tn=1024, out_dtype=F32, name="b_out", resid=h2)
    h4, ffn1 = _ffn_fwd(h3, w["ffn_norm"][1:2], w["ffn_w_gu"][1], w["ffn_w_down"][1], 1)

    loss, dh4, dh4b, g["final_norm"] = _loss_head(h4, w["final_norm"], tgt, name="loss_head")

    dh3, dh3b, dg_f1, partial[5], partial[7] = _ffn_bwd(dh4, dh4b, ffn1, w["ffn_norm"][1:2], w["ffn_w_gu"][1], w["ffn_w_down"][1], 1)

    do_b = _mm_nt(dh3b, w["b_w_out"], tm=512, to=1024, tn=1024, out_dtype=BF, name="b_do")
    partial[3] = _mm_tn(o_b, dh3b, tk=1024, tn=1024, tm=512, out_dtype=BF, name="b_dwout")
    dq, dk, dv, dcT, dcq = ride(lambda guest: _fox_bwd(qkv, cT, do_b, o_b, lse_b, name="b_attn_bwd", guest=guest), [5, 7])
    dz, g["b_f"] = _gate_bwd(zf, w["b_f"], dcT, dcq, name="b_gate_bwd")
    dqkv = _assemble([dq, dk, dv], [False] * 3, None, name="b_dproj")
    dn2 = _mm_nt(dz, w["b_w_f"], tm=512, to=1024, tn=GATE_LANES, out_dtype=F32, name="b_dn_gate")
    dn2 = _mm_nt(dqkv, w["b_w_qkv"], tm=512, to=1024, tn=1024, out_dtype=F32, name="b_dn", add=dn2)
    g_qkv = _mm_tn(n2, dqkv, tk=1024, tn=1024, tm=512, out_dtype=BF, name="b_dwqkv")
    g_f = _mm_tn(n2, dz, tk=1024, tn=GATE_LANES, tm=512, out_dtype=BF, name="b_dwf")
    g_b_in = jnp.concatenate([g_qkv, g_f[:, :N_HEADS]], axis=1)
    partial[2] = g_b_in.reshape(D_MODEL, N_CHIPS, -1).transpose(1, 0, 2)
    dh2, dh2b, g["b_norm"] = _rms_bwd(h2, w["b_norm"], dn2, dh3, name="b_dnorm")

    dh1, dh1b, dg_f0, partial[4], partial[6] = _ffn_bwd(dh2, dh2b, ffn0, w["ffn_norm"][0:1], w["ffn_w_gu"][0], w["ffn_w_down"][0], 0,
                                                      ride=lambda call: ride(call, [2, 3]))
    g["ffn_norm"] = jnp.concatenate([dg_f0, dg_f1], axis=0)

    do_a = _mm_nt(dh1b, w["a_w_out"], tm=512, to=1024, tn=1024, out_dtype=BF, name="a_do")
    partial[1] = _mm_tn(o_a, dh1b, tk=1024, tn=1024, tm=512, out_dtype=BF, name="a_dwout")
    riders = {0: [4], 1: [6, 1], 2: []}
    parts = []
    for gi, (window, dil) in enumerate(DILATED_PATTERNS):
        L = S // dil
        pv, cb = _group_view(proj, gi, dil)
        res = ride(lambda guest: _band_bwd(pv, pv, pv, cb(0), cb(1), cb(2), do_a.reshape(L, dil * 1024),
                                           o_a.reshape(L, dil * 1024), lse_a.reshape(L, dil * 128), dil=dil, T=128,
                                           window=window // dil, name=f"a_attn_bwd{gi}", guest=guest), riders[gi])
        parts += [t.reshape(S, 1024) for t in res]
    dproj = _assemble(parts, [True, True, False] * 3, rope_b, name="a_dproj")
    partial[0] = _mm_tn(n0, dproj, tk=1024, tn=1024, tm=512, out_dtype=BF, name="a_dwin")
    dn0 = ride(lambda guest: _mm_nt(dproj, w["a_w_in"], tm=512, to=1024, tn=1024, out_dtype=F32, name="a_dn", guest=guest), [0])
    dx, _, g["a_norm"] = _rms_bwd(x, w["a_norm"], dn0, dh1, name="a_dnorm")
    return loss, dx, g, partial, landed


ANY = pl.BlockSpec(memory_space=pl.ANY)


def _place():
    x, y, c = lax.axis_index("x"), lax.axis_index("y"), lax.axis_index("c")
    chips = [(1 - x, y), (x, 1 - y), (1 - x, 1 - y)]
    return x, y, c, chips


def _shard_slice(ref, kind, rows, cols, s, half):
    hr = rows // 2
    if kind == "col":
        return ref.at[pl.ds(half * hr, hr), pl.ds(pl.multiple_of(s * cols, 128), cols)]
    if kind == "row":
        return ref.at[pl.ds(pl.multiple_of(s * rows + half * hr, 16), hr), :]
    return ref.at[s, pl.ds(half * hr, hr), :]


def _whole_shape(kind, rows, cols):
    return {"col": (rows, N_CHIPS * cols), "row": (N_CHIPS * rows, cols), "stack": (N_CHIPS, rows, cols)}[kind]


def _own_block(kind, rows, tr, cols):
    per = rows // tr

    def spec(half_rows):
        off = (lambda p: 0) if half_rows is None else (lambda p: p[1] * (half_rows // tr))
        if kind == "col":
            return pl.BlockSpec((tr, cols), lambda i, p: (off(p) + i, p[0]))
        if kind == "row":
            return pl.BlockSpec((tr, cols), lambda i, p: (p[0] * per + off(p) + i, 0))
        return pl.BlockSpec((None, tr, cols), lambda i, p: (p[0], off(p) + i, 0))
    return spec


def _place_shard(shard, kind, place, *, name):
    rows, cols = shard.shape
    tr = 256 if rows % 256 == 0 else rows // 2

    def body(p_ref, s_ref, o_ref):
        o_ref[...] = s_ref[...].astype(BF)

    return pl.pallas_call(
        body,
        grid_spec=pltpu.PrefetchScalarGridSpec(
            num_scalar_prefetch=1, grid=(rows // tr,),
            in_specs=[pl.BlockSpec((tr, cols), lambda i, p: (i, 0))],
            out_specs=_own_block(kind, rows, tr, cols)(None)),
        out_shape=jax.ShapeDtypeStruct(_whole_shape(kind, rows, cols), BF),
        name=name, compiler_params=_params("arbitrary"),
    )(place, shard)


def _gather_weights(placed, kinds, dims):
    nw = len(placed)

    def body(*refs):
        dst = refs[nw:2 * nw]
        send_sems, recv_sems = refs[2 * nw:]
        x, y, c, chips = _place()
        me = 2 * x + y
        sibling = (x, y, 1 - c)

        def copy(wi, k, s, half, to):
            p = _shard_slice(dst[wi], kinds[wi], dims[wi][0], dims[wi][1], s, half)
            return pltpu.make_async_remote_copy(src_ref=p, dst_ref=p, send_sem=send_sems.at[wi * 6 + k],
                                                recv_sem=recv_sems.at[wi * 6 + k], device_id=to, device_id_type=MESH)

        first, passed = [], []
        for wi in range(nw):
            for j, chip in enumerate(chips):
                cp = copy(wi, j, me, c, (*chip, c))
                cp.start()
                first.append(cp)
        for wi in range(nw):
            for j, chip in enumerate(chips):
                s = 2 * chip[0] + chip[1]
                copy(wi, j, s, c, (x, y, c)).wait_recv()
                cp = copy(wi, 3 + j, s, c, sibling)
                cp.start()
                passed.append(cp)
        for wi in range(nw):
            for j, chip in enumerate(chips):
                s = 2 * chip[0] + chip[1]
                copy(wi, 3 + j, s, 1 - c, (x, y, c)).wait_recv()
        for cp in first + passed:
            cp.wait_send()

    return pl.pallas_call(
        body, in_specs=[ANY] * nw, out_specs=[ANY] * nw,
        out_shape=[jax.ShapeDtypeStruct(p.shape, p.dtype) for p in placed],
        input_output_aliases={wi: wi for wi in range(nw)},
        scratch_shapes=[pltpu.SemaphoreType.DMA((nw * 6,)), pltpu.SemaphoreType.DMA((nw * 6,))],
        name="gather_weights",
    )(*placed)


def _scatter_guest(partials, kinds, dims):
    nw = len(partials)

    def copies(src, send_sems, recv_sems, dst):
        x, y, c, chips = _place()
        me = 2 * x + y
        out = []
        for wi in range(nw):
            rows, cols = dims[wi]

            def part(s, half, wi=wi, rows=rows, cols=cols):
                return _shard_slice(src[wi], kinds[wi], rows, cols, s, half)

            for j, chip in enumerate(chips):
                s = 2 * chip[0] + chip[1]
                for half in range(2):
                    slot = 2 * j + (c if half == 0 else 1 - c)
                    out.append(pltpu.make_async_remote_copy(
                        src_ref=part(s, half), dst_ref=dst[wi].at[slot],
                        send_sem=send_sems.at[wi * 7 + 2 * j + half], recv_sem=recv_sems.at[wi * 7 + slot],
                        device_id=(*chip, half), device_id_type=MESH))
            out.append(pltpu.make_async_remote_copy(
                src_ref=part(me, 1 - c), dst_ref=dst[wi].at[6],
                send_sem=send_sems.at[wi * 7 + 6], recv_sem=recv_sems.at[wi * 7 + 6],
                device_id=(x, y, 1 - c), device_id_type=MESH))
        return out

    def start(src, dst, sems):
        for cp in copies(src, sems[0], sems[1], dst):
            cp.start()

    def finish(src, dst, sems):
        x, y, c, _ = _place()
        for wi in range(nw):
            for slot in range(7):
                pltpu.make_async_remote_copy(
                    src_ref=dst[wi].at[slot], dst_ref=dst[wi].at[slot],
                    send_sem=sems[0].at[wi * 7 + slot], recv_sem=sems[1].at[wi * 7 + slot],
                    device_id=(x, y, c), device_id_type=MESH).wait_recv()
        for cp in copies(src, sems[0], sems[1], dst):
            cp.wait_send()

    return dict(args=list(partials), out_shape=[jax.ShapeDtypeStruct((7, d[0] // 2, d[1]), BF) for d in dims],
                scratch=[pltpu.SemaphoreType.DMA((nw * 7,)), pltpu.SemaphoreType.DMA((nw * 7,))],
                start=start, finish=finish)


def _scatter_grads(partials, kinds, dims, *, name):
    guest = _scatter_guest(partials, kinds, dims)
    nw = len(partials)

    def body(*refs):
        parts = (refs[:nw], refs[nw:2 * nw], refs[2 * nw:])
        guest["start"](*parts)
        guest["finish"](*parts)

    return pl.pallas_call(body, in_specs=[ANY] * nw, out_specs=[ANY] * nw, out_shape=guest["out_shape"],
                          scratch_shapes=guest["scratch"], name=name)(*partials)


def _sum_slots(slots, partial, kind, dims, place, *, name, into=None, layer=None, n_layers=1):
    rows, cols = dims
    hr = rows // 2
    tr = hr if 8 * hr * cols * 2 <= 6 * 1024 * 1024 else 128
    assert hr % tr == 0

    def body(p_ref, b_ref, own_ref, *rest):
        o_ref = rest[-1]
        acc = own_ref[...].astype(F32)
        for k in range(7):
            acc = acc + b_ref[k].astype(F32)
        o_ref[...] = acc

    half = lambda p: p[1] * (hr // tr)
    if n_layers == 1:
        out_spec = pl.BlockSpec((tr, cols), lambda i, p: (half(p) + i, 0))
        out_shape = jax.ShapeDtypeStruct((rows, cols), F32)
    else:
        out_spec = pl.BlockSpec((None, tr, cols), lambda i, p: (layer, half(p) + i, 0))
        out_shape = jax.ShapeDtypeStruct((n_layers, rows, cols), F32)
    in_specs = [pl.BlockSpec((7, tr, cols), lambda i, p: (0, i, 0)), _own_block(kind, rows, tr, cols)(hr)]
    args = [place, slots, partial]
    aliases = {}
    if into is not None:
        in_specs.append(ANY)
        args.append(into)
        aliases = {3: 0}
    return pl.pallas_call(
        body,
        grid_spec=pltpu.PrefetchScalarGridSpec(num_scalar_prefetch=1, grid=(hr // tr,), in_specs=in_specs, out_specs=out_spec),
        out_shape=out_shape, input_output_aliases=aliases, name=name, compiler_params=_params("arbitrary"),
    )(*args)


def _pair_exchange(bufs, members):
    nw = len(members)

    def body(*refs):
        dst = refs[len(bufs):2 * len(bufs)]
        send_sems, recv_sems = refs[2 * len(bufs):]
        x, y, c, _ = _place()

        def rows_of(wi, half):
            bi, l = members[wi]
            ref = dst[bi] if l is None else dst[bi].at[l]
            hr = ref.shape[0] // 2
            return ref.at[pl.ds(pl.multiple_of(half * hr, 8), hr), :]

        def copy(wi, half, to):
            p = rows_of(wi, half)
            return pltpu.make_async_remote_copy(src_ref=p, dst_ref=p, send_sem=send_sems.at[wi], recv_sem=recv_sems.at[wi],
                                                device_id=to, device_id_type=MESH)

        sent = []
        for wi in range(nw):
            cp = copy(wi, c, (x, y, 1 - c))
            cp.start()
            sent.append(cp)
        for wi in range(nw):
            copy(wi, 1 - c, (x, y, c)).wait_recv()
        for cp in sent:
            cp.wait_send()

    return pl.pallas_call(
        body, in_specs=[ANY] * len(bufs), out_specs=[ANY] * len(bufs),
        out_shape=[jax.ShapeDtypeStruct(b.shape, b.dtype) for b in bufs],
        input_output_aliases={i: i for i in range(len(bufs))},
        scratch_shapes=[pltpu.SemaphoreType.DMA((nw,)), pltpu.SemaphoreType.DMA((nw,))],
        name="pair_exchange",
    )(*bufs)


SMALL_ROWS = 8


def _allreduce_small(v, *, name):
    assert v.shape == (SMALL_ROWS, D_MODEL)

    def body(v_ref, o_ref, buf, send_sems, recv_sems):
        x, y, c, _ = _place()
        me = 4 * x + 2 * y + c
        buf[me] = v_ref[...]
        sent = []
        for k in range(1, 8):
            bx, by, bc = (k >> 2) & 1, (k >> 1) & 1, k & 1
            peer = (1 - x if bx else x, 1 - y if by else y, 1 - c if bc else c)
            cp = pltpu.make_async_remote_copy(src_ref=v_ref, dst_ref=buf.at[me], send_sem=send_sems.at[k - 1],
                                              recv_sem=recv_sems.at[k - 1], device_id=peer, device_id_type=MESH)
            cp.start()
            sent.append(cp)
        for k in range(1, 8):
            bx, by, bc = (k >> 2) & 1, (k >> 1) & 1, k & 1
            peer = 4 * (1 - x if bx else x) + 2 * (1 - y if by else y) + (1 - c if bc else c)
            pltpu.make_async_remote_copy(src_ref=v_ref, dst_ref=buf.at[peer], send_sem=send_sems.at[k - 1],
                                         recv_sem=recv_sems.at[k - 1], device_id=(x, y, c), device_id_type=MESH).wait_recv()
        for cp in sent:
            cp.wait_send()
        acc = buf[0]
        for d in range(1, 8):
            acc = acc + buf[d]
        o_ref[...] = acc

    vmem = pl.BlockSpec(memory_space=pltpu.VMEM)
    return pl.pallas_call(
        body, in_specs=[vmem], out_specs=vmem, out_shape=jax.ShapeDtypeStruct(v.shape, F32),
        scratch_shapes=[pltpu.VMEM((8,) + v.shape, F32), pltpu.SemaphoreType.DMA((7,)), pltpu.SemaphoreType.DMA((7,))],
        name=name,
    )(v)


def _adamw(w, g, m, v, *, name):
    R, C = w.shape
    tr = R
    if R * C * 4 > 1024 * 1024:
        tr = max(t for t in range(8, R, 8) if R % t == 0 and t * C * 4 <= 1024 * 1024)

    def body(w_ref, g_ref, m_ref, v_ref, d_ref, m2_ref, v2_ref):
        gg = g_ref[...]
        m2 = ADAM_B1 * m_ref[...] + (1.0 - ADAM_B1) * gg
        v2 = ADAM_B2 * v_ref[...] + (1.0 - ADAM_B2) * jnp.square(gg)
        m_hat = m2 / (1.0 - ADAM_B1 ** ADAM_STEP)
        v_hat = v2 / (1.0 - ADAM_B2 ** ADAM_STEP)
        d_ref[...] = -ADAM_LR * (m_hat / (jnp.sqrt(v_hat) + ADAM_EPS) + ADAM_WD * w_ref[...])
        m2_ref[...] = m2
        v2_ref[...] = v2

    blk = pl.BlockSpec((tr, C), lambda i: (i, 0))
    out = jax.ShapeDtypeStruct((R, C), F32)
    return pl.pallas_call(
        body, grid=(R // tr,), in_specs=[blk] * 4, out_specs=[blk] * 3, out_shape=[out] * 3,
        name=name, compiler_params=_params("parallel"),
    )(w, g, m, v)


WEIGHT_ORDER = ("a_norm", "a_w_in", "a_w_out", "b_norm", "b_w_in", "b_f", "b_w_out", "ffn_norm", "ffn_w_gu",
                "ffn_w_down", "final_norm")
MATRICES = (("a_w_in", 0, "col"), ("a_w_out", 0, "row"), ("b_w_in", 0, "stack"), ("b_w_out", 0, "row"),
            ("ffn_w_gu", 0, "col"), ("ffn_w_gu", 1, "col"), ("ffn_w_down", 0, "row"), ("ffn_w_down", 1, "row"))
MATRIX_GROUPS = ([0], [1], [2], [3], [4, 5], [6, 7])
GROUP_NAMES = ("a_w_in", "a_w_out", "b_w_in", "b_w_out", "ffn_w_gu", "ffn_w_down")
QKV_COLS = 3 * N_HEADS * HEAD_DIM


def kernel(x, a_norm, a_w_in, a_w_out, b_norm, b_w_in, b_f, b_w_out, ffn_norm, ffn_w_gu, ffn_w_down, final_norm, loss_target, m_a_norm, m_a_w_in, m_a_w_out, m_b_norm, m_b_w_in, m_b_f, m_b_w_out, m_ffn_norm, m_ffn_w_gu, m_ffn_w_down, m_final_norm, v_a_norm, v_a_w_in, v_a_w_out, v_b_norm, v_b_w_in, v_b_f, v_b_w_out, v_ffn_norm, v_ffn_w_gu, v_ffn_w_down, v_final_norm):
    given = dict(a_norm=a_norm, a_w_in=a_w_in, a_w_out=a_w_out, b_norm=b_norm, b_w_in=b_w_in, b_f=b_f, b_w_out=b_w_out,
                 ffn_norm=ffn_norm, ffn_w_gu=ffn_w_gu, ffn_w_down=ffn_w_down, final_norm=final_norm)
    mom_m = dict(a_norm=m_a_norm, a_w_in=m_a_w_in, a_w_out=m_a_w_out, b_norm=m_b_norm, b_w_in=m_b_w_in, b_f=m_b_f,
                 b_w_out=m_b_w_out, ffn_norm=m_ffn_norm, ffn_w_gu=m_ffn_w_gu, ffn_w_down=m_ffn_w_down, final_norm=m_final_norm)
    mom_v = dict(a_norm=v_a_norm, a_w_in=v_a_w_in, a_w_out=v_a_w_out, b_norm=v_b_norm, b_w_in=v_b_w_in, b_f=v_b_f,
                 b_w_out=v_b_w_out, ffn_norm=v_ffn_norm, ffn_w_gu=v_ffn_w_gu, ffn_w_down=v_ffn_w_down, final_norm=v_final_norm)
    chip = 2 * lax.axis_index("x") + lax.axis_index("y")
    core = lax.axis_index("c")
    bn_cols = b_norm.shape[1]

    placed = lax.dynamic_update_slice(jnp.zeros((SMALL_ROWS, D_MODEL), F32), b_norm, (0, chip * bn_cols))
    placed = placed * (core == 0).astype(F32)
    b_norm_full = _allreduce_small(placed, name="gather_b_norm")[0:1]

    place = jnp.stack([chip, core]).astype(jnp.int32)
    kinds = [k for _, _, k in MATRICES]
    dims = [given[n][l].shape for n, l, _ in MATRICES]
    placed = [_place_shard(given[n][l], k, place, name=f"place_{n}{l}") for n, l, k in MATRICES]
    whole = _gather_weights(placed, kinds, dims)
    b_in = whole[2].transpose(1, 0, 2).reshape(D_MODEL, -1)
    gate_cols = b_in.shape[1] - QKV_COLS
    w = dict(a_norm=a_norm, a_w_in=whole[0], a_w_out=whole[1], b_norm=b_norm_full,
             b_w_qkv=b_in[:, :QKV_COLS], b_w_f=jnp.pad(b_in[:, QKV_COLS:], ((0, 0), (0, GATE_LANES - gate_cols))),
             b_f=jnp.pad(b_f, ((0, 0), (0, GATE_LANES - gate_cols))), b_w_out=whole[3],
             ffn_norm=ffn_norm, ffn_w_gu=(whole[4], whole[5]), ffn_w_down=(whole[6], whole[7]),
             final_norm=final_norm.reshape(1, D_MODEL))

    def exchange(indices, parts):
        return _scatter_guest(parts, [kinds[i] for i in indices], [dims[i] for i in indices])

    loss, dx, g, partials, slots = _local_step(x[0], loss_target[0], w, exchange)
    bufs, members = [], []
    for group in MATRIX_GROUPS:
        buf = None
        for l, wi in enumerate(group):
            n = MATRICES[wi][0]
            buf = _sum_slots(slots[wi], partials[wi], kinds[wi], dims[wi], place, name=f"sum_{n}{l}", into=buf,
                             layer=l, n_layers=len(group))
            members.append((len(bufs), l if len(group) > 1 else None))
        bufs.append(buf)
    reduced = dict(zip(GROUP_NAMES, _pair_exchange(bufs, members)))

    small = jnp.concatenate([g["a_norm"], g["b_norm"], g["ffn_norm"], g["final_norm"],
                             jnp.pad(g["b_f"], ((0, 0), (0, D_MODEL - GATE_LANES))),
                             jnp.zeros((SMALL_ROWS - 6, D_MODEL), F32)], axis=0)
    small = _allreduce_small(small, name="allreduce_small")
    grads = dict(reduced)
    grads["a_norm"] = small[0:1]
    grads["b_norm"] = lax.dynamic_slice(small, (1, chip * bn_cols), (1, bn_cols))
    grads["ffn_norm"] = small[2:4]
    grads["final_norm"] = small[4]
    grads["b_f"] = small[5:6, :gate_cols]

    out_g, out_d, out_m, out_v = [], [], [], []
    for n in WEIGHT_ORDER:
        shape = given[n].shape
        two_d = (1, shape[0]) if len(shape) == 1 else (-1, shape[-1])
        d, m2, v2 = _adamw(given[n].reshape(two_d), grads[n].reshape(two_d), mom_m[n].reshape(two_d),
                           mom_v[n].reshape(two_d), name=f"adamw_{n}")
        out_g.append(grads[n].reshape(shape))
        out_d.append(d.reshape(shape))
        out_m.append(m2.reshape(shape))
        out_v.append(v2.reshape(shape))

    total = lax.psum(loss[0, 0], MESH_AXES)
    return (total, dx[None], *out_g, *out_d, *out_m, *out_v)
```

```python
import functools

import jax
import jax.numpy as jnp
from jax import lax
from jax.experimental import pallas as pl
from jax.experimental.pallas import tpu as pltpu

F32 = jnp.float32
BF = jnp.bfloat16

D_MODEL = 1024
N_HEADS = 16
HEAD_DIM = 64
D_FF = 2816
DILATED_PATTERNS = ((128, 1), (512, 4), (2048, 16))
ROT_DIM = 16
ROPE_THETA = 500000.0
RMS_EPS = 1e-6
NEG_INF = -1e30
ATTN_SCALE = HEAD_DIM ** -0.5
GATE_LANES = 128
N_CHIPS = 4
MESH_AXES = ("x", "y", "c")
MESH = pl.DeviceIdType.MESH

ADAM_LR = 0.001
ADAM_B1 = 0.9
ADAM_B2 = 0.999
ADAM_EPS = 1e-08
ADAM_WD = 0.01
ADAM_STEP = 10

VMEM_LIMIT_BYTES = 56 * 1024 * 1024


def _params(*sem):
    return pltpu.CompilerParams(dimension_semantics=sem, vmem_limit_bytes=VMEM_LIMIT_BYTES)


def _hosted_call(body, *, grid, in_specs, out_specs, out_shape, scratch_shapes, args, name, guest=None):
    params = _params(*(["arbitrary"] * len(grid)))
    if guest is None:
        return pl.pallas_call(body, grid=grid, in_specs=in_specs, out_specs=out_specs, out_shape=out_shape,
                              scratch_shapes=scratch_shapes, name=name, compiler_params=params)(*args)
    n_in, n_out, n_scr = len(in_specs), len(out_specs), len(scratch_shapes)
    g_in, g_out = len(guest["args"]), len(guest["out_shape"])
    any_spec = pl.BlockSpec(memory_space=pl.ANY)

    def wrapped(*refs):
        i1 = n_in + g_in
        o1 = i1 + n_out
        o2 = o1 + g_out
        s1 = o2 + n_scr
        guest_refs = (refs[n_in:i1], refs[o1:o2], refs[s1:])
        ids = [pl.program_id(d) for d in range(len(grid))]
        first = functools.reduce(jnp.logical_and, [i == 0 for i in ids])
        last = functools.reduce(jnp.logical_and, [i == g - 1 for i, g in zip(ids, grid)])

        @pl.when(first)
        def _():
            guest["start"](*guest_refs)

        body(*refs[:n_in], *refs[i1:o1], *refs[o2:s1])

        @pl.when(last)
        def _():
            guest["finish"](*guest_refs)

    aliases = {n_in + k: n_out + k for k in range(g_in)} if guest.get("in_place") else {}
    return pl.pallas_call(
        wrapped, grid=grid, in_specs=list(in_specs) + [any_spec] * g_in, out_specs=list(out_specs) + [any_spec] * g_out,
        out_shape=list(out_shape) + list(guest["out_shape"]), scratch_shapes=list(scratch_shapes) + list(guest["scratch"]),
        input_output_aliases=aliases, name=name, compiler_params=params)(*args, *guest["args"])


def _rope_rotate(t, cos, sin_a, sin_b):
    outs = []
    for cidx in range(t.shape[1] // 128):
        tc = t[:, cidx * 128:(cidx + 1) * 128]
        outs.append(tc * cos + pltpu.roll(tc, 120, 1) * sin_a + pltpu.roll(tc, 8, 1) * sin_b)
    return jnp.concatenate(outs, axis=1)


def _mm_nn(a, b, *, tm, tn, out_dtype, name, resid=None, rope=None, guest=None):
    M, K = a.shape
    N = b.shape[1]
    assert M % tm == 0 and N % tn == 0 and b.shape[0] == K
    n_in = 2 + (resid is not None) + (3 if rope is not None else 0)

    def body(*refs):
        a_ref, b_ref = refs[0], refs[1]
        o_ref = refs[n_in]
        acc = jnp.dot(a_ref[...], b_ref[...], preferred_element_type=F32)
        if resid is not None:
            acc = acc + refs[2][...]
        if rope is not None:
            cos_ref, sa_ref, sb_ref = refs[n_in - 3:n_in]
            j = pl.program_id(1)

            @pl.when(j % 3 != 2)
            def _():
                o_ref[...] = _rope_rotate(acc, cos_ref[...], sa_ref[...], sb_ref[...]).astype(out_dtype)

            @pl.when(j % 3 == 2)
            def _():
                o_ref[...] = acc.astype(out_dtype)
        else:
            o_ref[...] = acc.astype(out_dtype)

    in_specs = [pl.BlockSpec((tm, K), lambda i, j: (i, 0)), pl.BlockSpec((K, tn), lambda i, j: (0, j))]
    args = [a, b]
    if resid is not None:
        in_specs.append(pl.BlockSpec((tm, tn), lambda i, j: (i, j)))
        args.append(resid)
    if rope is not None:
        assert tn == 1024
        for t in rope:
            in_specs.append(pl.BlockSpec((tm, 128), lambda i, j: (i, 0)))
            args.append(t)
    outs = _hosted_call(
        body, grid=(M // tm, N // tn), in_specs=in_specs, out_specs=[pl.BlockSpec((tm, tn), lambda i, j: (i, j))],
        out_shape=[jax.ShapeDtypeStruct((M, N), out_dtype)], scratch_shapes=[], args=args, name=name, guest=guest)
    return outs[0] if guest is None else (outs[0], outs[1:])


def _mm_nt(a, b, *, tm, to, tn, out_dtype, name, add=None, guest=None):
    M, N = a.shape
    O = b.shape[0]
    assert M % tm == 0 and O % to == 0 and N % tn == 0 and b.shape[1] == N
    nk = N // tn

    def body(*refs):
        a_ref, b_ref = refs[0], refs[1]
        o_ref, acc_ref = refs[-2], refs[-1]
        k = pl.program_id(2)

        @pl.when(k == 0)
        def _():
            if add is not None:
                acc_ref[...] = refs[2][...]
            else:
                acc_ref[...] = jnp.zeros_like(acc_ref)

        acc_ref[...] += lax.dot_general(a_ref[...], b_ref[...], (((1,), (1,)), ((), ())),
                                        preferred_element_type=F32)

        @pl.when(k == nk - 1)
        def _():
            o_ref[...] = acc_ref[...].astype(out_dtype)

    in_specs = [pl.BlockSpec((tm, tn), lambda i, j, k: (i, k)), pl.BlockSpec((to, tn), lambda i, j, k: (j, k))]
    args = [a, b]
    if add is not None:
        in_specs.append(pl.BlockSpec((tm, to), lambda i, j, k: (i, j)))
        args.append(add)
    outs = _hosted_call(
        body, grid=(M // tm, O // to, nk), in_specs=in_specs,
        out_specs=[pl.BlockSpec((tm, to), lambda i, j, k: (i, j))],
        out_shape=[jax.ShapeDtypeStruct((M, O), out_dtype)],
        scratch_shapes=[pltpu.VMEM((tm, to), F32)], args=args, name=name, guest=guest)
    return outs[0] if guest is None else (outs[0], outs[1:])


def _mm_tn(a, b, *, tk, tn, tm, out_dtype, name):
    M, K = a.shape
    N = b.shape[1]
    assert M % tm == 0 and K % tk == 0 and N % tn == 0 and b.shape[0] == M
    nm = M // tm

    def body(a_ref, b_ref, o_ref, acc_ref):
        m = pl.program_id(2)

        @pl.when(m == 0)
        def _():
            acc_ref[...] = jnp.zeros_like(acc_ref)

        acc_ref[...] += lax.dot_general(a_ref[...], b_ref[...], (((0,), (0,)), ((), ())),
                                        preferred_element_type=F32)

        @pl.when(m == nm - 1)
        def _():
            o_ref[...] = acc_ref[...].astype(out_dtype)

    return pl.pallas_call(
        body, grid=(K // tk, N // tn, nm),
        in_specs=[pl.BlockSpec((tm, tk), lambda i, j, m: (m, i)), pl.BlockSpec((tm, tn), lambda i, j, m: (m, j))],
        out_specs=pl.BlockSpec((tk, tn), lambda i, j, m: (i, j)),
        out_shape=jax.ShapeDtypeStruct((K, N), out_dtype),
        scratch_shapes=[pltpu.VMEM((tk, tn), F32)], name=name,
        compiler_params=_params("parallel", "parallel", "arbitrary"),
    )(a, b)


ROW_TILE = 512


def _rms_fwd(x, g, *, name):
    S, Dm = x.shape

    def body(x_ref, g_ref, o_ref):
        xf = x_ref[...]
        r = lax.rsqrt(jnp.mean(xf * xf, axis=-1, keepdims=True) + RMS_EPS)
        o_ref[...] = (xf * r * g_ref[...]).astype(BF)

    return pl.pallas_call(
        body, grid=(S // ROW_TILE,),
        in_specs=[pl.BlockSpec((ROW_TILE, Dm), lambda i: (i, 0)), pl.BlockSpec((1, Dm), lambda i: (0, 0))],
        out_specs=pl.BlockSpec((ROW_TILE, Dm), lambda i: (i, 0)),
        out_shape=jax.ShapeDtypeStruct((S, Dm), BF), name=name, compiler_params=_params("parallel"),
    )(x, g)


def _rms_bwd(x, g, dn, dres, *, name):
    S, Dm = x.shape

    def body(x_ref, g_ref, dn_ref, dres_ref, dx_ref, dxb_ref, dg_ref):
        i = pl.program_id(0)
        xf = x_ref[...]
        r = lax.rsqrt(jnp.mean(xf * xf, axis=-1, keepdims=True) + RMS_EPS)
        xh = xf * r
        dnf = dn_ref[...]
        dyg = dnf * g_ref[...]
        dx = dres_ref[...] + r * (dyg - xh * jnp.mean(dyg * xh, axis=-1, keepdims=True))
        dx_ref[...] = dx
        dxb_ref[...] = dx.astype(BF)

        @pl.when(i == 0)
        def _():
            dg_ref[...] = jnp.zeros_like(dg_ref)

        dg_ref[...] += jnp.sum(dnf * xh, axis=0, keepdims=True)

    row = pl.BlockSpec((ROW_TILE, Dm), lambda i: (i, 0))
    vec = pl.BlockSpec((1, Dm), lambda i: (0, 0))
    return pl.pallas_call(
        body, grid=(S // ROW_TILE,), in_specs=[row, vec, row, row], out_specs=[row, row, vec],
        out_shape=[jax.ShapeDtypeStruct((S, Dm), F32), jax.ShapeDtypeStruct((S, Dm), BF),
                   jax.ShapeDtypeStruct((1, Dm), F32)],
        name=name, compiler_params=_params("arbitrary"),
    )(x, g, dn, dres)


def _loss_head(h, g, tgt, *, name):
    S, Dm = h.shape

    def body(h_ref, g_ref, t_ref, loss_ref, dh_ref, dhb_ref, dg_ref):
        i = pl.program_id(0)
        xf = h_ref[...]
        r = lax.rsqrt(jnp.mean(xf * xf, axis=-1, keepdims=True) + RMS_EPS)
        xh = xf * r
        gv = g_ref[...]
        err = xh * gv - t_ref[...]
        dy = err * (1.0 / Dm)
        dyg = dy * gv
        dh = r * (dyg - xh * jnp.mean(dyg * xh, axis=-1, keepdims=True))
        dh_ref[...] = dh
        dhb_ref[...] = dh.astype(BF)

        @pl.when(i == 0)
        def _():
            dg_ref[...] = jnp.zeros_like(dg_ref)
            loss_ref[...] = jnp.zeros_like(loss_ref)

        dg_ref[...] += jnp.sum(dy * xh, axis=0, keepdims=True)
        part = 0.5 * jnp.sum(jnp.mean(err * err, axis=-1, keepdims=True), axis=0, keepdims=True)
        loss_ref[...] += jnp.broadcast_to(part, loss_ref.shape)

    row = pl.BlockSpec((ROW_TILE, Dm), lambda i: (i, 0))
    vec = pl.BlockSpec((1, Dm), lambda i: (0, 0))
    return pl.pallas_call(
        body, grid=(S // ROW_TILE,), in_specs=[row, vec, row],
        out_specs=[pl.BlockSpec((1, 128), lambda i: (0, 0)), row, row, vec],
        out_shape=[jax.ShapeDtypeStruct((1, 128), F32), jax.ShapeDtypeStruct((S, Dm), F32),
                   jax.ShapeDtypeStruct((S, Dm), BF), jax.ShapeDtypeStruct((1, Dm), F32)],
        name=name, compiler_params=_params("arbitrary"),
    )(h, g, tgt)


SWIGLU_ROWS = 256


def _swiglu_fwd(gu, *, name):
    S = gu.shape[0]

    def body(g_ref, u_ref, o_ref):
        g = g_ref[...].astype(F32)
        sig = 1.0 / (1.0 + jnp.exp(-g))
        o_ref[...] = (g * sig * u_ref[...].astype(F32)).astype(BF)

    return pl.pallas_call(
        body, grid=(S // SWIGLU_ROWS,),
        in_specs=[pl.BlockSpec((SWIGLU_ROWS, D_FF), lambda i: (i, 0)), pl.BlockSpec((SWIGLU_ROWS, D_FF), lambda i: (i, 1))],
        out_specs=pl.BlockSpec((SWIGLU_ROWS, D_FF), lambda i: (i, 0)),
        out_shape=jax.ShapeDtypeStruct((S, D_FF), BF), name=name, compiler_params=_params("parallel"),
    )(gu, gu)


def _swiglu_bwd(gu, dact, *, name):
    S = gu.shape[0]

    def body(g_ref, u_ref, d_ref, o_ref):
        g = g_ref[...].astype(F32)
        u = u_ref[...].astype(F32)
        d = d_ref[...].astype(F32)
        sig = 1.0 / (1.0 + jnp.exp(-g))
        o_ref[:, :D_FF] = (d * u * sig * (1.0 + g * (1.0 - sig))).astype(BF)
        o_ref[:, D_FF:] = (d * g * sig).astype(BF)

    return pl.pallas_call(
        body, grid=(S // SWIGLU_ROWS,),
        in_specs=[pl.BlockSpec((SWIGLU_ROWS, D_FF), lambda i: (i, 0)), pl.BlockSpec((SWIGLU_ROWS, D_FF), lambda i: (i, 1)),
                  pl.BlockSpec((SWIGLU_ROWS, D_FF), lambda i: (i, 0))],
        out_specs=pl.BlockSpec((SWIGLU_ROWS, 2 * D_FF), lambda i: (i, 0)),
        out_shape=jax.ShapeDtypeStruct((S, 2 * D_FF), BF), name=name, compiler_params=_params("parallel"),
    )(gu, gu, dact)


def _attn_fwd(qa, ka, va, qcb, kcb, vcb, *, dil, T, nkv, window, name, c=None, cT=None, o_dtype=F32):
    L = qa.shape[0]
    nq = L // T
    fox = c is not None

    def kv_block(n, j):
        return n - (nkv - 1) + j

    def body(*refs):
        if fox:
            q_ref, k_ref, v_ref, c_ref, ct_ref, o_ref, lse_ref, m_sc, l_sc, acc_sc = refs
        else:
            q_ref, k_ref, v_ref, o_ref, lse_ref, m_sc, l_sc, acc_sc = refs
        n = pl.program_id(1)
        j = pl.program_id(2)
        kb = kv_block(n, j)

        @pl.when(j == 0)
        def _():
            m_sc[...] = jnp.full(m_sc.shape, NEG_INF, F32)
            l_sc[...] = jnp.zeros_like(l_sc)
            acc_sc[...] = jnp.zeros_like(acc_sc)

        @pl.when(kb >= 0)
        def _():
            diff = (n * T + lax.broadcasted_iota(jnp.int32, (T, T), 0)) - (kb * T + lax.broadcasted_iota(jnp.int32, (T, T), 1))
            valid = diff >= 0
            if window is not None:
                valid = jnp.logical_and(valid, diff <= window)
            for h in range(N_HEADS):
                hs = slice(h * HEAD_DIM, (h + 1) * HEAD_DIM)
                qh = q_ref[:, hs] * jnp.asarray(ATTN_SCALE, BF)
                s = lax.dot_general(qh, k_ref[:, hs], (((1,), (1,)), ((), ())), preferred_element_type=F32)
                if fox:
                    s = s + c_ref[:, h:h + 1] - ct_ref[h:h + 1, :]
                s = jnp.where(valid, s, NEG_INF)
                m_prev = m_sc[:, h:h + 1]
                m_new = jnp.maximum(m_prev, jnp.max(s, axis=1, keepdims=True))
                alpha = jnp.exp(m_prev - m_new)
                p = jnp.exp(s - m_new)
                l_sc[:, h:h + 1] = alpha * l_sc[:, h:h + 1] + jnp.sum(p, axis=1, keepdims=True)
                acc_sc[:, hs] = alpha * acc_sc[:, hs] + jnp.dot(p.astype(BF), v_ref[:, hs], preferred_element_type=F32)
                m_sc[:, h:h + 1] = m_new

        @pl.when(j == nkv - 1)
        def _():
            lane = lax.broadcasted_iota(jnp.int32, (T, 128), 1)
            lse = jnp.zeros((T, 128), F32)
            for h in range(N_HEADS):
                hs = slice(h * HEAD_DIM, (h + 1) * HEAD_DIM)
                l = l_sc[:, h:h + 1]
                o_ref[:, hs] = (acc_sc[:, hs] / l).astype(o_dtype)
                lse = jnp.where(lane == h, m_sc[:, h:h + 1] + jnp.log(l), lse)
            lse_ref[...] = lse

    def kvi(n, j):
        return jnp.maximum(kv_block(n, j), 0)

    in_specs = [pl.BlockSpec((T, 1024), lambda r, n, j: (n, qcb(r))),
                pl.BlockSpec((T, 1024), lambda r, n, j: (kvi(n, j), kcb(r))),
                pl.BlockSpec((T, 1024), lambda r, n, j: (kvi(n, j), vcb(r)))]
    args = [qa, ka, va]
    if fox:
        in_specs += [pl.BlockSpec((T, GATE_LANES), lambda r, n, j: (n, 0)),
                     pl.BlockSpec((GATE_LANES, T), lambda r, n, j: (0, kvi(n, j)))]
        args += [c, cT]
    return pl.pallas_call(
        body, grid=(dil, nq, nkv), in_specs=in_specs,
        out_specs=[pl.BlockSpec((T, 1024), lambda r, n, j: (n, r)), pl.BlockSpec((T, 128), lambda r, n, j: (n, r))],
        out_shape=[jax.ShapeDtypeStruct((L, dil * 1024), o_dtype), jax.ShapeDtypeStruct((L, dil * 128), F32)],
        scratch_shapes=[pltpu.VMEM((T, 128), F32), pltpu.VMEM((T, 128), F32), pltpu.VMEM((T, 1024), F32)],
        name=name, compiler_params=_params("parallel", "parallel", "arbitrary"),
    )(*args)


def _attn_bwd(qa, ka, va, qcb, kcb, vcb, doa, oa, lsea, *, dil, T, nqs, window, name, c=None, cT=None):
    L = qa.shape[0]
    nq = L // T
    fox = c is not None

    def body(*refs):
        if fox:
            (q_ref, k_ref, v_ref, do_ref, o_ref, lse_ref, c_ref, ct_ref,
             dq_ref, dk_ref, dv_ref, dct_ref, dcq_ref, dq_sc, dk_sc, dv_sc, dc_sc, dcq_sc) = refs
        else:
            (q_ref, k_ref, v_ref, do_ref, o_ref, lse_ref,
             dq_ref, dk_ref, dv_ref, dq_sc, dk_sc, dv_sc) = refs
        kb = pl.program_id(1)
        jq = pl.program_id(2)
        qb = kb + jq

        @pl.when(jnp.logical_and(kb == 0, jq == 0))
        def _():
            dq_sc[...] = jnp.zeros_like(dq_sc)
            if fox:
                dcq_sc[...] = jnp.zeros_like(dcq_sc)

        @pl.when(jq == 0)
        def _():
            dk_sc[...] = jnp.zeros_like(dk_sc)
            dv_sc[...] = jnp.zeros_like(dv_sc)
            if fox:
                dc_sc[...] = jnp.zeros_like(dc_sc)

        @pl.when(qb < nq)
        def _():
            diff = (qb * T + lax.broadcasted_iota(jnp.int32, (T, T), 0)) - (kb * T + lax.broadcasted_iota(jnp.int32, (T, T), 1))
            valid = diff >= 0
            if window is not None:
                valid = jnp.logical_and(valid, diff <= window)
            qrows = pl.ds(pl.multiple_of(qb * T, T), T)
            for h in range(N_HEADS):
                hs = slice(h * HEAD_DIM, (h + 1) * HEAD_DIM)
                qh = q_ref[:, hs] * jnp.asarray(ATTN_SCALE, BF)
                kh = k_ref[:, hs]
                doh = do_ref[:, hs]
                s = lax.dot_general(qh, kh, (((1,), (1,)), ((), ())), preferred_element_type=F32)
                if fox:
                    s = s + c_ref[:, h:h + 1] - ct_ref[h:h + 1, :]
                s = jnp.where(valid, s, NEG_INF)
                p = jnp.exp(s - lse_ref[:, h:h + 1])
                dp = lax.dot_general(doh, v_ref[:, hs], (((1,), (1,)), ((), ())), preferred_element_type=F32)
                delta = jnp.sum(doh.astype(F32) * o_ref[:, hs].astype(F32), axis=1, keepdims=True)
                ds = p * (dp - delta)
                dsb = ds.astype(BF)
                dv_sc[:, hs] += lax.dot_general(p.astype(BF), doh, (((0,), (0,)), ((), ())), preferred_element_type=F32)
                dk_sc[:, hs] += lax.dot_general(dsb, qh, (((0,), (0,)), ((), ())), preferred_element_type=F32)
                dq_sc[qrows, hs] += jnp.dot(dsb, kh, preferred_element_type=F32) * ATTN_SCALE
                if fox:
                    dc_sc[h:h + 1, :] -= jnp.sum(ds, axis=0, keepdims=True)
                    dcq_sc[qrows, h:h + 1] += jnp.sum(ds, axis=1, keepdims=True)

        @pl.when(jq == nqs - 1)
        def _():
            dk_ref[...] = dk_sc[...].astype(BF)
            dv_ref[...] = dv_sc[...].astype(BF)
            if fox:
                dct_ref[...] = dc_sc[...]

        @pl.when(jnp.logical_and(kb == nq - 1, jq == nqs - 1))
        def _():
            def put(i, carry):
                rows = pl.ds(pl.multiple_of(i * T, T), T)
                dq_ref[rows, :] = dq_sc[rows, :].astype(BF)
                return carry
            lax.fori_loop(0, nq, put, 0)
            if fox:
                dcq_ref[...] = dcq_sc[...]

    def qi(kb, jq):
        return jnp.minimum(kb + jq, nq - 1)

    in_specs = [pl.BlockSpec((T, 1024), lambda r, kb, jq: (qi(kb, jq), qcb(r))),
                pl.BlockSpec((T, 1024), lambda r, kb, jq: (kb, kcb(r))),
                pl.BlockSpec((T, 1024), lambda r, kb, jq: (kb, vcb(r))),
                pl.BlockSpec((T, 1024), lambda r, kb, jq: (qi(kb, jq), r)),
                pl.BlockSpec((T, 1024), lambda r, kb, jq: (qi(kb, jq), r)),
                pl.BlockSpec((T, 128), lambda r, kb, jq: (qi(kb, jq), r))]
    args = [qa, ka, va, doa, oa, lsea]
    out_specs = [pl.BlockSpec((L, 1024), lambda r, kb, jq: (0, r)),
                 pl.BlockSpec((T, 1024), lambda r, kb, jq: (kb, r)),
                 pl.BlockSpec((T, 1024), lambda r, kb, jq: (kb, r))]
    out_shape = [jax.ShapeDtypeStruct((L, dil * 1024), BF)] * 3
    scratch = [pltpu.VMEM((L, 1024), F32), pltpu.VMEM((T, 1024), F32), pltpu.VMEM((T, 1024), F32)]
    if fox:
        in_specs += [pl.BlockSpec((T, GATE_LANES), lambda r, kb, jq: (qi(kb, jq), 0)),
                     pl.BlockSpec((GATE_LANES, T), lambda r, kb, jq: (0, kb))]
        args += [c, cT]
        out_specs.append(pl.BlockSpec((GATE_LANES, T), lambda r, kb, jq: (0, kb)))
        out_shape.append(jax.ShapeDtypeStruct((GATE_LANES, L), F32))
        scratch.append(pltpu.VMEM((GATE_LANES, T), F32))
        out_specs.append(pl.BlockSpec((L, GATE_LANES), lambda r, kb, jq: (0, 0)))
        out_shape.append(jax.ShapeDtypeStruct((L, GATE_LANES), F32))
        scratch.append(pltpu.VMEM((L, GATE_LANES), F32))
    return pl.pallas_call(
        body, grid=(dil, nq, nqs), in_specs=in_specs, out_specs=out_specs, out_shape=out_shape,
        scratch_shapes=scratch, name=name, compiler_params=_params("arbitrary", "arbitrary", "arbitrary"),
    )(*args)


def _band_masks(T, n):
    row = lax.broadcasted_iota(jnp.int32, (T, T), 0)
    col = lax.broadcasted_iota(jnp.int32, (T, T), 1)
    return jnp.logical_and(col >= row, n > 0), col <= row


def _band_fwd(qa, ka, va, qcb, kcb, vcb, *, dil, T, window, name, guest=None):
    L = qa.shape[0]
    nq = L // T
    assert window == T
    nt = (((1,), (1,)), ((), ()))

    def body(q_ref, kp_ref, kc_ref, vp_ref, vc_ref, o_ref, lse_ref):
        valid_prev, valid_cur = _band_masks(T, pl.program_id(1))
        lane = lax.broadcasted_iota(jnp.int32, (T, 128), 1)
        low = lane < HEAD_DIM
        ones = jnp.ones((T, 128), BF)
        lse = jnp.zeros((T, 128), F32)
        def scores(h):
            ps = slice((h // 2) * 128, (h // 2 + 1) * 128)
            qp = q_ref[:, ps] * jnp.asarray(ATTN_SCALE, BF)
            qm = jnp.where(low if h % 2 == 0 else jnp.logical_not(low), qp, jnp.zeros_like(qp))
            s0 = jnp.where(valid_prev, lax.dot_general(qm, kp_ref[:, ps], nt, preferred_element_type=F32), NEG_INF)
            s1 = jnp.where(valid_cur, lax.dot_general(qm, kc_ref[:, ps], nt, preferred_element_type=F32), NEG_INF)
            return s0, s1

        def softmax(s0, s1):
            m = jnp.maximum(jnp.max(s0, axis=1, keepdims=True), jnp.max(s1, axis=1, keepdims=True))
            return m, jnp.exp(s0 - m).astype(BF), jnp.exp(s1 - m).astype(BF)

        def weighted(h, p0, p1):
            ps = slice((h // 2) * 128, (h // 2 + 1) * 128)
            l = jnp.dot(p0, ones, preferred_element_type=F32) + jnp.dot(p1, ones, preferred_element_type=F32)
            acc = jnp.dot(p0, vp_ref[:, ps], preferred_element_type=F32) + jnp.dot(p1, vc_ref[:, ps], preferred_element_type=F32)
            return l, acc

        sc, pr, even = {}, {}, None
        for t in range(N_HEADS + 2):
            if t < N_HEADS:
                sc[t] = scores(t)
            done = None
            if t >= 2:
                m, p0, p1 = pr.pop(t - 2)
                done = (m,) + weighted(t - 2, p0, p1)
            if 1 <= t <= N_HEADS:
                pr[t - 1] = softmax(*sc.pop(t - 1))
            if done is not None:
                h = t - 2
                m, l, acc = done
                lse = jnp.where(lane == h, m + jnp.log(l), lse)
                if h % 2 == 0:
                    even = acc / l
                else:
                    o_ref[:, (h // 2) * 128:(h // 2 + 1) * 128] = jnp.where(low, even, acc / l)
        lse_ref[...] = lse

    def prev(n):
        return jnp.maximum(n - 1, 0)

    blk = lambda f, cb: pl.BlockSpec((T, 1024), lambda r, n: (f(n), cb(r)))
    same = lambda n: n
    outs = _hosted_call(
        body, grid=(dil, nq),
        in_specs=[blk(same, qcb), blk(prev, kcb), blk(same, kcb), blk(prev, vcb), blk(same, vcb)],
        out_specs=[pl.BlockSpec((T, 1024), lambda r, n: (n, r)), pl.BlockSpec((T, 128), lambda r, n: (n, r))],
        out_shape=[jax.ShapeDtypeStruct((L, dil * 1024), F32), jax.ShapeDtypeStruct((L, dil * 128), F32)],
        scratch_shapes=[], args=(qa, ka, ka, va, va), name=name, guest=guest)
    return outs if guest is None else (outs[:2], outs[2:])


def _band_bwd(qa, ka, va, qcb, kcb, vcb, doa, oa, lsea, *, dil, T, window, name, guest=None):
    L = qa.shape[0]
    nq = L // T
    assert window == T
    nt = (((1,), (1,)), ((), ()))
    tn = (((0,), (0,)), ((), ()))

    def body(q_ref, kp_ref, kc_ref, vp_ref, vc_ref, do_ref, o_ref, lse_ref, dq_ref, dk_ref, dv_ref, ck_sc, cv_sc):
        n = pl.program_id(1)

        @pl.when(n == 0)
        def _():
            ck_sc[...] = jnp.zeros_like(ck_sc)
            cv_sc[...] = jnp.zeros_like(cv_sc)

        @pl.when(n < nq)
        def _():
            valid_prev, valid_cur = _band_masks(T, n)
            low = lax.broadcasted_iota(jnp.int32, (T, 128), 1) < HEAD_DIM
            dot = functools.partial(lax.dot_general, preferred_element_type=F32)

            def pair(h):
                return slice((h // 2) * 128, (h // 2 + 1) * 128)

            def products(h):
                ps = pair(h)
                mask = low if h % 2 == 0 else jnp.logical_not(low)
                qp = q_ref[:, ps] * jnp.asarray(ATTN_SCALE, BF)
                dop = do_ref[:, ps]
                qm = jnp.where(mask, qp, jnp.zeros_like(qp))
                dom = jnp.where(mask, dop, jnp.zeros_like(dop))
                s0 = jnp.where(valid_prev, dot(qm, kp_ref[:, ps], nt), NEG_INF)
                s1 = jnp.where(valid_cur, dot(qm, kc_ref[:, ps], nt), NEG_INF)
                return qm, dom, s0, s1, dot(dom, vp_ref[:, ps], nt), dot(dom, vc_ref[:, ps], nt)

            def pointwise(h, qm, dom, s0, s1, dp0, dp1):
                ps = pair(h)
                mask = low if h % 2 == 0 else jnp.logical_not(low)
                prod = do_ref[:, ps].astype(F32) * o_ref[:, ps].astype(F32)
                delta = jnp.sum(jnp.where(mask, prod, 0.0), axis=1, keepdims=True)
                lse = lse_ref[:, h:h + 1]
                p0 = jnp.exp(s0 - lse)
                p1 = jnp.exp(s1 - lse)
                ds0 = (p0 * (dp0 - delta)).astype(BF)
                ds1 = (p1 * (dp1 - delta)).astype(BF)
                return qm, dom, p0.astype(BF), p1.astype(BF), ds0, ds1

            def gradients(h, qm, dom, p0, p1, ds0, ds1):
                ps = pair(h)
                dq = dot(ds0, kp_ref[:, ps], (((1,), (0,)), ((), ()))) + dot(ds1, kc_ref[:, ps], (((1,), (0,)), ((), ())))
                return dq, dot(ds0, qm, tn), dot(p0, dom, tn), dot(ds1, qm, tn), dot(p1, dom, tn)

            st1, st2, even = {}, {}, None
            for t in range(N_HEADS + 2):
                if t < N_HEADS:
                    st1[t] = products(t)
                done = gradients(t - 2, *st2.pop(t - 2)) if t >= 2 else None
                if 1 <= t <= N_HEADS:
                    st2[t - 1] = pointwise(t - 1, *st1.pop(t - 1))
                if done is not None:
                    h = t - 2
                    if h % 2 == 0:
                        even = done
                    else:
                        ps = pair(h)
                        dq_ref[:, ps] = (jnp.where(low, even[0], done[0]) * ATTN_SCALE).astype(BF)
                        dk_ref[:, ps] = (ck_sc[:, ps] + even[1] + done[1]).astype(BF)
                        dv_ref[:, ps] = (cv_sc[:, ps] + even[2] + done[2]).astype(BF)
                        ck_sc[:, ps] = even[3] + done[3]
                        cv_sc[:, ps] = even[4] + done[4]

        @pl.when(n == nq)
        def _():
            dk_ref[...] = ck_sc[...].astype(BF)
            dv_ref[...] = cv_sc[...].astype(BF)

    def cur(n):
        return jnp.minimum(n, nq - 1)

    def prev(n):
        return jnp.maximum(cur(n) - 1, 0)

    blk = lambda f, cb: pl.BlockSpec((T, 1024), lambda r, n: (f(n), cb(r)))
    own = lambda r: r
    outs = _hosted_call(
        body, grid=(dil, nq + 1),
        in_specs=[blk(cur, qcb), blk(prev, kcb), blk(cur, kcb), blk(prev, vcb), blk(cur, vcb), blk(cur, own), blk(cur, own),
                  pl.BlockSpec((T, 128), lambda r, n: (cur(n), r))],
        out_specs=[blk(cur, own), blk(lambda n: jnp.maximum(n - 1, 0), own), blk(lambda n: jnp.maximum(n - 1, 0), own)],
        out_shape=[jax.ShapeDtypeStruct((L, dil * 1024), BF)] * 3,
        scratch_shapes=[pltpu.VMEM((T, 1024), F32), pltpu.VMEM((T, 1024), F32)],
        args=(qa, ka, ka, va, va, doa, oa, lsea), name=name, guest=guest)
    return outs if guest is None else (outs[:3], outs[3:])


FOX_T = 256
FOX_ROWS = 128


def _fox_fwd(qkv, cT, *, name, guest=None):
    S = qkv.shape[0]
    T, R = FOX_T, FOX_ROWS
    nq = S // T
    nt = (((1,), (1,)), ((), ()))
    chains = [(h, rh) for h in range(N_HEADS) for rh in range(T // R)]

    def body(q_ref, k_ref, v_ref, ct_ref, o_ref, lse_ref, m_sc, l_sc, acc_sc):
        n = pl.program_id(0)
        j = pl.program_id(1)
        lane = lax.broadcasted_iota(jnp.int32, (R, 128), 1)
        low = lane < HEAD_DIM
        ones = jnp.ones((T, 128), BF)

        @pl.when(j == 0)
        def _():
            m_sc[...] = jnp.full(m_sc.shape, NEG_INF, F32)
            l_sc[...] = jnp.zeros_like(l_sc)
            acc_sc[...] = jnp.zeros_like(acc_sc)

        def step(diagonal):
            def pair(h):
                return slice((h // 2) * 128, (h // 2 + 1) * 128)

            def rows(rh):
                return slice(rh * R, (rh + 1) * R)

            def scores(h, rh):
                qp = q_ref[rows(rh), pair(h)] * jnp.asarray(ATTN_SCALE, BF)
                qm = jnp.where(low if h % 2 == 0 else jnp.logical_not(low), qp, jnp.zeros_like(qp))
                s = lax.dot_general(qm, k_ref[:, pair(h)], nt, preferred_element_type=F32) - ct_ref[h:h + 1, :]
                if diagonal:
                    keep = (lax.broadcasted_iota(jnp.int32, (R, T), 1)
                            <= rh * R + lax.broadcasted_iota(jnp.int32, (R, T), 0))
                    s = jnp.where(keep, s, NEG_INF)
                return s

            def softmax(h, rh, s):
                m_prev = m_sc[h, rows(rh), :]
                m_new = jnp.maximum(m_prev, jnp.max(s, axis=1, keepdims=True))
                p = jnp.exp(s - jnp.concatenate([m_new] * (T // 128), axis=1)).astype(BF)
                return m_new, jnp.exp(m_prev - m_new), p

            def weighted(h, p):
                vx = jnp.concatenate([v_ref[:, pair(h)], ones], axis=1)
                return jnp.dot(p, vx, preferred_element_type=F32)

            sc, pr, even = {}, {}, {}
            nch = len(chains)
            for t in range(nch + 2):
                if t < nch:
                    sc[t] = scores(*chains[t])
                done = None
                if t >= 2:
                    m_new, alpha, p = pr.pop(t - 2)
                    done = (m_new, alpha, weighted(chains[t - 2][0], p))
                if 1 <= t <= nch:
                    pr[t - 1] = softmax(*chains[t - 1], sc.pop(t - 1))
                if done is not None:
                    h, rh = chains[t - 2]
                    m_new, alpha, pv = done
                    m_sc[h, rows(rh), :] = m_new
                    l_sc[h, rows(rh), :] = alpha * l_sc[h, rows(rh), :] + pv[:, 128:]
                    if h % 2 == 0:
                        even[rh] = (alpha, pv[:, :128])
                    else:
                        a0, pv0 = even.pop(rh)
                        acc = acc_sc[h // 2, rows(rh), :]
                        acc_sc[h // 2, rows(rh), :] = jnp.where(low, a0 * acc + pv0, alpha * acc + pv[:, :128])

        @pl.when(j < n)
        def _():
            step(False)

        @pl.when(j == n)
        def _():
            step(True)
            lane_t = lax.broadcasted_iota(jnp.int32, (T, 128), 1)
            low_t = lane_t < HEAD_DIM
            lse = jnp.zeros((T, 128), F32)
            for h in range(N_HEADS):
                lse = jnp.where(lane_t == h, m_sc[h] + jnp.log(l_sc[h]), lse)
            lse_ref[...] = lse
            for hp in range(N_HEADS // 2):
                inv = jnp.where(low_t, 1.0 / l_sc[2 * hp], 1.0 / l_sc[2 * hp + 1])
                o_ref[:, hp * 128:(hp + 1) * 128] = (acc_sc[hp] * inv).astype(BF)

    def kv(n, j):
        return jnp.minimum(j, n)

    outs = _hosted_call(
        body, grid=(nq, nq),
        in_specs=[pl.BlockSpec((T, 1024), lambda n, j: (n, 0)), pl.BlockSpec((T, 1024), lambda n, j: (kv(n, j), 1)),
                  pl.BlockSpec((T, 1024), lambda n, j: (kv(n, j), 2)), pl.BlockSpec((GATE_LANES, T), lambda n, j: (0, kv(n, j)))],
        out_specs=[pl.BlockSpec((T, 1024), lambda n, j: (n, 0)), pl.BlockSpec((T, 128), lambda n, j: (n, 0))],
        out_shape=[jax.ShapeDtypeStruct((S, 1024), BF), jax.ShapeDtypeStruct((S, 128), F32)],
        scratch_shapes=[pltpu.VMEM((N_HEADS, T, 128), F32), pltpu.VMEM((N_HEADS, T, 128), F32),
                        pltpu.VMEM((N_HEADS // 2, T, 128), F32)],
        args=(qkv, qkv, qkv, cT), name=name, guest=guest)
    return outs if guest is None else (outs[:2], outs[2:])


def _fox_bwd(qkv, cT, do, o, lse, *, name, guest=None):
    S = qkv.shape[0]
    T, R = FOX_T, FOX_ROWS
    nq = S // T
    nt = (((1,), (1,)), ((), ()))
    tn = (((0,), (0,)), ((), ()))
    nn = (((1,), (0,)), ((), ()))
    chains = [(h, rh) for h in range(N_HEADS) for rh in range(T // R)]
    dot = functools.partial(lax.dot_general, preferred_element_type=F32)

    def body(q_ref, k_ref, v_ref, ct_ref, do_ref, o_ref, lse_ref, dq_ref, dk_ref, dv_ref, dct_ref, dcq_ref,
             dq_sc, dk_sc, dv_sc, dc_sc, dcq_sc):
        kb = pl.program_id(0)
        jq = pl.program_id(1)
        qb = kb + jq
        lane = lax.broadcasted_iota(jnp.int32, (R, 128), 1)
        low = lane < HEAD_DIM
        ones_k = jnp.ones((T, 128), BF)
        ones_r = jnp.ones((8, R), BF)

        @pl.when(jnp.logical_and(kb == 0, jq == 0))
        def _():
            dq_sc[...] = jnp.zeros_like(dq_sc)
            dcq_sc[...] = jnp.zeros_like(dcq_sc)

        @pl.when(jq == 0)
        def _():
            dk_sc[...] = jnp.zeros_like(dk_sc)
            dv_sc[...] = jnp.zeros_like(dv_sc)
            dc_sc[...] = jnp.zeros_like(dc_sc)

        def step(diagonal):
            def pair(h):
                return slice((h // 2) * 128, (h // 2 + 1) * 128)

            def rows(rh):
                return slice(rh * R, (rh + 1) * R)

            def qrows(rh):
                return pl.ds(pl.multiple_of(qb * T + rh * R, R), R)

            def products(h, rh):
                mask = low if h % 2 == 0 else jnp.logical_not(low)
                qp = q_ref[rows(rh), pair(h)] * jnp.asarray(ATTN_SCALE, BF)
                dop = do_ref[rows(rh), pair(h)]
                qm = jnp.where(mask, qp, jnp.zeros_like(qp))
                dom = jnp.where(mask, dop, jnp.zeros_like(dop))
                s = dot(qm, k_ref[:, pair(h)], nt) - ct_ref[h:h + 1, :]
                if diagonal:
                    keep = (lax.broadcasted_iota(jnp.int32, (R, T), 1)
                            <= rh * R + lax.broadcasted_iota(jnp.int32, (R, T), 0))
                    s = jnp.where(keep, s, NEG_INF)
                return qm, dom, s, dot(dom, v_ref[:, pair(h)], nt)

            def pointwise(h, rh, qm, dom, s, dp):
                mask = low if h % 2 == 0 else jnp.logical_not(low)
                prod = do_ref[rows(rh), pair(h)].astype(F32) * o_ref[rows(rh), pair(h)].astype(F32)
                delta = jnp.sum(jnp.where(mask, prod, 0.0), axis=1, keepdims=True)
                p = jnp.exp(s - lse_ref[rows(rh), h:h + 1])
                ds = (p * (dp - delta)).astype(BF)
                return qm, dom, p.astype(BF), ds

            def gradients(h, qm, dom, p, ds):
                return (dot(ds, k_ref[:, pair(h)], nn), dot(ds, qm, tn), dot(p, dom, tn),
                        dot(ds, ones_k, nn), dot(ones_r, ds, nn))

            st1, st2, even = {}, {}, {}
            dcq_tiles = [jnp.zeros((R, 128), F32) for _ in range(T // R)]
            nch = len(chains)
            for t in range(nch + 2):
                if t < nch:
                    st1[t] = products(*chains[t])
                done = gradients(chains[t - 2][0], *st2.pop(t - 2)) if t >= 2 else None
                if 1 <= t <= nch:
                    st2[t - 1] = pointwise(*chains[t - 1], *st1.pop(t - 1))
                if done is not None:
                    h, rh = chains[t - 2]
                    dq, dk, dv, rsum, csum = done
                    dcq_tiles[rh] = jnp.where(lane == h, rsum, dcq_tiles[rh])
                    dc_sc[h:h + 1, :] -= csum[0:1, :]
                    if h % 2 == 0:
                        even[rh] = (dq, dk, dv)
                    else:
                        dq0, dk0, dv0 = even.pop(rh)
                        ps = pair(h)
                        dq_sc[qrows(rh), ps] += jnp.where(low, dq0, dq) * ATTN_SCALE
                        dk_sc[:, ps] += dk0 + dk
                        dv_sc[:, ps] += dv0 + dv
            for rh in range(T // R):
                dcq_sc[qrows(rh), :] += dcq_tiles[rh]

        @pl.when(jnp.logical_and(jq > 0, qb < nq))
        def _():
            step(False)

        @pl.when(jq == 0)
        def _():
            step(True)

        @pl.when(jq == nq - 1)
        def _():
            dk_ref[...] = dk_sc[...].astype(BF)
            dv_ref[...] = dv_sc[...].astype(BF)
            dct_ref[...] = dc_sc[...]

        @pl.when(jnp.logical_and(kb == nq - 1, jq == nq - 1))
        def _():
            def put(i, carry):
                r = pl.ds(pl.multiple_of(i * T, T), T)
                dq_ref[r, :] = dq_sc[r, :].astype(BF)
                return carry
            lax.fori_loop(0, nq, put, 0)
            dcq_ref[...] = dcq_sc[...]

    def qi(kb, jq):
        return jnp.minimum(kb + jq, nq - 1)

    qblk = lambda col: pl.BlockSpec((T, 1024), lambda kb, jq: (qi(kb, jq), col))
    kblk = lambda col: pl.BlockSpec((T, 1024), lambda kb, jq: (kb, col))
    whole = pl.BlockSpec((S, 1024), lambda kb, jq: (0, 0))
    outs = _hosted_call(
        body, grid=(nq, nq),
        in_specs=[qblk(0), kblk(1), kblk(2), pl.BlockSpec((GATE_LANES, T), lambda kb, jq: (0, kb)), qblk(0), qblk(0),
                  pl.BlockSpec((T, 128), lambda kb, jq: (qi(kb, jq), 0))],
        out_specs=[whole, kblk(0), kblk(0), pl.BlockSpec((GATE_LANES, T), lambda kb, jq: (0, kb)),
                   pl.BlockSpec((S, GATE_LANES), lambda kb, jq: (0, 0))],
        out_shape=[jax.ShapeDtypeStruct((S, 1024), BF)] * 3 + [jax.ShapeDtypeStruct((GATE_LANES, S), F32),
                                                               jax.ShapeDtypeStruct((S, GATE_LANES), F32)],
        scratch_shapes=[pltpu.VMEM((S, 1024), F32), pltpu.VMEM((T, 1024), F32), pltpu.VMEM((T, 1024), F32),
                        pltpu.VMEM((GATE_LANES, T), F32), pltpu.VMEM((S, GATE_LANES), F32)],
        args=(qkv, qkv, qkv, cT, do, o, lse), name=name, guest=guest)
    return outs if guest is None else (outs[:5], outs[5:])


def _combine_groups(os, lses, *, name):
    S = os[0].shape[0]
    ng = len(os)

    def body(*refs):
        o_refs, l_refs = refs[:ng], refs[ng:2 * ng]
        out_ref, lse_ref = refs[2 * ng], refs[2 * ng + 1]
        ls = [r[...] for r in l_refs]
        m = functools.reduce(jnp.maximum, ls)
        es = [jnp.exp(l - m) for l in ls]
        den = functools.reduce(jnp.add, es)
        ws = [e / den for e in es]
        lse_ref[...] = m + jnp.log(den)
        for h in range(N_HEADS):
            hs = slice(h * HEAD_DIM, (h + 1) * HEAD_DIM)
            acc = ws[0][:, h:h + 1] * o_refs[0][:, hs]
            for g in range(1, ng):
                acc = acc + ws[g][:, h:h + 1] * o_refs[g][:, hs]
            out_ref[:, hs] = acc.astype(BF)

    row = pl.BlockSpec((ROW_TILE, 1024), lambda i: (i, 0))
    stat = pl.BlockSpec((ROW_TILE, 128), lambda i: (i, 0))
    return pl.pallas_call(
        body, grid=(S // ROW_TILE,), in_specs=[row] * ng + [stat] * ng, out_specs=[row, stat],
        out_shape=[jax.ShapeDtypeStruct((S, 1024), BF), jax.ShapeDtypeStruct((S, 128), F32)],
        name=name, compiler_params=_params("parallel"),
    )(*os, *lses)


def _assemble(parts, rope_flags, rope, *, name):
    S = parts[0].shape[0]
    n = len(parts)
    use_rope = any(rope_flags)

    def body(*refs):
        out_ref = refs[-1]
        for b in range(n):
            cols = slice(b * 1024, (b + 1) * 1024)
            if rope_flags[b]:
                cos_ref, sa_ref, sb_ref = refs[n:n + 3]
                out_ref[:, cols] = _rope_rotate(refs[b][...].astype(F32), cos_ref[...], sa_ref[...], sb_ref[...]).astype(BF)
            else:
                out_ref[:, cols] = refs[b][...]

    tm = 256
    row = pl.BlockSpec((tm, 1024), lambda i: (i, 0))
    in_specs = [row] * n
    args = list(parts)
    if use_rope:
        in_specs += [pl.BlockSpec((tm, 128), lambda i: (i, 0))] * 3
        args += list(rope)
    return pl.pallas_call(
        body, grid=(S // tm,), in_specs=in_specs, out_specs=pl.BlockSpec((tm, n * 1024), lambda i: (i, 0)),
        out_shape=jax.ShapeDtypeStruct((S, n * 1024), BF), name=name, compiler_params=_params("parallel"),
    )(*args)


GATE_ROWS = 512


def _gate_fwd(z, bf, *, name):
    S = z.shape[0]

    def body(z_ref, b_ref, c_ref, ct_ref, carry):
        i = pl.program_id(0)

        @pl.when(i == 0)
        def _():
            carry[...] = jnp.zeros_like(carry)

        zz = z_ref[...] + b_ref[...]
        logf = jnp.minimum(zz, 0.0) - jnp.log(1.0 + jnp.exp(-jnp.abs(zz)))
        tri = (lax.broadcasted_iota(jnp.int32, (GATE_ROWS, GATE_ROWS), 0)
               >= lax.broadcasted_iota(jnp.int32, (GATE_ROWS, GATE_ROWS), 1)).astype(F32)
        cs = jnp.dot(tri, logf, precision=lax.Precision.HIGHEST, preferred_element_type=F32) + carry[...]
        c_ref[...] = cs
        ct_ref[...] = cs.T
        carry[...] = cs[GATE_ROWS - 1:GATE_ROWS, :]

    return pl.pallas_call(
        body, grid=(S // GATE_ROWS,),
        in_specs=[pl.BlockSpec((GATE_ROWS, GATE_LANES), lambda i: (i, 0)), pl.BlockSpec((1, GATE_LANES), lambda i: (0, 0))],
        out_specs=[pl.BlockSpec((GATE_ROWS, GATE_LANES), lambda i: (i, 0)), pl.BlockSpec((GATE_LANES, GATE_ROWS), lambda i: (0, i))],
        out_shape=[jax.ShapeDtypeStruct((S, GATE_LANES), F32), jax.ShapeDtypeStruct((GATE_LANES, S), F32)],
        scratch_shapes=[pltpu.VMEM((1, GATE_LANES), F32)], name=name, compiler_params=_params("arbitrary"),
    )(z, bf)


def _gate_bwd(z, bf, dcT, dcq, *, name):
    S = z.shape[0]
    nb = S // GATE_ROWS

    def body(z_ref, b_ref, dct_ref, dcq_ref, dz_ref, db_ref, carry):
        i = pl.program_id(0)

        @pl.when(i == 0)
        def _():
            carry[...] = jnp.zeros_like(carry)
            db_ref[...] = jnp.zeros_like(db_ref)

        dc = dct_ref[...].T + dcq_ref[...]
        tri = (lax.broadcasted_iota(jnp.int32, (GATE_ROWS, GATE_ROWS), 0)
               <= lax.broadcasted_iota(jnp.int32, (GATE_ROWS, GATE_ROWS), 1)).astype(F32)
        dl = jnp.dot(tri, dc, precision=lax.Precision.HIGHEST, preferred_element_type=F32) + carry[...]
        carry[...] = dl[0:1, :]
        zz = z_ref[...] + b_ref[...]
        dz = dl * (1.0 / (1.0 + jnp.exp(zz)))
        lane = lax.broadcasted_iota(jnp.int32, dz.shape, 1)
        dz = jnp.where(lane < N_HEADS, dz, 0.0)
        dz_ref[...] = dz.astype(BF)
        db_ref[...] += jnp.sum(dz, axis=0, keepdims=True)

    return pl.pallas_call(
        body, grid=(nb,),
        in_specs=[pl.BlockSpec((GATE_ROWS, GATE_LANES), lambda i: (nb - 1 - i, 0)), pl.BlockSpec((1, GATE_LANES), lambda i: (0, 0)),
                  pl.BlockSpec((GATE_LANES, GATE_ROWS), lambda i: (0, nb - 1 - i)),
                  pl.BlockSpec((GATE_ROWS, GATE_LANES), lambda i: (nb - 1 - i, 0))],
        out_specs=[pl.BlockSpec((GATE_ROWS, GATE_LANES), lambda i: (nb - 1 - i, 0)), pl.BlockSpec((1, GATE_LANES), lambda i: (0, 0))],
        out_shape=[jax.ShapeDtypeStruct((S, GATE_LANES), BF), jax.ShapeDtypeStruct((1, GATE_LANES), F32)],
        scratch_shapes=[pltpu.VMEM((1, GATE_LANES), F32)], name=name, compiler_params=_params("arbitrary"),
    )(z, bf, dcT, dcq)


def _rope_tables(S):
    half = ROT_DIM // 2
    inv_freq = ROPE_THETA ** (-jnp.arange(half, dtype=F32) * 2.0 / ROT_DIM)
    ang = jnp.arange(S, dtype=F32)[:, None] * inv_freq[None, :]
    cos, sin = jnp.cos(ang), jnp.sin(ang)
    zero = jnp.zeros((S, HEAD_DIM - ROT_DIM), F32)
    zh = jnp.zeros((S, half), F32)
    cos_h = jnp.concatenate([cos, cos, jnp.ones_like(zero)], axis=1)
    sa_h = jnp.concatenate([-sin, zh, zero], axis=1)
    sb_h = jnp.concatenate([zh, sin, zero], axis=1)
    two = lambda t: jnp.concatenate([t, t], axis=1)
    return two(cos_h), two(sa_h), two(sb_h)


def _ffn_fwd(h, norm, w_gu, w_down, tag):
    n = _rms_fwd(h, norm, name=f"ffn{tag}_norm")
    gu = _mm_nn(n, w_gu, tm=1024, tn=512, out_dtype=BF, name=f"ffn{tag}_gu")
    act = _swiglu_fwd(gu, name=f"ffn{tag}_act")
    out = _mm_nn(act, w_down, tm=512, tn=1024, out_dtype=F32, name=f"ffn{tag}_down", resid=h)
    return out, (h, n, gu, act)


def _ffn_bwd(dh, dhb, saved, norm, w_gu, w_down, tag, ride=None):
    h, n, gu, act = saved
    dact = _mm_nt(dhb, w_down, tm=512, to=1408, tn=1024, out_dtype=BF, name=f"ffn{tag}_dact")
    dw_down = _mm_tn(act, dhb, tk=1408, tn=1024, tm=512, out_dtype=BF, name=f"ffn{tag}_dwdown")
    dgu = _swiglu_bwd(gu, dact, name=f"ffn{tag}_dgu")
    dn_call = lambda guest: _mm_nt(dgu, w_gu, tm=512, to=1024, tn=1408, out_dtype=F32, name=f"ffn{tag}_dn", guest=guest)
    dn = dn_call(None) if ride is None else ride(dn_call)
    dw_gu = _mm_tn(n, dgu, tk=1024, tn=1408, tm=512, out_dtype=BF, name=f"ffn{tag}_dwgu")
    dx, dxb, dg = _rms_bwd(h, norm, dn, dh, name=f"ffn{tag}_dnorm")
    return dx, dxb, dg, dw_gu, dw_down


def _group_view(proj, gi, dil):
    S = proj.shape[0]
    if dil == 1:
        return proj, (lambda t: (lambda r: gi * 3 + t))
    cols = proj[:, gi * 3 * 1024:(gi + 1) * 3 * 1024]
    return cols.reshape(S // dil, dil * 3 * 1024), (lambda t: (lambda r: r * 3 + t))


def _local_step(x, tgt, w, mats, fetch, exchange):
    S = x.shape[0]
    rope_f = _rope_tables(S)
    rope_b = (rope_f[0], -rope_f[1], -rope_f[2])
    g, partial, landed = {}, {}, {}
    w = dict(w, ffn_w_gu={}, ffn_w_down={})

    def bring(call, indices):
        bufs = [mats[wi] for wi in indices]
        if fetch is None:
            return call(None), bufs
        return call(fetch(indices, bufs))

    def ride(call, indices):
        guest = exchange(indices, [partial[wi] for wi in indices]) if indices else None
        res = call(guest)
        if guest is None:
            return res
        res, outs = res
        landed.update(zip(indices, outs))
        return res

    n0 = _rms_fwd(x, w["a_norm"], name="a_norm")
    proj, (w["ffn_w_gu"][0], w["ffn_w_down"][0]) = bring(
        lambda guest: _mm_nn(n0, w["a_w_in"], tm=512, tn=1024, out_dtype=BF, name="a_proj", rope=rope_f, guest=guest), [4, 6])
    o_parts, lse_parts = [], []
    for gi, (window, dil) in enumerate(DILATED_PATTERNS):
        pv, cb = _group_view(proj, gi, dil)
        attend = lambda guest: _band_fwd(pv, pv, pv, cb(0), cb(1), cb(2), dil=dil, T=128, window=window // dil,
                                         name=f"a_attn{gi}", guest=guest)
        if gi == 1:
            (o_g, lse_g), (b_in, w["b_w_out"]) = bring(attend, [2, 3])
        else:
            o_g, lse_g = attend(None)
        o_parts.append(o_g.reshape(S, 1024))
        lse_parts.append(lse_g.reshape(S, 128))
    b_in = b_in.transpose(1, 0, 2).reshape(D_MODEL, -1)
    w["b_w_qkv"] = b_in[:, :QKV_COLS]
    w["b_w_f"] = jnp.pad(b_in[:, QKV_COLS:], ((0, 0), (0, GATE_LANES + QKV_COLS - b_in.shape[1])))
    o_a, lse_a = _combine_groups(o_parts, lse_parts, name="a_combine")
    h1 = _mm_nn(o_a, w["a_w_out"], tm=512, tn=1024, out_dtype=F32, name="a_out", resid=x)
    h2, ffn0 = _ffn_fwd(h1, w["ffn_norm"][0:1], w["ffn_w_gu"][0], w["ffn_w_down"][0], 0)

    n2 = _rms_fwd(h2, w["b_norm"], name="b_norm")
    qkv = _mm_nn(n2, w["b_w_qkv"], tm=512, tn=1024, out_dtype=BF, name="b_proj")
    zf = _mm_nn(n2, w["b_w_f"], tm=512, tn=GATE_LANES, out_dtype=F32, name="b_gate_proj")
    _, cT = _gate_fwd(zf, w["b_f"], name="b_gate")
    (o_b, lse_b), (w["ffn_w_gu"][1], w["ffn_w_down"][1]) = bring(lambda guest: _fox_fwd(qkv, cT, name="b_attn", guest=guest), [5, 7])
    h3 = _mm_nn(o_b, w["b_w_out"], tm=512, tn=1024, out_dtype=F32, name="b_out", resid=h2)
    h4, ffn1 = _ffn_fwd(h3, w["ffn_norm"][1:2], w["ffn_w_gu"][1], w["ffn_w_down"][1], 1)

    loss, dh4, dh4b, g["final_norm"] = _loss_head(h4, w["final_norm"], tgt, name="loss_head")

    dh3, dh3b, dg_f1, partial[5], partial[7] = _ffn_bwd(dh4, dh4b, ffn1, w["ffn_norm"][1:2], w["ffn_w_gu"][1], w["ffn_w_down"][1], 1)

    do_b = _mm_nt(dh3b, w["b_w_out"], tm=512, to=1024, tn=1024, out_dtype=BF, name="b_do")
    partial[3] = _mm_tn(o_b, dh3b, tk=1024, tn=1024, tm=512, out_dtype=BF, name="b_dwout")
    dq, dk, dv, dcT, dcq = ride(lambda guest: _fox_bwd(qkv, cT, do_b, o_b, lse_b, name="b_attn_bwd", guest=guest), [5, 7])
    dz, g["b_f"] = _gate_bwd(zf, w["b_f"], dcT, dcq, name="b_gate_bwd")
    dqkv = _assemble([dq, dk, dv], [False] * 3, None, name="b_dproj")
    dn2 = _mm_nt(dz, w["b_w_f"], tm=512, to=1024, tn=GATE_LANES, out_dtype=F32, name="b_dn_gate")
    dn2 = _mm_nt(dqkv, w["b_w_qkv"], tm=512, to=1024, tn=1024, out_dtype=F32, name="b_dn", add=dn2)
    g_qkv = _mm_tn(n2, dqkv, tk=1024, tn=1024, tm=512, out_dtype=BF, name="b_dwqkv")
    g_f = _mm_tn(n2, dz, tk=1024, tn=GATE_LANES, tm=512, out_dtype=BF, name="b_dwf")
    g_b_in = jnp.concatenate([g_qkv, g_f[:, :N_HEADS]], axis=1)
    partial[2] = g_b_in.reshape(D_MODEL, N_CHIPS, -1).transpose(1, 0, 2)
    dh2, dh2b, g["b_norm"] = _rms_bwd(h2, w["b_norm"], dn2, dh3, name="b_dnorm")

    dh1, dh1b, dg_f0, partial[4], partial[6] = _ffn_bwd(dh2, dh2b, ffn0, w["ffn_norm"][0:1], w["ffn_w_gu"][0], w["ffn_w_down"][0], 0,
                                                      ride=lambda call: ride(call, [2, 3]))
    g["ffn_norm"] = jnp.concatenate([dg_f0, dg_f1], axis=0)

    do_a = _mm_nt(dh1b, w["a_w_out"], tm=512, to=1024, tn=1024, out_dtype=BF, name="a_do")
    partial[1] = _mm_tn(o_a, dh1b, tk=1024, tn=1024, tm=512, out_dtype=BF, name="a_dwout")
    riders = {0: [4], 1: [6, 1], 2: []}
    parts = []
    for gi, (window, dil) in enumerate(DILATED_PATTERNS):
        L = S // dil
        pv, cb = _group_view(proj, gi, dil)
        res = ride(lambda guest: _band_bwd(pv, pv, pv, cb(0), cb(1), cb(2), do_a.reshape(L, dil * 1024),
                                           o_a.reshape(L, dil * 1024), lse_a.reshape(L, dil * 128), dil=dil, T=128,
                                           window=window // dil, name=f"a_attn_bwd{gi}", guest=guest), riders[gi])
        parts += [t.reshape(S, 1024) for t in res]
    dproj = _assemble(parts, [True, True, False] * 3, rope_b, name="a_dproj")
    partial[0] = _mm_tn(n0, dproj, tk=1024, tn=1024, tm=512, out_dtype=BF, name="a_dwin")
    dn0 = ride(lambda guest: _mm_nt(dproj, w["a_w_in"], tm=512, to=1024, tn=1024, out_dtype=F32, name="a_dn", guest=guest), [0])
    dx, _, g["a_norm"] = _rms_bwd(x, w["a_norm"], dn0, dh1, name="a_dnorm")
    return loss, dx, g, partial, landed


ANY = pl.BlockSpec(memory_space=pl.ANY)


def _place():
    x, y, c = lax.axis_index("x"), lax.axis_index("y"), lax.axis_index("c")
    chips = [(1 - x, y), (x, 1 - y), (1 - x, 1 - y)]
    return x, y, c, chips


def _shard_slice(ref, kind, rows, cols, s, half):
    hr = rows // 2
    if kind == "col":
        return ref.at[pl.ds(half * hr, hr), pl.ds(pl.multiple_of(s * cols, 128), cols)]
    if kind == "row":
        return ref.at[pl.ds(pl.multiple_of(s * rows + half * hr, 16), hr), :]
    return ref.at[s, pl.ds(half * hr, hr), :]


def _whole_shape(kind, rows, cols):
    return {"col": (rows, N_CHIPS * cols), "row": (N_CHIPS * rows, cols), "stack": (N_CHIPS, rows, cols)}[kind]


def _own_block(kind, rows, tr, cols):
    per = rows // tr

    def spec(half_rows):
        off = (lambda p: 0) if half_rows is None else (lambda p: p[1] * (half_rows // tr))
        if kind == "col":
            return pl.BlockSpec((tr, cols), lambda i, p: (off(p) + i, p[0]))
        if kind == "row":
            return pl.BlockSpec((tr, cols), lambda i, p: (p[0] * per + off(p) + i, 0))
        return pl.BlockSpec((None, tr, cols), lambda i, p: (p[0], off(p) + i, 0))
    return spec


def _place_shard(shard, kind, place, *, name):
    rows, cols = shard.shape
    tr = 256 if rows % 256 == 0 else rows // 2

    def body(p_ref, s_ref, o_ref):
        o_ref[...] = s_ref[...].astype(BF)

    return pl.pallas_call(
        body,
        grid_spec=pltpu.PrefetchScalarGridSpec(
            num_scalar_prefetch=1, grid=(rows // tr,),
            in_specs=[pl.BlockSpec((tr, cols), lambda i, p: (i, 0))],
            out_specs=_own_block(kind, rows, tr, cols)(None)),
        out_shape=jax.ShapeDtypeStruct(_whole_shape(kind, rows, cols), BF),
        name=name, compiler_params=_params("arbitrary"),
    )(place, shard)


def _gather_weights(placed, kinds, dims):
    nw = len(placed)

    def body(*refs):
        dst = refs[nw:2 * nw]
        send_sems, recv_sems = refs[2 * nw:]
        x, y, c, chips = _place()
        me = 2 * x + y
        sibling = (x, y, 1 - c)

        def copy(wi, k, s, half, to):
            p = _shard_slice(dst[wi], kinds[wi], dims[wi][0], dims[wi][1], s, half)
            return pltpu.make_async_remote_copy(src_ref=p, dst_ref=p, send_sem=send_sems.at[wi * 6 + k],
                                                recv_sem=recv_sems.at[wi * 6 + k], device_id=to, device_id_type=MESH)

        first, passed = [], []
        for wi in range(nw):
            for j, chip in enumerate(chips):
                cp = copy(wi, j, me, c, (*chip, c))
                cp.start()
                first.append(cp)
        for wi in range(nw):
            for j, chip in enumerate(chips):
                s = 2 * chip[0] + chip[1]
                copy(wi, j, s, c, (x, y, c)).wait_recv()
                cp = copy(wi, 3 + j, s, c, sibling)
                cp.start()
                passed.append(cp)
        for wi in range(nw):
            for j, chip in enumerate(chips):
                s = 2 * chip[0] + chip[1]
                copy(wi, 3 + j, s, 1 - c, (x, y, c)).wait_recv()
        for cp in first + passed:
            cp.wait_send()

    return pl.pallas_call(
        body, in_specs=[ANY] * nw, out_specs=[ANY] * nw,
        out_shape=[jax.ShapeDtypeStruct(p.shape, p.dtype) for p in placed],
        input_output_aliases={wi: wi for wi in range(nw)},
        scratch_shapes=[pltpu.SemaphoreType.DMA((nw * 6,)), pltpu.SemaphoreType.DMA((nw * 6,))],
        name="gather_weights",
    )(*placed)


def _fetch_guest(placed, kinds, dims):
    nw = len(placed)

    def copies(dst, send_sems, recv_sems, incoming):
        x, y, c, chips = _place()
        out = []
        for wi in range(nw):
            for j, chip in enumerate(chips):
                s = 2 * chip[0] + chip[1] if incoming else 2 * x + y
                to = (x, y, c) if incoming else (*chip, c)
                for half in range(2):
                    p = _shard_slice(dst[wi], kinds[wi], dims[wi][0], dims[wi][1], s, half)
                    k = wi * 6 + 2 * j + half
                    out.append(pltpu.make_async_remote_copy(src_ref=p, dst_ref=p, send_sem=send_sems.at[k],
                                                            recv_sem=recv_sems.at[k], device_id=to, device_id_type=MESH))
        return out

    def start(src, dst, sems):
        for cp in copies(dst, sems[0], sems[1], False):
            cp.start()

    def finish(src, dst, sems):
        for cp in copies(dst, sems[0], sems[1], True):
            cp.wait_recv()
        for cp in copies(dst, sems[0], sems[1], False):
            cp.wait_send()

    return dict(args=list(placed), out_shape=[jax.ShapeDtypeStruct(p.shape, p.dtype) for p in placed],
                scratch=[pltpu.SemaphoreType.DMA((nw * 6,)), pltpu.SemaphoreType.DMA((nw * 6,))],
                start=start, finish=finish, in_place=True)


def _scatter_guest(partials, kinds, dims):
    nw = len(partials)

    def copies(src, send_sems, recv_sems, dst):
        x, y, c, chips = _place()
        me = 2 * x + y
        out = []
        for wi in range(nw):
            rows, cols = dims[wi]

            def part(s, half, wi=wi, rows=rows, cols=cols):
                return _shard_slice(src[wi], kinds[wi], rows, cols, s, half)

            for j, chip in enumerate(chips):
                s = 2 * chip[0] + chip[1]
                for half in range(2):
                    slot = 2 * j + (c if half == 0 else 1 - c)
                    out.append(pltpu.make_async_remote_copy(
                        src_ref=part(s, half), dst_ref=dst[wi].at[slot],
                        send_sem=send_sems.at[wi * 7 + 2 * j + half], recv_sem=recv_sems.at[wi * 7 + slot],
                        device_id=(*chip, half), device_id_type=MESH))
            out.append(pltpu.make_async_remote_copy(
                src_ref=part(me, 1 - c), dst_ref=dst[wi].at[6],
                send_sem=send_sems.at[wi * 7 + 6], recv_sem=recv_sems.at[wi * 7 + 6],
                device_id=(x, y, 1 - c), device_id_type=MESH))
        return out

    def start(src, dst, sems):
        for cp in copies(src, sems[0], sems[1], dst):
            cp.start()

    def finish(src, dst, sems):
        x, y, c, _ = _place()
        for wi in range(nw):
            for slot in range(7):
                pltpu.make_async_remote_copy(
                    src_ref=dst[wi].at[slot], dst_ref=dst[wi].at[slot],
                    send_sem=sems[0].at[wi * 7 + slot], recv_sem=sems[1].at[wi * 7 + slot],
                    device_id=(x, y, c), device_id_type=MESH).wait_recv()
        for cp in copies(src, sems[0], sems[1], dst):
            cp.wait_send()

    return dict(args=list(partials), out_shape=[jax.ShapeDtypeStruct((7, d[0] // 2, d[1]), BF) for d in dims],
                scratch=[pltpu.SemaphoreType.DMA((nw * 7,)), pltpu.SemaphoreType.DMA((nw * 7,))],
                start=start, finish=finish)


def _scatter_grads(partials, kinds, dims, *, name):
    guest = _scatter_guest(partials, kinds, dims)
    nw = len(partials)

    def body(*refs):
        parts = (refs[:nw], refs[nw:2 * nw], refs[2 * nw:])
        guest["start"](*parts)
        guest["finish"](*parts)

    return pl.pallas_call(body, in_specs=[ANY] * nw, out_specs=[ANY] * nw, out_shape=guest["out_shape"],
                          scratch_shapes=guest["scratch"], name=name)(*partials)


def _sum_slots(slots, partial, kind, dims, place, *, name, into=None, layer=None, n_layers=1):
    rows, cols = dims
    hr = rows // 2
    tr = hr if 8 * hr * cols * 2 <= 6 * 1024 * 1024 else 128
    assert hr % tr == 0

    def body(p_ref, b_ref, own_ref, *rest):
        o_ref = rest[-1]
        acc = own_ref[...].astype(F32)
        for k in range(7):
            acc = acc + b_ref[k].astype(F32)
        o_ref[...] = acc

    half = lambda p: p[1] * (hr // tr)
    if n_layers == 1:
        out_spec = pl.BlockSpec((tr, cols), lambda i, p: (half(p) + i, 0))
        out_shape = jax.ShapeDtypeStruct((rows, cols), F32)
    else:
        out_spec = pl.BlockSpec((None, tr, cols), lambda i, p: (layer, half(p) + i, 0))
        out_shape = jax.ShapeDtypeStruct((n_layers, rows, cols), F32)
    in_specs = [pl.BlockSpec((7, tr, cols), lambda i, p: (0, i, 0)), _own_block(kind, rows, tr, cols)(hr)]
    args = [place, slots, partial]
    aliases = {}
    if into is not None:
        in_specs.append(ANY)
        args.append(into)
        aliases = {3: 0}
    return pl.pallas_call(
        body,
        grid_spec=pltpu.PrefetchScalarGridSpec(num_scalar_prefetch=1, grid=(hr // tr,), in_specs=in_specs, out_specs=out_spec),
        out_shape=out_shape, input_output_aliases=aliases, name=name, compiler_params=_params("arbitrary"),
    )(*args)


def _pair_exchange(bufs, members):
    nw = len(members)

    def body(*refs):
        dst = refs[len(bufs):2 * len(bufs)]
        send_sems, recv_sems = refs[2 * len(bufs):]
        x, y, c, _ = _place()

        def rows_of(wi, half):
            bi, l = members[wi]
            ref = dst[bi] if l is None else dst[bi].at[l]
            hr = ref.shape[0] // 2
            return ref.at[pl.ds(pl.multiple_of(half * hr, 8), hr), :]

        def copy(wi, half, to):
            p = rows_of(wi, half)
            return pltpu.make_async_remote_copy(src_ref=p, dst_ref=p, send_sem=send_sems.at[wi], recv_sem=recv_sems.at[wi],
                                                device_id=to, device_id_type=MESH)

        sent = []
        for wi in range(nw):
            cp = copy(wi, c, (x, y, 1 - c))
            cp.start()
            sent.append(cp)
        for wi in range(nw):
            copy(wi, 1 - c, (x, y, c)).wait_recv()
        for cp in sent:
            cp.wait_send()

    return pl.pallas_call(
        body, in_specs=[ANY] * len(bufs), out_specs=[ANY] * len(bufs),
        out_shape=[jax.ShapeDtypeStruct(b.shape, b.dtype) for b in bufs],
        input_output_aliases={i: i for i in range(len(bufs))},
        scratch_shapes=[pltpu.SemaphoreType.DMA((nw,)), pltpu.SemaphoreType.DMA((nw,))],
        name="pair_exchange",
    )(*bufs)


SMALL_ROWS = 8


def _allreduce_small(v, *, name):
    assert v.shape == (SMALL_ROWS, D_MODEL)

    def body(v_ref, o_ref, buf, send_sems, recv_sems):
        x, y, c, _ = _place()
        me = 4 * x + 2 * y + c
        buf[me] = v_ref[...]
        sent = []
        for k in range(1, 8):
            bx, by, bc = (k >> 2) & 1, (k >> 1) & 1, k & 1
            peer = (1 - x if bx else x, 1 - y if by else y, 1 - c if bc else c)
            cp = pltpu.make_async_remote_copy(src_ref=v_ref, dst_ref=buf.at[me], send_sem=send_sems.at[k - 1],
                                              recv_sem=recv_sems.at[k - 1], device_id=peer, device_id_type=MESH)
            cp.start()
            sent.append(cp)
        for k in range(1, 8):
            bx, by, bc = (k >> 2) & 1, (k >> 1) & 1, k & 1
            peer = 4 * (1 - x if bx else x) + 2 * (1 - y if by else y) + (1 - c if bc else c)
            pltpu.make_async_remote_copy(src_ref=v_ref, dst_ref=buf.at[peer], send_sem=send_sems.at[k - 1],
                                         recv_sem=recv_sems.at[k - 1], device_id=(x, y, c), device_id_type=MESH).wait_recv()
        for cp in sent:
            cp.wait_send()
        acc = buf[0]
        for d in range(1, 8):
            acc = acc + buf[d]
        o_ref[...] = acc

    vmem = pl.BlockSpec(memory_space=pltpu.VMEM)
    return pl.pallas_call(
        body, in_specs=[vmem], out_specs=vmem, out_shape=jax.ShapeDtypeStruct(v.shape, F32),
        scratch_shapes=[pltpu.VMEM((8,) + v.shape, F32), pltpu.SemaphoreType.DMA((7,)), pltpu.SemaphoreType.DMA((7,))],
        name=name,
    )(v)


def _adamw(w, g, m, v, *, name):
    R, C = w.shape
    tr = R
    if R * C * 4 > 1024 * 1024:
        tr = max(t for t in range(8, R, 8) if R % t == 0 and t * C * 4 <= 1024 * 1024)

    def body(w_ref, g_ref, m_ref, v_ref, d_ref, m2_ref, v2_ref):
        gg = g_ref[...]
        m2 = ADAM_B1 * m_ref[...] + (1.0 - ADAM_B1) * gg
        v2 = ADAM_B2 * v_ref[...] + (1.0 - ADAM_B2) * jnp.square(gg)
        m_hat = m2 / (1.0 - ADAM_B1 ** ADAM_STEP)
        v_hat = v2 / (1.0 - ADAM_B2 ** ADAM_STEP)
        d_ref[...] = -ADAM_LR * (m_hat / (jnp.sqrt(v_hat) + ADAM_EPS) + ADAM_WD * w_ref[...])
        m2_ref[...] = m2
        v2_ref[...] = v2

    blk = pl.BlockSpec((tr, C), lambda i: (i, 0))
    out = jax.ShapeDtypeStruct((R, C), F32)
    return pl.pallas_call(
        body, grid=(R // tr,), in_specs=[blk] * 4, out_specs=[blk] * 3, out_shape=[out] * 3,
        name=name, compiler_params=_params("parallel"),
    )(w, g, m, v)


WEIGHT_ORDER = ("a_norm", "a_w_in", "a_w_out", "b_norm", "b_w_in", "b_f", "b_w_out", "ffn_norm", "ffn_w_gu",
                "ffn_w_down", "final_norm")
MATRICES = (("a_w_in", 0, "col"), ("a_w_out", 0, "row"), ("b_w_in", 0, "stack"), ("b_w_out", 0, "row"),
            ("ffn_w_gu", 0, "col"), ("ffn_w_gu", 1, "col"), ("ffn_w_down", 0, "row"), ("ffn_w_down", 1, "row"))
MATRIX_GROUPS = ([0], [1], [2], [3], [4, 5], [6, 7])
GROUP_NAMES = ("a_w_in", "a_w_out", "b_w_in", "b_w_out", "ffn_w_gu", "ffn_w_down")
QKV_COLS = 3 * N_HEADS * HEAD_DIM


def kernel(x, a_norm, a_w_in, a_w_out, b_norm, b_w_in, b_f, b_w_out, ffn_norm, ffn_w_gu, ffn_w_down, final_norm, loss_target, m_a_norm, m_a_w_in, m_a_w_out, m_b_norm, m_b_w_in, m_b_f, m_b_w_out, m_ffn_norm, m_ffn_w_gu, m_ffn_w_down, m_final_norm, v_a_norm, v_a_w_in, v_a_w_out, v_b_norm, v_b_w_in, v_b_f, v_b_w_out, v_ffn_norm, v_ffn_w_gu, v_ffn_w_down, v_final_norm):
    given = dict(a_norm=a_norm, a_w_in=a_w_in, a_w_out=a_w_out, b_norm=b_norm, b_w_in=b_w_in, b_f=b_f, b_w_out=b_w_out,
                 ffn_norm=ffn_norm, ffn_w_gu=ffn_w_gu, ffn_w_down=ffn_w_down, final_norm=final_norm)
    mom_m = dict(a_norm=m_a_norm, a_w_in=m_a_w_in, a_w_out=m_a_w_out, b_norm=m_b_norm, b_w_in=m_b_w_in, b_f=m_b_f,
                 b_w_out=m_b_w_out, ffn_norm=m_ffn_norm, ffn_w_gu=m_ffn_w_gu, ffn_w_down=m_ffn_w_down, final_norm=m_final_norm)
    mom_v = dict(a_norm=v_a_norm, a_w_in=v_a_w_in, a_w_out=v_a_w_out, b_norm=v_b_norm, b_w_in=v_b_w_in, b_f=v_b_f,
                 b_w_out=v_b_w_out, ffn_norm=v_ffn_norm, ffn_w_gu=v_ffn_w_gu, ffn_w_down=v_ffn_w_down, final_norm=v_final_norm)
    chip = 2 * lax.axis_index("x") + lax.axis_index("y")
    core = lax.axis_index("c")
    bn_cols = b_norm.shape[1]

    placed = lax.dynamic_update_slice(jnp.zeros((SMALL_ROWS, D_MODEL), F32), b_norm, (0, chip * bn_cols))
    placed = placed * (core == 0).astype(F32)
    b_norm_full = _allreduce_small(placed, name="gather_b_norm")[0:1]

    place = jnp.stack([chip, core]).astype(jnp.int32)
    kinds = [k for _, _, k in MATRICES]
    dims = [given[n][l].shape for n, l, _ in MATRICES]
    placed = [_place_shard(given[n][l], k, place, name=f"place_{n}{l}") for n, l, k in MATRICES]
    first = _gather_weights(placed[:2], kinds[:2], dims[:2])
    mats = dict(enumerate(list(first) + placed[2:]))
    gate_cols = b_f.shape[1]
    w = dict(a_norm=a_norm, a_w_in=mats[0], a_w_out=mats[1], b_norm=b_norm_full,
             b_f=jnp.pad(b_f, ((0, 0), (0, GATE_LANES - gate_cols))), ffn_norm=ffn_norm,
             final_norm=final_norm.reshape(1, D_MODEL))

    def fetch(indices, bufs):
        return _fetch_guest(bufs, [kinds[i] for i in indices], [dims[i] for i in indices])

    def exchange(indices, parts):
        return _scatter_guest(parts, [kinds[i] for i in indices], [dims[i] for i in indices])

    loss, dx, g, partials, slots = _local_step(x[0], loss_target[0], w, mats, fetch, exchange)
    bufs, members = [], []
    for group in MATRIX_GROUPS:
        buf = None
        for l, wi in enumerate(group):
            n = MATRICES[wi][0]
            buf = _sum_slots(slots[wi], partials[wi], kinds[wi], dims[wi], place, name=f"sum_{n}{l}", into=buf,
                             layer=l, n_layers=len(group))
            members.append((len(bufs), l if len(group) > 1 else None))
        bufs.append(buf)
    reduced = dict(zip(GROUP_NAMES, _pair_exchange(bufs, members)))

    small = jnp.concatenate([g["a_norm"], g["b_norm"], g["ffn_norm"], g["final_norm"],
                             jnp.pad(g["b_f"], ((0, 0), (0, D_MODEL - GATE_LANES))),
                             jnp.zeros((SMALL_ROWS - 6, D_MODEL), F32)], axis=0)
    small = _allreduce_small(small, name="allreduce_small")
    grads = dict(reduced)
    grads["a_norm"] = small[0:1]
    grads["b_norm"] = lax.dynamic_slice(small, (1, chip * bn_cols), (1, bn_cols))
    grads["ffn_norm"] = small[2:4]
    grads["final_norm"] = small[4]
    grads["b_f"] = small[5:6, :gate_cols]

    out_g, out_d, out_m, out_v = [], [], [], []
    for n in WEIGHT_ORDER:
        shape = given[n].shape
        two_d = (1, shape[0]) if len(shape) == 1 else (-1, shape[-1])
        d, m2, v2 = _adamw(given[n].reshape(two_d), grads[n].reshape(two_d), mom_m[n].reshape(two_d),
                           mom_v[n].reshape(two_d), name=f"adamw_{n}")
        out_g.append(grads[n].reshape(shape))
        out_d.append(d.reshape(shape))
        out_m.append(m2.reshape(shape))
        out_v.append(v2.reshape(shape))

    total = lax.psum(loss[0, 0], MESH_AXES)
    return (total, dx[None], *out_g, *out_d, *out_m, *out_v)
```

```python
import functools

import jax
import jax.numpy as jnp
from jax import lax
from jax.experimental import pallas as pl
from jax.experimental.pallas import tpu as pltpu

F32 = jnp.float32
BF = jnp.bfloat16

D_MODEL = 1024
N_HEADS = 16
HEAD_DIM = 64
D_FF = 2816
DILATED_PATTERNS = ((128, 1), (512, 4), (2048, 16))
ROT_DIM = 16
ROPE_THETA = 500000.0
RMS_EPS = 1e-6
NEG_INF = -1e30
ATTN_SCALE = HEAD_DIM ** -0.5
GATE_LANES = 128
N_CHIPS = 4
MESH_AXES = ("x", "y", "c")
MESH = pl.DeviceIdType.MESH

ADAM_LR = 0.001
ADAM_B1 = 0.9
ADAM_B2 = 0.999
ADAM_EPS = 1e-08
ADAM_WD = 0.01
ADAM_STEP = 10

VMEM_LIMIT_BYTES = 56 * 1024 * 1024


def _params(*sem):
    return pltpu.CompilerParams(dimension_semantics=sem, vmem_limit_bytes=VMEM_LIMIT_BYTES)


def _hosted_call(body, *, grid, in_specs, out_specs, out_shape, scratch_shapes, args, name, guest=None):
    params = _params(*(["arbitrary"] * len(grid)))
    if guest is None:
        return pl.pallas_call(body, grid=grid, in_specs=in_specs, out_specs=out_specs, out_shape=out_shape,
                              scratch_shapes=scratch_shapes, name=name, compiler_params=params)(*args)
    n_in, n_out, n_scr = len(in_specs), len(out_specs), len(scratch_shapes)
    g_in, g_out = len(guest["args"]), len(guest["out_shape"])
    any_spec = pl.BlockSpec(memory_space=pl.ANY)

    def wrapped(*refs):
        i1 = n_in + g_in
        o1 = i1 + n_out
        o2 = o1 + g_out
        s1 = o2 + n_scr
        guest_refs = (refs[n_in:i1], refs[o1:o2], refs[s1:])
        ids = [pl.program_id(d) for d in range(len(grid))]
        first = functools.reduce(jnp.logical_and, [i == 0 for i in ids])
        last = functools.reduce(jnp.logical_and, [i == g - 1 for i, g in zip(ids, grid)])

        @pl.when(first)
        def _():
            guest["start"](*guest_refs)

        body(*refs[:n_in], *refs[i1:o1], *refs[o2:s1])

        @pl.when(last)
        def _():
            guest["finish"](*guest_refs)

    aliases = {n_in + k: n_out + k for k in range(g_in)} if guest.get("in_place") else {}
    return pl.pallas_call(
        wrapped, grid=grid, in_specs=list(in_specs) + [any_spec] * g_in, out_specs=list(out_specs) + [any_spec] * g_out,
        out_shape=list(out_shape) + list(guest["out_shape"]), scratch_shapes=list(scratch_shapes) + list(guest["scratch"]),
        input_output_aliases=aliases, name=name, compiler_params=params)(*args, *guest["args"])


def _rope_rotate(t, cos, sin_a, sin_b):
    outs = []
    for cidx in range(t.shape[1] // 128):
        tc = t[:, cidx * 128:(cidx + 1) * 128]
        outs.append(tc * cos + pltpu.roll(tc, 120, 1) * sin_a + pltpu.roll(tc, 8, 1) * sin_b)
    return jnp.concatenate(outs, axis=1)


def _mm_nn(a, b, *, tm, tn, out_dtype, name, resid=None, rope=None, guest=None, groups=None):
    M, K = a.shape
    N = b.shape[1]
    assert M % tm == 0 and N % tn == 0 and b.shape[0] == K
    n_in = 2 + (resid is not None) + (3 if rope is not None else 0)
    if groups is not None:
        assert rope is not None and N == 3 * tn * len(groups)

    def body(*refs):
        a_ref, b_ref = refs[0], refs[1]
        o_ref = refs[n_in]
        acc = jnp.dot(a_ref[...], b_ref[...], preferred_element_type=F32)
        if resid is not None:
            acc = acc + refs[2][...]
        if groups is not None:
            cos_ref, sa_ref, sb_ref = refs[n_in - 3:n_in]
            j = pl.program_id(1)
            for g, d in enumerate(groups):
                for is_v in (False, True):
                    @pl.when(jnp.logical_and(j // 3 == g, (j % 3 == 2) == is_v))
                    def _(g=g, d=d, is_v=is_v):
                        val = acc if is_v else _rope_rotate(acc, cos_ref[...], sa_ref[...], sb_ref[...])
                        if d == 1:
                            refs[n_in + g][...] = val.astype(out_dtype)
                        else:
                            _to_view(val, refs[-1], refs[n_in + g], d, tn)
        elif rope is not None:
            cos_ref, sa_ref, sb_ref = refs[n_in - 3:n_in]
            j = pl.program_id(1)

            @pl.when(j % 3 != 2)
            def _():
                o_ref[...] = _rope_rotate(acc, cos_ref[...], sa_ref[...], sb_ref[...]).astype(out_dtype)

            @pl.when(j % 3 == 2)
            def _():
                o_ref[...] = acc.astype(out_dtype)
        else:
            o_ref[...] = acc.astype(out_dtype)

    in_specs = [pl.BlockSpec((tm, K), lambda i, j: (i, 0)), pl.BlockSpec((K, tn), lambda i, j: (0, j))]
    args = [a, b]
    if resid is not None:
        in_specs.append(pl.BlockSpec((tm, tn), lambda i, j: (i, j)))
        args.append(resid)
    if rope is not None:
        assert tn == 1024
        for t in rope:
            in_specs.append(pl.BlockSpec((tm, 128), lambda i, j: (i, 0)))
            args.append(t)
    if groups is None:
        out_specs = [pl.BlockSpec((tm, tn), lambda i, j: (i, j))]
        out_shape = [jax.ShapeDtypeStruct((M, N), out_dtype)]
        scratch = []
    else:
        out_specs = [pl.BlockSpec((tm // d, d * tn), lambda i, j, g=g: (i, jnp.clip(j - 3 * g, 0, 2)))
                     for g, d in enumerate(groups)]
        out_shape = [jax.ShapeDtypeStruct((M // d, d * 3 * tn), out_dtype) for d in groups]
        scratch = [pltpu.VMEM((tn // 128, tm, 128), F32)]
    outs = _hosted_call(body, grid=(M // tm, N // tn), in_specs=in_specs, out_specs=out_specs, out_shape=out_shape,
                        scratch_shapes=scratch, args=args, name=name, guest=guest)
    nout = len(out_shape)
    res = outs[0] if groups is None else list(outs[:nout])
    return res if guest is None else (res, outs[nout:])


def _mm_nt(a, b, *, tm, to, tn, out_dtype, name, add=None, guest=None, views=(1,)):
    M, N = a.shape
    O = b.shape[0]
    assert M % tm == 0 and O % to == 0 and N % tn == 0 and b.shape[1] == N
    nk = N // tn

    def body(*refs):
        a_ref, b_ref = refs[0], refs[1]
        n_in = 2 + (add is not None)
        o_refs = refs[n_in:n_in + len(views)]
        acc_ref = refs[n_in + len(views)]
        k = pl.program_id(2)

        @pl.when(k == 0)
        def _():
            if add is not None:
                acc_ref[...] = refs[2][...]
            else:
                acc_ref[...] = jnp.zeros_like(acc_ref)

        acc_ref[...] += lax.dot_general(a_ref[...], b_ref[...], (((1,), (1,)), ((), ())),
                                        preferred_element_type=F32)

        @pl.when(k == nk - 1)
        def _():
            for o_ref, d in zip(o_refs, views):
                if d == 1:
                    o_ref[...] = acc_ref[...].astype(out_dtype)
                else:
                    _to_view(acc_ref[...], refs[-1], o_ref, d, to)

    in_specs = [pl.BlockSpec((tm, tn), lambda i, j, k: (i, k)), pl.BlockSpec((to, tn), lambda i, j, k: (j, k))]
    args = [a, b]
    if add is not None:
        in_specs.append(pl.BlockSpec((tm, to), lambda i, j, k: (i, j)))
        args.append(add)
    assert views == (1,) or (to == O and to % 128 == 0)
    scratch = [pltpu.VMEM((tm, to), F32)] + ([pltpu.VMEM((to // 128, tm, 128), F32)] if views != (1,) else [])
    outs = _hosted_call(
        body, grid=(M // tm, O // to, nk), in_specs=in_specs,
        out_specs=[pl.BlockSpec((tm, to), lambda i, j, k: (i, j)) if d == 1 else _view_spec(tm, d, to) for d in views],
        out_shape=[jax.ShapeDtypeStruct((M // d, d * O), out_dtype) for d in views],
        scratch_shapes=scratch, args=args, name=name, guest=guest)
    nv = len(views)
    res = outs[0] if nv == 1 else list(outs[:nv])
    return res if guest is None else (res, outs[nv:])


def _mm_tn(a, b, *, tk, tn, tm, out_dtype, name):
    M, K = a.shape
    N = b.shape[1]
    assert M % tm == 0 and K % tk == 0 and N % tn == 0 and b.shape[0] == M
    nm = M // tm

    def body(a_ref, b_ref, o_ref, acc_ref):
        m = pl.program_id(2)

        @pl.when(m == 0)
        def _():
            acc_ref[...] = jnp.zeros_like(acc_ref)

        acc_ref[...] += lax.dot_general(a_ref[...], b_ref[...], (((0,), (0,)), ((), ())),
                                        preferred_element_type=F32)

        @pl.when(m == nm - 1)
        def _():
            o_ref[...] = acc_ref[...].astype(out_dtype)

    return pl.pallas_call(
        body, grid=(K // tk, N // tn, nm),
        in_specs=[pl.BlockSpec((tm, tk), lambda i, j, m: (m, i)), pl.BlockSpec((tm, tn), lambda i, j, m: (m, j))],
        out_specs=pl.BlockSpec((tk, tn), lambda i, j, m: (i, j)),
        out_shape=jax.ShapeDtypeStruct((K, N), out_dtype),
        scratch_shapes=[pltpu.VMEM((tk, tn), F32)], name=name,
        compiler_params=_params("parallel", "parallel", "arbitrary"),
    )(a, b)


ROW_TILE = 512


def _rms_fwd(x, g, *, name):
    S, Dm = x.shape

    def body(x_ref, g_ref, o_ref):
        xf = x_ref[...]
        r = lax.rsqrt(jnp.mean(xf * xf, axis=-1, keepdims=True) + RMS_EPS)
        o_ref[...] = (xf * r * g_ref[...]).astype(BF)

    return pl.pallas_call(
        body, grid=(S // ROW_TILE,),
        in_specs=[pl.BlockSpec((ROW_TILE, Dm), lambda i: (i, 0)), pl.BlockSpec((1, Dm), lambda i: (0, 0))],
        out_specs=pl.BlockSpec((ROW_TILE, Dm), lambda i: (i, 0)),
        out_shape=jax.ShapeDtypeStruct((S, Dm), BF), name=name, compiler_params=_params("parallel"),
    )(x, g)


def _rms_bwd(x, g, dn, dres, *, name):
    S, Dm = x.shape

    def body(x_ref, g_ref, dn_ref, dres_ref, dx_ref, dxb_ref, dg_ref):
        i = pl.program_id(0)
        xf = x_ref[...]
        r = lax.rsqrt(jnp.mean(xf * xf, axis=-1, keepdims=True) + RMS_EPS)
        xh = xf * r
        dnf = dn_ref[...]
        dyg = dnf * g_ref[...]
        dx = dres_ref[...] + r * (dyg - xh * jnp.mean(dyg * xh, axis=-1, keepdims=True))
        dx_ref[...] = dx
        dxb_ref[...] = dx.astype(BF)

        @pl.when(i == 0)
        def _():
            dg_ref[...] = jnp.zeros_like(dg_ref)

        dg_ref[...] += jnp.sum(dnf * xh, axis=0, keepdims=True)

    row = pl.BlockSpec((ROW_TILE, Dm), lambda i: (i, 0))
    vec = pl.BlockSpec((1, Dm), lambda i: (0, 0))
    return pl.pallas_call(
        body, grid=(S // ROW_TILE,), in_specs=[row, vec, row, row], out_specs=[row, row, vec],
        out_shape=[jax.ShapeDtypeStruct((S, Dm), F32), jax.ShapeDtypeStruct((S, Dm), BF),
                   jax.ShapeDtypeStruct((1, Dm), F32)],
        name=name, compiler_params=_params("arbitrary"),
    )(x, g, dn, dres)


def _loss_head(h, g, tgt, *, name):
    S, Dm = h.shape

    def body(h_ref, g_ref, t_ref, loss_ref, dh_ref, dhb_ref, dg_ref):
        i = pl.program_id(0)
        xf = h_ref[...]
        r = lax.rsqrt(jnp.mean(xf * xf, axis=-1, keepdims=True) + RMS_EPS)
        xh = xf * r
        gv = g_ref[...]
        err = xh * gv - t_ref[...]
        dy = err * (1.0 / Dm)
        dyg = dy * gv
        dh = r * (dyg - xh * jnp.mean(dyg * xh, axis=-1, keepdims=True))
        dh_ref[...] = dh
        dhb_ref[...] = dh.astype(BF)

        @pl.when(i == 0)
        def _():
            dg_ref[...] = jnp.zeros_like(dg_ref)
            loss_ref[...] = jnp.zeros_like(loss_ref)

        dg_ref[...] += jnp.sum(dy * xh, axis=0, keepdims=True)
        part = 0.5 * jnp.sum(jnp.mean(err * err, axis=-1, keepdims=True), axis=0, keepdims=True)
        loss_ref[...] += jnp.broadcast_to(part, loss_ref.shape)

    row = pl.BlockSpec((ROW_TILE, Dm), lambda i: (i, 0))
    vec = pl.BlockSpec((1, Dm), lambda i: (0, 0))
    return pl.pallas_call(
        body, grid=(S // ROW_TILE,), in_specs=[row, vec, row],
        out_specs=[pl.BlockSpec((1, 128), lambda i: (0, 0)), row, row, vec],
        out_shape=[jax.ShapeDtypeStruct((1, 128), F32), jax.ShapeDtypeStruct((S, Dm), F32),
                   jax.ShapeDtypeStruct((S, Dm), BF), jax.ShapeDtypeStruct((1, Dm), F32)],
        name=name, compiler_params=_params("arbitrary"),
    )(h, g, tgt)


SWIGLU_ROWS = 256


def _swiglu_fwd(gu, *, name):
    S = gu.shape[0]

    def body(g_ref, u_ref, o_ref):
        g = g_ref[...].astype(F32)
        sig = 1.0 / (1.0 + jnp.exp(-g))
        o_ref[...] = (g * sig * u_ref[...].astype(F32)).astype(BF)

    return pl.pallas_call(
        body, grid=(S // SWIGLU_ROWS,),
        in_specs=[pl.BlockSpec((SWIGLU_ROWS, D_FF), lambda i: (i, 0)), pl.BlockSpec((SWIGLU_ROWS, D_FF), lambda i: (i, 1))],
        out_specs=pl.BlockSpec((SWIGLU_ROWS, D_FF), lambda i: (i, 0)),
        out_shape=jax.ShapeDtypeStruct((S, D_FF), BF), name=name, compiler_params=_params("parallel"),
    )(gu, gu)


def _swiglu_bwd(gu, dact, *, name):
    S = gu.shape[0]

    def body(g_ref, u_ref, d_ref, o_ref):
        g = g_ref[...].astype(F32)
        u = u_ref[...].astype(F32)
        d = d_ref[...].astype(F32)
        sig = 1.0 / (1.0 + jnp.exp(-g))
        o_ref[:, :D_FF] = (d * u * sig * (1.0 + g * (1.0 - sig))).astype(BF)
        o_ref[:, D_FF:] = (d * g * sig).astype(BF)

    return pl.pallas_call(
        body, grid=(S // SWIGLU_ROWS,),
        in_specs=[pl.BlockSpec((SWIGLU_ROWS, D_FF), lambda i: (i, 0)), pl.BlockSpec((SWIGLU_ROWS, D_FF), lambda i: (i, 1)),
                  pl.BlockSpec((SWIGLU_ROWS, D_FF), lambda i: (i, 0))],
        out_specs=pl.BlockSpec((SWIGLU_ROWS, 2 * D_FF), lambda i: (i, 0)),
        out_shape=jax.ShapeDtypeStruct((S, 2 * D_FF), BF), name=name, compiler_params=_params("parallel"),
    )(gu, gu, dact)


def _attn_fwd(qa, ka, va, qcb, kcb, vcb, *, dil, T, nkv, window, name, c=None, cT=None, o_dtype=F32):
    L = qa.shape[0]
    nq = L // T
    fox = c is not None

    def kv_block(n, j):
        return n - (nkv - 1) + j

    def body(*refs):
        if fox:
            q_ref, k_ref, v_ref, c_ref, ct_ref, o_ref, lse_ref, m_sc, l_sc, acc_sc = refs
        else:
            q_ref, k_ref, v_ref, o_ref, lse_ref, m_sc, l_sc, acc_sc = refs
        n = pl.program_id(1)
        j = pl.program_id(2)
        kb = kv_block(n, j)

        @pl.when(j == 0)
        def _():
            m_sc[...] = jnp.full(m_sc.shape, NEG_INF, F32)
            l_sc[...] = jnp.zeros_like(l_sc)
            acc_sc[...] = jnp.zeros_like(acc_sc)

        @pl.when(kb >= 0)
        def _():
            diff = (n * T + lax.broadcasted_iota(jnp.int32, (T, T), 0)) - (kb * T + lax.broadcasted_iota(jnp.int32, (T, T), 1))
            valid = diff >= 0
            if window is not None:
                valid = jnp.logical_and(valid, diff <= window)
            for h in range(N_HEADS):
                hs = slice(h * HEAD_DIM, (h + 1) * HEAD_DIM)
                qh = q_ref[:, hs] * jnp.asarray(ATTN_SCALE, BF)
                s = lax.dot_general(qh, k_ref[:, hs], (((1,), (1,)), ((), ())), preferred_element_type=F32)
                if fox:
                    s = s + c_ref[:, h:h + 1] - ct_ref[h:h + 1, :]
                s = jnp.where(valid, s, NEG_INF)
                m_prev = m_sc[:, h:h + 1]
                m_new = jnp.maximum(m_prev, jnp.max(s, axis=1, keepdims=True))
                alpha = jnp.exp(m_prev - m_new)
                p = jnp.exp(s - m_new)
                l_sc[:, h:h + 1] = alpha * l_sc[:, h:h + 1] + jnp.sum(p, axis=1, keepdims=True)
                acc_sc[:, hs] = alpha * acc_sc[:, hs] + jnp.dot(p.astype(BF), v_ref[:, hs], preferred_element_type=F32)
                m_sc[:, h:h + 1] = m_new

        @pl.when(j == nkv - 1)
        def _():
            lane = lax.broadcasted_iota(jnp.int32, (T, 128), 1)
            lse = jnp.zeros((T, 128), F32)
            for h in range(N_HEADS):
                hs = slice(h * HEAD_DIM, (h + 1) * HEAD_DIM)
                l = l_sc[:, h:h + 1]
                o_ref[:, hs] = (acc_sc[:, hs] / l).astype(o_dtype)
                lse = jnp.where(lane == h, m_sc[:, h:h + 1] + jnp.log(l), lse)
            lse_ref[...] = lse

    def kvi(n, j):
        return jnp.maximum(kv_block(n, j), 0)

    in_specs = [pl.BlockSpec((T, 1024), lambda r, n, j: (n, qcb(r))),
                pl.BlockSpec((T, 1024), lambda r, n, j: (kvi(n, j), kcb(r))),
                pl.BlockSpec((T, 1024), lambda r, n, j: (kvi(n, j), vcb(r)))]
    args = [qa, ka, va]
    if fox:
        in_specs += [pl.BlockSpec((T, GATE_LANES), lambda r, n, j: (n, 0)),
                     pl.BlockSpec((GATE_LANES, T), lambda r, n, j: (0, kvi(n, j)))]
        args += [c, cT]
    return pl.pallas_call(
        body, grid=(dil, nq, nkv), in_specs=in_specs,
        out_specs=[pl.BlockSpec((T, 1024), lambda r, n, j: (n, r)), pl.BlockSpec((T, 128), lambda r, n, j: (n, r))],
        out_shape=[jax.ShapeDtypeStruct((L, dil * 1024), o_dtype), jax.ShapeDtypeStruct((L, dil * 128), F32)],
        scratch_shapes=[pltpu.VMEM((T, 128), F32), pltpu.VMEM((T, 128), F32), pltpu.VMEM((T, 1024), F32)],
        name=name, compiler_params=_params("parallel", "parallel", "arbitrary"),
    )(*args)


def _attn_bwd(qa, ka, va, qcb, kcb, vcb, doa, oa, lsea, *, dil, T, nqs, window, name, c=None, cT=None):
    L = qa.shape[0]
    nq = L // T
    fox = c is not None

    def body(*refs):
        if fox:
            (q_ref, k_ref, v_ref, do_ref, o_ref, lse_ref, c_ref, ct_ref,
             dq_ref, dk_ref, dv_ref, dct_ref, dcq_ref, dq_sc, dk_sc, dv_sc, dc_sc, dcq_sc) = refs
        else:
            (q_ref, k_ref, v_ref, do_ref, o_ref, lse_ref,
             dq_ref, dk_ref, dv_ref, dq_sc, dk_sc, dv_sc) = refs
        kb = pl.program_id(1)
        jq = pl.program_id(2)
        qb = kb + jq

        @pl.when(jnp.logical_and(kb == 0, jq == 0))
        def _():
            dq_sc[...] = jnp.zeros_like(dq_sc)
            if fox:
                dcq_sc[...] = jnp.zeros_like(dcq_sc)

        @pl.when(jq == 0)
        def _():
            dk_sc[...] = jnp.zeros_like(dk_sc)
            dv_sc[...] = jnp.zeros_like(dv_sc)
            if fox:
                dc_sc[...] = jnp.zeros_like(dc_sc)

        @pl.when(qb < nq)
        def _():
            diff = (qb * T + lax.broadcasted_iota(jnp.int32, (T, T), 0)) - (kb * T + lax.broadcasted_iota(jnp.int32, (T, T), 1))
            valid = diff >= 0
            if window is not None:
                valid = jnp.logical_and(valid, diff <= window)
            qrows = pl.ds(pl.multiple_of(qb * T, T), T)
            for h in range(N_HEADS):
                hs = slice(h * HEAD_DIM, (h + 1) * HEAD_DIM)
                qh = q_ref[:, hs] * jnp.asarray(ATTN_SCALE, BF)
                kh = k_ref[:, hs]
                doh = do_ref[:, hs]
                s = lax.dot_general(qh, kh, (((1,), (1,)), ((), ())), preferred_element_type=F32)
                if fox:
                    s = s + c_ref[:, h:h + 1] - ct_ref[h:h + 1, :]
                s = jnp.where(valid, s, NEG_INF)
                p = jnp.exp(s - lse_ref[:, h:h + 1])
                dp = lax.dot_general(doh, v_ref[:, hs], (((1,), (1,)), ((), ())), preferred_element_type=F32)
                delta = jnp.sum(doh.astype(F32) * o_ref[:, hs].astype(F32), axis=1, keepdims=True)
                ds = p * (dp - delta)
                dsb = ds.astype(BF)
                dv_sc[:, hs] += lax.dot_general(p.astype(BF), doh, (((0,), (0,)), ((), ())), preferred_element_type=F32)
                dk_sc[:, hs] += lax.dot_general(dsb, qh, (((0,), (0,)), ((), ())), preferred_element_type=F32)
                dq_sc[qrows, hs] += jnp.dot(dsb, kh, preferred_element_type=F32) * ATTN_SCALE
                if fox:
                    dc_sc[h:h + 1, :] -= jnp.sum(ds, axis=0, keepdims=True)
                    dcq_sc[qrows, h:h + 1] += jnp.sum(ds, axis=1, keepdims=True)

        @pl.when(jq == nqs - 1)
        def _():
            dk_ref[...] = dk_sc[...].astype(BF)
            dv_ref[...] = dv_sc[...].astype(BF)
            if fox:
                dct_ref[...] = dc_sc[...]

        @pl.when(jnp.logical_and(kb == nq - 1, jq == nqs - 1))
        def _():
            def put(i, carry):
                rows = pl.ds(pl.multiple_of(i * T, T), T)
                dq_ref[rows, :] = dq_sc[rows, :].astype(BF)
                return carry
            lax.fori_loop(0, nq, put, 0)
            if fox:
                dcq_ref[...] = dcq_sc[...]

    def qi(kb, jq):
        return jnp.minimum(kb + jq, nq - 1)

    in_specs = [pl.BlockSpec((T, 1024), lambda r, kb, jq: (qi(kb, jq), qcb(r))),
                pl.BlockSpec((T, 1024), lambda r, kb, jq: (kb, kcb(r))),
                pl.BlockSpec((T, 1024), lambda r, kb, jq: (kb, vcb(r))),
                pl.BlockSpec((T, 1024), lambda r, kb, jq: (qi(kb, jq), r)),
                pl.BlockSpec((T, 1024), lambda r, kb, jq: (qi(kb, jq), r)),
                pl.BlockSpec((T, 128), lambda r, kb, jq: (qi(kb, jq), r))]
    args = [qa, ka, va, doa, oa, lsea]
    out_specs = [pl.BlockSpec((L, 1024), lambda r, kb, jq: (0, r)),
                 pl.BlockSpec((T, 1024), lambda r, kb, jq: (kb, r)),
                 pl.BlockSpec((T, 1024), lambda r, kb, jq: (kb, r))]
    out_shape = [jax.ShapeDtypeStruct((L, dil * 1024), BF)] * 3
    scratch = [pltpu.VMEM((L, 1024), F32), pltpu.VMEM((T, 1024), F32), pltpu.VMEM((T, 1024), F32)]
    if fox:
        in_specs += [pl.BlockSpec((T, GATE_LANES), lambda r, kb, jq: (qi(kb, jq), 0)),
                     pl.BlockSpec((GATE_LANES, T), lambda r, kb, jq: (0, kb))]
        args += [c, cT]
        out_specs.append(pl.BlockSpec((GATE_LANES, T), lambda r, kb, jq: (0, kb)))
        out_shape.append(jax.ShapeDtypeStruct((GATE_LANES, L), F32))
        scratch.append(pltpu.VMEM((GATE_LANES, T), F32))
        out_specs.append(pl.BlockSpec((L, GATE_LANES), lambda r, kb, jq: (0, 0)))
        out_shape.append(jax.ShapeDtypeStruct((L, GATE_LANES), F32))
        scratch.append(pltpu.VMEM((L, GATE_LANES), F32))
    return pl.pallas_call(
        body, grid=(dil, nq, nqs), in_specs=in_specs, out_specs=out_specs, out_shape=out_shape,
        scratch_shapes=scratch, name=name, compiler_params=_params("arbitrary", "arbitrary", "arbitrary"),
    )(*args)


def _band_masks(T, n):
    row = lax.broadcasted_iota(jnp.int32, (T, T), 0)
    col = lax.broadcasted_iota(jnp.int32, (T, T), 1)
    return jnp.logical_and(col >= row, n > 0), col <= row


def _band_fwd(qa, ka, va, qcb, kcb, vcb, *, dil, T, window, name, guest=None):
    L = qa.shape[0]
    nq = L // T
    assert window == T
    nt = (((1,), (1,)), ((), ()))

    def body(q_ref, kp_ref, kc_ref, vp_ref, vc_ref, o_ref, lse_ref):
        valid_prev, valid_cur = _band_masks(T, pl.program_id(1))
        lane = lax.broadcasted_iota(jnp.int32, (T, 128), 1)
        low = lane < HEAD_DIM
        ones = jnp.ones((T, 128), BF)
        lse = jnp.zeros((T, 128), F32)
        def scores(h):
            ps = slice((h // 2) * 128, (h // 2 + 1) * 128)
            qp = q_ref[:, ps] * jnp.asarray(ATTN_SCALE, BF)
            qm = jnp.where(low if h % 2 == 0 else jnp.logical_not(low), qp, jnp.zeros_like(qp))
            s0 = jnp.where(valid_prev, lax.dot_general(qm, kp_ref[:, ps], nt, preferred_element_type=F32), NEG_INF)
            s1 = jnp.where(valid_cur, lax.dot_general(qm, kc_ref[:, ps], nt, preferred_element_type=F32), NEG_INF)
            return s0, s1

        def softmax(s0, s1):
            m = jnp.maximum(jnp.max(s0, axis=1, keepdims=True), jnp.max(s1, axis=1, keepdims=True))
            return m, jnp.exp(s0 - m).astype(BF), jnp.exp(s1 - m).astype(BF)

        def weighted(h, p0, p1):
            ps = slice((h // 2) * 128, (h // 2 + 1) * 128)
            l = jnp.dot(p0, ones, preferred_element_type=F32) + jnp.dot(p1, ones, preferred_element_type=F32)
            acc = jnp.dot(p0, vp_ref[:, ps], preferred_element_type=F32) + jnp.dot(p1, vc_ref[:, ps], preferred_element_type=F32)
            return l, acc

        sc, pr, even = {}, {}, None
        for t in range(N_HEADS + 2):
            if t < N_HEADS:
                sc[t] = scores(t)
            done = None
            if t >= 2:
                m, p0, p1 = pr.pop(t - 2)
                done = (m,) + weighted(t - 2, p0, p1)
            if 1 <= t <= N_HEADS:
                pr[t - 1] = softmax(*sc.pop(t - 1))
            if done is not None:
                h = t - 2
                m, l, acc = done
                lse = jnp.where(lane == h, m + jnp.log(l), lse)
                if h % 2 == 0:
                    even = acc / l
                else:
                    o_ref[:, (h // 2) * 128:(h // 2 + 1) * 128] = jnp.where(low, even, acc / l)
        lse_ref[...] = lse

    def prev(n):
        return jnp.maximum(n - 1, 0)

    blk = lambda f, cb: pl.BlockSpec((T, 1024), lambda r, n: (f(n), cb(r)))
    same = lambda n: n
    outs = _hosted_call(
        body, grid=(dil, nq),
        in_specs=[blk(same, qcb), blk(prev, kcb), blk(same, kcb), blk(prev, vcb), blk(same, vcb)],
        out_specs=[pl.BlockSpec((T, 1024), lambda r, n: (n, r)), pl.BlockSpec((T, 128), lambda r, n: (n, r))],
        out_shape=[jax.ShapeDtypeStruct((L, dil * 1024), F32), jax.ShapeDtypeStruct((L, dil * 128), F32)],
        scratch_shapes=[], args=(qa, ka, ka, va, va), name=name, guest=guest)
    return outs if guest is None else (outs[:2], outs[2:])


def _band_bwd(qa, ka, va, qcb, kcb, vcb, doa, oa, lsea, *, dil, T, window, name, guest=None):
    L = qa.shape[0]
    nq = L // T
    assert window == T
    nt = (((1,), (1,)), ((), ()))
    tn = (((0,), (0,)), ((), ()))

    def body(q_ref, kp_ref, kc_ref, vp_ref, vc_ref, do_ref, o_ref, lse_ref, dq_ref, dk_ref, dv_ref, ck_sc, cv_sc):
        n = pl.program_id(1)

        @pl.when(n == 0)
        def _():
            ck_sc[...] = jnp.zeros_like(ck_sc)
            cv_sc[...] = jnp.zeros_like(cv_sc)

        @pl.when(n < nq)
        def _():
            valid_prev, valid_cur = _band_masks(T, n)
            low = lax.broadcasted_iota(jnp.int32, (T, 128), 1) < HEAD_DIM
            dot = functools.partial(lax.dot_general, preferred_element_type=F32)

            def pair(h):
                return slice((h // 2) * 128, (h // 2 + 1) * 128)

            def products(h):
                ps = pair(h)
                mask = low if h % 2 == 0 else jnp.logical_not(low)
                qp = q_ref[:, ps] * jnp.asarray(ATTN_SCALE, BF)
                dop = do_ref[:, ps]
                qm = jnp.where(mask, qp, jnp.zeros_like(qp))
                dom = jnp.where(mask, dop, jnp.zeros_like(dop))
                s0 = jnp.where(valid_prev, dot(qm, kp_ref[:, ps], nt), NEG_INF)
                s1 = jnp.where(valid_cur, dot(qm, kc_ref[:, ps], nt), NEG_INF)
                return qm, dom, s0, s1, dot(dom, vp_ref[:, ps], nt), dot(dom, vc_ref[:, ps], nt)

            def pointwise(h, qm, dom, s0, s1, dp0, dp1):
                ps = pair(h)
                mask = low if h % 2 == 0 else jnp.logical_not(low)
                prod = do_ref[:, ps].astype(F32) * o_ref[:, ps].astype(F32)
                delta = jnp.sum(jnp.where(mask, prod, 0.0), axis=1, keepdims=True)
                lse = lse_ref[:, h:h + 1]
                p0 = jnp.exp(s0 - lse)
                p1 = jnp.exp(s1 - lse)
                ds0 = (p0 * (dp0 - delta)).astype(BF)
                ds1 = (p1 * (dp1 - delta)).astype(BF)
                return qm, dom, p0.astype(BF), p1.astype(BF), ds0, ds1

            def gradients(h, qm, dom, p0, p1, ds0, ds1):
                ps = pair(h)
                dq = dot(ds0, kp_ref[:, ps], (((1,), (0,)), ((), ()))) + dot(ds1, kc_ref[:, ps], (((1,), (0,)), ((), ())))
                return dq, dot(ds0, qm, tn), dot(p0, dom, tn), dot(ds1, qm, tn), dot(p1, dom, tn)

            st1, st2, even = {}, {}, None
            for t in range(N_HEADS + 2):
                if t < N_HEADS:
                    st1[t] = products(t)
                done = gradients(t - 2, *st2.pop(t - 2)) if t >= 2 else None
                if 1 <= t <= N_HEADS:
                    st2[t - 1] = pointwise(t - 1, *st1.pop(t - 1))
                if done is not None:
                    h = t - 2
                    if h % 2 == 0:
                        even = done
                    else:
                        ps = pair(h)
                        dq_ref[:, ps] = (jnp.where(low, even[0], done[0]) * ATTN_SCALE).astype(BF)
                        dk_ref[:, ps] = (ck_sc[:, ps] + even[1] + done[1]).astype(BF)
                        dv_ref[:, ps] = (cv_sc[:, ps] + even[2] + done[2]).astype(BF)
                        ck_sc[:, ps] = even[3] + done[3]
                        cv_sc[:, ps] = even[4] + done[4]

        @pl.when(n == nq)
        def _():
            dk_ref[...] = ck_sc[...].astype(BF)
            dv_ref[...] = cv_sc[...].astype(BF)

    def cur(n):
        return jnp.minimum(n, nq - 1)

    def prev(n):
        return jnp.maximum(cur(n) - 1, 0)

    blk = lambda f, cb: pl.BlockSpec((T, 1024), lambda r, n: (f(n), cb(r)))
    own = lambda r: r
    outs = _hosted_call(
        body, grid=(dil, nq + 1),
        in_specs=[blk(cur, qcb), blk(prev, kcb), blk(cur, kcb), blk(prev, vcb), blk(cur, vcb), blk(cur, own), blk(cur, own),
                  pl.BlockSpec((T, 128), lambda r, n: (cur(n), r))],
        out_specs=[blk(cur, own), blk(lambda n: jnp.maximum(n - 1, 0), own), blk(lambda n: jnp.maximum(n - 1, 0), own)],
        out_shape=[jax.ShapeDtypeStruct((L, dil * 1024), BF)] * 3,
        scratch_shapes=[pltpu.VMEM((T, 1024), F32), pltpu.VMEM((T, 1024), F32)],
        args=(qa, ka, ka, va, va, doa, oa, lsea), name=name, guest=guest)
    return outs if guest is None else (outs[:3], outs[3:])


FOX_T = 256
FOX_ROWS = 128


def _fox_fwd(qkv, cT, *, name, guest=None):
    S = qkv.shape[0]
    T, R = FOX_T, FOX_ROWS
    nq = S // T
    nt = (((1,), (1,)), ((), ()))
    chains = [(h, rh) for h in range(N_HEADS) for rh in range(T // R)]

    def body(q_ref, k_ref, v_ref, ct_ref, o_ref, lse_ref, m_sc, l_sc, acc_sc):
        n = pl.program_id(0)
        j = pl.program_id(1)
        lane = lax.broadcasted_iota(jnp.int32, (R, 128), 1)
        low = lane < HEAD_DIM
        ones = jnp.ones((T, 128), BF)

        @pl.when(j == 0)
        def _():
            m_sc[...] = jnp.full(m_sc.shape, NEG_INF, F32)
            l_sc[...] = jnp.zeros_like(l_sc)
            acc_sc[...] = jnp.zeros_like(acc_sc)

        def step(diagonal):
            def pair(h):
                return slice((h // 2) * 128, (h // 2 + 1) * 128)

            def rows(rh):
                return slice(rh * R, (rh + 1) * R)

            def scores(h, rh):
                qp = q_ref[rows(rh), pair(h)] * jnp.asarray(ATTN_SCALE, BF)
                qm = jnp.where(low if h % 2 == 0 else jnp.logical_not(low), qp, jnp.zeros_like(qp))
                s = lax.dot_general(qm, k_ref[:, pair(h)], nt, preferred_element_type=F32) - ct_ref[h:h + 1, :]
                if diagonal:
                    keep = (lax.broadcasted_iota(jnp.int32, (R, T), 1)
                            <= rh * R + lax.broadcasted_iota(jnp.int32, (R, T), 0))
                    s = jnp.where(keep, s, NEG_INF)
                return s

            def softmax(h, rh, s):
                m_prev = m_sc[h, rows(rh), :]
                m_new = jnp.maximum(m_prev, jnp.max(s, axis=1, keepdims=True))
                p = jnp.exp(s - jnp.concatenate([m_new] * (T // 128), axis=1)).astype(BF)
                return m_new, jnp.exp(m_prev - m_new), p

            def weighted(h, p):
                vx = jnp.concatenate([v_ref[:, pair(h)], ones], axis=1)
                return jnp.dot(p, vx, preferred_element_type=F32)

            sc, pr, even = {}, {}, {}
            nch = len(chains)
            for t in range(nch + 2):
                if t < nch:
                    sc[t] = scores(*chains[t])
                done = None
                if t >= 2:
                    m_new, alpha, p = pr.pop(t - 2)
                    done = (m_new, alpha, weighted(chains[t - 2][0], p))
                if 1 <= t <= nch:
                    pr[t - 1] = softmax(*chains[t - 1], sc.pop(t - 1))
                if done is not None:
                    h, rh = chains[t - 2]
                    m_new, alpha, pv = done
                    m_sc[h, rows(rh), :] = m_new
                    l_sc[h, rows(rh), :] = alpha * l_sc[h, rows(rh), :] + pv[:, 128:]
                    if h % 2 == 0:
                        even[rh] = (alpha, pv[:, :128])
                    else:
                        a0, pv0 = even.pop(rh)
                        acc = acc_sc[h // 2, rows(rh), :]
                        acc_sc[h // 2, rows(rh), :] = jnp.where(low, a0 * acc + pv0, alpha * acc + pv[:, :128])

        @pl.when(j < n)
        def _():
            step(False)

        @pl.when(j == n)
        def _():
            step(True)
            lane_t = lax.broadcasted_iota(jnp.int32, (T, 128), 1)
            low_t = lane_t < HEAD_DIM
            lse = jnp.zeros((T, 128), F32)
            for h in range(N_HEADS):
                lse = jnp.where(lane_t == h, m_sc[h] + jnp.log(l_sc[h]), lse)
            lse_ref[...] = lse
            for hp in range(N_HEADS // 2):
                inv = jnp.where(low_t, 1.0 / l_sc[2 * hp], 1.0 / l_sc[2 * hp + 1])
                o_ref[:, hp * 128:(hp + 1) * 128] = (acc_sc[hp] * inv).astype(BF)

    def kv(n, j):
        return jnp.minimum(j, n)

    outs = _hosted_call(
        body, grid=(nq, nq),
        in_specs=[pl.BlockSpec((T, 1024), lambda n, j: (n, 0)), pl.BlockSpec((T, 1024), lambda n, j: (kv(n, j), 1)),
                  pl.BlockSpec((T, 1024), lambda n, j: (kv(n, j), 2)), pl.BlockSpec((GATE_LANES, T), lambda n, j: (0, kv(n, j)))],
        out_specs=[pl.BlockSpec((T, 1024), lambda n, j: (n, 0)), pl.BlockSpec((T, 128), lambda n, j: (n, 0))],
        out_shape=[jax.ShapeDtypeStruct((S, 1024), BF), jax.ShapeDtypeStruct((S, 128), F32)],
        scratch_shapes=[pltpu.VMEM((N_HEADS, T, 128), F32), pltpu.VMEM((N_HEADS, T, 128), F32),
                        pltpu.VMEM((N_HEADS // 2, T, 128), F32)],
        args=(qkv, qkv, qkv, cT), name=name, guest=guest)
    return outs if guest is None else (outs[:2], outs[2:])


def _fox_bwd(qkv, cT, do, o, lse, *, name, guest=None):
    S = qkv.shape[0]
    T, R = FOX_T, FOX_ROWS
    nq = S // T
    nt = (((1,), (1,)), ((), ()))
    tn = (((0,), (0,)), ((), ()))
    nn = (((1,), (0,)), ((), ()))
    chains = [(h, rh) for h in range(N_HEADS) for rh in range(T // R)]
    dot = functools.partial(lax.dot_general, preferred_element_type=F32)

    def body(q_ref, k_ref, v_ref, ct_ref, do_ref, o_ref, lse_ref, dq_ref, dk_ref, dv_ref, dct_ref, dcq_ref,
             dq_sc, dk_sc, dv_sc, dc_sc, dcq_sc):
        kb = pl.program_id(0)
        jq = pl.program_id(1)
        qb = kb + jq
        lane = lax.broadcasted_iota(jnp.int32, (R, 128), 1)
        low = lane < HEAD_DIM
        ones_k = jnp.ones((T, 128), BF)
        ones_r = jnp.ones((8, R), BF)

        @pl.when(jnp.logical_and(kb == 0, jq == 0))
        def _():
            dq_sc[...] = jnp.zeros_like(dq_sc)
            dcq_sc[...] = jnp.zeros_like(dcq_sc)

        @pl.when(jq == 0)
        def _():
            dk_sc[...] = jnp.zeros_like(dk_sc)
            dv_sc[...] = jnp.zeros_like(dv_sc)
            dc_sc[...] = jnp.zeros_like(dc_sc)

        def step(diagonal):
            def pair(h):
                return slice((h // 2) * 128, (h // 2 + 1) * 128)

            def rows(rh):
                return slice(rh * R, (rh + 1) * R)

            def qrows(rh):
                return pl.ds(pl.multiple_of(qb * T + rh * R, R), R)

            def products(h, rh):
                mask = low if h % 2 == 0 else jnp.logical_not(low)
                qp = q_ref[rows(rh), pair(h)] * jnp.asarray(ATTN_SCALE, BF)
                dop = do_ref[rows(rh), pair(h)]
                qm = jnp.where(mask, qp, jnp.zeros_like(qp))
                dom = jnp.where(mask, dop, jnp.zeros_like(dop))
                s = dot(qm, k_ref[:, pair(h)], nt) - ct_ref[h:h + 1, :]
                if diagonal:
                    keep = (lax.broadcasted_iota(jnp.int32, (R, T), 1)
                            <= rh * R + lax.broadcasted_iota(jnp.int32, (R, T), 0))
                    s = jnp.where(keep, s, NEG_INF)
                return qm, dom, s, dot(dom, v_ref[:, pair(h)], nt)

            def pointwise(h, rh, qm, dom, s, dp):
                mask = low if h % 2 == 0 else jnp.logical_not(low)
                prod = do_ref[rows(rh), pair(h)].astype(F32) * o_ref[rows(rh), pair(h)].astype(F32)
                delta = jnp.sum(jnp.where(mask, prod, 0.0), axis=1, keepdims=True)
                p = jnp.exp(s - lse_ref[rows(rh), h:h + 1])
                ds = (p * (dp - delta)).astype(BF)
                return qm, dom, p.astype(BF), ds

            def gradients(h, qm, dom, p, ds):
                kx = jnp.concatenate([k_ref[:, pair(h)], ones_k], axis=1)
                return dot(ds, kx, nn), dot(qm, ds, tn), dot(dom, p, tn), dot(ones_r, ds, nn)

            st1, st2, even = {}, {}, {}
            dcq_tiles = [jnp.zeros((R, 128), F32) for _ in range(T // R)]
            nch = len(chains)
            for t in range(nch + 2):
                if t < nch:
                    st1[t] = products(*chains[t])
                done = gradients(chains[t - 2][0], *st2.pop(t - 2)) if t >= 2 else None
                if 1 <= t <= nch:
                    st2[t - 1] = pointwise(*chains[t - 1], *st1.pop(t - 1))
                if done is not None:
                    h, rh = chains[t - 2]
                    dq_rsum, dk, dv, csum = done
                    dq = dq_rsum[:, :128]
                    dcq_tiles[rh] = jnp.where(lane == h, dq_rsum[:, 128:], dcq_tiles[rh])
                    dc_sc[h:h + 1, :] -= csum[0:1, :]
                    if h % 2 == 0:
                        even[rh] = (dq, dk, dv)
                    else:
                        dq0, dk0, dv0 = even.pop(rh)
                        dq_sc[qrows(rh), pair(h)] += jnp.where(low, dq0, dq) * ATTN_SCALE
                        dk_sc[h // 2] += dk0 + dk
                        dv_sc[h // 2] += dv0 + dv
            for rh in range(T // R):
                dcq_sc[qrows(rh), :] += dcq_tiles[rh]

        @pl.when(jnp.logical_and(jq > 0, qb < nq))
        def _():
            step(False)

        @pl.when(jq == 0)
        def _():
            step(True)

        @pl.when(jq == nq - 1)
        def _():
            for hp in range(N_HEADS // 2):
                dk_ref[:, hp * 128:(hp + 1) * 128] = dk_sc[hp].T.astype(BF)
                dv_ref[:, hp * 128:(hp + 1) * 128] = dv_sc[hp].T.astype(BF)
            dct_ref[...] = dc_sc[...]

        @pl.when(jnp.logical_and(kb == nq - 1, jq == nq - 1))
        def _():
            def put(i, carry):
                r = pl.ds(pl.multiple_of(i * T, T), T)
                dq_ref[r, :] = dq_sc[r, :].astype(BF)
                return carry
            lax.fori_loop(0, nq, put, 0)
            dcq_ref[...] = dcq_sc[...]

    def qi(kb, jq):
        return jnp.minimum(kb + jq, nq - 1)

    qblk = lambda col: pl.BlockSpec((T, 1024), lambda kb, jq: (qi(kb, jq), col))
    kblk = lambda col: pl.BlockSpec((T, 1024), lambda kb, jq: (kb, col))
    whole = pl.BlockSpec((S, 1024), lambda kb, jq: (0, 0))
    outs = _hosted_call(
        body, grid=(nq, nq),
        in_specs=[qblk(0), kblk(1), kblk(2), pl.BlockSpec((GATE_LANES, T), lambda kb, jq: (0, kb)), qblk(0), qblk(0),
                  pl.BlockSpec((T, 128), lambda kb, jq: (qi(kb, jq), 0))],
        out_specs=[whole, kblk(0), kblk(0), pl.BlockSpec((GATE_LANES, T), lambda kb, jq: (0, kb)),
                   pl.BlockSpec((S, GATE_LANES), lambda kb, jq: (0, 0))],
        out_shape=[jax.ShapeDtypeStruct((S, 1024), BF)] * 3 + [jax.ShapeDtypeStruct((GATE_LANES, S), F32),
                                                               jax.ShapeDtypeStruct((S, GATE_LANES), F32)],
        scratch_shapes=[pltpu.VMEM((S, 1024), F32), pltpu.VMEM((N_HEADS // 2, 128, T), F32), pltpu.VMEM((N_HEADS // 2, 128, T), F32),
                        pltpu.VMEM((GATE_LANES, T), F32), pltpu.VMEM((S, GATE_LANES), F32)],
        args=(qkv, qkv, qkv, cT, do, o, lse), name=name, guest=guest)
    return outs if guest is None else (outs[:5], outs[5:])


def _to_natural(src_ref, buf, d, width):
    rows = buf.shape[1]
    for r in range(d):
        for ch in range(width // 128):
            lanes = slice(r * width + ch * 128, r * width + (ch + 1) * 128)
            buf.at[ch][pl.ds(r, rows // d, stride=d), :] = src_ref[:, lanes].astype(F32)
    return jnp.concatenate([buf[ch] for ch in range(width // 128)], axis=1)


def _to_view(val, buf, dst_ref, d, width):
    rows = buf.shape[1]
    for ch in range(width // 128):
        buf[ch] = val[:, ch * 128:(ch + 1) * 128]
    for r in range(d):
        for ch in range(width // 128):
            lanes = slice(r * width + ch * 128, r * width + (ch + 1) * 128)
            dst_ref[:, lanes] = buf.at[ch][pl.ds(r, rows // d, stride=d), :].astype(dst_ref.dtype)


def _view_spec(rows, d, width):
    return pl.BlockSpec((rows // d, d * width), lambda i, *_: (i, 0))


def _combine_groups(os, lses, dils, *, name):
    ng = len(os)
    S = os[0].shape[0] * dils[0]
    tm = ROW_TILE
    views = sorted(set(dils))

    def body(*refs):
        o_refs, l_refs = refs[:ng], refs[ng:2 * ng]
        outs = refs[2 * ng:2 * ng + 2 * len(views)]
        wide, narrow = refs[-2], refs[-1]
        ls = [l_refs[g][...] if dils[g] == 1 else _to_natural(l_refs[g], narrow, dils[g], 128) for g in range(ng)]
        m = functools.reduce(jnp.maximum, ls)
        es = [jnp.exp(l - m) for l in ls]
        den = functools.reduce(jnp.add, es)
        ws = [e / den for e in es]
        lse = m + jnp.log(den)
        og = [o_refs[g][...] if dils[g] == 1 else _to_natural(o_refs[g], wide, dils[g], 1024) for g in range(ng)]
        cols = []
        for h in range(N_HEADS):
            hs = slice(h * HEAD_DIM, (h + 1) * HEAD_DIM)
            acc = ws[0][:, h:h + 1] * og[0][:, hs]
            for g in range(1, ng):
                acc = acc + ws[g][:, h:h + 1] * og[g][:, hs]
            cols.append(acc)
        o = jnp.concatenate(cols, axis=1)
        for k, d in enumerate(views):
            if d == 1:
                outs[2 * k][...] = o.astype(BF)
                outs[2 * k + 1][...] = lse
            else:
                _to_view(o, wide, outs[2 * k], d, 1024)
                _to_view(lse, narrow, outs[2 * k + 1], d, 128)

    out_specs, out_shape = [], []
    for d in views:
        out_specs += [_view_spec(tm, d, 1024), _view_spec(tm, d, 128)]
        out_shape += [jax.ShapeDtypeStruct((S // d, d * 1024), BF), jax.ShapeDtypeStruct((S // d, d * 128), F32)]
    res = pl.pallas_call(
        body, grid=(S // tm,), in_specs=[_view_spec(tm, d, 1024) for d in dils] + [_view_spec(tm, d, 128) for d in dils],
        out_specs=out_specs, out_shape=out_shape,
        scratch_shapes=[pltpu.VMEM((8, tm, 128), F32), pltpu.VMEM((1, tm, 128), F32)],
        name=name, compiler_params=_params("parallel"),
    )(*os, *lses)
    return {d: (res[2 * k], res[2 * k + 1]) for k, d in enumerate(views)}


def _assemble(parts, rope_flags, rope, dils, *, name):
    n = len(parts)
    S = parts[0].shape[0] * dils[0]
    use_rope = any(rope_flags)
    tm = 256

    def body(*refs):
        out_ref, natural = refs[-2], refs[-1]
        for b in range(n):
            cols = slice(b * 1024, (b + 1) * 1024)
            d = dils[b]
            val = refs[b][...].astype(F32) if d == 1 else _to_natural(refs[b], natural, d, 1024)
            if rope_flags[b]:
                cos_ref, sa_ref, sb_ref = refs[n:n + 3]
                val = _rope_rotate(val, cos_ref[...], sa_ref[...], sb_ref[...])
            out_ref[:, cols] = val.astype(BF)

    in_specs = [_view_spec(tm, d, 1024) for d in dils]
    args = list(parts)
    if use_rope:
        in_specs += [pl.BlockSpec((tm, 128), lambda i: (i, 0))] * 3
        args += list(rope)
    return pl.pallas_call(
        body, grid=(S // tm,), in_specs=in_specs, out_specs=pl.BlockSpec((tm, n * 1024), lambda i: (i, 0)),
        out_shape=jax.ShapeDtypeStruct((S, n * 1024), BF), scratch_shapes=[pltpu.VMEM((8, tm, 128), F32)],
        name=name, compiler_params=_params("parallel"),
    )(*args)


GATE_ROWS = 512


def _gate_fwd(z, bf, *, name):
    S = z.shape[0]

    def body(z_ref, b_ref, c_ref, ct_ref, carry):
        i = pl.program_id(0)

        @pl.when(i == 0)
        def _():
            carry[...] = jnp.zeros_like(carry)

        zz = z_ref[...] + b_ref[...]
        logf = jnp.minimum(zz, 0.0) - jnp.log(1.0 + jnp.exp(-jnp.abs(zz)))
        tri = (lax.broadcasted_iota(jnp.int32, (GATE_ROWS, GATE_ROWS), 0)
               >= lax.broadcasted_iota(jnp.int32, (GATE_ROWS, GATE_ROWS), 1)).astype(F32)
        cs = jnp.dot(tri, logf, precision=lax.Precision.HIGHEST, preferred_element_type=F32) + carry[...]
        c_ref[...] = cs
        ct_ref[...] = cs.T
        carry[...] = cs[GATE_ROWS - 1:GATE_ROWS, :]

    return pl.pallas_call(
        body, grid=(S // GATE_ROWS,),
        in_specs=[pl.BlockSpec((GATE_ROWS, GATE_LANES), lambda i: (i, 0)), pl.BlockSpec((1, GATE_LANES), lambda i: (0, 0))],
        out_specs=[pl.BlockSpec((GATE_ROWS, GATE_LANES), lambda i: (i, 0)), pl.BlockSpec((GATE_LANES, GATE_ROWS), lambda i: (0, i))],
        out_shape=[jax.ShapeDtypeStruct((S, GATE_LANES), F32), jax.ShapeDtypeStruct((GATE_LANES, S), F32)],
        scratch_shapes=[pltpu.VMEM((1, GATE_LANES), F32)], name=name, compiler_params=_params("arbitrary"),
    )(z, bf)


def _gate_bwd(z, bf, dcT, dcq, *, name):
    S = z.shape[0]
    nb = S // GATE_ROWS

    def body(z_ref, b_ref, dct_ref, dcq_ref, dz_ref, db_ref, carry):
        i = pl.program_id(0)

        @pl.when(i == 0)
        def _():
            carry[...] = jnp.zeros_like(carry)
            db_ref[...] = jnp.zeros_like(db_ref)

        dc = dct_ref[...].T + dcq_ref[...]
        tri = (lax.broadcasted_iota(jnp.int32, (GATE_ROWS, GATE_ROWS), 0)
               <= lax.broadcasted_iota(jnp.int32, (GATE_ROWS, GATE_ROWS), 1)).astype(F32)
        dl = jnp.dot(tri, dc, precision=lax.Precision.HIGHEST, preferred_element_type=F32) + carry[...]
        carry[...] = dl[0:1, :]
        zz = z_ref[...] + b_ref[...]
        dz = dl * (1.0 / (1.0 + jnp.exp(zz)))
        lane = lax.broadcasted_iota(jnp.int32, dz.shape, 1)
        dz = jnp.where(lane < N_HEADS, dz, 0.0)
        dz_ref[...] = dz.astype(BF)
        db_ref[...] += jnp.sum(dz, axis=0, keepdims=True)

    return pl.pallas_call(
        body, grid=(nb,),
        in_specs=[pl.BlockSpec((GATE_ROWS, GATE_LANES), lambda i: (nb - 1 - i, 0)), pl.BlockSpec((1, GATE_LANES), lambda i: (0, 0)),
                  pl.BlockSpec((GATE_LANES, GATE_ROWS), lambda i: (0, nb - 1 - i)),
                  pl.BlockSpec((GATE_ROWS, GATE_LANES), lambda i: (nb - 1 - i, 0))],
        out_specs=[pl.BlockSpec((GATE_ROWS, GATE_LANES), lambda i: (nb - 1 - i, 0)), pl.BlockSpec((1, GATE_LANES), lambda i: (0, 0))],
        out_shape=[jax.ShapeDtypeStruct((S, GATE_LANES), BF), jax.ShapeDtypeStruct((1, GATE_LANES), F32)],
        scratch_shapes=[pltpu.VMEM((1, GATE_LANES), F32)], name=name, compiler_params=_params("arbitrary"),
    )(z, bf, dcT, dcq)


def _rope_tables(S):
    half = ROT_DIM // 2
    inv_freq = ROPE_THETA ** (-jnp.arange(half, dtype=F32) * 2.0 / ROT_DIM)
    ang = jnp.arange(S, dtype=F32)[:, None] * inv_freq[None, :]
    cos, sin = jnp.cos(ang), jnp.sin(ang)
    zero = jnp.zeros((S, HEAD_DIM - ROT_DIM), F32)
    zh = jnp.zeros((S, half), F32)
    cos_h = jnp.concatenate([cos, cos, jnp.ones_like(zero)], axis=1)
    sa_h = jnp.concatenate([-sin, zh, zero], axis=1)
    sb_h = jnp.concatenate([zh, sin, zero], axis=1)
    two = lambda t: jnp.concatenate([t, t], axis=1)
    return two(cos_h), two(sa_h), two(sb_h)


def _ffn_fwd(h, norm, w_gu, w_down, tag):
    n = _rms_fwd(h, norm, name=f"ffn{tag}_norm")
    gu = _mm_nn(n, w_gu, tm=1024, tn=512, out_dtype=BF, name=f"ffn{tag}_gu")
    act = _swiglu_fwd(gu, name=f"ffn{tag}_act")
    out = _mm_nn(act, w_down, tm=512, tn=1024, out_dtype=F32, name=f"ffn{tag}_down", resid=h)
    return out, (h, n, gu, act)


def _ffn_bwd(dh, dhb, saved, norm, w_gu, w_down, tag, ride=None):
    h, n, gu, act = saved
    dact = _mm_nt(dhb, w_down, tm=512, to=1408, tn=1024, out_dtype=BF, name=f"ffn{tag}_dact")
    dw_down = _mm_tn(act, dhb, tk=1408, tn=1024, tm=512, out_dtype=BF, name=f"ffn{tag}_dwdown")
    dgu = _swiglu_bwd(gu, dact, name=f"ffn{tag}_dgu")
    dn_call = lambda guest: _mm_nt(dgu, w_gu, tm=512, to=1024, tn=1408, out_dtype=F32, name=f"ffn{tag}_dn", guest=guest)
    dn = dn_call(None) if ride is None else ride(dn_call)
    dw_gu = _mm_tn(n, dgu, tk=1024, tn=1408, tm=512, out_dtype=BF, name=f"ffn{tag}_dwgu")
    dx, dxb, dg = _rms_bwd(h, norm, dn, dh, name=f"ffn{tag}_dnorm")
    return dx, dxb, dg, dw_gu, dw_down


def _local_step(x, tgt, w, mats, fetch, exchange):
    S = x.shape[0]
    rope_f = _rope_tables(S)
    rope_b = (rope_f[0], -rope_f[1], -rope_f[2])
    g, partial, landed = {}, {}, {}
    w = dict(w, ffn_w_gu={}, ffn_w_down={})

    def bring(call, indices):
        bufs = [mats[wi] for wi in indices]
        if fetch is None:
            return call(None), bufs
        return call(fetch(indices, bufs))

    def ride(call, indices):
        guest = exchange(indices, [partial[wi] for wi in indices]) if indices else None
        res = call(guest)
        if guest is None:
            return res
        res, outs = res
        landed.update(zip(indices, outs))
        return res

    n0 = _rms_fwd(x, w["a_norm"], name="a_norm")
    dils = [d for _, d in DILATED_PATTERNS]
    projs, (w["ffn_w_gu"][0], w["ffn_w_down"][0]) = bring(
        lambda guest: _mm_nn(n0, w["a_w_in"], tm=512, tn=1024, out_dtype=BF, name="a_proj", rope=rope_f, guest=guest,
                             groups=dils), [4, 6])
    block = lambda t, dil: (lambda r: t * dil + r)
    o_parts, lse_parts = [], []
    for gi, (window, dil) in enumerate(DILATED_PATTERNS):
        pv = projs[gi]
        attend = lambda guest: _band_fwd(pv, pv, pv, block(0, dil), block(1, dil), block(2, dil), dil=dil, T=128,
                                         window=window // dil, name=f"a_attn{gi}", guest=guest)
        if gi == 1:
            (o_g, lse_g), (b_in, w["b_w_out"]) = bring(attend, [2, 3])
        else:
            o_g, lse_g = attend(None)
        o_parts.append(o_g)
        lse_parts.append(lse_g)
    b_in = b_in.transpose(1, 0, 2).reshape(D_MODEL, -1)
    w["b_w_qkv"] = b_in[:, :QKV_COLS]
    w["b_w_f"] = jnp.pad(b_in[:, QKV_COLS:], ((0, 0), (0, GATE_LANES + QKV_COLS - b_in.shape[1])))
    mixed = _combine_groups(o_parts, lse_parts, dils, name="a_combine")
    o_a = mixed[1][0]
    h1 = _mm_nn(o_a, w["a_w_out"], tm=512, tn=1024, out_dtype=F32, name="a_out", resid=x)
    h2, ffn0 = _ffn_fwd(h1, w["ffn_norm"][0:1], w["ffn_w_gu"][0], w["ffn_w_down"][0], 0)

    n2 = _rms_fwd(h2, w["b_norm"], name="b_norm")
    qkv = _mm_nn(n2, w["b_w_qkv"], tm=512, tn=1024, out_dtype=BF, name="b_proj")
    zf = _mm_nn(n2, w["b_w_f"], tm=512, tn=GATE_LANES, out_dtype=F32, name="b_gate_proj")
    _, cT = _gate_fwd(zf, w["b_f"], name="b_gate")
    (o_b, lse_b), (w["ffn_w_gu"][1], w["ffn_w_down"][1]) = bring(lambda guest: _fox_fwd(qkv, cT, name="b_attn", guest=guest), [5, 7])
    h3 = _mm_nn(o_b, w["b_w_out"], tm=512, tn=1024, out_dtype=F32, name="b_out", resid=h2)
    h4, ffn1 = _ffn_fwd(h3, w["ffn_norm"][1:2], w["ffn_w_gu"][1], w["ffn_w_down"][1], 1)

    loss, dh4, dh4b, g["final_norm"] = _loss_head(h4, w["final_norm"], tgt, name="loss_head")

    dh3, dh3b, dg_f1, partial[5], partial[7] = _ffn_bwd(dh4, dh4b, ffn1, w["ffn_norm"][1:2], w["ffn_w_gu"][1], w["ffn_w_down"][1], 1)

    do_b = _mm_nt(dh3b, w["b_w_out"], tm=512, to=1024, tn=1024, out_dtype=BF, name="b_do")
    partial[3] = _mm_tn(o_b, dh3b, tk=1024, tn=1024, tm=512, out_dtype=BF, name="b_dwout")
    dq, dk, dv, dcT, dcq = ride(lambda guest: _fox_bwd(qkv, cT, do_b, o_b, lse_b, name="b_attn_bwd", guest=guest), [5, 7])
    dz, g["b_f"] = _gate_bwd(zf, w["b_f"], dcT, dcq, name="b_gate_bwd")
    dqkv = _assemble([dq, dk, dv], [False] * 3, None, [1] * 3, name="b_dproj")
    dn2 = _mm_nt(dz, w["b_w_f"], tm=512, to=1024, tn=GATE_LANES, out_dtype=F32, name="b_dn_gate")
    dn2 = _mm_nt(dqkv, w["b_w_qkv"], tm=512, to=1024, tn=1024, out_dtype=F32, name="b_dn", add=dn2)
    g_qkv = _mm_tn(n2, dqkv, tk=1024, tn=1024, tm=512, out_dtype=BF, name="b_dwqkv")
    g_f = _mm_tn(n2, dz, tk=1024, tn=GATE_LANES, tm=512, out_dtype=BF, name="b_dwf")
    g_b_in = jnp.concatenate([g_qkv, g_f[:, :N_HEADS]], axis=1)
    partial[2] = g_b_in.reshape(D_MODEL, N_CHIPS, -1).transpose(1, 0, 2)
    dh2, dh2b, g["b_norm"] = _rms_bwd(h2, w["b_norm"], dn2, dh3, name="b_dnorm")

    dh1, dh1b, dg_f0, partial[4], partial[6] = _ffn_bwd(dh2, dh2b, ffn0, w["ffn_norm"][0:1], w["ffn_w_gu"][0], w["ffn_w_down"][0], 0,
                                                      ride=lambda call: ride(call, [2, 3]))
    g["ffn_norm"] = jnp.concatenate([dg_f0, dg_f1], axis=0)

    views = tuple(sorted(set(dils)))
    do_a = dict(zip(views, _mm_nt(dh1b, w["a_w_out"], tm=512, to=1024, tn=1024, out_dtype=BF, name="a_do", views=views)))
    partial[1] = _mm_tn(o_a, dh1b, tk=1024, tn=1024, tm=512, out_dtype=BF, name="a_dwout")
    riders = {0: [4], 1: [6, 1], 2: []}
    parts = []
    for gi, (window, dil) in enumerate(DILATED_PATTERNS):
        pv = projs[gi]
        res = ride(lambda guest: _band_bwd(pv, pv, pv, block(0, dil), block(1, dil), block(2, dil), do_a[dil],
                                           mixed[dil][0], mixed[dil][1], dil=dil, T=128,
                                           window=window // dil, name=f"a_attn_bwd{gi}", guest=guest), riders[gi])
        parts += list(res)
    dproj = _assemble(parts, [True, True, False] * 3, rope_b, [d for _, d in DILATED_PATTERNS for _ in range(3)], name="a_dproj")
    partial[0] = _mm_tn(n0, dproj, tk=1024, tn=1024, tm=512, out_dtype=BF, name="a_dwin")
    dn0 = ride(lambda guest: _mm_nt(dproj, w["a_w_in"], tm=512, to=1024, tn=1024, out_dtype=F32, name="a_dn", guest=guest), [0])
    dx, _, g["a_norm"] = _rms_bwd(x, w["a_norm"], dn0, dh1, name="a_dnorm")
    return loss, dx, g, partial, landed


ANY = pl.BlockSpec(memory_space=pl.ANY)


def _place():
    x, y, c = lax.axis_index("x"), lax.axis_index("y"), lax.axis_index("c")
    chips = [(1 - x, y), (x, 1 - y), (1 - x, 1 - y)]
    return x, y, c, chips


def _shard_slice(ref, kind, rows, cols, s, half):
    hr = rows // 2
    if kind == "col":
        return ref.at[pl.ds(half * hr, hr), pl.ds(pl.multiple_of(s * cols, 128), cols)]
    if kind == "row":
        return ref.at[pl.ds(pl.multiple_of(s * rows + half * hr, 16), hr), :]
    return ref.at[s, pl.ds(half * hr, hr), :]


def _whole_shape(kind, rows, cols):
    return {"col": (rows, N_CHIPS * cols), "row": (N_CHIPS * rows, cols), "stack": (N_CHIPS, rows, cols)}[kind]


def _own_block(kind, rows, tr, cols):
    per = rows // tr

    def spec(half_rows):
        off = (lambda p: 0) if half_rows is None else (lambda p: p[1] * (half_rows // tr))
        if kind == "col":
            return pl.BlockSpec((tr, cols), lambda i, p: (off(p) + i, p[0]))
        if kind == "row":
            return pl.BlockSpec((tr, cols), lambda i, p: (p[0] * per + off(p) + i, 0))
        return pl.BlockSpec((None, tr, cols), lambda i, p: (p[0], off(p) + i, 0))
    return spec


def _place_shard(shard, kind, place, *, name):
    rows, cols = shard.shape
    tr = 256 if rows % 256 == 0 else rows // 2

    def body(p_ref, s_ref, o_ref):
        o_ref[...] = s_ref[...].astype(BF)

    return pl.pallas_call(
        body,
        grid_spec=pltpu.PrefetchScalarGridSpec(
            num_scalar_prefetch=1, grid=(rows // tr,),
            in_specs=[pl.BlockSpec((tr, cols), lambda i, p: (i, 0))],
            out_specs=_own_block(kind, rows, tr, cols)(None)),
        out_shape=jax.ShapeDtypeStruct(_whole_shape(kind, rows, cols), BF),
        name=name, compiler_params=_params("arbitrary"),
    )(place, shard)


def _gather_weights(placed, kinds, dims):
    nw = len(placed)

    def body(*refs):
        dst = refs[nw:2 * nw]
        send_sems, recv_sems = refs[2 * nw:]
        x, y, c, chips = _place()
        me = 2 * x + y
        sibling = (x, y, 1 - c)

        def copy(wi, k, s, half, to):
            p = _shard_slice(dst[wi], kinds[wi], dims[wi][0], dims[wi][1], s, half)
            return pltpu.make_async_remote_copy(src_ref=p, dst_ref=p, send_sem=send_sems.at[wi * 6 + k],
                                                recv_sem=recv_sems.at[wi * 6 + k], device_id=to, device_id_type=MESH)

        first, passed = [], []
        for wi in range(nw):
            for j, chip in enumerate(chips):
                cp = copy(wi, j, me, c, (*chip, c))
                cp.start()
                first.append(cp)
        for wi in range(nw):
            for j, chip in enumerate(chips):
                s = 2 * chip[0] + chip[1]
                copy(wi, j, s, c, (x, y, c)).wait_recv()
                cp = copy(wi, 3 + j, s, c, sibling)
                cp.start()
                passed.append(cp)
        for wi in range(nw):
            for j, chip in enumerate(chips):
                s = 2 * chip[0] + chip[1]
                copy(wi, 3 + j, s, 1 - c, (x, y, c)).wait_recv()
        for cp in first + passed:
            cp.wait_send()

    return pl.pallas_call(
        body, in_specs=[ANY] * nw, out_specs=[ANY] * nw,
        out_shape=[jax.ShapeDtypeStruct(p.shape, p.dtype) for p in placed],
        input_output_aliases={wi: wi for wi in range(nw)},
        scratch_shapes=[pltpu.SemaphoreType.DMA((nw * 6,)), pltpu.SemaphoreType.DMA((nw * 6,))],
        name="gather_weights",
    )(*placed)


def _fetch_guest(placed, kinds, dims):
    nw = len(placed)

    def copies(dst, send_sems, recv_sems, incoming):
        x, y, c, chips = _place()
        out = []
        for wi in range(nw):
            for j, chip in enumerate(chips):
                s = 2 * chip[0] + chip[1] if incoming else 2 * x + y
                to = (x, y, c) if incoming else (*chip, c)
                for half in range(2):
                    p = _shard_slice(dst[wi], kinds[wi], dims[wi][0], dims[wi][1], s, half)
                    k = wi * 6 + 2 * j + half
                    out.append(pltpu.make_async_remote_copy(src_ref=p, dst_ref=p, send_sem=send_sems.at[k],
                                                            recv_sem=recv_sems.at[k], device_id=to, device_id_type=MESH))
        return out

    def start(src, dst, sems):
        for cp in copies(dst, sems[0], sems[1], False):
            cp.start()

    def finish(src, dst, sems):
        for cp in copies(dst, sems[0], sems[1], True):
            cp.wait_recv()
        for cp in copies(dst, sems[0], sems[1], False):
            cp.wait_send()

    return dict(args=list(placed), out_shape=[jax.ShapeDtypeStruct(p.shape, p.dtype) for p in placed],
                scratch=[pltpu.SemaphoreType.DMA((nw * 6,)), pltpu.SemaphoreType.DMA((nw * 6,))],
                start=start, finish=finish, in_place=True)


def _scatter_guest(partials, kinds, dims):
    nw = len(partials)

    def copies(src, send_sems, recv_sems, dst):
        x, y, c, chips = _place()
        me = 2 * x + y
        out = []
        for wi in range(nw):
            rows, cols = dims[wi]

            def part(s, half, wi=wi, rows=rows, cols=cols):
                return _shard_slice(src[wi], kinds[wi], rows, cols, s, half)

            for j, chip in enumerate(chips):
                s = 2 * chip[0] + chip[1]
                for half in range(2):
                    slot = 2 * j + (c if half == 0 else 1 - c)
                    out.append(pltpu.make_async_remote_copy(
                        src_ref=part(s, half), dst_ref=dst[wi].at[slot],
                        send_sem=send_sems.at[wi * 7 + 2 * j + half], recv_sem=recv_sems.at[wi * 7 + slot],
                        device_id=(*chip, half), device_id_type=MESH))
            out.append(pltpu.make_async_remote_copy(
                src_ref=part(me, 1 - c), dst_ref=dst[wi].at[6],
                send_sem=send_sems.at[wi * 7 + 6], recv_sem=recv_sems.at[wi * 7 + 6],
                device_id=(x, y, 1 - c), device_id_type=MESH))
        return out

    def start(src, dst, sems):
        for cp in copies(src, sems[0], sems[1], dst):
            cp.start()

    def finish(src, dst, sems):
        x, y, c, _ = _place()
        for wi in range(nw):
            for slot in range(7):
                pltpu.make_async_remote_copy(
                    src_ref=dst[wi].at[slot], dst_ref=dst[wi].at[slot],
                    send_sem=sems[0].at[wi * 7 + slot], recv_sem=sems[1].at[wi * 7 + slot],
                    device_id=(x, y, c), device_id_type=MESH).wait_recv()
        for cp in copies(src, sems[0], sems[1], dst):
            cp.wait_send()

    return dict(args=list(partials), out_shape=[jax.ShapeDtypeStruct((7, d[0] // 2, d[1]), BF) for d in dims],
                scratch=[pltpu.SemaphoreType.DMA((nw * 7,)), pltpu.SemaphoreType.DMA((nw * 7,))],
                start=start, finish=finish)


def _scatter_grads(partials, kinds, dims, *, name):
    guest = _scatter_guest(partials, kinds, dims)
    nw = len(partials)

    def body(*refs):
        parts = (refs[:nw], refs[nw:2 * nw], refs[2 * nw:])
        guest["start"](*parts)
        guest["finish"](*parts)

    return pl.pallas_call(body, in_specs=[ANY] * nw, out_specs=[ANY] * nw, out_shape=guest["out_shape"],
                          scratch_shapes=guest["scratch"], name=name)(*partials)


def _sum_slots(slots, partial, kind, dims, place, *, name, into=None, layer=None, n_layers=1):
    rows, cols = dims
    hr = rows // 2
    tr = hr if 8 * hr * cols * 2 <= 6 * 1024 * 1024 else 128
    assert hr % tr == 0

    def body(p_ref, b_ref, own_ref, *rest):
        o_ref = rest[-1]
        acc = own_ref[...].astype(F32)
        for k in range(7):
            acc = acc + b_ref[k].astype(F32)
        o_ref[...] = acc

    half = lambda p: p[1] * (hr // tr)
    if n_layers == 1:
        out_spec = pl.BlockSpec((tr, cols), lambda i, p: (half(p) + i, 0))
        out_shape = jax.ShapeDtypeStruct((rows, cols), F32)
    else:
        out_spec = pl.BlockSpec((None, tr, cols), lambda i, p: (layer, half(p) + i, 0))
        out_shape = jax.ShapeDtypeStruct((n_layers, rows, cols), F32)
    in_specs = [pl.BlockSpec((7, tr, cols), lambda i, p: (0, i, 0)), _own_block(kind, rows, tr, cols)(hr)]
    args = [place, slots, partial]
    aliases = {}
    if into is not None:
        in_specs.append(ANY)
        args.append(into)
        aliases = {3: 0}
    return pl.pallas_call(
        body,
        grid_spec=pltpu.PrefetchScalarGridSpec(num_scalar_prefetch=1, grid=(hr // tr,), in_specs=in_specs, out_specs=out_spec),
        out_shape=out_shape, input_output_aliases=aliases, name=name, compiler_params=_params("arbitrary"),
    )(*args)


def _pair_exchange(bufs, members):
    nw = len(members)

    def body(*refs):
        dst = refs[len(bufs):2 * len(bufs)]
        send_sems, recv_sems = refs[2 * len(bufs):]
        x, y, c, _ = _place()

        def rows_of(wi, half):
            bi, l = members[wi]
            ref = dst[bi] if l is None else dst[bi].at[l]
            hr = ref.shape[0] // 2
            return ref.at[pl.ds(pl.multiple_of(half * hr, 8), hr), :]

        def copy(wi, half, to):
            p = rows_of(wi, half)
            return pltpu.make_async_remote_copy(src_ref=p, dst_ref=p, send_sem=send_sems.at[wi], recv_sem=recv_sems.at[wi],
                                                device_id=to, device_id_type=MESH)

        sent = []
        for wi in range(nw):
            cp = copy(wi, c, (x, y, 1 - c))
            cp.start()
            sent.append(cp)
        for wi in range(nw):
            copy(wi, 1 - c, (x, y, c)).wait_recv()
        for cp in sent:
            cp.wait_send()

    return pl.pallas_call(
        body, in_specs=[ANY] * len(bufs), out_specs=[ANY] * len(bufs),
        out_shape=[jax.ShapeDtypeStruct(b.shape, b.dtype) for b in bufs],
        input_output_aliases={i: i for i in range(len(bufs))},
        scratch_shapes=[pltpu.SemaphoreType.DMA((nw,)), pltpu.SemaphoreType.DMA((nw,))],
        name="pair_exchange",
    )(*bufs)


SMALL_ROWS = 8


def _allreduce_small(v, *, name):
    assert v.shape == (SMALL_ROWS, D_MODEL)

    def body(v_ref, o_ref, buf, send_sems, recv_sems):
        x, y, c, _ = _place()
        me = 4 * x + 2 * y + c
        buf[me] = v_ref[...]
        sent = []
        for k in range(1, 8):
            bx, by, bc = (k >> 2) & 1, (k >> 1) & 1, k & 1
            peer = (1 - x if bx else x, 1 - y if by else y, 1 - c if bc else c)
            cp = pltpu.make_async_remote_copy(src_ref=v_ref, dst_ref=buf.at[me], send_sem=send_sems.at[k - 1],
                                              recv_sem=recv_sems.at[k - 1], device_id=peer, device_id_type=MESH)
            cp.start()
            sent.append(cp)
        for k in range(1, 8):
            bx, by, bc = (k >> 2) & 1, (k >> 1) & 1, k & 1
            peer = 4 * (1 - x if bx else x) + 2 * (1 - y if by else y) + (1 - c if bc else c)
            pltpu.make_async_remote_copy(src_ref=v_ref, dst_ref=buf.at[peer], send_sem=send_sems.at[k - 1],
                                         recv_sem=recv_sems.at[k - 1], device_id=(x, y, c), device_id_type=MESH).wait_recv()
        for cp in sent:
            cp.wait_send()
        acc = buf[0]
        for d in range(1, 8):
            acc = acc + buf[d]
        o_ref[...] = acc

    vmem = pl.BlockSpec(memory_space=pltpu.VMEM)
    return pl.pallas_call(
        body, in_specs=[vmem], out_specs=vmem, out_shape=jax.ShapeDtypeStruct(v.shape, F32),
        scratch_shapes=[pltpu.VMEM((8,) + v.shape, F32), pltpu.SemaphoreType.DMA((7,)), pltpu.SemaphoreType.DMA((7,))],
        name=name,
    )(v)


def _adamw(w, g, m, v, *, name):
    R, C = w.shape
    tr = R
    if R * C * 4 > 1024 * 1024:
        tr = max(t for t in range(8, R, 8) if R % t == 0 and t * C * 4 <= 1024 * 1024)

    def body(w_ref, g_ref, m_ref, v_ref, d_ref, m2_ref, v2_ref):
        gg = g_ref[...]
        m2 = ADAM_B1 * m_ref[...] + (1.0 - ADAM_B1) * gg
        v2 = ADAM_B2 * v_ref[...] + (1.0 - ADAM_B2) * jnp.square(gg)
        m_hat = m2 / (1.0 - ADAM_B1 ** ADAM_STEP)
        v_hat = v2 / (1.0 - ADAM_B2 ** ADAM_STEP)
        d_ref[...] = -ADAM_LR * (m_hat / (jnp.sqrt(v_hat) + ADAM_EPS) + ADAM_WD * w_ref[...])
        m2_ref[...] = m2
        v2_ref[...] = v2

    blk = pl.BlockSpec((tr, C), lambda i: (i, 0))
    out = jax.ShapeDtypeStruct((R, C), F32)
    return pl.pallas_call(
        body, grid=(R // tr,), in_specs=[blk] * 4, out_specs=[blk] * 3, out_shape=[out] * 3,
        name=name, compiler_params=_params("parallel"),
    )(w, g, m, v)


WEIGHT_ORDER = ("a_norm", "a_w_in", "a_w_out", "b_norm", "b_w_in", "b_f", "b_w_out", "ffn_norm", "ffn_w_gu",
                "ffn_w_down", "final_norm")
MATRICES = (("a_w_in", 0, "col"), ("a_w_out", 0, "row"), ("b_w_in", 0, "stack"), ("b_w_out", 0, "row"),
            ("ffn_w_gu", 0, "col"), ("ffn_w_gu", 1, "col"), ("ffn_w_down", 0, "row"), ("ffn_w_down", 1, "row"))
MATRIX_GROUPS = ([0], [1], [2], [3], [4, 5], [6, 7])
GROUP_NAMES = ("a_w_in", "a_w_out", "b_w_in", "b_w_out", "ffn_w_gu", "ffn_w_down")
QKV_COLS = 3 * N_HEADS * HEAD_DIM


def kernel(x, a_norm, a_w_in, a_w_out, b_norm, b_w_in, b_f, b_w_out, ffn_norm, ffn_w_gu, ffn_w_down, final_norm, loss_target, m_a_norm, m_a_w_in, m_a_w_out, m_b_norm, m_b_w_in, m_b_f, m_b_w_out, m_ffn_norm, m_ffn_w_gu, m_ffn_w_down, m_final_norm, v_a_norm, v_a_w_in, v_a_w_out, v_b_norm, v_b_w_in, v_b_f, v_b_w_out, v_ffn_norm, v_ffn_w_gu, v_ffn_w_down, v_final_norm):
    given = dict(a_norm=a_norm, a_w_in=a_w_in, a_w_out=a_w_out, b_norm=b_norm, b_w_in=b_w_in, b_f=b_f, b_w_out=b_w_out,
                 ffn_norm=ffn_norm, ffn_w_gu=ffn_w_gu, ffn_w_down=ffn_w_down, final_norm=final_norm)
    mom_m = dict(a_norm=m_a_norm, a_w_in=m_a_w_in, a_w_out=m_a_w_out, b_norm=m_b_norm, b_w_in=m_b_w_in, b_f=m_b_f,
                 b_w_out=m_b_w_out, ffn_norm=m_ffn_norm, ffn_w_gu=m_ffn_w_gu, ffn_w_down=m_ffn_w_down, final_norm=m_final_norm)
    mom_v = dict(a_norm=v_a_norm, a_w_in=v_a_w_in, a_w_out=v_a_w_out, b_norm=v_b_norm, b_w_in=v_b_w_in, b_f=v_b_f,
                 b_w_out=v_b_w_out, ffn_norm=v_ffn_norm, ffn_w_gu=v_ffn_w_gu, ffn_w_down=v_ffn_w_down, final_norm=v_final_norm)
    chip = 2 * lax.axis_index("x") + lax.axis_index("y")
    core = lax.axis_index("c")
    bn_cols = b_norm.shape[1]

    placed = lax.dynamic_update_slice(jnp.zeros((SMALL_ROWS, D_MODEL), F32), b_norm, (0, chip * bn_cols))
    placed = placed * (core == 0).astype(F32)
    b_norm_full = _allreduce_small(placed, name="gather_b_norm")[0:1]

    place = jnp.stack([chip, core]).astype(jnp.int32)
    kinds = [k for _, _, k in MATRICES]
    dims = [given[n][l].shape for n, l, _ in MATRICES]
    placed = [_place_shard(given[n][l], k, place, name=f"place_{n}{l}") for n, l, k in MATRICES]
    first = _gather_weights(placed[:2], kinds[:2], dims[:2])
    mats = dict(enumerate(list(first) + placed[2:]))
    gate_cols = b_f.shape[1]
    w = dict(a_norm=a_norm, a_w_in=mats[0], a_w_out=mats[1], b_norm=b_norm_full,
             b_f=jnp.pad(b_f, ((0, 0), (0, GATE_LANES - gate_cols))), ffn_norm=ffn_norm,
             final_norm=final_norm.reshape(1, D_MODEL))

    def fetch(indices, bufs):
        return _fetch_guest(bufs, [kinds[i] for i in indices], [dims[i] for i in indices])

    def exchange(indices, parts):
        return _scatter_guest(parts, [kinds[i] for i in indices], [dims[i] for i in indices])

    loss, dx, g, partials, slots = _local_step(x[0], loss_target[0], w, mats, fetch, exchange)
    bufs, members = [], []
    for group in MATRIX_GROUPS:
        buf = None
        for l, wi in enumerate(group):
            n = MATRICES[wi][0]
            buf = _sum_slots(slots[wi], partials[wi], kinds[wi], dims[wi], place, name=f"sum_{n}{l}", into=buf,
                             layer=l, n_layers=len(group))
            members.append((len(bufs), l if len(group) > 1 else None))
        bufs.append(buf)
    reduced = dict(zip(GROUP_NAMES, _pair_exchange(bufs, members)))

    small = jnp.concatenate([g["a_norm"], g["b_norm"], g["ffn_norm"], g["final_norm"],
                             jnp.pad(g["b_f"], ((0, 0), (0, D_MODEL - GATE_LANES))),
                             jnp.zeros((SMALL_ROWS - 6, D_MODEL), F32)], axis=0)
    small = _allreduce_small(small, name="allreduce_small")
    grads = dict(reduced)
    grads["a_norm"] = small[0:1]
    grads["b_norm"] = lax.dynamic_slice(small, (1, chip * bn_cols), (1, bn_cols))
    grads["ffn_norm"] = small[2:4]
    grads["final_norm"] = small[4]
    grads["b_f"] = small[5:6, :gate_cols]

    out_g, out_d, out_m, out_v = [], [], [], []
    for n in WEIGHT_ORDER:
        shape = given[n].shape
        two_d = (1, shape[0]) if len(shape) == 1 else (-1, shape[-1])
        d, m2, v2 = _adamw(given[n].reshape(two_d), grads[n].reshape(two_d), mom_m[n].reshape(two_d),
                           mom_v[n].reshape(two_d), name=f"adamw_{n}")
        out_g.append(grads[n].reshape(shape))
        out_d.append(d.reshape(shape))
        out_m.append(m2.reshape(shape))
        out_v.append(v2.reshape(shape))

    total = lax.psum(loss[0, 0], MESH_AXES)
    return (total, dx[None], *out_g, *out_d, *out_m, *out_v)
```

```python
import functools

import jax
import jax.numpy as jnp
from jax import lax
from jax.experimental import pallas as pl
from jax.experimental.pallas import tpu as pltpu

F32 = jnp.float32
BF = jnp.bfloat16

D_MODEL = 1024
N_HEADS = 16
HEAD_DIM = 64
D_FF = 2816
DILATED_PATTERNS = ((128, 1), (512, 4), (2048, 16))
ROT_DIM = 16
ROPE_THETA = 500000.0
RMS_EPS = 1e-6
NEG_INF = -1e30
ATTN_SCALE = HEAD_DIM ** -0.5
GATE_LANES = 128
N_CHIPS = 4
MESH_AXES = ("x", "y", "c")
MESH = pl.DeviceIdType.MESH

ADAM_LR = 0.001
ADAM_B1 = 0.9
ADAM_B2 = 0.999
ADAM_EPS = 1e-08
ADAM_WD = 0.01
ADAM_STEP = 10

VMEM_LIMIT_BYTES = 56 * 1024 * 1024


def _params(*sem):
    return pltpu.CompilerParams(dimension_semantics=sem, vmem_limit_bytes=VMEM_LIMIT_BYTES)


def _hosted_call(body, *, grid, in_specs, out_specs, out_shape, scratch_shapes, args, name, guest=None, schedule=()):
    params = _params(*(["arbitrary"] * len(grid)))
    ns = len(schedule)

    def call(kernel, in_specs, out_specs, out_shape, scratch_shapes, aliases, args):
        spec = pltpu.PrefetchScalarGridSpec(num_scalar_prefetch=ns, grid=grid, in_specs=in_specs, out_specs=out_specs,
                                            scratch_shapes=scratch_shapes)
        return pl.pallas_call(kernel, grid_spec=spec, out_shape=out_shape, input_output_aliases=aliases, name=name,
                              compiler_params=params)(*schedule, *args)

    if guest is None:
        return call(body, in_specs, out_specs, out_shape, scratch_shapes, {}, args)
    n_in, n_out, n_scr = ns + len(in_specs), len(out_specs), len(scratch_shapes)
    g_in, g_out = len(guest["args"]), len(guest["out_shape"])
    any_spec = pl.BlockSpec(memory_space=pl.ANY)

    def wrapped(*refs):
        i1 = n_in + g_in
        o1 = i1 + n_out
        o2 = o1 + g_out
        s1 = o2 + n_scr
        guest_refs = (refs[n_in:i1], refs[o1:o2], refs[s1:])
        ids = [pl.program_id(d) for d in range(len(grid))]
        first = functools.reduce(jnp.logical_and, [i == 0 for i in ids])
        last = functools.reduce(jnp.logical_and, [i == g - 1 for i, g in zip(ids, grid)])

        @pl.when(first)
        def _():
            guest["start"](*guest_refs)

        body(*refs[:n_in], *refs[i1:o1], *refs[o2:s1])

        @pl.when(last)
        def _():
            guest["finish"](*guest_refs)

    aliases = {n_in + k: n_out + k for k in range(g_in)} if guest.get("in_place") else {}
    return call(wrapped, list(in_specs) + [any_spec] * g_in, list(out_specs) + [any_spec] * g_out,
                list(out_shape) + list(guest["out_shape"]), list(scratch_shapes) + list(guest["scratch"]), aliases,
                list(args) + list(guest["args"]))


def _rope_rotate(t, cos, sin_a, sin_b):
    outs = []
    for cidx in range(t.shape[1] // 128):
        tc = t[:, cidx * 128:(cidx + 1) * 128]
        outs.append(tc * cos + pltpu.roll(tc, 120, 1) * sin_a + pltpu.roll(tc, 8, 1) * sin_b)
    return jnp.concatenate(outs, axis=1)


def _mm_nn(a, b, *, tm, tn, out_dtype, name, resid=None, rope=None, guest=None, groups=None):
    M, K = a.shape
    N = b.shape[1]
    assert M % tm == 0 and N % tn == 0 and b.shape[0] == K
    n_in = 2 + (resid is not None) + (3 if rope is not None else 0)
    if groups is not None:
        assert rope is not None and N == 3 * tn * len(groups)

    def body(*refs):
        a_ref, b_ref = refs[0], refs[1]
        o_ref = refs[n_in]
        acc = jnp.dot(a_ref[...], b_ref[...], preferred_element_type=F32)
        if resid is not None:
            acc = acc + refs[2][...]
        if groups is not None:
            cos_ref, sa_ref, sb_ref = refs[n_in - 3:n_in]
            j = pl.program_id(1)
            for g, d in enumerate(groups):
                for is_v in (False, True):
                    @pl.when(jnp.logical_and(j // 3 == g, (j % 3 == 2) == is_v))
                    def _(g=g, d=d, is_v=is_v):
                        val = acc if is_v else _rope_rotate(acc, cos_ref[...], sa_ref[...], sb_ref[...])
                        if d == 1:
                            refs[n_in + g][...] = val.astype(out_dtype)
                        else:
                            _to_view(val, refs[-1], refs[n_in + g], d, tn)
        elif rope is not None:
            cos_ref, sa_ref, sb_ref = refs[n_in - 3:n_in]
            j = pl.program_id(1)

            @pl.when(j % 3 != 2)
            def _():
                o_ref[...] = _rope_rotate(acc, cos_ref[...], sa_ref[...], sb_ref[...]).astype(out_dtype)

            @pl.when(j % 3 == 2)
            def _():
                o_ref[...] = acc.astype(out_dtype)
        else:
            o_ref[...] = acc.astype(out_dtype)

    in_specs = [pl.BlockSpec((tm, K), lambda i, j: (i, 0)), pl.BlockSpec((K, tn), lambda i, j: (0, j))]
    args = [a, b]
    if resid is not None:
        in_specs.append(pl.BlockSpec((tm, tn), lambda i, j: (i, j)))
        args.append(resid)
    if rope is not None:
        assert tn == 1024
        for t in rope:
            in_specs.append(pl.BlockSpec((tm, 128), lambda i, j: (i, 0)))
            args.append(t)
    if groups is None:
        out_specs = [pl.BlockSpec((tm, tn), lambda i, j: (i, j))]
        out_shape = [jax.ShapeDtypeStruct((M, N), out_dtype)]
        scratch = []
    else:
        out_specs = [pl.BlockSpec((tm // d, d * tn), lambda i, j, g=g: (i, jnp.clip(j - 3 * g, 0, 2)))
                     for g, d in enumerate(groups)]
        out_shape = [jax.ShapeDtypeStruct((M // d, d * 3 * tn), out_dtype) for d in groups]
        scratch = [pltpu.VMEM((tn // 128, tm, 128), F32)]
    outs = _hosted_call(body, grid=(M // tm, N // tn), in_specs=in_specs, out_specs=out_specs, out_shape=out_shape,
                        scratch_shapes=scratch, args=args, name=name, guest=guest)
    nout = len(out_shape)
    res = outs[0] if groups is None else list(outs[:nout])
    return res if guest is None else (res, outs[nout:])


def _mm_nt(a, b, *, tm, to, tn, out_dtype, name, add=None, guest=None, views=(1,)):
    M, N = a.shape
    O = b.shape[0]
    assert M % tm == 0 and O % to == 0 and N % tn == 0 and b.shape[1] == N
    nk = N // tn

    def body(*refs):
        a_ref, b_ref = refs[0], refs[1]
        n_in = 2 + (add is not None)
        o_refs = refs[n_in:n_in + len(views)]
        acc_ref = refs[n_in + len(views)]
        k = pl.program_id(2)

        @pl.when(k == 0)
        def _():
            if add is not None:
                acc_ref[...] = refs[2][...]
            else:
                acc_ref[...] = jnp.zeros_like(acc_ref)

        acc_ref[...] += lax.dot_general(a_ref[...], b_ref[...], (((1,), (1,)), ((), ())),
                                        preferred_element_type=F32)

        @pl.when(k == nk - 1)
        def _():
            for o_ref, d in zip(o_refs, views):
                if d == 1:
                    o_ref[...] = acc_ref[...].astype(out_dtype)
                else:
                    _to_view(acc_ref[...], refs[-1], o_ref, d, to)

    in_specs = [pl.BlockSpec((tm, tn), lambda i, j, k: (i, k)), pl.BlockSpec((to, tn), lambda i, j, k: (j, k))]
    args = [a, b]
    if add is not None:
        in_specs.append(pl.BlockSpec((tm, to), lambda i, j, k: (i, j)))
        args.append(add)
    assert views == (1,) or (to == O and to % 128 == 0)
    scratch = [pltpu.VMEM((tm, to), F32)] + ([pltpu.VMEM((to // 128, tm, 128), F32)] if views != (1,) else [])
    outs = _hosted_call(
        body, grid=(M // tm, O // to, nk), in_specs=in_specs,
        out_specs=[pl.BlockSpec((tm, to), lambda i, j, k: (i, j)) if d == 1 else _view_spec(tm, d, to) for d in views],
        out_shape=[jax.ShapeDtypeStruct((M // d, d * O), out_dtype) for d in views],
        scratch_shapes=scratch, args=args, name=name, guest=guest)
    nv = len(views)
    res = outs[0] if nv == 1 else list(outs[:nv])
    return res if guest is None else (res, outs[nv:])


def _mm_tn(a, b, *, tk, tn, tm, out_dtype, name):
    M, K = a.shape
    N = b.shape[1]
    assert M % tm == 0 and K % tk == 0 and N % tn == 0 and b.shape[0] == M
    nm = M // tm

    def body(a_ref, b_ref, o_ref, acc_ref):
        m = pl.program_id(2)

        @pl.when(m == 0)
        def _():
            acc_ref[...] = jnp.zeros_like(acc_ref)

        acc_ref[...] += lax.dot_general(a_ref[...], b_ref[...], (((0,), (0,)), ((), ())),
                                        preferred_element_type=F32)

        @pl.when(m == nm - 1)
        def _():
            o_ref[...] = acc_ref[...].astype(out_dtype)

    return pl.pallas_call(
        body, grid=(K // tk, N // tn, nm),
        in_specs=[pl.BlockSpec((tm, tk), lambda i, j, m: (m, i)), pl.BlockSpec((tm, tn), lambda i, j, m: (m, j))],
        out_specs=pl.BlockSpec((tk, tn), lambda i, j, m: (i, j)),
        out_shape=jax.ShapeDtypeStruct((K, N), out_dtype),
        scratch_shapes=[pltpu.VMEM((tk, tn), F32)], name=name,
        compiler_params=_params("parallel", "parallel", "arbitrary"),
    )(a, b)


ROW_TILE = 512


def _rms_fwd(x, g, *, name):
    S, Dm = x.shape

    def body(x_ref, g_ref, o_ref):
        xf = x_ref[...]
        r = lax.rsqrt(jnp.mean(xf * xf, axis=-1, keepdims=True) + RMS_EPS)
        o_ref[...] = (xf * r * g_ref[...]).astype(BF)

    return pl.pallas_call(
        body, grid=(S // ROW_TILE,),
        in_specs=[pl.BlockSpec((ROW_TILE, Dm), lambda i: (i, 0)), pl.BlockSpec((1, Dm), lambda i: (0, 0))],
        out_specs=pl.BlockSpec((ROW_TILE, Dm), lambda i: (i, 0)),
        out_shape=jax.ShapeDtypeStruct((S, Dm), BF), name=name, compiler_params=_params("parallel"),
    )(x, g)


def _rms_bwd(x, g, dn, dres, *, name):
    S, Dm = x.shape

    def body(x_ref, g_ref, dn_ref, dres_ref, dx_ref, dxb_ref, dg_ref):
        i = pl.program_id(0)
        xf = x_ref[...]
        r = lax.rsqrt(jnp.mean(xf * xf, axis=-1, keepdims=True) + RMS_EPS)
        xh = xf * r
        dnf = dn_ref[...]
        dyg = dnf * g_ref[...]
        dx = dres_ref[...] + r * (dyg - xh * jnp.mean(dyg * xh, axis=-1, keepdims=True))
        dx_ref[...] = dx
        dxb_ref[...] = dx.astype(BF)

        @pl.when(i == 0)
        def _():
            dg_ref[...] = jnp.zeros_like(dg_ref)

        dg_ref[...] += jnp.sum(dnf * xh, axis=0, keepdims=True)

    row = pl.BlockSpec((ROW_TILE, Dm), lambda i: (i, 0))
    vec = pl.BlockSpec((1, Dm), lambda i: (0, 0))
    return pl.pallas_call(
        body, grid=(S // ROW_TILE,), in_specs=[row, vec, row, row], out_specs=[row, row, vec],
        out_shape=[jax.ShapeDtypeStruct((S, Dm), F32), jax.ShapeDtypeStruct((S, Dm), BF),
                   jax.ShapeDtypeStruct((1, Dm), F32)],
        name=name, compiler_params=_params("arbitrary"),
    )(x, g, dn, dres)


def _loss_head(h, g, tgt, *, name):
    S, Dm = h.shape

    def body(h_ref, g_ref, t_ref, loss_ref, dh_ref, dhb_ref, dg_ref):
        i = pl.program_id(0)
        xf = h_ref[...]
        r = lax.rsqrt(jnp.mean(xf * xf, axis=-1, keepdims=True) + RMS_EPS)
        xh = xf * r
        gv = g_ref[...]
        err = xh * gv - t_ref[...]
        dy = err * (1.0 / Dm)
        dyg = dy * gv
        dh = r * (dyg - xh * jnp.mean(dyg * xh, axis=-1, keepdims=True))
        dh_ref[...] = dh
        dhb_ref[...] = dh.astype(BF)

        @pl.when(i == 0)
        def _():
            dg_ref[...] = jnp.zeros_like(dg_ref)
            loss_ref[...] = jnp.zeros_like(loss_ref)

        dg_ref[...] += jnp.sum(dy * xh, axis=0, keepdims=True)
        part = 0.5 * jnp.sum(jnp.mean(err * err, axis=-1, keepdims=True), axis=0, keepdims=True)
        loss_ref[...] += jnp.broadcast_to(part, loss_ref.shape)

    row = pl.BlockSpec((ROW_TILE, Dm), lambda i: (i, 0))
    vec = pl.BlockSpec((1, Dm), lambda i: (0, 0))
    return pl.pallas_call(
        body, grid=(S // ROW_TILE,), in_specs=[row, vec, row],
        out_specs=[pl.BlockSpec((1, 128), lambda i: (0, 0)), row, row, vec],
        out_shape=[jax.ShapeDtypeStruct((1, 128), F32), jax.ShapeDtypeStruct((S, Dm), F32),
                   jax.ShapeDtypeStruct((S, Dm), BF), jax.ShapeDtypeStruct((1, Dm), F32)],
        name=name, compiler_params=_params("arbitrary"),
    )(h, g, tgt)


SWIGLU_ROWS = 256


def _swiglu_fwd(gu, *, name):
    S = gu.shape[0]

    def body(g_ref, u_ref, o_ref):
        g = g_ref[...].astype(F32)
        sig = 1.0 / (1.0 + jnp.exp(-g))
        o_ref[...] = (g * sig * u_ref[...].astype(F32)).astype(BF)

    return pl.pallas_call(
        body, grid=(S // SWIGLU_ROWS,),
        in_specs=[pl.BlockSpec((SWIGLU_ROWS, D_FF), lambda i: (i, 0)), pl.BlockSpec((SWIGLU_ROWS, D_FF), lambda i: (i, 1))],
        out_specs=pl.BlockSpec((SWIGLU_ROWS, D_FF), lambda i: (i, 0)),
        out_shape=jax.ShapeDtypeStruct((S, D_FF), BF), name=name, compiler_params=_params("parallel"),
    )(gu, gu)


def _swiglu_bwd(gu, dact, *, name):
    S = gu.shape[0]

    def body(g_ref, u_ref, d_ref, o_ref):
        g = g_ref[...].astype(F32)
        u = u_ref[...].astype(F32)
        d = d_ref[...].astype(F32)
        sig = 1.0 / (1.0 + jnp.exp(-g))
        o_ref[:, :D_FF] = (d * u * sig * (1.0 + g * (1.0 - sig))).astype(BF)
        o_ref[:, D_FF:] = (d * g * sig).astype(BF)

    return pl.pallas_call(
        body, grid=(S // SWIGLU_ROWS,),
        in_specs=[pl.BlockSpec((SWIGLU_ROWS, D_FF), lambda i: (i, 0)), pl.BlockSpec((SWIGLU_ROWS, D_FF), lambda i: (i, 1)),
                  pl.BlockSpec((SWIGLU_ROWS, D_FF), lambda i: (i, 0))],
        out_specs=pl.BlockSpec((SWIGLU_ROWS, 2 * D_FF), lambda i: (i, 0)),
        out_shape=jax.ShapeDtypeStruct((S, 2 * D_FF), BF), name=name, compiler_params=_params("parallel"),
    )(gu, gu, dact)


def _attn_fwd(qa, ka, va, qcb, kcb, vcb, *, dil, T, nkv, window, name, c=None, cT=None, o_dtype=F32):
    L = qa.shape[0]
    nq = L // T
    fox = c is not None

    def kv_block(n, j):
        return n - (nkv - 1) + j

    def body(*refs):
        if fox:
            q_ref, k_ref, v_ref, c_ref, ct_ref, o_ref, lse_ref, m_sc, l_sc, acc_sc = refs
        else:
            q_ref, k_ref, v_ref, o_ref, lse_ref, m_sc, l_sc, acc_sc = refs
        n = pl.program_id(1)
        j = pl.program_id(2)
        kb = kv_block(n, j)

        @pl.when(j == 0)
        def _():
            m_sc[...] = jnp.full(m_sc.shape, NEG_INF, F32)
            l_sc[...] = jnp.zeros_like(l_sc)
            acc_sc[...] = jnp.zeros_like(acc_sc)

        @pl.when(kb >= 0)
        def _():
            diff = (n * T + lax.broadcasted_iota(jnp.int32, (T, T), 0)) - (kb * T + lax.broadcasted_iota(jnp.int32, (T, T), 1))
            valid = diff >= 0
            if window is not None:
                valid = jnp.logical_and(valid, diff <= window)
            for h in range(N_HEADS):
                hs = slice(h * HEAD_DIM, (h + 1) * HEAD_DIM)
                qh = q_ref[:, hs] * jnp.asarray(ATTN_SCALE, BF)
                s = lax.dot_general(qh, k_ref[:, hs], (((1,), (1,)), ((), ())), preferred_element_type=F32)
                if fox:
                    s = s + c_ref[:, h:h + 1] - ct_ref[h:h + 1, :]
                s = jnp.where(valid, s, NEG_INF)
                m_prev = m_sc[:, h:h + 1]
                m_new = jnp.maximum(m_prev, jnp.max(s, axis=1, keepdims=True))
                alpha = jnp.exp(m_prev - m_new)
                p = jnp.exp(s - m_new)
                l_sc[:, h:h + 1] = alpha * l_sc[:, h:h + 1] + jnp.sum(p, axis=1, keepdims=True)
                acc_sc[:, hs] = alpha * acc_sc[:, hs] + jnp.dot(p.astype(BF), v_ref[:, hs], preferred_element_type=F32)
                m_sc[:, h:h + 1] = m_new

        @pl.when(j == nkv - 1)
        def _():
            lane = lax.broadcasted_iota(jnp.int32, (T, 128), 1)
            lse = jnp.zeros((T, 128), F32)
            for h in range(N_HEADS):
                hs = slice(h * HEAD_DIM, (h + 1) * HEAD_DIM)
                l = l_sc[:, h:h + 1]
                o_ref[:, hs] = (acc_sc[:, hs] / l).astype(o_dtype)
                lse = jnp.where(lane == h, m_sc[:, h:h + 1] + jnp.log(l), lse)
            lse_ref[...] = lse

    def kvi(n, j):
        return jnp.maximum(kv_block(n, j), 0)

    in_specs = [pl.BlockSpec((T, 1024), lambda r, n, j: (n, qcb(r))),
                pl.BlockSpec((T, 1024), lambda r, n, j: (kvi(n, j), kcb(r))),
                pl.BlockSpec((T, 1024), lambda r, n, j: (kvi(n, j), vcb(r)))]
    args = [qa, ka, va]
    if fox:
        in_specs += [pl.BlockSpec((T, GATE_LANES), lambda r, n, j: (n, 0)),
                     pl.BlockSpec((GATE_LANES, T), lambda r, n, j: (0, kvi(n, j)))]
        args += [c, cT]
    return pl.pallas_call(
        body, grid=(dil, nq, nkv), in_specs=in_specs,
        out_specs=[pl.BlockSpec((T, 1024), lambda r, n, j: (n, r)), pl.BlockSpec((T, 128), lambda r, n, j: (n, r))],
        out_shape=[jax.ShapeDtypeStruct((L, dil * 1024), o_dtype), jax.ShapeDtypeStruct((L, dil * 128), F32)],
        scratch_shapes=[pltpu.VMEM((T, 128), F32), pltpu.VMEM((T, 128), F32), pltpu.VMEM((T, 1024), F32)],
        name=name, compiler_params=_params("parallel", "parallel", "arbitrary"),
    )(*args)


def _attn_bwd(qa, ka, va, qcb, kcb, vcb, doa, oa, lsea, *, dil, T, nqs, window, name, c=None, cT=None):
    L = qa.shape[0]
    nq = L // T
    fox = c is not None

    def body(*refs):
        if fox:
            (q_ref, k_ref, v_ref, do_ref, o_ref, lse_ref, c_ref, ct_ref,
             dq_ref, dk_ref, dv_ref, dct_ref, dcq_ref, dq_sc, dk_sc, dv_sc, dc_sc, dcq_sc) = refs
        else:
            (q_ref, k_ref, v_ref, do_ref, o_ref, lse_ref,
             dq_ref, dk_ref, dv_ref, dq_sc, dk_sc, dv_sc) = refs
        kb = pl.program_id(1)
        jq = pl.program_id(2)
        qb = kb + jq

        @pl.when(jnp.logical_and(kb == 0, jq == 0))
        def _():
            dq_sc[...] = jnp.zeros_like(dq_sc)
            if fox:
                dcq_sc[...] = jnp.zeros_like(dcq_sc)

        @pl.when(jq == 0)
        def _():
            dk_sc[...] = jnp.zeros_like(dk_sc)
            dv_sc[...] = jnp.zeros_like(dv_sc)
            if fox:
                dc_sc[...] = jnp.zeros_like(dc_sc)

        @pl.when(qb < nq)
        def _():
            diff = (qb * T + lax.broadcasted_iota(jnp.int32, (T, T), 0)) - (kb * T + lax.broadcasted_iota(jnp.int32, (T, T), 1))
            valid = diff >= 0
            if window is not None:
                valid = jnp.logical_and(valid, diff <= window)
            qrows = pl.ds(pl.multiple_of(qb * T, T), T)
            for h in range(N_HEADS):
                hs = slice(h * HEAD_DIM, (h + 1) * HEAD_DIM)
                qh = q_ref[:, hs] * jnp.asarray(ATTN_SCALE, BF)
                kh = k_ref[:, hs]
                doh = do_ref[:, hs]
                s = lax.dot_general(qh, kh, (((1,), (1,)), ((), ())), preferred_element_type=F32)
                if fox:
                    s = s + c_ref[:, h:h + 1] - ct_ref[h:h + 1, :]
                s = jnp.where(valid, s, NEG_INF)
                p = jnp.exp(s - lse_ref[:, h:h + 1])
                dp = lax.dot_general(doh, v_ref[:, hs], (((1,), (1,)), ((), ())), preferred_element_type=F32)
                delta = jnp.sum(doh.astype(F32) * o_ref[:, hs].astype(F32), axis=1, keepdims=True)
                ds = p * (dp - delta)
                dsb = ds.astype(BF)
                dv_sc[:, hs] += lax.dot_general(p.astype(BF), doh, (((0,), (0,)), ((), ())), preferred_element_type=F32)
                dk_sc[:, hs] += lax.dot_general(dsb, qh, (((0,), (0,)), ((), ())), preferred_element_type=F32)
                dq_sc[qrows, hs] += jnp.dot(dsb, kh, preferred_element_type=F32) * ATTN_SCALE
                if fox:
                    dc_sc[h:h + 1, :] -= jnp.sum(ds, axis=0, keepdims=True)
                    dcq_sc[qrows, h:h + 1] += jnp.sum(ds, axis=1, keepdims=True)

        @pl.when(jq == nqs - 1)
        def _():
            dk_ref[...] = dk_sc[...].astype(BF)
            dv_ref[...] = dv_sc[...].astype(BF)
            if fox:
                dct_ref[...] = dc_sc[...]

        @pl.when(jnp.logical_and(kb == nq - 1, jq == nqs - 1))
        def _():
            def put(i, carry):
                rows = pl.ds(pl.multiple_of(i * T, T), T)
                dq_ref[rows, :] = dq_sc[rows, :].astype(BF)
                return carry
            lax.fori_loop(0, nq, put, 0)
            if fox:
                dcq_ref[...] = dcq_sc[...]

    def qi(kb, jq):
        return jnp.minimum(kb + jq, nq - 1)

    in_specs = [pl.BlockSpec((T, 1024), lambda r, kb, jq: (qi(kb, jq), qcb(r))),
                pl.BlockSpec((T, 1024), lambda r, kb, jq: (kb, kcb(r))),
                pl.BlockSpec((T, 1024), lambda r, kb, jq: (kb, vcb(r))),
                pl.BlockSpec((T, 1024), lambda r, kb, jq: (qi(kb, jq), r)),
                pl.BlockSpec((T, 1024), lambda r, kb, jq: (qi(kb, jq), r)),
                pl.BlockSpec((T, 128), lambda r, kb, jq: (qi(kb, jq), r))]
    args = [qa, ka, va, doa, oa, lsea]
    out_specs = [pl.BlockSpec((L, 1024), lambda r, kb, jq: (0, r)),
                 pl.BlockSpec((T, 1024), lambda r, kb, jq: (kb, r)),
                 pl.BlockSpec((T, 1024), lambda r, kb, jq: (kb, r))]
    out_shape = [jax.ShapeDtypeStruct((L, dil * 1024), BF)] * 3
    scratch = [pltpu.VMEM((L, 1024), F32), pltpu.VMEM((T, 1024), F32), pltpu.VMEM((T, 1024), F32)]
    if fox:
        in_specs += [pl.BlockSpec((T, GATE_LANES), lambda r, kb, jq: (qi(kb, jq), 0)),
                     pl.BlockSpec((GATE_LANES, T), lambda r, kb, jq: (0, kb))]
        args += [c, cT]
        out_specs.append(pl.BlockSpec((GATE_LANES, T), lambda r, kb, jq: (0, kb)))
        out_shape.append(jax.ShapeDtypeStruct((GATE_LANES, L), F32))
        scratch.append(pltpu.VMEM((GATE_LANES, T), F32))
        out_specs.append(pl.BlockSpec((L, GATE_LANES), lambda r, kb, jq: (0, 0)))
        out_shape.append(jax.ShapeDtypeStruct((L, GATE_LANES), F32))
        scratch.append(pltpu.VMEM((L, GATE_LANES), F32))
    return pl.pallas_call(
        body, grid=(dil, nq, nqs), in_specs=in_specs, out_specs=out_specs, out_shape=out_shape,
        scratch_shapes=scratch, name=name, compiler_params=_params("arbitrary", "arbitrary", "arbitrary"),
    )(*args)


def _band_masks(T, n):
    row = lax.broadcasted_iota(jnp.int32, (T, T), 0)
    col = lax.broadcasted_iota(jnp.int32, (T, T), 1)
    return jnp.logical_and(col >= row, n > 0), col <= row


def _band_fwd(qa, ka, va, qcb, kcb, vcb, *, dil, T, window, name, guest=None):
    L = qa.shape[0]
    nq = L // T
    assert window == T
    nt = (((1,), (1,)), ((), ()))

    def body(q_ref, kp_ref, kc_ref, vp_ref, vc_ref, o_ref, lse_ref):
        valid_prev, valid_cur = _band_masks(T, pl.program_id(1))
        lane = lax.broadcasted_iota(jnp.int32, (T, 128), 1)
        low = lane < HEAD_DIM
        ones = jnp.ones((T, 128), BF)
        lse = jnp.zeros((T, 128), F32)
        def scores(h):
            ps = slice((h // 2) * 128, (h // 2 + 1) * 128)
            qp = q_ref[:, ps] * jnp.asarray(ATTN_SCALE, BF)
            qm = jnp.where(low if h % 2 == 0 else jnp.logical_not(low), qp, jnp.zeros_like(qp))
            s0 = jnp.where(valid_prev, lax.dot_general(qm, kp_ref[:, ps], nt, preferred_element_type=F32), NEG_INF)
            s1 = jnp.where(valid_cur, lax.dot_general(qm, kc_ref[:, ps], nt, preferred_element_type=F32), NEG_INF)
            return s0, s1

        def softmax(s0, s1):
            m = jnp.maximum(jnp.max(s0, axis=1, keepdims=True), jnp.max(s1, axis=1, keepdims=True))
            return m, jnp.exp(s0 - m).astype(BF), jnp.exp(s1 - m).astype(BF)

        def weighted(h, p0, p1):
            ps = slice((h // 2) * 128, (h // 2 + 1) * 128)
            l = jnp.dot(p0, ones, preferred_element_type=F32) + jnp.dot(p1, ones, preferred_element_type=F32)
            acc = jnp.dot(p0, vp_ref[:, ps], preferred_element_type=F32) + jnp.dot(p1, vc_ref[:, ps], preferred_element_type=F32)
            return l, acc

        sc, pr, even = {}, {}, None
        for t in range(N_HEADS + 2):
            if t < N_HEADS:
                sc[t] = scores(t)
            done = None
            if t >= 2:
                m, p0, p1 = pr.pop(t - 2)
                done = (m,) + weighted(t - 2, p0, p1)
            if 1 <= t <= N_HEADS:
                pr[t - 1] = softmax(*sc.pop(t - 1))
            if done is not None:
                h = t - 2
                m, l, acc = done
                lse = jnp.where(lane == h, m + jnp.log(l), lse)
                if h % 2 == 0:
                    even = acc / l
                else:
                    o_ref[:, (h // 2) * 128:(h // 2 + 1) * 128] = jnp.where(low, even, acc / l)
        lse_ref[...] = lse

    def prev(n):
        return jnp.maximum(n - 1, 0)

    blk = lambda f, cb: pl.BlockSpec((T, 1024), lambda r, n: (f(n), cb(r)))
    same = lambda n: n
    outs = _hosted_call(
        body, grid=(dil, nq),
        in_specs=[blk(same, qcb), blk(prev, kcb), blk(same, kcb), blk(prev, vcb), blk(same, vcb)],
        out_specs=[pl.BlockSpec((T, 1024), lambda r, n: (n, r)), pl.BlockSpec((T, 128), lambda r, n: (n, r))],
        out_shape=[jax.ShapeDtypeStruct((L, dil * 1024), F32), jax.ShapeDtypeStruct((L, dil * 128), F32)],
        scratch_shapes=[], args=(qa, ka, ka, va, va), name=name, guest=guest)
    return outs if guest is None else (outs[:2], outs[2:])


def _band_bwd(qa, ka, va, qcb, kcb, vcb, doa, oa, lsea, *, dil, T, window, name, guest=None):
    L = qa.shape[0]
    nq = L // T
    assert window == T
    nt = (((1,), (1,)), ((), ()))
    tn = (((0,), (0,)), ((), ()))

    def body(q_ref, kp_ref, kc_ref, vp_ref, vc_ref, do_ref, o_ref, lse_ref, dq_ref, dk_ref, dv_ref, ck_sc, cv_sc):
        n = pl.program_id(1)

        @pl.when(n == 0)
        def _():
            ck_sc[...] = jnp.zeros_like(ck_sc)
            cv_sc[...] = jnp.zeros_like(cv_sc)

        @pl.when(n < nq)
        def _():
            valid_prev, valid_cur = _band_masks(T, n)
            low = lax.broadcasted_iota(jnp.int32, (T, 128), 1) < HEAD_DIM
            dot = functools.partial(lax.dot_general, preferred_element_type=F32)

            def pair(h):
                return slice((h // 2) * 128, (h // 2 + 1) * 128)

            def products(h):
                ps = pair(h)
                mask = low if h % 2 == 0 else jnp.logical_not(low)
                qp = q_ref[:, ps] * jnp.asarray(ATTN_SCALE, BF)
                dop = do_ref[:, ps]
                qm = jnp.where(mask, qp, jnp.zeros_like(qp))
                dom = jnp.where(mask, dop, jnp.zeros_like(dop))
                s0 = jnp.where(valid_prev, dot(qm, kp_ref[:, ps], nt), NEG_INF)
                s1 = jnp.where(valid_cur, dot(qm, kc_ref[:, ps], nt), NEG_INF)
                return qm, dom, s0, s1, dot(dom, vp_ref[:, ps], nt), dot(dom, vc_ref[:, ps], nt)

            def pointwise(h, qm, dom, s0, s1, dp0, dp1):
                ps = pair(h)
                mask = low if h % 2 == 0 else jnp.logical_not(low)
                prod = do_ref[:, ps].astype(F32) * o_ref[:, ps].astype(F32)
                delta = jnp.sum(jnp.where(mask, prod, 0.0), axis=1, keepdims=True)
                lse = lse_ref[:, h:h + 1]
                p0 = jnp.exp(s0 - lse)
                p1 = jnp.exp(s1 - lse)
                ds0 = (p0 * (dp0 - delta)).astype(BF)
                ds1 = (p1 * (dp1 - delta)).astype(BF)
                return qm, dom, p0.astype(BF), p1.astype(BF), ds0, ds1

            def gradients(h, qm, dom, p0, p1, ds0, ds1):
                ps = pair(h)
                dq = dot(ds0, kp_ref[:, ps], (((1,), (0,)), ((), ()))) + dot(ds1, kc_ref[:, ps], (((1,), (0,)), ((), ())))
                return dq, dot(ds0, qm, tn), dot(p0, dom, tn), dot(ds1, qm, tn), dot(p1, dom, tn)

            st1, st2, even = {}, {}, None
            for t in range(N_HEADS + 2):
                if t < N_HEADS:
                    st1[t] = products(t)
                done = gradients(t - 2, *st2.pop(t - 2)) if t >= 2 else None
                if 1 <= t <= N_HEADS:
                    st2[t - 1] = pointwise(t - 1, *st1.pop(t - 1))
                if done is not None:
                    h = t - 2
                    if h % 2 == 0:
                        even = done
                    else:
                        ps = pair(h)
                        dq_ref[:, ps] = (jnp.where(low, even[0], done[0]) * ATTN_SCALE).astype(BF)
                        dk_ref[:, ps] = (ck_sc[:, ps] + even[1] + done[1]).astype(BF)
                        dv_ref[:, ps] = (cv_sc[:, ps] + even[2] + done[2]).astype(BF)
                        ck_sc[:, ps] = even[3] + done[3]
                        cv_sc[:, ps] = even[4] + done[4]

        @pl.when(n == nq)
        def _():
            dk_ref[...] = ck_sc[...].astype(BF)
            dv_ref[...] = cv_sc[...].astype(BF)

    def cur(n):
        return jnp.minimum(n, nq - 1)

    def prev(n):
        return jnp.maximum(cur(n) - 1, 0)

    blk = lambda f, cb: pl.BlockSpec((T, 1024), lambda r, n: (f(n), cb(r)))
    own = lambda r: r
    outs = _hosted_call(
        body, grid=(dil, nq + 1),
        in_specs=[blk(cur, qcb), blk(prev, kcb), blk(cur, kcb), blk(prev, vcb), blk(cur, vcb), blk(cur, own), blk(cur, own),
                  pl.BlockSpec((T, 128), lambda r, n: (cur(n), r))],
        out_specs=[blk(cur, own), blk(lambda n: jnp.maximum(n - 1, 0), own), blk(lambda n: jnp.maximum(n - 1, 0), own)],
        out_shape=[jax.ShapeDtypeStruct((L, dil * 1024), BF)] * 3,
        scratch_shapes=[pltpu.VMEM((T, 1024), F32), pltpu.VMEM((T, 1024), F32)],
        args=(qa, ka, ka, va, va, doa, oa, lsea), name=name, guest=guest)
    return outs if guest is None else (outs[:3], outs[3:])


FOX_T = 256
FOX_TK = 512
FOX_ROWS = 128


def _fox_fwd(qkv, cT, *, name, guest=None):
    S = qkv.shape[0]
    T, TK, R = FOX_T, FOX_TK, FOX_ROWS
    nq = S // T
    nt = (((1,), (1,)), ((), ()))
    chains = [(h, rh) for h in range(N_HEADS) for rh in range(T // R)]
    pairs = [(n, j) for n in range(nq) for j in range((n * T + T - 1) // TK + 1)]
    schedule = [jnp.asarray([p[i] for p in pairs], jnp.int32) for i in range(2)]

    def body(n_tab, j_tab, q_ref, k_ref, v_ref, ct_ref, o_ref, lse_ref, m_sc, l_sc, acc_sc):
        n = n_tab[pl.program_id(0)]
        j = j_tab[pl.program_id(0)]
        last_j = (n * T + T - 1) // TK
        lane = lax.broadcasted_iota(jnp.int32, (R, 128), 1)
        low = lane < HEAD_DIM
        ones = jnp.ones((TK, 128), BF)

        @pl.when(j == 0)
        def _():
            m_sc[...] = jnp.full(m_sc.shape, NEG_INF, F32)
            l_sc[...] = jnp.zeros_like(l_sc)
            acc_sc[...] = jnp.zeros_like(acc_sc)

        def step(diagonal):
            def pair(h):
                return slice((h // 2) * 128, (h // 2 + 1) * 128)

            def rows(rh):
                return slice(rh * R, (rh + 1) * R)

            def scores(h, rh):
                qp = q_ref[rows(rh), pair(h)] * jnp.asarray(ATTN_SCALE, BF)
                qm = jnp.where(low if h % 2 == 0 else jnp.logical_not(low), qp, jnp.zeros_like(qp))
                s = lax.dot_general(qm, k_ref[:, pair(h)], nt, preferred_element_type=F32) - ct_ref[h:h + 1, :]
                if diagonal:
                    ahead = lax.broadcasted_iota(jnp.int32, (R, TK), 1) - lax.broadcasted_iota(jnp.int32, (R, TK), 0)
                    s = jnp.where(ahead <= n * T + rh * R - j * TK, s, NEG_INF)
                return s

            def softmax(h, rh, s):
                m_prev = m_sc[h, rows(rh), :]
                m_new = jnp.maximum(m_prev, jnp.max(s, axis=1, keepdims=True))
                p = jnp.exp(s - jnp.concatenate([m_new] * (TK // 128), axis=1)).astype(BF)
                return m_new, jnp.exp(m_prev - m_new), p

            def weighted(h, p):
                vx = jnp.concatenate([v_ref[:, pair(h)], ones], axis=1)
                return jnp.dot(p, vx, preferred_element_type=F32)

            sc, pr, even = {}, {}, {}
            nch = len(chains)
            for t in range(nch + 2):
                if t < nch:
                    sc[t] = scores(*chains[t])
                done = None
                if t >= 2:
                    m_new, alpha, p = pr.pop(t - 2)
                    done = (m_new, alpha, weighted(chains[t - 2][0], p))
                if 1 <= t <= nch:
                    pr[t - 1] = softmax(*chains[t - 1], sc.pop(t - 1))
                if done is not None:
                    h, rh = chains[t - 2]
                    m_new, alpha, pv = done
                    m_sc[h, rows(rh), :] = m_new
                    l_sc[h, rows(rh), :] = alpha * l_sc[h, rows(rh), :] + pv[:, 128:]
                    if h % 2 == 0:
                        even[rh] = (alpha, pv[:, :128])
                    else:
                        a0, pv0 = even.pop(rh)
                        acc = acc_sc[h // 2, rows(rh), :]
                        acc_sc[h // 2, rows(rh), :] = jnp.where(low, a0 * acc + pv0, alpha * acc + pv[:, :128])

        @pl.when(j < last_j)
        def _():
            step(False)

        @pl.when(j == last_j)
        def _():
            step(True)
            lane_t = lax.broadcasted_iota(jnp.int32, (T, 128), 1)
            low_t = lane_t < HEAD_DIM
            lse = jnp.zeros((T, 128), F32)
            for h in range(N_HEADS):
                lse = jnp.where(lane_t == h, m_sc[h] + jnp.log(l_sc[h]), lse)
            lse_ref[...] = lse
            for hp in range(N_HEADS // 2):
                inv = jnp.where(low_t, 1.0 / l_sc[2 * hp], 1.0 / l_sc[2 * hp + 1])
                o_ref[:, hp * 128:(hp + 1) * 128] = (acc_sc[hp] * inv).astype(BF)

    outs = _hosted_call(
        body, grid=(len(pairs),),
        in_specs=[pl.BlockSpec((T, 1024), lambda t, n, j: (n[t], 0)), pl.BlockSpec((TK, 1024), lambda t, n, j: (j[t], 1)),
                  pl.BlockSpec((TK, 1024), lambda t, n, j: (j[t], 2)), pl.BlockSpec((GATE_LANES, TK), lambda t, n, j: (0, j[t]))],
        out_specs=[pl.BlockSpec((T, 1024), lambda t, n, j: (n[t], 0)), pl.BlockSpec((T, 128), lambda t, n, j: (n[t], 0))],
        out_shape=[jax.ShapeDtypeStruct((S, 1024), BF), jax.ShapeDtypeStruct((S, 128), F32)],
        scratch_shapes=[pltpu.VMEM((N_HEADS, T, 128), F32), pltpu.VMEM((N_HEADS, T, 128), F32),
                        pltpu.VMEM((N_HEADS // 2, T, 128), F32)],
        args=(qkv, qkv, qkv, cT), name=name, guest=guest, schedule=schedule)
    return outs if guest is None else (outs[:2], outs[2:])


def _fox_bwd(qkv, cT, do, o, lse, *, name, guest=None):
    S = qkv.shape[0]
    T, R = FOX_T, FOX_ROWS
    nq = S // T
    nt = (((1,), (1,)), ((), ()))
    tn = (((0,), (0,)), ((), ()))
    nn = (((1,), (0,)), ((), ()))
    chains = [(h, rh) for h in range(N_HEADS) for rh in range(T // R)]
    dot = functools.partial(lax.dot_general, preferred_element_type=F32)
    pairs = [(kb, qb) for kb in range(nq) for qb in range(kb, nq)]
    schedule = [jnp.asarray([p[i] for p in pairs], jnp.int32) for i in range(2)]

    def body(kb_tab, qb_tab, q_ref, k_ref, v_ref, ct_ref, do_ref, o_ref, lse_ref, dq_ref, dk_ref, dv_ref, dct_ref, dcq_ref,
             dq_sc, dk_sc, dv_sc, dc_sc, dcq_sc):
        kb = kb_tab[pl.program_id(0)]
        qb = qb_tab[pl.program_id(0)]
        jq = qb - kb
        lane = lax.broadcasted_iota(jnp.int32, (R, 128), 1)
        low = lane < HEAD_DIM
        ones_k = jnp.ones((T, 128), BF)
        ones_r = jnp.ones((8, R), BF)

        @pl.when(jnp.logical_and(kb == 0, jq == 0))
        def _():
            dq_sc[...] = jnp.zeros_like(dq_sc)
            dcq_sc[...] = jnp.zeros_like(dcq_sc)

        @pl.when(jq == 0)
        def _():
            dk_sc[...] = jnp.zeros_like(dk_sc)
            dv_sc[...] = jnp.zeros_like(dv_sc)
            dc_sc[...] = jnp.zeros_like(dc_sc)

        def step(diagonal):
            def pair(h):
                return slice((h // 2) * 128, (h // 2 + 1) * 128)

            def rows(rh):
                return slice(rh * R, (rh + 1) * R)

            def qrows(rh):
                return pl.ds(pl.multiple_of(qb * T + rh * R, R), R)

            def products(h, rh):
                mask = low if h % 2 == 0 else jnp.logical_not(low)
                qp = q_ref[rows(rh), pair(h)] * jnp.asarray(ATTN_SCALE, BF)
                dop = do_ref[rows(rh), pair(h)]
                qm = jnp.where(mask, qp, jnp.zeros_like(qp))
                dom = jnp.where(mask, dop, jnp.zeros_like(dop))
                s = dot(qm, k_ref[:, pair(h)], nt) - ct_ref[h:h + 1, :]
                if diagonal:
                    keep = (lax.broadcasted_iota(jnp.int32, (R, T), 1)
                            <= rh * R + lax.broadcasted_iota(jnp.int32, (R, T), 0))
                    s = jnp.where(keep, s, NEG_INF)
                return qm, dom, s, dot(dom, v_ref[:, pair(h)], nt)

            def pointwise(h, rh, qm, dom, s, dp):
                mask = low if h % 2 == 0 else jnp.logical_not(low)
                prod = do_ref[rows(rh), pair(h)].astype(F32) * o_ref[rows(rh), pair(h)].astype(F32)
                delta = jnp.sum(jnp.where(mask, prod, 0.0), axis=1, keepdims=True)
                p = jnp.exp(s - lse_ref[rows(rh), h:h + 1])
                ds = (p * (dp - delta)).astype(BF)
                return qm, dom, p.astype(BF), ds

            def gradients(h, qm, dom, p, ds):
                kx = jnp.concatenate([k_ref[:, pair(h)], ones_k], axis=1)
                return dot(ds, kx, nn), dot(qm, ds, tn), dot(dom, p, tn), dot(ones_r, ds, nn)

            st1, st2, even = {}, {}, {}
            dcq_tiles = [jnp.zeros((R, 128), F32) for _ in range(T // R)]
            nch = len(chains)
            for t in range(nch + 2):
                if t < nch:
                    st1[t] = products(*chains[t])
                done = gradients(chains[t - 2][0], *st2.pop(t - 2)) if t >= 2 else None
                if 1 <= t <= nch:
                    st2[t - 1] = pointwise(*chains[t - 1], *st1.pop(t - 1))
                if done is not None:
                    h, rh = chains[t - 2]
                    dq_rsum, dk, dv, csum = done
                    dq = dq_rsum[:, :128]
                    dcq_tiles[rh] = jnp.where(lane == h, dq_rsum[:, 128:], dcq_tiles[rh])
                    dc_sc[h:h + 1, :] -= csum[0:1, :]
                    if h % 2 == 0:
                        even[rh] = (dq, dk, dv)
                    else:
                        dq0, dk0, dv0 = even.pop(rh)
                        dq_sc[qrows(rh), pair(h)] += jnp.where(low, dq0, dq) * ATTN_SCALE
                        dk_sc[h // 2] += dk0 + dk
                        dv_sc[h // 2] += dv0 + dv
            for rh in range(T // R):
                dcq_sc[qrows(rh), :] += dcq_tiles[rh]

        @pl.when(jq > 0)
        def _():
            step(False)

        @pl.when(jq == 0)
        def _():
            step(True)

        @pl.when(qb == nq - 1)
        def _():
            for hp in range(N_HEADS // 2):
                dk_ref[:, hp * 128:(hp + 1) * 128] = dk_sc[hp].T.astype(BF)
                dv_ref[:, hp * 128:(hp + 1) * 128] = dv_sc[hp].T.astype(BF)
            dct_ref[...] = dc_sc[...]

        @pl.when(kb == nq - 1)
        def _():
            def put(i, carry):
                r = pl.ds(pl.multiple_of(i * T, T), T)
                dq_ref[r, :] = dq_sc[r, :].astype(BF)
                return carry
            lax.fori_loop(0, nq, put, 0)
            dcq_ref[...] = dcq_sc[...]

    qblk = lambda col: pl.BlockSpec((T, 1024), lambda t, kb, qb: (qb[t], col))
    kblk = lambda col: pl.BlockSpec((T, 1024), lambda t, kb, qb: (kb[t], col))
    whole = pl.BlockSpec((S, 1024), lambda t, kb, qb: (0, 0))
    outs = _hosted_call(
        body, grid=(len(pairs),),
        in_specs=[qblk(0), kblk(1), kblk(2), pl.BlockSpec((GATE_LANES, T), lambda t, kb, qb: (0, kb[t])), qblk(0), qblk(0),
                  pl.BlockSpec((T, 128), lambda t, kb, qb: (qb[t], 0))],
        out_specs=[whole, kblk(0), kblk(0), pl.BlockSpec((GATE_LANES, T), lambda t, kb, qb: (0, kb[t])),
                   pl.BlockSpec((S, GATE_LANES), lambda t, kb, qb: (0, 0))],
        out_shape=[jax.ShapeDtypeStruct((S, 1024), BF)] * 3 + [jax.ShapeDtypeStruct((GATE_LANES, S), F32),
                                                               jax.ShapeDtypeStruct((S, GATE_LANES), F32)],
        scratch_shapes=[pltpu.VMEM((S, 1024), F32), pltpu.VMEM((N_HEADS // 2, 128, T), F32), pltpu.VMEM((N_HEADS // 2, 128, T), F32),
                        pltpu.VMEM((GATE_LANES, T), F32), pltpu.VMEM((S, GATE_LANES), F32)],
        args=(qkv, qkv, qkv, cT, do, o, lse), name=name, guest=guest, schedule=schedule)
    return outs if guest is None else (outs[:5], outs[5:])


def _to_natural(src_ref, buf, d, width):
    rows = buf.shape[1]
    for r in range(d):
        for ch in range(width // 128):
            lanes = slice(r * width + ch * 128, r * width + (ch + 1) * 128)
            buf.at[ch][pl.ds(r, rows // d, stride=d), :] = src_ref[:, lanes].astype(F32)
    return jnp.concatenate([buf[ch] for ch in range(width // 128)], axis=1)


def _to_view(val, buf, dst_ref, d, width):
    rows = buf.shape[1]
    for ch in range(width // 128):
        buf[ch] = val[:, ch * 128:(ch + 1) * 128]
    for r in range(d):
        for ch in range(width // 128):
            lanes = slice(r * width + ch * 128, r * width + (ch + 1) * 128)
            dst_ref[:, lanes] = buf.at[ch][pl.ds(r, rows // d, stride=d), :].astype(dst_ref.dtype)


def _view_spec(rows, d, width):
    return pl.BlockSpec((rows // d, d * width), lambda i, *_: (i, 0))


def _combine_groups(os, lses, dils, *, name):
    ng = len(os)
    S = os[0].shape[0] * dils[0]
    tm = ROW_TILE
    views = sorted(set(dils))

    def body(*refs):
        o_refs, l_refs = refs[:ng], refs[ng:2 * ng]
        outs = refs[2 * ng:2 * ng + 2 * len(views)]
        wide, narrow = refs[-2], refs[-1]
        ls = [l_refs[g][...] if dils[g] == 1 else _to_natural(l_refs[g], narrow, dils[g], 128) for g in range(ng)]
        m = functools.reduce(jnp.maximum, ls)
        es = [jnp.exp(l - m) for l in ls]
        den = functools.reduce(jnp.add, es)
        ws = [e / den for e in es]
        lse = m + jnp.log(den)
        og = [o_refs[g][...] if dils[g] == 1 else _to_natural(o_refs[g], wide, dils[g], 1024) for g in range(ng)]
        cols = []
        for h in range(N_HEADS):
            hs = slice(h * HEAD_DIM, (h + 1) * HEAD_DIM)
            acc = ws[0][:, h:h + 1] * og[0][:, hs]
            for g in range(1, ng):
                acc = acc + ws[g][:, h:h + 1] * og[g][:, hs]
            cols.append(acc)
        o = jnp.concatenate(cols, axis=1)
        for k, d in enumerate(views):
            if d == 1:
                outs[2 * k][...] = o.astype(BF)
                outs[2 * k + 1][...] = lse
            else:
                _to_view(o, wide, outs[2 * k], d, 1024)
                _to_view(lse, narrow, outs[2 * k + 1], d, 128)

    out_specs, out_shape = [], []
    for d in views:
        out_specs += [_view_spec(tm, d, 1024), _view_spec(tm, d, 128)]
        out_shape += [jax.ShapeDtypeStruct((S // d, d * 1024), BF), jax.ShapeDtypeStruct((S // d, d * 128), F32)]
    res = pl.pallas_call(
        body, grid=(S // tm,), in_specs=[_view_spec(tm, d, 1024) for d in dils] + [_view_spec(tm, d, 128) for d in dils],
        out_specs=out_specs, out_shape=out_shape,
        scratch_shapes=[pltpu.VMEM((8, tm, 128), F32), pltpu.VMEM((1, tm, 128), F32)],
        name=name, compiler_params=_params("parallel"),
    )(*os, *lses)
    return {d: (res[2 * k], res[2 * k + 1]) for k, d in enumerate(views)}


def _assemble(parts, rope_flags, rope, dils, *, name):
    n = len(parts)
    S = parts[0].shape[0] * dils[0]
    use_rope = any(rope_flags)
    tm = 256

    def body(*refs):
        out_ref, natural = refs[-2], refs[-1]
        for b in range(n):
            cols = slice(b * 1024, (b + 1) * 1024)
            d = dils[b]
            val = refs[b][...].astype(F32) if d == 1 else _to_natural(refs[b], natural, d, 1024)
            if rope_flags[b]:
                cos_ref, sa_ref, sb_ref = refs[n:n + 3]
                val = _rope_rotate(val, cos_ref[...], sa_ref[...], sb_ref[...])
            out_ref[:, cols] = val.astype(BF)

    in_specs = [_view_spec(tm, d, 1024) for d in dils]
    args = list(parts)
    if use_rope:
        in_specs += [pl.BlockSpec((tm, 128), lambda i: (i, 0))] * 3
        args += list(rope)
    return pl.pallas_call(
        body, grid=(S // tm,), in_specs=in_specs, out_specs=pl.BlockSpec((tm, n * 1024), lambda i: (i, 0)),
        out_shape=jax.ShapeDtypeStruct((S, n * 1024), BF), scratch_shapes=[pltpu.VMEM((8, tm, 128), F32)],
        name=name, compiler_params=_params("parallel"),
    )(*args)


GATE_ROWS = 512


def _gate_fwd(z, bf, *, name):
    S = z.shape[0]

    def body(z_ref, b_ref, c_ref, ct_ref, carry):
        i = pl.program_id(0)

        @pl.when(i == 0)
        def _():
            carry[...] = jnp.zeros_like(carry)

        zz = z_ref[...] + b_ref[...]
        logf = jnp.minimum(zz, 0.0) - jnp.log(1.0 + jnp.exp(-jnp.abs(zz)))
        tri = (lax.broadcasted_iota(jnp.int32, (GATE_ROWS, GATE_ROWS), 0)
               >= lax.broadcasted_iota(jnp.int32, (GATE_ROWS, GATE_ROWS), 1)).astype(F32)
        cs = jnp.dot(tri, logf, precision=lax.Precision.HIGHEST, preferred_element_type=F32) + carry[...]
        c_ref[...] = cs
        ct_ref[...] = cs.T
        carry[...] = cs[GATE_ROWS - 1:GATE_ROWS, :]

    return pl.pallas_call(
        body, grid=(S // GATE_ROWS,),
        in_specs=[pl.BlockSpec((GATE_ROWS, GATE_LANES), lambda i: (i, 0)), pl.BlockSpec((1, GATE_LANES), lambda i: (0, 0))],
        out_specs=[pl.BlockSpec((GATE_ROWS, GATE_LANES), lambda i: (i, 0)), pl.BlockSpec((GATE_LANES, GATE_ROWS), lambda i: (0, i))],
        out_shape=[jax.ShapeDtypeStruct((S, GATE_LANES), F32), jax.ShapeDtypeStruct((GATE_LANES, S), F32)],
        scratch_shapes=[pltpu.VMEM((1, GATE_LANES), F32)], name=name, compiler_params=_params("arbitrary"),
    )(z, bf)


def _gate_bwd(z, bf, dcT, dcq, *, name):
    S = z.shape[0]
    nb = S // GATE_ROWS

    def body(z_ref, b_ref, dct_ref, dcq_ref, dz_ref, db_ref, carry):
        i = pl.program_id(0)

        @pl.when(i == 0)
        def _():
            carry[...] = jnp.zeros_like(carry)
            db_ref[...] = jnp.zeros_like(db_ref)

        dc = dct_ref[...].T + dcq_ref[...]
        tri = (lax.broadcasted_iota(jnp.int32, (GATE_ROWS, GATE_ROWS), 0)
               <= lax.broadcasted_iota(jnp.int32, (GATE_ROWS, GATE_ROWS), 1)).astype(F32)
        dl = jnp.dot(tri, dc, precision=lax.Precision.HIGHEST, preferred_element_type=F32) + carry[...]
        carry[...] = dl[0:1, :]
        zz = z_ref[...] + b_ref[...]
        dz = dl * (1.0 / (1.0 + jnp.exp(zz)))
        lane = lax.broadcasted_iota(jnp.int32, dz.shape, 1)
        dz = jnp.where(lane < N_HEADS, dz, 0.0)
        dz_ref[...] = dz.astype(BF)
        db_ref[...] += jnp.sum(dz, axis=0, keepdims=True)

    return pl.pallas_call(
        body, grid=(nb,),
        in_specs=[pl.BlockSpec((GATE_ROWS, GATE_LANES), lambda i: (nb - 1 - i, 0)), pl.BlockSpec((1, GATE_LANES), lambda i: (0, 0)),
                  pl.BlockSpec((GATE_LANES, GATE_ROWS), lambda i: (0, nb - 1 - i)),
                  pl.BlockSpec((GATE_ROWS, GATE_LANES), lambda i: (nb - 1 - i, 0))],
        out_specs=[pl.BlockSpec((GATE_ROWS, GATE_LANES), lambda i: (nb - 1 - i, 0)), pl.BlockSpec((1, GATE_LANES), lambda i: (0, 0))],
        out_shape=[jax.ShapeDtypeStruct((S, GATE_LANES), BF), jax.ShapeDtypeStruct((1, GATE_LANES), F32)],
        scratch_shapes=[pltpu.VMEM((1, GATE_LANES), F32)], name=name, compiler_params=_params("arbitrary"),
    )(z, bf, dcT, dcq)


def _rope_tables(S):
    half = ROT_DIM // 2
    inv_freq = ROPE_THETA ** (-jnp.arange(half, dtype=F32) * 2.0 / ROT_DIM)
    ang = jnp.arange(S, dtype=F32)[:, None] * inv_freq[None, :]
    cos, sin = jnp.cos(ang), jnp.sin(ang)
    zero = jnp.zeros((S, HEAD_DIM - ROT_DIM), F32)
    zh = jnp.zeros((S, half), F32)
    cos_h = jnp.concatenate([cos, cos, jnp.ones_like(zero)], axis=1)
    sa_h = jnp.concatenate([-sin, zh, zero], axis=1)
    sb_h = jnp.concatenate([zh, sin, zero], axis=1)
    two = lambda t: jnp.concatenate([t, t], axis=1)
    return two(cos_h), two(sa_h), two(sb_h)


def _ffn_fwd(h, norm, w_gu, w_down, tag):
    n = _rms_fwd(h, norm, name=f"ffn{tag}_norm")
    gu = _mm_nn(n, w_gu, tm=1024, tn=512, out_dtype=BF, name=f"ffn{tag}_gu")
    act = _swiglu_fwd(gu, name=f"ffn{tag}_act")
    out = _mm_nn(act, w_down, tm=512, tn=1024, out_dtype=F32, name=f"ffn{tag}_down", resid=h)
    return out, (h, n, gu, act)


def _ffn_bwd(dh, dhb, saved, norm, w_gu, w_down, tag, ride=None):
    h, n, gu, act = saved
    dact = _mm_nt(dhb, w_down, tm=512, to=1408, tn=1024, out_dtype=BF, name=f"ffn{tag}_dact")
    dw_down = _mm_tn(act, dhb, tk=1408, tn=1024, tm=512, out_dtype=BF, name=f"ffn{tag}_dwdown")
    dgu = _swiglu_bwd(gu, dact, name=f"ffn{tag}_dgu")
    dn_call = lambda guest: _mm_nt(dgu, w_gu, tm=512, to=1024, tn=1408, out_dtype=F32, name=f"ffn{tag}_dn", guest=guest)
    dn = dn_call(None) if ride is None else ride(dn_call)
    dw_gu = _mm_tn(n, dgu, tk=1024, tn=1408, tm=512, out_dtype=BF, name=f"ffn{tag}_dwgu")
    dx, dxb, dg = _rms_bwd(h, norm, dn, dh, name=f"ffn{tag}_dnorm")
    return dx, dxb, dg, dw_gu, dw_down


def _local_step(x, tgt, w, mats, fetch, exchange):
    S = x.shape[0]
    rope_f = _rope_tables(S)
    rope_b = (rope_f[0], -rope_f[1], -rope_f[2])
    g, partial, landed = {}, {}, {}
    w = dict(w, ffn_w_gu={}, ffn_w_down={})

    def bring(call, indices):
        bufs = [mats[wi] for wi in indices]
        if fetch is None:
            return call(None), bufs
        return call(fetch(indices, bufs))

    def ride(call, indices):
        guest = exchange(indices, [partial[wi] for wi in indices]) if indices else None
        res = call(guest)
        if guest is None:
            return res
        res, outs = res
        landed.update(zip(indices, outs))
        return res

    n0 = _rms_fwd(x, w["a_norm"], name="a_norm")
    dils = [d for _, d in DILATED_PATTERNS]
    projs, (w["ffn_w_gu"][0], w["ffn_w_down"][0]) = bring(
        lambda guest: _mm_nn(n0, w["a_w_in"], tm=512, tn=1024, out_dtype=BF, name="a_proj", rope=rope_f, guest=guest,
                             groups=dils), [4, 6])
    block = lambda t, dil: (lambda r: t * dil + r)
    o_parts, lse_parts = [], []
    for gi, (window, dil) in enumerate(DILATED_PATTERNS):
        pv = projs[gi]
        attend = lambda guest: _band_fwd(pv, pv, pv, block(0, dil), block(1, dil), block(2, dil), dil=dil, T=128,
                                         window=window // dil, name=f"a_attn{gi}", guest=guest)
        if gi == 1:
            (o_g, lse_g), (b_in, w["b_w_out"]) = bring(attend, [2, 3])
        else:
            o_g, lse_g = attend(None)
        o_parts.append(o_g)
        lse_parts.append(lse_g)
    b_in = b_in.transpose(1, 0, 2).reshape(D_MODEL, -1)
    w["b_w_qkv"] = b_in[:, :QKV_COLS]
    w["b_w_f"] = jnp.pad(b_in[:, QKV_COLS:], ((0, 0), (0, GATE_LANES + QKV_COLS - b_in.shape[1])))
    mixed = _combine_groups(o_parts, lse_parts, dils, name="a_combine")
    o_a = mixed[1][0]
    h1 = _mm_nn(o_a, w["a_w_out"], tm=512, tn=1024, out_dtype=F32, name="a_out", resid=x)
    h2, ffn0 = _ffn_fwd(h1, w["ffn_norm"][0:1], w["ffn_w_gu"][0], w["ffn_w_down"][0], 0)

    n2 = _rms_fwd(h2, w["b_norm"], name="b_norm")
    qkv = _mm_nn(n2, w["b_w_qkv"], tm=512, tn=1024, out_dtype=BF, name="b_proj")
    zf = _mm_nn(n2, w["b_w_f"], tm=512, tn=GATE_LANES, out_dtype=F32, name="b_gate_proj")
    _, cT = _gate_fwd(zf, w["b_f"], name="b_gate")
    (o_b, lse_b), (w["ffn_w_gu"][1], w["ffn_w_down"][1]) = bring(lambda guest: _fox_fwd(qkv, cT, name="b_attn", guest=guest), [5, 7])
    h3 = _mm_nn(o_b, w["b_w_out"], tm=512, tn=1024, out_dtype=F32, name="b_out", resid=h2)
    h4, ffn1 = _ffn_fwd(h3, w["ffn_norm"][1:2], w["ffn_w_gu"][1], w["ffn_w_down"][1], 1)

    loss, dh4, dh4b, g["final_norm"] = _loss_head(h4, w["final_norm"], tgt, name="loss_head")

    dh3, dh3b, dg_f1, partial[5], partial[7] = _ffn_bwd(dh4, dh4b, ffn1, w["ffn_norm"][1:2], w["ffn_w_gu"][1], w["ffn_w_down"][1], 1)

    do_b = _mm_nt(dh3b, w["b_w_out"], tm=512, to=1024, tn=1024, out_dtype=BF, name="b_do")
    partial[3] = _mm_tn(o_b, dh3b, tk=1024, tn=1024, tm=512, out_dtype=BF, name="b_dwout")
    dq, dk, dv, dcT, dcq = ride(lambda guest: _fox_bwd(qkv, cT, do_b, o_b, lse_b, name="b_attn_bwd", guest=guest), [5, 7])
    dz, g["b_f"] = _gate_bwd(zf, w["b_f"], dcT, dcq, name="b_gate_bwd")
    dqkv = _assemble([dq, dk, dv], [False] * 3, None, [1] * 3, name="b_dproj")
    dn2 = _mm_nt(dz, w["b_w_f"], tm=512, to=1024, tn=GATE_LANES, out_dtype=F32, name="b_dn_gate")
    dn2 = _mm_nt(dqkv, w["b_w_qkv"], tm=512, to=1024, tn=1024, out_dtype=F32, name="b_dn", add=dn2)
    g_qkv = _mm_tn(n2, dqkv, tk=1024, tn=1024, tm=512, out_dtype=BF, name="b_dwqkv")
    g_f = _mm_tn(n2, dz, tk=1024, tn=GATE_LANES, tm=512, out_dtype=BF, name="b_dwf")
    g_b_in = jnp.concatenate([g_qkv, g_f[:, :N_HEADS]], axis=1)
    partial[2] = g_b_in.reshape(D_MODEL, N_CHIPS, -1).transpose(1, 0, 2)
    dh2, dh2b, g["b_norm"] = _rms_bwd(h2, w["b_norm"], dn2, dh3, name="b_dnorm")

    dh1, dh1b, dg_f0, partial[4], partial[6] = _ffn_bwd(dh2, dh2b, ffn0, w["ffn_norm"][0:1], w["ffn_w_gu"][0], w["ffn_w_down"][0], 0,
                                                      ride=lambda call: ride(call, [2, 3]))
    g["ffn_norm"] = jnp.concatenate([dg_f0, dg_f1], axis=0)

    views = tuple(sorted(set(dils)))
    do_a = dict(zip(views, _mm_nt(dh1b, w["a_w_out"], tm=512, to=1024, tn=1024, out_dtype=BF, name="a_do", views=views)))
    partial[1] = _mm_tn(o_a, dh1b, tk=1024, tn=1024, tm=512, out_dtype=BF, name="a_dwout")
    riders = {0: [4], 1: [6, 1], 2: []}
    parts = []
    for gi, (window, dil) in enumerate(DILATED_PATTERNS):
        pv = projs[gi]
        res = ride(lambda guest: _band_bwd(pv, pv, pv, block(0, dil), block(1, dil), block(2, dil), do_a[dil],
                                           mixed[dil][0], mixed[dil][1], dil=dil, T=128,
                                           window=window // dil, name=f"a_attn_bwd{gi}", guest=guest), riders[gi])
        parts += list(res)
    dproj = _assemble(parts, [True, True, False] * 3, rope_b, [d for _, d in DILATED_PATTERNS for _ in range(3)], name="a_dproj")
    partial[0] = _mm_tn(n0, dproj, tk=1024, tn=1024, tm=512, out_dtype=BF, name="a_dwin")
    dn0 = ride(lambda guest: _mm_nt(dproj, w["a_w_in"], tm=512, to=1024, tn=1024, out_dtype=F32, name="a_dn", guest=guest), [0])
    dx, _, g["a_norm"] = _rms_bwd(x, w["a_norm"], dn0, dh1, name="a_dnorm")
    return loss, dx, g, partial, landed


ANY = pl.BlockSpec(memory_space=pl.ANY)


def _place():
    x, y, c = lax.axis_index("x"), lax.axis_index("y"), lax.axis_index("c")
    chips = [(1 - x, y), (x, 1 - y), (1 - x, 1 - y)]
    return x, y, c, chips


def _shard_slice(ref, kind, rows, cols, s, half):
    hr = rows // 2
    if kind == "col":
        return ref.at[pl.ds(half * hr, hr), pl.ds(pl.multiple_of(s * cols, 128), cols)]
    if kind == "row":
        return ref.at[pl.ds(pl.multiple_of(s * rows + half * hr, 16), hr), :]
    return ref.at[s, pl.ds(half * hr, hr), :]


def _whole_shape(kind, rows, cols):
    return {"col": (rows, N_CHIPS * cols), "row": (N_CHIPS * rows, cols), "stack": (N_CHIPS, rows, cols)}[kind]


def _own_block(kind, rows, tr, cols):
    per = rows // tr

    def spec(half_rows):
        off = (lambda p: 0) if half_rows is None else (lambda p: p[1] * (half_rows // tr))
        if kind == "col":
            return pl.BlockSpec((tr, cols), lambda i, p: (off(p) + i, p[0]))
        if kind == "row":
            return pl.BlockSpec((tr, cols), lambda i, p: (p[0] * per + off(p) + i, 0))
        return pl.BlockSpec((None, tr, cols), lambda i, p: (p[0], off(p) + i, 0))
    return spec


def _place_shard(shards, layer, kind, place, *, name):
    _, rows, cols = shards.shape
    tr = 256 if rows % 256 == 0 else rows // 2

    def body(p_ref, s_ref, o_ref):
        o_ref[...] = s_ref[...].astype(BF)

    return pl.pallas_call(
        body,
        grid_spec=pltpu.PrefetchScalarGridSpec(
            num_scalar_prefetch=1, grid=(rows // tr,),
            in_specs=[pl.BlockSpec((None, tr, cols), lambda i, p: (layer, i, 0))],
            out_specs=_own_block(kind, rows, tr, cols)(None)),
        out_shape=jax.ShapeDtypeStruct(_whole_shape(kind, rows, cols), BF),
        name=name, compiler_params=_params("arbitrary"),
    )(place, shards)


def _gather_weights(placed, kinds, dims):
    nw = len(placed)

    def body(*refs):
        dst = refs[nw:2 * nw]
        send_sems, recv_sems = refs[2 * nw:]
        x, y, c, chips = _place()
        me = 2 * x + y
        sibling = (x, y, 1 - c)

        def copy(wi, k, s, half, to):
            p = _shard_slice(dst[wi], kinds[wi], dims[wi][0], dims[wi][1], s, half)
            return pltpu.make_async_remote_copy(src_ref=p, dst_ref=p, send_sem=send_sems.at[wi * 6 + k],
                                                recv_sem=recv_sems.at[wi * 6 + k], device_id=to, device_id_type=MESH)

        first, passed = [], []
        for wi in range(nw):
            for j, chip in enumerate(chips):
                cp = copy(wi, j, me, c, (*chip, c))
                cp.start()
                first.append(cp)
        for wi in range(nw):
            for j, chip in enumerate(chips):
                s = 2 * chip[0] + chip[1]
                copy(wi, j, s, c, (x, y, c)).wait_recv()
                cp = copy(wi, 3 + j, s, c, sibling)
                cp.start()
                passed.append(cp)
        for wi in range(nw):
            for j, chip in enumerate(chips):
                s = 2 * chip[0] + chip[1]
                copy(wi, 3 + j, s, 1 - c, (x, y, c)).wait_recv()
        for cp in first + passed:
            cp.wait_send()

    return pl.pallas_call(
        body, in_specs=[ANY] * nw, out_specs=[ANY] * nw,
        out_shape=[jax.ShapeDtypeStruct(p.shape, p.dtype) for p in placed],
        input_output_aliases={wi: wi for wi in range(nw)},
        scratch_shapes=[pltpu.SemaphoreType.DMA((nw * 6,)), pltpu.SemaphoreType.DMA((nw * 6,))],
        name="gather_weights",
    )(*placed)


def _fetch_guest(placed, kinds, dims):
    nw = len(placed)

    def copies(dst, send_sems, recv_sems, incoming):
        x, y, c, chips = _place()
        out = []
        for wi in range(nw):
            for j, chip in enumerate(chips):
                s = 2 * chip[0] + chip[1] if incoming else 2 * x + y
                to = (x, y, c) if incoming else (*chip, c)
                for half in range(2):
                    p = _shard_slice(dst[wi], kinds[wi], dims[wi][0], dims[wi][1], s, half)
                    k = wi * 6 + 2 * j + half
                    out.append(pltpu.make_async_remote_copy(src_ref=p, dst_ref=p, send_sem=send_sems.at[k],
                                                            recv_sem=recv_sems.at[k], device_id=to, device_id_type=MESH))
        return out

    def start(src, dst, sems):
        for cp in copies(dst, sems[0], sems[1], False):
            cp.start()

    def finish(src, dst, sems):
        for cp in copies(dst, sems[0], sems[1], True):
            cp.wait_recv()
        for cp in copies(dst, sems[0], sems[1], False):
            cp.wait_send()

    return dict(args=list(placed), out_shape=[jax.ShapeDtypeStruct(p.shape, p.dtype) for p in placed],
                scratch=[pltpu.SemaphoreType.DMA((nw * 6,)), pltpu.SemaphoreType.DMA((nw * 6,))],
                start=start, finish=finish, in_place=True)


def _scatter_guest(partials, kinds, dims):
    nw = len(partials)

    def copies(src, send_sems, recv_sems, dst):
        x, y, c, chips = _place()
        me = 2 * x + y
        out = []
        for wi in range(nw):
            rows, cols = dims[wi]

            def part(s, half, wi=wi, rows=rows, cols=cols):
                return _shard_slice(src[wi], kinds[wi], rows, cols, s, half)

            for j, chip in enumerate(chips):
                s = 2 * chip[0] + chip[1]
                for half in range(2):
                    slot = 2 * j + (c if half == 0 else 1 - c)
                    out.append(pltpu.make_async_remote_copy(
                        src_ref=part(s, half), dst_ref=dst[wi].at[slot],
                        send_sem=send_sems.at[wi * 7 + 2 * j + half], recv_sem=recv_sems.at[wi * 7 + slot],
                        device_id=(*chip, half), device_id_type=MESH))
            out.append(pltpu.make_async_remote_copy(
                src_ref=part(me, 1 - c), dst_ref=dst[wi].at[6],
                send_sem=send_sems.at[wi * 7 + 6], recv_sem=recv_sems.at[wi * 7 + 6],
                device_id=(x, y, 1 - c), device_id_type=MESH))
        return out

    def start(src, dst, sems):
        for cp in copies(src, sems[0], sems[1], dst):
            cp.start()

    def finish(src, dst, sems):
        x, y, c, _ = _place()
        for wi in range(nw):
            for slot in range(7):
                pltpu.make_async_remote_copy(
                    src_ref=dst[wi].at[slot], dst_ref=dst[wi].at[slot],
                    send_sem=sems[0].at[wi * 7 + slot], recv_sem=sems[1].at[wi * 7 + slot],
                    device_id=(x, y, c), device_id_type=MESH).wait_recv()
        for cp in copies(src, sems[0], sems[1], dst):
            cp.wait_send()

    return dict(args=list(partials), out_shape=[jax.ShapeDtypeStruct((7, d[0] // 2, d[1]), BF) for d in dims],
                scratch=[pltpu.SemaphoreType.DMA((nw * 7,)), pltpu.SemaphoreType.DMA((nw * 7,))],
                start=start, finish=finish)


def _scatter_grads(partials, kinds, dims, *, name):
    guest = _scatter_guest(partials, kinds, dims)
    nw = len(partials)

    def body(*refs):
        parts = (refs[:nw], refs[nw:2 * nw], refs[2 * nw:])
        guest["start"](*parts)
        guest["finish"](*parts)

    return pl.pallas_call(body, in_specs=[ANY] * nw, out_specs=[ANY] * nw, out_shape=guest["out_shape"],
                          scratch_shapes=guest["scratch"], name=name)(*partials)


def _sum_slots(slots, partial, kind, dims, place, *, name, into=None, layer=None, n_layers=1):
    rows, cols = dims
    hr = rows // 2
    tr = hr if 8 * hr * cols * 2 <= 6 * 1024 * 1024 else 128
    assert hr % tr == 0

    def body(p_ref, b_ref, own_ref, *rest):
        o_ref = rest[-1]
        acc = own_ref[...].astype(F32)
        for k in range(7):
            acc = acc + b_ref[k].astype(F32)
        o_ref[...] = acc

    half = lambda p: p[1] * (hr // tr)
    if n_layers == 1:
        out_spec = pl.BlockSpec((tr, cols), lambda i, p: (half(p) + i, 0))
        out_shape = jax.ShapeDtypeStruct((rows, cols), F32)
    else:
        out_spec = pl.BlockSpec((None, tr, cols), lambda i, p: (layer, half(p) + i, 0))
        out_shape = jax.ShapeDtypeStruct((n_layers, rows, cols), F32)
    in_specs = [pl.BlockSpec((7, tr, cols), lambda i, p: (0, i, 0)), _own_block(kind, rows, tr, cols)(hr)]
    args = [place, slots, partial]
    aliases = {}
    if into is not None:
        in_specs.append(ANY)
        args.append(into)
        aliases = {3: 0}
    return pl.pallas_call(
        body,
        grid_spec=pltpu.PrefetchScalarGridSpec(num_scalar_prefetch=1, grid=(hr // tr,), in_specs=in_specs, out_specs=out_spec),
        out_shape=out_shape, input_output_aliases=aliases, name=name, compiler_params=_params("arbitrary"),
    )(*args)


def _pair_exchange(bufs, members):
    nw = len(members)

    def body(*refs):
        dst = refs[len(bufs):2 * len(bufs)]
        send_sems, recv_sems = refs[2 * len(bufs):]
        x, y, c, _ = _place()

        def rows_of(wi, half):
            bi, l = members[wi]
            ref = dst[bi] if l is None else dst[bi].at[l]
            hr = ref.shape[0] // 2
            return ref.at[pl.ds(pl.multiple_of(half * hr, 8), hr), :]

        def copy(wi, half, to):
            p = rows_of(wi, half)
            return pltpu.make_async_remote_copy(src_ref=p, dst_ref=p, send_sem=send_sems.at[wi], recv_sem=recv_sems.at[wi],
                                                device_id=to, device_id_type=MESH)

        sent = []
        for wi in range(nw):
            cp = copy(wi, c, (x, y, 1 - c))
            cp.start()
            sent.append(cp)
        for wi in range(nw):
            copy(wi, 1 - c, (x, y, c)).wait_recv()
        for cp in sent:
            cp.wait_send()

    return pl.pallas_call(
        body, in_specs=[ANY] * len(bufs), out_specs=[ANY] * len(bufs),
        out_shape=[jax.ShapeDtypeStruct(b.shape, b.dtype) for b in bufs],
        input_output_aliases={i: i for i in range(len(bufs))},
        scratch_shapes=[pltpu.SemaphoreType.DMA((nw,)), pltpu.SemaphoreType.DMA((nw,))],
        name="pair_exchange",
    )(*bufs)


SMALL_ROWS = 8


def _allreduce_small(v, *, name):
    assert v.shape == (SMALL_ROWS, D_MODEL)

    def body(v_ref, o_ref, buf, send_sems, recv_sems):
        x, y, c, _ = _place()
        me = 4 * x + 2 * y + c
        buf[me] = v_ref[...]
        sent = []
        for k in range(1, 8):
            bx, by, bc = (k >> 2) & 1, (k >> 1) & 1, k & 1
            peer = (1 - x if bx else x, 1 - y if by else y, 1 - c if bc else c)
            cp = pltpu.make_async_remote_copy(src_ref=v_ref, dst_ref=buf.at[me], send_sem=send_sems.at[k - 1],
                                              recv_sem=recv_sems.at[k - 1], device_id=peer, device_id_type=MESH)
            cp.start()
            sent.append(cp)
        for k in range(1, 8):
            bx, by, bc = (k >> 2) & 1, (k >> 1) & 1, k & 1
            peer = 4 * (1 - x if bx else x) + 2 * (1 - y if by else y) + (1 - c if bc else c)
            pltpu.make_async_remote_copy(src_ref=v_ref, dst_ref=buf.at[peer], send_sem=send_sems.at[k - 1],
                                         recv_sem=recv_sems.at[k - 1], device_id=(x, y, c), device_id_type=MESH).wait_recv()
        for cp in sent:
            cp.wait_send()
        acc = buf[0]
        for d in range(1, 8):
            acc = acc + buf[d]
        o_ref[...] = acc

    vmem = pl.BlockSpec(memory_space=pltpu.VMEM)
    return pl.pallas_call(
        body, in_specs=[vmem], out_specs=vmem, out_shape=jax.ShapeDtypeStruct(v.shape, F32),
        scratch_shapes=[pltpu.VMEM((8,) + v.shape, F32), pltpu.SemaphoreType.DMA((7,)), pltpu.SemaphoreType.DMA((7,))],
        name=name,
    )(v)


def _adamw(w, g, m, v, *, name):
    R, C = w.shape
    tr = R
    if R * C * 4 > 1024 * 1024:
        tr = max(t for t in range(8, R, 8) if R % t == 0 and t * C * 4 <= 1024 * 1024)

    def body(w_ref, g_ref, m_ref, v_ref, d_ref, m2_ref, v2_ref):
        gg = g_ref[...]
        m2 = ADAM_B1 * m_ref[...] + (1.0 - ADAM_B1) * gg
        v2 = ADAM_B2 * v_ref[...] + (1.0 - ADAM_B2) * jnp.square(gg)
        m_hat = m2 / (1.0 - ADAM_B1 ** ADAM_STEP)
        v_hat = v2 / (1.0 - ADAM_B2 ** ADAM_STEP)
        d_ref[...] = -ADAM_LR * (m_hat / (jnp.sqrt(v_hat) + ADAM_EPS) + ADAM_WD * w_ref[...])
        m2_ref[...] = m2
        v2_ref[...] = v2

    blk = pl.BlockSpec((tr, C), lambda i: (i, 0))
    out = jax.ShapeDtypeStruct((R, C), F32)
    return pl.pallas_call(
        body, grid=(R // tr,), in_specs=[blk] * 4, out_specs=[blk] * 3, out_shape=[out] * 3,
        name=name, compiler_params=_params("parallel"),
    )(w, g, m, v)


WEIGHT_ORDER = ("a_norm", "a_w_in", "a_w_out", "b_norm", "b_w_in", "b_f", "b_w_out", "ffn_norm", "ffn_w_gu",
                "ffn_w_down", "final_norm")
MATRICES = (("a_w_in", 0, "col"), ("a_w_out", 0, "row"), ("b_w_in", 0, "stack"), ("b_w_out", 0, "row"),
            ("ffn_w_gu", 0, "col"), ("ffn_w_gu", 1, "col"), ("ffn_w_down", 0, "row"), ("ffn_w_down", 1, "row"))
MATRIX_GROUPS = ([0], [1], [2], [3], [4, 5], [6, 7])
GROUP_NAMES = ("a_w_in", "a_w_out", "b_w_in", "b_w_out", "ffn_w_gu", "ffn_w_down")
QKV_COLS = 3 * N_HEADS * HEAD_DIM


def kernel(x, a_norm, a_w_in, a_w_out, b_norm, b_w_in, b_f, b_w_out, ffn_norm, ffn_w_gu, ffn_w_down, final_norm, loss_target, m_a_norm, m_a_w_in, m_a_w_out, m_b_norm, m_b_w_in, m_b_f, m_b_w_out, m_ffn_norm, m_ffn_w_gu, m_ffn_w_down, m_final_norm, v_a_norm, v_a_w_in, v_a_w_out, v_b_norm, v_b_w_in, v_b_f, v_b_w_out, v_ffn_norm, v_ffn_w_gu, v_ffn_w_down, v_final_norm):
    given = dict(a_norm=a_norm, a_w_in=a_w_in, a_w_out=a_w_out, b_norm=b_norm, b_w_in=b_w_in, b_f=b_f, b_w_out=b_w_out,
                 ffn_norm=ffn_norm, ffn_w_gu=ffn_w_gu, ffn_w_down=ffn_w_down, final_norm=final_norm)
    mom_m = dict(a_norm=m_a_norm, a_w_in=m_a_w_in, a_w_out=m_a_w_out, b_norm=m_b_norm, b_w_in=m_b_w_in, b_f=m_b_f,
                 b_w_out=m_b_w_out, ffn_norm=m_ffn_norm, ffn_w_gu=m_ffn_w_gu, ffn_w_down=m_ffn_w_down, final_norm=m_final_norm)
    mom_v = dict(a_norm=v_a_norm, a_w_in=v_a_w_in, a_w_out=v_a_w_out, b_norm=v_b_norm, b_w_in=v_b_w_in, b_f=v_b_f,
                 b_w_out=v_b_w_out, ffn_norm=v_ffn_norm, ffn_w_gu=v_ffn_w_gu, ffn_w_down=v_ffn_w_down, final_norm=v_final_norm)
    chip = 2 * lax.axis_index("x") + lax.axis_index("y")
    core = lax.axis_index("c")
    bn_cols = b_norm.shape[1]

    placed = lax.dynamic_update_slice(jnp.zeros((SMALL_ROWS, D_MODEL), F32), b_norm, (0, chip * bn_cols))
    placed = placed * (core == 0).astype(F32)
    b_norm_full = _allreduce_small(placed, name="gather_b_norm")[0:1]

    place = jnp.stack([chip, core]).astype(jnp.int32)
    kinds = [k for _, _, k in MATRICES]
    dims = [given[n].shape[1:] for n, _, _ in MATRICES]
    placed = [_place_shard(given[n], l, k, place, name=f"place_{n}{l}") for n, l, k in MATRICES]
    first = _gather_weights(placed[:2], kinds[:2], dims[:2])
    mats = dict(enumerate(list(first) + placed[2:]))
    gate_cols = b_f.shape[1]
    w = dict(a_norm=a_norm, a_w_in=mats[0], a_w_out=mats[1], b_norm=b_norm_full,
             b_f=jnp.pad(b_f, ((0, 0), (0, GATE_LANES - gate_cols))), ffn_norm=ffn_norm,
             final_norm=final_norm.reshape(1, D_MODEL))

    def fetch(indices, bufs):
        return _fetch_guest(bufs, [kinds[i] for i in indices], [dims[i] for i in indices])

    def exchange(indices, parts):
        return _scatter_guest(parts, [kinds[i] for i in indices], [dims[i] for i in indices])

    loss, dx, g, partials, slots = _local_step(x[0], loss_target[0], w, mats, fetch, exchange)
    bufs, members = [], []
    for group in MATRIX_GROUPS:
        buf = None
        for l, wi in enumerate(group):
            n = MATRICES[wi][0]
            buf = _sum_slots(slots[wi], partials[wi], kinds[wi], dims[wi], place, name=f"sum_{n}{l}", into=buf,
                             layer=l, n_layers=len(group))
            members.append((len(bufs), l if len(group) > 1 else None))
        bufs.append(buf)
    reduced = dict(zip(GROUP_NAMES, _pair_exchange(bufs, members)))

    small = jnp.concatenate([g["a_norm"], g["b_norm"], g["ffn_norm"], g["final_norm"],
                             jnp.pad(g["b_f"], ((0, 0), (0, D_MODEL - GATE_LANES))),
                             jnp.zeros((SMALL_ROWS - 6, D_MODEL), F32)], axis=0)
    small = _allreduce_small(small, name="allreduce_small")
    grads = dict(reduced)
    grads["a_norm"] = small[0:1]
    grads["b_norm"] = lax.dynamic_slice(small, (1, chip * bn_cols), (1, bn_cols))
    grads["ffn_norm"] = small[2:4]
    grads["final_norm"] = small[4]
    grads["b_f"] = small[5:6, :gate_cols]

    out_g, out_d, out_m, out_v = [], [], [], []
    for n in WEIGHT_ORDER:
        shape = given[n].shape
        two_d = (1, shape[0]) if len(shape) == 1 else (-1, shape[-1])
        d, m2, v2 = _adamw(given[n].reshape(two_d), grads[n].reshape(two_d), mom_m[n].reshape(two_d),
                           mom_v[n].reshape(two_d), name=f"adamw_{n}")
        out_g.append(grads[n].reshape(shape))
        out_d.append(d.reshape(shape))
        out_m.append(m2.reshape(shape))
        out_v.append(v2.reshape(shape))

    total = lax.psum(loss[0, 0], MESH_AXES)
    return (total, dx[None], *out_g, *out_d, *out_m, *out_v)
```

```python
import functools

import jax
import jax.numpy as jnp
from jax import lax
from jax.experimental import pallas as pl
from jax.experimental.pallas import tpu as pltpu

F32 = jnp.float32
BF = jnp.bfloat16

D_MODEL = 1024
N_HEADS = 16
HEAD_DIM = 64
D_FF = 2816
DILATED_PATTERNS = ((128, 1), (512, 4), (2048, 16))
ROT_DIM = 16
ROPE_THETA = 500000.0
RMS_EPS = 1e-6
NEG_INF = -1e30
ATTN_SCALE = HEAD_DIM ** -0.5
GATE_LANES = 128
N_CHIPS = 4
MESH_AXES = ("x", "y", "c")
MESH = pl.DeviceIdType.MESH

ADAM_LR = 0.001
ADAM_B1 = 0.9
ADAM_B2 = 0.999
ADAM_EPS = 1e-08
ADAM_WD = 0.01
ADAM_STEP = 10

VMEM_LIMIT_BYTES = 56 * 1024 * 1024


def _params(*sem):
    return pltpu.CompilerParams(dimension_semantics=sem, vmem_limit_bytes=VMEM_LIMIT_BYTES)


def _hosted_call(body, *, grid, in_specs, out_specs, out_shape, scratch_shapes, args, name, guest=None, schedule=()):
    params = _params(*(["arbitrary"] * len(grid)))
    ns = len(schedule)

    def call(kernel, in_specs, out_specs, out_shape, scratch_shapes, aliases, args):
        spec = pltpu.PrefetchScalarGridSpec(num_scalar_prefetch=ns, grid=grid, in_specs=in_specs, out_specs=out_specs,
                                            scratch_shapes=scratch_shapes)
        return pl.pallas_call(kernel, grid_spec=spec, out_shape=out_shape, input_output_aliases=aliases, name=name,
                              compiler_params=params)(*schedule, *args)

    if guest is None:
        return call(body, in_specs, out_specs, out_shape, scratch_shapes, {}, args)
    n_in, n_out, n_scr = ns + len(in_specs), len(out_specs), len(scratch_shapes)
    g_in, g_out = len(guest["args"]), len(guest["out_shape"])
    any_spec = pl.BlockSpec(memory_space=pl.ANY)

    def wrapped(*refs):
        i1 = n_in + g_in
        o1 = i1 + n_out
        o2 = o1 + g_out
        s1 = o2 + n_scr
        guest_refs = (refs[n_in:i1], refs[o1:o2], refs[s1:])
        ids = [pl.program_id(d) for d in range(len(grid))]
        first = functools.reduce(jnp.logical_and, [i == 0 for i in ids])
        last = functools.reduce(jnp.logical_and, [i == g - 1 for i, g in zip(ids, grid)])

        @pl.when(first)
        def _():
            guest["start"](*guest_refs)

        body(*refs[:n_in], *refs[i1:o1], *refs[o2:s1])

        @pl.when(last)
        def _():
            guest["finish"](*guest_refs)

    aliases = {n_in + k: n_out + k for k in range(g_in)} if guest.get("in_place") else {}
    return call(wrapped, list(in_specs) + [any_spec] * g_in, list(out_specs) + [any_spec] * g_out,
                list(out_shape) + list(guest["out_shape"]), list(scratch_shapes) + list(guest["scratch"]), aliases,
                list(args) + list(guest["args"]))


def _rope_rotate(t, cos, sin_a, sin_b):
    outs = []
    for cidx in range(t.shape[1] // 128):
        tc = t[:, cidx * 128:(cidx + 1) * 128]
        outs.append(tc * cos + pltpu.roll(tc, 120, 1) * sin_a + pltpu.roll(tc, 8, 1) * sin_b)
    return jnp.concatenate(outs, axis=1)


def _mm_nn(a, b, *, tm, tn, out_dtype, name, resid=None, rope=None, guest=None, groups=None):
    M, K = a.shape
    N = b.shape[1]
    assert M % tm == 0 and N % tn == 0 and b.shape[0] == K
    n_in = 2 + (resid is not None) + (3 if rope is not None else 0)
    if groups is not None:
        assert rope is not None and N == 3 * tn * len(groups)

    def body(*refs):
        a_ref, b_ref = refs[0], refs[1]
        o_ref = refs[n_in]
        acc = jnp.dot(a_ref[...], b_ref[...], preferred_element_type=F32)
        if resid is not None:
            acc = acc + refs[2][...]
        if groups is not None:
            cos_ref, sa_ref, sb_ref = refs[n_in - 3:n_in]
            j = pl.program_id(1)
            for g, d in enumerate(groups):
                for is_v in (False, True):
                    @pl.when(jnp.logical_and(j // 3 == g, (j % 3 == 2) == is_v))
                    def _(g=g, d=d, is_v=is_v):
                        val = acc if is_v else _rope_rotate(acc, cos_ref[...], sa_ref[...], sb_ref[...])
                        if d == 1:
                            refs[n_in + g][...] = val.astype(out_dtype)
                        else:
                            _to_view(val, refs[-1], refs[n_in + g], d, tn)
        elif rope is not None:
            cos_ref, sa_ref, sb_ref = refs[n_in - 3:n_in]
            j = pl.program_id(1)

            @pl.when(j % 3 != 2)
            def _():
                o_ref[...] = _rope_rotate(acc, cos_ref[...], sa_ref[...], sb_ref[...]).astype(out_dtype)

            @pl.when(j % 3 == 2)
            def _():
                o_ref[...] = acc.astype(out_dtype)
        else:
            o_ref[...] = acc.astype(out_dtype)

    in_specs = [pl.BlockSpec((tm, K), lambda i, j: (i, 0)), pl.BlockSpec((K, tn), lambda i, j: (0, j))]
    args = [a, b]
    if resid is not None:
        in_specs.append(pl.BlockSpec((tm, tn), lambda i, j: (i, j)))
        args.append(resid)
    if rope is not None:
        assert tn == 1024
        for t in rope:
            in_specs.append(pl.BlockSpec((tm, 128), lambda i, j: (i, 0)))
            args.append(t)
    if groups is None:
        out_specs = [pl.BlockSpec((tm, tn), lambda i, j: (i, j))]
        out_shape = [jax.ShapeDtypeStruct((M, N), out_dtype)]
        scratch = []
    else:
        out_specs = [pl.BlockSpec((tm // d, d * tn), lambda i, j, g=g: (i, jnp.clip(j - 3 * g, 0, 2)))
                     for g, d in enumerate(groups)]
        out_shape = [jax.ShapeDtypeStruct((M // d, d * 3 * tn), out_dtype) for d in groups]
        scratch = [pltpu.VMEM((tn // 128, tm, 128), F32)]
    outs = _hosted_call(body, grid=(M // tm, N // tn), in_specs=in_specs, out_specs=out_specs, out_shape=out_shape,
                        scratch_shapes=scratch, args=args, name=name, guest=guest)
    nout = len(out_shape)
    res = outs[0] if groups is None else list(outs[:nout])
    return res if guest is None else (res, outs[nout:])


def _mm_nt(a, b, *, tm, to, tn, out_dtype, name, add=None, guest=None, views=(1,)):
    M, N = a.shape
    O = b.shape[0]
    assert M % tm == 0 and O % to == 0 and N % tn == 0 and b.shape[1] == N
    nk = N // tn

    def body(*refs):
        a_ref, b_ref = refs[0], refs[1]
        n_in = 2 + (add is not None)
        o_refs = refs[n_in:n_in + len(views)]
        acc_ref = refs[n_in + len(views)]
        k = pl.program_id(2)

        @pl.when(k == 0)
        def _():
            if add is not None:
                acc_ref[...] = refs[2][...]
            else:
                acc_ref[...] = jnp.zeros_like(acc_ref)

        acc_ref[...] += lax.dot_general(a_ref[...], b_ref[...], (((1,), (1,)), ((), ())),
                                        preferred_element_type=F32)

        @pl.when(k == nk - 1)
        def _():
            for o_ref, d in zip(o_refs, views):
                if d == 1:
                    o_ref[...] = acc_ref[...].astype(out_dtype)
                else:
                    _to_view(acc_ref[...], refs[-1], o_ref, d, to)

    in_specs = [pl.BlockSpec((tm, tn), lambda i, j, k: (i, k)), pl.BlockSpec((to, tn), lambda i, j, k: (j, k))]
    args = [a, b]
    if add is not None:
        in_specs.append(pl.BlockSpec((tm, to), lambda i, j, k: (i, j)))
        args.append(add)
    assert views == (1,) or (to == O and to % 128 == 0)
    scratch = [pltpu.VMEM((tm, to), F32)] + ([pltpu.VMEM((to // 128, tm, 128), F32)] if views != (1,) else [])
    outs = _hosted_call(
        body, grid=(M // tm, O // to, nk), in_specs=in_specs,
        out_specs=[pl.BlockSpec((tm, to), lambda i, j, k: (i, j)) if d == 1 else _view_spec(tm, d, to) for d in views],
        out_shape=[jax.ShapeDtypeStruct((M // d, d * O), out_dtype) for d in views],
        scratch_shapes=scratch, args=args, name=name, guest=guest)
    nv = len(views)
    res = outs[0] if nv == 1 else list(outs[:nv])
    return res if guest is None else (res, outs[nv:])


def _mm_tn(a, b, *, tk, tn, tm, out_dtype, name):
    M, K = a.shape
    N = b.shape[1]
    assert M % tm == 0 and K % tk == 0 and N % tn == 0 and b.shape[0] == M
    nm = M // tm

    def body(a_ref, b_ref, o_ref, acc_ref):
        m = pl.program_id(2)

        @pl.when(m == 0)
        def _():
            acc_ref[...] = jnp.zeros_like(acc_ref)

        acc_ref[...] += lax.dot_general(a_ref[...], b_ref[...], (((0,), (0,)), ((), ())),
                                        preferred_element_type=F32)

        @pl.when(m == nm - 1)
        def _():
            o_ref[...] = acc_ref[...].astype(out_dtype)

    return pl.pallas_call(
        body, grid=(K // tk, N // tn, nm),
        in_specs=[pl.BlockSpec((tm, tk), lambda i, j, m: (m, i)), pl.BlockSpec((tm, tn), lambda i, j, m: (m, j))],
        out_specs=pl.BlockSpec((tk, tn), lambda i, j, m: (i, j)),
        out_shape=jax.ShapeDtypeStruct((K, N), out_dtype),
        scratch_shapes=[pltpu.VMEM((tk, tn), F32)], name=name,
        compiler_params=_params("parallel", "parallel", "arbitrary"),
    )(a, b)


ROW_TILE = 512


def _rms_fwd(x, g, *, name):
    S, Dm = x.shape

    def body(x_ref, g_ref, o_ref):
        xf = x_ref[...]
        r = lax.rsqrt(jnp.mean(xf * xf, axis=-1, keepdims=True) + RMS_EPS)
        o_ref[...] = (xf * r * g_ref[...]).astype(BF)

    return pl.pallas_call(
        body, grid=(S // ROW_TILE,),
        in_specs=[pl.BlockSpec((ROW_TILE, Dm), lambda i: (i, 0)), pl.BlockSpec((1, Dm), lambda i: (0, 0))],
        out_specs=pl.BlockSpec((ROW_TILE, Dm), lambda i: (i, 0)),
        out_shape=jax.ShapeDtypeStruct((S, Dm), BF), name=name, compiler_params=_params("parallel"),
    )(x, g)


def _rms_bwd(x, g, dn, dres, *, name):
    S, Dm = x.shape

    def body(x_ref, g_ref, dn_ref, dres_ref, dx_ref, dxb_ref, dg_ref):
        i = pl.program_id(0)
        xf = x_ref[...]
        r = lax.rsqrt(jnp.mean(xf * xf, axis=-1, keepdims=True) + RMS_EPS)
        xh = xf * r
        dnf = dn_ref[...]
        dyg = dnf * g_ref[...]
        dx = dres_ref[...] + r * (dyg - xh * jnp.mean(dyg * xh, axis=-1, keepdims=True))
        dx_ref[...] = dx
        dxb_ref[...] = dx.astype(BF)

        @pl.when(i == 0)
        def _():
            dg_ref[...] = jnp.zeros_like(dg_ref)

        dg_ref[...] += jnp.sum(dnf * xh, axis=0, keepdims=True)

    row = pl.BlockSpec((ROW_TILE, Dm), lambda i: (i, 0))
    vec = pl.BlockSpec((1, Dm), lambda i: (0, 0))
    return pl.pallas_call(
        body, grid=(S // ROW_TILE,), in_specs=[row, vec, row, row], out_specs=[row, row, vec],
        out_shape=[jax.ShapeDtypeStruct((S, Dm), F32), jax.ShapeDtypeStruct((S, Dm), BF),
                   jax.ShapeDtypeStruct((1, Dm), F32)],
        name=name, compiler_params=_params("arbitrary"),
    )(x, g, dn, dres)


def _loss_head(h, g, tgt, *, name):
    S, Dm = h.shape

    def body(h_ref, g_ref, t_ref, loss_ref, dh_ref, dhb_ref, dg_ref):
        i = pl.program_id(0)
        xf = h_ref[...]
        r = lax.rsqrt(jnp.mean(xf * xf, axis=-1, keepdims=True) + RMS_EPS)
        xh = xf * r
        gv = g_ref[...]
        err = xh * gv - t_ref[...]
        dy = err * (1.0 / Dm)
        dyg = dy * gv
        dh = r * (dyg - xh * jnp.mean(dyg * xh, axis=-1, keepdims=True))
        dh_ref[...] = dh
        dhb_ref[...] = dh.astype(BF)

        @pl.when(i == 0)
        def _():
            dg_ref[...] = jnp.zeros_like(dg_ref)
            loss_ref[...] = jnp.zeros_like(loss_ref)

        dg_ref[...] += jnp.sum(dy * xh, axis=0, keepdims=True)
        part = 0.5 * jnp.sum(jnp.mean(err * err, axis=-1, keepdims=True), axis=0, keepdims=True)
        loss_ref[...] += jnp.broadcast_to(part, loss_ref.shape)

    row = pl.BlockSpec((ROW_TILE, Dm), lambda i: (i, 0))
    vec = pl.BlockSpec((1, Dm), lambda i: (0, 0))
    return pl.pallas_call(
        body, grid=(S // ROW_TILE,), in_specs=[row, vec, row],
        out_specs=[pl.BlockSpec((1, 128), lambda i: (0, 0)), row, row, vec],
        out_shape=[jax.ShapeDtypeStruct((1, 128), F32), jax.ShapeDtypeStruct((S, Dm), F32),
                   jax.ShapeDtypeStruct((S, Dm), BF), jax.ShapeDtypeStruct((1, Dm), F32)],
        name=name, compiler_params=_params("arbitrary"),
    )(h, g, tgt)


SWIGLU_ROWS = 256


def _swiglu_fwd(gu, *, name):
    S = gu.shape[0]

    def body(g_ref, u_ref, o_ref):
        g = g_ref[...].astype(F32)
        sig = 1.0 / (1.0 + jnp.exp(-g))
        o_ref[...] = (g * sig * u_ref[...].astype(F32)).astype(BF)

    return pl.pallas_call(
        body, grid=(S // SWIGLU_ROWS,),
        in_specs=[pl.BlockSpec((SWIGLU_ROWS, D_FF), lambda i: (i, 0)), pl.BlockSpec((SWIGLU_ROWS, D_FF), lambda i: (i, 1))],
        out_specs=pl.BlockSpec((SWIGLU_ROWS, D_FF), lambda i: (i, 0)),
        out_shape=jax.ShapeDtypeStruct((S, D_FF), BF), name=name, compiler_params=_params("parallel"),
    )(gu, gu)


def _swiglu_bwd(gu, dact, *, name):
    S = gu.shape[0]

    def body(g_ref, u_ref, d_ref, o_ref):
        g = g_ref[...].astype(F32)
        u = u_ref[...].astype(F32)
        d = d_ref[...].astype(F32)
        sig = 1.0 / (1.0 + jnp.exp(-g))
        o_ref[:, :D_FF] = (d * u * sig * (1.0 + g * (1.0 - sig))).astype(BF)
        o_ref[:, D_FF:] = (d * g * sig).astype(BF)

    return pl.pallas_call(
        body, grid=(S // SWIGLU_ROWS,),
        in_specs=[pl.BlockSpec((SWIGLU_ROWS, D_FF), lambda i: (i, 0)), pl.BlockSpec((SWIGLU_ROWS, D_FF), lambda i: (i, 1)),
                  pl.BlockSpec((SWIGLU_ROWS, D_FF), lambda i: (i, 0))],
        out_specs=pl.BlockSpec((SWIGLU_ROWS, 2 * D_FF), lambda i: (i, 0)),
        out_shape=jax.ShapeDtypeStruct((S, 2 * D_FF), BF), name=name, compiler_params=_params("parallel"),
    )(gu, gu, dact)


def _band_masks(T, n):
    row = lax.broadcasted_iota(jnp.int32, (T, T), 0)
    col = lax.broadcasted_iota(jnp.int32, (T, T), 1)
    return jnp.logical_and(col >= row, n > 0), col <= row


def _band_fwd(qa, ka, va, qcb, kcb, vcb, *, dil, T, window, name, guest=None):
    L = qa.shape[0]
    nq = L // T
    assert window == T
    nt = (((1,), (1,)), ((), ()))

    def body(q_ref, kp_ref, kc_ref, vp_ref, vc_ref, o_ref, lse_ref):
        valid_prev, valid_cur = _band_masks(T, pl.program_id(1))
        lane = lax.broadcasted_iota(jnp.int32, (T, 128), 1)
        low = lane < HEAD_DIM
        ones = jnp.ones((T, 128), BF)
        lse = jnp.zeros((T, 128), F32)
        def scores(h):
            ps = slice((h // 2) * 128, (h // 2 + 1) * 128)
            qp = q_ref[:, ps] * jnp.asarray(ATTN_SCALE, BF)
            qm = jnp.where(low if h % 2 == 0 else jnp.logical_not(low), qp, jnp.zeros_like(qp))
            s0 = jnp.where(valid_prev, lax.dot_general(qm, kp_ref[:, ps], nt, preferred_element_type=F32), NEG_INF)
            s1 = jnp.where(valid_cur, lax.dot_general(qm, kc_ref[:, ps], nt, preferred_element_type=F32), NEG_INF)
            return s0, s1

        def softmax(s0, s1):
            m = jnp.maximum(jnp.max(s0, axis=1, keepdims=True), jnp.max(s1, axis=1, keepdims=True))
            return m, jnp.exp(s0 - m).astype(BF), jnp.exp(s1 - m).astype(BF)

        def weighted(h, p0, p1):
            ps = slice((h // 2) * 128, (h // 2 + 1) * 128)
            l = jnp.dot(p0, ones, preferred_element_type=F32) + jnp.dot(p1, ones, preferred_element_type=F32)
            acc = jnp.dot(p0, vp_ref[:, ps], preferred_element_type=F32) + jnp.dot(p1, vc_ref[:, ps], preferred_element_type=F32)
            return l, acc

        sc, pr, even = {}, {}, None
        for t in range(N_HEADS + 2):
            if t < N_HEADS:
                sc[t] = scores(t)
            done = None
            if t >= 2:
                m, p0, p1 = pr.pop(t - 2)
                done = (m,) + weighted(t - 2, p0, p1)
            if 1 <= t <= N_HEADS:
                pr[t - 1] = softmax(*sc.pop(t - 1))
            if done is not None:
                h = t - 2
                m, l, acc = done
                lse = jnp.where(lane == h, m + jnp.log(l), lse)
                if h % 2 == 0:
                    even = acc / l
                else:
                    o_ref[:, (h // 2) * 128:(h // 2 + 1) * 128] = jnp.where(low, even, acc / l)
        lse_ref[...] = lse

    def prev(n):
        return jnp.maximum(n - 1, 0)

    blk = lambda f, cb: pl.BlockSpec((T, 1024), lambda r, n: (f(n), cb(r)))
    same = lambda n: n
    outs = _hosted_call(
        body, grid=(dil, nq),
        in_specs=[blk(same, qcb), blk(prev, kcb), blk(same, kcb), blk(prev, vcb), blk(same, vcb)],
        out_specs=[pl.BlockSpec((T, 1024), lambda r, n: (n, r)), pl.BlockSpec((T, 128), lambda r, n: (n, r))],
        out_shape=[jax.ShapeDtypeStruct((L, dil * 1024), F32), jax.ShapeDtypeStruct((L, dil * 128), F32)],
        scratch_shapes=[], args=(qa, ka, ka, va, va), name=name, guest=guest)
    return outs if guest is None else (outs[:2], outs[2:])


def _band_bwd(qa, ka, va, qcb, kcb, vcb, doa, oa, lsea, *, dil, T, window, name, guest=None):
    L = qa.shape[0]
    nq = L // T
    assert window == T
    nt = (((1,), (1,)), ((), ()))
    tn = (((0,), (0,)), ((), ()))

    def body(q_ref, kp_ref, kc_ref, vp_ref, vc_ref, do_ref, o_ref, lse_ref, dq_ref, dk_ref, dv_ref, ck_sc, cv_sc):
        n = pl.program_id(1)

        @pl.when(n == 0)
        def _():
            ck_sc[...] = jnp.zeros_like(ck_sc)
            cv_sc[...] = jnp.zeros_like(cv_sc)

        @pl.when(n < nq)
        def _():
            valid_prev, valid_cur = _band_masks(T, n)
            low = lax.broadcasted_iota(jnp.int32, (T, 128), 1) < HEAD_DIM
            dot = functools.partial(lax.dot_general, preferred_element_type=F32)

            def pair(h):
                return slice((h // 2) * 128, (h // 2 + 1) * 128)

            def products(h):
                ps = pair(h)
                mask = low if h % 2 == 0 else jnp.logical_not(low)
                qp = q_ref[:, ps] * jnp.asarray(ATTN_SCALE, BF)
                dop = do_ref[:, ps]
                qm = jnp.where(mask, qp, jnp.zeros_like(qp))
                dom = jnp.where(mask, dop, jnp.zeros_like(dop))
                s0 = jnp.where(valid_prev, dot(qm, kp_ref[:, ps], nt), NEG_INF)
                s1 = jnp.where(valid_cur, dot(qm, kc_ref[:, ps], nt), NEG_INF)
                return qm, dom, s0, s1, dot(dom, vp_ref[:, ps], nt), dot(dom, vc_ref[:, ps], nt)

            def pointwise(h, qm, dom, s0, s1, dp0, dp1):
                ps = pair(h)
                mask = low if h % 2 == 0 else jnp.logical_not(low)
                prod = do_ref[:, ps].astype(F32) * o_ref[:, ps].astype(F32)
                delta = jnp.sum(jnp.where(mask, prod, 0.0), axis=1, keepdims=True)
                lse = lse_ref[:, h:h + 1]
                p0 = jnp.exp(s0 - lse)
                p1 = jnp.exp(s1 - lse)
                ds0 = (p0 * (dp0 - delta)).astype(BF)
                ds1 = (p1 * (dp1 - delta)).astype(BF)
                return qm, dom, p0.astype(BF), p1.astype(BF), ds0, ds1

            def gradients(h, qm, dom, p0, p1, ds0, ds1):
                ps = pair(h)
                dq = dot(ds0, kp_ref[:, ps], (((1,), (0,)), ((), ()))) + dot(ds1, kc_ref[:, ps], (((1,), (0,)), ((), ())))
                return dq, dot(ds0, qm, tn), dot(p0, dom, tn), dot(ds1, qm, tn), dot(p1, dom, tn)

            st1, st2, even = {}, {}, None
            for t in range(N_HEADS + 2):
                if t < N_HEADS:
                    st1[t] = products(t)
                done = gradients(t - 2, *st2.pop(t - 2)) if t >= 2 else None
                if 1 <= t <= N_HEADS:
                    st2[t - 1] = pointwise(t - 1, *st1.pop(t - 1))
                if done is not None:
                    h = t - 2
                    if h % 2 == 0:
                        even = done
                    else:
                        ps = pair(h)
                        dq_ref[:, ps] = (jnp.where(low, even[0], done[0]) * ATTN_SCALE).astype(BF)
                        dk_ref[:, ps] = (ck_sc[:, ps] + even[1] + done[1]).astype(BF)
                        dv_ref[:, ps] = (cv_sc[:, ps] + even[2] + done[2]).astype(BF)
                        ck_sc[:, ps] = even[3] + done[3]
                        cv_sc[:, ps] = even[4] + done[4]

        @pl.when(n == nq)
        def _():
            dk_ref[...] = ck_sc[...].astype(BF)
            dv_ref[...] = cv_sc[...].astype(BF)

    def cur(n):
        return jnp.minimum(n, nq - 1)

    def prev(n):
        return jnp.maximum(cur(n) - 1, 0)

    blk = lambda f, cb: pl.BlockSpec((T, 1024), lambda r, n: (f(n), cb(r)))
    own = lambda r: r
    outs = _hosted_call(
        body, grid=(dil, nq + 1),
        in_specs=[blk(cur, qcb), blk(prev, kcb), blk(cur, kcb), blk(prev, vcb), blk(cur, vcb), blk(cur, own), blk(cur, own),
                  pl.BlockSpec((T, 128), lambda r, n: (cur(n), r))],
        out_specs=[blk(cur, own), blk(lambda n: jnp.maximum(n - 1, 0), own), blk(lambda n: jnp.maximum(n - 1, 0), own)],
        out_shape=[jax.ShapeDtypeStruct((L, dil * 1024), BF)] * 3,
        scratch_shapes=[pltpu.VMEM((T, 1024), F32), pltpu.VMEM((T, 1024), F32)],
        args=(qa, ka, ka, va, va, doa, oa, lsea), name=name, guest=guest)
    return outs if guest is None else (outs[:3], outs[3:])


FOX_T = 256
FOX_TK = 512
FOX_TQ_BWD = 512
FOX_ROWS = 128


def _fox_fwd(qkv, cT, *, name, guest=None):
    S = qkv.shape[0]
    T, TK, R = FOX_T, FOX_TK, FOX_ROWS
    nq = S // T
    nt = (((1,), (1,)), ((), ()))
    chains = [(h, rh) for h in range(N_HEADS) for rh in range(T // R)]
    pairs = [(n, j) for n in range(nq) for j in range((n * T + T - 1) // TK + 1)]
    schedule = [jnp.asarray([p[i] for p in pairs], jnp.int32) for i in range(2)]

    def body(n_tab, j_tab, q_ref, k_ref, v_ref, ct_ref, o_ref, lse_ref, m_sc, l_sc, acc_sc):
        n = n_tab[pl.program_id(0)]
        j = j_tab[pl.program_id(0)]
        last_j = (n * T + T - 1) // TK
        lane = lax.broadcasted_iota(jnp.int32, (R, 128), 1)
        low = lane < HEAD_DIM
        ones = jnp.ones((TK, 128), BF)

        @pl.when(j == 0)
        def _():
            m_sc[...] = jnp.full(m_sc.shape, NEG_INF, F32)
            l_sc[...] = jnp.zeros_like(l_sc)
            acc_sc[...] = jnp.zeros_like(acc_sc)

        def step(diagonal):
            def pair(h):
                return slice((h // 2) * 128, (h // 2 + 1) * 128)

            def rows(rh):
                return slice(rh * R, (rh + 1) * R)

            def scores(h, rh):
                qp = q_ref[rows(rh), pair(h)] * jnp.asarray(ATTN_SCALE, BF)
                qm = jnp.where(low if h % 2 == 0 else jnp.logical_not(low), qp, jnp.zeros_like(qp))
                s = lax.dot_general(qm, k_ref[:, pair(h)], nt, preferred_element_type=F32) - ct_ref[h:h + 1, :]
                if diagonal:
                    ahead = lax.broadcasted_iota(jnp.int32, (R, TK), 1) - lax.broadcasted_iota(jnp.int32, (R, TK), 0)
                    s = jnp.where(ahead <= n * T + rh * R - j * TK, s, NEG_INF)
                return s

            def softmax(h, rh, s):
                m_prev = m_sc[h, rows(rh), :]
                m_new = jnp.maximum(m_prev, jnp.max(s, axis=1, keepdims=True))
                p = jnp.exp(s - jnp.concatenate([m_new] * (TK // 128), axis=1)).astype(BF)
                return m_new, jnp.exp(m_prev - m_new), p

            def weighted(h, p):
                vx = jnp.concatenate([v_ref[:, pair(h)], ones], axis=1)
                return jnp.dot(p, vx, preferred_element_type=F32)

            sc, pr, even = {}, {}, {}
            nch = len(chains)
            for t in range(nch + 2):
                if t < nch:
                    sc[t] = scores(*chains[t])
                done = None
                if t >= 2:
                    m_new, alpha, p = pr.pop(t - 2)
                    done = (m_new, alpha, weighted(chains[t - 2][0], p))
                if 1 <= t <= nch:
                    pr[t - 1] = softmax(*chains[t - 1], sc.pop(t - 1))
                if done is not None:
                    h, rh = chains[t - 2]
                    m_new, alpha, pv = done
                    m_sc[h, rows(rh), :] = m_new
                    l_sc[h, rows(rh), :] = alpha * l_sc[h, rows(rh), :] + pv[:, 128:]
                    if h % 2 == 0:
                        even[rh] = (alpha, pv[:, :128])
                    else:
                        a0, pv0 = even.pop(rh)
                        acc = acc_sc[h // 2, rows(rh), :]
                        acc_sc[h // 2, rows(rh), :] = jnp.where(low, a0 * acc + pv0, alpha * acc + pv[:, :128])

        @pl.when(j < last_j)
        def _():
            step(False)

        @pl.when(j == last_j)
        def _():
            step(True)
            lane_t = lax.broadcasted_iota(jnp.int32, (T, 128), 1)
            low_t = lane_t < HEAD_DIM
            lse = jnp.zeros((T, 128), F32)
            for h in range(N_HEADS):
                lse = jnp.where(lane_t == h, m_sc[h] + jnp.log(l_sc[h]), lse)
            lse_ref[...] = lse
            for hp in range(N_HEADS // 2):
                inv = jnp.where(low_t, 1.0 / l_sc[2 * hp], 1.0 / l_sc[2 * hp + 1])
                o_ref[:, hp * 128:(hp + 1) * 128] = (acc_sc[hp] * inv).astype(BF)

    outs = _hosted_call(
        body, grid=(len(pairs),),
        in_specs=[pl.BlockSpec((T, 1024), lambda t, n, j: (n[t], 0)), pl.BlockSpec((TK, 1024), lambda t, n, j: (j[t], 1)),
                  pl.BlockSpec((TK, 1024), lambda t, n, j: (j[t], 2)), pl.BlockSpec((GATE_LANES, TK), lambda t, n, j: (0, j[t]))],
        out_specs=[pl.BlockSpec((T, 1024), lambda t, n, j: (n[t], 0)), pl.BlockSpec((T, 128), lambda t, n, j: (n[t], 0))],
        out_shape=[jax.ShapeDtypeStruct((S, 1024), BF), jax.ShapeDtypeStruct((S, 128), F32)],
        scratch_shapes=[pltpu.VMEM((N_HEADS, T, 128), F32), pltpu.VMEM((N_HEADS, T, 128), F32),
                        pltpu.VMEM((N_HEADS // 2, T, 128), F32)],
        args=(qkv, qkv, qkv, cT), name=name, guest=guest, schedule=schedule)
    return outs if guest is None else (outs[:2], outs[2:])


def _fox_bwd(qkv, cT, do, o, lse, *, name, guest=None):
    S = qkv.shape[0]
    T, TQ, R = FOX_T, FOX_TQ_BWD, FOX_ROWS
    nk, nq = S // T, S // TQ
    nt = (((1,), (1,)), ((), ()))
    tn = (((0,), (0,)), ((), ()))
    nn = (((1,), (0,)), ((), ()))
    chains = [(h, rh) for h in range(N_HEADS) for rh in range(TQ // R)]
    dot = functools.partial(lax.dot_general, preferred_element_type=F32)
    pairs = [(kb, qb) for kb in range(nk) for qb in range(kb * T // TQ, nq)]
    schedule = [jnp.asarray([p[i] for p in pairs], jnp.int32) for i in range(2)]

    def body(kb_tab, qb_tab, q_ref, k_ref, v_ref, ct_ref, do_ref, o_ref, lse_ref, dq_ref, dk_ref, dv_ref, dct_ref, dcq_ref,
             dq_sc, dk_sc, dv_sc, dc_sc, dcq_sc):
        kb = kb_tab[pl.program_id(0)]
        qb = qb_tab[pl.program_id(0)]
        jq = qb - kb * T // TQ
        lane = lax.broadcasted_iota(jnp.int32, (R, 128), 1)
        low = lane < HEAD_DIM
        ones_k = jnp.ones((T, 128), BF)
        ones_r = jnp.ones((8, R), BF)

        @pl.when(jnp.logical_and(kb == 0, jq == 0))
        def _():
            dq_sc[...] = jnp.zeros_like(dq_sc)
            dcq_sc[...] = jnp.zeros_like(dcq_sc)

        @pl.when(jq == 0)
        def _():
            dk_sc[...] = jnp.zeros_like(dk_sc)
            dv_sc[...] = jnp.zeros_like(dv_sc)
            dc_sc[...] = jnp.zeros_like(dc_sc)

        def step(diagonal):
            def pair(h):
                return slice((h // 2) * 128, (h // 2 + 1) * 128)

            def rows(rh):
                return slice(rh * R, (rh + 1) * R)

            def qrows(rh):
                return pl.ds(pl.multiple_of(qb * TQ + rh * R, R), R)

            def products(h, rh):
                mask = low if h % 2 == 0 else jnp.logical_not(low)
                qp = q_ref[rows(rh), pair(h)] * jnp.asarray(ATTN_SCALE, BF)
                dop = do_ref[rows(rh), pair(h)]
                qm = jnp.where(mask, qp, jnp.zeros_like(qp))
                dom = jnp.where(mask, dop, jnp.zeros_like(dop))
                s = dot(qm, k_ref[:, pair(h)], nt) - ct_ref[h:h + 1, :]
                if diagonal:
                    ahead = lax.broadcasted_iota(jnp.int32, (R, T), 1) - lax.broadcasted_iota(jnp.int32, (R, T), 0)
                    s = jnp.where(ahead <= qb * TQ + rh * R - kb * T, s, NEG_INF)
                return qm, dom, s, dot(dom, v_ref[:, pair(h)], nt)

            def pointwise(h, rh, qm, dom, s, dp):
                mask = low if h % 2 == 0 else jnp.logical_not(low)
                prod = do_ref[rows(rh), pair(h)].astype(F32) * o_ref[rows(rh), pair(h)].astype(F32)
                delta = jnp.sum(jnp.where(mask, prod, 0.0), axis=1, keepdims=True)
                p = jnp.exp(s - lse_ref[rows(rh), h:h + 1])
                ds = (p * (dp - delta)).astype(BF)
                return qm, dom, p.astype(BF), ds

            def gradients(h, qm, dom, p, ds):
                kx = jnp.concatenate([k_ref[:, pair(h)], ones_k], axis=1)
                return dot(ds, kx, nn), dot(qm, ds, tn), dot(dom, p, tn), dot(ones_r, ds, nn)

            st1, st2, even = {}, {}, {}
            dcq_tiles = [jnp.zeros((R, 128), F32) for _ in range(TQ // R)]
            nch = len(chains)
            for t in range(nch + 2):
                if t < nch:
                    st1[t] = products(*chains[t])
                done = gradients(chains[t - 2][0], *st2.pop(t - 2)) if t >= 2 else None
                if 1 <= t <= nch:
                    st2[t - 1] = pointwise(*chains[t - 1], *st1.pop(t - 1))
                if done is not None:
                    h, rh = chains[t - 2]
                    dq_rsum, dk, dv, csum = done
                    dq = dq_rsum[:, :128]
                    dcq_tiles[rh] = jnp.where(lane == h, dq_rsum[:, 128:], dcq_tiles[rh])
                    dc_sc[h:h + 1, :] -= csum[0:1, :]
                    if h % 2 == 0:
                        even[rh] = (dq, dk, dv)
                    else:
                        dq0, dk0, dv0 = even.pop(rh)
                        dq_sc[qrows(rh), pair(h)] += jnp.where(low, dq0, dq) * ATTN_SCALE
                        dk_sc[h // 2] += dk0 + dk
                        dv_sc[h // 2] += dv0 + dv
            for rh in range(TQ // R):
                dcq_sc[qrows(rh), :] += dcq_tiles[rh]

        @pl.when(jq > 0)
        def _():
            step(False)

        @pl.when(jq == 0)
        def _():
            step(True)

        @pl.when(qb == nq - 1)
        def _():
            for hp in range(N_HEADS // 2):
                dk_ref[:, hp * 128:(hp + 1) * 128] = dk_sc[hp].T.astype(BF)
                dv_ref[:, hp * 128:(hp + 1) * 128] = dv_sc[hp].T.astype(BF)
            dct_ref[...] = dc_sc[...]

        @pl.when(jnp.logical_and(kb == nk - 1, qb == nq - 1))
        def _():
            def put(i, carry):
                r = pl.ds(pl.multiple_of(i * T, T), T)
                dq_ref[r, :] = dq_sc[r, :].astype(BF)
                return carry
            lax.fori_loop(0, nk, put, 0)
            dcq_ref[...] = dcq_sc[...]

    qblk = lambda col: pl.BlockSpec((TQ, 1024), lambda t, kb, qb: (qb[t], col))
    kblk = lambda col: pl.BlockSpec((T, 1024), lambda t, kb, qb: (kb[t], col))
    whole = pl.BlockSpec((S, 1024), lambda t, kb, qb: (0, 0))
    outs = _hosted_call(
        body, grid=(len(pairs),),
        in_specs=[qblk(0), kblk(1), kblk(2), pl.BlockSpec((GATE_LANES, T), lambda t, kb, qb: (0, kb[t])), qblk(0), qblk(0),
                  pl.BlockSpec((TQ, 128), lambda t, kb, qb: (qb[t], 0))],
        out_specs=[whole, kblk(0), kblk(0), pl.BlockSpec((GATE_LANES, T), lambda t, kb, qb: (0, kb[t])),
                   pl.BlockSpec((S, GATE_LANES), lambda t, kb, qb: (0, 0))],
        out_shape=[jax.ShapeDtypeStruct((S, 1024), BF)] * 3 + [jax.ShapeDtypeStruct((GATE_LANES, S), F32),
                                                               jax.ShapeDtypeStruct((S, GATE_LANES), F32)],
        scratch_shapes=[pltpu.VMEM((S, 1024), F32), pltpu.VMEM((N_HEADS // 2, 128, T), F32), pltpu.VMEM((N_HEADS // 2, 128, T), F32),
                        pltpu.VMEM((GATE_LANES, T), F32), pltpu.VMEM((S, GATE_LANES), F32)],
        args=(qkv, qkv, qkv, cT, do, o, lse), name=name, guest=guest, schedule=schedule)
    return outs if guest is None else (outs[:5], outs[5:])


def _to_natural(src_ref, buf, d, width):
    rows = buf.shape[1]
    for r in range(d):
        for ch in range(width // 128):
            lanes = slice(r * width + ch * 128, r * width + (ch + 1) * 128)
            buf.at[ch][pl.ds(r, rows // d, stride=d), :] = src_ref[:, lanes].astype(F32)
    return jnp.concatenate([buf[ch] for ch in range(width // 128)], axis=1)


def _to_view(val, buf, dst_ref, d, width):
    rows = buf.shape[1]
    for ch in range(width // 128):
        buf[ch] = val[:, ch * 128:(ch + 1) * 128]
    for r in range(d):
        for ch in range(width // 128):
            lanes = slice(r * width + ch * 128, r * width + (ch + 1) * 128)
            dst_ref[:, lanes] = buf.at[ch][pl.ds(r, rows // d, stride=d), :].astype(dst_ref.dtype)


def _view_spec(rows, d, width):
    return pl.BlockSpec((rows // d, d * width), lambda i, *_: (i, 0))


def _combine_groups(os, lses, dils, *, name):
    ng = len(os)
    S = os[0].shape[0] * dils[0]
    tm = ROW_TILE
    views = sorted(set(dils))

    def body(*refs):
        o_refs, l_refs = refs[:ng], refs[ng:2 * ng]
        outs = refs[2 * ng:2 * ng + 2 * len(views)]
        wide, narrow = refs[-2], refs[-1]
        ls = [l_refs[g][...] if dils[g] == 1 else _to_natural(l_refs[g], narrow, dils[g], 128) for g in range(ng)]
        m = functools.reduce(jnp.maximum, ls)
        es = [jnp.exp(l - m) for l in ls]
        den = functools.reduce(jnp.add, es)
        ws = [e / den for e in es]
        lse = m + jnp.log(den)
        og = [o_refs[g][...] if dils[g] == 1 else _to_natural(o_refs[g], wide, dils[g], 1024) for g in range(ng)]
        cols = []
        for h in range(N_HEADS):
            hs = slice(h * HEAD_DIM, (h + 1) * HEAD_DIM)
            acc = ws[0][:, h:h + 1] * og[0][:, hs]
            for g in range(1, ng):
                acc = acc + ws[g][:, h:h + 1] * og[g][:, hs]
            cols.append(acc)
        o = jnp.concatenate(cols, axis=1)
        for k, d in enumerate(views):
            if d == 1:
                outs[2 * k][...] = o.astype(BF)
                outs[2 * k + 1][...] = lse
            else:
                _to_view(o, wide, outs[2 * k], d, 1024)
                _to_view(lse, narrow, outs[2 * k + 1], d, 128)

    out_specs, out_shape = [], []
    for d in views:
        out_specs += [_view_spec(tm, d, 1024), _view_spec(tm, d, 128)]
        out_shape += [jax.ShapeDtypeStruct((S // d, d * 1024), BF), jax.ShapeDtypeStruct((S // d, d * 128), F32)]
    res = pl.pallas_call(
        body, grid=(S // tm,), in_specs=[_view_spec(tm, d, 1024) for d in dils] + [_view_spec(tm, d, 128) for d in dils],
        out_specs=out_specs, out_shape=out_shape,
        scratch_shapes=[pltpu.VMEM((8, tm, 128), F32), pltpu.VMEM((1, tm, 128), F32)],
        name=name, compiler_params=_params("parallel"),
    )(*os, *lses)
    return {d: (res[2 * k], res[2 * k + 1]) for k, d in enumerate(views)}


def _assemble(parts, rope_flags, rope, dils, *, name):
    n = len(parts)
    S = parts[0].shape[0] * dils[0]
    use_rope = any(rope_flags)
    tm = 256

    def body(*refs):
        out_ref, natural = refs[-2], refs[-1]
        for b in range(n):
            cols = slice(b * 1024, (b + 1) * 1024)
            d = dils[b]
            val = refs[b][...].astype(F32) if d == 1 else _to_natural(refs[b], natural, d, 1024)
            if rope_flags[b]:
                cos_ref, sa_ref, sb_ref = refs[n:n + 3]
                val = _rope_rotate(val, cos_ref[...], sa_ref[...], sb_ref[...])
            out_ref[:, cols] = val.astype(BF)

    in_specs = [_view_spec(tm, d, 1024) for d in dils]
    args = list(parts)
    if use_rope:
        in_specs += [pl.BlockSpec((tm, 128), lambda i: (i, 0))] * 3
        args += list(rope)
    return pl.pallas_call(
        body, grid=(S // tm,), in_specs=in_specs, out_specs=pl.BlockSpec((tm, n * 1024), lambda i: (i, 0)),
        out_shape=jax.ShapeDtypeStruct((S, n * 1024), BF), scratch_shapes=[pltpu.VMEM((8, tm, 128), F32)],
        name=name, compiler_params=_params("parallel"),
    )(*args)


GATE_ROWS = 512


def _gate_fwd(z, bf, *, name):
    S = z.shape[0]

    def body(z_ref, b_ref, ct_ref, carry):
        i = pl.program_id(0)

        @pl.when(i == 0)
        def _():
            carry[...] = jnp.zeros_like(carry)

        zz = z_ref[...] + b_ref[...]
        logf = jnp.minimum(zz, 0.0) - jnp.log(1.0 + jnp.exp(-jnp.abs(zz)))
        tri = (lax.broadcasted_iota(jnp.int32, (GATE_ROWS, GATE_ROWS), 0)
               >= lax.broadcasted_iota(jnp.int32, (GATE_ROWS, GATE_ROWS), 1)).astype(F32)
        cs = jnp.dot(tri, logf, precision=lax.Precision.HIGHEST, preferred_element_type=F32) + carry[...]
        ct_ref[...] = cs.T
        carry[...] = cs[GATE_ROWS - 1:GATE_ROWS, :]

    return pl.pallas_call(
        body, grid=(S // GATE_ROWS,),
        in_specs=[pl.BlockSpec((GATE_ROWS, GATE_LANES), lambda i: (i, 0)), pl.BlockSpec((1, GATE_LANES), lambda i: (0, 0))],
        out_specs=pl.BlockSpec((GATE_LANES, GATE_ROWS), lambda i: (0, i)),
        out_shape=jax.ShapeDtypeStruct((GATE_LANES, S), F32),
        scratch_shapes=[pltpu.VMEM((1, GATE_LANES), F32)], name=name, compiler_params=_params("arbitrary"),
    )(z, bf)


def _gate_bwd(z, bf, dcT, dcq, *, name):
    S = z.shape[0]
    nb = S // GATE_ROWS

    def body(z_ref, b_ref, dct_ref, dcq_ref, dz_ref, db_ref, carry):
        i = pl.program_id(0)

        @pl.when(i == 0)
        def _():
            carry[...] = jnp.zeros_like(carry)
            db_ref[...] = jnp.zeros_like(db_ref)

        dc = dct_ref[...].T + dcq_ref[...]
        tri = (lax.broadcasted_iota(jnp.int32, (GATE_ROWS, GATE_ROWS), 0)
               <= lax.broadcasted_iota(jnp.int32, (GATE_ROWS, GATE_ROWS), 1)).astype(F32)
        dl = jnp.dot(tri, dc, precision=lax.Precision.HIGHEST, preferred_element_type=F32) + carry[...]
        carry[...] = dl[0:1, :]
        zz = z_ref[...] + b_ref[...]
        dz = dl * (1.0 / (1.0 + jnp.exp(zz)))
        lane = lax.broadcasted_iota(jnp.int32, dz.shape, 1)
        dz = jnp.where(lane < N_HEADS, dz, 0.0)
        dz_ref[...] = dz.astype(BF)
        db_ref[...] += jnp.sum(dz, axis=0, keepdims=True)

    return pl.pallas_call(
        body, grid=(nb,),
        in_specs=[pl.BlockSpec((GATE_ROWS, GATE_LANES), lambda i: (nb - 1 - i, 0)), pl.BlockSpec((1, GATE_LANES), lambda i: (0, 0)),
                  pl.BlockSpec((GATE_LANES, GATE_ROWS), lambda i: (0, nb - 1 - i)),
                  pl.BlockSpec((GATE_ROWS, GATE_LANES), lambda i: (nb - 1 - i, 0))],
        out_specs=[pl.BlockSpec((GATE_ROWS, GATE_LANES), lambda i: (nb - 1 - i, 0)), pl.BlockSpec((1, GATE_LANES), lambda i: (0, 0))],
        out_shape=[jax.ShapeDtypeStruct((S, GATE_LANES), BF), jax.ShapeDtypeStruct((1, GATE_LANES), F32)],
        scratch_shapes=[pltpu.VMEM((1, GATE_LANES), F32)], name=name, compiler_params=_params("arbitrary"),
    )(z, bf, dcT, dcq)


def _rope_tables(S):
    half = ROT_DIM // 2
    inv_freq = ROPE_THETA ** (-jnp.arange(half, dtype=F32) * 2.0 / ROT_DIM)
    ang = jnp.arange(S, dtype=F32)[:, None] * inv_freq[None, :]
    cos, sin = jnp.cos(ang), jnp.sin(ang)
    zero = jnp.zeros((S, HEAD_DIM - ROT_DIM), F32)
    zh = jnp.zeros((S, half), F32)
    cos_h = jnp.concatenate([cos, cos, jnp.ones_like(zero)], axis=1)
    sa_h = jnp.concatenate([-sin, zh, zero], axis=1)
    sb_h = jnp.concatenate([zh, sin, zero], axis=1)
    two = lambda t: jnp.concatenate([t, t], axis=1)
    return two(cos_h), two(sa_h), two(sb_h)


def _ffn_fwd(h, norm, w_gu, w_down, tag):
    n = _rms_fwd(h, norm, name=f"ffn{tag}_norm")
    gu = _mm_nn(n, w_gu, tm=1024, tn=512, out_dtype=BF, name=f"ffn{tag}_gu")
    act = _swiglu_fwd(gu, name=f"ffn{tag}_act")
    out = _mm_nn(act, w_down, tm=512, tn=1024, out_dtype=F32, name=f"ffn{tag}_down", resid=h)
    return out, (h, n, gu, act)


def _ffn_bwd(dh, dhb, saved, norm, w_gu, w_down, tag, ride=None):
    h, n, gu, act = saved
    dact = _mm_nt(dhb, w_down, tm=512, to=1408, tn=1024, out_dtype=BF, name=f"ffn{tag}_dact")
    dw_down = _mm_tn(act, dhb, tk=1408, tn=1024, tm=512, out_dtype=BF, name=f"ffn{tag}_dwdown")
    dgu = _swiglu_bwd(gu, dact, name=f"ffn{tag}_dgu")
    dn_call = lambda guest: _mm_nt(dgu, w_gu, tm=512, to=1024, tn=1408, out_dtype=F32, name=f"ffn{tag}_dn", guest=guest)
    dn = dn_call(None) if ride is None else ride(dn_call)
    dw_gu = _mm_tn(n, dgu, tk=1024, tn=1408, tm=512, out_dtype=BF, name=f"ffn{tag}_dwgu")
    dx, dxb, dg = _rms_bwd(h, norm, dn, dh, name=f"ffn{tag}_dnorm")
    return dx, dxb, dg, dw_gu, dw_down


def _local_step(x, tgt, w, mats, fetch, exchange):
    S = x.shape[0]
    rope_f = _rope_tables(S)
    rope_b = (rope_f[0], -rope_f[1], -rope_f[2])
    g, partial, landed = {}, {}, {}
    w = dict(w, ffn_w_gu={}, ffn_w_down={})

    def bring(call, indices):
        bufs = [mats[wi] for wi in indices]
        if fetch is None:
            return call(None), bufs
        return call(fetch(indices, bufs))

    def ride(call, indices):
        guest = exchange(indices, [partial[wi] for wi in indices]) if indices else None
        res = call(guest)
        if guest is None:
            return res
        res, outs = res
        landed.update(zip(indices, outs))
        return res

    n0 = _rms_fwd(x, w["a_norm"], name="a_norm")
    dils = [d for _, d in DILATED_PATTERNS]
    projs, (w["ffn_w_gu"][0], w["ffn_w_down"][0]) = bring(
        lambda guest: _mm_nn(n0, w["a_w_in"], tm=512, tn=1024, out_dtype=BF, name="a_proj", rope=rope_f, guest=guest,
                             groups=dils), [4, 6])
    block = lambda t, dil: (lambda r: t * dil + r)
    o_parts, lse_parts = [], []
    for gi, (window, dil) in enumerate(DILATED_PATTERNS):
        pv = projs[gi]
        attend = lambda guest: _band_fwd(pv, pv, pv, block(0, dil), block(1, dil), block(2, dil), dil=dil, T=128,
                                         window=window // dil, name=f"a_attn{gi}", guest=guest)
        if gi == 1:
            (o_g, lse_g), (b_in, w["b_w_out"]) = bring(attend, [2, 3])
        else:
            o_g, lse_g = attend(None)
        o_parts.append(o_g)
        lse_parts.append(lse_g)
    b_in = b_in.transpose(1, 0, 2).reshape(D_MODEL, -1)
    w["b_w_qkv"] = b_in[:, :QKV_COLS]
    w["b_w_f"] = jnp.pad(b_in[:, QKV_COLS:], ((0, 0), (0, GATE_LANES + QKV_COLS - b_in.shape[1])))
    mixed = _combine_groups(o_parts, lse_parts, dils, name="a_combine")
    o_a = mixed[1][0]
    h1 = _mm_nn(o_a, w["a_w_out"], tm=512, tn=1024, out_dtype=F32, name="a_out", resid=x)
    h2, ffn0 = _ffn_fwd(h1, w["ffn_norm"][0:1], w["ffn_w_gu"][0], w["ffn_w_down"][0], 0)

    n2 = _rms_fwd(h2, w["b_norm"], name="b_norm")
    qkv = _mm_nn(n2, w["b_w_qkv"], tm=512, tn=1024, out_dtype=BF, name="b_proj")
    zf = _mm_nn(n2, w["b_w_f"], tm=512, tn=GATE_LANES, out_dtype=F32, name="b_gate_proj")
    cT = _gate_fwd(zf, w["b_f"], name="b_gate")
    (o_b, lse_b), (w["ffn_w_gu"][1], w["ffn_w_down"][1]) = bring(lambda guest: _fox_fwd(qkv, cT, name="b_attn", guest=guest), [5, 7])
    h3 = _mm_nn(o_b, w["b_w_out"], tm=512, tn=1024, out_dtype=F32, name="b_out", resid=h2)
    h4, ffn1 = _ffn_fwd(h3, w["ffn_norm"][1:2], w["ffn_w_gu"][1], w["ffn_w_down"][1], 1)

    loss, dh4, dh4b, g["final_norm"] = _loss_head(h4, w["final_norm"], tgt, name="loss_head")

    dh3, dh3b, dg_f1, partial[5], partial[7] = _ffn_bwd(dh4, dh4b, ffn1, w["ffn_norm"][1:2], w["ffn_w_gu"][1], w["ffn_w_down"][1], 1)

    do_b = _mm_nt(dh3b, w["b_w_out"], tm=512, to=1024, tn=1024, out_dtype=BF, name="b_do")
    partial[3] = _mm_tn(o_b, dh3b, tk=1024, tn=1024, tm=512, out_dtype=BF, name="b_dwout")
    dq, dk, dv, dcT, dcq = ride(lambda guest: _fox_bwd(qkv, cT, do_b, o_b, lse_b, name="b_attn_bwd", guest=guest), [5, 7])
    dz, g["b_f"] = _gate_bwd(zf, w["b_f"], dcT, dcq, name="b_gate_bwd")
    dqkv = _assemble([dq, dk, dv], [False] * 3, None, [1] * 3, name="b_dproj")
    dn2 = _mm_nt(dz, w["b_w_f"], tm=512, to=1024, tn=GATE_LANES, out_dtype=F32, name="b_dn_gate")
    dn2 = _mm_nt(dqkv, w["b_w_qkv"], tm=512, to=1024, tn=1024, out_dtype=F32, name="b_dn", add=dn2)
    g_qkv = _mm_tn(n2, dqkv, tk=1024, tn=1024, tm=512, out_dtype=BF, name="b_dwqkv")
    g_f = _mm_tn(n2, dz, tk=1024, tn=GATE_LANES, tm=512, out_dtype=BF, name="b_dwf")
    g_b_in = jnp.concatenate([g_qkv, g_f[:, :N_HEADS]], axis=1)
    partial[2] = g_b_in.reshape(D_MODEL, N_CHIPS, -1).transpose(1, 0, 2)
    dh2, dh2b, g["b_norm"] = _rms_bwd(h2, w["b_norm"], dn2, dh3, name="b_dnorm")

    dh1, dh1b, dg_f0, partial[4], partial[6] = _ffn_bwd(dh2, dh2b, ffn0, w["ffn_norm"][0:1], w["ffn_w_gu"][0], w["ffn_w_down"][0], 0,
                                                      ride=lambda call: ride(call, [2, 3]))
    g["ffn_norm"] = jnp.concatenate([dg_f0, dg_f1], axis=0)

    views = tuple(sorted(set(dils)))
    do_a = dict(zip(views, _mm_nt(dh1b, w["a_w_out"], tm=512, to=1024, tn=1024, out_dtype=BF, name="a_do", views=views)))
    partial[1] = _mm_tn(o_a, dh1b, tk=1024, tn=1024, tm=512, out_dtype=BF, name="a_dwout")
    riders = {0: [4], 1: [6, 1], 2: []}
    parts = []
    for gi, (window, dil) in enumerate(DILATED_PATTERNS):
        pv = projs[gi]
        res = ride(lambda guest: _band_bwd(pv, pv, pv, block(0, dil), block(1, dil), block(2, dil), do_a[dil],
                                           mixed[dil][0], mixed[dil][1], dil=dil, T=128,
                                           window=window // dil, name=f"a_attn_bwd{gi}", guest=guest), riders[gi])
        parts += list(res)
    dproj = _assemble(parts, [True, True, False] * 3, rope_b, [d for _, d in DILATED_PATTERNS for _ in range(3)], name="a_dproj")
    partial[0] = _mm_tn(n0, dproj, tk=1024, tn=1024, tm=512, out_dtype=BF, name="a_dwin")
    dn0 = ride(lambda guest: _mm_nt(dproj, w["a_w_in"], tm=512, to=1024, tn=1024, out_dtype=F32, name="a_dn", guest=guest), [0])
    dx, _, g["a_norm"] = _rms_bwd(x, w["a_norm"], dn0, dh1, name="a_dnorm")
    return loss, dx, g, partial, landed


ANY = pl.BlockSpec(memory_space=pl.ANY)


def _place():
    x, y, c = lax.axis_index("x"), lax.axis_index("y"), lax.axis_index("c")
    chips = [(1 - x, y), (x, 1 - y), (1 - x, 1 - y)]
    return x, y, c, chips


def _shard_slice(ref, kind, rows, cols, s, half):
    hr = rows // 2
    if kind == "col":
        return ref.at[pl.ds(half * hr, hr), pl.ds(pl.multiple_of(s * cols, 128), cols)]
    if kind == "row":
        return ref.at[pl.ds(pl.multiple_of(s * rows + half * hr, 16), hr), :]
    return ref.at[s, pl.ds(half * hr, hr), :]


def _whole_shape(kind, rows, cols):
    return {"col": (rows, N_CHIPS * cols), "row": (N_CHIPS * rows, cols), "stack": (N_CHIPS, rows, cols)}[kind]


def _own_block(kind, rows, tr, cols):
    per = rows // tr

    def spec(half_rows):
        off = (lambda p: 0) if half_rows is None else (lambda p: p[1] * (half_rows // tr))
        if kind == "col":
            return pl.BlockSpec((tr, cols), lambda i, p: (off(p) + i, p[0]))
        if kind == "row":
            return pl.BlockSpec((tr, cols), lambda i, p: (p[0] * per + off(p) + i, 0))
        return pl.BlockSpec((None, tr, cols), lambda i, p: (p[0], off(p) + i, 0))
    return spec


def _place_shard(shards, layer, kind, place, *, name):
    _, rows, cols = shards.shape
    tr = 256 if rows % 256 == 0 else rows // 2

    def body(p_ref, s_ref, o_ref):
        o_ref[...] = s_ref[...].astype(BF)

    return pl.pallas_call(
        body,
        grid_spec=pltpu.PrefetchScalarGridSpec(
            num_scalar_prefetch=1, grid=(rows // tr,),
            in_specs=[pl.BlockSpec((None, tr, cols), lambda i, p: (layer, i, 0))],
            out_specs=_own_block(kind, rows, tr, cols)(None)),
        out_shape=jax.ShapeDtypeStruct(_whole_shape(kind, rows, cols), BF),
        name=name, compiler_params=_params("arbitrary"),
    )(place, shards)


def _gather_weights(placed, kinds, dims):
    nw = len(placed)

    def body(*refs):
        dst = refs[nw:2 * nw]
        send_sems, recv_sems = refs[2 * nw:]
        x, y, c, chips = _place()
        me = 2 * x + y
        sibling = (x, y, 1 - c)

        def copy(wi, k, s, half, to):
            p = _shard_slice(dst[wi], kinds[wi], dims[wi][0], dims[wi][1], s, half)
            return pltpu.make_async_remote_copy(src_ref=p, dst_ref=p, send_sem=send_sems.at[wi * 6 + k],
                                                recv_sem=recv_sems.at[wi * 6 + k], device_id=to, device_id_type=MESH)

        first, passed = [], []
        for wi in range(nw):
            for j, chip in enumerate(chips):
                cp = copy(wi, j, me, c, (*chip, c))
                cp.start()
                first.append(cp)
        for wi in range(nw):
            for j, chip in enumerate(chips):
                s = 2 * chip[0] + chip[1]
                copy(wi, j, s, c, (x, y, c)).wait_recv()
                cp = copy(wi, 3 + j, s, c, sibling)
                cp.start()
                passed.append(cp)
        for wi in range(nw):
            for j, chip in enumerate(chips):
                s = 2 * chip[0] + chip[1]
                copy(wi, 3 + j, s, 1 - c, (x, y, c)).wait_recv()
        for cp in first + passed:
            cp.wait_send()

    return pl.pallas_call(
        body, in_specs=[ANY] * nw, out_specs=[ANY] * nw,
        out_shape=[jax.ShapeDtypeStruct(p.shape, p.dtype) for p in placed],
        input_output_aliases={wi: wi for wi in range(nw)},
        scratch_shapes=[pltpu.SemaphoreType.DMA((nw * 6,)), pltpu.SemaphoreType.DMA((nw * 6,))],
        name="gather_weights",
    )(*placed)


def _fetch_guest(placed, kinds, dims):
    nw = len(placed)

    def copies(dst, send_sems, recv_sems, incoming):
        x, y, c, chips = _place()
        out = []
        for wi in range(nw):
            for j, chip in enumerate(chips):
                s = 2 * chip[0] + chip[1] if incoming else 2 * x + y
                to = (x, y, c) if incoming else (*chip, c)
                for half in range(2):
                    p = _shard_slice(dst[wi], kinds[wi], dims[wi][0], dims[wi][1], s, half)
                    k = wi * 6 + 2 * j + half
                    out.append(pltpu.make_async_remote_copy(src_ref=p, dst_ref=p, send_sem=send_sems.at[k],
                                                            recv_sem=recv_sems.at[k], device_id=to, device_id_type=MESH))
        return out

    def start(src, dst, sems):
        for cp in copies(dst, sems[0], sems[1], False):
            cp.start()

    def finish(src, dst, sems):
        for cp in copies(dst, sems[0], sems[1], True):
            cp.wait_recv()
        for cp in copies(dst, sems[0], sems[1], False):
            cp.wait_send()

    return dict(args=list(placed), out_shape=[jax.ShapeDtypeStruct(p.shape, p.dtype) for p in placed],
                scratch=[pltpu.SemaphoreType.DMA((nw * 6,)), pltpu.SemaphoreType.DMA((nw * 6,))],
                start=start, finish=finish, in_place=True)


def _scatter_guest(partials, kinds, dims):
    nw = len(partials)

    def copies(src, send_sems, recv_sems, dst):
        x, y, c, chips = _place()
        me = 2 * x + y
        out = []
        for wi in range(nw):
            rows, cols = dims[wi]

            def part(s, half, wi=wi, rows=rows, cols=cols):
                return _shard_slice(src[wi], kinds[wi], rows, cols, s, half)

            for j, chip in enumerate(chips):
                s = 2 * chip[0] + chip[1]
                for half in range(2):
                    slot = 2 * j + (c if half == 0 else 1 - c)
                    out.append(pltpu.make_async_remote_copy(
                        src_ref=part(s, half), dst_ref=dst[wi].at[slot],
                        send_sem=send_sems.at[wi * 7 + 2 * j + half], recv_sem=recv_sems.at[wi * 7 + slot],
                        device_id=(*chip, half), device_id_type=MESH))
            out.append(pltpu.make_async_remote_copy(
                src_ref=part(me, 1 - c), dst_ref=dst[wi].at[6],
                send_sem=send_sems.at[wi * 7 + 6], recv_sem=recv_sems.at[wi * 7 + 6],
                device_id=(x, y, 1 - c), device_id_type=MESH))
        return out

    def start(src, dst, sems):
        for cp in copies(src, sems[0], sems[1], dst):
            cp.start()

    def finish(src, dst, sems):
        x, y, c, _ = _place()
        for wi in range(nw):
            for slot in range(7):
                pltpu.make_async_remote_copy(
                    src_ref=dst[wi].at[slot], dst_ref=dst[wi].at[slot],
                    send_sem=sems[0].at[wi * 7 + slot], recv_sem=sems[1].at[wi * 7 + slot],
                    device_id=(x, y, c), device_id_type=MESH).wait_recv()
        for cp in copies(src, sems[0], sems[1], dst):
            cp.wait_send()

    return dict(args=list(partials), out_shape=[jax.ShapeDtypeStruct((7, d[0] // 2, d[1]), BF) for d in dims],
                scratch=[pltpu.SemaphoreType.DMA((nw * 7,)), pltpu.SemaphoreType.DMA((nw * 7,))],
                start=start, finish=finish)


def _sum_slots(slots, partial, kind, dims, place, *, name, into=None, layer=None, n_layers=1):
    rows, cols = dims
    hr = rows // 2
    tr = hr if 8 * hr * cols * 2 <= 6 * 1024 * 1024 else 128
    assert hr % tr == 0

    def body(p_ref, b_ref, own_ref, *rest):
        o_ref = rest[-1]
        acc = own_ref[...].astype(F32)
        for k in range(7):
            acc = acc + b_ref[k].astype(F32)
        o_ref[...] = acc

    half = lambda p: p[1] * (hr // tr)
    if n_layers == 1:
        out_spec = pl.BlockSpec((tr, cols), lambda i, p: (half(p) + i, 0))
        out_shape = jax.ShapeDtypeStruct((rows, cols), F32)
    else:
        out_spec = pl.BlockSpec((None, tr, cols), lambda i, p: (layer, half(p) + i, 0))
        out_shape = jax.ShapeDtypeStruct((n_layers, rows, cols), F32)
    in_specs = [pl.BlockSpec((7, tr, cols), lambda i, p: (0, i, 0)), _own_block(kind, rows, tr, cols)(hr)]
    args = [place, slots, partial]
    aliases = {}
    if into is not None:
        in_specs.append(ANY)
        args.append(into)
        aliases = {3: 0}
    return pl.pallas_call(
        body,
        grid_spec=pltpu.PrefetchScalarGridSpec(num_scalar_prefetch=1, grid=(hr // tr,), in_specs=in_specs, out_specs=out_spec),
        out_shape=out_shape, input_output_aliases=aliases, name=name, compiler_params=_params("arbitrary"),
    )(*args)


def _pair_exchange(bufs, members):
    nw = len(members)

    def body(*refs):
        dst = refs[len(bufs):2 * len(bufs)]
        send_sems, recv_sems = refs[2 * len(bufs):]
        x, y, c, _ = _place()

        def rows_of(wi, half):
            bi, l = members[wi]
            ref = dst[bi] if l is None else dst[bi].at[l]
            hr = ref.shape[0] // 2
            return ref.at[pl.ds(pl.multiple_of(half * hr, 8), hr), :]

        def copy(wi, half, to):
            p = rows_of(wi, half)
            return pltpu.make_async_remote_copy(src_ref=p, dst_ref=p, send_sem=send_sems.at[wi], recv_sem=recv_sems.at[wi],
                                                device_id=to, device_id_type=MESH)

        sent = []
        for wi in range(nw):
            cp = copy(wi, c, (x, y, 1 - c))
            cp.start()
            sent.append(cp)
        for wi in range(nw):
            copy(wi, 1 - c, (x, y, c)).wait_recv()
        for cp in sent:
            cp.wait_send()

    return pl.pallas_call(
        body, in_specs=[ANY] * len(bufs), out_specs=[ANY] * len(bufs),
        out_shape=[jax.ShapeDtypeStruct(b.shape, b.dtype) for b in bufs],
        input_output_aliases={i: i for i in range(len(bufs))},
        scratch_shapes=[pltpu.SemaphoreType.DMA((nw,)), pltpu.SemaphoreType.DMA((nw,))],
        name="pair_exchange",
    )(*bufs)


SMALL_ROWS = 8


def _allreduce_small(v, *, name):
    assert v.shape == (SMALL_ROWS, D_MODEL)

    def body(v_ref, o_ref, buf, send_sems, recv_sems):
        x, y, c, _ = _place()
        me = 4 * x + 2 * y + c
        buf[me] = v_ref[...]
        sent = []
        for k in range(1, 8):
            bx, by, bc = (k >> 2) & 1, (k >> 1) & 1, k & 1
            peer = (1 - x if bx else x, 1 - y if by else y, 1 - c if bc else c)
            cp = pltpu.make_async_remote_copy(src_ref=v_ref, dst_ref=buf.at[me], send_sem=send_sems.at[k - 1],
                                              recv_sem=recv_sems.at[k - 1], device_id=peer, device_id_type=MESH)
            cp.start()
            sent.append(cp)
        for k in range(1, 8):
            bx, by, bc = (k >> 2) & 1, (k >> 1) & 1, k & 1
            peer = 4 * (1 - x if bx else x) + 2 * (1 - y if by else y) + (1 - c if bc else c)
            pltpu.make_async_remote_copy(src_ref=v_ref, dst_ref=buf.at[peer], send_sem=send_sems.at[k - 1],
                                         recv_sem=recv_sems.at[k - 1], device_id=(x, y, c), device_id_type=MESH).wait_recv()
        for cp in sent:
            cp.wait_send()
        acc = buf[0]
        for d in range(1, 8):
            acc = acc + buf[d]
        o_ref[...] = acc

    vmem = pl.BlockSpec(memory_space=pltpu.VMEM)
    return pl.pallas_call(
        body, in_specs=[vmem], out_specs=vmem, out_shape=jax.ShapeDtypeStruct(v.shape, F32),
        scratch_shapes=[pltpu.VMEM((8,) + v.shape, F32), pltpu.SemaphoreType.DMA((7,)), pltpu.SemaphoreType.DMA((7,))],
        name=name,
    )(v)


def _adamw(w, g, m, v, *, name):
    R, C = w.shape
    tr = R
    if R * C * 4 > 1024 * 1024:
        tr = max(t for t in range(8, R, 8) if R % t == 0 and t * C * 4 <= 1024 * 1024)

    def body(w_ref, g_ref, m_ref, v_ref, d_ref, m2_ref, v2_ref):
        gg = g_ref[...]
        m2 = ADAM_B1 * m_ref[...] + (1.0 - ADAM_B1) * gg
        v2 = ADAM_B2 * v_ref[...] + (1.0 - ADAM_B2) * jnp.square(gg)
        m_hat = m2 / (1.0 - ADAM_B1 ** ADAM_STEP)
        v_hat = v2 / (1.0 - ADAM_B2 ** ADAM_STEP)
        d_ref[...] = -ADAM_LR * (m_hat / (jnp.sqrt(v_hat) + ADAM_EPS) + ADAM_WD * w_ref[...])
        m2_ref[...] = m2
        v2_ref[...] = v2

    blk = pl.BlockSpec((tr, C), lambda i: (i, 0))
    out = jax.ShapeDtypeStruct((R, C), F32)
    return pl.pallas_call(
        body, grid=(R // tr,), in_specs=[blk] * 4, out_specs=[blk] * 3, out_shape=[out] * 3,
        name=name, compiler_params=_params("parallel"),
    )(w, g, m, v)


WEIGHT_ORDER = ("a_norm", "a_w_in", "a_w_out", "b_norm", "b_w_in", "b_f", "b_w_out", "ffn_norm", "ffn_w_gu",
                "ffn_w_down", "final_norm")
MATRICES = (("a_w_in", 0, "col"), ("a_w_out", 0, "row"), ("b_w_in", 0, "stack"), ("b_w_out", 0, "row"),
            ("ffn_w_gu", 0, "col"), ("ffn_w_gu", 1, "col"), ("ffn_w_down", 0, "row"), ("ffn_w_down", 1, "row"))
MATRIX_GROUPS = ([0], [1], [2], [3], [4, 5], [6, 7])
GROUP_NAMES = ("a_w_in", "a_w_out", "b_w_in", "b_w_out", "ffn_w_gu", "ffn_w_down")
QKV_COLS = 3 * N_HEADS * HEAD_DIM


def kernel(x, a_norm, a_w_in, a_w_out, b_norm, b_w_in, b_f, b_w_out, ffn_norm, ffn_w_gu, ffn_w_down, final_norm, loss_target, m_a_norm, m_a_w_in, m_a_w_out, m_b_norm, m_b_w_in, m_b_f, m_b_w_out, m_ffn_norm, m_ffn_w_gu, m_ffn_w_down, m_final_norm, v_a_norm, v_a_w_in, v_a_w_out, v_b_norm, v_b_w_in, v_b_f, v_b_w_out, v_ffn_norm, v_ffn_w_gu, v_ffn_w_down, v_final_norm):
    given = dict(a_norm=a_norm, a_w_in=a_w_in, a_w_out=a_w_out, b_norm=b_norm, b_w_in=b_w_in, b_f=b_f, b_w_out=b_w_out,
                 ffn_norm=ffn_norm, ffn_w_gu=ffn_w_gu, ffn_w_down=ffn_w_down, final_norm=final_norm)
    mom_m = dict(a_norm=m_a_norm, a_w_in=m_a_w_in, a_w_out=m_a_w_out, b_norm=m_b_norm, b_w_in=m_b_w_in, b_f=m_b_f,
                 b_w_out=m_b_w_out, ffn_norm=m_ffn_norm, ffn_w_gu=m_ffn_w_gu, ffn_w_down=m_ffn_w_down, final_norm=m_final_norm)
    mom_v = dict(a_norm=v_a_norm, a_w_in=v_a_w_in, a_w_out=v_a_w_out, b_norm=v_b_norm, b_w_in=v_b_w_in, b_f=v_b_f,
                 b_w_out=v_b_w_out, ffn_norm=v_ffn_norm, ffn_w_gu=v_ffn_w_gu, ffn_w_down=v_ffn_w_down, final_norm=v_final_norm)
    chip = 2 * lax.axis_index("x") + lax.axis_index("y")
    core = lax.axis_index("c")
    bn_cols = b_norm.shape[1]

    placed = lax.dynamic_update_slice(jnp.zeros((SMALL_ROWS, D_MODEL), F32), b_norm, (0, chip * bn_cols))
    placed = placed * (core == 0).astype(F32)
    b_norm_full = _allreduce_small(placed, name="gather_b_norm")[0:1]

    place = jnp.stack([chip, core]).astype(jnp.int32)
    kinds = [k for _, _, k in MATRICES]
    dims = [given[n].shape[1:] for n, _, _ in MATRICES]
    placed = [_place_shard(given[n], l, k, place, name=f"place_{n}{l}") for n, l, k in MATRICES]
    first = _gather_weights(placed[:2], kinds[:2], dims[:2])
    mats = dict(enumerate(list(first) + placed[2:]))
    gate_cols = b_f.shape[1]
    w = dict(a_norm=a_norm, a_w_in=mats[0], a_w_out=mats[1], b_norm=b_norm_full,
             b_f=jnp.pad(b_f, ((0, 0), (0, GATE_LANES - gate_cols))), ffn_norm=ffn_norm,
             final_norm=final_norm.reshape(1, D_MODEL))

    def fetch(indices, bufs):
        return _fetch_guest(bufs, [kinds[i] for i in indices], [dims[i] for i in indices])

    def exchange(indices, parts):
        return _scatter_guest(parts, [kinds[i] for i in indices], [dims[i] for i in indices])

    loss, dx, g, partials, slots = _local_step(x[0], loss_target[0], w, mats, fetch, exchange)
    bufs, members = [], []
    for group in MATRIX_GROUPS:
        buf = None
        for l, wi in enumerate(group):
            n = MATRICES[wi][0]
            buf = _sum_slots(slots[wi], partials[wi], kinds[wi], dims[wi], place, name=f"sum_{n}{l}", into=buf,
                             layer=l, n_layers=len(group))
            members.append((len(bufs), l if len(group) > 1 else None))
        bufs.append(buf)
    reduced = dict(zip(GROUP_NAMES, _pair_exchange(bufs, members)))

    small = jnp.concatenate([g["a_norm"], g["b_norm"], g["ffn_norm"], g["final_norm"],
                             jnp.pad(g["b_f"], ((0, 0), (0, D_MODEL - GATE_LANES))),
                             jnp.zeros((SMALL_ROWS - 6, D_MODEL), F32)], axis=0)
    small = _allreduce_small(small, name="allreduce_small")
    grads = dict(reduced)
    grads["a_norm"] = small[0:1]
    grads["b_norm"] = lax.dynamic_slice(small, (1, chip * bn_cols), (1, bn_cols))
    grads["ffn_norm"] = small[2:4]
    grads["final_norm"] = small[4]
    grads["b_f"] = small[5:6, :gate_cols]

    out_g, out_d, out_m, out_v = [], [], [], []
    for n in WEIGHT_ORDER:
        shape = given[n].shape
        two_d = (1, shape[0]) if len(shape) == 1 else (-1, shape[-1])
        d, m2, v2 = _adamw(given[n].reshape(two_d), grads[n].reshape(two_d), mom_m[n].reshape(two_d),
                           mom_v[n].reshape(two_d), name=f"adamw_{n}")
        out_g.append(grads[n].reshape(shape))
        out_d.append(d.reshape(shape))
        out_m.append(m2.reshape(shape))
        out_v.append(v2.reshape(shape))

    total = lax.psum(loss[0, 0], MESH_AXES)
    return (total, dx[None], *out_g, *out_d, *out_m, *out_v)
```

```python
import functools

import jax
import jax.numpy as jnp
from jax import lax
from jax.experimental import pallas as pl
from jax.experimental.pallas import tpu as pltpu

F32 = jnp.float32
BF = jnp.bfloat16

D_MODEL = 1024
N_HEADS = 16
HEAD_DIM = 64
D_FF = 2816
DILATED_PATTERNS = ((128, 1), (512, 4), (2048, 16))
ROT_DIM = 16
ROPE_THETA = 500000.0
RMS_EPS = 1e-6
NEG_INF = -1e30
ATTN_SCALE = HEAD_DIM ** -0.5
GATE_LANES = 128
N_CHIPS = 4
MESH_AXES = ("x", "y", "c")
MESH = pl.DeviceIdType.MESH

ADAM_LR = 0.001
ADAM_B1 = 0.9
ADAM_B2 = 0.999
ADAM_EPS = 1e-08
ADAM_WD = 0.01
ADAM_STEP = 10

VMEM_LIMIT_BYTES = 56 * 1024 * 1024


def _params(*sem):
    return pltpu.CompilerParams(dimension_semantics=sem, vmem_limit_bytes=VMEM_LIMIT_BYTES)


def _hosted_call(body, *, grid, in_specs, out_specs, out_shape, scratch_shapes, args, name, guest=None, schedule=()):
    params = _params(*(["arbitrary"] * len(grid)))
    ns = len(schedule)

    def call(kernel, in_specs, out_specs, out_shape, scratch_shapes, aliases, args):
        spec = pltpu.PrefetchScalarGridSpec(num_scalar_prefetch=ns, grid=grid, in_specs=in_specs, out_specs=out_specs,
                                            scratch_shapes=scratch_shapes)
        return pl.pallas_call(kernel, grid_spec=spec, out_shape=out_shape, input_output_aliases=aliases, name=name,
                              compiler_params=params)(*schedule, *args)

    if guest is None:
        return call(body, in_specs, out_specs, out_shape, scratch_shapes, {}, args)
    n_in, n_out, n_scr = ns + len(in_specs), len(out_specs), len(scratch_shapes)
    g_in, g_out = len(guest["args"]), len(guest["out_shape"])
    any_spec = pl.BlockSpec(memory_space=pl.ANY)

    def wrapped(*refs):
        i1 = n_in + g_in
        o1 = i1 + n_out
        o2 = o1 + g_out
        s1 = o2 + n_scr
        guest_refs = (refs[n_in:i1], refs[o1:o2], refs[s1:])
        ids = [pl.program_id(d) for d in range(len(grid))]
        first = functools.reduce(jnp.logical_and, [i == 0 for i in ids])
        last = functools.reduce(jnp.logical_and, [i == g - 1 for i, g in zip(ids, grid)])

        @pl.when(first)
        def _():
            guest["start"](*guest_refs)

        body(*refs[:n_in], *refs[i1:o1], *refs[o2:s1])

        @pl.when(last)
        def _():
            guest["finish"](*guest_refs)

    aliases = {n_in + k: n_out + k for k in range(g_in)} if guest.get("in_place") else {}
    return call(wrapped, list(in_specs) + [any_spec] * g_in, list(out_specs) + [any_spec] * g_out,
                list(out_shape) + list(guest["out_shape"]), list(scratch_shapes) + list(guest["scratch"]), aliases,
                list(args) + list(guest["args"]))


def _rope_rotate(t, cos, sin_a, sin_b):
    outs = []
    for cidx in range(t.shape[1] // 128):
        tc = t[:, cidx * 128:(cidx + 1) * 128]
        outs.append(tc * cos + pltpu.roll(tc, 120, 1) * sin_a + pltpu.roll(tc, 8, 1) * sin_b)
    return jnp.concatenate(outs, axis=1)


def _mm_nn(a, b, *, tm, tn, out_dtype, name, resid=None, rope=None, guest=None, groups=None):
    M, K = a.shape
    N = b.shape[1]
    assert M % tm == 0 and N % tn == 0 and b.shape[0] == K
    n_in = 2 + (resid is not None) + (3 if rope is not None else 0)
    if groups is not None:
        assert rope is not None and N == 3 * tn * len(groups)

    def body(*refs):
        a_ref, b_ref = refs[0], refs[1]
        o_ref = refs[n_in]
        acc = jnp.dot(a_ref[...], b_ref[...], preferred_element_type=F32)
        if resid is not None:
            acc = acc + refs[2][...]
        if groups is not None:
            cos_ref, sa_ref, sb_ref = refs[n_in - 3:n_in]
            j = pl.program_id(1)
            for g, d in enumerate(groups):
                for is_v in (False, True):
                    @pl.when(jnp.logical_and(j // 3 == g, (j % 3 == 2) == is_v))
                    def _(g=g, d=d, is_v=is_v):
                        val = acc if is_v else _rope_rotate(acc, cos_ref[...], sa_ref[...], sb_ref[...])
                        if d == 1:
                            refs[n_in + g][...] = val.astype(out_dtype)
                        else:
                            _to_view(val, refs[-1], refs[n_in + g], d, tn)
        elif rope is not None:
            cos_ref, sa_ref, sb_ref = refs[n_in - 3:n_in]
            j = pl.program_id(1)

            @pl.when(j % 3 != 2)
            def _():
                o_ref[...] = _rope_rotate(acc, cos_ref[...], sa_ref[...], sb_ref[...]).astype(out_dtype)

            @pl.when(j % 3 == 2)
            def _():
                o_ref[...] = acc.astype(out_dtype)
        else:
            o_ref[...] = acc.astype(out_dtype)

    in_specs = [pl.BlockSpec((tm, K), lambda i, j: (i, 0)), pl.BlockSpec((K, tn), lambda i, j: (0, j))]
    args = [a, b]
    if resid is not None:
        in_specs.append(pl.BlockSpec((tm, tn), lambda i, j: (i, j)))
        args.append(resid)
    if rope is not None:
        assert tn == 1024
        for t in rope:
            in_specs.append(pl.BlockSpec((tm, 128), lambda i, j: (i, 0)))
            args.append(t)
    if groups is None:
        out_specs = [pl.BlockSpec((tm, tn), lambda i, j: (i, j))]
        out_shape = [jax.ShapeDtypeStruct((M, N), out_dtype)]
        scratch = []
    else:
        out_specs = [pl.BlockSpec((tm // d, d * tn), lambda i, j, g=g: (i, jnp.clip(j - 3 * g, 0, 2)))
                     for g, d in enumerate(groups)]
        out_shape = [jax.ShapeDtypeStruct((M // d, d * 3 * tn), out_dtype) for d in groups]
        scratch = [pltpu.VMEM((tn // 128, tm, 128), F32)]
    outs = _hosted_call(body, grid=(M // tm, N // tn), in_specs=in_specs, out_specs=out_specs, out_shape=out_shape,
                        scratch_shapes=scratch, args=args, name=name, guest=guest)
    nout = len(out_shape)
    res = outs[0] if groups is None else list(outs[:nout])
    return res if guest is None else (res, outs[nout:])


def _mm_nt(a, b, *, tm, to, tn, out_dtype, name, add=None, guest=None, views=(1,)):
    M, N = a.shape
    O = b.shape[0]
    assert M % tm == 0 and O % to == 0 and N % tn == 0 and b.shape[1] == N
    nk = N // tn

    def body(*refs):
        a_ref, b_ref = refs[0], refs[1]
        n_in = 2 + (add is not None)
        o_refs = refs[n_in:n_in + len(views)]
        acc_ref = refs[n_in + len(views)]
        k = pl.program_id(2)

        @pl.when(k == 0)
        def _():
            if add is not None:
                acc_ref[...] = refs[2][...]
            else:
                acc_ref[...] = jnp.zeros_like(acc_ref)

        acc_ref[...] += lax.dot_general(a_ref[...], b_ref[...], (((1,), (1,)), ((), ())),
                                        preferred_element_type=F32)

        @pl.when(k == nk - 1)
        def _():
            for o_ref, d in zip(o_refs, views):
                if d == 1:
                    o_ref[...] = acc_ref[...].astype(out_dtype)
                else:
                    _to_view(acc_ref[...], refs[-1], o_ref, d, to)

    in_specs = [pl.BlockSpec((tm, tn), lambda i, j, k: (i, k)), pl.BlockSpec((to, tn), lambda i, j, k: (j, k))]
    args = [a, b]
    if add is not None:
        in_specs.append(pl.BlockSpec((tm, to), lambda i, j, k: (i, j)))
        args.append(add)
    assert views == (1,) or (to == O and to % 128 == 0)
    scratch = [pltpu.VMEM((tm, to), F32)] + ([pltpu.VMEM((to // 128, tm, 128), F32)] if views != (1,) else [])
    outs = _hosted_call(
        body, grid=(M // tm, O // to, nk), in_specs=in_specs,
        out_specs=[pl.BlockSpec((tm, to), lambda i, j, k: (i, j)) if d == 1 else _view_spec(tm, d, to) for d in views],
        out_shape=[jax.ShapeDtypeStruct((M // d, d * O), out_dtype) for d in views],
        scratch_shapes=scratch, args=args, name=name, guest=guest)
    nv = len(views)
    res = outs[0] if nv == 1 else list(outs[:nv])
    return res if guest is None else (res, outs[nv:])


def _mm_tn(a, b, *, tk, tn, tm, out_dtype, name):
    M, K = a.shape
    N = b.shape[1]
    assert M % tm == 0 and K % tk == 0 and N % tn == 0 and b.shape[0] == M
    nm = M // tm

    def body(a_ref, b_ref, o_ref, acc_ref):
        m = pl.program_id(2)

        @pl.when(m == 0)
        def _():
            acc_ref[...] = jnp.zeros_like(acc_ref)

        acc_ref[...] += lax.dot_general(a_ref[...], b_ref[...], (((0,), (0,)), ((), ())),
                                        preferred_element_type=F32)

        @pl.when(m == nm - 1)
        def _():
            o_ref[...] = acc_ref[...].astype(out_dtype)

    return pl.pallas_call(
        body, grid=(K // tk, N // tn, nm),
        in_specs=[pl.BlockSpec((tm, tk), lambda i, j, m: (m, i)), pl.BlockSpec((tm, tn), lambda i, j, m: (m, j))],
        out_specs=pl.BlockSpec((tk, tn), lambda i, j, m: (i, j)),
        out_shape=jax.ShapeDtypeStruct((K, N), out_dtype),
        scratch_shapes=[pltpu.VMEM((tk, tn), F32)], name=name,
        compiler_params=_params("parallel", "parallel", "arbitrary"),
    )(a, b)


ROW_TILE = 512


def _rms_fwd(x, g, *, name):
    S, Dm = x.shape

    def body(x_ref, g_ref, o_ref):
        xf = x_ref[...]
        r = lax.rsqrt(jnp.mean(xf * xf, axis=-1, keepdims=True) + RMS_EPS)
        o_ref[...] = (xf * r * g_ref[...]).astype(BF)

    return pl.pallas_call(
        body, grid=(S // ROW_TILE,),
        in_specs=[pl.BlockSpec((ROW_TILE, Dm), lambda i: (i, 0)), pl.BlockSpec((1, Dm), lambda i: (0, 0))],
        out_specs=pl.BlockSpec((ROW_TILE, Dm), lambda i: (i, 0)),
        out_shape=jax.ShapeDtypeStruct((S, Dm), BF), name=name, compiler_params=_params("parallel"),
    )(x, g)


def _rms_bwd(x, g, dn, dres, *, name):
    S, Dm = x.shape

    def body(x_ref, g_ref, dn_ref, dres_ref, dx_ref, dxb_ref, dg_ref):
        i = pl.program_id(0)
        xf = x_ref[...]
        r = lax.rsqrt(jnp.mean(xf * xf, axis=-1, keepdims=True) + RMS_EPS)
        xh = xf * r
        dnf = dn_ref[...]
        dyg = dnf * g_ref[...]
        dx = dres_ref[...] + r * (dyg - xh * jnp.mean(dyg * xh, axis=-1, keepdims=True))
        dx_ref[...] = dx
        dxb_ref[...] = dx.astype(BF)

        @pl.when(i == 0)
        def _():
            dg_ref[...] = jnp.zeros_like(dg_ref)

        dg_ref[...] += jnp.sum(dnf * xh, axis=0, keepdims=True)

    row = pl.BlockSpec((ROW_TILE, Dm), lambda i: (i, 0))
    vec = pl.BlockSpec((1, Dm), lambda i: (0, 0))
    return pl.pallas_call(
        body, grid=(S // ROW_TILE,), in_specs=[row, vec, row, row], out_specs=[row, row, vec],
        out_shape=[jax.ShapeDtypeStruct((S, Dm), F32), jax.ShapeDtypeStruct((S, Dm), BF),
                   jax.ShapeDtypeStruct((1, Dm), F32)],
        name=name, compiler_params=_params("arbitrary"),
    )(x, g, dn, dres)


def _loss_head(h, g, tgt, *, name):
    S, Dm = h.shape

    def body(h_ref, g_ref, t_ref, loss_ref, dh_ref, dhb_ref, dg_ref):
        i = pl.program_id(0)
        xf = h_ref[...]
        r = lax.rsqrt(jnp.mean(xf * xf, axis=-1, keepdims=True) + RMS_EPS)
        xh = xf * r
        gv = g_ref[...]
        err = xh * gv - t_ref[...]
        dy = err * (1.0 / Dm)
        dyg = dy * gv
        dh = r * (dyg - xh * jnp.mean(dyg * xh, axis=-1, keepdims=True))
        dh_ref[...] = dh
        dhb_ref[...] = dh.astype(BF)

        @pl.when(i == 0)
        def _():
            dg_ref[...] = jnp.zeros_like(dg_ref)
            loss_ref[...] = jnp.zeros_like(loss_ref)

        dg_ref[...] += jnp.sum(dy * xh, axis=0, keepdims=True)
        part = 0.5 * jnp.sum(jnp.mean(err * err, axis=-1, keepdims=True), axis=0, keepdims=True)
        loss_ref[...] += jnp.broadcast_to(part, loss_ref.shape)

    row = pl.BlockSpec((ROW_TILE, Dm), lambda i: (i, 0))
    vec = pl.BlockSpec((1, Dm), lambda i: (0, 0))
    return pl.pallas_call(
        body, grid=(S // ROW_TILE,), in_specs=[row, vec, row],
        out_specs=[pl.BlockSpec((1, 128), lambda i: (0, 0)), row, row, vec],
        out_shape=[jax.ShapeDtypeStruct((1, 128), F32), jax.ShapeDtypeStruct((S, Dm), F32),
                   jax.ShapeDtypeStruct((S, Dm), BF), jax.ShapeDtypeStruct((1, Dm), F32)],
        name=name, compiler_params=_params("arbitrary"),
    )(h, g, tgt)


SWIGLU_ROWS = 256


def _swiglu_fwd(gu, *, name):
    S = gu.shape[0]

    def body(g_ref, u_ref, o_ref):
        g = g_ref[...].astype(F32)
        sig = 1.0 / (1.0 + jnp.exp(-g))
        o_ref[...] = (g * sig * u_ref[...].astype(F32)).astype(BF)

    return pl.pallas_call(
        body, grid=(S // SWIGLU_ROWS,),
        in_specs=[pl.BlockSpec((SWIGLU_ROWS, D_FF), lambda i: (i, 0)), pl.BlockSpec((SWIGLU_ROWS, D_FF), lambda i: (i, 1))],
        out_specs=pl.BlockSpec((SWIGLU_ROWS, D_FF), lambda i: (i, 0)),
        out_shape=jax.ShapeDtypeStruct((S, D_FF), BF), name=name, compiler_params=_params("parallel"),
    )(gu, gu)


def _swiglu_bwd(gu, dact, *, name):
    S = gu.shape[0]

    def body(g_ref, u_ref, d_ref, o_ref):
        g = g_ref[...].astype(F32)
        u = u_ref[...].astype(F32)
        d = d_ref[...].astype(F32)
        sig = 1.0 / (1.0 + jnp.exp(-g))
        o_ref[:, :D_FF] = (d * u * sig * (1.0 + g * (1.0 - sig))).astype(BF)
        o_ref[:, D_FF:] = (d * g * sig).astype(BF)

    return pl.pallas_call(
        body, grid=(S // SWIGLU_ROWS,),
        in_specs=[pl.BlockSpec((SWIGLU_ROWS, D_FF), lambda i: (i, 0)), pl.BlockSpec((SWIGLU_ROWS, D_FF), lambda i: (i, 1)),
                  pl.BlockSpec((SWIGLU_ROWS, D_FF), lambda i: (i, 0))],
        out_specs=pl.BlockSpec((SWIGLU_ROWS, 2 * D_FF), lambda i: (i, 0)),
        out_shape=jax.ShapeDtypeStruct((S, 2 * D_FF), BF), name=name, compiler_params=_params("parallel"),
    )(gu, gu, dact)


def _band_masks(T, n):
    row = lax.broadcasted_iota(jnp.int32, (T, T), 0)
    col = lax.broadcasted_iota(jnp.int32, (T, T), 1)
    return jnp.logical_and(col >= row, n > 0), col <= row


def _band_fwd(qa, ka, va, qcb, kcb, vcb, *, dil, T, window, name, guest=None):
    L = qa.shape[0]
    nq = L // T
    assert window == T
    nt = (((1,), (1,)), ((), ()))

    def body(q_ref, kp_ref, kc_ref, vp_ref, vc_ref, o_ref, lse_ref):
        valid_prev, valid_cur = _band_masks(T, pl.program_id(1))
        lane = lax.broadcasted_iota(jnp.int32, (T, 128), 1)
        low = lane < HEAD_DIM
        ones = jnp.ones((T, 128), BF)
        lse = jnp.zeros((T, 128), F32)
        def scores(h):
            ps = slice((h // 2) * 128, (h // 2 + 1) * 128)
            qp = q_ref[:, ps] * jnp.asarray(ATTN_SCALE, BF)
            qm = jnp.where(low if h % 2 == 0 else jnp.logical_not(low), qp, jnp.zeros_like(qp))
            s0 = jnp.where(valid_prev, lax.dot_general(qm, kp_ref[:, ps], nt, preferred_element_type=F32), NEG_INF)
            s1 = jnp.where(valid_cur, lax.dot_general(qm, kc_ref[:, ps], nt, preferred_element_type=F32), NEG_INF)
            return s0, s1

        def softmax(s0, s1):
            m = jnp.maximum(jnp.max(s0, axis=1, keepdims=True), jnp.max(s1, axis=1, keepdims=True))
            return m, jnp.exp(s0 - m).astype(BF), jnp.exp(s1 - m).astype(BF)

        def weighted(h, p0, p1):
            ps = slice((h // 2) * 128, (h // 2 + 1) * 128)
            l = jnp.dot(p0, ones, preferred_element_type=F32) + jnp.dot(p1, ones, preferred_element_type=F32)
            acc = jnp.dot(p0, vp_ref[:, ps], preferred_element_type=F32) + jnp.dot(p1, vc_ref[:, ps], preferred_element_type=F32)
            return l, acc

        sc, pr, even = {}, {}, None
        for t in range(N_HEADS + 2):
            if t < N_HEADS:
                sc[t] = scores(t)
            done = None
            if t >= 2:
                m, p0, p1 = pr.pop(t - 2)
                done = (m,) + weighted(t - 2, p0, p1)
            if 1 <= t <= N_HEADS:
                pr[t - 1] = softmax(*sc.pop(t - 1))
            if done is not None:
                h = t - 2
                m, l, acc = done
                lse = jnp.where(lane == h, m + jnp.log(l), lse)
                if h % 2 == 0:
                    even = acc / l
                else:
                    o_ref[:, (h // 2) * 128:(h // 2 + 1) * 128] = jnp.where(low, even, acc / l)
        lse_ref[...] = lse

    def prev(n):
        return jnp.maximum(n - 1, 0)

    blk = lambda f, cb: pl.BlockSpec((T, 1024), lambda r, n: (f(n), cb(r)))
    same = lambda n: n
    outs = _hosted_call(
        body, grid=(dil, nq),
        in_specs=[blk(same, qcb), blk(prev, kcb), blk(same, kcb), blk(prev, vcb), blk(same, vcb)],
        out_specs=[pl.BlockSpec((T, 1024), lambda r, n: (n, r)), pl.BlockSpec((T, 128), lambda r, n: (n, r))],
        out_shape=[jax.ShapeDtypeStruct((L, dil * 1024), F32), jax.ShapeDtypeStruct((L, dil * 128), F32)],
        scratch_shapes=[], args=(qa, ka, ka, va, va), name=name, guest=guest)
    return outs if guest is None else (outs[:2], outs[2:])


def _band_bwd(qa, ka, va, qcb, kcb, vcb, doa, oa, lsea, *, dil, T, window, name, guest=None):
    L = qa.shape[0]
    nq = L // T
    assert window == T
    nt = (((1,), (1,)), ((), ()))
    tn = (((0,), (0,)), ((), ()))

    def body(q_ref, kp_ref, kc_ref, vp_ref, vc_ref, do_ref, o_ref, lse_ref, dq_ref, dk_ref, dv_ref, ck_sc, cv_sc):
        n = pl.program_id(1)

        @pl.when(n == 0)
        def _():
            ck_sc[...] = jnp.zeros_like(ck_sc)
            cv_sc[...] = jnp.zeros_like(cv_sc)

        @pl.when(n < nq)
        def _():
            valid_prev, valid_cur = _band_masks(T, n)
            low = lax.broadcasted_iota(jnp.int32, (T, 128), 1) < HEAD_DIM
            dot = functools.partial(lax.dot_general, preferred_element_type=F32)

            def pair(h):
                return slice((h // 2) * 128, (h // 2 + 1) * 128)

            def products(h):
                ps = pair(h)
                mask = low if h % 2 == 0 else jnp.logical_not(low)
                qp = q_ref[:, ps] * jnp.asarray(ATTN_SCALE, BF)
                dop = do_ref[:, ps]
                qm = jnp.where(mask, qp, jnp.zeros_like(qp))
                dom = jnp.where(mask, dop, jnp.zeros_like(dop))
                s0 = jnp.where(valid_prev, dot(qm, kp_ref[:, ps], nt), NEG_INF)
                s1 = jnp.where(valid_cur, dot(qm, kc_ref[:, ps], nt), NEG_INF)
                return qm, dom, s0, s1, dot(dom, vp_ref[:, ps], nt), dot(dom, vc_ref[:, ps], nt)

            def pointwise(h, qm, dom, s0, s1, dp0, dp1):
                ps = pair(h)
                mask = low if h % 2 == 0 else jnp.logical_not(low)
                prod = do_ref[:, ps].astype(F32) * o_ref[:, ps].astype(F32)
                delta = jnp.sum(jnp.where(mask, prod, 0.0), axis=1, keepdims=True)
                lse = lse_ref[:, h:h + 1]
                p0 = jnp.exp(s0 - lse)
                p1 = jnp.exp(s1 - lse)
                ds0 = (p0 * (dp0 - delta)).astype(BF)
                ds1 = (p1 * (dp1 - delta)).astype(BF)
                return qm, dom, p0.astype(BF), p1.astype(BF), ds0, ds1

            def gradients(h, qm, dom, p0, p1, ds0, ds1):
                ps = pair(h)
                dq = dot(ds0, kp_ref[:, ps], (((1,), (0,)), ((), ()))) + dot(ds1, kc_ref[:, ps], (((1,), (0,)), ((), ())))
                return dq, dot(ds0, qm, tn), dot(p0, dom, tn), dot(ds1, qm, tn), dot(p1, dom, tn)

            st1, st2, even = {}, {}, None
            for t in range(N_HEADS + 2):
                if t < N_HEADS:
                    st1[t] = products(t)
                done = gradients(t - 2, *st2.pop(t - 2)) if t >= 2 else None
                if 1 <= t <= N_HEADS:
                    st2[t - 1] = pointwise(t - 1, *st1.pop(t - 1))
                if done is not None:
                    h = t - 2
                    if h % 2 == 0:
                        even = done
                    else:
                        ps = pair(h)
                        dq_ref[:, ps] = (jnp.where(low, even[0], done[0]) * ATTN_SCALE).astype(BF)
                        dk_ref[:, ps] = (ck_sc[:, ps] + even[1] + done[1]).astype(BF)
                        dv_ref[:, ps] = (cv_sc[:, ps] + even[2] + done[2]).astype(BF)
                        ck_sc[:, ps] = even[3] + done[3]
                        cv_sc[:, ps] = even[4] + done[4]

        @pl.when(n == nq)
        def _():
            dk_ref[...] = ck_sc[...].astype(BF)
            dv_ref[...] = cv_sc[...].astype(BF)

    def cur(n):
        return jnp.minimum(n, nq - 1)

    def prev(n):
        return jnp.maximum(cur(n) - 1, 0)

    blk = lambda f, cb: pl.BlockSpec((T, 1024), lambda r, n: (f(n), cb(r)))
    own = lambda r: r
    outs = _hosted_call(
        body, grid=(dil, nq + 1),
        in_specs=[blk(cur, qcb), blk(prev, kcb), blk(cur, kcb), blk(prev, vcb), blk(cur, vcb), blk(cur, own), blk(cur, own),
                  pl.BlockSpec((T, 128), lambda r, n: (cur(n), r))],
        out_specs=[blk(cur, own), blk(lambda n: jnp.maximum(n - 1, 0), own), blk(lambda n: jnp.maximum(n - 1, 0), own)],
        out_shape=[jax.ShapeDtypeStruct((L, dil * 1024), BF)] * 3,
        scratch_shapes=[pltpu.VMEM((T, 1024), F32), pltpu.VMEM((T, 1024), F32)],
        args=(qa, ka, ka, va, va, doa, oa, lsea), name=name, guest=guest)
    return outs if guest is None else (outs[:3], outs[3:])


FOX_T = 256
FOX_TK = 512
FOX_TQ_BWD = 512
FOX_ROWS = 128


def _fox_fwd(qkv, cT, *, name, guest=None):
    S = qkv.shape[0]
    T, TK, R = FOX_T, FOX_TK, FOX_ROWS
    nq = S // T
    nt = (((1,), (1,)), ((), ()))
    chains = [(h, rh) for h in range(N_HEADS) for rh in range(T // R)]
    pairs = [(n, j) for n in range(nq) for j in range((n * T + T - 1) // TK + 1)]
    schedule = [jnp.asarray([p[i] for p in pairs], jnp.int32) for i in range(2)]

    def body(n_tab, j_tab, q_ref, k_ref, v_ref, ct_ref, o_ref, lse_ref, m_sc, l_sc, acc_sc):
        n = n_tab[pl.program_id(0)]
        j = j_tab[pl.program_id(0)]
        last_j = (n * T + T - 1) // TK
        lane = lax.broadcasted_iota(jnp.int32, (R, 128), 1)
        low = lane < HEAD_DIM
        ones = jnp.ones((TK, 128), BF)

        @pl.when(j == 0)
        def _():
            m_sc[...] = jnp.full(m_sc.shape, NEG_INF, F32)
            l_sc[...] = jnp.zeros_like(l_sc)
            acc_sc[...] = jnp.zeros_like(acc_sc)

        def step(diagonal):
            def pair(h):
                return slice((h // 2) * 128, (h // 2 + 1) * 128)

            def rows(rh):
                return slice(rh * R, (rh + 1) * R)

            def scores(h, rh):
                qp = q_ref[rows(rh), pair(h)] * jnp.asarray(ATTN_SCALE, BF)
                qm = jnp.where(low if h % 2 == 0 else jnp.logical_not(low), qp, jnp.zeros_like(qp))
                s = lax.dot_general(qm, k_ref[:, pair(h)], nt, preferred_element_type=F32) - ct_ref[h:h + 1, :]
                if diagonal:
                    ahead = lax.broadcasted_iota(jnp.int32, (R, TK), 1) - lax.broadcasted_iota(jnp.int32, (R, TK), 0)
                    s = jnp.where(ahead <= n * T + rh * R - j * TK, s, NEG_INF)
                return s

            def softmax(h, rh, s):
                m_prev = m_sc[h, rows(rh), :]
                m_new = jnp.maximum(m_prev, jnp.max(s, axis=1, keepdims=True))
                p = jnp.exp(s - jnp.concatenate([m_new] * (TK // 128), axis=1)).astype(BF)
                return m_new, jnp.exp(m_prev - m_new), p

            def weighted(h, p):
                vx = jnp.concatenate([v_ref[:, pair(h)], ones], axis=1)
                return jnp.dot(p, vx, preferred_element_type=F32)

            sc, pr, even = {}, {}, {}
            nch = len(chains)
            for t in range(nch + 2):
                if t < nch:
                    sc[t] = scores(*chains[t])
                done = None
                if t >= 2:
                    m_new, alpha, p = pr.pop(t - 2)
                    done = (m_new, alpha, weighted(chains[t - 2][0], p))
                if 1 <= t <= nch:
                    pr[t - 1] = softmax(*chains[t - 1], sc.pop(t - 1))
                if done is not None:
                    h, rh = chains[t - 2]
                    m_new, alpha, pv = done
                    m_sc[h, rows(rh), :] = m_new
                    l_sc[h, rows(rh), :] = alpha * l_sc[h, rows(rh), :] + pv[:, 128:]
                    if h % 2 == 0:
                        even[rh] = (alpha, pv[:, :128])
                    else:
                        a0, pv0 = even.pop(rh)
                        acc = acc_sc[h // 2, rows(rh), :]
                        acc_sc[h // 2, rows(rh), :] = jnp.where(low, a0 * acc + pv0, alpha * acc + pv[:, :128])

        @pl.when(j < last_j)
        def _():
            step(False)

        @pl.when(j == last_j)
        def _():
            step(True)
            lane_t = lax.broadcasted_iota(jnp.int32, (T, 128), 1)
            low_t = lane_t < HEAD_DIM
            lse = jnp.zeros((T, 128), F32)
            for h in range(N_HEADS):
                lse = jnp.where(lane_t == h, m_sc[h] + jnp.log(l_sc[h]), lse)
            lse_ref[...] = lse
            for hp in range(N_HEADS // 2):
                inv = jnp.where(low_t, 1.0 / l_sc[2 * hp], 1.0 / l_sc[2 * hp + 1])
                o_ref[:, hp * 128:(hp + 1) * 128] = (acc_sc[hp] * inv).astype(BF)

    outs = _hosted_call(
        body, grid=(len(pairs),),
        in_specs=[pl.BlockSpec((T, 1024), lambda t, n, j: (n[t], 0)), pl.BlockSpec((TK, 1024), lambda t, n, j: (j[t], 1)),
                  pl.BlockSpec((TK, 1024), lambda t, n, j: (j[t], 2)), pl.BlockSpec((GATE_LANES, TK), lambda t, n, j: (0, j[t]))],
        out_specs=[pl.BlockSpec((T, 1024), lambda t, n, j: (n[t], 0)), pl.BlockSpec((T, 128), lambda t, n, j: (n[t], 0))],
        out_shape=[jax.ShapeDtypeStruct((S, 1024), BF), jax.ShapeDtypeStruct((S, 128), F32)],
        scratch_shapes=[pltpu.VMEM((N_HEADS, T, 128), F32), pltpu.VMEM((N_HEADS, T, 128), F32),
                        pltpu.VMEM((N_HEADS // 2, T, 128), F32)],
        args=(qkv, qkv, qkv, cT), name=name, guest=guest, schedule=schedule)
    return outs if guest is None else (outs[:2], outs[2:])


def _fox_bwd(qkv, cT, do, o, lse, *, name, guest=None):
    S = qkv.shape[0]
    T, TQ, R = FOX_T, FOX_TQ_BWD, FOX_ROWS
    nk, nq = S // T, S // TQ
    nt = (((1,), (1,)), ((), ()))
    tn = (((0,), (0,)), ((), ()))
    nn = (((1,), (0,)), ((), ()))
    chains = [(h, rh) for h in range(N_HEADS) for rh in range(TQ // R)]
    dot = functools.partial(lax.dot_general, preferred_element_type=F32)
    pairs = [(kb, qb) for kb in range(nk) for qb in range(kb * T // TQ, nq)]
    schedule = [jnp.asarray([p[i] for p in pairs], jnp.int32) for i in range(2)]

    def body(kb_tab, qb_tab, q_ref, k_ref, v_ref, ct_ref, do_ref, o_ref, lse_ref, dq_ref, dk_ref, dv_ref, dct_ref, dcq_ref,
             dq_sc, dk_sc, dv_sc, dc_sc, dcq_sc):
        kb = kb_tab[pl.program_id(0)]
        qb = qb_tab[pl.program_id(0)]
        jq = qb - kb * T // TQ
        lane = lax.broadcasted_iota(jnp.int32, (R, 128), 1)
        low = lane < HEAD_DIM
        ones_k = jnp.ones((T, 128), BF)
        ones_r = jnp.ones((8, R), BF)

        @pl.when(jnp.logical_and(kb == 0, jq == 0))
        def _():
            dq_sc[...] = jnp.zeros_like(dq_sc)
            dcq_sc[...] = jnp.zeros_like(dcq_sc)

        @pl.when(jq == 0)
        def _():
            dk_sc[...] = jnp.zeros_like(dk_sc)
            dv_sc[...] = jnp.zeros_like(dv_sc)
            dc_sc[...] = jnp.zeros_like(dc_sc)

        def step(diagonal):
            def pair(h):
                return slice((h // 2) * 128, (h // 2 + 1) * 128)

            def rows(rh):
                return slice(rh * R, (rh + 1) * R)

            def qrows(rh):
                return pl.ds(pl.multiple_of(qb * TQ + rh * R, R), R)

            def products(h, rh):
                mask = low if h % 2 == 0 else jnp.logical_not(low)
                qp = q_ref[rows(rh), pair(h)] * jnp.asarray(ATTN_SCALE, BF)
                dop = do_ref[rows(rh), pair(h)]
                qm = jnp.where(mask, qp, jnp.zeros_like(qp))
                dom = jnp.where(mask, dop, jnp.zeros_like(dop))
                s = dot(qm, k_ref[:, pair(h)], nt) - ct_ref[h:h + 1, :]
                if diagonal:
                    ahead = lax.broadcasted_iota(jnp.int32, (R, T), 1) - lax.broadcasted_iota(jnp.int32, (R, T), 0)
                    s = jnp.where(ahead <= qb * TQ + rh * R - kb * T, s, NEG_INF)
                return qm, dom, s, dot(dom, v_ref[:, pair(h)], nt)

            def pointwise(h, rh, qm, dom, s, dp):
                mask = low if h % 2 == 0 else jnp.logical_not(low)
                prod = do_ref[rows(rh), pair(h)].astype(F32) * o_ref[rows(rh), pair(h)].astype(F32)
                delta = jnp.sum(jnp.where(mask, prod, 0.0), axis=1, keepdims=True)
                p = jnp.exp(s - lse_ref[rows(rh), h:h + 1])
                ds = (p * (dp - delta)).astype(BF)
                return qm, dom, p.astype(BF), ds

            def gradients(h, qm, dom, p, ds):
                kx = jnp.concatenate([k_ref[:, pair(h)], ones_k], axis=1)
                return dot(ds, kx, nn), dot(qm, ds, tn), dot(dom, p, tn), dot(ones_r, ds, nn)

            st1, st2, even = {}, {}, {}
            dcq_tiles = [jnp.zeros((R, 128), F32) for _ in range(TQ // R)]
            nch = len(chains)
            for t in range(nch + 2):
                if t < nch:
                    st1[t] = products(*chains[t])
                done = gradients(chains[t - 2][0], *st2.pop(t - 2)) if t >= 2 else None
                if 1 <= t <= nch:
                    st2[t - 1] = pointwise(*chains[t - 1], *st1.pop(t - 1))
                if done is not None:
                    h, rh = chains[t - 2]
                    dq_rsum, dk, dv, csum = done
                    dq = dq_rsum[:, :128]
                    dcq_tiles[rh] = jnp.where(lane == h, dq_rsum[:, 128:], dcq_tiles[rh])
                    dc_sc[h:h + 1, :] -= csum[0:1, :]
                    if h % 2 == 0:
                        even[rh] = (dq, dk, dv)
                    else:
                        dq0, dk0, dv0 = even.pop(rh)
                        dq_sc[qrows(rh), pair(h)] += jnp.where(low, dq0, dq) * ATTN_SCALE
                        dk_sc[h // 2] += dk0 + dk
                        dv_sc[h // 2] += dv0 + dv
            for rh in range(TQ // R):
                dcq_sc[qrows(rh), :] += dcq_tiles[rh]

        @pl.when(jq > 0)
        def _():
            step(False)

        @pl.when(jq == 0)
        def _():
            step(True)

        @pl.when(qb == nq - 1)
        def _():
            for hp in range(N_HEADS // 2):
                dk_ref[:, hp * 128:(hp + 1) * 128] = dk_sc[hp].T.astype(BF)
                dv_ref[:, hp * 128:(hp + 1) * 128] = dv_sc[hp].T.astype(BF)
            dct_ref[...] = dc_sc[...]

        @pl.when(jnp.logical_and(kb == nk - 1, qb == nq - 1))
        def _():
            def put(i, carry):
                r = pl.ds(pl.multiple_of(i * T, T), T)
                dq_ref[r, :] = dq_sc[r, :].astype(BF)
                return carry
            lax.fori_loop(0, nk, put, 0)
            dcq_ref[...] = dcq_sc[...]

    qblk = lambda col: pl.BlockSpec((TQ, 1024), lambda t, kb, qb: (qb[t], col))
    kblk = lambda col: pl.BlockSpec((T, 1024), lambda t, kb, qb: (kb[t], col))
    whole = pl.BlockSpec((S, 1024), lambda t, kb, qb: (0, 0))
    outs = _hosted_call(
        body, grid=(len(pairs),),
        in_specs=[qblk(0), kblk(1), kblk(2), pl.BlockSpec((GATE_LANES, T), lambda t, kb, qb: (0, kb[t])), qblk(0), qblk(0),
                  pl.BlockSpec((TQ, 128), lambda t, kb, qb: (qb[t], 0))],
        out_specs=[whole, kblk(0), kblk(0), pl.BlockSpec((GATE_LANES, T), lambda t, kb, qb: (0, kb[t])),
                   pl.BlockSpec((S, GATE_LANES), lambda t, kb, qb: (0, 0))],
        out_shape=[jax.ShapeDtypeStruct((S, 1024), BF)] * 3 + [jax.ShapeDtypeStruct((GATE_LANES, S), F32),
                                                               jax.ShapeDtypeStruct((S, GATE_LANES), F32)],
        scratch_shapes=[pltpu.VMEM((S, 1024), F32), pltpu.VMEM((N_HEADS // 2, 128, T), F32), pltpu.VMEM((N_HEADS // 2, 128, T), F32),
                        pltpu.VMEM((GATE_LANES, T), F32), pltpu.VMEM((S, GATE_LANES), F32)],
        args=(qkv, qkv, qkv, cT, do, o, lse), name=name, guest=guest, schedule=schedule)
    return outs if guest is None else (outs[:5], outs[5:])


def _to_natural(src_ref, buf, d, width):
    rows = buf.shape[1]
    for r in range(d):
        for ch in range(width // 128):
            lanes = slice(r * width + ch * 128, r * width + (ch + 1) * 128)
            buf.at[ch][pl.ds(r, rows // d, stride=d), :] = src_ref[:, lanes].astype(F32)
    return jnp.concatenate([buf[ch] for ch in range(width // 128)], axis=1)


def _to_view(val, buf, dst_ref, d, width):
    rows = buf.shape[1]
    for ch in range(width // 128):
        buf[ch] = val[:, ch * 128:(ch + 1) * 128]
    for r in range(d):
        for ch in range(width // 128):
            lanes = slice(r * width + ch * 128, r * width + (ch + 1) * 128)
            dst_ref[:, lanes] = buf.at[ch][pl.ds(r, rows // d, stride=d), :].astype(dst_ref.dtype)


def _view_spec(rows, d, width):
    return pl.BlockSpec((rows // d, d * width), lambda i, *_: (i, 0))


def _combine_groups(os, lses, dils, *, name):
    ng = len(os)
    S = os[0].shape[0] * dils[0]
    tm = ROW_TILE
    views = sorted(set(dils))

    def body(*refs):
        o_refs, l_refs = refs[:ng], refs[ng:2 * ng]
        outs = refs[2 * ng:2 * ng + 2 * len(views)]
        wide, narrow = refs[-2], refs[-1]
        ls = [l_refs[g][...] if dils[g] == 1 else _to_natural(l_refs[g], narrow, dils[g], 128) for g in range(ng)]
        m = functools.reduce(jnp.maximum, ls)
        es = [jnp.exp(l - m) for l in ls]
        den = functools.reduce(jnp.add, es)
        ws = [e / den for e in es]
        lse = m + jnp.log(den)
        og = [o_refs[g][...] if dils[g] == 1 else _to_natural(o_refs[g], wide, dils[g], 1024) for g in range(ng)]
        cols = []
        for h in range(N_HEADS):
            hs = slice(h * HEAD_DIM, (h + 1) * HEAD_DIM)
            acc = ws[0][:, h:h + 1] * og[0][:, hs]
            for g in range(1, ng):
                acc = acc + ws[g][:, h:h + 1] * og[g][:, hs]
            cols.append(acc)
        o = jnp.concatenate(cols, axis=1)
        for k, d in enumerate(views):
            if d == 1:
                outs[2 * k][...] = o.astype(BF)
                outs[2 * k + 1][...] = lse
            else:
                _to_view(o, wide, outs[2 * k], d, 1024)
                _to_view(lse, narrow, outs[2 * k + 1], d, 128)

    out_specs, out_shape = [], []
    for d in views:
        out_specs += [_view_spec(tm, d, 1024), _view_spec(tm, d, 128)]
        out_shape += [jax.ShapeDtypeStruct((S // d, d * 1024), BF), jax.ShapeDtypeStruct((S // d, d * 128), F32)]
    res = pl.pallas_call(
        body, grid=(S // tm,), in_specs=[_view_spec(tm, d, 1024) for d in dils] + [_view_spec(tm, d, 128) for d in dils],
        out_specs=out_specs, out_shape=out_shape,
        scratch_shapes=[pltpu.VMEM((8, tm, 128), F32), pltpu.VMEM((1, tm, 128), F32)],
        name=name, compiler_params=_params("parallel"),
    )(*os, *lses)
    return {d: (res[2 * k], res[2 * k + 1]) for k, d in enumerate(views)}


def _assemble(parts, rope_flags, rope, dils, *, name):
    n = len(parts)
    S = parts[0].shape[0] * dils[0]
    use_rope = any(rope_flags)
    tm = 256

    def body(*refs):
        out_ref, natural = refs[-2], refs[-1]
        for b in range(n):
            cols = slice(b * 1024, (b + 1) * 1024)
            d = dils[b]
            val = refs[b][...].astype(F32) if d == 1 else _to_natural(refs[b], natural, d, 1024)
            if rope_flags[b]:
                cos_ref, sa_ref, sb_ref = refs[n:n + 3]
                val = _rope_rotate(val, cos_ref[...], sa_ref[...], sb_ref[...])
            out_ref[:, cols] = val.astype(BF)

    in_specs = [_view_spec(tm, d, 1024) for d in dils]
    args = list(parts)
    if use_rope:
        in_specs += [pl.BlockSpec((tm, 128), lambda i: (i, 0))] * 3
        args += list(rope)
    return pl.pallas_call(
        body, grid=(S // tm,), in_specs=in_specs, out_specs=pl.BlockSpec((tm, n * 1024), lambda i: (i, 0)),
        out_shape=jax.ShapeDtypeStruct((S, n * 1024), BF), scratch_shapes=[pltpu.VMEM((8, tm, 128), F32)],
        name=name, compiler_params=_params("parallel"),
    )(*args)


GATE_ROWS = 512


def _gate_fwd(z, bf, *, name):
    S = z.shape[0]

    def body(z_ref, b_ref, ct_ref, carry):
        i = pl.program_id(0)

        @pl.when(i == 0)
        def _():
            carry[...] = jnp.zeros_like(carry)

        zz = z_ref[...] + b_ref[...]
        logf = jnp.minimum(zz, 0.0) - jnp.log(1.0 + jnp.exp(-jnp.abs(zz)))
        tri = (lax.broadcasted_iota(jnp.int32, (GATE_ROWS, GATE_ROWS), 0)
               >= lax.broadcasted_iota(jnp.int32, (GATE_ROWS, GATE_ROWS), 1)).astype(F32)
        cs = jnp.dot(tri, logf, precision=lax.Precision.HIGHEST, preferred_element_type=F32) + carry[...]
        ct_ref[...] = cs.T
        carry[...] = cs[GATE_ROWS - 1:GATE_ROWS, :]

    return pl.pallas_call(
        body, grid=(S // GATE_ROWS,),
        in_specs=[pl.BlockSpec((GATE_ROWS, GATE_LANES), lambda i: (i, 0)), pl.BlockSpec((1, GATE_LANES), lambda i: (0, 0))],
        out_specs=pl.BlockSpec((GATE_LANES, GATE_ROWS), lambda i: (0, i)),
        out_shape=jax.ShapeDtypeStruct((GATE_LANES, S), F32),
        scratch_shapes=[pltpu.VMEM((1, GATE_LANES), F32)], name=name, compiler_params=_params("arbitrary"),
    )(z, bf)


def _gate_bwd(z, bf, dcT, dcq, *, name):
    S = z.shape[0]
    nb = S // GATE_ROWS

    def body(z_ref, b_ref, dct_ref, dcq_ref, dz_ref, db_ref, carry):
        i = pl.program_id(0)

        @pl.when(i == 0)
        def _():
            carry[...] = jnp.zeros_like(carry)
            db_ref[...] = jnp.zeros_like(db_ref)

        dc = dct_ref[...].T + dcq_ref[...]
        tri = (lax.broadcasted_iota(jnp.int32, (GATE_ROWS, GATE_ROWS), 0)
               <= lax.broadcasted_iota(jnp.int32, (GATE_ROWS, GATE_ROWS), 1)).astype(F32)
        dl = jnp.dot(tri, dc, precision=lax.Precision.HIGHEST, preferred_element_type=F32) + carry[...]
        carry[...] = dl[0:1, :]
        zz = z_ref[...] + b_ref[...]
        dz = dl * (1.0 / (1.0 + jnp.exp(zz)))
        lane = lax.broadcasted_iota(jnp.int32, dz.shape, 1)
        dz = jnp.where(lane < N_HEADS, dz, 0.0)
        dz_ref[...] = dz.astype(BF)
        db_ref[...] += jnp.sum(dz, axis=0, keepdims=True)

    return pl.pallas_call(
        body, grid=(nb,),
        in_specs=[pl.BlockSpec((GATE_ROWS, GATE_LANES), lambda i: (nb - 1 - i, 0)), pl.BlockSpec((1, GATE_LANES), lambda i: (0, 0)),
                  pl.BlockSpec((GATE_LANES, GATE_ROWS), lambda i: (0, nb - 1 - i)),
                  pl.BlockSpec((GATE_ROWS, GATE_LANES), lambda i: (nb - 1 - i, 0))],
        out_specs=[pl.BlockSpec((GATE_ROWS, GATE_LANES), lambda i: (nb - 1 - i, 0)), pl.BlockSpec((1, GATE_LANES), lambda i: (0, 0))],
        out_shape=[jax.ShapeDtypeStruct((S, GATE_LANES), BF), jax.ShapeDtypeStruct((1, GATE_LANES), F32)],
        scratch_shapes=[pltpu.VMEM((1, GATE_LANES), F32)], name=name, compiler_params=_params("arbitrary"),
    )(z, bf, dcT, dcq)


def _rope_tables(S):
    half = ROT_DIM // 2
    inv_freq = ROPE_THETA ** (-jnp.arange(half, dtype=F32) * 2.0 / ROT_DIM)
    ang = jnp.arange(S, dtype=F32)[:, None] * inv_freq[None, :]
    cos, sin = jnp.cos(ang), jnp.sin(ang)
    zero = jnp.zeros((S, HEAD_DIM - ROT_DIM), F32)
    zh = jnp.zeros((S, half), F32)
    cos_h = jnp.concatenate([cos, cos, jnp.ones_like(zero)], axis=1)
    sa_h = jnp.concatenate([-sin, zh, zero], axis=1)
    sb_h = jnp.concatenate([zh, sin, zero], axis=1)
    two = lambda t: jnp.concatenate([t, t], axis=1)
    return two(cos_h), two(sa_h), two(sb_h)


def _ffn_fwd(h, norm, w_gu, w_down, tag):
    n = _rms_fwd(h, norm, name=f"ffn{tag}_norm")
    gu = _mm_nn(n, w_gu, tm=1024, tn=512, out_dtype=BF, name=f"ffn{tag}_gu")
    act = _swiglu_fwd(gu, name=f"ffn{tag}_act")
    out = _mm_nn(act, w_down, tm=512, tn=1024, out_dtype=F32, name=f"ffn{tag}_down", resid=h)
    return out, (h, n, gu, act)


def _ffn_bwd(dh, dhb, saved, norm, w_gu, w_down, tag, ride=None):
    h, n, gu, act = saved
    dact = _mm_nt(dhb, w_down, tm=512, to=1408, tn=1024, out_dtype=BF, name=f"ffn{tag}_dact")
    dw_down = _mm_tn(act, dhb, tk=1408, tn=1024, tm=512, out_dtype=BF, name=f"ffn{tag}_dwdown")
    dgu = _swiglu_bwd(gu, dact, name=f"ffn{tag}_dgu")
    dn_call = lambda guest: _mm_nt(dgu, w_gu, tm=512, to=1024, tn=1408, out_dtype=F32, name=f"ffn{tag}_dn", guest=guest)
    dn = dn_call(None) if ride is None else ride(dn_call)
    dw_gu = _mm_tn(n, dgu, tk=1024, tn=1408, tm=512, out_dtype=BF, name=f"ffn{tag}_dwgu")
    dx, dxb, dg = _rms_bwd(h, norm, dn, dh, name=f"ffn{tag}_dnorm")
    return dx, dxb, dg, dw_gu, dw_down


def _local_step(x, tgt, w, mats, fetch, exchange):
    S = x.shape[0]
    rope_f = _rope_tables(S)
    rope_b = (rope_f[0], -rope_f[1], -rope_f[2])
    g, partial, landed = {}, {}, {}
    w = dict(w, ffn_w_gu={}, ffn_w_down={})

    def bring(call, indices):
        bufs = [mats[wi] for wi in indices]
        if fetch is None:
            return call(None), bufs
        return call(fetch(indices, bufs))

    def ride(call, indices):
        guest = exchange(indices, [partial[wi] for wi in indices]) if indices else None
        res = call(guest)
        if guest is None:
            return res
        res, outs = res
        landed.update(zip(indices, outs))
        return res

    n0 = _rms_fwd(x, w["a_norm"], name="a_norm")
    dils = [d for _, d in DILATED_PATTERNS]
    projs, (w["ffn_w_gu"][0], w["ffn_w_down"][0]) = bring(
        lambda guest: _mm_nn(n0, w["a_w_in"], tm=512, tn=1024, out_dtype=BF, name="a_proj", rope=rope_f, guest=guest,
                             groups=dils), [4, 6])
    block = lambda t, dil: (lambda r: t * dil + r)
    o_parts, lse_parts = [], []
    for gi, (window, dil) in enumerate(DILATED_PATTERNS):
        pv = projs[gi]
        attend = lambda guest: _band_fwd(pv, pv, pv, block(0, dil), block(1, dil), block(2, dil), dil=dil, T=128,
                                         window=window // dil, name=f"a_attn{gi}", guest=guest)
        if gi == 0:
            (o_g, lse_g), (w["a_w_out"],) = bring(attend, [1])
        elif gi == 1:
            (o_g, lse_g), (b_in,) = bring(attend, [2])
        else:
            (o_g, lse_g), (w["b_w_out"],) = bring(attend, [3])
        o_parts.append(o_g)
        lse_parts.append(lse_g)
    b_in = b_in.transpose(1, 0, 2).reshape(D_MODEL, -1)
    w["b_w_qkv"] = b_in[:, :QKV_COLS]
    w["b_w_f"] = jnp.pad(b_in[:, QKV_COLS:], ((0, 0), (0, GATE_LANES + QKV_COLS - b_in.shape[1])))
    mixed = _combine_groups(o_parts, lse_parts, dils, name="a_combine")
    o_a = mixed[1][0]
    h1 = _mm_nn(o_a, w["a_w_out"], tm=512, tn=1024, out_dtype=F32, name="a_out", resid=x)
    h2, ffn0 = _ffn_fwd(h1, w["ffn_norm"][0:1], w["ffn_w_gu"][0], w["ffn_w_down"][0], 0)

    n2 = _rms_fwd(h2, w["b_norm"], name="b_norm")
    qkv = _mm_nn(n2, w["b_w_qkv"], tm=512, tn=1024, out_dtype=BF, name="b_proj")
    zf = _mm_nn(n2, w["b_w_f"], tm=512, tn=GATE_LANES, out_dtype=F32, name="b_gate_proj")
    cT = _gate_fwd(zf, w["b_f"], name="b_gate")
    (o_b, lse_b), (w["ffn_w_gu"][1], w["ffn_w_down"][1]) = bring(lambda guest: _fox_fwd(qkv, cT, name="b_attn", guest=guest), [5, 7])
    h3 = _mm_nn(o_b, w["b_w_out"], tm=512, tn=1024, out_dtype=F32, name="b_out", resid=h2)
    h4, ffn1 = _ffn_fwd(h3, w["ffn_norm"][1:2], w["ffn_w_gu"][1], w["ffn_w_down"][1], 1)

    loss, dh4, dh4b, g["final_norm"] = _loss_head(h4, w["final_norm"], tgt, name="loss_head")

    dh3, dh3b, dg_f1, partial[5], partial[7] = _ffn_bwd(dh4, dh4b, ffn1, w["ffn_norm"][1:2], w["ffn_w_gu"][1], w["ffn_w_down"][1], 1)

    do_b = _mm_nt(dh3b, w["b_w_out"], tm=512, to=1024, tn=1024, out_dtype=BF, name="b_do")
    partial[3] = _mm_tn(o_b, dh3b, tk=1024, tn=1024, tm=512, out_dtype=BF, name="b_dwout")
    dq, dk, dv, dcT, dcq = ride(lambda guest: _fox_bwd(qkv, cT, do_b, o_b, lse_b, name="b_attn_bwd", guest=guest), [5, 7, 3])
    dz, g["b_f"] = _gate_bwd(zf, w["b_f"], dcT, dcq, name="b_gate_bwd")
    dqkv = _assemble([dq, dk, dv], [False] * 3, None, [1] * 3, name="b_dproj")
    dn2 = _mm_nt(dz, w["b_w_f"], tm=512, to=1024, tn=GATE_LANES, out_dtype=F32, name="b_dn_gate")
    dn2 = _mm_nt(dqkv, w["b_w_qkv"], tm=512, to=1024, tn=1024, out_dtype=F32, name="b_dn", add=dn2)
    g_qkv = _mm_tn(n2, dqkv, tk=1024, tn=1024, tm=512, out_dtype=BF, name="b_dwqkv")
    g_f = _mm_tn(n2, dz, tk=1024, tn=GATE_LANES, tm=512, out_dtype=BF, name="b_dwf")
    g_b_in = jnp.concatenate([g_qkv, g_f[:, :N_HEADS]], axis=1)
    partial[2] = g_b_in.reshape(D_MODEL, N_CHIPS, -1).transpose(1, 0, 2)
    dh2, dh2b, g["b_norm"] = _rms_bwd(h2, w["b_norm"], dn2, dh3, name="b_dnorm")

    dh1, dh1b, dg_f0, partial[4], partial[6] = _ffn_bwd(dh2, dh2b, ffn0, w["ffn_norm"][0:1], w["ffn_w_gu"][0], w["ffn_w_down"][0], 0,
                                                      ride=lambda call: ride(call, [2]))
    g["ffn_norm"] = jnp.concatenate([dg_f0, dg_f1], axis=0)

    views = tuple(sorted(set(dils)))
    do_a = dict(zip(views, _mm_nt(dh1b, w["a_w_out"], tm=512, to=1024, tn=1024, out_dtype=BF, name="a_do", views=views)))
    partial[1] = _mm_tn(o_a, dh1b, tk=1024, tn=1024, tm=512, out_dtype=BF, name="a_dwout")
    riders = {0: [4], 1: [6, 1], 2: []}
    parts = []
    for gi, (window, dil) in enumerate(DILATED_PATTERNS):
        pv = projs[gi]
        res = ride(lambda guest: _band_bwd(pv, pv, pv, block(0, dil), block(1, dil), block(2, dil), do_a[dil],
                                           mixed[dil][0], mixed[dil][1], dil=dil, T=128,
                                           window=window // dil, name=f"a_attn_bwd{gi}", guest=guest), riders[gi])
        parts += list(res)
    dproj = _assemble(parts, [True, True, False] * 3, rope_b, [d for _, d in DILATED_PATTERNS for _ in range(3)], name="a_dproj")
    partial[0] = _mm_tn(n0, dproj, tk=1024, tn=1024, tm=512, out_dtype=BF, name="a_dwin")
    dn0 = ride(lambda guest: _mm_nt(dproj, w["a_w_in"], tm=512, to=1024, tn=1024, out_dtype=F32, name="a_dn", guest=guest), [0])
    dx, _, g["a_norm"] = _rms_bwd(x, w["a_norm"], dn0, dh1, name="a_dnorm")
    return loss, dx, g, partial, landed


ANY = pl.BlockSpec(memory_space=pl.ANY)


def _place():
    x, y, c = lax.axis_index("x"), lax.axis_index("y"), lax.axis_index("c")
    chips = [(1 - x, y), (x, 1 - y), (1 - x, 1 - y)]
    return x, y, c, chips


def _shard_slice(ref, kind, rows, cols, s, half):
    hr = rows // 2
    if kind == "col":
        return ref.at[pl.ds(half * hr, hr), pl.ds(pl.multiple_of(s * cols, 128), cols)]
    if kind == "row":
        return ref.at[pl.ds(pl.multiple_of(s * rows + half * hr, 16), hr), :]
    return ref.at[s, pl.ds(half * hr, hr), :]


def _whole_shape(kind, rows, cols):
    return {"col": (rows, N_CHIPS * cols), "row": (N_CHIPS * rows, cols), "stack": (N_CHIPS, rows, cols)}[kind]


def _own_block(kind, rows, tr, cols):
    per = rows // tr

    def spec(half_rows):
        off = (lambda p: 0) if half_rows is None else (lambda p: p[1] * (half_rows // tr))
        if kind == "col":
            return pl.BlockSpec((tr, cols), lambda i, p: (off(p) + i, p[0]))
        if kind == "row":
            return pl.BlockSpec((tr, cols), lambda i, p: (p[0] * per + off(p) + i, 0))
        return pl.BlockSpec((None, tr, cols), lambda i, p: (p[0], off(p) + i, 0))
    return spec


def _place_shard(shards, layer, kind, place, *, name):
    _, rows, cols = shards.shape
    tr = 256 if rows % 256 == 0 else rows // 2

    def body(p_ref, s_ref, o_ref):
        o_ref[...] = s_ref[...].astype(BF)

    return pl.pallas_call(
        body,
        grid_spec=pltpu.PrefetchScalarGridSpec(
            num_scalar_prefetch=1, grid=(rows // tr,),
            in_specs=[pl.BlockSpec((None, tr, cols), lambda i, p: (layer, i, 0))],
            out_specs=_own_block(kind, rows, tr, cols)(None)),
        out_shape=jax.ShapeDtypeStruct(_whole_shape(kind, rows, cols), BF),
        name=name, compiler_params=_params("arbitrary"),
    )(place, shards)


def _gather_weights(placed, kinds, dims):
    nw = len(placed)

    def body(*refs):
        dst = refs[nw:2 * nw]
        send_sems, recv_sems = refs[2 * nw:]
        x, y, c, chips = _place()
        me = 2 * x + y
        sibling = (x, y, 1 - c)

        def copy(wi, k, s, half, to):
            p = _shard_slice(dst[wi], kinds[wi], dims[wi][0], dims[wi][1], s, half)
            return pltpu.make_async_remote_copy(src_ref=p, dst_ref=p, send_sem=send_sems.at[wi * 6 + k],
                                                recv_sem=recv_sems.at[wi * 6 + k], device_id=to, device_id_type=MESH)

        first, passed = [], []
        for wi in range(nw):
            for j, chip in enumerate(chips):
                cp = copy(wi, j, me, c, (*chip, c))
                cp.start()
                first.append(cp)
        for wi in range(nw):
            for j, chip in enumerate(chips):
                s = 2 * chip[0] + chip[1]
                copy(wi, j, s, c, (x, y, c)).wait_recv()
                cp = copy(wi, 3 + j, s, c, sibling)
                cp.start()
                passed.append(cp)
        for wi in range(nw):
            for j, chip in enumerate(chips):
                s = 2 * chip[0] + chip[1]
                copy(wi, 3 + j, s, 1 - c, (x, y, c)).wait_recv()
        for cp in first + passed:
            cp.wait_send()

    return pl.pallas_call(
        body, in_specs=[ANY] * nw, out_specs=[ANY] * nw,
        out_shape=[jax.ShapeDtypeStruct(p.shape, p.dtype) for p in placed],
        input_output_aliases={wi: wi for wi in range(nw)},
        scratch_shapes=[pltpu.SemaphoreType.DMA((nw * 6,)), pltpu.SemaphoreType.DMA((nw * 6,))],
        name="gather_weights",
    )(*placed)


def _fetch_guest(placed, kinds, dims):
    nw = len(placed)

    def copies(dst, send_sems, recv_sems, incoming):
        x, y, c, chips = _place()
        out = []
        for wi in range(nw):
            for j, chip in enumerate(chips):
                s = 2 * chip[0] + chip[1] if incoming else 2 * x + y
                to = (x, y, c) if incoming else (*chip, c)
                for half in range(2):
                    p = _shard_slice(dst[wi], kinds[wi], dims[wi][0], dims[wi][1], s, half)
                    k = wi * 6 + 2 * j + half
                    out.append(pltpu.make_async_remote_copy(src_ref=p, dst_ref=p, send_sem=send_sems.at[k],
                                                            recv_sem=recv_sems.at[k], device_id=to, device_id_type=MESH))
        return out

    def start(src, dst, sems):
        for cp in copies(dst, sems[0], sems[1], False):
            cp.start()

    def finish(src, dst, sems):
        for cp in copies(dst, sems[0], sems[1], True):
            cp.wait_recv()
        for cp in copies(dst, sems[0], sems[1], False):
            cp.wait_send()

    return dict(args=list(placed), out_shape=[jax.ShapeDtypeStruct(p.shape, p.dtype) for p in placed],
                scratch=[pltpu.SemaphoreType.DMA((nw * 6,)), pltpu.SemaphoreType.DMA((nw * 6,))],
                start=start, finish=finish, in_place=True)


def _scatter_guest(partials, kinds, dims):
    nw = len(partials)

    def copies(src, send_sems, recv_sems, dst):
        x, y, c, chips = _place()
        me = 2 * x + y
        out = []
        for wi in range(nw):
            rows, cols = dims[wi]

            def part(s, half, wi=wi, rows=rows, cols=cols):
                return _shard_slice(src[wi], kinds[wi], rows, cols, s, half)

            for j, chip in enumerate(chips):
                s = 2 * chip[0] + chip[1]
                for half in range(2):
                    slot = 2 * j + (c if half == 0 else 1 - c)
                    out.append(pltpu.make_async_remote_copy(
                        src_ref=part(s, half), dst_ref=dst[wi].at[slot],
                        send_sem=send_sems.at[wi * 7 + 2 * j + half], recv_sem=recv_sems.at[wi * 7 + slot],
                        device_id=(*chip, half), device_id_type=MESH))
            out.append(pltpu.make_async_remote_copy(
                src_ref=part(me, 1 - c), dst_ref=dst[wi].at[6],
                send_sem=send_sems.at[wi * 7 + 6], recv_sem=recv_sems.at[wi * 7 + 6],
                device_id=(x, y, 1 - c), device_id_type=MESH))
        return out

    def start(src, dst, sems):
        for cp in copies(src, sems[0], sems[1], dst):
            cp.start()

    def finish(src, dst, sems):
        x, y, c, _ = _place()
        for wi in range(nw):
            for slot in range(7):
                pltpu.make_async_remote_copy(
                    src_ref=dst[wi].at[slot], dst_ref=dst[wi].at[slot],
                    send_sem=sems[0].at[wi * 7 + slot], recv_sem=sems[1].at[wi * 7 + slot],
                    device_id=(x, y, c), device_id_type=MESH).wait_recv()
        for cp in copies(src, sems[0], sems[1], dst):
            cp.wait_send()

    return dict(args=list(partials), out_shape=[jax.ShapeDtypeStruct((7, d[0] // 2, d[1]), BF) for d in dims],
                scratch=[pltpu.SemaphoreType.DMA((nw * 7,)), pltpu.SemaphoreType.DMA((nw * 7,))],
                start=start, finish=finish)


def _sum_slots(slots, partial, kind, dims, place, *, name, into=None, layer=None, n_layers=1):
    rows, cols = dims
    hr = rows // 2
    tr = hr if 8 * hr * cols * 2 <= 6 * 1024 * 1024 else 128
    assert hr % tr == 0

    def body(p_ref, b_ref, own_ref, *rest):
        o_ref = rest[-1]
        acc = own_ref[...].astype(F32)
        for k in range(7):
            acc = acc + b_ref[k].astype(F32)
        o_ref[...] = acc

    half = lambda p: p[1] * (hr // tr)
    if n_layers == 1:
        out_spec = pl.BlockSpec((tr, cols), lambda i, p: (half(p) + i, 0))
        out_shape = jax.ShapeDtypeStruct((rows, cols), F32)
    else:
        out_spec = pl.BlockSpec((None, tr, cols), lambda i, p: (layer, half(p) + i, 0))
        out_shape = jax.ShapeDtypeStruct((n_layers, rows, cols), F32)
    in_specs = [pl.BlockSpec((7, tr, cols), lambda i, p: (0, i, 0)), _own_block(kind, rows, tr, cols)(hr)]
    args = [place, slots, partial]
    aliases = {}
    if into is not None:
        in_specs.append(ANY)
        args.append(into)
        aliases = {3: 0}
    return pl.pallas_call(
        body,
        grid_spec=pltpu.PrefetchScalarGridSpec(num_scalar_prefetch=1, grid=(hr // tr,), in_specs=in_specs, out_specs=out_spec),
        out_shape=out_shape, input_output_aliases=aliases, name=name, compiler_params=_params("arbitrary"),
    )(*args)


def _pair_exchange(bufs, members):
    nw = len(members)

    def body(*refs):
        dst = refs[len(bufs):2 * len(bufs)]
        send_sems, recv_sems = refs[2 * len(bufs):]
        x, y, c, _ = _place()

        def rows_of(wi, half):
            bi, l = members[wi]
            ref = dst[bi] if l is None else dst[bi].at[l]
            hr = ref.shape[0] // 2
            return ref.at[pl.ds(pl.multiple_of(half * hr, 8), hr), :]

        def copy(wi, half, to):
            p = rows_of(wi, half)
            return pltpu.make_async_remote_copy(src_ref=p, dst_ref=p, send_sem=send_sems.at[wi], recv_sem=recv_sems.at[wi],
                                                device_id=to, device_id_type=MESH)

        sent = []
        for wi in range(nw):
            cp = copy(wi, c, (x, y, 1 - c))
            cp.start()
            sent.append(cp)
        for wi in range(nw):
            copy(wi, 1 - c, (x, y, c)).wait_recv()
        for cp in sent:
            cp.wait_send()

    return pl.pallas_call(
        body, in_specs=[ANY] * len(bufs), out_specs=[ANY] * len(bufs),
        out_shape=[jax.ShapeDtypeStruct(b.shape, b.dtype) for b in bufs],
        input_output_aliases={i: i for i in range(len(bufs))},
        scratch_shapes=[pltpu.SemaphoreType.DMA((nw,)), pltpu.SemaphoreType.DMA((nw,))],
        name="pair_exchange",
    )(*bufs)


SMALL_ROWS = 8


def _allreduce_small(v, *, name):
    assert v.shape == (SMALL_ROWS, D_MODEL)

    def body(v_ref, o_ref, buf, send_sems, recv_sems):
        x, y, c, _ = _place()
        me = 4 * x + 2 * y + c
        buf[me] = v_ref[...]
        sent = []
        for k in range(1, 8):
            bx, by, bc = (k >> 2) & 1, (k >> 1) & 1, k & 1
            peer = (1 - x if bx else x, 1 - y if by else y, 1 - c if bc else c)
            cp = pltpu.make_async_remote_copy(src_ref=v_ref, dst_ref=buf.at[me], send_sem=send_sems.at[k - 1],
                                              recv_sem=recv_sems.at[k - 1], device_id=peer, device_id_type=MESH)
            cp.start()
            sent.append(cp)
        for k in range(1, 8):
            bx, by, bc = (k >> 2) & 1, (k >> 1) & 1, k & 1
            peer = 4 * (1 - x if bx else x) + 2 * (1 - y if by else y) + (1 - c if bc else c)
            pltpu.make_async_remote_copy(src_ref=v_ref, dst_ref=buf.at[peer], send_sem=send_sems.at[k - 1],
                                         recv_sem=recv_sems.at[k - 1], device_id=(x, y, c), device_id_type=MESH).wait_recv()
        for cp in sent:
            cp.wait_send()
        acc = buf[0]
        for d in range(1, 8):
            acc = acc + buf[d]
        o_ref[...] = acc

    vmem = pl.BlockSpec(memory_space=pltpu.VMEM)
    return pl.pallas_call(
        body, in_specs=[vmem], out_specs=vmem, out_shape=jax.ShapeDtypeStruct(v.shape, F32),
        scratch_shapes=[pltpu.VMEM((8,) + v.shape, F32), pltpu.SemaphoreType.DMA((7,)), pltpu.SemaphoreType.DMA((7,))],
        name=name,
    )(v)


def _adamw(w, g, m, v, *, name):
    R, C = w.shape
    tr = R
    if R * C * 4 > 1024 * 1024:
        tr = max(t for t in range(8, R, 8) if R % t == 0 and t * C * 4 <= 1024 * 1024)

    def body(w_ref, g_ref, m_ref, v_ref, d_ref, m2_ref, v2_ref):
        gg = g_ref[...]
        m2 = ADAM_B1 * m_ref[...] + (1.0 - ADAM_B1) * gg
        v2 = ADAM_B2 * v_ref[...] + (1.0 - ADAM_B2) * jnp.square(gg)
        m_hat = m2 / (1.0 - ADAM_B1 ** ADAM_STEP)
        v_hat = v2 / (1.0 - ADAM_B2 ** ADAM_STEP)
        d_ref[...] = -ADAM_LR * (m_hat / (jnp.sqrt(v_hat) + ADAM_EPS) + ADAM_WD * w_ref[...])
        m2_ref[...] = m2
        v2_ref[...] = v2

    blk = pl.BlockSpec((tr, C), lambda i: (i, 0))
    out = jax.ShapeDtypeStruct((R, C), F32)
    return pl.pallas_call(
        body, grid=(R // tr,), in_specs=[blk] * 4, out_specs=[blk] * 3, out_shape=[out] * 3,
        name=name, compiler_params=_params("parallel"),
    )(w, g, m, v)


WEIGHT_ORDER = ("a_norm", "a_w_in", "a_w_out", "b_norm", "b_w_in", "b_f", "b_w_out", "ffn_norm", "ffn_w_gu",
                "ffn_w_down", "final_norm")
MATRICES = (("a_w_in", 0, "col"), ("a_w_out", 0, "row"), ("b_w_in", 0, "stack"), ("b_w_out", 0, "row"),
            ("ffn_w_gu", 0, "col"), ("ffn_w_gu", 1, "col"), ("ffn_w_down", 0, "row"), ("ffn_w_down", 1, "row"))
MATRIX_GROUPS = ([0], [1], [2], [3], [4, 5], [6, 7])
GROUP_NAMES = ("a_w_in", "a_w_out", "b_w_in", "b_w_out", "ffn_w_gu", "ffn_w_down")
QKV_COLS = 3 * N_HEADS * HEAD_DIM


def kernel(x, a_norm, a_w_in, a_w_out, b_norm, b_w_in, b_f, b_w_out, ffn_norm, ffn_w_gu, ffn_w_down, final_norm, loss_target, m_a_norm, m_a_w_in, m_a_w_out, m_b_norm, m_b_w_in, m_b_f, m_b_w_out, m_ffn_norm, m_ffn_w_gu, m_ffn_w_down, m_final_norm, v_a_norm, v_a_w_in, v_a_w_out, v_b_norm, v_b_w_in, v_b_f, v_b_w_out, v_ffn_norm, v_ffn_w_gu, v_ffn_w_down, v_final_norm):
    given = dict(a_norm=a_norm, a_w_in=a_w_in, a_w_out=a_w_out, b_norm=b_norm, b_w_in=b_w_in, b_f=b_f, b_w_out=b_w_out,
                 ffn_norm=ffn_norm, ffn_w_gu=ffn_w_gu, ffn_w_down=ffn_w_down, final_norm=final_norm)
    mom_m = dict(a_norm=m_a_norm, a_w_in=m_a_w_in, a_w_out=m_a_w_out, b_norm=m_b_norm, b_w_in=m_b_w_in, b_f=m_b_f,
                 b_w_out=m_b_w_out, ffn_norm=m_ffn_norm, ffn_w_gu=m_ffn_w_gu, ffn_w_down=m_ffn_w_down, final_norm=m_final_norm)
    mom_v = dict(a_norm=v_a_norm, a_w_in=v_a_w_in, a_w_out=v_a_w_out, b_norm=v_b_norm, b_w_in=v_b_w_in, b_f=v_b_f,
                 b_w_out=v_b_w_out, ffn_norm=v_ffn_norm, ffn_w_gu=v_ffn_w_gu, ffn_w_down=v_ffn_w_down, final_norm=v_final_norm)
    chip = 2 * lax.axis_index("x") + lax.axis_index("y")
    core = lax.axis_index("c")
    bn_cols = b_norm.shape[1]

    placed = lax.dynamic_update_slice(jnp.zeros((SMALL_ROWS, D_MODEL), F32), b_norm, (0, chip * bn_cols))
    placed = placed * (core == 0).astype(F32)
    b_norm_full = _allreduce_small(placed, name="gather_b_norm")[0:1]

    place = jnp.stack([chip, core]).astype(jnp.int32)
    kinds = [k for _, _, k in MATRICES]
    dims = [given[n].shape[1:] for n, _, _ in MATRICES]
    placed = [_place_shard(given[n], l, k, place, name=f"place_{n}{l}") for n, l, k in MATRICES]
    first = _gather_weights(placed[:1], kinds[:1], dims[:1])
    mats = dict(enumerate(list(first) + placed[1:]))
    gate_cols = b_f.shape[1]
    w = dict(a_norm=a_norm, a_w_in=mats[0], b_norm=b_norm_full,
             b_f=jnp.pad(b_f, ((0, 0), (0, GATE_LANES - gate_cols))), ffn_norm=ffn_norm,
             final_norm=final_norm.reshape(1, D_MODEL))

    def fetch(indices, bufs):
        return _fetch_guest(bufs, [kinds[i] for i in indices], [dims[i] for i in indices])

    def exchange(indices, parts):
        return _scatter_guest(parts, [kinds[i] for i in indices], [dims[i] for i in indices])

    loss, dx, g, partials, slots = _local_step(x[0], loss_target[0], w, mats, fetch, exchange)
    bufs, members = [], []
    for group in MATRIX_GROUPS:
        buf = None
        for l, wi in enumerate(group):
            n = MATRICES[wi][0]
            buf = _sum_slots(slots[wi], partials[wi], kinds[wi], dims[wi], place, name=f"sum_{n}{l}", into=buf,
                             layer=l, n_layers=len(group))
            members.append((len(bufs), l if len(group) > 1 else None))
        bufs.append(buf)
    reduced = dict(zip(GROUP_NAMES, _pair_exchange(bufs, members)))

    small = jnp.concatenate([g["a_norm"], g["b_norm"], g["ffn_norm"], g["final_norm"],
                             jnp.pad(g["b_f"], ((0, 0), (0, D_MODEL - GATE_LANES))),
                             jnp.zeros((SMALL_ROWS - 6, D_MODEL), F32)], axis=0)
    small = _allreduce_small(small, name="allreduce_small")
    grads = dict(reduced)
    grads["a_norm"] = small[0:1]
    grads["b_norm"] = lax.dynamic_slice(small, (1, chip * bn_cols), (1, bn_cols))
    grads["ffn_norm"] = small[2:4]
    grads["final_norm"] = small[4]
    grads["b_f"] = small[5:6, :gate_cols]

    out_g, out_d, out_m, out_v = [], [], [], []
    for n in WEIGHT_ORDER:
        shape = given[n].shape
        two_d = (1, shape[0]) if len(shape) == 1 else (-1, shape[-1])
        d, m2, v2 = _adamw(given[n].reshape(two_d), grads[n].reshape(two_d), mom_m[n].reshape(two_d),
                           mom_v[n].reshape(two_d), name=f"adamw_{n}")
        out_g.append(grads[n].reshape(shape))
        out_d.append(d.reshape(shape))
        out_m.append(m2.reshape(shape))
        out_v.append(v2.reshape(shape))

    total = lax.psum(loss[0, 0], MESH_AXES)
    return (total, dx[None], *out_g, *out_d, *out_m, *out_v)
```

```python
import functools

import jax
import jax.numpy as jnp
from jax import lax
from jax.experimental import pallas as pl
from jax.experimental.pallas import tpu as pltpu

F32 = jnp.float32
BF = jnp.bfloat16

D_MODEL = 1024
N_HEADS = 16
HEAD_DIM = 64
D_FF = 2816
DILATED_PATTERNS = ((128, 1), (512, 4), (2048, 16))
ROT_DIM = 16
ROPE_THETA = 500000.0
RMS_EPS = 1e-6
NEG_INF = -1e30
ATTN_SCALE = HEAD_DIM ** -0.5
GATE_LANES = 128
N_CHIPS = 4
MESH_AXES = ("x", "y", "c")
MESH = pl.DeviceIdType.MESH

ADAM_LR = 0.001
ADAM_B1 = 0.9
ADAM_B2 = 0.999
ADAM_EPS = 1e-08
ADAM_WD = 0.01
ADAM_STEP = 10

VMEM_LIMIT_BYTES = 56 * 1024 * 1024


def _params(*sem):
    return pltpu.CompilerParams(dimension_semantics=sem, vmem_limit_bytes=VMEM_LIMIT_BYTES)


def _hosted_call(body, *, grid, in_specs, out_specs, out_shape, scratch_shapes, args, name, guest=None, schedule=()):
    params = _params(*(["arbitrary"] * len(grid)))
    ns = len(schedule)

    def call(kernel, in_specs, out_specs, out_shape, scratch_shapes, aliases, args):
        spec = pltpu.PrefetchScalarGridSpec(num_scalar_prefetch=ns, grid=grid, in_specs=in_specs, out_specs=out_specs,
                                            scratch_shapes=scratch_shapes)
        return pl.pallas_call(kernel, grid_spec=spec, out_shape=out_shape, input_output_aliases=aliases, name=name,
                              compiler_params=params)(*schedule, *args)

    if guest is None:
        return call(body, in_specs, out_specs, out_shape, scratch_shapes, {}, args)
    n_in, n_out, n_scr = ns + len(in_specs), len(out_specs), len(scratch_shapes)
    g_in, g_out = len(guest["args"]), len(guest["out_shape"])
    any_spec = pl.BlockSpec(memory_space=pl.ANY)

    def wrapped(*refs):
        i1 = n_in + g_in
        o1 = i1 + n_out
        o2 = o1 + g_out
        s1 = o2 + n_scr
        guest_refs = (refs[n_in:i1], refs[o1:o2], refs[s1:])
        ids = [pl.program_id(d) for d in range(len(grid))]
        first = functools.reduce(jnp.logical_and, [i == 0 for i in ids])
        last = functools.reduce(jnp.logical_and, [i == g - 1 for i, g in zip(ids, grid)])

        @pl.when(first)
        def _():
            guest["start"](*guest_refs)

        body(*refs[:n_in], *refs[i1:o1], *refs[o2:s1])

        @pl.when(last)
        def _():
            guest["finish"](*guest_refs)

    aliases = {n_in + k: n_out + k for k in range(g_in)} if guest.get("in_place") else {}
    return call(wrapped, list(in_specs) + [any_spec] * g_in, list(out_specs) + [any_spec] * g_out,
                list(out_shape) + list(guest["out_shape"]), list(scratch_shapes) + list(guest["scratch"]), aliases,
                list(args) + list(guest["args"]))


def _rope_rotate(t, cos, sin_a, sin_b):
    outs = []
    for cidx in range(t.shape[1] // 128):
        tc = t[:, cidx * 128:(cidx + 1) * 128]
        outs.append(tc * cos + pltpu.roll(tc, 120, 1) * sin_a + pltpu.roll(tc, 8, 1) * sin_b)
    return jnp.concatenate(outs, axis=1)


def _mm_nn(a, b, *, tm, tn, out_dtype, name, resid=None, rope=None, guest=None, groups=None):
    M, K = a.shape
    N = b.shape[1]
    assert M % tm == 0 and N % tn == 0 and b.shape[0] == K
    n_in = 2 + (resid is not None) + (3 if rope is not None else 0)
    if groups is not None:
        assert rope is not None and N == 3 * tn * len(groups)

    def body(*refs):
        a_ref, b_ref = refs[0], refs[1]
        o_ref = refs[n_in]
        acc = jnp.dot(a_ref[...], b_ref[...], preferred_element_type=F32)
        if resid is not None:
            acc = acc + refs[2][...]
        if groups is not None:
            cos_ref, sa_ref, sb_ref = refs[n_in - 3:n_in]
            j = pl.program_id(1)
            for g, d in enumerate(groups):
                for is_v in (False, True):
                    @pl.when(jnp.logical_and(j // 3 == g, (j % 3 == 2) == is_v))
                    def _(g=g, d=d, is_v=is_v):
                        val = acc if is_v else _rope_rotate(acc, cos_ref[...], sa_ref[...], sb_ref[...])
                        if d == 1:
                            refs[n_in + g][...] = val.astype(out_dtype)
                        else:
                            _to_view(val, refs[-1], refs[n_in + g], d, tn)
        elif rope is not None:
            cos_ref, sa_ref, sb_ref = refs[n_in - 3:n_in]
            j = pl.program_id(1)

            @pl.when(j % 3 != 2)
            def _():
                o_ref[...] = _rope_rotate(acc, cos_ref[...], sa_ref[...], sb_ref[...]).astype(out_dtype)

            @pl.when(j % 3 == 2)
            def _():
                o_ref[...] = acc.astype(out_dtype)
        else:
            o_ref[...] = acc.astype(out_dtype)

    in_specs = [pl.BlockSpec((tm, K), lambda i, j: (i, 0)), pl.BlockSpec((K, tn), lambda i, j: (0, j))]
    args = [a, b]
    if resid is not None:
        in_specs.append(pl.BlockSpec((tm, tn), lambda i, j: (i, j)))
        args.append(resid)
    if rope is not None:
        assert tn == 1024
        for t in rope:
            in_specs.append(pl.BlockSpec((tm, 128), lambda i, j: (i, 0)))
            args.append(t)
    if groups is None:
        out_specs = [pl.BlockSpec((tm, tn), lambda i, j: (i, j))]
        out_shape = [jax.ShapeDtypeStruct((M, N), out_dtype)]
        scratch = []
    else:
        out_specs = [pl.BlockSpec((tm // d, d * tn), lambda i, j, g=g: (i, jnp.clip(j - 3 * g, 0, 2)))
                     for g, d in enumerate(groups)]
        out_shape = [jax.ShapeDtypeStruct((M // d, d * 3 * tn), out_dtype) for d in groups]
        scratch = [pltpu.VMEM((tn // 128, tm, 128), F32)]
    outs = _hosted_call(body, grid=(M // tm, N // tn), in_specs=in_specs, out_specs=out_specs, out_shape=out_shape,
                        scratch_shapes=scratch, args=args, name=name, guest=guest)
    nout = len(out_shape)
    res = outs[0] if groups is None else list(outs[:nout])
    return res if guest is None else (res, outs[nout:])


def _mm_nt(a, b, *, tm, to, tn, out_dtype, name, add=None, guest=None, views=(1,)):
    M, N = a.shape
    O = b.shape[0]
    assert M % tm == 0 and O % to == 0 and N % tn == 0 and b.shape[1] == N
    nk = N // tn

    def body(*refs):
        a_ref, b_ref = refs[0], refs[1]
        n_in = 2 + (add is not None)
        o_refs = refs[n_in:n_in + len(views)]
        acc_ref = refs[n_in + len(views)]
        k = pl.program_id(2)

        @pl.when(k == 0)
        def _():
            if add is not None:
                acc_ref[...] = refs[2][...]
            else:
                acc_ref[...] = jnp.zeros_like(acc_ref)

        acc_ref[...] += lax.dot_general(a_ref[...], b_ref[...], (((1,), (1,)), ((), ())),
                                        preferred_element_type=F32)

        @pl.when(k == nk - 1)
        def _():
            for o_ref, d in zip(o_refs, views):
                if d == 1:
                    o_ref[...] = acc_ref[...].astype(out_dtype)
                else:
                    _to_view(acc_ref[...], refs[-1], o_ref, d, to)

    in_specs = [pl.BlockSpec((tm, tn), lambda i, j, k: (i, k)), pl.BlockSpec((to, tn), lambda i, j, k: (j, k))]
    args = [a, b]
    if add is not None:
        in_specs.append(pl.BlockSpec((tm, to), lambda i, j, k: (i, j)))
        args.append(add)
    assert views == (1,) or (to == O and to % 128 == 0)
    scratch = [pltpu.VMEM((tm, to), F32)] + ([pltpu.VMEM((to // 128, tm, 128), F32)] if views != (1,) else [])
    outs = _hosted_call(
        body, grid=(M // tm, O // to, nk), in_specs=in_specs,
        out_specs=[pl.BlockSpec((tm, to), lambda i, j, k: (i, j)) if d == 1 else _view_spec(tm, d, to) for d in views],
        out_shape=[jax.ShapeDtypeStruct((M // d, d * O), out_dtype) for d in views],
        scratch_shapes=scratch, args=args, name=name, guest=guest)
    nv = len(views)
    res = outs[0] if nv == 1 else list(outs[:nv])
    return res if guest is None else (res, outs[nv:])


def _mm_tn(a, b, *, tk, tn, tm, out_dtype, name):
    M, K = a.shape
    N = b.shape[1]
    assert M % tm == 0 and K % tk == 0 and N % tn == 0 and b.shape[0] == M
    nm = M // tm

    def body(a_ref, b_ref, o_ref, acc_ref):
        m = pl.program_id(2)

        @pl.when(m == 0)
        def _():
            acc_ref[...] = jnp.zeros_like(acc_ref)

        acc_ref[...] += lax.dot_general(a_ref[...], b_ref[...], (((0,), (0,)), ((), ())),
                                        preferred_element_type=F32)

        @pl.when(m == nm - 1)
        def _():
            o_ref[...] = acc_ref[...].astype(out_dtype)

    return pl.pallas_call(
        body, grid=(K // tk, N // tn, nm),
        in_specs=[pl.BlockSpec((tm, tk), lambda i, j, m: (m, i)), pl.BlockSpec((tm, tn), lambda i, j, m: (m, j))],
        out_specs=pl.BlockSpec((tk, tn), lambda i, j, m: (i, j)),
        out_shape=jax.ShapeDtypeStruct((K, N), out_dtype),
        scratch_shapes=[pltpu.VMEM((tk, tn), F32)], name=name,
        compiler_params=_params("parallel", "parallel", "arbitrary"),
    )(a, b)


ROW_TILE = 512
MM_ROWS = 1024


def _rms_fwd(x, g, *, name):
    S, Dm = x.shape

    def body(x_ref, g_ref, o_ref):
        xf = x_ref[...]
        r = lax.rsqrt(jnp.mean(xf * xf, axis=-1, keepdims=True) + RMS_EPS)
        o_ref[...] = (xf * r * g_ref[...]).astype(BF)

    return pl.pallas_call(
        body, grid=(S // ROW_TILE,),
        in_specs=[pl.BlockSpec((ROW_TILE, Dm), lambda i: (i, 0)), pl.BlockSpec((1, Dm), lambda i: (0, 0))],
        out_specs=pl.BlockSpec((ROW_TILE, Dm), lambda i: (i, 0)),
        out_shape=jax.ShapeDtypeStruct((S, Dm), BF), name=name, compiler_params=_params("parallel"),
    )(x, g)


def _rms_bwd(x, g, dn, dres, *, name):
    S, Dm = x.shape

    def body(x_ref, g_ref, dn_ref, dres_ref, dx_ref, dxb_ref, dg_ref):
        i = pl.program_id(0)
        xf = x_ref[...]
        r = lax.rsqrt(jnp.mean(xf * xf, axis=-1, keepdims=True) + RMS_EPS)
        xh = xf * r
        dnf = dn_ref[...]
        dyg = dnf * g_ref[...]
        dx = dres_ref[...] + r * (dyg - xh * jnp.mean(dyg * xh, axis=-1, keepdims=True))
        dx_ref[...] = dx
        dxb_ref[...] = dx.astype(BF)

        @pl.when(i == 0)
        def _():
            dg_ref[...] = jnp.zeros_like(dg_ref)

        dg_ref[...] += jnp.sum(dnf * xh, axis=0, keepdims=True)

    row = pl.BlockSpec((ROW_TILE, Dm), lambda i: (i, 0))
    vec = pl.BlockSpec((1, Dm), lambda i: (0, 0))
    return pl.pallas_call(
        body, grid=(S // ROW_TILE,), in_specs=[row, vec, row, row], out_specs=[row, row, vec],
        out_shape=[jax.ShapeDtypeStruct((S, Dm), F32), jax.ShapeDtypeStruct((S, Dm), BF),
                   jax.ShapeDtypeStruct((1, Dm), F32)],
        name=name, compiler_params=_params("arbitrary"),
    )(x, g, dn, dres)


def _loss_head(h, g, tgt, *, name):
    S, Dm = h.shape

    def body(h_ref, g_ref, t_ref, loss_ref, dh_ref, dhb_ref, dg_ref):
        i = pl.program_id(0)
        xf = h_ref[...]
        r = lax.rsqrt(jnp.mean(xf * xf, axis=-1, keepdims=True) + RMS_EPS)
        xh = xf * r
        gv = g_ref[...]
        err = xh * gv - t_ref[...]
        dy = err * (1.0 / Dm)
        dyg = dy * gv
        dh = r * (dyg - xh * jnp.mean(dyg * xh, axis=-1, keepdims=True))
        dh_ref[...] = dh
        dhb_ref[...] = dh.astype(BF)

        @pl.when(i == 0)
        def _():
            dg_ref[...] = jnp.zeros_like(dg_ref)
            loss_ref[...] = jnp.zeros_like(loss_ref)

        dg_ref[...] += jnp.sum(dy * xh, axis=0, keepdims=True)
        part = 0.5 * jnp.sum(jnp.mean(err * err, axis=-1, keepdims=True), axis=0, keepdims=True)
        loss_ref[...] += jnp.broadcast_to(part, loss_ref.shape)

    row = pl.BlockSpec((ROW_TILE, Dm), lambda i: (i, 0))
    vec = pl.BlockSpec((1, Dm), lambda i: (0, 0))
    return pl.pallas_call(
        body, grid=(S // ROW_TILE,), in_specs=[row, vec, row],
        out_specs=[pl.BlockSpec((1, 128), lambda i: (0, 0)), row, row, vec],
        out_shape=[jax.ShapeDtypeStruct((1, 128), F32), jax.ShapeDtypeStruct((S, Dm), F32),
                   jax.ShapeDtypeStruct((S, Dm), BF), jax.ShapeDtypeStruct((1, Dm), F32)],
        name=name, compiler_params=_params("arbitrary"),
    )(h, g, tgt)


SWIGLU_ROWS = 512


def _swiglu_fwd(gu, *, name):
    S = gu.shape[0]

    def body(g_ref, u_ref, o_ref):
        g = g_ref[...].astype(F32)
        sig = 1.0 / (1.0 + jnp.exp(-g))
        o_ref[...] = (g * sig * u_ref[...].astype(F32)).astype(BF)

    return pl.pallas_call(
        body, grid=(S // SWIGLU_ROWS,),
        in_specs=[pl.BlockSpec((SWIGLU_ROWS, D_FF), lambda i: (i, 0)), pl.BlockSpec((SWIGLU_ROWS, D_FF), lambda i: (i, 1))],
        out_specs=pl.BlockSpec((SWIGLU_ROWS, D_FF), lambda i: (i, 0)),
        out_shape=jax.ShapeDtypeStruct((S, D_FF), BF), name=name, compiler_params=_params("parallel"),
    )(gu, gu)


def _swiglu_bwd(gu, dact, *, name):
    S = gu.shape[0]

    def body(g_ref, u_ref, d_ref, o_ref):
        g = g_ref[...].astype(F32)
        u = u_ref[...].astype(F32)
        d = d_ref[...].astype(F32)
        sig = 1.0 / (1.0 + jnp.exp(-g))
        o_ref[:, :D_FF] = (d * u * sig * (1.0 + g * (1.0 - sig))).astype(BF)
        o_ref[:, D_FF:] = (d * g * sig).astype(BF)

    return pl.pallas_call(
        body, grid=(S // SWIGLU_ROWS,),
        in_specs=[pl.BlockSpec((SWIGLU_ROWS, D_FF), lambda i: (i, 0)), pl.BlockSpec((SWIGLU_ROWS, D_FF), lambda i: (i, 1)),
                  pl.BlockSpec((SWIGLU_ROWS, D_FF), lambda i: (i, 0))],
        out_specs=pl.BlockSpec((SWIGLU_ROWS, 2 * D_FF), lambda i: (i, 0)),
        out_shape=jax.ShapeDtypeStruct((S, 2 * D_FF), BF), name=name, compiler_params=_params("parallel"),
    )(gu, gu, dact)


def _band_masks(T, n):
    row = lax.broadcasted_iota(jnp.int32, (T, T), 0)
    col = lax.broadcasted_iota(jnp.int32, (T, T), 1)
    return jnp.logical_and(col >= row, n > 0), col <= row


def _band_fwd(qa, ka, va, qcb, kcb, vcb, *, dil, T, window, name, guest=None):
    L = qa.shape[0]
    nq = L // T
    assert window == T
    nt = (((1,), (1,)), ((), ()))

    def body(q_ref, kp_ref, kc_ref, vp_ref, vc_ref, o_ref, lse_ref):
        valid_prev, valid_cur = _band_masks(T, pl.program_id(1))
        lane = lax.broadcasted_iota(jnp.int32, (T, 128), 1)
        low = lane < HEAD_DIM
        ones = jnp.ones((T, 128), BF)
        lse = jnp.zeros((T, 128), F32)
        def scores(h):
            ps = slice((h // 2) * 128, (h // 2 + 1) * 128)
            qp = q_ref[:, ps] * jnp.asarray(ATTN_SCALE, BF)
            qm = jnp.where(low if h % 2 == 0 else jnp.logical_not(low), qp, jnp.zeros_like(qp))
            s0 = jnp.where(valid_prev, lax.dot_general(qm, kp_ref[:, ps], nt, preferred_element_type=F32), NEG_INF)
            s1 = jnp.where(valid_cur, lax.dot_general(qm, kc_ref[:, ps], nt, preferred_element_type=F32), NEG_INF)
            return s0, s1

        def softmax(s0, s1):
            m = jnp.maximum(jnp.max(s0, axis=1, keepdims=True), jnp.max(s1, axis=1, keepdims=True))
            return m, jnp.exp(s0 - m).astype(BF), jnp.exp(s1 - m).astype(BF)

        def weighted(h, p0, p1):
            ps = slice((h // 2) * 128, (h // 2 + 1) * 128)
            l = jnp.dot(p0, ones, preferred_element_type=F32) + jnp.dot(p1, ones, preferred_element_type=F32)
            acc = jnp.dot(p0, vp_ref[:, ps], preferred_element_type=F32) + jnp.dot(p1, vc_ref[:, ps], preferred_element_type=F32)
            return l, acc

        sc, pr, even = {}, {}, None
        for t in range(N_HEADS + 2):
            if t < N_HEADS:
                sc[t] = scores(t)
            done = None
            if t >= 2:
                m, p0, p1 = pr.pop(t - 2)
                done = (m,) + weighted(t - 2, p0, p1)
            if 1 <= t <= N_HEADS:
                pr[t - 1] = softmax(*sc.pop(t - 1))
            if done is not None:
                h = t - 2
                m, l, acc = done
                lse = jnp.where(lane == h, m + jnp.log(l), lse)
                if h % 2 == 0:
                    even = acc / l
                else:
                    o_ref[:, (h // 2) * 128:(h // 2 + 1) * 128] = jnp.where(low, even, acc / l)
        lse_ref[...] = lse

    def prev(n):
        return jnp.maximum(n - 1, 0)

    blk = lambda f, cb: pl.BlockSpec((T, 1024), lambda r, n: (f(n), cb(r)))
    same = lambda n: n
    outs = _hosted_call(
        body, grid=(dil, nq),
        in_specs=[blk(same, qcb), blk(prev, kcb), blk(same, kcb), blk(prev, vcb), blk(same, vcb)],
        out_specs=[pl.BlockSpec((T, 1024), lambda r, n: (n, r)), pl.BlockSpec((T, 128), lambda r, n: (n, r))],
        out_shape=[jax.ShapeDtypeStruct((L, dil * 1024), F32), jax.ShapeDtypeStruct((L, dil * 128), F32)],
        scratch_shapes=[], args=(qa, ka, ka, va, va), name=name, guest=guest)
    return outs if guest is None else (outs[:2], outs[2:])


def _band_bwd(qa, ka, va, qcb, kcb, vcb, doa, oa, lsea, *, dil, T, window, name, guest=None):
    L = qa.shape[0]
    nq = L // T
    assert window == T
    nt = (((1,), (1,)), ((), ()))
    tn = (((0,), (0,)), ((), ()))

    def body(q_ref, kp_ref, kc_ref, vp_ref, vc_ref, do_ref, o_ref, lse_ref, dq_ref, dk_ref, dv_ref, ck_sc, cv_sc):
        n = pl.program_id(1)

        @pl.when(n == 0)
        def _():
            ck_sc[...] = jnp.zeros_like(ck_sc)
            cv_sc[...] = jnp.zeros_like(cv_sc)

        @pl.when(n < nq)
        def _():
            valid_prev, valid_cur = _band_masks(T, n)
            low = lax.broadcasted_iota(jnp.int32, (T, 128), 1) < HEAD_DIM
            dot = functools.partial(lax.dot_general, preferred_element_type=F32)

            def pair(h):
                return slice((h // 2) * 128, (h // 2 + 1) * 128)

            def products(h):
                ps = pair(h)
                mask = low if h % 2 == 0 else jnp.logical_not(low)
                qp = q_ref[:, ps] * jnp.asarray(ATTN_SCALE, BF)
                dop = do_ref[:, ps]
                qm = jnp.where(mask, qp, jnp.zeros_like(qp))
                dom = jnp.where(mask, dop, jnp.zeros_like(dop))
                s0 = jnp.where(valid_prev, dot(qm, kp_ref[:, ps], nt), NEG_INF)
                s1 = jnp.where(valid_cur, dot(qm, kc_ref[:, ps], nt), NEG_INF)
                return qm, dom, s0, s1, dot(dom, vp_ref[:, ps], nt), dot(dom, vc_ref[:, ps], nt)

            def pointwise(h, qm, dom, s0, s1, dp0, dp1):
                ps = pair(h)
                mask = low if h % 2 == 0 else jnp.logical_not(low)
                prod = do_ref[:, ps].astype(F32) * o_ref[:, ps].astype(F32)
                delta = jnp.sum(jnp.where(mask, prod, 0.0), axis=1, keepdims=True)
                lse = lse_ref[:, h:h + 1]
                p0 = jnp.exp(s0 - lse)
                p1 = jnp.exp(s1 - lse)
                ds0 = (p0 * (dp0 - delta)).astype(BF)
                ds1 = (p1 * (dp1 - delta)).astype(BF)
                return qm, dom, p0.astype(BF), p1.astype(BF), ds0, ds1

            def gradients(h, qm, dom, p0, p1, ds0, ds1):
                ps = pair(h)
                dq = dot(ds0, kp_ref[:, ps], (((1,), (0,)), ((), ()))) + dot(ds1, kc_ref[:, ps], (((1,), (0,)), ((), ())))
                return dq, dot(ds0, qm, tn), dot(p0, dom, tn), dot(ds1, qm, tn), dot(p1, dom, tn)

            st1, st2, even = {}, {}, None
            for t in range(N_HEADS + 2):
                if t < N_HEADS:
                    st1[t] = products(t)
                done = gradients(t - 2, *st2.pop(t - 2)) if t >= 2 else None
                if 1 <= t <= N_HEADS:
                    st2[t - 1] = pointwise(t - 1, *st1.pop(t - 1))
                if done is not None:
                    h = t - 2
                    if h % 2 == 0:
                        even = done
                    else:
                        ps = pair(h)
                        dq_ref[:, ps] = (jnp.where(low, even[0], done[0]) * ATTN_SCALE).astype(BF)
                        dk_ref[:, ps] = (ck_sc[:, ps] + even[1] + done[1]).astype(BF)
                        dv_ref[:, ps] = (cv_sc[:, ps] + even[2] + done[2]).astype(BF)
                        ck_sc[:, ps] = even[3] + done[3]
                        cv_sc[:, ps] = even[4] + done[4]

        @pl.when(n == nq)
        def _():
            dk_ref[...] = ck_sc[...].astype(BF)
            dv_ref[...] = cv_sc[...].astype(BF)

    def cur(n):
        return jnp.minimum(n, nq - 1)

    def prev(n):
        return jnp.maximum(cur(n) - 1, 0)

    blk = lambda f, cb: pl.BlockSpec((T, 1024), lambda r, n: (f(n), cb(r)))
    own = lambda r: r
    outs = _hosted_call(
        body, grid=(dil, nq + 1),
        in_specs=[blk(cur, qcb), blk(prev, kcb), blk(cur, kcb), blk(prev, vcb), blk(cur, vcb), blk(cur, own), blk(cur, own),
                  pl.BlockSpec((T, 128), lambda r, n: (cur(n), r))],
        out_specs=[blk(cur, own), blk(lambda n: jnp.maximum(n - 1, 0), own), blk(lambda n: jnp.maximum(n - 1, 0), own)],
        out_shape=[jax.ShapeDtypeStruct((L, dil * 1024), BF)] * 3,
        scratch_shapes=[pltpu.VMEM((T, 1024), F32), pltpu.VMEM((T, 1024), F32)],
        args=(qa, ka, ka, va, va, doa, oa, lsea), name=name, guest=guest)
    return outs if guest is None else (outs[:3], outs[3:])


FOX_T = 256
FOX_TK = 512
FOX_TQ_BWD = 512
FOX_ROWS = 128


def _fox_fwd(qkv, cT, *, name, guest=None):
    S = qkv.shape[0]
    T, TK, R = FOX_T, FOX_TK, FOX_ROWS
    nq = S // T
    nt = (((1,), (1,)), ((), ()))
    chains = [(h, rh) for h in range(N_HEADS) for rh in range(T // R)]
    pairs = [(n, j) for n in range(nq) for j in range((n * T + T - 1) // TK + 1)]
    schedule = [jnp.asarray([p[i] for p in pairs], jnp.int32) for i in range(2)]

    def body(n_tab, j_tab, q_ref, k_ref, v_ref, ct_ref, o_ref, lse_ref, m_sc, l_sc, acc_sc):
        n = n_tab[pl.program_id(0)]
        j = j_tab[pl.program_id(0)]
        last_j = (n * T + T - 1) // TK
        lane = lax.broadcasted_iota(jnp.int32, (R, 128), 1)
        low = lane < HEAD_DIM
        ones = jnp.ones((TK, 128), BF)

        @pl.when(j == 0)
        def _():
            m_sc[...] = jnp.full(m_sc.shape, NEG_INF, F32)
            l_sc[...] = jnp.zeros_like(l_sc)
            acc_sc[...] = jnp.zeros_like(acc_sc)

        def step(diagonal):
            def pair(h):
                return slice((h // 2) * 128, (h // 2 + 1) * 128)

            def rows(rh):
                return slice(rh * R, (rh + 1) * R)

            def scores(h, rh):
                qp = q_ref[rows(rh), pair(h)] * jnp.asarray(ATTN_SCALE, BF)
                qm = jnp.where(low if h % 2 == 0 else jnp.logical_not(low), qp, jnp.zeros_like(qp))
                s = lax.dot_general(qm, k_ref[:, pair(h)], nt, preferred_element_type=F32) - ct_ref[h:h + 1, :]
                if diagonal:
                    ahead = lax.broadcasted_iota(jnp.int32, (R, TK), 1) - lax.broadcasted_iota(jnp.int32, (R, TK), 0)
                    s = jnp.where(ahead <= n * T + rh * R - j * TK, s, NEG_INF)
                return s

            def softmax(h, rh, s):
                m_prev = m_sc[h, rows(rh), :]
                m_new = jnp.maximum(m_prev, jnp.max(s, axis=1, keepdims=True))
                p = jnp.exp(s - jnp.concatenate([m_new] * (TK // 128), axis=1)).astype(BF)
                return m_new, jnp.exp(m_prev - m_new), p

            def weighted(h, p):
                vx = jnp.concatenate([v_ref[:, pair(h)], ones], axis=1)
                return jnp.dot(p, vx, preferred_element_type=F32)

            sc, pr, even = {}, {}, {}
            nch = len(chains)
            for t in range(nch + 2):
                if t < nch:
                    sc[t] = scores(*chains[t])
                done = None
                if t >= 2:
                    m_new, alpha, p = pr.pop(t - 2)
                    done = (m_new, alpha, weighted(chains[t - 2][0], p))
                if 1 <= t <= nch:
                    pr[t - 1] = softmax(*chains[t - 1], sc.pop(t - 1))
                if done is not None:
                    h, rh = chains[t - 2]
                    m_new, alpha, pv = done
                    m_sc[h, rows(rh), :] = m_new
                    l_sc[h, rows(rh), :] = alpha * l_sc[h, rows(rh), :] + pv[:, 128:]
                    if h % 2 == 0:
                        even[rh] = (alpha, pv[:, :128])
                    else:
                        a0, pv0 = even.pop(rh)
                        acc = acc_sc[h // 2, rows(rh), :]
                        acc_sc[h // 2, rows(rh), :] = jnp.where(low, a0 * acc + pv0, alpha * acc + pv[:, :128])

        @pl.when(j < last_j)
        def _():
            step(False)

        @pl.when(j == last_j)
        def _():
            step(True)
            lane_t = lax.broadcasted_iota(jnp.int32, (T, 128), 1)
            low_t = lane_t < HEAD_DIM
            lse = jnp.zeros((T, 128), F32)
            for h in range(N_HEADS):
                lse = jnp.where(lane_t == h, m_sc[h] + jnp.log(l_sc[h]), lse)
            lse_ref[...] = lse
            for hp in range(N_HEADS // 2):
                inv = jnp.where(low_t, 1.0 / l_sc[2 * hp], 1.0 / l_sc[2 * hp + 1])
                o_ref[:, hp * 128:(hp + 1) * 128] = (acc_sc[hp] * inv).astype(BF)

    outs = _hosted_call(
        body, grid=(len(pairs),),
        in_specs=[pl.BlockSpec((T, 1024), lambda t, n, j: (n[t], 0)), pl.BlockSpec((TK, 1024), lambda t, n, j: (j[t], 1)),
                  pl.BlockSpec((TK, 1024), lambda t, n, j: (j[t], 2)), pl.BlockSpec((GATE_LANES, TK), lambda t, n, j: (0, j[t]))],
        out_specs=[pl.BlockSpec((T, 1024), lambda t, n, j: (n[t], 0)), pl.BlockSpec((T, 128), lambda t, n, j: (n[t], 0))],
        out_shape=[jax.ShapeDtypeStruct((S, 1024), BF), jax.ShapeDtypeStruct((S, 128), F32)],
        scratch_shapes=[pltpu.VMEM((N_HEADS, T, 128), F32), pltpu.VMEM((N_HEADS, T, 128), F32),
                        pltpu.VMEM((N_HEADS // 2, T, 128), F32)],
        args=(qkv, qkv, qkv, cT), name=name, guest=guest, schedule=schedule)
    return outs if guest is None else (outs[:2], outs[2:])


def _fox_bwd(qkv, cT, do, o, lse, *, name, guest=None):
    S = qkv.shape[0]
    T, TQ, R = FOX_T, FOX_TQ_BWD, FOX_ROWS
    nk, nq = S // T, S // TQ
    nt = (((1,), (1,)), ((), ()))
    tn = (((0,), (0,)), ((), ()))
    nn = (((1,), (0,)), ((), ()))
    chains = [(h, rh) for h in range(N_HEADS) for rh in range(TQ // R)]
    dot = functools.partial(lax.dot_general, preferred_element_type=F32)
    pairs = [(kb, qb) for kb in range(nk) for qb in range(kb * T // TQ, nq)]
    schedule = [jnp.asarray([p[i] for p in pairs], jnp.int32) for i in range(2)]

    def body(kb_tab, qb_tab, q_ref, k_ref, v_ref, ct_ref, do_ref, o_ref, lse_ref, dq_ref, dk_ref, dv_ref, dct_ref, dcq_ref,
             dq_sc, dk_sc, dv_sc, dc_sc, dcq_sc):
        kb = kb_tab[pl.program_id(0)]
        qb = qb_tab[pl.program_id(0)]
        jq = qb - kb * T // TQ
        lane = lax.broadcasted_iota(jnp.int32, (R, 128), 1)
        low = lane < HEAD_DIM
        ones_k = jnp.ones((T, 128), BF)
        ones_r = jnp.ones((8, R), BF)

        @pl.when(jnp.logical_and(kb == 0, jq == 0))
        def _():
            dq_sc[...] = jnp.zeros_like(dq_sc)
            dcq_sc[...] = jnp.zeros_like(dcq_sc)

        @pl.when(jq == 0)
        def _():
            dk_sc[...] = jnp.zeros_like(dk_sc)
            dv_sc[...] = jnp.zeros_like(dv_sc)
            dc_sc[...] = jnp.zeros_like(dc_sc)

        def step(diagonal):
            def pair(h):
                return slice((h // 2) * 128, (h // 2 + 1) * 128)

            def rows(rh):
                return slice(rh * R, (rh + 1) * R)

            def qrows(rh):
                return pl.ds(pl.multiple_of(qb * TQ + rh * R, R), R)

            def products(h, rh):
                mask = low if h % 2 == 0 else jnp.logical_not(low)
                qp = q_ref[rows(rh), pair(h)] * jnp.asarray(ATTN_SCALE, BF)
                dop = do_ref[rows(rh), pair(h)]
                qm = jnp.where(mask, qp, jnp.zeros_like(qp))
                dom = jnp.where(mask, dop, jnp.zeros_like(dop))
                s = dot(qm, k_ref[:, pair(h)], nt) - ct_ref[h:h + 1, :]
                if diagonal:
                    ahead = lax.broadcasted_iota(jnp.int32, (R, T), 1) - lax.broadcasted_iota(jnp.int32, (R, T), 0)
                    s = jnp.where(ahead <= qb * TQ + rh * R - kb * T, s, NEG_INF)
                return qm, dom, s, dot(dom, v_ref[:, pair(h)], nt)

            def pointwise(h, rh, qm, dom, s, dp):
                mask = low if h % 2 == 0 else jnp.logical_not(low)
                prod = do_ref[rows(rh), pair(h)].astype(F32) * o_ref[rows(rh), pair(h)].astype(F32)
                delta = jnp.sum(jnp.where(mask, prod, 0.0), axis=1, keepdims=True)
                p = jnp.exp(s - lse_ref[rows(rh), h:h + 1])
                ds = (p * (dp - delta)).astype(BF)
                return qm, dom, p.astype(BF), ds

            def gradients(h, qm, dom, p, ds):
                kx = jnp.concatenate([k_ref[:, pair(h)], ones_k], axis=1)
                return dot(ds, kx, nn), dot(qm, ds, tn), dot(dom, p, tn), dot(ones_r, ds, nn)

            st1, st2, even = {}, {}, {}
            dcq_tiles = [jnp.zeros((R, 128), F32) for _ in range(TQ // R)]
            nch = len(chains)
            for t in range(nch + 2):
                if t < nch:
                    st1[t] = products(*chains[t])
                done = gradients(chains[t - 2][0], *st2.pop(t - 2)) if t >= 2 else None
                if 1 <= t <= nch:
                    st2[t - 1] = pointwise(*chains[t - 1], *st1.pop(t - 1))
                if done is not None:
                    h, rh = chains[t - 2]
                    dq_rsum, dk, dv, csum = done
                    dq = dq_rsum[:, :128]
                    dcq_tiles[rh] = jnp.where(lane == h, dq_rsum[:, 128:], dcq_tiles[rh])
                    dc_sc[h:h + 1, :] -= csum[0:1, :]
                    if h % 2 == 0:
                        even[rh] = (dq, dk, dv)
                    else:
                        dq0, dk0, dv0 = even.pop(rh)
                        dq_sc[qrows(rh), pair(h)] += jnp.where(low, dq0, dq) * ATTN_SCALE
                        dk_sc[h // 2] += dk0 + dk
                        dv_sc[h // 2] += dv0 + dv
            for rh in range(TQ // R):
                dcq_sc[qrows(rh), :] += dcq_tiles[rh]

        @pl.when(jq > 0)
        def _():
            step(False)

        @pl.when(jq == 0)
        def _():
            step(True)

        @pl.when(qb == nq - 1)
        def _():
            for hp in range(N_HEADS // 2):
                dk_ref[:, hp * 128:(hp + 1) * 128] = dk_sc[hp].T.astype(BF)
                dv_ref[:, hp * 128:(hp + 1) * 128] = dv_sc[hp].T.astype(BF)
            dct_ref[...] = dc_sc[...]

        @pl.when(jnp.logical_and(kb == nk - 1, qb == nq - 1))
        def _():
            def put(i, carry):
                r = pl.ds(pl.multiple_of(i * T, T), T)
                dq_ref[r, :] = dq_sc[r, :].astype(BF)
                return carry
            lax.fori_loop(0, nk, put, 0)
            dcq_ref[...] = dcq_sc[...]

    qblk = lambda col: pl.BlockSpec((TQ, 1024), lambda t, kb, qb: (qb[t], col))
    kblk = lambda col: pl.BlockSpec((T, 1024), lambda t, kb, qb: (kb[t], col))
    whole = pl.BlockSpec((S, 1024), lambda t, kb, qb: (0, 0))
    outs = _hosted_call(
        body, grid=(len(pairs),),
        in_specs=[qblk(0), kblk(1), kblk(2), pl.BlockSpec((GATE_LANES, T), lambda t, kb, qb: (0, kb[t])), qblk(0), qblk(0),
                  pl.BlockSpec((TQ, 128), lambda t, kb, qb: (qb[t], 0))],
        out_specs=[whole, kblk(0), kblk(0), pl.BlockSpec((GATE_LANES, T), lambda t, kb, qb: (0, kb[t])),
                   pl.BlockSpec((S, GATE_LANES), lambda t, kb, qb: (0, 0))],
        out_shape=[jax.ShapeDtypeStruct((S, 1024), BF)] * 3 + [jax.ShapeDtypeStruct((GATE_LANES, S), F32),
                                                               jax.ShapeDtypeStruct((S, GATE_LANES), F32)],
        scratch_shapes=[pltpu.VMEM((S, 1024), F32), pltpu.VMEM((N_HEADS // 2, 128, T), F32), pltpu.VMEM((N_HEADS // 2, 128, T), F32),
                        pltpu.VMEM((GATE_LANES, T), F32), pltpu.VMEM((S, GATE_LANES), F32)],
        args=(qkv, qkv, qkv, cT, do, o, lse), name=name, guest=guest, schedule=schedule)
    return outs if guest is None else (outs[:5], outs[5:])


def _to_natural(src_ref, buf, d, width):
    rows = buf.shape[1]
    for r in range(d):
        for ch in range(width // 128):
            lanes = slice(r * width + ch * 128, r * width + (ch + 1) * 128)
            buf.at[ch][pl.ds(r, rows // d, stride=d), :] = src_ref[:, lanes].astype(F32)
    return jnp.concatenate([buf[ch] for ch in range(width // 128)], axis=1)


def _to_view(val, buf, dst_ref, d, width):
    rows = buf.shape[1]
    for ch in range(width // 128):
        buf[ch] = val[:, ch * 128:(ch + 1) * 128]
    for r in range(d):
        for ch in range(width // 128):
            lanes = slice(r * width + ch * 128, r * width + (ch + 1) * 128)
            dst_ref[:, lanes] = buf.at[ch][pl.ds(r, rows // d, stride=d), :].astype(dst_ref.dtype)


def _view_spec(rows, d, width):
    return pl.BlockSpec((rows // d, d * width), lambda i, *_: (i, 0))


def _combine_groups(os, lses, dils, *, name):
    ng = len(os)
    S = os[0].shape[0] * dils[0]
    tm = ROW_TILE
    views = sorted(set(dils))

    def body(*refs):
        o_refs, l_refs = refs[:ng], refs[ng:2 * ng]
        outs = refs[2 * ng:2 * ng + 2 * len(views)]
        wide, narrow = refs[-2], refs[-1]
        ls = [l_refs[g][...] if dils[g] == 1 else _to_natural(l_refs[g], narrow, dils[g], 128) for g in range(ng)]
        m = functools.reduce(jnp.maximum, ls)
        es = [jnp.exp(l - m) for l in ls]
        den = functools.reduce(jnp.add, es)
        ws = [e / den for e in es]
        lse = m + jnp.log(den)
        og = [o_refs[g][...] if dils[g] == 1 else _to_natural(o_refs[g], wide, dils[g], 1024) for g in range(ng)]
        cols = []
        for h in range(N_HEADS):
            hs = slice(h * HEAD_DIM, (h + 1) * HEAD_DIM)
            acc = ws[0][:, h:h + 1] * og[0][:, hs]
            for g in range(1, ng):
                acc = acc + ws[g][:, h:h + 1] * og[g][:, hs]
            cols.append(acc)
        o = jnp.concatenate(cols, axis=1)
        for k, d in enumerate(views):
            if d == 1:
                outs[2 * k][...] = o.astype(BF)
                outs[2 * k + 1][...] = lse
            else:
                _to_view(o, wide, outs[2 * k], d, 1024)
                _to_view(lse, narrow, outs[2 * k + 1], d, 128)

    out_specs, out_shape = [], []
    for d in views:
        out_specs += [_view_spec(tm, d, 1024), _view_spec(tm, d, 128)]
        out_shape += [jax.ShapeDtypeStruct((S // d, d * 1024), BF), jax.ShapeDtypeStruct((S // d, d * 128), F32)]
    res = pl.pallas_call(
        body, grid=(S // tm,), in_specs=[_view_spec(tm, d, 1024) for d in dils] + [_view_spec(tm, d, 128) for d in dils],
        out_specs=out_specs, out_shape=out_shape,
        scratch_shapes=[pltpu.VMEM((8, tm, 128), F32), pltpu.VMEM((1, tm, 128), F32)],
        name=name, compiler_params=_params("parallel"),
    )(*os, *lses)
    return {d: (res[2 * k], res[2 * k + 1]) for k, d in enumerate(views)}


def _assemble(parts, rope_flags, rope, dils, *, name):
    n = len(parts)
    S = parts[0].shape[0] * dils[0]
    use_rope = any(rope_flags)
    tm = 256

    def body(*refs):
        out_ref, natural = refs[-2], refs[-1]
        for b in range(n):
            cols = slice(b * 1024, (b + 1) * 1024)
            d = dils[b]
            val = refs[b][...].astype(F32) if d == 1 else _to_natural(refs[b], natural, d, 1024)
            if rope_flags[b]:
                cos_ref, sa_ref, sb_ref = refs[n:n + 3]
                val = _rope_rotate(val, cos_ref[...], sa_ref[...], sb_ref[...])
            out_ref[:, cols] = val.astype(BF)

    in_specs = [_view_spec(tm, d, 1024) for d in dils]
    args = list(parts)
    if use_rope:
        in_specs += [pl.BlockSpec((tm, 128), lambda i: (i, 0))] * 3
        args += list(rope)
    return pl.pallas_call(
        body, grid=(S // tm,), in_specs=in_specs, out_specs=pl.BlockSpec((tm, n * 1024), lambda i: (i, 0)),
        out_shape=jax.ShapeDtypeStruct((S, n * 1024), BF), scratch_shapes=[pltpu.VMEM((8, tm, 128), F32)],
        name=name, compiler_params=_params("parallel"),
    )(*args)


GATE_ROWS = 512


def _gate_fwd(z, bf, *, name):
    S = z.shape[0]

    def body(z_ref, b_ref, ct_ref, carry):
        i = pl.program_id(0)

        @pl.when(i == 0)
        def _():
            carry[...] = jnp.zeros_like(carry)

        zz = z_ref[...] + b_ref[...]
        logf = jnp.minimum(zz, 0.0) - jnp.log(1.0 + jnp.exp(-jnp.abs(zz)))
        tri = (lax.broadcasted_iota(jnp.int32, (GATE_ROWS, GATE_ROWS), 0)
               >= lax.broadcasted_iota(jnp.int32, (GATE_ROWS, GATE_ROWS), 1)).astype(F32)
        cs = jnp.dot(tri, logf, precision=lax.Precision.HIGHEST, preferred_element_type=F32) + carry[...]
        ct_ref[...] = cs.T
        carry[...] = cs[GATE_ROWS - 1:GATE_ROWS, :]

    return pl.pallas_call(
        body, grid=(S // GATE_ROWS,),
        in_specs=[pl.BlockSpec((GATE_ROWS, GATE_LANES), lambda i: (i, 0)), pl.BlockSpec((1, GATE_LANES), lambda i: (0, 0))],
        out_specs=pl.BlockSpec((GATE_LANES, GATE_ROWS), lambda i: (0, i)),
        out_shape=jax.ShapeDtypeStruct((GATE_LANES, S), F32),
        scratch_shapes=[pltpu.VMEM((1, GATE_LANES), F32)], name=name, compiler_params=_params("arbitrary"),
    )(z, bf)


def _gate_bwd(z, bf, dcT, dcq, *, name):
    S = z.shape[0]
    nb = S // GATE_ROWS

    def body(z_ref, b_ref, dct_ref, dcq_ref, dz_ref, db_ref, carry):
        i = pl.program_id(0)

        @pl.when(i == 0)
        def _():
            carry[...] = jnp.zeros_like(carry)
            db_ref[...] = jnp.zeros_like(db_ref)

        dc = dct_ref[...].T + dcq_ref[...]
        tri = (lax.broadcasted_iota(jnp.int32, (GATE_ROWS, GATE_ROWS), 0)
               <= lax.broadcasted_iota(jnp.int32, (GATE_ROWS, GATE_ROWS), 1)).astype(F32)
        dl = jnp.dot(tri, dc, precision=lax.Precision.HIGHEST, preferred_element_type=F32) + carry[...]
        carry[...] = dl[0:1, :]
        zz = z_ref[...] + b_ref[...]
        dz = dl * (1.0 / (1.0 + jnp.exp(zz)))
        lane = lax.broadcasted_iota(jnp.int32, dz.shape, 1)
        dz = jnp.where(lane < N_HEADS, dz, 0.0)
        dz_ref[...] = dz.astype(BF)
        db_ref[...] += jnp.sum(dz, axis=0, keepdims=True)

    return pl.pallas_call(
        body, grid=(nb,),
        in_specs=[pl.BlockSpec((GATE_ROWS, GATE_LANES), lambda i: (nb - 1 - i, 0)), pl.BlockSpec((1, GATE_LANES), lambda i: (0, 0)),
                  pl.BlockSpec((GATE_LANES, GATE_ROWS), lambda i: (0, nb - 1 - i)),
                  pl.BlockSpec((GATE_ROWS, GATE_LANES), lambda i: (nb - 1 - i, 0))],
        out_specs=[pl.BlockSpec((GATE_ROWS, GATE_LANES), lambda i: (nb - 1 - i, 0)), pl.BlockSpec((1, GATE_LANES), lambda i: (0, 0))],
        out_shape=[jax.ShapeDtypeStruct((S, GATE_LANES), BF), jax.ShapeDtypeStruct((1, GATE_LANES), F32)],
        scratch_shapes=[pltpu.VMEM((1, GATE_LANES), F32)], name=name, compiler_params=_params("arbitrary"),
    )(z, bf, dcT, dcq)


def _rope_tables(S):
    half = ROT_DIM // 2
    inv_freq = ROPE_THETA ** (-jnp.arange(half, dtype=F32) * 2.0 / ROT_DIM)
    ang = jnp.arange(S, dtype=F32)[:, None] * inv_freq[None, :]
    cos, sin = jnp.cos(ang), jnp.sin(ang)
    zero = jnp.zeros((S, HEAD_DIM - ROT_DIM), F32)
    zh = jnp.zeros((S, half), F32)
    cos_h = jnp.concatenate([cos, cos, jnp.ones_like(zero)], axis=1)
    sa_h = jnp.concatenate([-sin, zh, zero], axis=1)
    sb_h = jnp.concatenate([zh, sin, zero], axis=1)
    two = lambda t: jnp.concatenate([t, t], axis=1)
    return two(cos_h), two(sa_h), two(sb_h)


def _ffn_fwd(h, norm, w_gu, w_down, tag):
    n = _rms_fwd(h, norm, name=f"ffn{tag}_norm")
    gu = _mm_nn(n, w_gu, tm=MM_ROWS, tn=1408, out_dtype=BF, name=f"ffn{tag}_gu")
    act = _swiglu_fwd(gu, name=f"ffn{tag}_act")
    out = _mm_nn(act, w_down, tm=MM_ROWS // 2, tn=1024, out_dtype=F32, name=f"ffn{tag}_down", resid=h)
    return out, (h, n, gu, act)


def _ffn_bwd(dh, dhb, saved, norm, w_gu, w_down, tag, ride=None):
    h, n, gu, act = saved
    dact = _mm_nt(dhb, w_down, tm=MM_ROWS, to=1408, tn=1024, out_dtype=BF, name=f"ffn{tag}_dact")
    dw_down = _mm_tn(act, dhb, tk=1408, tn=1024, tm=MM_ROWS, out_dtype=BF, name=f"ffn{tag}_dwdown")
    dgu = _swiglu_bwd(gu, dact, name=f"ffn{tag}_dgu")
    dn_call = lambda guest: _mm_nt(dgu, w_gu, tm=MM_ROWS, to=1024, tn=1408, out_dtype=F32, name=f"ffn{tag}_dn", guest=guest)
    dn = dn_call(None) if ride is None else ride(dn_call)
    dw_gu = _mm_tn(n, dgu, tk=1024, tn=1408, tm=MM_ROWS, out_dtype=BF, name=f"ffn{tag}_dwgu")
    dx, dxb, dg = _rms_bwd(h, norm, dn, dh, name=f"ffn{tag}_dnorm")
    return dx, dxb, dg, dw_gu, dw_down


def _local_step(x, tgt, w, mats, fetch, exchange):
    S = x.shape[0]
    rope_f = _rope_tables(S)
    rope_b = (rope_f[0], -rope_f[1], -rope_f[2])
    g, partial, landed = {}, {}, {}
    w = dict(w, ffn_w_gu={}, ffn_w_down={})

    def bring(call, indices):
        bufs = [mats[wi] for wi in indices]
        if fetch is None:
            return call(None), bufs
        return call(fetch(indices, bufs))

    def ride(call, indices):
        guest = exchange(indices, [partial[wi] for wi in indices]) if indices else None
        res = call(guest)
        if guest is None:
            return res
        res, outs = res
        landed.update(zip(indices, outs))
        return res

    n0 = _rms_fwd(x, w["a_norm"], name="a_norm")
    dils = [d for _, d in DILATED_PATTERNS]
    projs, (w["ffn_w_gu"][0], w["ffn_w_down"][0]) = bring(
        lambda guest: _mm_nn(n0, w["a_w_in"], tm=MM_ROWS, tn=1024, out_dtype=BF, name="a_proj", rope=rope_f, guest=guest,
                             groups=dils), [4, 6])
    block = lambda t, dil: (lambda r: t * dil + r)
    o_parts, lse_parts = [], []
    for gi, (window, dil) in enumerate(DILATED_PATTERNS):
        pv = projs[gi]
        attend = lambda guest: _band_fwd(pv, pv, pv, block(0, dil), block(1, dil), block(2, dil), dil=dil, T=128,
                                         window=window // dil, name=f"a_attn{gi}", guest=guest)
        if gi == 0:
            (o_g, lse_g), (w["a_w_out"],) = bring(attend, [1])
        elif gi == 1:
            (o_g, lse_g), (b_in,) = bring(attend, [2])
        else:
            (o_g, lse_g), (w["b_w_out"],) = bring(attend, [3])
        o_parts.append(o_g)
        lse_parts.append(lse_g)
    b_in = b_in.transpose(1, 0, 2).reshape(D_MODEL, -1)
    w["b_w_qkv"] = b_in[:, :QKV_COLS]
    w["b_w_f"] = jnp.pad(b_in[:, QKV_COLS:], ((0, 0), (0, GATE_LANES + QKV_COLS - b_in.shape[1])))
    mixed = _combine_groups(o_parts, lse_parts, dils, name="a_combine")
    o_a = mixed[1][0]
    h1 = _mm_nn(o_a, w["a_w_out"], tm=MM_ROWS, tn=1024, out_dtype=F32, name="a_out", resid=x)
    h2, ffn0 = _ffn_fwd(h1, w["ffn_norm"][0:1], w["ffn_w_gu"][0], w["ffn_w_down"][0], 0)

    n2 = _rms_fwd(h2, w["b_norm"], name="b_norm")
    qkv = _mm_nn(n2, w["b_w_qkv"], tm=MM_ROWS, tn=1024, out_dtype=BF, name="b_proj")
    zf = _mm_nn(n2, w["b_w_f"], tm=MM_ROWS, tn=GATE_LANES, out_dtype=F32, name="b_gate_proj")
    cT = _gate_fwd(zf, w["b_f"], name="b_gate")
    (o_b, lse_b), (w["ffn_w_gu"][1], w["ffn_w_down"][1]) = bring(lambda guest: _fox_fwd(qkv, cT, name="b_attn", guest=guest), [5, 7])
    h3 = _mm_nn(o_b, w["b_w_out"], tm=MM_ROWS, tn=1024, out_dtype=F32, name="b_out", resid=h2)
    h4, ffn1 = _ffn_fwd(h3, w["ffn_norm"][1:2], w["ffn_w_gu"][1], w["ffn_w_down"][1], 1)

    loss, dh4, dh4b, g["final_norm"] = _loss_head(h4, w["final_norm"], tgt, name="loss_head")

    dh3, dh3b, dg_f1, partial[5], partial[7] = _ffn_bwd(dh4, dh4b, ffn1, w["ffn_norm"][1:2], w["ffn_w_gu"][1], w["ffn_w_down"][1], 1)

    do_b = _mm_nt(dh3b, w["b_w_out"], tm=MM_ROWS, to=1024, tn=1024, out_dtype=BF, name="b_do")
    partial[3] = _mm_tn(o_b, dh3b, tk=1024, tn=1024, tm=MM_ROWS, out_dtype=BF, name="b_dwout")
    dq, dk, dv, dcT, dcq = ride(lambda guest: _fox_bwd(qkv, cT, do_b, o_b, lse_b, name="b_attn_bwd", guest=guest), [5, 7, 3])
    dz, g["b_f"] = _gate_bwd(zf, w["b_f"], dcT, dcq, name="b_gate_bwd")
    dqkv = _assemble([dq, dk, dv], [False] * 3, None, [1] * 3, name="b_dproj")
    dn2 = _mm_nt(dz, w["b_w_f"], tm=MM_ROWS, to=1024, tn=GATE_LANES, out_dtype=F32, name="b_dn_gate")
    dn2 = _mm_nt(dqkv, w["b_w_qkv"], tm=MM_ROWS, to=1024, tn=1024, out_dtype=F32, name="b_dn", add=dn2)
    g_qkv = _mm_tn(n2, dqkv, tk=1024, tn=1024, tm=MM_ROWS, out_dtype=BF, name="b_dwqkv")
    g_f = _mm_tn(n2, dz, tk=1024, tn=GATE_LANES, tm=MM_ROWS, out_dtype=BF, name="b_dwf")
    g_b_in = jnp.concatenate([g_qkv, g_f[:, :N_HEADS]], axis=1)
    partial[2] = g_b_in.reshape(D_MODEL, N_CHIPS, -1).transpose(1, 0, 2)
    dh2, dh2b, g["b_norm"] = _rms_bwd(h2, w["b_norm"], dn2, dh3, name="b_dnorm")

    dh1, dh1b, dg_f0, partial[4], partial[6] = _ffn_bwd(dh2, dh2b, ffn0, w["ffn_norm"][0:1], w["ffn_w_gu"][0], w["ffn_w_down"][0], 0,
                                                      ride=lambda call: ride(call, [2]))
    g["ffn_norm"] = jnp.concatenate([dg_f0, dg_f1], axis=0)

    views = tuple(sorted(set(dils)))
    do_a = dict(zip(views, _mm_nt(dh1b, w["a_w_out"], tm=MM_ROWS, to=1024, tn=1024, out_dtype=BF, name="a_do", views=views)))
    partial[1] = _mm_tn(o_a, dh1b, tk=1024, tn=1024, tm=MM_ROWS, out_dtype=BF, name="a_dwout")
    riders = {0: [4], 1: [6, 1], 2: []}
    parts = []
    for gi, (window, dil) in enumerate(DILATED_PATTERNS):
        pv = projs[gi]
        res = ride(lambda guest: _band_bwd(pv, pv, pv, block(0, dil), block(1, dil), block(2, dil), do_a[dil],
                                           mixed[dil][0], mixed[dil][1], dil=dil, T=128,
                                           window=window // dil, name=f"a_attn_bwd{gi}", guest=guest), riders[gi])
        parts += list(res)
    dproj = _assemble(parts, [True, True, False] * 3, rope_b, [d for _, d in DILATED_PATTERNS for _ in range(3)], name="a_dproj")
    partial[0] = _mm_tn(n0, dproj, tk=1024, tn=1024, tm=MM_ROWS, out_dtype=BF, name="a_dwin")
    dn0 = ride(lambda guest: _mm_nt(dproj, w["a_w_in"], tm=MM_ROWS, to=1024, tn=1024, out_dtype=F32, name="a_dn", guest=guest), [0])
    dx, _, g["a_norm"] = _rms_bwd(x, w["a_norm"], dn0, dh1, name="a_dnorm")
    return loss, dx, g, partial, landed


ANY = pl.BlockSpec(memory_space=pl.ANY)


def _place():
    x, y, c = lax.axis_index("x"), lax.axis_index("y"), lax.axis_index("c")
    chips = [(1 - x, y), (x, 1 - y), (1 - x, 1 - y)]
    return x, y, c, chips


def _shard_slice(ref, kind, rows, cols, s, half):
    hr = rows // 2
    if kind == "col":
        return ref.at[pl.ds(half * hr, hr), pl.ds(pl.multiple_of(s * cols, 128), cols)]
    if kind == "row":
        return ref.at[pl.ds(pl.multiple_of(s * rows + half * hr, 16), hr), :]
    return ref.at[s, pl.ds(half * hr, hr), :]


def _whole_shape(kind, rows, cols):
    return {"col": (rows, N_CHIPS * cols), "row": (N_CHIPS * rows, cols), "stack": (N_CHIPS, rows, cols)}[kind]


def _own_block(kind, rows, tr, cols):
    per = rows // tr

    def spec(half_rows):
        off = (lambda p: 0) if half_rows is None else (lambda p: p[1] * (half_rows // tr))
        if kind == "col":
            return pl.BlockSpec((tr, cols), lambda i, p: (off(p) + i, p[0]))
        if kind == "row":
            return pl.BlockSpec((tr, cols), lambda i, p: (p[0] * per + off(p) + i, 0))
        return pl.BlockSpec((None, tr, cols), lambda i, p: (p[0], off(p) + i, 0))
    return spec


def _place_shard(shards, layer, kind, place, *, name):
    _, rows, cols = shards.shape
    tr = 256 if rows % 256 == 0 else rows // 2

    def body(p_ref, s_ref, o_ref):
        o_ref[...] = s_ref[...].astype(BF)

    return pl.pallas_call(
        body,
        grid_spec=pltpu.PrefetchScalarGridSpec(
            num_scalar_prefetch=1, grid=(rows // tr,),
            in_specs=[pl.BlockSpec((None, tr, cols), lambda i, p: (layer, i, 0))],
            out_specs=_own_block(kind, rows, tr, cols)(None)),
        out_shape=jax.ShapeDtypeStruct(_whole_shape(kind, rows, cols), BF),
        name=name, compiler_params=_params("arbitrary"),
    )(place, shards)


def _gather_weights(placed, kinds, dims):
    nw = len(placed)

    def body(*refs):
        dst = refs[nw:2 * nw]
        send_sems, recv_sems = refs[2 * nw:]
        x, y, c, chips = _place()
        me = 2 * x + y
        sibling = (x, y, 1 - c)

        def copy(wi, k, s, half, to):
            p = _shard_slice(dst[wi], kinds[wi], dims[wi][0], dims[wi][1], s, half)
            return pltpu.make_async_remote_copy(src_ref=p, dst_ref=p, send_sem=send_sems.at[wi * 6 + k],
                                                recv_sem=recv_sems.at[wi * 6 + k], device_id=to, device_id_type=MESH)

        first, passed = [], []
        for wi in range(nw):
            for j, chip in enumerate(chips):
                cp = copy(wi, j, me, c, (*chip, c))
                cp.start()
                first.append(cp)
        for wi in range(nw):
            for j, chip in enumerate(chips):
                s = 2 * chip[0] + chip[1]
                copy(wi, j, s, c, (x, y, c)).wait_recv()
                cp = copy(wi, 3 + j, s, c, sibling)
                cp.start()
                passed.append(cp)
        for wi in range(nw):
            for j, chip in enumerate(chips):
                s = 2 * chip[0] + chip[1]
                copy(wi, 3 + j, s, 1 - c, (x, y, c)).wait_recv()
        for cp in first + passed:
            cp.wait_send()

    return pl.pallas_call(
        body, in_specs=[ANY] * nw, out_specs=[ANY] * nw,
        out_shape=[jax.ShapeDtypeStruct(p.shape, p.dtype) for p in placed],
        input_output_aliases={wi: wi for wi in range(nw)},
        scratch_shapes=[pltpu.SemaphoreType.DMA((nw * 6,)), pltpu.SemaphoreType.DMA((nw * 6,))],
        name="gather_weights",
    )(*placed)


def _fetch_guest(placed, kinds, dims):
    nw = len(placed)

    def copies(dst, send_sems, recv_sems, incoming):
        x, y, c, chips = _place()
        out = []
        for wi in range(nw):
            for j, chip in enumerate(chips):
                s = 2 * chip[0] + chip[1] if incoming else 2 * x + y
                to = (x, y, c) if incoming else (*chip, c)
                for half in range(2):
                    p = _shard_slice(dst[wi], kinds[wi], dims[wi][0], dims[wi][1], s, half)
                    k = wi * 6 + 2 * j + half
                    out.append(pltpu.make_async_remote_copy(src_ref=p, dst_ref=p, send_sem=send_sems.at[k],
                                                            recv_sem=recv_sems.at[k], device_id=to, device_id_type=MESH))
        return out

    def start(src, dst, sems):
        for cp in copies(dst, sems[0], sems[1], False):
            cp.start()

    def finish(src, dst, sems):
        for cp in copies(dst, sems[0], sems[1], True):
            cp.wait_recv()
        for cp in copies(dst, sems[0], sems[1], False):
            cp.wait_send()

    return dict(args=list(placed), out_shape=[jax.ShapeDtypeStruct(p.shape, p.dtype) for p in placed],
                scratch=[pltpu.SemaphoreType.DMA((nw * 6,)), pltpu.SemaphoreType.DMA((nw * 6,))],
                start=start, finish=finish, in_place=True)


def _scatter_guest(partials, kinds, dims):
    nw = len(partials)

    def copies(src, send_sems, recv_sems, dst):
        x, y, c, chips = _place()
        me = 2 * x + y
        out = []
        for wi in range(nw):
            rows, cols = dims[wi]

            def part(s, half, wi=wi, rows=rows, cols=cols):
                return _shard_slice(src[wi], kinds[wi], rows, cols, s, half)

            for j, chip in enumerate(chips):
                s = 2 * chip[0] + chip[1]
                for half in range(2):
                    slot = 2 * j + (c if half == 0 else 1 - c)
                    out.append(pltpu.make_async_remote_copy(
                        src_ref=part(s, half), dst_ref=dst[wi].at[slot],
                        send_sem=send_sems.at[wi * 7 + 2 * j + half], recv_sem=recv_sems.at[wi * 7 + slot],
                        device_id=(*chip, half), device_id_type=MESH))
            out.append(pltpu.make_async_remote_copy(
                src_ref=part(me, 1 - c), dst_ref=dst[wi].at[6],
                send_sem=send_sems.at[wi * 7 + 6], recv_sem=recv_sems.at[wi * 7 + 6],
                device_id=(x, y, 1 - c), device_id_type=MESH))
        return out

    def start(src, dst, sems):
        for cp in copies(src, sems[0], sems[1], dst):
            cp.start()

    def finish(src, dst, sems):
        x, y, c, _ = _place()
        for wi in range(nw):
            for slot in range(7):
                pltpu.make_async_remote_copy(
                    src_ref=dst[wi].at[slot], dst_ref=dst[wi].at[slot],
                    send_sem=sems[0].at[wi * 7 + slot], recv_sem=sems[1].at[wi * 7 + slot],
                    device_id=(x, y, c), device_id_type=MESH).wait_recv()
        for cp in copies(src, sems[0], sems[1], dst):
            cp.wait_send()

    return dict(args=list(partials), out_shape=[jax.ShapeDtypeStruct((7, d[0] // 2, d[1]), BF) for d in dims],
                scratch=[pltpu.SemaphoreType.DMA((nw * 7,)), pltpu.SemaphoreType.DMA((nw * 7,))],
                start=start, finish=finish)


def _sum_slots(slots, partial, kind, dims, place, *, name, into=None, layer=None, n_layers=1):
    rows, cols = dims
    hr = rows // 2
    tr = hr if 8 * hr * cols * 2 <= 6 * 1024 * 1024 else 128
    assert hr % tr == 0

    def body(p_ref, b_ref, own_ref, *rest):
        o_ref = rest[-1]
        acc = own_ref[...].astype(F32)
        for k in range(7):
            acc = acc + b_ref[k].astype(F32)
        o_ref[...] = acc

    half = lambda p: p[1] * (hr // tr)
    if n_layers == 1:
        out_spec = pl.BlockSpec((tr, cols), lambda i, p: (half(p) + i, 0))
        out_shape = jax.ShapeDtypeStruct((rows, cols), F32)
    else:
        out_spec = pl.BlockSpec((None, tr, cols), lambda i, p: (layer, half(p) + i, 0))
        out_shape = jax.ShapeDtypeStruct((n_layers, rows, cols), F32)
    in_specs = [pl.BlockSpec((7, tr, cols), lambda i, p: (0, i, 0)), _own_block(kind, rows, tr, cols)(hr)]
    args = [place, slots, partial]
    aliases = {}
    if into is not None:
        in_specs.append(ANY)
        args.append(into)
        aliases = {3: 0}
    return pl.pallas_call(
        body,
        grid_spec=pltpu.PrefetchScalarGridSpec(num_scalar_prefetch=1, grid=(hr // tr,), in_specs=in_specs, out_specs=out_spec),
        out_shape=out_shape, input_output_aliases=aliases, name=name, compiler_params=_params("arbitrary"),
    )(*args)


def _pair_exchange(bufs, members):
    nw = len(members)

    def body(*refs):
        dst = refs[len(bufs):2 * len(bufs)]
        send_sems, recv_sems = refs[2 * len(bufs):]
        x, y, c, _ = _place()

        def rows_of(wi, half):
            bi, l = members[wi]
            ref = dst[bi] if l is None else dst[bi].at[l]
            hr = ref.shape[0] // 2
            return ref.at[pl.ds(pl.multiple_of(half * hr, 8), hr), :]

        def copy(wi, half, to):
            p = rows_of(wi, half)
            return pltpu.make_async_remote_copy(src_ref=p, dst_ref=p, send_sem=send_sems.at[wi], recv_sem=recv_sems.at[wi],
                                                device_id=to, device_id_type=MESH)

        sent = []
        for wi in range(nw):
            cp = copy(wi, c, (x, y, 1 - c))
            cp.start()
            sent.append(cp)
        for wi in range(nw):
            copy(wi, 1 - c, (x, y, c)).wait_recv()
        for cp in sent:
            cp.wait_send()

    return pl.pallas_call(
        body, in_specs=[ANY] * len(bufs), out_specs=[ANY] * len(bufs),
        out_shape=[jax.ShapeDtypeStruct(b.shape, b.dtype) for b in bufs],
        input_output_aliases={i: i for i in range(len(bufs))},
        scratch_shapes=[pltpu.SemaphoreType.DMA((nw,)), pltpu.SemaphoreType.DMA((nw,))],
        name="pair_exchange",
    )(*bufs)


SMALL_ROWS = 8


def _allreduce_small(v, *, name):
    assert v.shape == (SMALL_ROWS, D_MODEL)

    def body(v_ref, o_ref, buf, send_sems, recv_sems):
        x, y, c, _ = _place()
        me = 4 * x + 2 * y + c
        buf[me] = v_ref[...]
        sent = []
        for k in range(1, 8):
            bx, by, bc = (k >> 2) & 1, (k >> 1) & 1, k & 1
            peer = (1 - x if bx else x, 1 - y if by else y, 1 - c if bc else c)
            cp = pltpu.make_async_remote_copy(src_ref=v_ref, dst_ref=buf.at[me], send_sem=send_sems.at[k - 1],
                                              recv_sem=recv_sems.at[k - 1], device_id=peer, device_id_type=MESH)
            cp.start()
            sent.append(cp)
        for k in range(1, 8):
            bx, by, bc = (k >> 2) & 1, (k >> 1) & 1, k & 1
            peer = 4 * (1 - x if bx else x) + 2 * (1 - y if by else y) + (1 - c if bc else c)
            pltpu.make_async_remote_copy(src_ref=v_ref, dst_ref=buf.at[peer], send_sem=send_sems.at[k - 1],
                                         recv_sem=recv_sems.at[k - 1], device_id=(x, y, c), device_id_type=MESH).wait_recv()
        for cp in sent:
            cp.wait_send()
        acc = buf[0]
        for d in range(1, 8):
            acc = acc + buf[d]
        o_ref[...] = acc

    vmem = pl.BlockSpec(memory_space=pltpu.VMEM)
    return pl.pallas_call(
        body, in_specs=[vmem], out_specs=vmem, out_shape=jax.ShapeDtypeStruct(v.shape, F32),
        scratch_shapes=[pltpu.VMEM((8,) + v.shape, F32), pltpu.SemaphoreType.DMA((7,)), pltpu.SemaphoreType.DMA((7,))],
        name=name,
    )(v)


def _adamw(w, g, m, v, *, name):
    R, C = w.shape
    tr = R
    if R * C * 4 > 1024 * 1024:
        tr = max(t for t in range(8, R, 8) if R % t == 0 and t * C * 4 <= 1024 * 1024)

    def body(w_ref, g_ref, m_ref, v_ref, d_ref, m2_ref, v2_ref):
        gg = g_ref[...]
        m2 = ADAM_B1 * m_ref[...] + (1.0 - ADAM_B1) * gg
        v2 = ADAM_B2 * v_ref[...] + (1.0 - ADAM_B2) * jnp.square(gg)
        m_hat = m2 / (1.0 - ADAM_B1 ** ADAM_STEP)
        v_hat = v2 / (1.0 - ADAM_B2 ** ADAM_STEP)
        d_ref[...] = -ADAM_LR * (m_hat / (jnp.sqrt(v_hat) + ADAM_EPS) + ADAM_WD * w_ref[...])
        m2_ref[...] = m2
        v2_ref[...] = v2

    blk = pl.BlockSpec((tr, C), lambda i: (i, 0))
    out = jax.ShapeDtypeStruct((R, C), F32)
    return pl.pallas_call(
        body, grid=(R // tr,), in_specs=[blk] * 4, out_specs=[blk] * 3, out_shape=[out] * 3,
        name=name, compiler_params=_params("parallel"),
    )(w, g, m, v)


WEIGHT_ORDER = ("a_norm", "a_w_in", "a_w_out", "b_norm", "b_w_in", "b_f", "b_w_out", "ffn_norm", "ffn_w_gu",
                "ffn_w_down", "final_norm")
MATRICES = (("a_w_in", 0, "col"), ("a_w_out", 0, "row"), ("b_w_in", 0, "stack"), ("b_w_out", 0, "row"),
            ("ffn_w_gu", 0, "col"), ("ffn_w_gu", 1, "col"), ("ffn_w_down", 0, "row"), ("ffn_w_down", 1, "row"))
MATRIX_GROUPS = ([0], [1], [2], [3], [4, 5], [6, 7])
GROUP_NAMES = ("a_w_in", "a_w_out", "b_w_in", "b_w_out", "ffn_w_gu", "ffn_w_down")
QKV_COLS = 3 * N_HEADS * HEAD_DIM


def kernel(x, a_norm, a_w_in, a_w_out, b_norm, b_w_in, b_f, b_w_out, ffn_norm, ffn_w_gu, ffn_w_down, final_norm, loss_target, m_a_norm, m_a_w_in, m_a_w_out, m_b_norm, m_b_w_in, m_b_f, m_b_w_out, m_ffn_norm, m_ffn_w_gu, m_ffn_w_down, m_final_norm, v_a_norm, v_a_w_in, v_a_w_out, v_b_norm, v_b_w_in, v_b_f, v_b_w_out, v_ffn_norm, v_ffn_w_gu, v_ffn_w_down, v_final_norm):
    given = dict(a_norm=a_norm, a_w_in=a_w_in, a_w_out=a_w_out, b_norm=b_norm, b_w_in=b_w_in, b_f=b_f, b_w_out=b_w_out,
                 ffn_norm=ffn_norm, ffn_w_gu=ffn_w_gu, ffn_w_down=ffn_w_down, final_norm=final_norm)
    mom_m = dict(a_norm=m_a_norm, a_w_in=m_a_w_in, a_w_out=m_a_w_out, b_norm=m_b_norm, b_w_in=m_b_w_in, b_f=m_b_f,
                 b_w_out=m_b_w_out, ffn_norm=m_ffn_norm, ffn_w_gu=m_ffn_w_gu, ffn_w_down=m_ffn_w_down, final_norm=m_final_norm)
    mom_v = dict(a_norm=v_a_norm, a_w_in=v_a_w_in, a_w_out=v_a_w_out, b_norm=v_b_norm, b_w_in=v_b_w_in, b_f=v_b_f,
                 b_w_out=v_b_w_out, ffn_norm=v_ffn_norm, ffn_w_gu=v_ffn_w_gu, ffn_w_down=v_ffn_w_down, final_norm=v_final_norm)
    chip = 2 * lax.axis_index("x") + lax.axis_index("y")
    core = lax.axis_index("c")
    bn_cols = b_norm.shape[1]

    placed = lax.dynamic_update_slice(jnp.zeros((SMALL_ROWS, D_MODEL), F32), b_norm, (0, chip * bn_cols))
    placed = placed * (core == 0).astype(F32)
    b_norm_full = _allreduce_small(placed, name="gather_b_norm")[0:1]

    place = jnp.stack([chip, core]).astype(jnp.int32)
    kinds = [k for _, _, k in MATRICES]
    dims = [given[n].shape[1:] for n, _, _ in MATRICES]
    placed = [_place_shard(given[n], l, k, place, name=f"place_{n}{l}") for n, l, k in MATRICES]
    first = _gather_weights(placed[:1], kinds[:1], dims[:1])
    mats = dict(enumerate(list(first) + placed[1:]))
    gate_cols = b_f.shape[1]
    w = dict(a_norm=a_norm, a_w_in=mats[0], b_norm=b_norm_full,
             b_f=jnp.pad(b_f, ((0, 0), (0, GATE_LANES - gate_cols))), ffn_norm=ffn_norm,
             final_norm=final_norm.reshape(1, D_MODEL))

    def fetch(indices, bufs):
        return _fetch_guest(bufs, [kinds[i] for i in indices], [dims[i] for i in indices])

    def exchange(indices, parts):
        return _scatter_guest(parts, [kinds[i] for i in indices], [dims[i] for i in indices])

    loss, dx, g, partials, slots = _local_step(x[0], loss_target[0], w, mats, fetch, exchange)
    bufs, members = [], []
    for group in MATRIX_GROUPS:
        buf = None
        for l, wi in enumerate(group):
            n = MATRICES[wi][0]
            buf = _sum_slots(slots[wi], partials[wi], kinds[wi], dims[wi], place, name=f"sum_{n}{l}", into=buf,
                             layer=l, n_layers=len(group))
            members.append((len(bufs), l if len(group) > 1 else None))
        bufs.append(buf)
    reduced = dict(zip(GROUP_NAMES, _pair_exchange(bufs, members)))

    small = jnp.concatenate([g["a_norm"], g["b_norm"], g["ffn_norm"], g["final_norm"],
                             jnp.pad(g["b_f"], ((0, 0), (0, D_MODEL - GATE_LANES))),
                             jnp.zeros((SMALL_ROWS - 6, D_MODEL), F32)], axis=0)
    small = _allreduce_small(small, name="allreduce_small")
    grads = dict(reduced)
    grads["a_norm"] = small[0:1]
    grads["b_norm"] = lax.dynamic_slice(small, (1, chip * bn_cols), (1, bn_cols))
    grads["ffn_norm"] = small[2:4]
    grads["final_norm"] = small[4]
    grads["b_f"] = small[5:6, :gate_cols]

    out_g, out_d, out_m, out_v = [], [], [], []
    for n in WEIGHT_ORDER:
        shape = given[n].shape
        two_d = (1, shape[0]) if len(shape) == 1 else (-1, shape[-1])
        d, m2, v2 = _adamw(given[n].reshape(two_d), grads[n].reshape(two_d), mom_m[n].reshape(two_d),
                           mom_v[n].reshape(two_d), name=f"adamw_{n}")
        out_g.append(grads[n].reshape(shape))
        out_d.append(d.reshape(shape))
        out_m.append(m2.reshape(shape))
        out_v.append(v2.reshape(shape))

    total = lax.psum(loss[0, 0], MESH_AXES)
    return (total, dx[None], *out_g, *out_d, *out_m, *out_v)
```

```python
import functools

import jax
import jax.numpy as jnp
from jax import lax
from jax.experimental import pallas as pl
from jax.experimental.pallas import tpu as pltpu

F32 = jnp.float32
BF = jnp.bfloat16

D_MODEL = 1024
N_HEADS = 16
HEAD_DIM = 64
D_FF = 2816
DILATED_PATTERNS = ((128, 1), (512, 4), (2048, 16))
ROT_DIM = 16
ROPE_THETA = 500000.0
RMS_EPS = 1e-6
NEG_INF = -1e30
ATTN_SCALE = HEAD_DIM ** -0.5
GATE_LANES = 128
N_CHIPS = 4
MESH_AXES = ("x", "y", "c")
MESH = pl.DeviceIdType.MESH

ADAM_LR = 0.001
ADAM_B1 = 0.9
ADAM_B2 = 0.999
ADAM_EPS = 1e-08
ADAM_WD = 0.01
ADAM_STEP = 10

VMEM_LIMIT_BYTES = 56 * 1024 * 1024


def _params(*sem):
    return pltpu.CompilerParams(dimension_semantics=sem, vmem_limit_bytes=VMEM_LIMIT_BYTES)


def _hosted_call(body, *, grid, in_specs, out_specs, out_shape, scratch_shapes, args, name, guest=None, schedule=()):
    params = _params(*(["arbitrary"] * len(grid)))
    ns = len(schedule)

    def call(kernel, in_specs, out_specs, out_shape, scratch_shapes, aliases, args):
        spec = pltpu.PrefetchScalarGridSpec(num_scalar_prefetch=ns, grid=grid, in_specs=in_specs, out_specs=out_specs,
                                            scratch_shapes=scratch_shapes)
        return pl.pallas_call(kernel, grid_spec=spec, out_shape=out_shape, input_output_aliases=aliases, name=name,
                              compiler_params=params)(*schedule, *args)

    if guest is None:
        return call(body, in_specs, out_specs, out_shape, scratch_shapes, {}, args)
    n_in, n_out, n_scr = ns + len(in_specs), len(out_specs), len(scratch_shapes)
    g_in, g_out = len(guest["args"]), len(guest["out_shape"])
    any_spec = pl.BlockSpec(memory_space=pl.ANY)

    def wrapped(*refs):
        i1 = n_in + g_in
        o1 = i1 + n_out
        o2 = o1 + g_out
        s1 = o2 + n_scr
        guest_refs = (refs[n_in:i1], refs[o1:o2], refs[s1:])
        ids = [pl.program_id(d) for d in range(len(grid))]
        first = functools.reduce(jnp.logical_and, [i == 0 for i in ids])
        last = functools.reduce(jnp.logical_and, [i == g - 1 for i, g in zip(ids, grid)])

        @pl.when(first)
        def _():
            guest["start"](*guest_refs)

        body(*refs[:n_in], *refs[i1:o1], *refs[o2:s1])

        @pl.when(last)
        def _():
            guest["finish"](*guest_refs)

    aliases = {n_in + k: n_out + k for k in range(g_in)} if guest.get("in_place") else {}
    return call(wrapped, list(in_specs) + [any_spec] * g_in, list(out_specs) + [any_spec] * g_out,
                list(out_shape) + list(guest["out_shape"]), list(scratch_shapes) + list(guest["scratch"]), aliases,
                list(args) + list(guest["args"]))


def _rope_rotate(t, cos, sin_a, sin_b):
    outs = []
    for cidx in range(t.shape[1] // 128):
        tc = t[:, cidx * 128:(cidx + 1) * 128]
        outs.append(tc * cos + pltpu.roll(tc, 120, 1) * sin_a + pltpu.roll(tc, 8, 1) * sin_b)
    return jnp.concatenate(outs, axis=1)


def _mm_nn(a, b, *, tm, tn, out_dtype, name, resid=None, rope=None, guest=None, groups=None):
    M, K = a.shape
    N = b.shape[1]
    assert M % tm == 0 and N % tn == 0 and b.shape[0] == K
    n_in = 2 + (resid is not None) + (3 if rope is not None else 0)
    if groups is not None:
        assert rope is not None and N == 3 * tn * len(groups)

    def body(*refs):
        a_ref, b_ref = refs[0], refs[1]
        o_ref = refs[n_in]
        acc = jnp.dot(a_ref[...], b_ref[...], preferred_element_type=F32)
        if resid is not None:
            acc = acc + refs[2][...]
        if groups is not None:
            cos_ref, sa_ref, sb_ref = refs[n_in - 3:n_in]
            j = pl.program_id(1)
            for g, d in enumerate(groups):
                for is_v in (False, True):
                    @pl.when(jnp.logical_and(j // 3 == g, (j % 3 == 2) == is_v))
                    def _(g=g, d=d, is_v=is_v):
                        val = acc if is_v else _rope_rotate(acc, cos_ref[...], sa_ref[...], sb_ref[...])
                        if d == 1:
                            refs[n_in + g][...] = val.astype(out_dtype)
                        else:
                            _to_view(val, refs[-1], refs[n_in + g], d, tn)
        elif rope is not None:
            cos_ref, sa_ref, sb_ref = refs[n_in - 3:n_in]
            j = pl.program_id(1)

            @pl.when(j % 3 != 2)
            def _():
                o_ref[...] = _rope_rotate(acc, cos_ref[...], sa_ref[...], sb_ref[...]).astype(out_dtype)

            @pl.when(j % 3 == 2)
            def _():
                o_ref[...] = acc.astype(out_dtype)
        else:
            o_ref[...] = acc.astype(out_dtype)

    in_specs = [pl.BlockSpec((tm, K), lambda i, j: (i, 0)), pl.BlockSpec((K, tn), lambda i, j: (0, j))]
    args = [a, b]
    if resid is not None:
        in_specs.append(pl.BlockSpec((tm, tn), lambda i, j: (i, j)))
        args.append(resid)
    if rope is not None:
        assert tn == 1024
        for t in rope:
            in_specs.append(pl.BlockSpec((tm, 128), lambda i, j: (i, 0)))
            args.append(t)
    if groups is None:
        out_specs = [pl.BlockSpec((tm, tn), lambda i, j: (i, j))]
        out_shape = [jax.ShapeDtypeStruct((M, N), out_dtype)]
        scratch = []
    else:
        out_specs = [pl.BlockSpec((tm // d, d * tn), lambda i, j, g=g: (i, jnp.clip(j - 3 * g, 0, 2)))
                     for g, d in enumerate(groups)]
        out_shape = [jax.ShapeDtypeStruct((M // d, d * 3 * tn), out_dtype) for d in groups]
        scratch = [pltpu.VMEM((tn // 128, tm, 128), F32)]
    outs = _hosted_call(body, grid=(M // tm, N // tn), in_specs=in_specs, out_specs=out_specs, out_shape=out_shape,
                        scratch_shapes=scratch, args=args, name=name, guest=guest)
    nout = len(out_shape)
    res = outs[0] if groups is None else list(outs[:nout])
    return res if guest is None else (res, outs[nout:])


def _mm_nt(a, b, *, tm, to, tn, out_dtype, name, add=None, guest=None, views=(1,)):
    M, N = a.shape
    O = b.shape[0]
    assert M % tm == 0 and O % to == 0 and N % tn == 0 and b.shape[1] == N
    nk = N // tn

    def body(*refs):
        a_ref, b_ref = refs[0], refs[1]
        n_in = 2 + (add is not None)
        o_refs = refs[n_in:n_in + len(views)]
        acc_ref = refs[n_in + len(views)]
        k = pl.program_id(2)

        @pl.when(k == 0)
        def _():
            if add is not None:
                acc_ref[...] = refs[2][...]
            else:
                acc_ref[...] = jnp.zeros_like(acc_ref)

        acc_ref[...] += lax.dot_general(a_ref[...], b_ref[...], (((1,), (1,)), ((), ())),
                                        preferred_element_type=F32)

        @pl.when(k == nk - 1)
        def _():
            for o_ref, d in zip(o_refs, views):
                if d == 1:
                    o_ref[...] = acc_ref[...].astype(out_dtype)
                else:
                    _to_view(acc_ref[...], refs[-1], o_ref, d, to)

    in_specs = [pl.BlockSpec((tm, tn), lambda i, j, k: (i, k)), pl.BlockSpec((to, tn), lambda i, j, k: (j, k))]
    args = [a, b]
    if add is not None:
        in_specs.append(pl.BlockSpec((tm, to), lambda i, j, k: (i, j)))
        args.append(add)
    assert views == (1,) or (to == O and to % 128 == 0)
    scratch = [pltpu.VMEM((tm, to), F32)] + ([pltpu.VMEM((to // 128, tm, 128), F32)] if views != (1,) else [])
    outs = _hosted_call(
        body, grid=(M // tm, O // to, nk), in_specs=in_specs,
        out_specs=[pl.BlockSpec((tm, to), lambda i, j, k: (i, j)) if d == 1 else _view_spec(tm, d, to) for d in views],
        out_shape=[jax.ShapeDtypeStruct((M // d, d * O), out_dtype) for d in views],
        scratch_shapes=scratch, args=args, name=name, guest=guest)
    nv = len(views)
    res = outs[0] if nv == 1 else list(outs[:nv])
    return res if guest is None else (res, outs[nv:])


def _mm_tn(a, b, *, tk, tn, tm, out_dtype, name):
    M, K = a.shape
    N = b.shape[1]
    assert M % tm == 0 and K % tk == 0 and N % tn == 0 and b.shape[0] == M
    nm = M // tm

    def body(a_ref, b_ref, o_ref, acc_ref):
        m = pl.program_id(2)

        @pl.when(m == 0)
        def _():
            acc_ref[...] = jnp.zeros_like(acc_ref)

        acc_ref[...] += lax.dot_general(a_ref[...], b_ref[...], (((0,), (0,)), ((), ())),
                                        preferred_element_type=F32)

        @pl.when(m == nm - 1)
        def _():
            o_ref[...] = acc_ref[...].astype(out_dtype)

    return pl.pallas_call(
        body, grid=(K // tk, N // tn, nm),
        in_specs=[pl.BlockSpec((tm, tk), lambda i, j, m: (m, i)), pl.BlockSpec((tm, tn), lambda i, j, m: (m, j))],
        out_specs=pl.BlockSpec((tk, tn), lambda i, j, m: (i, j)),
        out_shape=jax.ShapeDtypeStruct((K, N), out_dtype),
        scratch_shapes=[pltpu.VMEM((tk, tn), F32)], name=name,
        compiler_params=_params("parallel", "parallel", "arbitrary"),
    )(a, b)


ROW_TILE = 512
MM_ROWS = 1024


def _rms_fwd(x, g, *, name):
    S, Dm = x.shape

    def body(x_ref, g_ref, o_ref):
        xf = x_ref[...]
        r = lax.rsqrt(jnp.mean(xf * xf, axis=-1, keepdims=True) + RMS_EPS)
        o_ref[...] = (xf * r * g_ref[...]).astype(BF)

    return pl.pallas_call(
        body, grid=(S // ROW_TILE,),
        in_specs=[pl.BlockSpec((ROW_TILE, Dm), lambda i: (i, 0)), pl.BlockSpec((1, Dm), lambda i: (0, 0))],
        out_specs=pl.BlockSpec((ROW_TILE, Dm), lambda i: (i, 0)),
        out_shape=jax.ShapeDtypeStruct((S, Dm), BF), name=name, compiler_params=_params("parallel"),
    )(x, g)


def _rms_bwd(x, g, dn, dres, *, name):
    S, Dm = x.shape

    def body(x_ref, g_ref, dn_ref, dres_ref, dx_ref, dxb_ref, dg_ref):
        i = pl.program_id(0)
        xf = x_ref[...]
        r = lax.rsqrt(jnp.mean(xf * xf, axis=-1, keepdims=True) + RMS_EPS)
        xh = xf * r
        dnf = dn_ref[...]
        dyg = dnf * g_ref[...]
        dx = dres_ref[...] + r * (dyg - xh * jnp.mean(dyg * xh, axis=-1, keepdims=True))
        dx_ref[...] = dx
        dxb_ref[...] = dx.astype(BF)

        @pl.when(i == 0)
        def _():
            dg_ref[...] = jnp.zeros_like(dg_ref)

        dg_ref[...] += jnp.sum(dnf * xh, axis=0, keepdims=True)

    row = pl.BlockSpec((ROW_TILE, Dm), lambda i: (i, 0))
    vec = pl.BlockSpec((1, Dm), lambda i: (0, 0))
    return pl.pallas_call(
        body, grid=(S // ROW_TILE,), in_specs=[row, vec, row, row], out_specs=[row, row, vec],
        out_shape=[jax.ShapeDtypeStruct((S, Dm), F32), jax.ShapeDtypeStruct((S, Dm), BF),
                   jax.ShapeDtypeStruct((1, Dm), F32)],
        name=name, compiler_params=_params("arbitrary"),
    )(x, g, dn, dres)


def _loss_head(h, g, tgt, *, name):
    S, Dm = h.shape

    def body(h_ref, g_ref, t_ref, loss_ref, dh_ref, dhb_ref, dg_ref):
        i = pl.program_id(0)
        xf = h_ref[...]
        r = lax.rsqrt(jnp.mean(xf * xf, axis=-1, keepdims=True) + RMS_EPS)
        xh = xf * r
        gv = g_ref[...]
        err = xh * gv - t_ref[...]
        dy = err * (1.0 / Dm)
        dyg = dy * gv
        dh = r * (dyg - xh * jnp.mean(dyg * xh, axis=-1, keepdims=True))
        dh_ref[...] = dh
        dhb_ref[...] = dh.astype(BF)

        @pl.when(i == 0)
        def _():
            dg_ref[...] = jnp.zeros_like(dg_ref)
            loss_ref[...] = jnp.zeros_like(loss_ref)

        dg_ref[...] += jnp.sum(dy * xh, axis=0, keepdims=True)
        part = 0.5 * jnp.sum(jnp.mean(err * err, axis=-1, keepdims=True), axis=0, keepdims=True)
        loss_ref[...] += jnp.broadcast_to(part, loss_ref.shape)

    row = pl.BlockSpec((ROW_TILE, Dm), lambda i: (i, 0))
    vec = pl.BlockSpec((1, Dm), lambda i: (0, 0))
    return pl.pallas_call(
        body, grid=(S // ROW_TILE,), in_specs=[row, vec, row],
        out_specs=[pl.BlockSpec((1, 128), lambda i: (0, 0)), row, row, vec],
        out_shape=[jax.ShapeDtypeStruct((1, 128), F32), jax.ShapeDtypeStruct((S, Dm), F32),
                   jax.ShapeDtypeStruct((S, Dm), BF), jax.ShapeDtypeStruct((1, Dm), F32)],
        name=name, compiler_params=_params("arbitrary"),
    )(h, g, tgt)


SWIGLU_ROWS = 512


def _swiglu_fwd(gu, *, name):
    S = gu.shape[0]

    def body(g_ref, u_ref, o_ref):
        g = g_ref[...].astype(F32)
        sig = 1.0 / (1.0 + jnp.exp(-g))
        o_ref[...] = (g * sig * u_ref[...].astype(F32)).astype(BF)

    return pl.pallas_call(
        body, grid=(S // SWIGLU_ROWS,),
        in_specs=[pl.BlockSpec((SWIGLU_ROWS, D_FF), lambda i: (i, 0)), pl.BlockSpec((SWIGLU_ROWS, D_FF), lambda i: (i, 1))],
        out_specs=pl.BlockSpec((SWIGLU_ROWS, D_FF), lambda i: (i, 0)),
        out_shape=jax.ShapeDtypeStruct((S, D_FF), BF), name=name, compiler_params=_params("parallel"),
    )(gu, gu)


def _swiglu_bwd(gu, dact, *, name):
    S = gu.shape[0]

    def body(g_ref, u_ref, d_ref, o_ref):
        g = g_ref[...].astype(F32)
        u = u_ref[...].astype(F32)
        d = d_ref[...].astype(F32)
        sig = 1.0 / (1.0 + jnp.exp(-g))
        o_ref[:, :D_FF] = (d * u * sig * (1.0 + g * (1.0 - sig))).astype(BF)
        o_ref[:, D_FF:] = (d * g * sig).astype(BF)

    return pl.pallas_call(
        body, grid=(S // SWIGLU_ROWS,),
        in_specs=[pl.BlockSpec((SWIGLU_ROWS, D_FF), lambda i: (i, 0)), pl.BlockSpec((SWIGLU_ROWS, D_FF), lambda i: (i, 1)),
                  pl.BlockSpec((SWIGLU_ROWS, D_FF), lambda i: (i, 0))],
        out_specs=pl.BlockSpec((SWIGLU_ROWS, 2 * D_FF), lambda i: (i, 0)),
        out_shape=jax.ShapeDtypeStruct((S, 2 * D_FF), BF), name=name, compiler_params=_params("parallel"),
    )(gu, gu, dact)


def _band_masks(T, n):
    row = lax.broadcasted_iota(jnp.int32, (T, T), 0)
    col = lax.broadcasted_iota(jnp.int32, (T, T), 1)
    return jnp.logical_and(col >= row, n > 0), col <= row


def _band_fwd(qa, ka, va, qcb, kcb, vcb, *, dil, T, window, name, guest=None):
    L = qa.shape[0]
    nq = L // T
    assert window == T
    nt = (((1,), (1,)), ((), ()))

    def body(q_ref, kp_ref, kc_ref, vp_ref, vc_ref, o_ref, lse_ref):
        valid_prev, valid_cur = _band_masks(T, pl.program_id(1))
        lane = lax.broadcasted_iota(jnp.int32, (T, 128), 1)
        low = lane < HEAD_DIM
        ones = jnp.ones((T, 128), BF)
        lse = jnp.zeros((T, 128), F32)
        def scores(h):
            ps = slice((h // 2) * 128, (h // 2 + 1) * 128)
            qp = q_ref[:, ps] * jnp.asarray(ATTN_SCALE, BF)
            qm = jnp.where(low if h % 2 == 0 else jnp.logical_not(low), qp, jnp.zeros_like(qp))
            s0 = jnp.where(valid_prev, lax.dot_general(qm, kp_ref[:, ps], nt, preferred_element_type=F32), NEG_INF)
            s1 = jnp.where(valid_cur, lax.dot_general(qm, kc_ref[:, ps], nt, preferred_element_type=F32), NEG_INF)
            return s0, s1

        def softmax(s0, s1):
            m = jnp.maximum(jnp.max(s0, axis=1, keepdims=True), jnp.max(s1, axis=1, keepdims=True))
            return m, jnp.exp(s0 - m).astype(BF), jnp.exp(s1 - m).astype(BF)

        def weighted(h, p0, p1):
            ps = slice((h // 2) * 128, (h // 2 + 1) * 128)
            l = jnp.dot(p0, ones, preferred_element_type=F32) + jnp.dot(p1, ones, preferred_element_type=F32)
            acc = jnp.dot(p0, vp_ref[:, ps], preferred_element_type=F32) + jnp.dot(p1, vc_ref[:, ps], preferred_element_type=F32)
            return l, acc

        sc, pr, even = {}, {}, None
        for t in range(N_HEADS + 2):
            if t < N_HEADS:
                sc[t] = scores(t)
            done = None
            if t >= 2:
                m, p0, p1 = pr.pop(t - 2)
                done = (m,) + weighted(t - 2, p0, p1)
            if 1 <= t <= N_HEADS:
                pr[t - 1] = softmax(*sc.pop(t - 1))
            if done is not None:
                h = t - 2
                m, l, acc = done
                lse = jnp.where(lane == h, m + jnp.log(l), lse)
                if h % 2 == 0:
                    even = acc / l
                else:
                    o_ref[:, (h // 2) * 128:(h // 2 + 1) * 128] = jnp.where(low, even, acc / l)
        lse_ref[...] = lse

    def prev(n):
        return jnp.maximum(n - 1, 0)

    blk = lambda f, cb: pl.BlockSpec((T, 1024), lambda r, n: (f(n), cb(r)))
    same = lambda n: n
    outs = _hosted_call(
        body, grid=(dil, nq),
        in_specs=[blk(same, qcb), blk(prev, kcb), blk(same, kcb), blk(prev, vcb), blk(same, vcb)],
        out_specs=[pl.BlockSpec((T, 1024), lambda r, n: (n, r)), pl.BlockSpec((T, 128), lambda r, n: (n, r))],
        out_shape=[jax.ShapeDtypeStruct((L, dil * 1024), F32), jax.ShapeDtypeStruct((L, dil * 128), F32)],
        scratch_shapes=[], args=(qa, ka, ka, va, va), name=name, guest=guest)
    return outs if guest is None else (outs[:2], outs[2:])


def _band_bwd(qa, ka, va, qcb, kcb, vcb, doa, oa, lsea, *, dil, T, window, name, guest=None):
    L = qa.shape[0]
    nq = L // T
    assert window == T
    nt = (((1,), (1,)), ((), ()))
    tn = (((0,), (0,)), ((), ()))

    def body(q_ref, kp_ref, kc_ref, vp_ref, vc_ref, do_ref, o_ref, lse_ref, dq_ref, dk_ref, dv_ref, ck_sc, cv_sc):
        n = pl.program_id(1)

        @pl.when(n == 0)
        def _():
            ck_sc[...] = jnp.zeros_like(ck_sc)
            cv_sc[...] = jnp.zeros_like(cv_sc)

        @pl.when(n < nq)
        def _():
            valid_prev, valid_cur = _band_masks(T, n)
            low = lax.broadcasted_iota(jnp.int32, (T, 128), 1) < HEAD_DIM
            dot = functools.partial(lax.dot_general, preferred_element_type=F32)

            def pair(h):
                return slice((h // 2) * 128, (h // 2 + 1) * 128)

            def products(h):
                ps = pair(h)
                mask = low if h % 2 == 0 else jnp.logical_not(low)
                qp = q_ref[:, ps] * jnp.asarray(ATTN_SCALE, BF)
                dop = do_ref[:, ps]
                qm = jnp.where(mask, qp, jnp.zeros_like(qp))
                dom = jnp.where(mask, dop, jnp.zeros_like(dop))
                s0 = jnp.where(valid_prev, dot(qm, kp_ref[:, ps], nt), NEG_INF)
                s1 = jnp.where(valid_cur, dot(qm, kc_ref[:, ps], nt), NEG_INF)
                return qm, dom, s0, s1, dot(dom, vp_ref[:, ps], nt), dot(dom, vc_ref[:, ps], nt)

            def pointwise(h, qm, dom, s0, s1, dp0, dp1):
                ps = pair(h)
                mask = low if h % 2 == 0 else jnp.logical_not(low)
                prod = do_ref[:, ps].astype(F32) * o_ref[:, ps].astype(F32)
                delta = jnp.sum(jnp.where(mask, prod, 0.0), axis=1, keepdims=True)
                lse = lse_ref[:, h:h + 1]
                p0 = jnp.exp(s0 - lse)
                p1 = jnp.exp(s1 - lse)
                ds0 = (p0 * (dp0 - delta)).astype(BF)
                ds1 = (p1 * (dp1 - delta)).astype(BF)
                return qm, dom, p0.astype(BF), p1.astype(BF), ds0, ds1

            def gradients(h, qm, dom, p0, p1, ds0, ds1):
                ps = pair(h)
                dq = dot(ds0, kp_ref[:, ps], (((1,), (0,)), ((), ()))) + dot(ds1, kc_ref[:, ps], (((1,), (0,)), ((), ())))
                return dq, dot(ds0, qm, tn), dot(p0, dom, tn), dot(ds1, qm, tn), dot(p1, dom, tn)

            st1, st2, even = {}, {}, None
            for t in range(N_HEADS + 2):
                if t < N_HEADS:
                    st1[t] = products(t)
                done = gradients(t - 2, *st2.pop(t - 2)) if t >= 2 else None
                if 1 <= t <= N_HEADS:
                    st2[t - 1] = pointwise(t - 1, *st1.pop(t - 1))
                if done is not None:
                    h = t - 2
                    if h % 2 == 0:
                        even = done
                    else:
                        ps = pair(h)
                        dq_ref[:, ps] = (jnp.where(low, even[0], done[0]) * ATTN_SCALE).astype(BF)
                        dk_ref[:, ps] = (ck_sc[:, ps] + even[1] + done[1]).astype(BF)
                        dv_ref[:, ps] = (cv_sc[:, ps] + even[2] + done[2]).astype(BF)
                        ck_sc[:, ps] = even[3] + done[3]
                        cv_sc[:, ps] = even[4] + done[4]

        @pl.when(n == nq)
        def _():
            dk_ref[...] = ck_sc[...].astype(BF)
            dv_ref[...] = cv_sc[...].astype(BF)

    def cur(n):
        return jnp.minimum(n, nq - 1)

    def prev(n):
        return jnp.maximum(cur(n) - 1, 0)

    blk = lambda f, cb: pl.BlockSpec((T, 1024), lambda r, n: (f(n), cb(r)))
    own = lambda r: r
    outs = _hosted_call(
        body, grid=(dil, nq + 1),
        in_specs=[blk(cur, qcb), blk(prev, kcb), blk(cur, kcb), blk(prev, vcb), blk(cur, vcb), blk(cur, own), blk(cur, own),
                  pl.BlockSpec((T, 128), lambda r, n: (cur(n), r))],
        out_specs=[blk(cur, own), blk(lambda n: jnp.maximum(n - 1, 0), own), blk(lambda n: jnp.maximum(n - 1, 0), own)],
        out_shape=[jax.ShapeDtypeStruct((L, dil * 1024), BF)] * 3,
        scratch_shapes=[pltpu.VMEM((T, 1024), F32), pltpu.VMEM((T, 1024), F32)],
        args=(qa, ka, ka, va, va, doa, oa, lsea), name=name, guest=guest)
    return outs if guest is None else (outs[:3], outs[3:])


FOX_T = 256
FOX_TK = 512
FOX_TQ_BWD = 512
FOX_ROWS = 256


def _fox_fwd(qkv, cT, *, name, guest=None):
    S = qkv.shape[0]
    T, TK, R = FOX_T, FOX_TK, FOX_ROWS
    nq = S // T
    nt = (((1,), (1,)), ((), ()))
    chains = [(h, rh) for h in range(N_HEADS) for rh in range(T // R)]
    pairs = [(n, j) for n in range(nq) for j in range((n * T + T - 1) // TK + 1)]
    schedule = [jnp.asarray([p[i] for p in pairs], jnp.int32) for i in range(2)]

    def body(n_tab, j_tab, q_ref, k_ref, v_ref, ct_ref, o_ref, lse_ref, m_sc, l_sc, acc_sc):
        n = n_tab[pl.program_id(0)]
        j = j_tab[pl.program_id(0)]
        last_j = (n * T + T - 1) // TK
        lane = lax.broadcasted_iota(jnp.int32, (R, 128), 1)
        low = lane < HEAD_DIM
        ones = jnp.ones((TK, 128), BF)

        @pl.when(j == 0)
        def _():
            m_sc[...] = jnp.full(m_sc.shape, NEG_INF, F32)
            l_sc[...] = jnp.zeros_like(l_sc)
            acc_sc[...] = jnp.zeros_like(acc_sc)

        def step(diagonal):
            def pair(h):
                return slice((h // 2) * 128, (h // 2 + 1) * 128)

            def rows(rh):
                return slice(rh * R, (rh + 1) * R)

            def scores(h, rh):
                qp = q_ref[rows(rh), pair(h)] * jnp.asarray(ATTN_SCALE, BF)
                qm = jnp.where(low if h % 2 == 0 else jnp.logical_not(low), qp, jnp.zeros_like(qp))
                s = lax.dot_general(qm, k_ref[:, pair(h)], nt, preferred_element_type=F32) - ct_ref[h:h + 1, :]
                if diagonal:
                    ahead = lax.broadcasted_iota(jnp.int32, (R, TK), 1) - lax.broadcasted_iota(jnp.int32, (R, TK), 0)
                    s = jnp.where(ahead <= n * T + rh * R - j * TK, s, NEG_INF)
                return s

            def softmax(h, rh, s):
                m_prev = m_sc[h, rows(rh), :]
                m_new = jnp.maximum(m_prev, jnp.max(s, axis=1, keepdims=True))
                p = jnp.exp(s - jnp.concatenate([m_new] * (TK // 128), axis=1)).astype(BF)
                return m_new, jnp.exp(m_prev - m_new), p

            def weighted(h, p):
                vx = jnp.concatenate([v_ref[:, pair(h)], ones], axis=1)
                return jnp.dot(p, vx, preferred_element_type=F32)

            sc, pr, even = {}, {}, {}
            nch = len(chains)
            for t in range(nch + 2):
                if t < nch:
                    sc[t] = scores(*chains[t])
                done = None
                if t >= 2:
                    m_new, alpha, p = pr.pop(t - 2)
                    done = (m_new, alpha, weighted(chains[t - 2][0], p))
                if 1 <= t <= nch:
                    pr[t - 1] = softmax(*chains[t - 1], sc.pop(t - 1))
                if done is not None:
                    h, rh = chains[t - 2]
                    m_new, alpha, pv = done
                    m_sc[h, rows(rh), :] = m_new
                    l_sc[h, rows(rh), :] = alpha * l_sc[h, rows(rh), :] + pv[:, 128:]
                    if h % 2 == 0:
                        even[rh] = (alpha, pv[:, :128])
                    else:
                        a0, pv0 = even.pop(rh)
                        acc = acc_sc[h // 2, rows(rh), :]
                        acc_sc[h // 2, rows(rh), :] = jnp.where(low, a0 * acc + pv0, alpha * acc + pv[:, :128])

        @pl.when(j < last_j)
        def _():
            step(False)

        @pl.when(j == last_j)
        def _():
            step(True)
            lane_t = lax.broadcasted_iota(jnp.int32, (T, 128), 1)
            low_t = lane_t < HEAD_DIM
            lse = jnp.zeros((T, 128), F32)
            for h in range(N_HEADS):
                lse = jnp.where(lane_t == h, m_sc[h] + jnp.log(l_sc[h]), lse)
            lse_ref[...] = lse
            for hp in range(N_HEADS // 2):
                inv = jnp.where(low_t, 1.0 / l_sc[2 * hp], 1.0 / l_sc[2 * hp + 1])
                o_ref[:, hp * 128:(hp + 1) * 128] = (acc_sc[hp] * inv).astype(BF)

    outs = _hosted_call(
        body, grid=(len(pairs),),
        in_specs=[pl.BlockSpec((T, 1024), lambda t, n, j: (n[t], 0)), pl.BlockSpec((TK, 1024), lambda t, n, j: (j[t], 1)),
                  pl.BlockSpec((TK, 1024), lambda t, n, j: (j[t], 2)), pl.BlockSpec((GATE_LANES, TK), lambda t, n, j: (0, j[t]))],
        out_specs=[pl.BlockSpec((T, 1024), lambda t, n, j: (n[t], 0)), pl.BlockSpec((T, 128), lambda t, n, j: (n[t], 0))],
        out_shape=[jax.ShapeDtypeStruct((S, 1024), BF), jax.ShapeDtypeStruct((S, 128), F32)],
        scratch_shapes=[pltpu.VMEM((N_HEADS, T, 128), F32), pltpu.VMEM((N_HEADS, T, 128), F32),
                        pltpu.VMEM((N_HEADS // 2, T, 128), F32)],
        args=(qkv, qkv, qkv, cT), name=name, guest=guest, schedule=schedule)
    return outs if guest is None else (outs[:2], outs[2:])


def _fox_bwd(qkv, cT, do, o, lse, *, name, guest=None):
    S = qkv.shape[0]
    T, TQ, R = FOX_T, FOX_TQ_BWD, FOX_ROWS
    nk, nq = S // T, S // TQ
    nt = (((1,), (1,)), ((), ()))
    tn = (((0,), (0,)), ((), ()))
    nn = (((1,), (0,)), ((), ()))
    chains = [(h, rh) for h in range(N_HEADS) for rh in range(TQ // R)]
    dot = functools.partial(lax.dot_general, preferred_element_type=F32)
    pairs = [(kb, qb) for kb in range(nk) for qb in range(kb * T // TQ, nq)]
    schedule = [jnp.asarray([p[i] for p in pairs], jnp.int32) for i in range(2)]

    def body(kb_tab, qb_tab, q_ref, k_ref, v_ref, ct_ref, do_ref, o_ref, lse_ref, dq_ref, dk_ref, dv_ref, dct_ref, dcq_ref,
             dq_sc, dk_sc, dv_sc, dc_sc, dcq_sc):
        kb = kb_tab[pl.program_id(0)]
        qb = qb_tab[pl.program_id(0)]
        jq = qb - kb * T // TQ
        lane = lax.broadcasted_iota(jnp.int32, (R, 128), 1)
        low = lane < HEAD_DIM
        ones_k = jnp.ones((T, 128), BF)
        ones_r = jnp.ones((8, R), BF)

        @pl.when(jnp.logical_and(kb == 0, jq == 0))
        def _():
            dq_sc[...] = jnp.zeros_like(dq_sc)
            dcq_sc[...] = jnp.zeros_like(dcq_sc)

        @pl.when(jq == 0)
        def _():
            dk_sc[...] = jnp.zeros_like(dk_sc)
            dv_sc[...] = jnp.zeros_like(dv_sc)
            dc_sc[...] = jnp.zeros_like(dc_sc)

        def step(diagonal):
            def pair(h):
                return slice((h // 2) * 128, (h // 2 + 1) * 128)

            def rows(rh):
                return slice(rh * R, (rh + 1) * R)

            def qrows(rh):
                return pl.ds(pl.multiple_of(qb * TQ + rh * R, R), R)

            def products(h, rh):
                mask = low if h % 2 == 0 else jnp.logical_not(low)
                qp = q_ref[rows(rh), pair(h)] * jnp.asarray(ATTN_SCALE, BF)
                dop = do_ref[rows(rh), pair(h)]
                qm = jnp.where(mask, qp, jnp.zeros_like(qp))
                dom = jnp.where(mask, dop, jnp.zeros_like(dop))
                s = dot(qm, k_ref[:, pair(h)], nt) - ct_ref[h:h + 1, :]
                if diagonal:
                    ahead = lax.broadcasted_iota(jnp.int32, (R, T), 1) - lax.broadcasted_iota(jnp.int32, (R, T), 0)
                    s = jnp.where(ahead <= qb * TQ + rh * R - kb * T, s, NEG_INF)
                return qm, dom, s, dot(dom, v_ref[:, pair(h)], nt)

            def pointwise(h, rh, qm, dom, s, dp):
                mask = low if h % 2 == 0 else jnp.logical_not(low)
                prod = do_ref[rows(rh), pair(h)].astype(F32) * o_ref[rows(rh), pair(h)].astype(F32)
                delta = jnp.sum(jnp.where(mask, prod, 0.0), axis=1, keepdims=True)
                p = jnp.exp(s - lse_ref[rows(rh), h:h + 1])
                ds = (p * (dp - delta)).astype(BF)
                return qm, dom, p.astype(BF), ds

            def gradients(h, qm, dom, p, ds):
                kx = jnp.concatenate([k_ref[:, pair(h)], ones_k], axis=1)
                return dot(ds, kx, nn), dot(qm, ds, tn), dot(dom, p, tn), dot(ones_r, ds, nn)

            st1, st2, even = {}, {}, {}
            dcq_tiles = [jnp.zeros((R, 128), F32) for _ in range(TQ // R)]
            nch = len(chains)
            for t in range(nch + 2):
                if t < nch:
                    st1[t] = products(*chains[t])
                done = gradients(chains[t - 2][0], *st2.pop(t - 2)) if t >= 2 else None
                if 1 <= t <= nch:
                    st2[t - 1] = pointwise(*chains[t - 1], *st1.pop(t - 1))
                if done is not None:
                    h, rh = chains[t - 2]
                    dq_rsum, dk, dv, csum = done
                    dq = dq_rsum[:, :128]
                    dcq_tiles[rh] = jnp.where(lane == h, dq_rsum[:, 128:], dcq_tiles[rh])
                    dc_sc[h:h + 1, :] -= csum[0:1, :]
                    if h % 2 == 0:
                        even[rh] = (dq, dk, dv)
                    else:
                        dq0, dk0, dv0 = even.pop(rh)
                        dq_sc[qrows(rh), pair(h)] += jnp.where(low, dq0, dq) * ATTN_SCALE
                        dk_sc[h // 2] += dk0 + dk
                        dv_sc[h // 2] += dv0 + dv
            for rh in range(TQ // R):
                dcq_sc[qrows(rh), :] += dcq_tiles[rh]

        @pl.when(jq > 0)
        def _():
            step(False)

        @pl.when(jq == 0)
        def _():
            step(True)

        @pl.when(qb == nq - 1)
        def _():
            for hp in range(N_HEADS // 2):
                dk_ref[:, hp * 128:(hp + 1) * 128] = dk_sc[hp].T.astype(BF)
                dv_ref[:, hp * 128:(hp + 1) * 128] = dv_sc[hp].T.astype(BF)
            dct_ref[...] = dc_sc[...]

        @pl.when(jnp.logical_and(kb == nk - 1, qb == nq - 1))
        def _():
            def put(i, carry):
                r = pl.ds(pl.multiple_of(i * T, T), T)
                dq_ref[r, :] = dq_sc[r, :].astype(BF)
                return carry
            lax.fori_loop(0, nk, put, 0)
            dcq_ref[...] = dcq_sc[...]

    qblk = lambda col: pl.BlockSpec((TQ, 1024), lambda t, kb, qb: (qb[t], col))
    kblk = lambda col: pl.BlockSpec((T, 1024), lambda t, kb, qb: (kb[t], col))
    whole = pl.BlockSpec((S, 1024), lambda t, kb, qb: (0, 0))
    outs = _hosted_call(
        body, grid=(len(pairs),),
        in_specs=[qblk(0), kblk(1), kblk(2), pl.BlockSpec((GATE_LANES, T), lambda t, kb, qb: (0, kb[t])), qblk(0), qblk(0),
                  pl.BlockSpec((TQ, 128), lambda t, kb, qb: (qb[t], 0))],
        out_specs=[whole, kblk(0), kblk(0), pl.BlockSpec((GATE_LANES, T), lambda t, kb, qb: (0, kb[t])),
                   pl.BlockSpec((S, GATE_LANES), lambda t, kb, qb: (0, 0))],
        out_shape=[jax.ShapeDtypeStruct((S, 1024), BF)] * 3 + [jax.ShapeDtypeStruct((GATE_LANES, S), F32),
                                                               jax.ShapeDtypeStruct((S, GATE_LANES), F32)],
        scratch_shapes=[pltpu.VMEM((S, 1024), F32), pltpu.VMEM((N_HEADS // 2, 128, T), F32), pltpu.VMEM((N_HEADS // 2, 128, T), F32),
                        pltpu.VMEM((GATE_LANES, T), F32), pltpu.VMEM((S, GATE_LANES), F32)],
        args=(qkv, qkv, qkv, cT, do, o, lse), name=name, guest=guest, schedule=schedule)
    return outs if guest is None else (outs[:5], outs[5:])


def _to_natural(src_ref, buf, d, width):
    rows = buf.shape[1]
    for r in range(d):
        for ch in range(width // 128):
            lanes = slice(r * width + ch * 128, r * width + (ch + 1) * 128)
            buf.at[ch][pl.ds(r, rows // d, stride=d), :] = src_ref[:, lanes].astype(F32)
    return jnp.concatenate([buf[ch] for ch in range(width // 128)], axis=1)


def _to_view(val, buf, dst_ref, d, width):
    rows = buf.shape[1]
    for ch in range(width // 128):
        buf[ch] = val[:, ch * 128:(ch + 1) * 128]
    for r in range(d):
        for ch in range(width // 128):
            lanes = slice(r * width + ch * 128, r * width + (ch + 1) * 128)
            dst_ref[:, lanes] = buf.at[ch][pl.ds(r, rows // d, stride=d), :].astype(dst_ref.dtype)


def _view_spec(rows, d, width):
    return pl.BlockSpec((rows // d, d * width), lambda i, *_: (i, 0))


def _combine_groups(os, lses, dils, *, name):
    ng = len(os)
    S = os[0].shape[0] * dils[0]
    tm = ROW_TILE
    views = sorted(set(dils))

    def body(*refs):
        o_refs, l_refs = refs[:ng], refs[ng:2 * ng]
        outs = refs[2 * ng:2 * ng + 2 * len(views)]
        wide, narrow = refs[-2], refs[-1]
        ls = [l_refs[g][...] if dils[g] == 1 else _to_natural(l_refs[g], narrow, dils[g], 128) for g in range(ng)]
        m = functools.reduce(jnp.maximum, ls)
        es = [jnp.exp(l - m) for l in ls]
        den = functools.reduce(jnp.add, es)
        ws = [e / den for e in es]
        lse = m + jnp.log(den)
        og = [o_refs[g][...] if dils[g] == 1 else _to_natural(o_refs[g], wide, dils[g], 1024) for g in range(ng)]
        cols = []
        for h in range(N_HEADS):
            hs = slice(h * HEAD_DIM, (h + 1) * HEAD_DIM)
            acc = ws[0][:, h:h + 1] * og[0][:, hs]
            for g in range(1, ng):
                acc = acc + ws[g][:, h:h + 1] * og[g][:, hs]
            cols.append(acc)
        o = jnp.concatenate(cols, axis=1)
        for k, d in enumerate(views):
            if d == 1:
                outs[2 * k][...] = o.astype(BF)
                outs[2 * k + 1][...] = lse
            else:
                _to_view(o, wide, outs[2 * k], d, 1024)
                _to_view(lse, narrow, outs[2 * k + 1], d, 128)

    out_specs, out_shape = [], []
    for d in views:
        out_specs += [_view_spec(tm, d, 1024), _view_spec(tm, d, 128)]
        out_shape += [jax.ShapeDtypeStruct((S // d, d * 1024), BF), jax.ShapeDtypeStruct((S // d, d * 128), F32)]
    res = pl.pallas_call(
        body, grid=(S // tm,), in_specs=[_view_spec(tm, d, 1024) for d in dils] + [_view_spec(tm, d, 128) for d in dils],
        out_specs=out_specs, out_shape=out_shape,
        scratch_shapes=[pltpu.VMEM((8, tm, 128), F32), pltpu.VMEM((1, tm, 128), F32)],
        name=name, compiler_params=_params("parallel"),
    )(*os, *lses)
    return {d: (res[2 * k], res[2 * k + 1]) for k, d in enumerate(views)}


def _assemble(parts, rope_flags, rope, dils, *, name):
    n = len(parts)
    S = parts[0].shape[0] * dils[0]
    use_rope = any(rope_flags)
    tm = 256

    def body(*refs):
        out_ref, natural = refs[-2], refs[-1]
        for b in range(n):
            cols = slice(b * 1024, (b + 1) * 1024)
            d = dils[b]
            val = refs[b][...].astype(F32) if d == 1 else _to_natural(refs[b], natural, d, 1024)
            if rope_flags[b]:
                cos_ref, sa_ref, sb_ref = refs[n:n + 3]
                val = _rope_rotate(val, cos_ref[...], sa_ref[...], sb_ref[...])
            out_ref[:, cols] = val.astype(BF)

    in_specs = [_view_spec(tm, d, 1024) for d in dils]
    args = list(parts)
    if use_rope:
        in_specs += [pl.BlockSpec((tm, 128), lambda i: (i, 0))] * 3
        args += list(rope)
    return pl.pallas_call(
        body, grid=(S // tm,), in_specs=in_specs, out_specs=pl.BlockSpec((tm, n * 1024), lambda i: (i, 0)),
        out_shape=jax.ShapeDtypeStruct((S, n * 1024), BF), scratch_shapes=[pltpu.VMEM((8, tm, 128), F32)],
        name=name, compiler_params=_params("parallel"),
    )(*args)


GATE_ROWS = 512


def _gate_fwd(z, bf, *, name):
    S = z.shape[0]

    def body(z_ref, b_ref, ct_ref, carry):
        i = pl.program_id(0)

        @pl.when(i == 0)
        def _():
            carry[...] = jnp.zeros_like(carry)

        zz = z_ref[...] + b_ref[...]
        logf = jnp.minimum(zz, 0.0) - jnp.log(1.0 + jnp.exp(-jnp.abs(zz)))
        tri = (lax.broadcasted_iota(jnp.int32, (GATE_ROWS, GATE_ROWS), 0)
               >= lax.broadcasted_iota(jnp.int32, (GATE_ROWS, GATE_ROWS), 1)).astype(F32)
        cs = jnp.dot(tri, logf, precision=lax.Precision.HIGHEST, preferred_element_type=F32) + carry[...]
        ct_ref[...] = cs.T
        carry[...] = cs[GATE_ROWS - 1:GATE_ROWS, :]

    return pl.pallas_call(
        body, grid=(S // GATE_ROWS,),
        in_specs=[pl.BlockSpec((GATE_ROWS, GATE_LANES), lambda i: (i, 0)), pl.BlockSpec((1, GATE_LANES), lambda i: (0, 0))],
        out_specs=pl.BlockSpec((GATE_LANES, GATE_ROWS), lambda i: (0, i)),
        out_shape=jax.ShapeDtypeStruct((GATE_LANES, S), F32),
        scratch_shapes=[pltpu.VMEM((1, GATE_LANES), F32)], name=name, compiler_params=_params("arbitrary"),
    )(z, bf)


def _gate_bwd(z, bf, dcT, dcq, *, name):
    S = z.shape[0]
    nb = S // GATE_ROWS

    def body(z_ref, b_ref, dct_ref, dcq_ref, dz_ref, db_ref, carry):
        i = pl.program_id(0)

        @pl.when(i == 0)
        def _():
            carry[...] = jnp.zeros_like(carry)
            db_ref[...] = jnp.zeros_like(db_ref)

        dc = dct_ref[...].T + dcq_ref[...]
        tri = (lax.broadcasted_iota(jnp.int32, (GATE_ROWS, GATE_ROWS), 0)
               <= lax.broadcasted_iota(jnp.int32, (GATE_ROWS, GATE_ROWS), 1)).astype(F32)
        dl = jnp.dot(tri, dc, precision=lax.Precision.HIGHEST, preferred_element_type=F32) + carry[...]
        carry[...] = dl[0:1, :]
        zz = z_ref[...] + b_ref[...]
        dz = dl * (1.0 / (1.0 + jnp.exp(zz)))
        lane = lax.broadcasted_iota(jnp.int32, dz.shape, 1)
        dz = jnp.where(lane < N_HEADS, dz, 0.0)
        dz_ref[...] = dz.astype(BF)
        db_ref[...] += jnp.sum(dz, axis=0, keepdims=True)

    return pl.pallas_call(
        body, grid=(nb,),
        in_specs=[pl.BlockSpec((GATE_ROWS, GATE_LANES), lambda i: (nb - 1 - i, 0)), pl.BlockSpec((1, GATE_LANES), lambda i: (0, 0)),
                  pl.BlockSpec((GATE_LANES, GATE_ROWS), lambda i: (0, nb - 1 - i)),
                  pl.BlockSpec((GATE_ROWS, GATE_LANES), lambda i: (nb - 1 - i, 0))],
        out_specs=[pl.BlockSpec((GATE_ROWS, GATE_LANES), lambda i: (nb - 1 - i, 0)), pl.BlockSpec((1, GATE_LANES), lambda i: (0, 0))],
        out_shape=[jax.ShapeDtypeStruct((S, GATE_LANES), BF), jax.ShapeDtypeStruct((1, GATE_LANES), F32)],
        scratch_shapes=[pltpu.VMEM((1, GATE_LANES), F32)], name=name, compiler_params=_params("arbitrary"),
    )(z, bf, dcT, dcq)


def _rope_tables(S):
    half = ROT_DIM // 2
    inv_freq = ROPE_THETA ** (-jnp.arange(half, dtype=F32) * 2.0 / ROT_DIM)
    ang = jnp.arange(S, dtype=F32)[:, None] * inv_freq[None, :]
    cos, sin = jnp.cos(ang), jnp.sin(ang)
    zero = jnp.zeros((S, HEAD_DIM - ROT_DIM), F32)
    zh = jnp.zeros((S, half), F32)
    cos_h = jnp.concatenate([cos, cos, jnp.ones_like(zero)], axis=1)
    sa_h = jnp.concatenate([-sin, zh, zero], axis=1)
    sb_h = jnp.concatenate([zh, sin, zero], axis=1)
    two = lambda t: jnp.concatenate([t, t], axis=1)
    return two(cos_h), two(sa_h), two(sb_h)


def _ffn_fwd(h, norm, w_gu, w_down, tag):
    n = _rms_fwd(h, norm, name=f"ffn{tag}_norm")
    gu = _mm_nn(n, w_gu, tm=MM_ROWS, tn=1408, out_dtype=BF, name=f"ffn{tag}_gu")
    act = _swiglu_fwd(gu, name=f"ffn{tag}_act")
    out = _mm_nn(act, w_down, tm=MM_ROWS // 2, tn=1024, out_dtype=F32, name=f"ffn{tag}_down", resid=h)
    return out, (h, n, gu, act)


def _ffn_bwd(dh, dhb, saved, norm, w_gu, w_down, tag, ride=None):
    h, n, gu, act = saved
    dact = _mm_nt(dhb, w_down, tm=MM_ROWS, to=1408, tn=1024, out_dtype=BF, name=f"ffn{tag}_dact")
    dw_down = _mm_tn(act, dhb, tk=1408, tn=1024, tm=MM_ROWS, out_dtype=BF, name=f"ffn{tag}_dwdown")
    dgu = _swiglu_bwd(gu, dact, name=f"ffn{tag}_dgu")
    dn_call = lambda guest: _mm_nt(dgu, w_gu, tm=MM_ROWS, to=1024, tn=1408, out_dtype=F32, name=f"ffn{tag}_dn", guest=guest)
    dn = dn_call(None) if ride is None else ride(dn_call)
    dw_gu = _mm_tn(n, dgu, tk=1024, tn=1408, tm=MM_ROWS, out_dtype=BF, name=f"ffn{tag}_dwgu")
    dx, dxb, dg = _rms_bwd(h, norm, dn, dh, name=f"ffn{tag}_dnorm")
    return dx, dxb, dg, dw_gu, dw_down


def _local_step(x, tgt, w, mats, fetch, exchange):
    S = x.shape[0]
    rope_f = _rope_tables(S)
    rope_b = (rope_f[0], -rope_f[1], -rope_f[2])
    g, partial, landed = {}, {}, {}
    w = dict(w, ffn_w_gu={}, ffn_w_down={})

    def bring(call, indices):
        bufs = [mats[wi] for wi in indices]
        if fetch is None:
            return call(None), bufs
        return call(fetch(indices, bufs))

    def ride(call, indices):
        guest = exchange(indices, [partial[wi] for wi in indices]) if indices else None
        res = call(guest)
        if guest is None:
            return res
        res, outs = res
        landed.update(zip(indices, outs))
        return res

    n0 = _rms_fwd(x, w["a_norm"], name="a_norm")
    dils = [d for _, d in DILATED_PATTERNS]
    projs, (w["ffn_w_gu"][0], w["ffn_w_down"][0]) = bring(
        lambda guest: _mm_nn(n0, w["a_w_in"], tm=MM_ROWS, tn=1024, out_dtype=BF, name="a_proj", rope=rope_f, guest=guest,
                             groups=dils), [4, 6])
    block = lambda t, dil: (lambda r: t * dil + r)
    o_parts, lse_parts = [], []
    for gi, (window, dil) in enumerate(DILATED_PATTERNS):
        pv = projs[gi]
        attend = lambda guest: _band_fwd(pv, pv, pv, block(0, dil), block(1, dil), block(2, dil), dil=dil, T=128,
                                         window=window // dil, name=f"a_attn{gi}", guest=guest)
        if gi == 0:
            (o_g, lse_g), (w["a_w_out"],) = bring(attend, [1])
        elif gi == 1:
            (o_g, lse_g), (b_in,) = bring(attend, [2])
        else:
            (o_g, lse_g), (w["b_w_out"],) = bring(attend, [3])
        o_parts.append(o_g)
        lse_parts.append(lse_g)
    b_in = b_in.transpose(1, 0, 2).reshape(D_MODEL, -1)
    w["b_w_qkv"] = b_in[:, :QKV_COLS]
    w["b_w_f"] = jnp.pad(b_in[:, QKV_COLS:], ((0, 0), (0, GATE_LANES + QKV_COLS - b_in.shape[1])))
    mixed = _combine_groups(o_parts, lse_parts, dils, name="a_combine")
    o_a = mixed[1][0]
    h1 = _mm_nn(o_a, w["a_w_out"], tm=MM_ROWS, tn=1024, out_dtype=F32, name="a_out", resid=x)
    h2, ffn0 = _ffn_fwd(h1, w["ffn_norm"][0:1], w["ffn_w_gu"][0], w["ffn_w_down"][0], 0)

    n2 = _rms_fwd(h2, w["b_norm"], name="b_norm")
    qkv = _mm_nn(n2, w["b_w_qkv"], tm=MM_ROWS, tn=1024, out_dtype=BF, name="b_proj")
    zf = _mm_nn(n2, w["b_w_f"], tm=MM_ROWS, tn=GATE_LANES, out_dtype=F32, name="b_gate_proj")
    cT = _gate_fwd(zf, w["b_f"], name="b_gate")
    (o_b, lse_b), (w["ffn_w_gu"][1], w["ffn_w_down"][1]) = bring(lambda guest: _fox_fwd(qkv, cT, name="b_attn", guest=guest), [5, 7])
    h3 = _mm_nn(o_b, w["b_w_out"], tm=MM_ROWS, tn=1024, out_dtype=F32, name="b_out", resid=h2)
    h4, ffn1 = _ffn_fwd(h3, w["ffn_norm"][1:2], w["ffn_w_gu"][1], w["ffn_w_down"][1], 1)

    loss, dh4, dh4b, g["final_norm"] = _loss_head(h4, w["final_norm"], tgt, name="loss_head")

    dh3, dh3b, dg_f1, partial[5], partial[7] = _ffn_bwd(dh4, dh4b, ffn1, w["ffn_norm"][1:2], w["ffn_w_gu"][1], w["ffn_w_down"][1], 1)

    do_b = _mm_nt(dh3b, w["b_w_out"], tm=MM_ROWS, to=1024, tn=1024, out_dtype=BF, name="b_do")
    partial[3] = _mm_tn(o_b, dh3b, tk=1024, tn=1024, tm=MM_ROWS, out_dtype=BF, name="b_dwout")
    dq, dk, dv, dcT, dcq = ride(lambda guest: _fox_bwd(qkv, cT, do_b, o_b, lse_b, name="b_attn_bwd", guest=guest), [5, 7, 3])
    dz, g["b_f"] = _gate_bwd(zf, w["b_f"], dcT, dcq, name="b_gate_bwd")
    dqkv = _assemble([dq, dk, dv], [False] * 3, None, [1] * 3, name="b_dproj")
    dn2 = _mm_nt(dz, w["b_w_f"], tm=MM_ROWS, to=1024, tn=GATE_LANES, out_dtype=F32, name="b_dn_gate")
    dn2 = _mm_nt(dqkv, w["b_w_qkv"], tm=MM_ROWS, to=1024, tn=1024, out_dtype=F32, name="b_dn", add=dn2)
    g_qkv = _mm_tn(n2, dqkv, tk=1024, tn=1024, tm=MM_ROWS, out_dtype=BF, name="b_dwqkv")
    g_f = _mm_tn(n2, dz, tk=1024, tn=GATE_LANES, tm=MM_ROWS, out_dtype=BF, name="b_dwf")
    g_b_in = jnp.concatenate([g_qkv, g_f[:, :N_HEADS]], axis=1)
    partial[2] = g_b_in.reshape(D_MODEL, N_CHIPS, -1).transpose(1, 0, 2)
    dh2, dh2b, g["b_norm"] = _rms_bwd(h2, w["b_norm"], dn2, dh3, name="b_dnorm")

    dh1, dh1b, dg_f0, partial[4], partial[6] = _ffn_bwd(dh2, dh2b, ffn0, w["ffn_norm"][0:1], w["ffn_w_gu"][0], w["ffn_w_down"][0], 0,
                                                      ride=lambda call: ride(call, [2]))
    g["ffn_norm"] = jnp.concatenate([dg_f0, dg_f1], axis=0)

    views = tuple(sorted(set(dils)))
    do_a = dict(zip(views, _mm_nt(dh1b, w["a_w_out"], tm=MM_ROWS, to=1024, tn=1024, out_dtype=BF, name="a_do", views=views)))
    partial[1] = _mm_tn(o_a, dh1b, tk=1024, tn=1024, tm=MM_ROWS, out_dtype=BF, name="a_dwout")
    riders = {0: [4], 1: [6, 1], 2: []}
    parts = []
    for gi, (window, dil) in enumerate(DILATED_PATTERNS):
        pv = projs[gi]
        res = ride(lambda guest: _band_bwd(pv, pv, pv, block(0, dil), block(1, dil), block(2, dil), do_a[dil],
                                           mixed[dil][0], mixed[dil][1], dil=dil, T=128,
                                           window=window // dil, name=f"a_attn_bwd{gi}", guest=guest), riders[gi])
        parts += list(res)
    dproj = _assemble(parts, [True, True, False] * 3, rope_b, [d for _, d in DILATED_PATTERNS for _ in range(3)], name="a_dproj")
    partial[0] = _mm_tn(n0, dproj, tk=1024, tn=1024, tm=MM_ROWS, out_dtype=BF, name="a_dwin")
    dn0 = ride(lambda guest: _mm_nt(dproj, w["a_w_in"], tm=MM_ROWS, to=1024, tn=1024, out_dtype=F32, name="a_dn", guest=guest), [0])
    dx, _, g["a_norm"] = _rms_bwd(x, w["a_norm"], dn0, dh1, name="a_dnorm")
    return loss, dx, g, partial, landed


ANY = pl.BlockSpec(memory_space=pl.ANY)


def _place():
    x, y, c = lax.axis_index("x"), lax.axis_index("y"), lax.axis_index("c")
    chips = [(1 - x, y), (x, 1 - y), (1 - x, 1 - y)]
    return x, y, c, chips


def _shard_slice(ref, kind, rows, cols, s, half):
    hr = rows // 2
    if kind == "col":
        return ref.at[pl.ds(half * hr, hr), pl.ds(pl.multiple_of(s * cols, 128), cols)]
    if kind == "row":
        return ref.at[pl.ds(pl.multiple_of(s * rows + half * hr, 16), hr), :]
    return ref.at[s, pl.ds(half * hr, hr), :]


def _whole_shape(kind, rows, cols):
    return {"col": (rows, N_CHIPS * cols), "row": (N_CHIPS * rows, cols), "stack": (N_CHIPS, rows, cols)}[kind]


def _own_block(kind, rows, tr, cols):
    per = rows // tr

    def spec(half_rows):
        off = (lambda p: 0) if half_rows is None else (lambda p: p[1] * (half_rows // tr))
        if kind == "col":
            return pl.BlockSpec((tr, cols), lambda i, p: (off(p) + i, p[0]))
        if kind == "row":
            return pl.BlockSpec((tr, cols), lambda i, p: (p[0] * per + off(p) + i, 0))
        return pl.BlockSpec((None, tr, cols), lambda i, p: (p[0], off(p) + i, 0))
    return spec


def _place_shard(shards, layer, kind, place, *, name):
    _, rows, cols = shards.shape
    tr = 256 if rows % 256 == 0 else rows // 2

    def body(p_ref, s_ref, o_ref):
        o_ref[...] = s_ref[...].astype(BF)

    return pl.pallas_call(
        body,
        grid_spec=pltpu.PrefetchScalarGridSpec(
            num_scalar_prefetch=1, grid=(rows // tr,),
            in_specs=[pl.BlockSpec((None, tr, cols), lambda i, p: (layer, i, 0))],
            out_specs=_own_block(kind, rows, tr, cols)(None)),
        out_shape=jax.ShapeDtypeStruct(_whole_shape(kind, rows, cols), BF),
        name=name, compiler_params=_params("arbitrary"),
    )(place, shards)


def _gather_weights(placed, kinds, dims):
    nw = len(placed)

    def body(*refs):
        dst = refs[nw:2 * nw]
        send_sems, recv_sems = refs[2 * nw:]
        x, y, c, chips = _place()
        me = 2 * x + y
        sibling = (x, y, 1 - c)

        def copy(wi, k, s, half, to):
            p = _shard_slice(dst[wi], kinds[wi], dims[wi][0], dims[wi][1], s, half)
            return pltpu.make_async_remote_copy(src_ref=p, dst_ref=p, send_sem=send_sems.at[wi * 6 + k],
                                                recv_sem=recv_sems.at[wi * 6 + k], device_id=to, device_id_type=MESH)

        first, passed = [], []
        for wi in range(nw):
            for j, chip in enumerate(chips):
                cp = copy(wi, j, me, c, (*chip, c))
                cp.start()
                first.append(cp)
        for wi in range(nw):
            for j, chip in enumerate(chips):
                s = 2 * chip[0] + chip[1]
                copy(wi, j, s, c, (x, y, c)).wait_recv()
                cp = copy(wi, 3 + j, s, c, sibling)
                cp.start()
                passed.append(cp)
        for wi in range(nw):
            for j, chip in enumerate(chips):
                s = 2 * chip[0] + chip[1]
                copy(wi, 3 + j, s, 1 - c, (x, y, c)).wait_recv()
        for cp in first + passed:
            cp.wait_send()

    return pl.pallas_call(
        body, in_specs=[ANY] * nw, out_specs=[ANY] * nw,
        out_shape=[jax.ShapeDtypeStruct(p.shape, p.dtype) for p in placed],
        input_output_aliases={wi: wi for wi in range(nw)},
        scratch_shapes=[pltpu.SemaphoreType.DMA((nw * 6,)), pltpu.SemaphoreType.DMA((nw * 6,))],
        name="gather_weights",
    )(*placed)


def _fetch_guest(placed, kinds, dims):
    nw = len(placed)

    def copies(dst, send_sems, recv_sems, incoming):
        x, y, c, chips = _place()
        out = []
        for wi in range(nw):
            for j, chip in enumerate(chips):
                s = 2 * chip[0] + chip[1] if incoming else 2 * x + y
                to = (x, y, c) if incoming else (*chip, c)
                for half in range(2):
                    p = _shard_slice(dst[wi], kinds[wi], dims[wi][0], dims[wi][1], s, half)
                    k = wi * 6 + 2 * j + half
                    out.append(pltpu.make_async_remote_copy(src_ref=p, dst_ref=p, send_sem=send_sems.at[k],
                                                            recv_sem=recv_sems.at[k], device_id=to, device_id_type=MESH))
        return out

    def start(src, dst, sems):
        for cp in copies(dst, sems[0], sems[1], False):
            cp.start()

    def finish(src, dst, sems):
        for cp in copies(dst, sems[0], sems[1], True):
            cp.wait_recv()
        for cp in copies(dst, sems[0], sems[1], False):
            cp.wait_send()

    return dict(args=list(placed), out_shape=[jax.ShapeDtypeStruct(p.shape, p.dtype) for p in placed],
                scratch=[pltpu.SemaphoreType.DMA((nw * 6,)), pltpu.SemaphoreType.DMA((nw * 6,))],
                start=start, finish=finish, in_place=True)


def _scatter_guest(partials, kinds, dims):
    nw = len(partials)

    def copies(src, send_sems, recv_sems, dst):
        x, y, c, chips = _place()
        me = 2 * x + y
        out = []
        for wi in range(nw):
            rows, cols = dims[wi]

            def part(s, half, wi=wi, rows=rows, cols=cols):
                return _shard_slice(src[wi], kinds[wi], rows, cols, s, half)

            for j, chip in enumerate(chips):
                s = 2 * chip[0] + chip[1]
                for half in range(2):
                    slot = 2 * j + (c if half == 0 else 1 - c)
                    out.append(pltpu.make_async_remote_copy(
                        src_ref=part(s, half), dst_ref=dst[wi].at[slot],
                        send_sem=send_sems.at[wi * 7 + 2 * j + half], recv_sem=recv_sems.at[wi * 7 + slot],
                        device_id=(*chip, half), device_id_type=MESH))
            out.append(pltpu.make_async_remote_copy(
                src_ref=part(me, 1 - c), dst_ref=dst[wi].at[6],
                send_sem=send_sems.at[wi * 7 + 6], recv_sem=recv_sems.at[wi * 7 + 6],
                device_id=(x, y, 1 - c), device_id_type=MESH))
        return out

    def start(src, dst, sems):
        for cp in copies(src, sems[0], sems[1], dst):
            cp.start()

    def finish(src, dst, sems):
        x, y, c, _ = _place()
        for wi in range(nw):
            for slot in range(7):
                pltpu.make_async_remote_copy(
                    src_ref=dst[wi].at[slot], dst_ref=dst[wi].at[slot],
                    send_sem=sems[0].at[wi * 7 + slot], recv_sem=sems[1].at[wi * 7 + slot],
                    device_id=(x, y, c), device_id_type=MESH).wait_recv()
        for cp in copies(src, sems[0], sems[1], dst):
            cp.wait_send()

    return dict(args=list(partials), out_shape=[jax.ShapeDtypeStruct((7, d[0] // 2, d[1]), BF) for d in dims],
                scratch=[pltpu.SemaphoreType.DMA((nw * 7,)), pltpu.SemaphoreType.DMA((nw * 7,))],
                start=start, finish=finish)


def _sum_slots(slots, partial, kind, dims, place, *, name, into=None, layer=None, n_layers=1):
    rows, cols = dims
    hr = rows // 2
    tr = hr if 8 * hr * cols * 2 <= 6 * 1024 * 1024 else 128
    assert hr % tr == 0

    def body(p_ref, b_ref, own_ref, *rest):
        o_ref = rest[-1]
        acc = own_ref[...].astype(F32)
        for k in range(7):
            acc = acc + b_ref[k].astype(F32)
        o_ref[...] = acc

    half = lambda p: p[1] * (hr // tr)
    if n_layers == 1:
        out_spec = pl.BlockSpec((tr, cols), lambda i, p: (half(p) + i, 0))
        out_shape = jax.ShapeDtypeStruct((rows, cols), F32)
    else:
        out_spec = pl.BlockSpec((None, tr, cols), lambda i, p: (layer, half(p) + i, 0))
        out_shape = jax.ShapeDtypeStruct((n_layers, rows, cols), F32)
    in_specs = [pl.BlockSpec((7, tr, cols), lambda i, p: (0, i, 0)), _own_block(kind, rows, tr, cols)(hr)]
    args = [place, slots, partial]
    aliases = {}
    if into is not None:
        in_specs.append(ANY)
        args.append(into)
        aliases = {3: 0}
    return pl.pallas_call(
        body,
        grid_spec=pltpu.PrefetchScalarGridSpec(num_scalar_prefetch=1, grid=(hr // tr,), in_specs=in_specs, out_specs=out_spec),
        out_shape=out_shape, input_output_aliases=aliases, name=name, compiler_params=_params("arbitrary"),
    )(*args)


def _pair_exchange(bufs, members):
    nw = len(members)

    def body(*refs):
        dst = refs[len(bufs):2 * len(bufs)]
        send_sems, recv_sems = refs[2 * len(bufs):]
        x, y, c, _ = _place()

        def rows_of(wi, half):
            bi, l = members[wi]
            ref = dst[bi] if l is None else dst[bi].at[l]
            hr = ref.shape[0] // 2
            return ref.at[pl.ds(pl.multiple_of(half * hr, 8), hr), :]

        def copy(wi, half, to):
            p = rows_of(wi, half)
            return pltpu.make_async_remote_copy(src_ref=p, dst_ref=p, send_sem=send_sems.at[wi], recv_sem=recv_sems.at[wi],
                                                device_id=to, device_id_type=MESH)

        sent = []
        for wi in range(nw):
            cp = copy(wi, c, (x, y, 1 - c))
            cp.start()
            sent.append(cp)
        for wi in range(nw):
            copy(wi, 1 - c, (x, y, c)).wait_recv()
        for cp in sent:
            cp.wait_send()

    return pl.pallas_call(
        body, in_specs=[ANY] * len(bufs), out_specs=[ANY] * len(bufs),
        out_shape=[jax.ShapeDtypeStruct(b.shape, b.dtype) for b in bufs],
        input_output_aliases={i: i for i in range(len(bufs))},
        scratch_shapes=[pltpu.SemaphoreType.DMA((nw,)), pltpu.SemaphoreType.DMA((nw,))],
        name="pair_exchange",
    )(*bufs)


SMALL_ROWS = 8


def _allreduce_small(v, *, name):
    assert v.shape == (SMALL_ROWS, D_MODEL)

    def body(v_ref, o_ref, buf, send_sems, recv_sems):
        x, y, c, _ = _place()
        me = 4 * x + 2 * y + c
        buf[me] = v_ref[...]
        sent = []
        for k in range(1, 8):
            bx, by, bc = (k >> 2) & 1, (k >> 1) & 1, k & 1
            peer = (1 - x if bx else x, 1 - y if by else y, 1 - c if bc else c)
            cp = pltpu.make_async_remote_copy(src_ref=v_ref, dst_ref=buf.at[me], send_sem=send_sems.at[k - 1],
                                              recv_sem=recv_sems.at[k - 1], device_id=peer, device_id_type=MESH)
            cp.start()
            sent.append(cp)
        for k in range(1, 8):
            bx, by, bc = (k >> 2) & 1, (k >> 1) & 1, k & 1
            peer = 4 * (1 - x if bx else x) + 2 * (1 - y if by else y) + (1 - c if bc else c)
            pltpu.make_async_remote_copy(src_ref=v_ref, dst_ref=buf.at[peer], send_sem=send_sems.at[k - 1],
                                         recv_sem=recv_sems.at[k - 1], device_id=(x, y, c), device_id_type=MESH).wait_recv()
        for cp in sent:
            cp.wait_send()
        acc = buf[0]
        for d in range(1, 8):
            acc = acc + buf[d]
        o_ref[...] = acc

    vmem = pl.BlockSpec(memory_space=pltpu.VMEM)
    return pl.pallas_call(
        body, in_specs=[vmem], out_specs=vmem, out_shape=jax.ShapeDtypeStruct(v.shape, F32),
        scratch_shapes=[pltpu.VMEM((8,) + v.shape, F32), pltpu.SemaphoreType.DMA((7,)), pltpu.SemaphoreType.DMA((7,))],
        name=name,
    )(v)


def _adamw(w, g, m, v, *, name):
    R, C = w.shape
    tr = R
    if R * C * 4 > 1024 * 1024:
        tr = max(t for t in range(8, R, 8) if R % t == 0 and t * C * 4 <= 1024 * 1024)

    def body(w_ref, g_ref, m_ref, v_ref, d_ref, m2_ref, v2_ref):
        gg = g_ref[...]
        m2 = ADAM_B1 * m_ref[...] + (1.0 - ADAM_B1) * gg
        v2 = ADAM_B2 * v_ref[...] + (1.0 - ADAM_B2) * jnp.square(gg)
        m_hat = m2 / (1.0 - ADAM_B1 ** ADAM_STEP)
        v_hat = v2 / (1.0 - ADAM_B2 ** ADAM_STEP)
        d_ref[...] = -ADAM_LR * (m_hat / (jnp.sqrt(v_hat) + ADAM_EPS) + ADAM_WD * w_ref[...])
        m2_ref[...] = m2
        v2_ref[...] = v2

    blk = pl.BlockSpec((tr, C), lambda i: (i, 0))
    out = jax.ShapeDtypeStruct((R, C), F32)
    return pl.pallas_call(
        body, grid=(R // tr,), in_specs=[blk] * 4, out_specs=[blk] * 3, out_shape=[out] * 3,
        name=name, compiler_params=_params("parallel"),
    )(w, g, m, v)


WEIGHT_ORDER = ("a_norm", "a_w_in", "a_w_out", "b_norm", "b_w_in", "b_f", "b_w_out", "ffn_norm", "ffn_w_gu",
                "ffn_w_down", "final_norm")
MATRICES = (("a_w_in", 0, "col"), ("a_w_out", 0, "row"), ("b_w_in", 0, "stack"), ("b_w_out", 0, "row"),
            ("ffn_w_gu", 0, "col"), ("ffn_w_gu", 1, "col"), ("ffn_w_down", 0, "row"), ("ffn_w_down", 1, "row"))
MATRIX_GROUPS = ([0], [1], [2], [3], [4, 5], [6, 7])
GROUP_NAMES = ("a_w_in", "a_w_out", "b_w_in", "b_w_out", "ffn_w_gu", "ffn_w_down")
QKV_COLS = 3 * N_HEADS * HEAD_DIM


def kernel(x, a_norm, a_w_in, a_w_out, b_norm, b_w_in, b_f, b_w_out, ffn_norm, ffn_w_gu, ffn_w_down, final_norm, loss_target, m_a_norm, m_a_w_in, m_a_w_out, m_b_norm, m_b_w_in, m_b_f, m_b_w_out, m_ffn_norm, m_ffn_w_gu, m_ffn_w_down, m_final_norm, v_a_norm, v_a_w_in, v_a_w_out, v_b_norm, v_b_w_in, v_b_f, v_b_w_out, v_ffn_norm, v_ffn_w_gu, v_ffn_w_down, v_final_norm):
    given = dict(a_norm=a_norm, a_w_in=a_w_in, a_w_out=a_w_out, b_norm=b_norm, b_w_in=b_w_in, b_f=b_f, b_w_out=b_w_out,
                 ffn_norm=ffn_norm, ffn_w_gu=ffn_w_gu, ffn_w_down=ffn_w_down, final_norm=final_norm)
    mom_m = dict(a_norm=m_a_norm, a_w_in=m_a_w_in, a_w_out=m_a_w_out, b_norm=m_b_norm, b_w_in=m_b_w_in, b_f=m_b_f,
                 b_w_out=m_b_w_out, ffn_norm=m_ffn_norm, ffn_w_gu=m_ffn_w_gu, ffn_w_down=m_ffn_w_down, final_norm=m_final_norm)
    mom_v = dict(a_norm=v_a_norm, a_w_in=v_a_w_in, a_w_out=v_a_w_out, b_norm=v_b_norm, b_w_in=v_b_w_in, b_f=v_b_f,
                 b_w_out=v_b_w_out, ffn_norm=v_ffn_norm, ffn_w_gu=v_ffn_w_gu, ffn_w_down=v_ffn_w_down, final_norm=v_final_norm)
    chip = 2 * lax.axis_index("x") + lax.axis_index("y")
    core = lax.axis_index("c")
    bn_cols = b_norm.shape[1]

    placed = lax.dynamic_update_slice(jnp.zeros((SMALL_ROWS, D_MODEL), F32), b_norm, (0, chip * bn_cols))
    placed = placed * (core == 0).astype(F32)
    b_norm_full = _allreduce_small(placed, name="gather_b_norm")[0:1]

    place = jnp.stack([chip, core]).astype(jnp.int32)
    kinds = [k for _, _, k in MATRICES]
    dims = [given[n].shape[1:] for n, _, _ in MATRICES]
    placed = [_place_shard(given[n], l, k, place, name=f"place_{n}{l}") for n, l, k in MATRICES]
    first = _gather_weights(placed[:1], kinds[:1], dims[:1])
    mats = dict(enumerate(list(first) + placed[1:]))
    gate_cols = b_f.shape[1]
    w = dict(a_norm=a_norm, a_w_in=mats[0], b_norm=b_norm_full,
             b_f=jnp.pad(b_f, ((0, 0), (0, GATE_LANES - gate_cols))), ffn_norm=ffn_norm,
             final_norm=final_norm.reshape(1, D_MODEL))

    def fetch(indices, bufs):
        return _fetch_guest(bufs, [kinds[i] for i in indices], [dims[i] for i in indices])

    def exchange(indices, parts):
        return _scatter_guest(parts, [kinds[i] for i in indices], [dims[i] for i in indices])

    loss, dx, g, partials, slots = _local_step(x[0], loss_target[0], w, mats, fetch, exchange)
    bufs, members = [], []
    for group in MATRIX_GROUPS:
        buf = None
        for l, wi in enumerate(group):
            n = MATRICES[wi][0]
            buf = _sum_slots(slots[wi], partials[wi], kinds[wi], dims[wi], place, name=f"sum_{n}{l}", into=buf,
                             layer=l, n_layers=len(group))
            members.append((len(bufs), l if len(group) > 1 else None))
        bufs.append(buf)
    reduced = dict(zip(GROUP_NAMES, _pair_exchange(bufs, members)))

    small = jnp.concatenate([g["a_norm"], g["b_norm"], g["ffn_norm"], g["final_norm"],
                             jnp.pad(g["b_f"], ((0, 0), (0, D_MODEL - GATE_LANES))),
                             jnp.zeros((SMALL_ROWS - 6, D_MODEL), F32)], axis=0)
    small = _allreduce_small(small, name="allreduce_small")
    grads = dict(reduced)
    grads["a_norm"] = small[0:1]
    grads["b_norm"] = lax.dynamic_slice(small, (1, chip * bn_cols), (1, bn_cols))
    grads["ffn_norm"] = small[2:4]
    grads["final_norm"] = small[4]
    grads["b_f"] = small[5:6, :gate_cols]

    out_g, out_d, out_m, out_v = [], [], [], []
    for n in WEIGHT_ORDER:
        shape = given[n].shape
        two_d = (1, shape[0]) if len(shape) == 1 else (-1, shape[-1])
        d, m2, v2 = _adamw(given[n].reshape(two_d), grads[n].reshape(two_d), mom_m[n].reshape(two_d),
                           mom_v[n].reshape(two_d), name=f"adamw_{n}")
        out_g.append(grads[n].reshape(shape))
        out_d.append(d.reshape(shape))
        out_m.append(m2.reshape(shape))
        out_v.append(v2.reshape(shape))

    total = lax.psum(loss[0, 0], MESH_AXES)
    return (total, dx[None], *out_g, *out_d, *out_m, *out_v)
```

```python
import functools

import jax
import jax.numpy as jnp
from jax import lax
from jax.experimental import pallas as pl
from jax.experimental.pallas import tpu as pltpu

F32 = jnp.float32
BF = jnp.bfloat16

D_MODEL = 1024
N_HEADS = 16
HEAD_DIM = 64
D_FF = 2816
DILATED_PATTERNS = ((128, 1), (512, 4), (2048, 16))
ROT_DIM = 16
ROPE_THETA = 500000.0
RMS_EPS = 1e-6
NEG_INF = -1e30
ATTN_SCALE = HEAD_DIM ** -0.5
GATE_LANES = 128
N_CHIPS = 4
MESH_AXES = ("x", "y", "c")
MESH = pl.DeviceIdType.MESH

ADAM_LR = 0.001
ADAM_B1 = 0.9
ADAM_B2 = 0.999
ADAM_EPS = 1e-08
ADAM_WD = 0.01
ADAM_STEP = 10

VMEM_LIMIT_BYTES = 56 * 1024 * 1024


def _params(*sem):
    return pltpu.CompilerParams(dimension_semantics=sem, vmem_limit_bytes=VMEM_LIMIT_BYTES)


def _hosted_call(body, *, grid, in_specs, out_specs, out_shape, scratch_shapes, args, name, guest=None, schedule=()):
    params = _params(*(["arbitrary"] * len(grid)))
    ns = len(schedule)

    def call(kernel, in_specs, out_specs, out_shape, scratch_shapes, aliases, args):
        spec = pltpu.PrefetchScalarGridSpec(num_scalar_prefetch=ns, grid=grid, in_specs=in_specs, out_specs=out_specs,
                                            scratch_shapes=scratch_shapes)
        return pl.pallas_call(kernel, grid_spec=spec, out_shape=out_shape, input_output_aliases=aliases, name=name,
                              compiler_params=params)(*schedule, *args)

    if guest is None:
        return call(body, in_specs, out_specs, out_shape, scratch_shapes, {}, args)
    n_in, n_out, n_scr = ns + len(in_specs), len(out_specs), len(scratch_shapes)
    g_in, g_out = len(guest["args"]), len(guest["out_shape"])
    any_spec = pl.BlockSpec(memory_space=pl.ANY)

    def wrapped(*refs):
        i1 = n_in + g_in
        o1 = i1 + n_out
        o2 = o1 + g_out
        s1 = o2 + n_scr
        guest_refs = (refs[n_in:i1], refs[o1:o2], refs[s1:])
        ids = [pl.program_id(d) for d in range(len(grid))]
        first = functools.reduce(jnp.logical_and, [i == 0 for i in ids])
        last = functools.reduce(jnp.logical_and, [i == g - 1 for i, g in zip(ids, grid)])

        @pl.when(first)
        def _():
            guest["start"](*guest_refs)

        body(*refs[:n_in], *refs[i1:o1], *refs[o2:s1])

        @pl.when(last)
        def _():
            guest["finish"](*guest_refs)

    aliases = {n_in + k: n_out + k for k in range(g_in)} if guest.get("in_place") else {}
    return call(wrapped, list(in_specs) + [any_spec] * g_in, list(out_specs) + [any_spec] * g_out,
                list(out_shape) + list(guest["out_shape"]), list(scratch_shapes) + list(guest["scratch"]), aliases,
                list(args) + list(guest["args"]))


def _rope_rotate(t, cos, sin_a, sin_b):
    outs = []
    for cidx in range(t.shape[1] // 128):
        tc = t[:, cidx * 128:(cidx + 1) * 128]
        outs.append(tc * cos + pltpu.roll(tc, 120, 1) * sin_a + pltpu.roll(tc, 8, 1) * sin_b)
    return jnp.concatenate(outs, axis=1)


def _mm_nn(a, b, *, tm, tn, out_dtype, name, resid=None, rope=None, guest=None, groups=None):
    M, K = a.shape
    N = b.shape[1]
    assert M % tm == 0 and N % tn == 0 and b.shape[0] == K
    n_in = 2 + (resid is not None) + (3 if rope is not None else 0)
    if groups is not None:
        assert rope is not None and N == 3 * tn * len(groups)

    def body(*refs):
        a_ref, b_ref = refs[0], refs[1]
        o_ref = refs[n_in]
        acc = jnp.dot(a_ref[...], b_ref[...], preferred_element_type=F32)
        if resid is not None:
            acc = acc + refs[2][...]
        if groups is not None:
            cos_ref, sa_ref, sb_ref = refs[n_in - 3:n_in]
            j = pl.program_id(1)
            for g, d in enumerate(groups):
                for is_v in (False, True):
                    @pl.when(jnp.logical_and(j // 3 == g, (j % 3 == 2) == is_v))
                    def _(g=g, d=d, is_v=is_v):
                        val = acc if is_v else _rope_rotate(acc, cos_ref[...], sa_ref[...], sb_ref[...])
                        if d == 1:
                            refs[n_in + g][...] = val.astype(out_dtype)
                        else:
                            _to_view(val, refs[-1], refs[n_in + g], d, tn)
        elif rope is not None:
            cos_ref, sa_ref, sb_ref = refs[n_in - 3:n_in]
            j = pl.program_id(1)

            @pl.when(j % 3 != 2)
            def _():
                o_ref[...] = _rope_rotate(acc, cos_ref[...], sa_ref[...], sb_ref[...]).astype(out_dtype)

            @pl.when(j % 3 == 2)
            def _():
                o_ref[...] = acc.astype(out_dtype)
        else:
            o_ref[...] = acc.astype(out_dtype)

    in_specs = [pl.BlockSpec((tm, K), lambda i, j: (i, 0)), pl.BlockSpec((K, tn), lambda i, j: (0, j))]
    args = [a, b]
    if resid is not None:
        in_specs.append(pl.BlockSpec((tm, tn), lambda i, j: (i, j)))
        args.append(resid)
    if rope is not None:
        assert tn == 1024
        for t in rope:
            in_specs.append(pl.BlockSpec((tm, 128), lambda i, j: (i, 0)))
            args.append(t)
    if groups is None:
        out_specs = [pl.BlockSpec((tm, tn), lambda i, j: (i, j))]
        out_shape = [jax.ShapeDtypeStruct((M, N), out_dtype)]
        scratch = []
    else:
        out_specs = [pl.BlockSpec((tm // d, d * tn), lambda i, j, g=g: (i, jnp.clip(j - 3 * g, 0, 2)))
                     for g, d in enumerate(groups)]
        out_shape = [jax.ShapeDtypeStruct((M // d, d * 3 * tn), out_dtype) for d in groups]
        scratch = [pltpu.VMEM((tn // 128, tm, 128), F32)]
    outs = _hosted_call(body, grid=(M // tm, N // tn), in_specs=in_specs, out_specs=out_specs, out_shape=out_shape,
                        scratch_shapes=scratch, args=args, name=name, guest=guest)
    nout = len(out_shape)
    res = outs[0] if groups is None else list(outs[:nout])
    return res if guest is None else (res, outs[nout:])


def _mm_nt(a, b, *, tm, to, tn, out_dtype, name, add=None, guest=None, views=(1,)):
    M, N = a.shape
    O = b.shape[0]
    assert M % tm == 0 and O % to == 0 and N % tn == 0 and b.shape[1] == N
    nk = N // tn

    def body(*refs):
        a_ref, b_ref = refs[0], refs[1]
        n_in = 2 + (add is not None)
        o_refs = refs[n_in:n_in + len(views)]
        acc_ref = refs[n_in + len(views)]
        k = pl.program_id(2)

        @pl.when(k == 0)
        def _():
            if add is not None:
                acc_ref[...] = refs[2][...]
            else:
                acc_ref[...] = jnp.zeros_like(acc_ref)

        acc_ref[...] += lax.dot_general(a_ref[...], b_ref[...], (((1,), (1,)), ((), ())),
                                        preferred_element_type=F32)

        @pl.when(k == nk - 1)
        def _():
            for o_ref, d in zip(o_refs, views):
                if d == 1:
                    o_ref[...] = acc_ref[...].astype(out_dtype)
                else:
                    _to_view(acc_ref[...], refs[-1], o_ref, d, to)

    in_specs = [pl.BlockSpec((tm, tn), lambda i, j, k: (i, k)), pl.BlockSpec((to, tn), lambda i, j, k: (j, k))]
    args = [a, b]
    if add is not None:
        in_specs.append(pl.BlockSpec((tm, to), lambda i, j, k: (i, j)))
        args.append(add)
    assert views == (1,) or (to == O and to % 128 == 0)
    scratch = [pltpu.VMEM((tm, to), F32)] + ([pltpu.VMEM((to // 128, tm, 128), F32)] if views != (1,) else [])
    outs = _hosted_call(
        body, grid=(M // tm, O // to, nk), in_specs=in_specs,
        out_specs=[pl.BlockSpec((tm, to), lambda i, j, k: (i, j)) if d == 1 else _view_spec(tm, d, to) for d in views],
        out_shape=[jax.ShapeDtypeStruct((M // d, d * O), out_dtype) for d in views],
        scratch_shapes=scratch, args=args, name=name, guest=guest)
    nv = len(views)
    res = outs[0] if nv == 1 else list(outs[:nv])
    return res if guest is None else (res, outs[nv:])


def _mm_tn(a, b, *, tk, tn, tm, out_dtype, name):
    M, K = a.shape
    N = b.shape[1]
    assert M % tm == 0 and K % tk == 0 and N % tn == 0 and b.shape[0] == M
    nm = M // tm

    def body(a_ref, b_ref, o_ref, acc_ref):
        m = pl.program_id(2)

        @pl.when(m == 0)
        def _():
            acc_ref[...] = jnp.zeros_like(acc_ref)

        acc_ref[...] += lax.dot_general(a_ref[...], b_ref[...], (((0,), (0,)), ((), ())),
                                        preferred_element_type=F32)

        @pl.when(m == nm - 1)
        def _():
            o_ref[...] = acc_ref[...].astype(out_dtype)

    return pl.pallas_call(
        body, grid=(K // tk, N // tn, nm),
        in_specs=[pl.BlockSpec((tm, tk), lambda i, j, m: (m, i)), pl.BlockSpec((tm, tn), lambda i, j, m: (m, j))],
        out_specs=pl.BlockSpec((tk, tn), lambda i, j, m: (i, j)),
        out_shape=jax.ShapeDtypeStruct((K, N), out_dtype),
        scratch_shapes=[pltpu.VMEM((tk, tn), F32)], name=name,
        compiler_params=_params("parallel", "parallel", "arbitrary"),
    )(a, b)


ROW_TILE = 512
MM_ROWS = 1024


def _rms_fwd(x, g, *, name):
    S, Dm = x.shape

    def body(x_ref, g_ref, o_ref):
        xf = x_ref[...]
        r = lax.rsqrt(jnp.mean(xf * xf, axis=-1, keepdims=True) + RMS_EPS)
        o_ref[...] = (xf * r * g_ref[...]).astype(BF)

    return pl.pallas_call(
        body, grid=(S // ROW_TILE,),
        in_specs=[pl.BlockSpec((ROW_TILE, Dm), lambda i: (i, 0)), pl.BlockSpec((1, Dm), lambda i: (0, 0))],
        out_specs=pl.BlockSpec((ROW_TILE, Dm), lambda i: (i, 0)),
        out_shape=jax.ShapeDtypeStruct((S, Dm), BF), name=name, compiler_params=_params("parallel"),
    )(x, g)


def _rms_bwd(x, g, dn, dres, *, name):
    S, Dm = x.shape

    def body(x_ref, g_ref, dn_ref, dres_ref, dx_ref, dxb_ref, dg_ref):
        i = pl.program_id(0)
        xf = x_ref[...]
        r = lax.rsqrt(jnp.mean(xf * xf, axis=-1, keepdims=True) + RMS_EPS)
        xh = xf * r
        dnf = dn_ref[...]
        dyg = dnf * g_ref[...]
        dx = dres_ref[...] + r * (dyg - xh * jnp.mean(dyg * xh, axis=-1, keepdims=True))
        dx_ref[...] = dx
        dxb_ref[...] = dx.astype(BF)

        @pl.when(i == 0)
        def _():
            dg_ref[...] = jnp.zeros_like(dg_ref)

        dg_ref[...] += jnp.sum(dnf * xh, axis=0, keepdims=True)

    row = pl.BlockSpec((ROW_TILE, Dm), lambda i: (i, 0))
    vec = pl.BlockSpec((1, Dm), lambda i: (0, 0))
    return pl.pallas_call(
        body, grid=(S // ROW_TILE,), in_specs=[row, vec, row, row], out_specs=[row, row, vec],
        out_shape=[jax.ShapeDtypeStruct((S, Dm), F32), jax.ShapeDtypeStruct((S, Dm), BF),
                   jax.ShapeDtypeStruct((1, Dm), F32)],
        name=name, compiler_params=_params("arbitrary"),
    )(x, g, dn, dres)


def _loss_head(h, g, tgt, *, name):
    S, Dm = h.shape

    def body(h_ref, g_ref, t_ref, loss_ref, dh_ref, dhb_ref, dg_ref):
        i = pl.program_id(0)
        xf = h_ref[...]
        r = lax.rsqrt(jnp.mean(xf * xf, axis=-1, keepdims=True) + RMS_EPS)
        xh = xf * r
        gv = g_ref[...]
        err = xh * gv - t_ref[...]
        dy = err * (1.0 / Dm)
        dyg = dy * gv
        dh = r * (dyg - xh * jnp.mean(dyg * xh, axis=-1, keepdims=True))
        dh_ref[...] = dh
        dhb_ref[...] = dh.astype(BF)

        @pl.when(i == 0)
        def _():
            dg_ref[...] = jnp.zeros_like(dg_ref)
            loss_ref[...] = jnp.zeros_like(loss_ref)

        dg_ref[...] += jnp.sum(dy * xh, axis=0, keepdims=True)
        part = 0.5 * jnp.sum(jnp.mean(err * err, axis=-1, keepdims=True), axis=0, keepdims=True)
        loss_ref[...] += jnp.broadcast_to(part, loss_ref.shape)

    row = pl.BlockSpec((ROW_TILE, Dm), lambda i: (i, 0))
    vec = pl.BlockSpec((1, Dm), lambda i: (0, 0))
    return pl.pallas_call(
        body, grid=(S // ROW_TILE,), in_specs=[row, vec, row],
        out_specs=[pl.BlockSpec((1, 128), lambda i: (0, 0)), row, row, vec],
        out_shape=[jax.ShapeDtypeStruct((1, 128), F32), jax.ShapeDtypeStruct((S, Dm), F32),
                   jax.ShapeDtypeStruct((S, Dm), BF), jax.ShapeDtypeStruct((1, Dm), F32)],
        name=name, compiler_params=_params("arbitrary"),
    )(h, g, tgt)


SWIGLU_ROWS = 512


def _swiglu_fwd(gu, *, name):
    S = gu.shape[0]

    def body(g_ref, u_ref, o_ref):
        g = g_ref[...].astype(F32)
        sig = 1.0 / (1.0 + jnp.exp(-g))
        o_ref[...] = (g * sig * u_ref[...].astype(F32)).astype(BF)

    return pl.pallas_call(
        body, grid=(S // SWIGLU_ROWS,),
        in_specs=[pl.BlockSpec((SWIGLU_ROWS, D_FF), lambda i: (i, 0)), pl.BlockSpec((SWIGLU_ROWS, D_FF), lambda i: (i, 1))],
        out_specs=pl.BlockSpec((SWIGLU_ROWS, D_FF), lambda i: (i, 0)),
        out_shape=jax.ShapeDtypeStruct((S, D_FF), BF), name=name, compiler_params=_params("parallel"),
    )(gu, gu)


def _swiglu_bwd(gu, dact, *, name):
    S = gu.shape[0]

    def body(g_ref, u_ref, d_ref, o_ref):
        g = g_ref[...].astype(F32)
        u = u_ref[...].astype(F32)
        d = d_ref[...].astype(F32)
        sig = 1.0 / (1.0 + jnp.exp(-g))
        o_ref[:, :D_FF] = (d * u * sig * (1.0 + g * (1.0 - sig))).astype(BF)
        o_ref[:, D_FF:] = (d * g * sig).astype(BF)

    return pl.pallas_call(
        body, grid=(S // SWIGLU_ROWS,),
        in_specs=[pl.BlockSpec((SWIGLU_ROWS, D_FF), lambda i: (i, 0)), pl.BlockSpec((SWIGLU_ROWS, D_FF), lambda i: (i, 1)),
                  pl.BlockSpec((SWIGLU_ROWS, D_FF), lambda i: (i, 0))],
        out_specs=pl.BlockSpec((SWIGLU_ROWS, 2 * D_FF), lambda i: (i, 0)),
        out_shape=jax.ShapeDtypeStruct((S, 2 * D_FF), BF), name=name, compiler_params=_params("parallel"),
    )(gu, gu, dact)


def _band_masks(T, n):
    row = lax.broadcasted_iota(jnp.int32, (T, T), 0)
    col = lax.broadcasted_iota(jnp.int32, (T, T), 1)
    return jnp.logical_and(col >= row, n > 0), col <= row


def _band_fwd(qa, ka, va, qcb, kcb, vcb, *, dil, T, window, name, guest=None):
    L = qa.shape[0]
    nq = L // T
    assert window == T
    nt = (((1,), (1,)), ((), ()))

    def body(q_ref, kp_ref, kc_ref, vp_ref, vc_ref, o_ref, lse_ref):
        valid_prev, valid_cur = _band_masks(T, pl.program_id(1))
        lane = lax.broadcasted_iota(jnp.int32, (T, 128), 1)
        low = lane < HEAD_DIM
        ones = jnp.ones((T, 128), BF)
        lse = jnp.zeros((T, 128), F32)
        def scores(h):
            ps = slice((h // 2) * 128, (h // 2 + 1) * 128)
            qp = q_ref[:, ps] * jnp.asarray(ATTN_SCALE, BF)
            qm = jnp.where(low if h % 2 == 0 else jnp.logical_not(low), qp, jnp.zeros_like(qp))
            s0 = jnp.where(valid_prev, lax.dot_general(qm, kp_ref[:, ps], nt, preferred_element_type=F32), NEG_INF)
            s1 = jnp.where(valid_cur, lax.dot_general(qm, kc_ref[:, ps], nt, preferred_element_type=F32), NEG_INF)
            return s0, s1

        def softmax(s0, s1):
            m = jnp.maximum(jnp.max(s0, axis=1, keepdims=True), jnp.max(s1, axis=1, keepdims=True))
            return m, jnp.exp(s0 - m).astype(BF), jnp.exp(s1 - m).astype(BF)

        def weighted(h, p0, p1):
            ps = slice((h // 2) * 128, (h // 2 + 1) * 128)
            l = jnp.dot(p0, ones, preferred_element_type=F32) + jnp.dot(p1, ones, preferred_element_type=F32)
            acc = jnp.dot(p0, vp_ref[:, ps], preferred_element_type=F32) + jnp.dot(p1, vc_ref[:, ps], preferred_element_type=F32)
            return l, acc

        sc, pr, even = {}, {}, None
        for t in range(N_HEADS + 2):
            if t < N_HEADS:
                sc[t] = scores(t)
            done = None
            if t >= 2:
                m, p0, p1 = pr.pop(t - 2)
                done = (m,) + weighted(t - 2, p0, p1)
            if 1 <= t <= N_HEADS:
                pr[t - 1] = softmax(*sc.pop(t - 1))
            if done is not None:
                h = t - 2
                m, l, acc = done
                lse = jnp.where(lane == h, m + jnp.log(l), lse)
                if h % 2 == 0:
                    even = acc / l
                else:
                    o_ref[:, (h // 2) * 128:(h // 2 + 1) * 128] = jnp.where(low, even, acc / l)
        lse_ref[...] = lse

    def prev(n):
        return jnp.maximum(n - 1, 0)

    blk = lambda f, cb: pl.BlockSpec((T, 1024), lambda r, n: (f(n), cb(r)))
    same = lambda n: n
    outs = _hosted_call(
        body, grid=(dil, nq),
        in_specs=[blk(same, qcb), blk(prev, kcb), blk(same, kcb), blk(prev, vcb), blk(same, vcb)],
        out_specs=[pl.BlockSpec((T, 1024), lambda r, n: (n, r)), pl.BlockSpec((T, 128), lambda r, n: (n, r))],
        out_shape=[jax.ShapeDtypeStruct((L, dil * 1024), F32), jax.ShapeDtypeStruct((L, dil * 128), F32)],
        scratch_shapes=[], args=(qa, ka, ka, va, va), name=name, guest=guest)
    return outs if guest is None else (outs[:2], outs[2:])


def _band_bwd(qa, ka, va, qcb, kcb, vcb, doa, oa, lsea, *, dil, T, window, name, guest=None):
    L = qa.shape[0]
    nq = L // T
    assert window == T
    nt = (((1,), (1,)), ((), ()))
    tn = (((0,), (0,)), ((), ()))

    def body(q_ref, kp_ref, kc_ref, vp_ref, vc_ref, do_ref, o_ref, lse_ref, dq_ref, dk_ref, dv_ref, ck_sc, cv_sc):
        n = pl.program_id(1)

        @pl.when(n == 0)
        def _():
            ck_sc[...] = jnp.zeros_like(ck_sc)
            cv_sc[...] = jnp.zeros_like(cv_sc)

        @pl.when(n < nq)
        def _():
            valid_prev, valid_cur = _band_masks(T, n)
            low = lax.broadcasted_iota(jnp.int32, (T, 128), 1) < HEAD_DIM
            dot = functools.partial(lax.dot_general, preferred_element_type=F32)

            def pair(h):
                return slice((h // 2) * 128, (h // 2 + 1) * 128)

            def products(h):
                ps = pair(h)
                mask = low if h % 2 == 0 else jnp.logical_not(low)
                qp = q_ref[:, ps] * jnp.asarray(ATTN_SCALE, BF)
                dop = do_ref[:, ps]
                qm = jnp.where(mask, qp, jnp.zeros_like(qp))
                dom = jnp.where(mask, dop, jnp.zeros_like(dop))
                s0 = jnp.where(valid_prev, dot(qm, kp_ref[:, ps], nt), NEG_INF)
                s1 = jnp.where(valid_cur, dot(qm, kc_ref[:, ps], nt), NEG_INF)
                return qm, dom, s0, s1, dot(dom, vp_ref[:, ps], nt), dot(dom, vc_ref[:, ps], nt)

            def pointwise(h, qm, dom, s0, s1, dp0, dp1):
                ps = pair(h)
                mask = low if h % 2 == 0 else jnp.logical_not(low)
                prod = do_ref[:, ps].astype(F32) * o_ref[:, ps].astype(F32)
                delta = jnp.sum(jnp.where(mask, prod, 0.0), axis=1, keepdims=True)
                lse = lse_ref[:, h:h + 1]
                p0 = jnp.exp(s0 - lse)
                p1 = jnp.exp(s1 - lse)
                ds0 = (p0 * (dp0 - delta)).astype(BF)
                ds1 = (p1 * (dp1 - delta)).astype(BF)
                return qm, dom, p0.astype(BF), p1.astype(BF), ds0, ds1

            def gradients(h, qm, dom, p0, p1, ds0, ds1):
                ps = pair(h)
                dq = dot(ds0, kp_ref[:, ps], (((1,), (0,)), ((), ()))) + dot(ds1, kc_ref[:, ps], (((1,), (0,)), ((), ())))
                return dq, dot(ds0, qm, tn), dot(p0, dom, tn), dot(ds1, qm, tn), dot(p1, dom, tn)

            st1, st2, even = {}, {}, None
            for t in range(N_HEADS + 2):
                if t < N_HEADS:
                    st1[t] = products(t)
                done = gradients(t - 2, *st2.pop(t - 2)) if t >= 2 else None
                if 1 <= t <= N_HEADS:
                    st2[t - 1] = pointwise(t - 1, *st1.pop(t - 1))
                if done is not None:
                    h = t - 2
                    if h % 2 == 0:
                        even = done
                    else:
                        ps = pair(h)
                        dq_ref[:, ps] = (jnp.where(low, even[0], done[0]) * ATTN_SCALE).astype(BF)
                        dk_ref[:, ps] = (ck_sc[:, ps] + even[1] + done[1]).astype(BF)
                        dv_ref[:, ps] = (cv_sc[:, ps] + even[2] + done[2]).astype(BF)
                        ck_sc[:, ps] = even[3] + done[3]
                        cv_sc[:, ps] = even[4] + done[4]

        @pl.when(n == nq)
        def _():
            dk_ref[...] = ck_sc[...].astype(BF)
            dv_ref[...] = cv_sc[...].astype(BF)

    def cur(n):
        return jnp.minimum(n, nq - 1)

    def prev(n):
        return jnp.maximum(cur(n) - 1, 0)

    blk = lambda f, cb: pl.BlockSpec((T, 1024), lambda r, n: (f(n), cb(r)))
    own = lambda r: r
    outs = _hosted_call(
        body, grid=(dil, nq + 1),
        in_specs=[blk(cur, qcb), blk(prev, kcb), blk(cur, kcb), blk(prev, vcb), blk(cur, vcb), blk(cur, own), blk(cur, own),
                  pl.BlockSpec((T, 128), lambda r, n: (cur(n), r))],
        out_specs=[blk(cur, own), blk(lambda n: jnp.maximum(n - 1, 0), own), blk(lambda n: jnp.maximum(n - 1, 0), own)],
        out_shape=[jax.ShapeDtypeStruct((L, dil * 1024), BF)] * 3,
        scratch_shapes=[pltpu.VMEM((T, 1024), F32), pltpu.VMEM((T, 1024), F32)],
        args=(qa, ka, ka, va, va, doa, oa, lsea), name=name, guest=guest)
    return outs if guest is None else (outs[:3], outs[3:])


FOX_T = 256
FOX_TQ = 512
FOX_TK = 512
FOX_TQ_BWD = 512
FOX_ROWS = 256


def _fox_fwd(qkv, cT, *, name, guest=None):
    S = qkv.shape[0]
    T, TK, R = FOX_TQ, FOX_TK, FOX_ROWS
    nq = S // T
    nt = (((1,), (1,)), ((), ()))
    chains = [(h, rh) for h in range(N_HEADS) for rh in range(T // R)]
    pairs = [(n, j) for n in range(nq) for j in range((n * T + T - 1) // TK + 1)]
    schedule = [jnp.asarray([p[i] for p in pairs], jnp.int32) for i in range(2)]

    def body(n_tab, j_tab, q_ref, k_ref, v_ref, ct_ref, o_ref, lse_ref, m_sc, l_sc, acc_sc):
        n = n_tab[pl.program_id(0)]
        j = j_tab[pl.program_id(0)]
        last_j = (n * T + T - 1) // TK
        lane = lax.broadcasted_iota(jnp.int32, (R, 128), 1)
        low = lane < HEAD_DIM
        ones = jnp.ones((TK, 128), BF)

        @pl.when(j == 0)
        def _():
            m_sc[...] = jnp.full(m_sc.shape, NEG_INF, F32)
            l_sc[...] = jnp.zeros_like(l_sc)
            acc_sc[...] = jnp.zeros_like(acc_sc)

        def step(diagonal):
            def pair(h):
                return slice((h // 2) * 128, (h // 2 + 1) * 128)

            def rows(rh):
                return slice(rh * R, (rh + 1) * R)

            def scores(h, rh):
                qp = q_ref[rows(rh), pair(h)] * jnp.asarray(ATTN_SCALE, BF)
                qm = jnp.where(low if h % 2 == 0 else jnp.logical_not(low), qp, jnp.zeros_like(qp))
                s = lax.dot_general(qm, k_ref[:, pair(h)], nt, preferred_element_type=F32) - ct_ref[h:h + 1, :]
                if diagonal:
                    ahead = lax.broadcasted_iota(jnp.int32, (R, TK), 1) - lax.broadcasted_iota(jnp.int32, (R, TK), 0)
                    s = jnp.where(ahead <= n * T + rh * R - j * TK, s, NEG_INF)
                return s

            def softmax(h, rh, s):
                m_prev = m_sc[h, rows(rh), :]
                m_new = jnp.maximum(m_prev, jnp.max(s, axis=1, keepdims=True))
                p = jnp.exp(s - jnp.concatenate([m_new] * (TK // 128), axis=1)).astype(BF)
                return m_new, jnp.exp(m_prev - m_new), p

            def weighted(h, p):
                vx = jnp.concatenate([v_ref[:, pair(h)], ones], axis=1)
                return jnp.dot(p, vx, preferred_element_type=F32)

            sc, pr, even = {}, {}, {}
            nch = len(chains)
            for t in range(nch + 2):
                if t < nch:
                    sc[t] = scores(*chains[t])
                done = None
                if t >= 2:
                    m_new, alpha, p = pr.pop(t - 2)
                    done = (m_new, alpha, weighted(chains[t - 2][0], p))
                if 1 <= t <= nch:
                    pr[t - 1] = softmax(*chains[t - 1], sc.pop(t - 1))
                if done is not None:
                    h, rh = chains[t - 2]
                    m_new, alpha, pv = done
                    m_sc[h, rows(rh), :] = m_new
                    l_sc[h, rows(rh), :] = alpha * l_sc[h, rows(rh), :] + pv[:, 128:]
                    if h % 2 == 0:
                        even[rh] = (alpha, pv[:, :128])
                    else:
                        a0, pv0 = even.pop(rh)
                        acc = acc_sc[h // 2, rows(rh), :]
                        acc_sc[h // 2, rows(rh), :] = jnp.where(low, a0 * acc + pv0, alpha * acc + pv[:, :128])

        @pl.when(j < last_j)
        def _():
            step(False)

        @pl.when(j == last_j)
        def _():
            step(True)
            lane_t = lax.broadcasted_iota(jnp.int32, (T, 128), 1)
            low_t = lane_t < HEAD_DIM
            lse = jnp.zeros((T, 128), F32)
            for h in range(N_HEADS):
                lse = jnp.where(lane_t == h, m_sc[h] + jnp.log(l_sc[h]), lse)
            lse_ref[...] = lse
            for hp in range(N_HEADS // 2):
                inv = jnp.where(low_t, 1.0 / l_sc[2 * hp], 1.0 / l_sc[2 * hp + 1])
                o_ref[:, hp * 128:(hp + 1) * 128] = (acc_sc[hp] * inv).astype(BF)

    outs = _hosted_call(
        body, grid=(len(pairs),),
        in_specs=[pl.BlockSpec((T, 1024), lambda t, n, j: (n[t], 0)), pl.BlockSpec((TK, 1024), lambda t, n, j: (j[t], 1)),
                  pl.BlockSpec((TK, 1024), lambda t, n, j: (j[t], 2)), pl.BlockSpec((GATE_LANES, TK), lambda t, n, j: (0, j[t]))],
        out_specs=[pl.BlockSpec((T, 1024), lambda t, n, j: (n[t], 0)), pl.BlockSpec((T, 128), lambda t, n, j: (n[t], 0))],
        out_shape=[jax.ShapeDtypeStruct((S, 1024), BF), jax.ShapeDtypeStruct((S, 128), F32)],
        scratch_shapes=[pltpu.VMEM((N_HEADS, T, 128), F32), pltpu.VMEM((N_HEADS, T, 128), F32),
                        pltpu.VMEM((N_HEADS // 2, T, 128), F32)],
        args=(qkv, qkv, qkv, cT), name=name, guest=guest, schedule=schedule)
    return outs if guest is None else (outs[:2], outs[2:])


def _fox_bwd(qkv, cT, do, o, lse, *, name, guest=None):
    S = qkv.shape[0]
    T, TQ, R = FOX_T, FOX_TQ_BWD, FOX_ROWS
    nk, nq = S // T, S // TQ
    nt = (((1,), (1,)), ((), ()))
    tn = (((0,), (0,)), ((), ()))
    nn = (((1,), (0,)), ((), ()))
    chains = [(h, rh) for h in range(N_HEADS) for rh in range(TQ // R)]
    dot = functools.partial(lax.dot_general, preferred_element_type=F32)
    pairs = [(kb, qb) for kb in range(nk) for qb in range(kb * T // TQ, nq)]
    schedule = [jnp.asarray([p[i] for p in pairs], jnp.int32) for i in range(2)]

    def body(kb_tab, qb_tab, q_ref, k_ref, v_ref, ct_ref, do_ref, o_ref, lse_ref, dq_ref, dk_ref, dv_ref, dct_ref, dcq_ref,
             dq_sc, dk_sc, dv_sc, dc_sc, dcq_sc):
        kb = kb_tab[pl.program_id(0)]
        qb = qb_tab[pl.program_id(0)]
        jq = qb - kb * T // TQ
        lane = lax.broadcasted_iota(jnp.int32, (R, 128), 1)
        low = lane < HEAD_DIM
        ones_k = jnp.ones((T, 128), BF)
        ones_r = jnp.ones((8, R), BF)

        @pl.when(jnp.logical_and(kb == 0, jq == 0))
        def _():
            dq_sc[...] = jnp.zeros_like(dq_sc)
            dcq_sc[...] = jnp.zeros_like(dcq_sc)

        @pl.when(jq == 0)
        def _():
            dk_sc[...] = jnp.zeros_like(dk_sc)
            dv_sc[...] = jnp.zeros_like(dv_sc)
            dc_sc[...] = jnp.zeros_like(dc_sc)

        def step(diagonal):
            def pair(h):
                return slice((h // 2) * 128, (h // 2 + 1) * 128)

            def rows(rh):
                return slice(rh * R, (rh + 1) * R)

            def qrows(rh):
                return pl.ds(pl.multiple_of(qb * TQ + rh * R, R), R)

            def products(h, rh):
                mask = low if h % 2 == 0 else jnp.logical_not(low)
                qp = q_ref[rows(rh), pair(h)] * jnp.asarray(ATTN_SCALE, BF)
                dop = do_ref[rows(rh), pair(h)]
                qm = jnp.where(mask, qp, jnp.zeros_like(qp))
                dom = jnp.where(mask, dop, jnp.zeros_like(dop))
                s = dot(qm, k_ref[:, pair(h)], nt) - ct_ref[h:h + 1, :]
                if diagonal:
                    ahead = lax.broadcasted_iota(jnp.int32, (R, T), 1) - lax.broadcasted_iota(jnp.int32, (R, T), 0)
                    s = jnp.where(ahead <= qb * TQ + rh * R - kb * T, s, NEG_INF)
                return qm, dom, s, dot(dom, v_ref[:, pair(h)], nt)

            def pointwise(h, rh, qm, dom, s, dp):
                mask = low if h % 2 == 0 else jnp.logical_not(low)
                prod = do_ref[rows(rh), pair(h)].astype(F32) * o_ref[rows(rh), pair(h)].astype(F32)
                delta = jnp.sum(jnp.where(mask, prod, 0.0), axis=1, keepdims=True)
                p = jnp.exp(s - lse_ref[rows(rh), h:h + 1])
                ds = (p * (dp - delta)).astype(BF)
                return qm, dom, p.astype(BF), ds

            def gradients(h, qm, dom, p, ds):
                kx = jnp.concatenate([k_ref[:, pair(h)], ones_k], axis=1)
                return dot(ds, kx, nn), dot(qm, ds, tn), dot(dom, p, tn), dot(ones_r, ds, nn)

            st1, st2, even = {}, {}, {}
            dcq_tiles = [jnp.zeros((R, 128), F32) for _ in range(TQ // R)]
            nch = len(chains)
            for t in range(nch + 2):
                if t < nch:
                    st1[t] = products(*chains[t])
                done = gradients(chains[t - 2][0], *st2.pop(t - 2)) if t >= 2 else None
                if 1 <= t <= nch:
                    st2[t - 1] = pointwise(*chains[t - 1], *st1.pop(t - 1))
                if done is not None:
                    h, rh = chains[t - 2]
                    dq_rsum, dk, dv, csum = done
                    dq = dq_rsum[:, :128]
                    dcq_tiles[rh] = jnp.where(lane == h, dq_rsum[:, 128:], dcq_tiles[rh])
                    dc_sc[h:h + 1, :] -= csum[0:1, :]
                    if h % 2 == 0:
                        even[rh] = (dq, dk, dv)
                    else:
                        dq0, dk0, dv0 = even.pop(rh)
                        dq_sc[qrows(rh), pair(h)] += jnp.where(low, dq0, dq) * ATTN_SCALE
                        dk_sc[h // 2] += dk0 + dk
                        dv_sc[h // 2] += dv0 + dv
            for rh in range(TQ // R):
                dcq_sc[qrows(rh), :] += dcq_tiles[rh]

        @pl.when(jq > 0)
        def _():
            step(False)

        @pl.when(jq == 0)
        def _():
            step(True)

        @pl.when(qb == nq - 1)
        def _():
            for hp in range(N_HEADS // 2):
                dk_ref[:, hp * 128:(hp + 1) * 128] = dk_sc[hp].T.astype(BF)
                dv_ref[:, hp * 128:(hp + 1) * 128] = dv_sc[hp].T.astype(BF)
            dct_ref[...] = dc_sc[...]

        @pl.when(jnp.logical_and(kb == nk - 1, qb == nq - 1))
        def _():
            def put(i, carry):
                r = pl.ds(pl.multiple_of(i * T, T), T)
                dq_ref[r, :] = dq_sc[r, :].astype(BF)
                return carry
            lax.fori_loop(0, nk, put, 0)
            dcq_ref[...] = dcq_sc[...]

    qblk = lambda col: pl.BlockSpec((TQ, 1024), lambda t, kb, qb: (qb[t], col))
    kblk = lambda col: pl.BlockSpec((T, 1024), lambda t, kb, qb: (kb[t], col))
    whole = pl.BlockSpec((S, 1024), lambda t, kb, qb: (0, 0))
    outs = _hosted_call(
        body, grid=(len(pairs),),
        in_specs=[qblk(0), kblk(1), kblk(2), pl.BlockSpec((GATE_LANES, T), lambda t, kb, qb: (0, kb[t])), qblk(0), qblk(0),
                  pl.BlockSpec((TQ, 128), lambda t, kb, qb: (qb[t], 0))],
        out_specs=[whole, kblk(0), kblk(0), pl.BlockSpec((GATE_LANES, T), lambda t, kb, qb: (0, kb[t])),
                   pl.BlockSpec((S, GATE_LANES), lambda t, kb, qb: (0, 0))],
        out_shape=[jax.ShapeDtypeStruct((S, 1024), BF)] * 3 + [jax.ShapeDtypeStruct((GATE_LANES, S), F32),
                                                               jax.ShapeDtypeStruct((S, GATE_LANES), F32)],
        scratch_shapes=[pltpu.VMEM((S, 1024), F32), pltpu.VMEM((N_HEADS // 2, 128, T), F32), pltpu.VMEM((N_HEADS // 2, 128, T), F32),
                        pltpu.VMEM((GATE_LANES, T), F32), pltpu.VMEM((S, GATE_LANES), F32)],
        args=(qkv, qkv, qkv, cT, do, o, lse), name=name, guest=guest, schedule=schedule)
    return outs if guest is None else (outs[:5], outs[5:])


def _to_natural(src_ref, buf, d, width):
    rows = buf.shape[1]
    for r in range(d):
        for ch in range(width // 128):
            lanes = slice(r * width + ch * 128, r * width + (ch + 1) * 128)
            buf.at[ch][pl.ds(r, rows // d, stride=d), :] = src_ref[:, lanes].astype(F32)
    return jnp.concatenate([buf[ch] for ch in range(width // 128)], axis=1)


def _to_view(val, buf, dst_ref, d, width):
    rows = buf.shape[1]
    for ch in range(width // 128):
        buf[ch] = val[:, ch * 128:(ch + 1) * 128]
    for r in range(d):
        for ch in range(width // 128):
            lanes = slice(r * width + ch * 128, r * width + (ch + 1) * 128)
            dst_ref[:, lanes] = buf.at[ch][pl.ds(r, rows // d, stride=d), :].astype(dst_ref.dtype)


def _view_spec(rows, d, width):
    return pl.BlockSpec((rows // d, d * width), lambda i, *_: (i, 0))


def _combine_groups(os, lses, dils, *, name):
    ng = len(os)
    S = os[0].shape[0] * dils[0]
    tm = ROW_TILE
    views = sorted(set(dils))

    def body(*refs):
        o_refs, l_refs = refs[:ng], refs[ng:2 * ng]
        outs = refs[2 * ng:2 * ng + 2 * len(views)]
        wide, narrow = refs[-2], refs[-1]
        ls = [l_refs[g][...] if dils[g] == 1 else _to_natural(l_refs[g], narrow, dils[g], 128) for g in range(ng)]
        m = functools.reduce(jnp.maximum, ls)
        es = [jnp.exp(l - m) for l in ls]
        den = functools.reduce(jnp.add, es)
        ws = [e / den for e in es]
        lse = m + jnp.log(den)
        og = [o_refs[g][...] if dils[g] == 1 else _to_natural(o_refs[g], wide, dils[g], 1024) for g in range(ng)]
        cols = []
        for h in range(N_HEADS):
            hs = slice(h * HEAD_DIM, (h + 1) * HEAD_DIM)
            acc = ws[0][:, h:h + 1] * og[0][:, hs]
            for g in range(1, ng):
                acc = acc + ws[g][:, h:h + 1] * og[g][:, hs]
            cols.append(acc)
        o = jnp.concatenate(cols, axis=1)
        for k, d in enumerate(views):
            if d == 1:
                outs[2 * k][...] = o.astype(BF)
                outs[2 * k + 1][...] = lse
            else:
                _to_view(o, wide, outs[2 * k], d, 1024)
                _to_view(lse, narrow, outs[2 * k + 1], d, 128)

    out_specs, out_shape = [], []
    for d in views:
        out_specs += [_view_spec(tm, d, 1024), _view_spec(tm, d, 128)]
        out_shape += [jax.ShapeDtypeStruct((S // d, d * 1024), BF), jax.ShapeDtypeStruct((S // d, d * 128), F32)]
    res = pl.pallas_call(
        body, grid=(S // tm,), in_specs=[_view_spec(tm, d, 1024) for d in dils] + [_view_spec(tm, d, 128) for d in dils],
        out_specs=out_specs, out_shape=out_shape,
        scratch_shapes=[pltpu.VMEM((8, tm, 128), F32), pltpu.VMEM((1, tm, 128), F32)],
        name=name, compiler_params=_params("parallel"),
    )(*os, *lses)
    return {d: (res[2 * k], res[2 * k + 1]) for k, d in enumerate(views)}


def _assemble(parts, rope_flags, rope, dils, *, name):
    n = len(parts)
    S = parts[0].shape[0] * dils[0]
    use_rope = any(rope_flags)
    tm = ROW_TILE

    def body(*refs):
        out_ref, natural = refs[-2], refs[-1]
        for b in range(n):
            cols = slice(b * 1024, (b + 1) * 1024)
            d = dils[b]
            val = refs[b][...].astype(F32) if d == 1 else _to_natural(refs[b], natural, d, 1024)
            if rope_flags[b]:
                cos_ref, sa_ref, sb_ref = refs[n:n + 3]
                val = _rope_rotate(val, cos_ref[...], sa_ref[...], sb_ref[...])
            out_ref[:, cols] = val.astype(BF)

    in_specs = [_view_spec(tm, d, 1024) for d in dils]
    args = list(parts)
    if use_rope:
        in_specs += [pl.BlockSpec((tm, 128), lambda i: (i, 0))] * 3
        args += list(rope)
    return pl.pallas_call(
        body, grid=(S // tm,), in_specs=in_specs, out_specs=pl.BlockSpec((tm, n * 1024), lambda i: (i, 0)),
        out_shape=jax.ShapeDtypeStruct((S, n * 1024), BF), scratch_shapes=[pltpu.VMEM((8, tm, 128), F32)],
        name=name, compiler_params=_params("parallel"),
    )(*args)


GATE_ROWS = 512


def _gate_fwd(z, bf, *, name):
    S = z.shape[0]

    def body(z_ref, b_ref, ct_ref, carry):
        i = pl.program_id(0)

        @pl.when(i == 0)
        def _():
            carry[...] = jnp.zeros_like(carry)

        zz = z_ref[...] + b_ref[...]
        logf = jnp.minimum(zz, 0.0) - jnp.log(1.0 + jnp.exp(-jnp.abs(zz)))
        tri = (lax.broadcasted_iota(jnp.int32, (GATE_ROWS, GATE_ROWS), 0)
               >= lax.broadcasted_iota(jnp.int32, (GATE_ROWS, GATE_ROWS), 1)).astype(F32)
        cs = jnp.dot(tri, logf, precision=lax.Precision.HIGHEST, preferred_element_type=F32) + carry[...]
        ct_ref[...] = cs.T
        carry[...] = cs[GATE_ROWS - 1:GATE_ROWS, :]

    return pl.pallas_call(
        body, grid=(S // GATE_ROWS,),
        in_specs=[pl.BlockSpec((GATE_ROWS, GATE_LANES), lambda i: (i, 0)), pl.BlockSpec((1, GATE_LANES), lambda i: (0, 0))],
        out_specs=pl.BlockSpec((GATE_LANES, GATE_ROWS), lambda i: (0, i)),
        out_shape=jax.ShapeDtypeStruct((GATE_LANES, S), F32),
        scratch_shapes=[pltpu.VMEM((1, GATE_LANES), F32)], name=name, compiler_params=_params("arbitrary"),
    )(z, bf)


def _gate_bwd(z, bf, dcT, dcq, *, name):
    S = z.shape[0]
    nb = S // GATE_ROWS

    def body(z_ref, b_ref, dct_ref, dcq_ref, dz_ref, db_ref, carry):
        i = pl.program_id(0)

        @pl.when(i == 0)
        def _():
            carry[...] = jnp.zeros_like(carry)
            db_ref[...] = jnp.zeros_like(db_ref)

        dc = dct_ref[...].T + dcq_ref[...]
        tri = (lax.broadcasted_iota(jnp.int32, (GATE_ROWS, GATE_ROWS), 0)
               <= lax.broadcasted_iota(jnp.int32, (GATE_ROWS, GATE_ROWS), 1)).astype(F32)
        dl = jnp.dot(tri, dc, precision=lax.Precision.HIGHEST, preferred_element_type=F32) + carry[...]
        carry[...] = dl[0:1, :]
        zz = z_ref[...] + b_ref[...]
        dz = dl * (1.0 / (1.0 + jnp.exp(zz)))
        lane = lax.broadcasted_iota(jnp.int32, dz.shape, 1)
        dz = jnp.where(lane < N_HEADS, dz, 0.0)
        dz_ref[...] = dz.astype(BF)
        db_ref[...] += jnp.sum(dz, axis=0, keepdims=True)

    return pl.pallas_call(
        body, grid=(nb,),
        in_specs=[pl.BlockSpec((GATE_ROWS, GATE_LANES), lambda i: (nb - 1 - i, 0)), pl.BlockSpec((1, GATE_LANES), lambda i: (0, 0)),
                  pl.BlockSpec((GATE_LANES, GATE_ROWS), lambda i: (0, nb - 1 - i)),
                  pl.BlockSpec((GATE_ROWS, GATE_LANES), lambda i: (nb - 1 - i, 0))],
        out_specs=[pl.BlockSpec((GATE_ROWS, GATE_LANES), lambda i: (nb - 1 - i, 0)), pl.BlockSpec((1, GATE_LANES), lambda i: (0, 0))],
        out_shape=[jax.ShapeDtypeStruct((S, GATE_LANES), BF), jax.ShapeDtypeStruct((1, GATE_LANES), F32)],
        scratch_shapes=[pltpu.VMEM((1, GATE_LANES), F32)], name=name, compiler_params=_params("arbitrary"),
    )(z, bf, dcT, dcq)


def _rope_tables(S):
    half = ROT_DIM // 2
    inv_freq = ROPE_THETA ** (-jnp.arange(half, dtype=F32) * 2.0 / ROT_DIM)
    ang = jnp.arange(S, dtype=F32)[:, None] * inv_freq[None, :]
    cos, sin = jnp.cos(ang), jnp.sin(ang)
    zero = jnp.zeros((S, HEAD_DIM - ROT_DIM), F32)
    zh = jnp.zeros((S, half), F32)
    cos_h = jnp.concatenate([cos, cos, jnp.ones_like(zero)], axis=1)
    sa_h = jnp.concatenate([-sin, zh, zero], axis=1)
    sb_h = jnp.concatenate([zh, sin, zero], axis=1)
    two = lambda t: jnp.concatenate([t, t], axis=1)
    return two(cos_h), two(sa_h), two(sb_h)


def _ffn_fwd(h, norm, w_gu, w_down, tag):
    n = _rms_fwd(h, norm, name=f"ffn{tag}_norm")
    gu = _mm_nn(n, w_gu, tm=MM_ROWS, tn=1408, out_dtype=BF, name=f"ffn{tag}_gu")
    act = _swiglu_fwd(gu, name=f"ffn{tag}_act")
    out = _mm_nn(act, w_down, tm=MM_ROWS // 2, tn=1024, out_dtype=F32, name=f"ffn{tag}_down", resid=h)
    return out, (h, n, gu, act)


def _ffn_bwd(dh, dhb, saved, norm, w_gu, w_down, tag, ride=None):
    h, n, gu, act = saved
    dact = _mm_nt(dhb, w_down, tm=MM_ROWS, to=1408, tn=1024, out_dtype=BF, name=f"ffn{tag}_dact")
    dw_down = _mm_tn(act, dhb, tk=1408, tn=1024, tm=MM_ROWS, out_dtype=BF, name=f"ffn{tag}_dwdown")
    dgu = _swiglu_bwd(gu, dact, name=f"ffn{tag}_dgu")
    dn_call = lambda guest: _mm_nt(dgu, w_gu, tm=MM_ROWS, to=1024, tn=1408, out_dtype=F32, name=f"ffn{tag}_dn", guest=guest)
    dn = dn_call(None) if ride is None else ride(dn_call)
    dw_gu = _mm_tn(n, dgu, tk=1024, tn=1408, tm=MM_ROWS, out_dtype=BF, name=f"ffn{tag}_dwgu")
    dx, dxb, dg = _rms_bwd(h, norm, dn, dh, name=f"ffn{tag}_dnorm")
    return dx, dxb, dg, dw_gu, dw_down


def _local_step(x, tgt, w, mats, fetch, exchange):
    S = x.shape[0]
    rope_f = _rope_tables(S)
    rope_b = (rope_f[0], -rope_f[1], -rope_f[2])
    g, partial, landed = {}, {}, {}
    w = dict(w, ffn_w_gu={}, ffn_w_down={})

    def bring(call, indices):
        bufs = [mats[wi] for wi in indices]
        if fetch is None:
            return call(None), bufs
        return call(fetch(indices, bufs))

    def ride(call, indices):
        guest = exchange(indices, [partial[wi] for wi in indices]) if indices else None
        res = call(guest)
        if guest is None:
            return res
        res, outs = res
        landed.update(zip(indices, outs))
        return res

    n0 = _rms_fwd(x, w["a_norm"], name="a_norm")
    dils = [d for _, d in DILATED_PATTERNS]
    projs, (w["ffn_w_gu"][0], w["ffn_w_down"][0]) = bring(
        lambda guest: _mm_nn(n0, w["a_w_in"], tm=MM_ROWS, tn=1024, out_dtype=BF, name="a_proj", rope=rope_f, guest=guest,
                             groups=dils), [4, 6])
    block = lambda t, dil: (lambda r: t * dil + r)
    o_parts, lse_parts = [], []
    for gi, (window, dil) in enumerate(DILATED_PATTERNS):
        pv = projs[gi]
        attend = lambda guest: _band_fwd(pv, pv, pv, block(0, dil), block(1, dil), block(2, dil), dil=dil, T=128,
                                         window=window // dil, name=f"a_attn{gi}", guest=guest)
        if gi == 0:
            (o_g, lse_g), (w["a_w_out"],) = bring(attend, [1])
        elif gi == 1:
            (o_g, lse_g), (b_in,) = bring(attend, [2])
        else:
            (o_g, lse_g), (w["b_w_out"],) = bring(attend, [3])
        o_parts.append(o_g)
        lse_parts.append(lse_g)
    b_in = b_in.transpose(1, 0, 2).reshape(D_MODEL, -1)
    w["b_w_qkv"] = b_in[:, :QKV_COLS]
    w["b_w_f"] = jnp.pad(b_in[:, QKV_COLS:], ((0, 0), (0, GATE_LANES + QKV_COLS - b_in.shape[1])))
    mixed = _combine_groups(o_parts, lse_parts, dils, name="a_combine")
    o_a = mixed[1][0]
    h1 = _mm_nn(o_a, w["a_w_out"], tm=MM_ROWS, tn=1024, out_dtype=F32, name="a_out", resid=x)
    h2, ffn0 = _ffn_fwd(h1, w["ffn_norm"][0:1], w["ffn_w_gu"][0], w["ffn_w_down"][0], 0)

    n2 = _rms_fwd(h2, w["b_norm"], name="b_norm")
    qkv = _mm_nn(n2, w["b_w_qkv"], tm=MM_ROWS, tn=1024, out_dtype=BF, name="b_proj")
    zf = _mm_nn(n2, w["b_w_f"], tm=MM_ROWS, tn=GATE_LANES, out_dtype=F32, name="b_gate_proj")
    cT = _gate_fwd(zf, w["b_f"], name="b_gate")
    (o_b, lse_b), (w["ffn_w_gu"][1], w["ffn_w_down"][1]) = bring(lambda guest: _fox_fwd(qkv, cT, name="b_attn", guest=guest), [5, 7])
    h3 = _mm_nn(o_b, w["b_w_out"], tm=MM_ROWS, tn=1024, out_dtype=F32, name="b_out", resid=h2)
    h4, ffn1 = _ffn_fwd(h3, w["ffn_norm"][1:2], w["ffn_w_gu"][1], w["ffn_w_down"][1], 1)

    loss, dh4, dh4b, g["final_norm"] = _loss_head(h4, w["final_norm"], tgt, name="loss_head")

    dh3, dh3b, dg_f1, partial[5], partial[7] = _ffn_bwd(dh4, dh4b, ffn1, w["ffn_norm"][1:2], w["ffn_w_gu"][1], w["ffn_w_down"][1], 1)

    do_b = _mm_nt(dh3b, w["b_w_out"], tm=MM_ROWS, to=1024, tn=1024, out_dtype=BF, name="b_do")
    partial[3] = _mm_tn(o_b, dh3b, tk=1024, tn=1024, tm=MM_ROWS, out_dtype=BF, name="b_dwout")
    dq, dk, dv, dcT, dcq = ride(lambda guest: _fox_bwd(qkv, cT, do_b, o_b, lse_b, name="b_attn_bwd", guest=guest), [5, 7, 3])
    dz, g["b_f"] = _gate_bwd(zf, w["b_f"], dcT, dcq, name="b_gate_bwd")
    dqkv = _assemble([dq, dk, dv], [False] * 3, None, [1] * 3, name="b_dproj")
    dn2 = _mm_nt(dz, w["b_w_f"], tm=MM_ROWS, to=1024, tn=GATE_LANES, out_dtype=F32, name="b_dn_gate")
    dn2 = _mm_nt(dqkv, w["b_w_qkv"], tm=MM_ROWS, to=1024, tn=1024, out_dtype=F32, name="b_dn", add=dn2)
    g_qkv = _mm_tn(n2, dqkv, tk=1024, tn=1024, tm=MM_ROWS, out_dtype=BF, name="b_dwqkv")
    g_f = _mm_tn(n2, dz, tk=1024, tn=GATE_LANES, tm=MM_ROWS, out_dtype=BF, name="b_dwf")
    g_b_in = jnp.concatenate([g_qkv, g_f[:, :N_HEADS]], axis=1)
    partial[2] = g_b_in.reshape(D_MODEL, N_CHIPS, -1).transpose(1, 0, 2)
    dh2, dh2b, g["b_norm"] = _rms_bwd(h2, w["b_norm"], dn2, dh3, name="b_dnorm")

    dh1, dh1b, dg_f0, partial[4], partial[6] = _ffn_bwd(dh2, dh2b, ffn0, w["ffn_norm"][0:1], w["ffn_w_gu"][0], w["ffn_w_down"][0], 0,
                                                      ride=lambda call: ride(call, [2]))
    g["ffn_norm"] = jnp.concatenate([dg_f0, dg_f1], axis=0)

    views = tuple(sorted(set(dils)))
    do_a = dict(zip(views, _mm_nt(dh1b, w["a_w_out"], tm=MM_ROWS, to=1024, tn=1024, out_dtype=BF, name="a_do", views=views)))
    partial[1] = _mm_tn(o_a, dh1b, tk=1024, tn=1024, tm=MM_ROWS, out_dtype=BF, name="a_dwout")
    riders = {0: [4], 1: [6, 1], 2: []}
    parts = []
    for gi, (window, dil) in enumerate(DILATED_PATTERNS):
        pv = projs[gi]
        res = ride(lambda guest: _band_bwd(pv, pv, pv, block(0, dil), block(1, dil), block(2, dil), do_a[dil],
                                           mixed[dil][0], mixed[dil][1], dil=dil, T=128,
                                           window=window // dil, name=f"a_attn_bwd{gi}", guest=guest), riders[gi])
        parts += list(res)
    dproj = _assemble(parts, [True, True, False] * 3, rope_b, [d for _, d in DILATED_PATTERNS for _ in range(3)], name="a_dproj")
    partial[0] = _mm_tn(n0, dproj, tk=1024, tn=1024, tm=MM_ROWS, out_dtype=BF, name="a_dwin")
    dn0 = ride(lambda guest: _mm_nt(dproj, w["a_w_in"], tm=MM_ROWS, to=1024, tn=1024, out_dtype=F32, name="a_dn", guest=guest), [0])
    dx, _, g["a_norm"] = _rms_bwd(x, w["a_norm"], dn0, dh1, name="a_dnorm")
    return loss, dx, g, partial, landed


ANY = pl.BlockSpec(memory_space=pl.ANY)


def _place():
    x, y, c = lax.axis_index("x"), lax.axis_index("y"), lax.axis_index("c")
    chips = [(1 - x, y), (x, 1 - y), (1 - x, 1 - y)]
    return x, y, c, chips


def _shard_slice(ref, kind, rows, cols, s, half):
    hr = rows // 2
    if kind == "col":
        return ref.at[pl.ds(half * hr, hr), pl.ds(pl.multiple_of(s * cols, 128), cols)]
    if kind == "row":
        return ref.at[pl.ds(pl.multiple_of(s * rows + half * hr, 16), hr), :]
    return ref.at[s, pl.ds(half * hr, hr), :]


def _whole_shape(kind, rows, cols):
    return {"col": (rows, N_CHIPS * cols), "row": (N_CHIPS * rows, cols), "stack": (N_CHIPS, rows, cols)}[kind]


def _own_block(kind, rows, tr, cols):
    per = rows // tr

    def spec(half_rows):
        off = (lambda p: 0) if half_rows is None else (lambda p: p[1] * (half_rows // tr))
        if kind == "col":
            return pl.BlockSpec((tr, cols), lambda i, p: (off(p) + i, p[0]))
        if kind == "row":
            return pl.BlockSpec((tr, cols), lambda i, p: (p[0] * per + off(p) + i, 0))
        return pl.BlockSpec((None, tr, cols), lambda i, p: (p[0], off(p) + i, 0))
    return spec


def _place_shard(shards, layer, kind, place, *, name):
    _, rows, cols = shards.shape
    tr = 256 if rows % 256 == 0 else rows // 2

    def body(p_ref, s_ref, o_ref):
        o_ref[...] = s_ref[...].astype(BF)

    return pl.pallas_call(
        body,
        grid_spec=pltpu.PrefetchScalarGridSpec(
            num_scalar_prefetch=1, grid=(rows // tr,),
            in_specs=[pl.BlockSpec((None, tr, cols), lambda i, p: (layer, i, 0))],
            out_specs=_own_block(kind, rows, tr, cols)(None)),
        out_shape=jax.ShapeDtypeStruct(_whole_shape(kind, rows, cols), BF),
        name=name, compiler_params=_params("arbitrary"),
    )(place, shards)


def _gather_weights(placed, kinds, dims):
    nw = len(placed)

    def body(*refs):
        dst = refs[nw:2 * nw]
        send_sems, recv_sems = refs[2 * nw:]
        x, y, c, chips = _place()
        me = 2 * x + y
        sibling = (x, y, 1 - c)

        def copy(wi, k, s, half, to):
            p = _shard_slice(dst[wi], kinds[wi], dims[wi][0], dims[wi][1], s, half)
            return pltpu.make_async_remote_copy(src_ref=p, dst_ref=p, send_sem=send_sems.at[wi * 6 + k],
                                                recv_sem=recv_sems.at[wi * 6 + k], device_id=to, device_id_type=MESH)

        first, passed = [], []
        for wi in range(nw):
            for j, chip in enumerate(chips):
                cp = copy(wi, j, me, c, (*chip, c))
                cp.start()
                first.append(cp)
        for wi in range(nw):
            for j, chip in enumerate(chips):
                s = 2 * chip[0] + chip[1]
                copy(wi, j, s, c, (x, y, c)).wait_recv()
                cp = copy(wi, 3 + j, s, c, sibling)
                cp.start()
                passed.append(cp)
        for wi in range(nw):
            for j, chip in enumerate(chips):
                s = 2 * chip[0] + chip[1]
                copy(wi, 3 + j, s, 1 - c, (x, y, c)).wait_recv()
        for cp in first + passed:
            cp.wait_send()

    return pl.pallas_call(
        body, in_specs=[ANY] * nw, out_specs=[ANY] * nw,
        out_shape=[jax.ShapeDtypeStruct(p.shape, p.dtype) for p in placed],
        input_output_aliases={wi: wi for wi in range(nw)},
        scratch_shapes=[pltpu.SemaphoreType.DMA((nw * 6,)), pltpu.SemaphoreType.DMA((nw * 6,))],
        name="gather_weights",
    )(*placed)


def _fetch_guest(placed, kinds, dims):
    nw = len(placed)

    def copies(dst, send_sems, recv_sems, incoming):
        x, y, c, chips = _place()
        out = []
        for wi in range(nw):
            for j, chip in enumerate(chips):
                s = 2 * chip[0] + chip[1] if incoming else 2 * x + y
                to = (x, y, c) if incoming else (*chip, c)
                for half in range(2):
                    p = _shard_slice(dst[wi], kinds[wi], dims[wi][0], dims[wi][1], s, half)
                    k = wi * 6 + 2 * j + half
                    out.append(pltpu.make_async_remote_copy(src_ref=p, dst_ref=p, send_sem=send_sems.at[k],
                                                            recv_sem=recv_sems.at[k], device_id=to, device_id_type=MESH))
        return out

    def start(src, dst, sems):
        for cp in copies(dst, sems[0], sems[1], False):
            cp.start()

    def finish(src, dst, sems):
        for cp in copies(dst, sems[0], sems[1], True):
            cp.wait_recv()
        for cp in copies(dst, sems[0], sems[1], False):
            cp.wait_send()

    return dict(args=list(placed), out_shape=[jax.ShapeDtypeStruct(p.shape, p.dtype) for p in placed],
                scratch=[pltpu.SemaphoreType.DMA((nw * 6,)), pltpu.SemaphoreType.DMA((nw * 6,))],
                start=start, finish=finish, in_place=True)


def _scatter_guest(partials, kinds, dims):
    nw = len(partials)

    def copies(src, send_sems, recv_sems, dst):
        x, y, c, chips = _place()
        me = 2 * x + y
        out = []
        for wi in range(nw):
            rows, cols = dims[wi]

            def part(s, half, wi=wi, rows=rows, cols=cols):
                return _shard_slice(src[wi], kinds[wi], rows, cols, s, half)

            for j, chip in enumerate(chips):
                s = 2 * chip[0] + chip[1]
                for half in range(2):
                    slot = 2 * j + (c if half == 0 else 1 - c)
                    out.append(pltpu.make_async_remote_copy(
                        src_ref=part(s, half), dst_ref=dst[wi].at[slot],
                        send_sem=send_sems.at[wi * 7 + 2 * j + half], recv_sem=recv_sems.at[wi * 7 + slot],
                        device_id=(*chip, half), device_id_type=MESH))
            out.append(pltpu.make_async_remote_copy(
                src_ref=part(me, 1 - c), dst_ref=dst[wi].at[6],
                send_sem=send_sems.at[wi * 7 + 6], recv_sem=recv_sems.at[wi * 7 + 6],
                device_id=(x, y, 1 - c), device_id_type=MESH))
        return out

    def start(src, dst, sems):
        for cp in copies(src, sems[0], sems[1], dst):
            cp.start()

    def finish(src, dst, sems):
        x, y, c, _ = _place()
        for wi in range(nw):
            for slot in range(7):
                pltpu.make_async_remote_copy(
                    src_ref=dst[wi].at[slot], dst_ref=dst[wi].at[slot],
                    send_sem=sems[0].at[wi * 7 + slot], recv_sem=sems[1].at[wi * 7 + slot],
                    device_id=(x, y, c), device_id_type=MESH).wait_recv()
        for cp in copies(src, sems[0], sems[1], dst):
            cp.wait_send()

    return dict(args=list(partials), out_shape=[jax.ShapeDtypeStruct((7, d[0] // 2, d[1]), BF) for d in dims],
                scratch=[pltpu.SemaphoreType.DMA((nw * 7,)), pltpu.SemaphoreType.DMA((nw * 7,))],
                start=start, finish=finish)


def _sum_slots(slots, partial, kind, dims, place, *, name, into=None, layer=None, n_layers=1):
    rows, cols = dims
    hr = rows // 2
    tr = hr if 8 * hr * cols * 2 <= 6 * 1024 * 1024 else 128
    assert hr % tr == 0

    def body(p_ref, b_ref, own_ref, *rest):
        o_ref = rest[-1]
        acc = own_ref[...].astype(F32)
        for k in range(7):
            acc = acc + b_ref[k].astype(F32)
        o_ref[...] = acc

    half = lambda p: p[1] * (hr // tr)
    if n_layers == 1:
        out_spec = pl.BlockSpec((tr, cols), lambda i, p: (half(p) + i, 0))
        out_shape = jax.ShapeDtypeStruct((rows, cols), F32)
    else:
        out_spec = pl.BlockSpec((None, tr, cols), lambda i, p: (layer, half(p) + i, 0))
        out_shape = jax.ShapeDtypeStruct((n_layers, rows, cols), F32)
    in_specs = [pl.BlockSpec((7, tr, cols), lambda i, p: (0, i, 0)), _own_block(kind, rows, tr, cols)(hr)]
    args = [place, slots, partial]
    aliases = {}
    if into is not None:
        in_specs.append(ANY)
        args.append(into)
        aliases = {3: 0}
    return pl.pallas_call(
        body,
        grid_spec=pltpu.PrefetchScalarGridSpec(num_scalar_prefetch=1, grid=(hr // tr,), in_specs=in_specs, out_specs=out_spec),
        out_shape=out_shape, input_output_aliases=aliases, name=name, compiler_params=_params("arbitrary"),
    )(*args)


def _pair_exchange(bufs, members):
    nw = len(members)

    def body(*refs):
        dst = refs[len(bufs):2 * len(bufs)]
        send_sems, recv_sems = refs[2 * len(bufs):]
        x, y, c, _ = _place()

        def rows_of(wi, half):
            bi, l = members[wi]
            ref = dst[bi] if l is None else dst[bi].at[l]
            hr = ref.shape[0] // 2
            return ref.at[pl.ds(pl.multiple_of(half * hr, 8), hr), :]

        def copy(wi, half, to):
            p = rows_of(wi, half)
            return pltpu.make_async_remote_copy(src_ref=p, dst_ref=p, send_sem=send_sems.at[wi], recv_sem=recv_sems.at[wi],
                                                device_id=to, device_id_type=MESH)

        sent = []
        for wi in range(nw):
            cp = copy(wi, c, (x, y, 1 - c))
            cp.start()
            sent.append(cp)
        for wi in range(nw):
            copy(wi, 1 - c, (x, y, c)).wait_recv()
        for cp in sent:
            cp.wait_send()

    return pl.pallas_call(
        body, in_specs=[ANY] * len(bufs), out_specs=[ANY] * len(bufs),
        out_shape=[jax.ShapeDtypeStruct(b.shape, b.dtype) for b in bufs],
        input_output_aliases={i: i for i in range(len(bufs))},
        scratch_shapes=[pltpu.SemaphoreType.DMA((nw,)), pltpu.SemaphoreType.DMA((nw,))],
        name="pair_exchange",
    )(*bufs)


SMALL_ROWS = 8


def _allreduce_small(v, *, name):
    assert v.shape == (SMALL_ROWS, D_MODEL)

    def body(v_ref, o_ref, buf, send_sems, recv_sems):
        x, y, c, _ = _place()
        me = 4 * x + 2 * y + c
        buf[me] = v_ref[...]
        sent = []
        for k in range(1, 8):
            bx, by, bc = (k >> 2) & 1, (k >> 1) & 1, k & 1
            peer = (1 - x if bx else x, 1 - y if by else y, 1 - c if bc else c)
            cp = pltpu.make_async_remote_copy(src_ref=v_ref, dst_ref=buf.at[me], send_sem=send_sems.at[k - 1],
                                              recv_sem=recv_sems.at[k - 1], device_id=peer, device_id_type=MESH)
            cp.start()
            sent.append(cp)
        for k in range(1, 8):
            bx, by, bc = (k >> 2) & 1, (k >> 1) & 1, k & 1
            peer = 4 * (1 - x if bx else x) + 2 * (1 - y if by else y) + (1 - c if bc else c)
            pltpu.make_async_remote_copy(src_ref=v_ref, dst_ref=buf.at[peer], send_sem=send_sems.at[k - 1],
                                         recv_sem=recv_sems.at[k - 1], device_id=(x, y, c), device_id_type=MESH).wait_recv()
        for cp in sent:
            cp.wait_send()
        acc = buf[0]
        for d in range(1, 8):
            acc = acc + buf[d]
        o_ref[...] = acc

    vmem = pl.BlockSpec(memory_space=pltpu.VMEM)
    return pl.pallas_call(
        body, in_specs=[vmem], out_specs=vmem, out_shape=jax.ShapeDtypeStruct(v.shape, F32),
        scratch_shapes=[pltpu.VMEM((8,) + v.shape, F32), pltpu.SemaphoreType.DMA((7,)), pltpu.SemaphoreType.DMA((7,))],
        name=name,
    )(v)


def _adamw(w, g, m, v, *, name):
    R, C = w.shape
    tr = R
    if R * C * 4 > 1024 * 1024:
        tr = max(t for t in range(8, R, 8) if R % t == 0 and t * C * 4 <= 1024 * 1024)

    def body(w_ref, g_ref, m_ref, v_ref, d_ref, m2_ref, v2_ref):
        gg = g_ref[...]
        m2 = ADAM_B1 * m_ref[...] + (1.0 - ADAM_B1) * gg
        v2 = ADAM_B2 * v_ref[...] + (1.0 - ADAM_B2) * jnp.square(gg)
        m_hat = m2 / (1.0 - ADAM_B1 ** ADAM_STEP)
        v_hat = v2 / (1.0 - ADAM_B2 ** ADAM_STEP)
        d_ref[...] = -ADAM_LR * (m_hat / (jnp.sqrt(v_hat) + ADAM_EPS) + ADAM_WD * w_ref[...])
        m2_ref[...] = m2
        v2_ref[...] = v2

    blk = pl.BlockSpec((tr, C), lambda i: (i, 0))
    out = jax.ShapeDtypeStruct((R, C), F32)
    return pl.pallas_call(
        body, grid=(R // tr,), in_specs=[blk] * 4, out_specs=[blk] * 3, out_shape=[out] * 3,
        name=name, compiler_params=_params("parallel"),
    )(w, g, m, v)


WEIGHT_ORDER = ("a_norm", "a_w_in", "a_w_out", "b_norm", "b_w_in", "b_f", "b_w_out", "ffn_norm", "ffn_w_gu",
                "ffn_w_down", "final_norm")
MATRICES = (("a_w_in", 0, "col"), ("a_w_out", 0, "row"), ("b_w_in", 0, "stack"), ("b_w_out", 0, "row"),
            ("ffn_w_gu", 0, "col"), ("ffn_w_gu", 1, "col"), ("ffn_w_down", 0, "row"), ("ffn_w_down", 1, "row"))
MATRIX_GROUPS = ([0], [1], [2], [3], [4, 5], [6, 7])
GROUP_NAMES = ("a_w_in", "a_w_out", "b_w_in", "b_w_out", "ffn_w_gu", "ffn_w_down")
QKV_COLS = 3 * N_HEADS * HEAD_DIM


def kernel(x, a_norm, a_w_in, a_w_out, b_norm, b_w_in, b_f, b_w_out, ffn_norm, ffn_w_gu, ffn_w_down, final_norm, loss_target, m_a_norm, m_a_w_in, m_a_w_out, m_b_norm, m_b_w_in, m_b_f, m_b_w_out, m_ffn_norm, m_ffn_w_gu, m_ffn_w_down, m_final_norm, v_a_norm, v_a_w_in, v_a_w_out, v_b_norm, v_b_w_in, v_b_f, v_b_w_out, v_ffn_norm, v_ffn_w_gu, v_ffn_w_down, v_final_norm):
    given = dict(a_norm=a_norm, a_w_in=a_w_in, a_w_out=a_w_out, b_norm=b_norm, b_w_in=b_w_in, b_f=b_f, b_w_out=b_w_out,
                 ffn_norm=ffn_norm, ffn_w_gu=ffn_w_gu, ffn_w_down=ffn_w_down, final_norm=final_norm)
    mom_m = dict(a_norm=m_a_norm, a_w_in=m_a_w_in, a_w_out=m_a_w_out, b_norm=m_b_norm, b_w_in=m_b_w_in, b_f=m_b_f,
                 b_w_out=m_b_w_out, ffn_norm=m_ffn_norm, ffn_w_gu=m_ffn_w_gu, ffn_w_down=m_ffn_w_down, final_norm=m_final_norm)
    mom_v = dict(a_norm=v_a_norm, a_w_in=v_a_w_in, a_w_out=v_a_w_out, b_norm=v_b_norm, b_w_in=v_b_w_in, b_f=v_b_f,
                 b_w_out=v_b_w_out, ffn_norm=v_ffn_norm, ffn_w_gu=v_ffn_w_gu, ffn_w_down=v_ffn_w_down, final_norm=v_final_norm)
    chip = 2 * lax.axis_index("x") + lax.axis_index("y")
    core = lax.axis_index("c")
    bn_cols = b_norm.shape[1]

    placed = lax.dynamic_update_slice(jnp.zeros((SMALL_ROWS, D_MODEL), F32), b_norm, (0, chip * bn_cols))
    placed = placed * (core == 0).astype(F32)
    b_norm_full = _allreduce_small(placed, name="gather_b_norm")[0:1]

    place = jnp.stack([chip, core]).astype(jnp.int32)
    kinds = [k for _, _, k in MATRICES]
    dims = [given[n].shape[1:] for n, _, _ in MATRICES]
    placed = [_place_shard(given[n], l, k, place, name=f"place_{n}{l}") for n, l, k in MATRICES]
    first = _gather_weights(placed[:1], kinds[:1], dims[:1])
    mats = dict(enumerate(list(first) + placed[1:]))
    gate_cols = b_f.shape[1]
    w = dict(a_norm=a_norm, a_w_in=mats[0], b_norm=b_norm_full,
             b_f=jnp.pad(b_f, ((0, 0), (0, GATE_LANES - gate_cols))), ffn_norm=ffn_norm,
             final_norm=final_norm.reshape(1, D_MODEL))

    def fetch(indices, bufs):
        return _fetch_guest(bufs, [kinds[i] for i in indices], [dims[i] for i in indices])

    def exchange(indices, parts):
        return _scatter_guest(parts, [kinds[i] for i in indices], [dims[i] for i in indices])

    loss, dx, g, partials, slots = _local_step(x[0], loss_target[0], w, mats, fetch, exchange)
    bufs, members = [], []
    for group in MATRIX_GROUPS:
        buf = None
        for l, wi in enumerate(group):
            n = MATRICES[wi][0]
            buf = _sum_slots(slots[wi], partials[wi], kinds[wi], dims[wi], place, name=f"sum_{n}{l}", into=buf,
                             layer=l, n_layers=len(group))
            members.append((len(bufs), l if len(group) > 1 else None))
        bufs.append(buf)
    reduced = dict(zip(GROUP_NAMES, _pair_exchange(bufs, members)))

    small = jnp.concatenate([g["a_norm"], g["b_norm"], g["ffn_norm"], g["final_norm"],
                             jnp.pad(g["b_f"], ((0, 0), (0, D_MODEL - GATE_LANES))),
                             jnp.zeros((SMALL_ROWS - 6, D_MODEL), F32)], axis=0)
    small = _allreduce_small(small, name="allreduce_small")
    grads = dict(reduced)
    grads["a_norm"] = small[0:1]
    grads["b_norm"] = lax.dynamic_slice(small, (1, chip * bn_cols), (1, bn_cols))
    grads["ffn_norm"] = small[2:4]
    grads["final_norm"] = small[4]
    grads["b_f"] = small[5:6, :gate_cols]

    out_g, out_d, out_m, out_v = [], [], [], []
    for n in WEIGHT_ORDER:
        shape = given[n].shape
        two_d = (1, shape[0]) if len(shape) == 1 else (-1, shape[-1])
        d, m2, v2 = _adamw(given[n].reshape(two_d), grads[n].reshape(two_d), mom_m[n].reshape(two_d),
                           mom_v[n].reshape(two_d), name=f"adamw_{n}")
        out_g.append(grads[n].reshape(shape))
        out_d.append(d.reshape(shape))
        out_m.append(m2.reshape(shape))
        out_v.append(v2.reshape(shape))

    total = lax.psum(loss[0, 0], MESH_AXES)
    return (total, dx[None], *out_g, *out_d, *out_m, *out_v)
```

```python
import functools

import jax
import jax.numpy as jnp
from jax import lax
from jax.experimental import pallas as pl
from jax.experimental.pallas import tpu as pltpu

F32 = jnp.float32
BF = jnp.bfloat16

D_MODEL = 1024
N_HEADS = 16
HEAD_DIM = 64
D_FF = 2816
DILATED_PATTERNS = ((128, 1), (512, 4), (2048, 16))
ROT_DIM = 16
ROPE_THETA = 500000.0
RMS_EPS = 1e-6
NEG_INF = -1e30
ATTN_SCALE = HEAD_DIM ** -0.5
GATE_LANES = 128
N_CHIPS = 4
MESH_AXES = ("x", "y", "c")
MESH = pl.DeviceIdType.MESH

ADAM_LR = 0.001
ADAM_B1 = 0.9
ADAM_B2 = 0.999
ADAM_EPS = 1e-08
ADAM_WD = 0.01
ADAM_STEP = 10

VMEM_LIMIT_BYTES = 56 * 1024 * 1024


def _params(*sem):
    return pltpu.CompilerParams(dimension_semantics=sem, vmem_limit_bytes=VMEM_LIMIT_BYTES)


def _hosted_call(body, *, grid, in_specs, out_specs, out_shape, scratch_shapes, args, name, guest=None, schedule=()):
    params = _params(*(["arbitrary"] * len(grid)))
    ns = len(schedule)

    def call(kernel, in_specs, out_specs, out_shape, scratch_shapes, aliases, args):
        spec = pltpu.PrefetchScalarGridSpec(num_scalar_prefetch=ns, grid=grid, in_specs=in_specs, out_specs=out_specs,
                                            scratch_shapes=scratch_shapes)
        return pl.pallas_call(kernel, grid_spec=spec, out_shape=out_shape, input_output_aliases=aliases, name=name,
                              compiler_params=params)(*schedule, *args)

    if guest is None:
        return call(body, in_specs, out_specs, out_shape, scratch_shapes, {}, args)
    n_in, n_out, n_scr = ns + len(in_specs), len(out_specs), len(scratch_shapes)
    g_in, g_out = len(guest["args"]), len(guest["out_shape"])
    any_spec = pl.BlockSpec(memory_space=pl.ANY)

    def wrapped(*refs):
        i1 = n_in + g_in
        o1 = i1 + n_out
        o2 = o1 + g_out
        s1 = o2 + n_scr
        guest_refs = (refs[n_in:i1], refs[o1:o2], refs[s1:])
        ids = [pl.program_id(d) for d in range(len(grid))]
        first = functools.reduce(jnp.logical_and, [i == 0 for i in ids])
        last = functools.reduce(jnp.logical_and, [i == g - 1 for i, g in zip(ids, grid)])

        @pl.when(first)
        def _():
            guest["start"](*guest_refs)

        body(*refs[:n_in], *refs[i1:o1], *refs[o2:s1])

        @pl.when(last)
        def _():
            guest["finish"](*guest_refs)

    aliases = {n_in + k: n_out + k for k in range(g_in)} if guest.get("in_place") else {}
    return call(wrapped, list(in_specs) + [any_spec] * g_in, list(out_specs) + [any_spec] * g_out,
                list(out_shape) + list(guest["out_shape"]), list(scratch_shapes) + list(guest["scratch"]), aliases,
                list(args) + list(guest["args"]))


def _rope_rotate(t, cos, sin_a, sin_b):
    outs = []
    for cidx in range(t.shape[1] // 128):
        tc = t[:, cidx * 128:(cidx + 1) * 128]
        outs.append(tc * cos + pltpu.roll(tc, 120, 1) * sin_a + pltpu.roll(tc, 8, 1) * sin_b)
    return jnp.concatenate(outs, axis=1)


def _mm_nn(a, b, *, tm, tn, out_dtype, name, resid=None, rope=None, guest=None, groups=None):
    M, K = a.shape
    N = b.shape[1]
    assert M % tm == 0 and N % tn == 0 and b.shape[0] == K
    n_in = 2 + (resid is not None) + (3 if rope is not None else 0)
    if groups is not None:
        assert rope is not None and N == 3 * tn * len(groups)

    def body(*refs):
        a_ref, b_ref = refs[0], refs[1]
        o_ref = refs[n_in]
        acc = jnp.dot(a_ref[...], b_ref[...], preferred_element_type=F32)
        if resid is not None:
            acc = acc + refs[2][...]
        if groups is not None:
            cos_ref, sa_ref, sb_ref = refs[n_in - 3:n_in]
            j = pl.program_id(1)
            for g, d in enumerate(groups):
                for is_v in (False, True):
                    @pl.when(jnp.logical_and(j // 3 == g, (j % 3 == 2) == is_v))
                    def _(g=g, d=d, is_v=is_v):
                        val = acc if is_v else _rope_rotate(acc, cos_ref[...], sa_ref[...], sb_ref[...])
                        if d == 1:
                            refs[n_in + g][...] = val.astype(out_dtype)
                        else:
                            _to_view(val, refs[-1], refs[n_in + g], d, tn)
        elif rope is not None:
            cos_ref, sa_ref, sb_ref = refs[n_in - 3:n_in]
            j = pl.program_id(1)

            @pl.when(j % 3 != 2)
            def _():
                o_ref[...] = _rope_rotate(acc, cos_ref[...], sa_ref[...], sb_ref[...]).astype(out_dtype)

            @pl.when(j % 3 == 2)
            def _():
                o_ref[...] = acc.astype(out_dtype)
        else:
            o_ref[...] = acc.astype(out_dtype)

    in_specs = [pl.BlockSpec((tm, K), lambda i, j: (i, 0)), pl.BlockSpec((K, tn), lambda i, j: (0, j))]
    args = [a, b]
    if resid is not None:
        in_specs.append(pl.BlockSpec((tm, tn), lambda i, j: (i, j)))
        args.append(resid)
    if rope is not None:
        assert tn == 1024
        for t in rope:
            in_specs.append(pl.BlockSpec((tm, 128), lambda i, j: (i, 0)))
            args.append(t)
    if groups is None:
        out_specs = [pl.BlockSpec((tm, tn), lambda i, j: (i, j))]
        out_shape = [jax.ShapeDtypeStruct((M, N), out_dtype)]
        scratch = []
    else:
        out_specs = [pl.BlockSpec((tm // d, d * tn), lambda i, j, g=g: (i, jnp.clip(j - 3 * g, 0, 2)))
                     for g, d in enumerate(groups)]
        out_shape = [jax.ShapeDtypeStruct((M // d, d * 3 * tn), out_dtype) for d in groups]
        scratch = [pltpu.VMEM((tn // 128, tm, 128), F32)]
    outs = _hosted_call(body, grid=(M // tm, N // tn), in_specs=in_specs, out_specs=out_specs, out_shape=out_shape,
                        scratch_shapes=scratch, args=args, name=name, guest=guest)
    nout = len(out_shape)
    res = outs[0] if groups is None else list(outs[:nout])
    return res if guest is None else (res, outs[nout:])


def _mm_nt(a, b, *, tm, to, tn, out_dtype, name, add=None, guest=None, views=(1,)):
    M, N = a.shape
    O = b.shape[0]
    assert M % tm == 0 and O % to == 0 and N % tn == 0 and b.shape[1] == N
    nk = N // tn

    def body(*refs):
        a_ref, b_ref = refs[0], refs[1]
        n_in = 2 + (add is not None)
        o_refs = refs[n_in:n_in + len(views)]
        acc_ref = refs[n_in + len(views)]
        k = pl.program_id(2)

        @pl.when(k == 0)
        def _():
            if add is not None:
                acc_ref[...] = refs[2][...]
            else:
                acc_ref[...] = jnp.zeros_like(acc_ref)

        acc_ref[...] += lax.dot_general(a_ref[...], b_ref[...], (((1,), (1,)), ((), ())),
                                        preferred_element_type=F32)

        @pl.when(k == nk - 1)
        def _():
            for o_ref, d in zip(o_refs, views):
                if d == 1:
                    o_ref[...] = acc_ref[...].astype(out_dtype)
                else:
                    _to_view(acc_ref[...], refs[-1], o_ref, d, to)

    in_specs = [pl.BlockSpec((tm, tn), lambda i, j, k: (i, k)), pl.BlockSpec((to, tn), lambda i, j, k: (j, k))]
    args = [a, b]
    if add is not None:
        in_specs.append(pl.BlockSpec((tm, to), lambda i, j, k: (i, j)))
        args.append(add)
    assert views == (1,) or (to == O and to % 128 == 0)
    scratch = [pltpu.VMEM((tm, to), F32)] + ([pltpu.VMEM((to // 128, tm, 128), F32)] if views != (1,) else [])
    outs = _hosted_call(
        body, grid=(M // tm, O // to, nk), in_specs=in_specs,
        out_specs=[pl.BlockSpec((tm, to), lambda i, j, k: (i, j)) if d == 1 else _view_spec(tm, d, to) for d in views],
        out_shape=[jax.ShapeDtypeStruct((M // d, d * O), out_dtype) for d in views],
        scratch_shapes=scratch, args=args, name=name, guest=guest)
    nv = len(views)
    res = outs[0] if nv == 1 else list(outs[:nv])
    return res if guest is None else (res, outs[nv:])


def _mm_tn(a, b, *, tk, tn, tm, out_dtype, name):
    M, K = a.shape
    N = b.shape[1]
    assert M % tm == 0 and K % tk == 0 and N % tn == 0 and b.shape[0] == M
    nm = M // tm

    def body(a_ref, b_ref, o_ref, acc_ref):
        m = pl.program_id(2)

        @pl.when(m == 0)
        def _():
            acc_ref[...] = jnp.zeros_like(acc_ref)

        acc_ref[...] += lax.dot_general(a_ref[...], b_ref[...], (((0,), (0,)), ((), ())),
                                        preferred_element_type=F32)

        @pl.when(m == nm - 1)
        def _():
            o_ref[...] = acc_ref[...].astype(out_dtype)

    return pl.pallas_call(
        body, grid=(K // tk, N // tn, nm),
        in_specs=[pl.BlockSpec((tm, tk), lambda i, j, m: (m, i)), pl.BlockSpec((tm, tn), lambda i, j, m: (m, j))],
        out_specs=pl.BlockSpec((tk, tn), lambda i, j, m: (i, j)),
        out_shape=jax.ShapeDtypeStruct((K, N), out_dtype),
        scratch_shapes=[pltpu.VMEM((tk, tn), F32)], name=name,
        compiler_params=_params("parallel", "parallel", "arbitrary"),
    )(a, b)


ROW_TILE = 512
MM_ROWS = 1024


def _rms_fwd(x, g, *, name):
    S, Dm = x.shape

    def body(x_ref, g_ref, o_ref):
        xf = x_ref[...]
        r = lax.rsqrt(jnp.mean(xf * xf, axis=-1, keepdims=True) + RMS_EPS)
        o_ref[...] = (xf * r * g_ref[...]).astype(BF)

    return pl.pallas_call(
        body, grid=(S // ROW_TILE,),
        in_specs=[pl.BlockSpec((ROW_TILE, Dm), lambda i: (i, 0)), pl.BlockSpec((1, Dm), lambda i: (0, 0))],
        out_specs=pl.BlockSpec((ROW_TILE, Dm), lambda i: (i, 0)),
        out_shape=jax.ShapeDtypeStruct((S, Dm), BF), name=name, compiler_params=_params("parallel"),
    )(x, g)


def _rms_bwd(x, g, dn, dres, *, name):
    S, Dm = x.shape

    def body(x_ref, g_ref, dn_ref, dres_ref, dx_ref, dxb_ref, dg_ref):
        i = pl.program_id(0)
        xf = x_ref[...]
        r = lax.rsqrt(jnp.mean(xf * xf, axis=-1, keepdims=True) + RMS_EPS)
        xh = xf * r
        dnf = dn_ref[...]
        dyg = dnf * g_ref[...]
        dx = dres_ref[...] + r * (dyg - xh * jnp.mean(dyg * xh, axis=-1, keepdims=True))
        dx_ref[...] = dx
        dxb_ref[...] = dx.astype(BF)

        @pl.when(i == 0)
        def _():
            dg_ref[...] = jnp.zeros_like(dg_ref)

        dg_ref[...] += jnp.sum(dnf * xh, axis=0, keepdims=True)

    row = pl.BlockSpec((ROW_TILE, Dm), lambda i: (i, 0))
    vec = pl.BlockSpec((1, Dm), lambda i: (0, 0))
    return pl.pallas_call(
        body, grid=(S // ROW_TILE,), in_specs=[row, vec, row, row], out_specs=[row, row, vec],
        out_shape=[jax.ShapeDtypeStruct((S, Dm), F32), jax.ShapeDtypeStruct((S, Dm), BF),
                   jax.ShapeDtypeStruct((1, Dm), F32)],
        name=name, compiler_params=_params("arbitrary"),
    )(x, g, dn, dres)


def _loss_head(h, g, tgt, *, name):
    S, Dm = h.shape

    def body(h_ref, g_ref, t_ref, loss_ref, dh_ref, dhb_ref, dg_ref):
        i = pl.program_id(0)
        xf = h_ref[...]
        r = lax.rsqrt(jnp.mean(xf * xf, axis=-1, keepdims=True) + RMS_EPS)
        xh = xf * r
        gv = g_ref[...]
        err = xh * gv - t_ref[...]
        dy = err * (1.0 / Dm)
        dyg = dy * gv
        dh = r * (dyg - xh * jnp.mean(dyg * xh, axis=-1, keepdims=True))
        dh_ref[...] = dh
        dhb_ref[...] = dh.astype(BF)

        @pl.when(i == 0)
        def _():
            dg_ref[...] = jnp.zeros_like(dg_ref)
            loss_ref[...] = jnp.zeros_like(loss_ref)

        dg_ref[...] += jnp.sum(dy * xh, axis=0, keepdims=True)
        part = 0.5 * jnp.sum(jnp.mean(err * err, axis=-1, keepdims=True), axis=0, keepdims=True)
        loss_ref[...] += jnp.broadcast_to(part, loss_ref.shape)

    row = pl.BlockSpec((ROW_TILE, Dm), lambda i: (i, 0))
    vec = pl.BlockSpec((1, Dm), lambda i: (0, 0))
    return pl.pallas_call(
        body, grid=(S // ROW_TILE,), in_specs=[row, vec, row],
        out_specs=[pl.BlockSpec((1, 128), lambda i: (0, 0)), row, row, vec],
        out_shape=[jax.ShapeDtypeStruct((1, 128), F32), jax.ShapeDtypeStruct((S, Dm), F32),
                   jax.ShapeDtypeStruct((S, Dm), BF), jax.ShapeDtypeStruct((1, Dm), F32)],
        name=name, compiler_params=_params("arbitrary"),
    )(h, g, tgt)


SWIGLU_ROWS = 512


def _swiglu_fwd(gu, *, name):
    S = gu.shape[0]

    def body(g_ref, u_ref, o_ref):
        g = g_ref[...].astype(F32)
        sig = 1.0 / (1.0 + jnp.exp(-g))
        o_ref[...] = (g * sig * u_ref[...].astype(F32)).astype(BF)

    return pl.pallas_call(
        body, grid=(S // SWIGLU_ROWS,),
        in_specs=[pl.BlockSpec((SWIGLU_ROWS, D_FF), lambda i: (i, 0)), pl.BlockSpec((SWIGLU_ROWS, D_FF), lambda i: (i, 1))],
        out_specs=pl.BlockSpec((SWIGLU_ROWS, D_FF), lambda i: (i, 0)),
        out_shape=jax.ShapeDtypeStruct((S, D_FF), BF), name=name, compiler_params=_params("parallel"),
    )(gu, gu)


def _swiglu_bwd(gu, dact, *, name):
    S = gu.shape[0]

    def body(g_ref, u_ref, d_ref, o_ref):
        g = g_ref[...].astype(F32)
        u = u_ref[...].astype(F32)
        d = d_ref[...].astype(F32)
        sig = 1.0 / (1.0 + jnp.exp(-g))
        o_ref[:, :D_FF] = (d * u * sig * (1.0 + g * (1.0 - sig))).astype(BF)
        o_ref[:, D_FF:] = (d * g * sig).astype(BF)

    return pl.pallas_call(
        body, grid=(S // SWIGLU_ROWS,),
        in_specs=[pl.BlockSpec((SWIGLU_ROWS, D_FF), lambda i: (i, 0)), pl.BlockSpec((SWIGLU_ROWS, D_FF), lambda i: (i, 1)),
                  pl.BlockSpec((SWIGLU_ROWS, D_FF), lambda i: (i, 0))],
        out_specs=pl.BlockSpec((SWIGLU_ROWS, 2 * D_FF), lambda i: (i, 0)),
        out_shape=jax.ShapeDtypeStruct((S, 2 * D_FF), BF), name=name, compiler_params=_params("parallel"),
    )(gu, gu, dact)


def _band_masks(T, n):
    row = lax.broadcasted_iota(jnp.int32, (T, T), 0)
    col = lax.broadcasted_iota(jnp.int32, (T, T), 1)
    return jnp.logical_and(col >= row, n > 0), col <= row


def _band_fwd(qa, ka, va, qcb, kcb, vcb, *, dil, T, window, name, guest=None):
    L = qa.shape[0]
    nq = L // T
    assert window == T
    nt = (((1,), (1,)), ((), ()))

    def body(q_ref, kp_ref, kc_ref, vp_ref, vc_ref, o_ref, lse_ref):
        valid_prev, valid_cur = _band_masks(T, pl.program_id(1))
        lane = lax.broadcasted_iota(jnp.int32, (T, 128), 1)
        low = lane < HEAD_DIM
        ones = jnp.ones((T, 128), BF)
        lse = jnp.zeros((T, 128), F32)
        def scores(h):
            ps = slice((h // 2) * 128, (h // 2 + 1) * 128)
            qp = q_ref[:, ps] * jnp.asarray(ATTN_SCALE, BF)
            qm = jnp.where(low if h % 2 == 0 else jnp.logical_not(low), qp, jnp.zeros_like(qp))
            s0 = jnp.where(valid_prev, lax.dot_general(qm, kp_ref[:, ps], nt, preferred_element_type=F32), NEG_INF)
            s1 = jnp.where(valid_cur, lax.dot_general(qm, kc_ref[:, ps], nt, preferred_element_type=F32), NEG_INF)
            return s0, s1

        def softmax(s0, s1):
            m = jnp.maximum(jnp.max(s0, axis=1, keepdims=True), jnp.max(s1, axis=1, keepdims=True))
            return m, jnp.exp(s0 - m).astype(BF), jnp.exp(s1 - m).astype(BF)

        def weighted(h, p0, p1):
            ps = slice((h // 2) * 128, (h // 2 + 1) * 128)
            l = jnp.dot(p0, ones, preferred_element_type=F32) + jnp.dot(p1, ones, preferred_element_type=F32)
            acc = jnp.dot(p0, vp_ref[:, ps], preferred_element_type=F32) + jnp.dot(p1, vc_ref[:, ps], preferred_element_type=F32)
            return l, acc

        sc, pr, even = {}, {}, None
        for t in range(N_HEADS + 2):
            if t < N_HEADS:
                sc[t] = scores(t)
            done = None
            if t >= 2:
                m, p0, p1 = pr.pop(t - 2)
                done = (m,) + weighted(t - 2, p0, p1)
            if 1 <= t <= N_HEADS:
                pr[t - 1] = softmax(*sc.pop(t - 1))
            if done is not None:
                h = t - 2
                m, l, acc = done
                lse = jnp.where(lane == h, m + jnp.log(l), lse)
                if h % 2 == 0:
                    even = acc / l
                else:
                    o_ref[:, (h // 2) * 128:(h // 2 + 1) * 128] = jnp.where(low, even, acc / l)
        lse_ref[...] = lse

    def prev(n):
        return jnp.maximum(n - 1, 0)

    blk = lambda f, cb: pl.BlockSpec((T, 1024), lambda r, n: (f(n), cb(r)))
    same = lambda n: n
    outs = _hosted_call(
        body, grid=(dil, nq),
        in_specs=[blk(same, qcb), blk(prev, kcb), blk(same, kcb), blk(prev, vcb), blk(same, vcb)],
        out_specs=[pl.BlockSpec((T, 1024), lambda r, n: (n, r)), pl.BlockSpec((T, 128), lambda r, n: (n, r))],
        out_shape=[jax.ShapeDtypeStruct((L, dil * 1024), F32), jax.ShapeDtypeStruct((L, dil * 128), F32)],
        scratch_shapes=[], args=(qa, ka, ka, va, va), name=name, guest=guest)
    return outs if guest is None else (outs[:2], outs[2:])


def _band_bwd(qa, ka, va, qcb, kcb, vcb, doa, oa, lsea, *, dil, T, window, name, guest=None):
    L = qa.shape[0]
    nq = L // T
    assert window == T
    nt = (((1,), (1,)), ((), ()))
    tn = (((0,), (0,)), ((), ()))

    def body(q_ref, kp_ref, kc_ref, vp_ref, vc_ref, do_ref, o_ref, lse_ref, dq_ref, dk_ref, dv_ref, ck_sc, cv_sc):
        n = pl.program_id(1)

        @pl.when(n == 0)
        def _():
            ck_sc[...] = jnp.zeros_like(ck_sc)
            cv_sc[...] = jnp.zeros_like(cv_sc)

        @pl.when(n < nq)
        def _():
            valid_prev, valid_cur = _band_masks(T, n)
            low = lax.broadcasted_iota(jnp.int32, (T, 128), 1) < HEAD_DIM
            dot = functools.partial(lax.dot_general, preferred_element_type=F32)

            def pair(h):
                return slice((h // 2) * 128, (h // 2 + 1) * 128)

            def products(h):
                ps = pair(h)
                mask = low if h % 2 == 0 else jnp.logical_not(low)
                qp = q_ref[:, ps] * jnp.asarray(ATTN_SCALE, BF)
                dop = do_ref[:, ps]
                qm = jnp.where(mask, qp, jnp.zeros_like(qp))
                dom = jnp.where(mask, dop, jnp.zeros_like(dop))
                s0 = jnp.where(valid_prev, dot(qm, kp_ref[:, ps], nt), NEG_INF)
                s1 = jnp.where(valid_cur, dot(qm, kc_ref[:, ps], nt), NEG_INF)
                return qm, dom, s0, s1, dot(dom, vp_ref[:, ps], nt), dot(dom, vc_ref[:, ps], nt)

            def pointwise(h, qm, dom, s0, s1, dp0, dp1):
                ps = pair(h)
                mask = low if h % 2 == 0 else jnp.logical_not(low)
                prod = do_ref[:, ps].astype(F32) * o_ref[:, ps].astype(F32)
                delta = jnp.sum(jnp.where(mask, prod, 0.0), axis=1, keepdims=True)
                lse = lse_ref[:, h:h + 1]
                p0 = jnp.exp(s0 - lse)
                p1 = jnp.exp(s1 - lse)
                ds0 = (p0 * (dp0 - delta)).astype(BF)
                ds1 = (p1 * (dp1 - delta)).astype(BF)
                return qm, dom, p0.astype(BF), p1.astype(BF), ds0, ds1

            def gradients(h, qm, dom, p0, p1, ds0, ds1):
                ps = pair(h)
                dq = dot(ds0, kp_ref[:, ps], (((1,), (0,)), ((), ()))) + dot(ds1, kc_ref[:, ps], (((1,), (0,)), ((), ())))
                return dq, dot(ds0, qm, tn), dot(p0, dom, tn), dot(ds1, qm, tn), dot(p1, dom, tn)

            st1, st2, even = {}, {}, None
            for t in range(N_HEADS + 2):
                if t < N_HEADS:
                    st1[t] = products(t)
                done = gradients(t - 2, *st2.pop(t - 2)) if t >= 2 else None
                if 1 <= t <= N_HEADS:
                    st2[t - 1] = pointwise(t - 1, *st1.pop(t - 1))
                if done is not None:
                    h = t - 2
                    if h % 2 == 0:
                        even = done
                    else:
                        ps = pair(h)
                        dq_ref[:, ps] = (jnp.where(low, even[0], done[0]) * ATTN_SCALE).astype(BF)
                        dk_ref[:, ps] = (ck_sc[:, ps] + even[1] + done[1]).astype(BF)
                        dv_ref[:, ps] = (cv_sc[:, ps] + even[2] + done[2]).astype(BF)
                        ck_sc[:, ps] = even[3] + done[3]
                        cv_sc[:, ps] = even[4] + done[4]

        @pl.when(n == nq)
        def _():
            dk_ref[...] = ck_sc[...].astype(BF)
            dv_ref[...] = cv_sc[...].astype(BF)

    def cur(n):
        return jnp.minimum(n, nq - 1)

    def prev(n):
        return jnp.maximum(cur(n) - 1, 0)

    blk = lambda f, cb: pl.BlockSpec((T, 1024), lambda r, n: (f(n), cb(r)))
    own = lambda r: r
    outs = _hosted_call(
        body, grid=(dil, nq + 1),
        in_specs=[blk(cur, qcb), blk(prev, kcb), blk(cur, kcb), blk(prev, vcb), blk(cur, vcb), blk(cur, own), blk(cur, own),
                  pl.BlockSpec((T, 128), lambda r, n: (cur(n), r))],
        out_specs=[blk(cur, own), blk(lambda n: jnp.maximum(n - 1, 0), own), blk(lambda n: jnp.maximum(n - 1, 0), own)],
        out_shape=[jax.ShapeDtypeStruct((L, dil * 1024), BF)] * 3,
        scratch_shapes=[pltpu.VMEM((T, 1024), F32), pltpu.VMEM((T, 1024), F32)],
        args=(qa, ka, ka, va, va, doa, oa, lsea), name=name, guest=guest)
    return outs if guest is None else (outs[:3], outs[3:])


FOX_T = 512
FOX_TQ = 512
FOX_TK = 512
FOX_TQ_BWD = 512
FOX_ROWS = 256


def _fox_fwd(qkv, cT, *, name, guest=None):
    S = qkv.shape[0]
    T, TK, R = FOX_TQ, FOX_TK, FOX_ROWS
    nq = S // T
    nt = (((1,), (1,)), ((), ()))
    chains = [(h, rh) for h in range(N_HEADS) for rh in range(T // R)]
    pairs = [(n, j) for n in range(nq) for j in range((n * T + T - 1) // TK + 1)]
    schedule = [jnp.asarray([p[i] for p in pairs], jnp.int32) for i in range(2)]

    def body(n_tab, j_tab, q_ref, k_ref, v_ref, ct_ref, o_ref, lse_ref, m_sc, l_sc, acc_sc):
        n = n_tab[pl.program_id(0)]
        j = j_tab[pl.program_id(0)]
        last_j = (n * T + T - 1) // TK
        lane = lax.broadcasted_iota(jnp.int32, (R, 128), 1)
        low = lane < HEAD_DIM
        ones = jnp.ones((TK, 128), BF)

        @pl.when(j == 0)
        def _():
            m_sc[...] = jnp.full(m_sc.shape, NEG_INF, F32)
            l_sc[...] = jnp.zeros_like(l_sc)
            acc_sc[...] = jnp.zeros_like(acc_sc)

        def step(diagonal):
            def pair(h):
                return slice((h // 2) * 128, (h // 2 + 1) * 128)

            def rows(rh):
                return slice(rh * R, (rh + 1) * R)

            def scores(h, rh):
                qp = q_ref[rows(rh), pair(h)] * jnp.asarray(ATTN_SCALE, BF)
                qm = jnp.where(low if h % 2 == 0 else jnp.logical_not(low), qp, jnp.zeros_like(qp))
                s = lax.dot_general(qm, k_ref[:, pair(h)], nt, preferred_element_type=F32) - ct_ref[h:h + 1, :]
                if diagonal:
                    ahead = lax.broadcasted_iota(jnp.int32, (R, TK), 1) - lax.broadcasted_iota(jnp.int32, (R, TK), 0)
                    s = jnp.where(ahead <= n * T + rh * R - j * TK, s, NEG_INF)
                return s

            def softmax(h, rh, s):
                m_prev = m_sc[h, rows(rh), :]
                m_new = jnp.maximum(m_prev, jnp.max(s, axis=1, keepdims=True))
                p = jnp.exp(s - jnp.concatenate([m_new] * (TK // 128), axis=1)).astype(BF)
                return m_new, jnp.exp(m_prev - m_new), p

            def weighted(h, p):
                vx = jnp.concatenate([v_ref[:, pair(h)], ones], axis=1)
                return jnp.dot(p, vx, preferred_element_type=F32)

            sc, pr, even = {}, {}, {}
            nch = len(chains)
            for t in range(nch + 2):
                if t < nch:
                    sc[t] = scores(*chains[t])
                done = None
                if t >= 2:
                    m_new, alpha, p = pr.pop(t - 2)
                    done = (m_new, alpha, weighted(chains[t - 2][0], p))
                if 1 <= t <= nch:
                    pr[t - 1] = softmax(*chains[t - 1], sc.pop(t - 1))
                if done is not None:
                    h, rh = chains[t - 2]
                    m_new, alpha, pv = done
                    m_sc[h, rows(rh), :] = m_new
                    l_sc[h, rows(rh), :] = alpha * l_sc[h, rows(rh), :] + pv[:, 128:]
                    if h % 2 == 0:
                        even[rh] = (alpha, pv[:, :128])
                    else:
                        a0, pv0 = even.pop(rh)
                        acc = acc_sc[h // 2, rows(rh), :]
                        acc_sc[h // 2, rows(rh), :] = jnp.where(low, a0 * acc + pv0, alpha * acc + pv[:, :128])

        @pl.when(j < last_j)
        def _():
            step(False)

        @pl.when(j == last_j)
        def _():
            step(True)
            lane_t = lax.broadcasted_iota(jnp.int32, (T, 128), 1)
            low_t = lane_t < HEAD_DIM
            lse = jnp.zeros((T, 128), F32)
            for h in range(N_HEADS):
                lse = jnp.where(lane_t == h, m_sc[h] + jnp.log(l_sc[h]), lse)
            lse_ref[...] = lse
            for hp in range(N_HEADS // 2):
                inv = jnp.where(low_t, 1.0 / l_sc[2 * hp], 1.0 / l_sc[2 * hp + 1])
                o_ref[:, hp * 128:(hp + 1) * 128] = (acc_sc[hp] * inv).astype(BF)

    outs = _hosted_call(
        body, grid=(len(pairs),),
        in_specs=[pl.BlockSpec((T, 1024), lambda t, n, j: (n[t], 0)), pl.BlockSpec((TK, 1024), lambda t, n, j: (j[t], 1)),
                  pl.BlockSpec((TK, 1024), lambda t, n, j: (j[t], 2)), pl.BlockSpec((GATE_LANES, TK), lambda t, n, j: (0, j[t]))],
        out_specs=[pl.BlockSpec((T, 1024), lambda t, n, j: (n[t], 0)), pl.BlockSpec((T, 128), lambda t, n, j: (n[t], 0))],
        out_shape=[jax.ShapeDtypeStruct((S, 1024), BF), jax.ShapeDtypeStruct((S, 128), F32)],
        scratch_shapes=[pltpu.VMEM((N_HEADS, T, 128), F32), pltpu.VMEM((N_HEADS, T, 128), F32),
                        pltpu.VMEM((N_HEADS // 2, T, 128), F32)],
        args=(qkv, qkv, qkv, cT), name=name, guest=guest, schedule=schedule)
    return outs if guest is None else (outs[:2], outs[2:])


def _fox_bwd(qkv, cT, do, o, lse, *, name, guest=None):
    S = qkv.shape[0]
    T, TQ, R = FOX_T, FOX_TQ_BWD, FOX_ROWS
    nk, nq = S // T, S // TQ
    nt = (((1,), (1,)), ((), ()))
    tn = (((0,), (0,)), ((), ()))
    nn = (((1,), (0,)), ((), ()))
    chains = [(h, rh) for h in range(N_HEADS) for rh in range(TQ // R)]
    dot = functools.partial(lax.dot_general, preferred_element_type=F32)
    pairs = [(kb, qb) for kb in range(nk) for qb in range(kb * T // TQ, nq)]
    schedule = [jnp.asarray([p[i] for p in pairs], jnp.int32) for i in range(2)]

    def body(kb_tab, qb_tab, q_ref, k_ref, v_ref, ct_ref, do_ref, o_ref, lse_ref, dq_ref, dk_ref, dv_ref, dct_ref, dcq_ref,
             dq_sc, dk_sc, dv_sc, dc_sc, dcq_sc):
        kb = kb_tab[pl.program_id(0)]
        qb = qb_tab[pl.program_id(0)]
        jq = qb - kb * T // TQ
        lane = lax.broadcasted_iota(jnp.int32, (R, 128), 1)
        low = lane < HEAD_DIM
        ones_k = jnp.ones((T, 128), BF)
        ones_r = jnp.ones((8, R), BF)

        @pl.when(jnp.logical_and(kb == 0, jq == 0))
        def _():
            dq_sc[...] = jnp.zeros_like(dq_sc)
            dcq_sc[...] = jnp.zeros_like(dcq_sc)

        @pl.when(jq == 0)
        def _():
            dk_sc[...] = jnp.zeros_like(dk_sc)
            dv_sc[...] = jnp.zeros_like(dv_sc)
            dc_sc[...] = jnp.zeros_like(dc_sc)

        def step(diagonal):
            def pair(h):
                return slice((h // 2) * 128, (h // 2 + 1) * 128)

            def rows(rh):
                return slice(rh * R, (rh + 1) * R)

            def qrows(rh):
                return pl.ds(pl.multiple_of(qb * TQ + rh * R, R), R)

            def products(h, rh):
                mask = low if h % 2 == 0 else jnp.logical_not(low)
                qp = q_ref[rows(rh), pair(h)] * jnp.asarray(ATTN_SCALE, BF)
                dop = do_ref[rows(rh), pair(h)]
                qm = jnp.where(mask, qp, jnp.zeros_like(qp))
                dom = jnp.where(mask, dop, jnp.zeros_like(dop))
                s = dot(qm, k_ref[:, pair(h)], nt) - ct_ref[h:h + 1, :]
                if diagonal:
                    ahead = lax.broadcasted_iota(jnp.int32, (R, T), 1) - lax.broadcasted_iota(jnp.int32, (R, T), 0)
                    s = jnp.where(ahead <= qb * TQ + rh * R - kb * T, s, NEG_INF)
                return qm, dom, s, dot(dom, v_ref[:, pair(h)], nt)

            def pointwise(h, rh, qm, dom, s, dp):
                mask = low if h % 2 == 0 else jnp.logical_not(low)
                prod = do_ref[rows(rh), pair(h)].astype(F32) * o_ref[rows(rh), pair(h)].astype(F32)
                delta = jnp.sum(jnp.where(mask, prod, 0.0), axis=1, keepdims=True)
                p = jnp.exp(s - lse_ref[rows(rh), h:h + 1])
                ds = (p * (dp - delta)).astype(BF)
                return qm, dom, p.astype(BF), ds

            def gradients(h, qm, dom, p, ds):
                kx = jnp.concatenate([k_ref[:, pair(h)], ones_k], axis=1)
                return dot(ds, kx, nn), dot(qm, ds, tn), dot(dom, p, tn), dot(ones_r, ds, nn)

            st1, st2, even = {}, {}, {}
            dcq_tiles = [jnp.zeros((R, 128), F32) for _ in range(TQ // R)]
            nch = len(chains)
            for t in range(nch + 2):
                if t < nch:
                    st1[t] = products(*chains[t])
                done = gradients(chains[t - 2][0], *st2.pop(t - 2)) if t >= 2 else None
                if 1 <= t <= nch:
                    st2[t - 1] = pointwise(*chains[t - 1], *st1.pop(t - 1))
                if done is not None:
                    h, rh = chains[t - 2]
                    dq_rsum, dk, dv, csum = done
                    dq = dq_rsum[:, :128]
                    dcq_tiles[rh] = jnp.where(lane == h, dq_rsum[:, 128:], dcq_tiles[rh])
                    dc_sc[h:h + 1, :] -= csum[0:1, :]
                    if h % 2 == 0:
                        even[rh] = (dq, dk, dv)
                    else:
                        dq0, dk0, dv0 = even.pop(rh)
                        dq_sc[qrows(rh), pair(h)] += jnp.where(low, dq0, dq) * ATTN_SCALE
                        dk_sc[h // 2] += dk0 + dk
                        dv_sc[h // 2] += dv0 + dv
            for rh in range(TQ // R):
                dcq_sc[qrows(rh), :] += dcq_tiles[rh]

        @pl.when(jq > 0)
        def _():
            step(False)

        @pl.when(jq == 0)
        def _():
            step(True)

        @pl.when(qb == nq - 1)
        def _():
            for hp in range(N_HEADS // 2):
                dk_ref[:, hp * 128:(hp + 1) * 128] = dk_sc[hp].T.astype(BF)
                dv_ref[:, hp * 128:(hp + 1) * 128] = dv_sc[hp].T.astype(BF)
            dct_ref[...] = dc_sc[...]

        @pl.when(jnp.logical_and(kb == nk - 1, qb == nq - 1))
        def _():
            def put(i, carry):
                r = pl.ds(pl.multiple_of(i * T, T), T)
                dq_ref[r, :] = dq_sc[r, :].astype(BF)
                return carry
            lax.fori_loop(0, nk, put, 0)
            dcq_ref[...] = dcq_sc[...]

    qblk = lambda col: pl.BlockSpec((TQ, 1024), lambda t, kb, qb: (qb[t], col))
    kblk = lambda col: pl.BlockSpec((T, 1024), lambda t, kb, qb: (kb[t], col))
    whole = pl.BlockSpec((S, 1024), lambda t, kb, qb: (0, 0))
    outs = _hosted_call(
        body, grid=(len(pairs),),
        in_specs=[qblk(0), kblk(1), kblk(2), pl.BlockSpec((GATE_LANES, T), lambda t, kb, qb: (0, kb[t])), qblk(0), qblk(0),
                  pl.BlockSpec((TQ, 128), lambda t, kb, qb: (qb[t], 0))],
        out_specs=[whole, kblk(0), kblk(0), pl.BlockSpec((GATE_LANES, T), lambda t, kb, qb: (0, kb[t])),
                   pl.BlockSpec((S, GATE_LANES), lambda t, kb, qb: (0, 0))],
        out_shape=[jax.ShapeDtypeStruct((S, 1024), BF)] * 3 + [jax.ShapeDtypeStruct((GATE_LANES, S), F32),
                                                               jax.ShapeDtypeStruct((S, GATE_LANES), F32)],
        scratch_shapes=[pltpu.VMEM((S, 1024), F32), pltpu.VMEM((N_HEADS // 2, 128, T), F32), pltpu.VMEM((N_HEADS // 2, 128, T), F32),
                        pltpu.VMEM((GATE_LANES, T), F32), pltpu.VMEM((S, GATE_LANES), F32)],
        args=(qkv, qkv, qkv, cT, do, o, lse), name=name, guest=guest, schedule=schedule)
    return outs if guest is None else (outs[:5], outs[5:])


def _to_natural(src_ref, buf, d, width):
    rows = buf.shape[1]
    for r in range(d):
        for ch in range(width // 128):
            lanes = slice(r * width + ch * 128, r * width + (ch + 1) * 128)
            buf.at[ch][pl.ds(r, rows // d, stride=d), :] = src_ref[:, lanes].astype(F32)
    return jnp.concatenate([buf[ch] for ch in range(width // 128)], axis=1)


def _to_view(val, buf, dst_ref, d, width):
    rows = buf.shape[1]
    for ch in range(width // 128):
        buf[ch] = val[:, ch * 128:(ch + 1) * 128]
    for r in range(d):
        for ch in range(width // 128):
            lanes = slice(r * width + ch * 128, r * width + (ch + 1) * 128)
            dst_ref[:, lanes] = buf.at[ch][pl.ds(r, rows // d, stride=d), :].astype(dst_ref.dtype)


def _view_spec(rows, d, width):
    return pl.BlockSpec((rows // d, d * width), lambda i, *_: (i, 0))


def _combine_groups(os, lses, dils, *, name):
    ng = len(os)
    S = os[0].shape[0] * dils[0]
    tm = ROW_TILE
    views = sorted(set(dils))

    def body(*refs):
        o_refs, l_refs = refs[:ng], refs[ng:2 * ng]
        outs = refs[2 * ng:2 * ng + 2 * len(views)]
        wide, narrow = refs[-2], refs[-1]
        ls = [l_refs[g][...] if dils[g] == 1 else _to_natural(l_refs[g], narrow, dils[g], 128) for g in range(ng)]
        m = functools.reduce(jnp.maximum, ls)
        es = [jnp.exp(l - m) for l in ls]
        den = functools.reduce(jnp.add, es)
        ws = [e / den for e in es]
        lse = m + jnp.log(den)
        og = [o_refs[g][...] if dils[g] == 1 else _to_natural(o_refs[g], wide, dils[g], 1024) for g in range(ng)]
        cols = []
        for h in range(N_HEADS):
            hs = slice(h * HEAD_DIM, (h + 1) * HEAD_DIM)
            acc = ws[0][:, h:h + 1] * og[0][:, hs]
            for g in range(1, ng):
                acc = acc + ws[g][:, h:h + 1] * og[g][:, hs]
            cols.append(acc)
        o = jnp.concatenate(cols, axis=1)
        for k, d in enumerate(views):
            if d == 1:
                outs[2 * k][...] = o.astype(BF)
                outs[2 * k + 1][...] = lse
            else:
                _to_view(o, wide, outs[2 * k], d, 1024)
                _to_view(lse, narrow, outs[2 * k + 1], d, 128)

    out_specs, out_shape = [], []
    for d in views:
        out_specs += [_view_spec(tm, d, 1024), _view_spec(tm, d, 128)]
        out_shape += [jax.ShapeDtypeStruct((S // d, d * 1024), BF), jax.ShapeDtypeStruct((S // d, d * 128), F32)]
    res = pl.pallas_call(
        body, grid=(S // tm,), in_specs=[_view_spec(tm, d, 1024) for d in dils] + [_view_spec(tm, d, 128) for d in dils],
        out_specs=out_specs, out_shape=out_shape,
        scratch_shapes=[pltpu.VMEM((8, tm, 128), F32), pltpu.VMEM((1, tm, 128), F32)],
        name=name, compiler_params=_params("parallel"),
    )(*os, *lses)
    return {d: (res[2 * k], res[2 * k + 1]) for k, d in enumerate(views)}


def _assemble(parts, rope_flags, rope, dils, *, name):
    n = len(parts)
    S = parts[0].shape[0] * dils[0]
    use_rope = any(rope_flags)
    tm = ROW_TILE

    def body(*refs):
        out_ref, natural = refs[-2], refs[-1]
        for b in range(n):
            cols = slice(b * 1024, (b + 1) * 1024)
            d = dils[b]
            val = refs[b][...].astype(F32) if d == 1 else _to_natural(refs[b], natural, d, 1024)
            if rope_flags[b]:
                cos_ref, sa_ref, sb_ref = refs[n:n + 3]
                val = _rope_rotate(val, cos_ref[...], sa_ref[...], sb_ref[...])
            out_ref[:, cols] = val.astype(BF)

    in_specs = [_view_spec(tm, d, 1024) for d in dils]
    args = list(parts)
    if use_rope:
        in_specs += [pl.BlockSpec((tm, 128), lambda i: (i, 0))] * 3
        args += list(rope)
    return pl.pallas_call(
        body, grid=(S // tm,), in_specs=in_specs, out_specs=pl.BlockSpec((tm, n * 1024), lambda i: (i, 0)),
        out_shape=jax.ShapeDtypeStruct((S, n * 1024), BF), scratch_shapes=[pltpu.VMEM((8, tm, 128), F32)],
        name=name, compiler_params=_params("parallel"),
    )(*args)


GATE_ROWS = 512


def _gate_fwd(z, bf, *, name):
    S = z.shape[0]

    def body(z_ref, b_ref, ct_ref, carry):
        i = pl.program_id(0)

        @pl.when(i == 0)
        def _():
            carry[...] = jnp.zeros_like(carry)

        zz = z_ref[...] + b_ref[...]
        logf = jnp.minimum(zz, 0.0) - jnp.log(1.0 + jnp.exp(-jnp.abs(zz)))
        tri = (lax.broadcasted_iota(jnp.int32, (GATE_ROWS, GATE_ROWS), 0)
               >= lax.broadcasted_iota(jnp.int32, (GATE_ROWS, GATE_ROWS), 1)).astype(F32)
        cs = jnp.dot(tri, logf, precision=lax.Precision.HIGHEST, preferred_element_type=F32) + carry[...]
        ct_ref[...] = cs.T
        carry[...] = cs[GATE_ROWS - 1:GATE_ROWS, :]

    return pl.pallas_call(
        body, grid=(S // GATE_ROWS,),
        in_specs=[pl.BlockSpec((GATE_ROWS, GATE_LANES), lambda i: (i, 0)), pl.BlockSpec((1, GATE_LANES), lambda i: (0, 0))],
        out_specs=pl.BlockSpec((GATE_LANES, GATE_ROWS), lambda i: (0, i)),
        out_shape=jax.ShapeDtypeStruct((GATE_LANES, S), F32),
        scratch_shapes=[pltpu.VMEM((1, GATE_LANES), F32)], name=name, compiler_params=_params("arbitrary"),
    )(z, bf)


def _gate_bwd(z, bf, dcT, dcq, *, name):
    S = z.shape[0]
    nb = S // GATE_ROWS

    def body(z_ref, b_ref, dct_ref, dcq_ref, dz_ref, db_ref, carry):
        i = pl.program_id(0)

        @pl.when(i == 0)
        def _():
            carry[...] = jnp.zeros_like(carry)
            db_ref[...] = jnp.zeros_like(db_ref)

        dc = dct_ref[...].T + dcq_ref[...]
        tri = (lax.broadcasted_iota(jnp.int32, (GATE_ROWS, GATE_ROWS), 0)
               <= lax.broadcasted_iota(jnp.int32, (GATE_ROWS, GATE_ROWS), 1)).astype(F32)
        dl = jnp.dot(tri, dc, precision=lax.Precision.HIGHEST, preferred_element_type=F32) + carry[...]
        carry[...] = dl[0:1, :]
        zz = z_ref[...] + b_ref[...]
        dz = dl * (1.0 / (1.0 + jnp.exp(zz)))
        lane = lax.broadcasted_iota(jnp.int32, dz.shape, 1)
        dz = jnp.where(lane < N_HEADS, dz, 0.0)
        dz_ref[...] = dz.astype(BF)
        db_ref[...] += jnp.sum(dz, axis=0, keepdims=True)

    return pl.pallas_call(
        body, grid=(nb,),
        in_specs=[pl.BlockSpec((GATE_ROWS, GATE_LANES), lambda i: (nb - 1 - i, 0)), pl.BlockSpec((1, GATE_LANES), lambda i: (0, 0)),
                  pl.BlockSpec((GATE_LANES, GATE_ROWS), lambda i: (0, nb - 1 - i)),
                  pl.BlockSpec((GATE_ROWS, GATE_LANES), lambda i: (nb - 1 - i, 0))],
        out_specs=[pl.BlockSpec((GATE_ROWS, GATE_LANES), lambda i: (nb - 1 - i, 0)), pl.BlockSpec((1, GATE_LANES), lambda i: (0, 0))],
        out_shape=[jax.ShapeDtypeStruct((S, GATE_LANES), BF), jax.ShapeDtypeStruct((1, GATE_LANES), F32)],
        scratch_shapes=[pltpu.VMEM((1, GATE_LANES), F32)], name=name, compiler_params=_params("arbitrary"),
    )(z, bf, dcT, dcq)


def _rope_tables(S):
    half = ROT_DIM // 2
    inv_freq = ROPE_THETA ** (-jnp.arange(half, dtype=F32) * 2.0 / ROT_DIM)
    ang = jnp.arange(S, dtype=F32)[:, None] * inv_freq[None, :]
    cos, sin = jnp.cos(ang), jnp.sin(ang)
    zero = jnp.zeros((S, HEAD_DIM - ROT_DIM), F32)
    zh = jnp.zeros((S, half), F32)
    cos_h = jnp.concatenate([cos, cos, jnp.ones_like(zero)], axis=1)
    sa_h = jnp.concatenate([-sin, zh, zero], axis=1)
    sb_h = jnp.concatenate([zh, sin, zero], axis=1)
    two = lambda t: jnp.concatenate([t, t], axis=1)
    return two(cos_h), two(sa_h), two(sb_h)


def _ffn_fwd(h, norm, w_gu, w_down, tag):
    n = _rms_fwd(h, norm, name=f"ffn{tag}_norm")
    gu = _mm_nn(n, w_gu, tm=MM_ROWS, tn=1408, out_dtype=BF, name=f"ffn{tag}_gu")
    act = _swiglu_fwd(gu, name=f"ffn{tag}_act")
    out = _mm_nn(act, w_down, tm=MM_ROWS // 2, tn=1024, out_dtype=F32, name=f"ffn{tag}_down", resid=h)
    return out, (h, n, gu, act)


def _ffn_bwd(dh, dhb, saved, norm, w_gu, w_down, tag, ride=None):
    h, n, gu, act = saved
    dact = _mm_nt(dhb, w_down, tm=MM_ROWS, to=1408, tn=1024, out_dtype=BF, name=f"ffn{tag}_dact")
    dw_down = _mm_tn(act, dhb, tk=1408, tn=1024, tm=MM_ROWS, out_dtype=BF, name=f"ffn{tag}_dwdown")
    dgu = _swiglu_bwd(gu, dact, name=f"ffn{tag}_dgu")
    dn_call = lambda guest: _mm_nt(dgu, w_gu, tm=MM_ROWS, to=1024, tn=1408, out_dtype=F32, name=f"ffn{tag}_dn", guest=guest)
    dn = dn_call(None) if ride is None else ride(dn_call)
    dw_gu = _mm_tn(n, dgu, tk=1024, tn=1408, tm=MM_ROWS, out_dtype=BF, name=f"ffn{tag}_dwgu")
    dx, dxb, dg = _rms_bwd(h, norm, dn, dh, name=f"ffn{tag}_dnorm")
    return dx, dxb, dg, dw_gu, dw_down


def _local_step(x, tgt, w, mats, fetch, exchange):
    S = x.shape[0]
    rope_f = _rope_tables(S)
    rope_b = (rope_f[0], -rope_f[1], -rope_f[2])
    g, partial, landed = {}, {}, {}
    w = dict(w, ffn_w_gu={}, ffn_w_down={})

    def bring(call, indices):
        bufs = [mats[wi] for wi in indices]
        if fetch is None:
            return call(None), bufs
        return call(fetch(indices, bufs))

    def ride(call, indices):
        guest = exchange(indices, [partial[wi] for wi in indices]) if indices else None
        res = call(guest)
        if guest is None:
            return res
        res, outs = res
        landed.update(zip(indices, outs))
        return res

    n0 = _rms_fwd(x, w["a_norm"], name="a_norm")
    dils = [d for _, d in DILATED_PATTERNS]
    projs, (w["ffn_w_gu"][0], w["ffn_w_down"][0]) = bring(
        lambda guest: _mm_nn(n0, w["a_w_in"], tm=MM_ROWS, tn=1024, out_dtype=BF, name="a_proj", rope=rope_f, guest=guest,
                             groups=dils), [4, 6])
    block = lambda t, dil: (lambda r: t * dil + r)
    o_parts, lse_parts = [], []
    for gi, (window, dil) in enumerate(DILATED_PATTERNS):
        pv = projs[gi]
        attend = lambda guest: _band_fwd(pv, pv, pv, block(0, dil), block(1, dil), block(2, dil), dil=dil, T=128,
                                         window=window // dil, name=f"a_attn{gi}", guest=guest)
        if gi == 0:
            (o_g, lse_g), (w["a_w_out"],) = bring(attend, [1])
        elif gi == 1:
            (o_g, lse_g), (b_in,) = bring(attend, [2])
        else:
            (o_g, lse_g), (w["b_w_out"],) = bring(attend, [3])
        o_parts.append(o_g)
        lse_parts.append(lse_g)
    b_in = b_in.transpose(1, 0, 2).reshape(D_MODEL, -1)
    w["b_w_qkv"] = b_in[:, :QKV_COLS]
    w["b_w_f"] = jnp.pad(b_in[:, QKV_COLS:], ((0, 0), (0, GATE_LANES + QKV_COLS - b_in.shape[1])))
    mixed = _combine_groups(o_parts, lse_parts, dils, name="a_combine")
    o_a = mixed[1][0]
    h1 = _mm_nn(o_a, w["a_w_out"], tm=MM_ROWS, tn=1024, out_dtype=F32, name="a_out", resid=x)
    h2, ffn0 = _ffn_fwd(h1, w["ffn_norm"][0:1], w["ffn_w_gu"][0], w["ffn_w_down"][0], 0)

    n2 = _rms_fwd(h2, w["b_norm"], name="b_norm")
    qkv = _mm_nn(n2, w["b_w_qkv"], tm=MM_ROWS, tn=1024, out_dtype=BF, name="b_proj")
    zf = _mm_nn(n2, w["b_w_f"], tm=MM_ROWS, tn=GATE_LANES, out_dtype=F32, name="b_gate_proj")
    cT = _gate_fwd(zf, w["b_f"], name="b_gate")
    (o_b, lse_b), (w["ffn_w_gu"][1], w["ffn_w_down"][1]) = bring(lambda guest: _fox_fwd(qkv, cT, name="b_attn", guest=guest), [5, 7])
    h3 = _mm_nn(o_b, w["b_w_out"], tm=MM_ROWS, tn=1024, out_dtype=F32, name="b_out", resid=h2)
    h4, ffn1 = _ffn_fwd(h3, w["ffn_norm"][1:2], w["ffn_w_gu"][1], w["ffn_w_down"][1], 1)

    loss, dh4, dh4b, g["final_norm"] = _loss_head(h4, w["final_norm"], tgt, name="loss_head")

    dh3, dh3b, dg_f1, partial[5], partial[7] = _ffn_bwd(dh4, dh4b, ffn1, w["ffn_norm"][1:2], w["ffn_w_gu"][1], w["ffn_w_down"][1], 1)

    do_b = _mm_nt(dh3b, w["b_w_out"], tm=MM_ROWS, to=1024, tn=1024, out_dtype=BF, name="b_do")
    partial[3] = _mm_tn(o_b, dh3b, tk=1024, tn=1024, tm=MM_ROWS, out_dtype=BF, name="b_dwout")
    dq, dk, dv, dcT, dcq = ride(lambda guest: _fox_bwd(qkv, cT, do_b, o_b, lse_b, name="b_attn_bwd", guest=guest), [5, 7, 3])
    dz, g["b_f"] = _gate_bwd(zf, w["b_f"], dcT, dcq, name="b_gate_bwd")
    dqkv = _assemble([dq, dk, dv], [False] * 3, None, [1] * 3, name="b_dproj")
    dn2 = _mm_nt(dz, w["b_w_f"], tm=MM_ROWS, to=1024, tn=GATE_LANES, out_dtype=F32, name="b_dn_gate")
    dn2 = _mm_nt(dqkv, w["b_w_qkv"], tm=MM_ROWS, to=1024, tn=1024, out_dtype=F32, name="b_dn", add=dn2)
    g_qkv = _mm_tn(n2, dqkv, tk=1024, tn=1024, tm=MM_ROWS, out_dtype=BF, name="b_dwqkv")
    g_f = _mm_tn(n2, dz, tk=1024, tn=GATE_LANES, tm=MM_ROWS, out_dtype=BF, name="b_dwf")
    g_b_in = jnp.concatenate([g_qkv, g_f[:, :N_HEADS]], axis=1)
    partial[2] = g_b_in.reshape(D_MODEL, N_CHIPS, -1).transpose(1, 0, 2)
    dh2, dh2b, g["b_norm"] = _rms_bwd(h2, w["b_norm"], dn2, dh3, name="b_dnorm")

    dh1, dh1b, dg_f0, partial[4], partial[6] = _ffn_bwd(dh2, dh2b, ffn0, w["ffn_norm"][0:1], w["ffn_w_gu"][0], w["ffn_w_down"][0], 0,
                                                      ride=lambda call: ride(call, [2]))
    g["ffn_norm"] = jnp.concatenate([dg_f0, dg_f1], axis=0)

    views = tuple(sorted(set(dils)))
    do_a = dict(zip(views, _mm_nt(dh1b, w["a_w_out"], tm=MM_ROWS, to=1024, tn=1024, out_dtype=BF, name="a_do", views=views)))
    partial[1] = _mm_tn(o_a, dh1b, tk=1024, tn=1024, tm=MM_ROWS, out_dtype=BF, name="a_dwout")
    riders = {0: [4], 1: [6, 1], 2: []}
    parts = []
    for gi, (window, dil) in enumerate(DILATED_PATTERNS):
        pv = projs[gi]
        res = ride(lambda guest: _band_bwd(pv, pv, pv, block(0, dil), block(1, dil), block(2, dil), do_a[dil],
                                           mixed[dil][0], mixed[dil][1], dil=dil, T=128,
                                           window=window // dil, name=f"a_attn_bwd{gi}", guest=guest), riders[gi])
        parts += list(res)
    dproj = _assemble(parts, [True, True, False] * 3, rope_b, [d for _, d in DILATED_PATTERNS for _ in range(3)], name="a_dproj")
    partial[0] = _mm_tn(n0, dproj, tk=1024, tn=1024, tm=MM_ROWS, out_dtype=BF, name="a_dwin")
    dn0 = ride(lambda guest: _mm_nt(dproj, w["a_w_in"], tm=MM_ROWS, to=1024, tn=1024, out_dtype=F32, name="a_dn", guest=guest), [0])
    dx, _, g["a_norm"] = _rms_bwd(x, w["a_norm"], dn0, dh1, name="a_dnorm")
    return loss, dx, g, partial, landed


ANY = pl.BlockSpec(memory_space=pl.ANY)


def _place():
    x, y, c = lax.axis_index("x"), lax.axis_index("y"), lax.axis_index("c")
    chips = [(1 - x, y), (x, 1 - y), (1 - x, 1 - y)]
    return x, y, c, chips


def _shard_slice(ref, kind, rows, cols, s, half):
    hr = rows // 2
    if kind == "col":
        return ref.at[pl.ds(half * hr, hr), pl.ds(pl.multiple_of(s * cols, 128), cols)]
    if kind == "row":
        return ref.at[pl.ds(pl.multiple_of(s * rows + half * hr, 16), hr), :]
    return ref.at[s, pl.ds(half * hr, hr), :]


def _whole_shape(kind, rows, cols):
    return {"col": (rows, N_CHIPS * cols), "row": (N_CHIPS * rows, cols), "stack": (N_CHIPS, rows, cols)}[kind]


def _own_block(kind, rows, tr, cols):
    per = rows // tr

    def spec(half_rows):
        off = (lambda p: 0) if half_rows is None else (lambda p: p[1] * (half_rows // tr))
        if kind == "col":
            return pl.BlockSpec((tr, cols), lambda i, p: (off(p) + i, p[0]))
        if kind == "row":
            return pl.BlockSpec((tr, cols), lambda i, p: (p[0] * per + off(p) + i, 0))
        return pl.BlockSpec((None, tr, cols), lambda i, p: (p[0], off(p) + i, 0))
    return spec


def _place_shard(shards, layer, kind, place, *, name):
    _, rows, cols = shards.shape
    tr = 256 if rows % 256 == 0 else rows // 2

    def body(p_ref, s_ref, o_ref):
        o_ref[...] = s_ref[...].astype(BF)

    return pl.pallas_call(
        body,
        grid_spec=pltpu.PrefetchScalarGridSpec(
            num_scalar_prefetch=1, grid=(rows // tr,),
            in_specs=[pl.BlockSpec((None, tr, cols), lambda i, p: (layer, i, 0))],
            out_specs=_own_block(kind, rows, tr, cols)(None)),
        out_shape=jax.ShapeDtypeStruct(_whole_shape(kind, rows, cols), BF),
        name=name, compiler_params=_params("arbitrary"),
    )(place, shards)


def _gather_weights(placed, kinds, dims):
    nw = len(placed)

    def body(*refs):
        dst = refs[nw:2 * nw]
        send_sems, recv_sems = refs[2 * nw:]
        x, y, c, chips = _place()
        me = 2 * x + y
        sibling = (x, y, 1 - c)

        def copy(wi, k, s, half, to):
            p = _shard_slice(dst[wi], kinds[wi], dims[wi][0], dims[wi][1], s, half)
            return pltpu.make_async_remote_copy(src_ref=p, dst_ref=p, send_sem=send_sems.at[wi * 6 + k],
                                                recv_sem=recv_sems.at[wi * 6 + k], device_id=to, device_id_type=MESH)

        first, passed = [], []
        for wi in range(nw):
            for j, chip in enumerate(chips):
                cp = copy(wi, j, me, c, (*chip, c))
                cp.start()
                first.append(cp)
        for wi in range(nw):
            for j, chip in enumerate(chips):
                s = 2 * chip[0] + chip[1]
                copy(wi, j, s, c, (x, y, c)).wait_recv()
                cp = copy(wi, 3 + j, s, c, sibling)
                cp.start()
                passed.append(cp)
        for wi in range(nw):
            for j, chip in enumerate(chips):
                s = 2 * chip[0] + chip[1]
                copy(wi, 3 + j, s, 1 - c, (x, y, c)).wait_recv()
        for cp in first + passed:
            cp.wait_send()

    return pl.pallas_call(
        body, in_specs=[ANY] * nw, out_specs=[ANY] * nw,
        out_shape=[jax.ShapeDtypeStruct(p.shape, p.dtype) for p in placed],
        input_output_aliases={wi: wi for wi in range(nw)},
        scratch_shapes=[pltpu.SemaphoreType.DMA((nw * 6,)), pltpu.SemaphoreType.DMA((nw * 6,))],
        name="gather_weights",
    )(*placed)


def _fetch_guest(placed, kinds, dims):
    nw = len(placed)

    def copies(dst, send_sems, recv_sems, incoming):
        x, y, c, chips = _place()
        out = []
        for wi in range(nw):
            for j, chip in enumerate(chips):
                s = 2 * chip[0] + chip[1] if incoming else 2 * x + y
                to = (x, y, c) if incoming else (*chip, c)
                for half in range(2):
                    p = _shard_slice(dst[wi], kinds[wi], dims[wi][0], dims[wi][1], s, half)
                    k = wi * 6 + 2 * j + half
                    out.append(pltpu.make_async_remote_copy(src_ref=p, dst_ref=p, send_sem=send_sems.at[k],
                                                            recv_sem=recv_sems.at[k], device_id=to, device_id_type=MESH))
        return out

    def start(src, dst, sems):
        for cp in copies(dst, sems[0], sems[1], False):
            cp.start()

    def finish(src, dst, sems):
        for cp in copies(dst, sems[0], sems[1], True):
            cp.wait_recv()
        for cp in copies(dst, sems[0], sems[1], False):
            cp.wait_send()

    return dict(args=list(placed), out_shape=[jax.ShapeDtypeStruct(p.shape, p.dtype) for p in placed],
                scratch=[pltpu.SemaphoreType.DMA((nw * 6,)), pltpu.SemaphoreType.DMA((nw * 6,))],
                start=start, finish=finish, in_place=True)


def _scatter_guest(partials, kinds, dims):
    nw = len(partials)

    def copies(src, send_sems, recv_sems, dst):
        x, y, c, chips = _place()
        me = 2 * x + y
        out = []
        for wi in range(nw):
            rows, cols = dims[wi]

            def part(s, half, wi=wi, rows=rows, cols=cols):
                return _shard_slice(src[wi], kinds[wi], rows, cols, s, half)

            for j, chip in enumerate(chips):
                s = 2 * chip[0] + chip[1]
                for half in range(2):
                    slot = 2 * j + (c if half == 0 else 1 - c)
                    out.append(pltpu.make_async_remote_copy(
                        src_ref=part(s, half), dst_ref=dst[wi].at[slot],
                        send_sem=send_sems.at[wi * 7 + 2 * j + half], recv_sem=recv_sems.at[wi * 7 + slot],
                        device_id=(*chip, half), device_id_type=MESH))
            out.append(pltpu.make_async_remote_copy(
                src_ref=part(me, 1 - c), dst_ref=dst[wi].at[6],
                send_sem=send_sems.at[wi * 7 + 6], recv_sem=recv_sems.at[wi * 7 + 6],
                device_id=(x, y, 1 - c), device_id_type=MESH))
        return out

    def start(src, dst, sems):
        for cp in copies(src, sems[0], sems[1], dst):
            cp.start()

    def finish(src, dst, sems):
        x, y, c, _ = _place()
        for wi in range(nw):
            for slot in range(7):
                pltpu.make_async_remote_copy(
                    src_ref=dst[wi].at[slot], dst_ref=dst[wi].at[slot],
                    send_sem=sems[0].at[wi * 7 + slot], recv_sem=sems[1].at[wi * 7 + slot],
                    device_id=(x, y, c), device_id_type=MESH).wait_recv()
        for cp in copies(src, sems[0], sems[1], dst):
            cp.wait_send()

    return dict(args=list(partials), out_shape=[jax.ShapeDtypeStruct((7, d[0] // 2, d[1]), BF) for d in dims],
                scratch=[pltpu.SemaphoreType.DMA((nw * 7,)), pltpu.SemaphoreType.DMA((nw * 7,))],
                start=start, finish=finish)


def _sum_slots(slots, partial, kind, dims, place, *, name, into=None, layer=None, n_layers=1):
    rows, cols = dims
    hr = rows // 2
    tr = hr if 8 * hr * cols * 2 <= 6 * 1024 * 1024 else 128
    assert hr % tr == 0

    def body(p_ref, b_ref, own_ref, *rest):
        o_ref = rest[-1]
        acc = own_ref[...].astype(F32)
        for k in range(7):
            acc = acc + b_ref[k].astype(F32)
        o_ref[...] = acc

    half = lambda p: p[1] * (hr // tr)
    if n_layers == 1:
        out_spec = pl.BlockSpec((tr, cols), lambda i, p: (half(p) + i, 0))
        out_shape = jax.ShapeDtypeStruct((rows, cols), F32)
    else:
        out_spec = pl.BlockSpec((None, tr, cols), lambda i, p: (layer, half(p) + i, 0))
        out_shape = jax.ShapeDtypeStruct((n_layers, rows, cols), F32)
    in_specs = [pl.BlockSpec((7, tr, cols), lambda i, p: (0, i, 0)), _own_block(kind, rows, tr, cols)(hr)]
    args = [place, slots, partial]
    aliases = {}
    if into is not None:
        in_specs.append(ANY)
        args.append(into)
        aliases = {3: 0}
    return pl.pallas_call(
        body,
        grid_spec=pltpu.PrefetchScalarGridSpec(num_scalar_prefetch=1, grid=(hr // tr,), in_specs=in_specs, out_specs=out_spec),
        out_shape=out_shape, input_output_aliases=aliases, name=name, compiler_params=_params("arbitrary"),
    )(*args)


def _pair_exchange(bufs, members):
    nw = len(members)

    def body(*refs):
        dst = refs[len(bufs):2 * len(bufs)]
        send_sems, recv_sems = refs[2 * len(bufs):]
        x, y, c, _ = _place()

        def rows_of(wi, half):
            bi, l = members[wi]
            ref = dst[bi] if l is None else dst[bi].at[l]
            hr = ref.shape[0] // 2
            return ref.at[pl.ds(pl.multiple_of(half * hr, 8), hr), :]

        def copy(wi, half, to):
            p = rows_of(wi, half)
            return pltpu.make_async_remote_copy(src_ref=p, dst_ref=p, send_sem=send_sems.at[wi], recv_sem=recv_sems.at[wi],
                                                device_id=to, device_id_type=MESH)

        sent = []
        for wi in range(nw):
            cp = copy(wi, c, (x, y, 1 - c))
            cp.start()
            sent.append(cp)
        for wi in range(nw):
            copy(wi, 1 - c, (x, y, c)).wait_recv()
        for cp in sent:
            cp.wait_send()

    return pl.pallas_call(
        body, in_specs=[ANY] * len(bufs), out_specs=[ANY] * len(bufs),
        out_shape=[jax.ShapeDtypeStruct(b.shape, b.dtype) for b in bufs],
        input_output_aliases={i: i for i in range(len(bufs))},
        scratch_shapes=[pltpu.SemaphoreType.DMA((nw,)), pltpu.SemaphoreType.DMA((nw,))],
        name="pair_exchange",
    )(*bufs)


SMALL_ROWS = 8


def _allreduce_small(v, *, name):
    assert v.shape == (SMALL_ROWS, D_MODEL)

    def body(v_ref, o_ref, buf, send_sems, recv_sems):
        x, y, c, _ = _place()
        me = 4 * x + 2 * y + c
        buf[me] = v_ref[...]
        sent = []
        for k in range(1, 8):
            bx, by, bc = (k >> 2) & 1, (k >> 1) & 1, k & 1
            peer = (1 - x if bx else x, 1 - y if by else y, 1 - c if bc else c)
            cp = pltpu.make_async_remote_copy(src_ref=v_ref, dst_ref=buf.at[me], send_sem=send_sems.at[k - 1],
                                              recv_sem=recv_sems.at[k - 1], device_id=peer, device_id_type=MESH)
            cp.start()
            sent.append(cp)
        for k in range(1, 8):
            bx, by, bc = (k >> 2) & 1, (k >> 1) & 1, k & 1
            peer = 4 * (1 - x if bx else x) + 2 * (1 - y if by else y) + (1 - c if bc else c)
            pltpu.make_async_remote_copy(src_ref=v_ref, dst_ref=buf.at[peer], send_sem=send_sems.at[k - 1],
                                         recv_sem=recv_sems.at[k - 1], device_id=(x, y, c), device_id_type=MESH).wait_recv()
        for cp in sent:
            cp.wait_send()
        acc = buf[0]
        for d in range(1, 8):
            acc = acc + buf[d]
        o_ref[...] = acc

    vmem = pl.BlockSpec(memory_space=pltpu.VMEM)
    return pl.pallas_call(
        body, in_specs=[vmem], out_specs=vmem, out_shape=jax.ShapeDtypeStruct(v.shape, F32),
        scratch_shapes=[pltpu.VMEM((8,) + v.shape, F32), pltpu.SemaphoreType.DMA((7,)), pltpu.SemaphoreType.DMA((7,))],
        name=name,
    )(v)


def _adamw(w, g, m, v, *, name):
    R, C = w.shape
    tr = R
    if R * C * 4 > 1024 * 1024:
        tr = max(t for t in range(8, R, 8) if R % t == 0 and t * C * 4 <= 1024 * 1024)

    def body(w_ref, g_ref, m_ref, v_ref, d_ref, m2_ref, v2_ref):
        gg = g_ref[...]
        m2 = ADAM_B1 * m_ref[...] + (1.0 - ADAM_B1) * gg
        v2 = ADAM_B2 * v_ref[...] + (1.0 - ADAM_B2) * jnp.square(gg)
        m_hat = m2 / (1.0 - ADAM_B1 ** ADAM_STEP)
        v_hat = v2 / (1.0 - ADAM_B2 ** ADAM_STEP)
        d_ref[...] = -ADAM_LR * (m_hat / (jnp.sqrt(v_hat) + ADAM_EPS) + ADAM_WD * w_ref[...])
        m2_ref[...] = m2
        v2_ref[...] = v2

    blk = pl.BlockSpec((tr, C), lambda i: (i, 0))
    out = jax.ShapeDtypeStruct((R, C), F32)
    return pl.pallas_call(
        body, grid=(R // tr,), in_specs=[blk] * 4, out_specs=[blk] * 3, out_shape=[out] * 3,
        name=name, compiler_params=_params("parallel"),
    )(w, g, m, v)


WEIGHT_ORDER = ("a_norm", "a_w_in", "a_w_out", "b_norm", "b_w_in", "b_f", "b_w_out", "ffn_norm", "ffn_w_gu",
                "ffn_w_down", "final_norm")
MATRICES = (("a_w_in", 0, "col"), ("a_w_out", 0, "row"), ("b_w_in", 0, "stack"), ("b_w_out", 0, "row"),
            ("ffn_w_gu", 0, "col"), ("ffn_w_gu", 1, "col"), ("ffn_w_down", 0, "row"), ("ffn_w_down", 1, "row"))
MATRIX_GROUPS = ([0], [1], [2], [3], [4, 5], [6, 7])
GROUP_NAMES = ("a_w_in", "a_w_out", "b_w_in", "b_w_out", "ffn_w_gu", "ffn_w_down")
QKV_COLS = 3 * N_HEADS * HEAD_DIM


def kernel(x, a_norm, a_w_in, a_w_out, b_norm, b_w_in, b_f, b_w_out, ffn_norm, ffn_w_gu, ffn_w_down, final_norm, loss_target, m_a_norm, m_a_w_in, m_a_w_out, m_b_norm, m_b_w_in, m_b_f, m_b_w_out, m_ffn_norm, m_ffn_w_gu, m_ffn_w_down, m_final_norm, v_a_norm, v_a_w_in, v_a_w_out, v_b_norm, v_b_w_in, v_b_f, v_b_w_out, v_ffn_norm, v_ffn_w_gu, v_ffn_w_down, v_final_norm):
    given = dict(a_norm=a_norm, a_w_in=a_w_in, a_w_out=a_w_out, b_norm=b_norm, b_w_in=b_w_in, b_f=b_f, b_w_out=b_w_out,
                 ffn_norm=ffn_norm, ffn_w_gu=ffn_w_gu, ffn_w_down=ffn_w_down, final_norm=final_norm)
    mom_m = dict(a_norm=m_a_norm, a_w_in=m_a_w_in, a_w_out=m_a_w_out, b_norm=m_b_norm, b_w_in=m_b_w_in, b_f=m_b_f,
                 b_w_out=m_b_w_out, ffn_norm=m_ffn_norm, ffn_w_gu=m_ffn_w_gu, ffn_w_down=m_ffn_w_down, final_norm=m_final_norm)
    mom_v = dict(a_norm=v_a_norm, a_w_in=v_a_w_in, a_w_out=v_a_w_out, b_norm=v_b_norm, b_w_in=v_b_w_in, b_f=v_b_f,
                 b_w_out=v_b_w_out, ffn_norm=v_ffn_norm, ffn_w_gu=v_ffn_w_gu, ffn_w_down=v_ffn_w_down, final_norm=v_final_norm)
    chip = 2 * lax.axis_index("x") + lax.axis_index("y")
    core = lax.axis_index("c")
    bn_cols = b_norm.shape[1]

    placed = lax.dynamic_update_slice(jnp.zeros((SMALL_ROWS, D_MODEL), F32), b_norm, (0, chip * bn_cols))
    placed = placed * (core == 0).astype(F32)
    b_norm_full = _allreduce_small(placed, name="gather_b_norm")[0:1]

    place = jnp.stack([chip, core]).astype(jnp.int32)
    kinds = [k for _, _, k in MATRICES]
    dims = [given[n].shape[1:] for n, _, _ in MATRICES]
    placed = [_place_shard(given[n], l, k, place, name=f"place_{n}{l}") for n, l, k in MATRICES]
    first = _gather_weights(placed[:1], kinds[:1], dims[:1])
    mats = dict(enumerate(list(first) + placed[1:]))
    gate_cols = b_f.shape[1]
    w = dict(a_norm=a_norm, a_w_in=mats[0], b_norm=b_norm_full,
             b_f=jnp.pad(b_f, ((0, 0), (0, GATE_LANES - gate_cols))), ffn_norm=ffn_norm,
             final_norm=final_norm.reshape(1, D_MODEL))

    def fetch(indices, bufs):
        return _fetch_guest(bufs, [kinds[i] for i in indices], [dims[i] for i in indices])

    def exchange(indices, parts):
        return _scatter_guest(parts, [kinds[i] for i in indices], [dims[i] for i in indices])

    loss, dx, g, partials, slots = _local_step(x[0], loss_target[0], w, mats, fetch, exchange)
    bufs, members = [], []
    for group in MATRIX_GROUPS:
        buf = None
        for l, wi in enumerate(group):
            n = MATRICES[wi][0]
            buf = _sum_slots(slots[wi], partials[wi], kinds[wi], dims[wi], place, name=f"sum_{n}{l}", into=buf,
                             layer=l, n_layers=len(group))
            members.append((len(bufs), l if len(group) > 1 else None))
        bufs.append(buf)
    reduced = dict(zip(GROUP_NAMES, _pair_exchange(bufs, members)))

    small = jnp.concatenate([g["a_norm"], g["b_norm"], g["ffn_norm"], g["final_norm"],
                             jnp.pad(g["b_f"], ((0, 0), (0, D_MODEL - GATE_LANES))),
                             jnp.zeros((SMALL_ROWS - 6, D_MODEL), F32)], axis=0)
    small = _allreduce_small(small, name="allreduce_small")
    grads = dict(reduced)
    grads["a_norm"] = small[0:1]
    grads["b_norm"] = lax.dynamic_slice(small, (1, chip * bn_cols), (1, bn_cols))
    grads["ffn_norm"] = small[2:4]
    grads["final_norm"] = small[4]
    grads["b_f"] = small[5:6, :gate_cols]

    out_g, out_d, out_m, out_v = [], [], [], []
    for n in WEIGHT_ORDER:
        shape = given[n].shape
        two_d = (1, shape[0]) if len(shape) == 1 else (-1, shape[-1])
        d, m2, v2 = _adamw(given[n].reshape(two_d), grads[n].reshape(two_d), mom_m[n].reshape(two_d),
                           mom_v[n].reshape(two_d), name=f"adamw_{n}")
        out_g.append(grads[n].reshape(shape))
        out_d.append(d.reshape(shape))
        out_m.append(m2.reshape(shape))
        out_v.append(v2.reshape(shape))

    total = lax.psum(loss[0, 0], MESH_AXES)
    return (total, dx[None], *out_g, *out_d, *out_m, *out_v)
```

```python
import functools

import jax
import jax.numpy as jnp
from jax import lax
from jax.experimental import pallas as pl
from jax.experimental.pallas import tpu as pltpu

F32 = jnp.float32
BF = jnp.bfloat16

D_MODEL = 1024
N_HEADS = 16
HEAD_DIM = 64
D_FF = 2816
DILATED_PATTERNS = ((128, 1), (512, 4), (2048, 16))
ROT_DIM = 16
ROPE_THETA = 500000.0
RMS_EPS = 1e-6
NEG_INF = -1e30
ATTN_SCALE = HEAD_DIM ** -0.5
GATE_LANES = 128
N_CHIPS = 4
MESH_AXES = ("x", "y", "c")
MESH = pl.DeviceIdType.MESH

ADAM_LR = 0.001
ADAM_B1 = 0.9
ADAM_B2 = 0.999
ADAM_EPS = 1e-08
ADAM_WD = 0.01
ADAM_STEP = 10

VMEM_LIMIT_BYTES = 56 * 1024 * 1024


def _params(*sem):
    return pltpu.CompilerParams(dimension_semantics=sem, vmem_limit_bytes=VMEM_LIMIT_BYTES)


def _hosted_call(body, *, grid, in_specs, out_specs, out_shape, scratch_shapes, args, name, guest=None, schedule=()):
    params = _params(*(["arbitrary"] * len(grid)))
    ns = len(schedule)

    def call(kernel, in_specs, out_specs, out_shape, scratch_shapes, aliases, args):
        spec = pltpu.PrefetchScalarGridSpec(num_scalar_prefetch=ns, grid=grid, in_specs=in_specs, out_specs=out_specs,
                                            scratch_shapes=scratch_shapes)
        return pl.pallas_call(kernel, grid_spec=spec, out_shape=out_shape, input_output_aliases=aliases, name=name,
                              compiler_params=params)(*schedule, *args)

    if guest is None:
        return call(body, in_specs, out_specs, out_shape, scratch_shapes, {}, args)
    n_in, n_out, n_scr = ns + len(in_specs), len(out_specs), len(scratch_shapes)
    g_in, g_out = len(guest["args"]), len(guest["out_shape"])
    any_spec = pl.BlockSpec(memory_space=pl.ANY)

    def wrapped(*refs):
        i1 = n_in + g_in
        o1 = i1 + n_out
        o2 = o1 + g_out
        s1 = o2 + n_scr
        guest_refs = (refs[n_in:i1], refs[o1:o2], refs[s1:])
        ids = [pl.program_id(d) for d in range(len(grid))]
        first = functools.reduce(jnp.logical_and, [i == 0 for i in ids])
        last = functools.reduce(jnp.logical_and, [i == g - 1 for i, g in zip(ids, grid)])

        @pl.when(first)
        def _():
            guest["start"](*guest_refs)

        body(*refs[:n_in], *refs[i1:o1], *refs[o2:s1])

        @pl.when(last)
        def _():
            guest["finish"](*guest_refs)

    aliases = {n_in + k: n_out + k for k in range(g_in)} if guest.get("in_place") else {}
    return call(wrapped, list(in_specs) + [any_spec] * g_in, list(out_specs) + [any_spec] * g_out,
                list(out_shape) + list(guest["out_shape"]), list(scratch_shapes) + list(guest["scratch"]), aliases,
                list(args) + list(guest["args"]))


def _rope_rotate(t, cos, sin_a, sin_b):
    outs = []
    for cidx in range(t.shape[1] // 128):
        tc = t[:, cidx * 128:(cidx + 1) * 128]
        outs.append(tc * cos + pltpu.roll(tc, 120, 1) * sin_a + pltpu.roll(tc, 8, 1) * sin_b)
    return jnp.concatenate(outs, axis=1)


def _mm_nn(a, b, *, tm, tn, out_dtype, name, resid=None, rope=None, guest=None, groups=None):
    M, K = a.shape
    N = b.shape[1]
    assert M % tm == 0 and N % tn == 0 and b.shape[0] == K
    n_in = 2 + (resid is not None) + (3 if rope is not None else 0)
    if groups is not None:
        assert rope is not None and N == 3 * tn * len(groups)

    def body(*refs):
        a_ref, b_ref = refs[0], refs[1]
        o_ref = refs[n_in]
        acc = jnp.dot(a_ref[...], b_ref[...], preferred_element_type=F32)
        if resid is not None:
            acc = acc + refs[2][...]
        if groups is not None:
            cos_ref, sa_ref, sb_ref = refs[n_in - 3:n_in]
            j = pl.program_id(1)
            for g, d in enumerate(groups):
                for is_v in (False, True):
                    @pl.when(jnp.logical_and(j // 3 == g, (j % 3 == 2) == is_v))
                    def _(g=g, d=d, is_v=is_v):
                        val = acc if is_v else _rope_rotate(acc, cos_ref[...], sa_ref[...], sb_ref[...])
                        if d == 1:
                            refs[n_in + g][...] = val.astype(out_dtype)
                        else:
                            _to_view(val, refs[-1], refs[n_in + g], d, tn)
        elif rope is not None:
            cos_ref, sa_ref, sb_ref = refs[n_in - 3:n_in]
            j = pl.program_id(1)

            @pl.when(j % 3 != 2)
            def _():
                o_ref[...] = _rope_rotate(acc, cos_ref[...], sa_ref[...], sb_ref[...]).astype(out_dtype)

            @pl.when(j % 3 == 2)
            def _():
                o_ref[...] = acc.astype(out_dtype)
        else:
            o_ref[...] = acc.astype(out_dtype)

    in_specs = [pl.BlockSpec((tm, K), lambda i, j: (i, 0)), pl.BlockSpec((K, tn), lambda i, j: (0, j))]
    args = [a, b]
    if resid is not None:
        in_specs.append(pl.BlockSpec((tm, tn), lambda i, j: (i, j)))
        args.append(resid)
    if rope is not None:
        assert tn == 1024
        for t in rope:
            in_specs.append(pl.BlockSpec((tm, 128), lambda i, j: (i, 0)))
            args.append(t)
    if groups is None:
        out_specs = [pl.BlockSpec((tm, tn), lambda i, j: (i, j))]
        out_shape = [jax.ShapeDtypeStruct((M, N), out_dtype)]
        scratch = []
    else:
        out_specs = [pl.BlockSpec((tm // d, d * tn), lambda i, j, g=g: (i, jnp.clip(j - 3 * g, 0, 2)))
                     for g, d in enumerate(groups)]
        out_shape = [jax.ShapeDtypeStruct((M // d, d * 3 * tn), out_dtype) for d in groups]
        scratch = [pltpu.VMEM((tn // 128, tm, 128), F32)]
    outs = _hosted_call(body, grid=(M // tm, N // tn), in_specs=in_specs, out_specs=out_specs, out_shape=out_shape,
                        scratch_shapes=scratch, args=args, name=name, guest=guest)
    nout = len(out_shape)
    res = outs[0] if groups is None else list(outs[:nout])
    return res if guest is None else (res, outs[nout:])


def _mm_nt(a, b, *, tm, to, tn, out_dtype, name, add=None, guest=None, views=(1,)):
    M, N = a.shape
    O = b.shape[0]
    assert M % tm == 0 and O % to == 0 and N % tn == 0 and b.shape[1] == N
    nk = N // tn

    def body(*refs):
        a_ref, b_ref = refs[0], refs[1]
        n_in = 2 + (add is not None)
        o_refs = refs[n_in:n_in + len(views)]
        acc_ref = refs[n_in + len(views)]
        k = pl.program_id(2)

        @pl.when(k == 0)
        def _():
            if add is not None:
                acc_ref[...] = refs[2][...]
            else:
                acc_ref[...] = jnp.zeros_like(acc_ref)

        acc_ref[...] += lax.dot_general(a_ref[...], b_ref[...], (((1,), (1,)), ((), ())),
                                        preferred_element_type=F32)

        @pl.when(k == nk - 1)
        def _():
            for o_ref, d in zip(o_refs, views):
                if d == 1:
                    o_ref[...] = acc_ref[...].astype(out_dtype)
                else:
                    _to_view(acc_ref[...], refs[-1], o_ref, d, to)

    in_specs = [pl.BlockSpec((tm, tn), lambda i, j, k: (i, k)), pl.BlockSpec((to, tn), lambda i, j, k: (j, k))]
    args = [a, b]
    if add is not None:
        in_specs.append(pl.BlockSpec((tm, to), lambda i, j, k: (i, j)))
        args.append(add)
    assert views == (1,) or (to == O and to % 128 == 0)
    scratch = [pltpu.VMEM((tm, to), F32)] + ([pltpu.VMEM((to // 128, tm, 128), F32)] if views != (1,) else [])
    outs = _hosted_call(
        body, grid=(M // tm, O // to, nk), in_specs=in_specs,
        out_specs=[pl.BlockSpec((tm, to), lambda i, j, k: (i, j)) if d == 1 else _view_spec(tm, d, to) for d in views],
        out_shape=[jax.ShapeDtypeStruct((M // d, d * O), out_dtype) for d in views],
        scratch_shapes=scratch, args=args, name=name, guest=guest)
    nv = len(views)
    res = outs[0] if nv == 1 else list(outs[:nv])
    return res if guest is None else (res, outs[nv:])


def _mm_tn(a, b, *, tk, tn, tm, out_dtype, name):
    M, K = a.shape
    N = b.shape[1]
    assert M % tm == 0 and K % tk == 0 and N % tn == 0 and b.shape[0] == M
    nm = M // tm

    def body(a_ref, b_ref, o_ref, acc_ref):
        m = pl.program_id(2)

        @pl.when(m == 0)
        def _():
            acc_ref[...] = jnp.zeros_like(acc_ref)

        acc_ref[...] += lax.dot_general(a_ref[...], b_ref[...], (((0,), (0,)), ((), ())),
                                        preferred_element_type=F32)

        @pl.when(m == nm - 1)
        def _():
            o_ref[...] = acc_ref[...].astype(out_dtype)

    return pl.pallas_call(
        body, grid=(K // tk, N // tn, nm),
        in_specs=[pl.BlockSpec((tm, tk), lambda i, j, m: (m, i)), pl.BlockSpec((tm, tn), lambda i, j, m: (m, j))],
        out_specs=pl.BlockSpec((tk, tn), lambda i, j, m: (i, j)),
        out_shape=jax.ShapeDtypeStruct((K, N), out_dtype),
        scratch_shapes=[pltpu.VMEM((tk, tn), F32)], name=name,
        compiler_params=_params("parallel", "parallel", "arbitrary"),
    )(a, b)


ROW_TILE = 512
MM_ROWS = 1024


def _rms_fwd(x, g, *, name):
    S, Dm = x.shape

    def body(x_ref, g_ref, o_ref):
        xf = x_ref[...]
        r = lax.rsqrt(jnp.mean(xf * xf, axis=-1, keepdims=True) + RMS_EPS)
        o_ref[...] = (xf * r * g_ref[...]).astype(BF)

    return pl.pallas_call(
        body, grid=(S // ROW_TILE,),
        in_specs=[pl.BlockSpec((ROW_TILE, Dm), lambda i: (i, 0)), pl.BlockSpec((1, Dm), lambda i: (0, 0))],
        out_specs=pl.BlockSpec((ROW_TILE, Dm), lambda i: (i, 0)),
        out_shape=jax.ShapeDtypeStruct((S, Dm), BF), name=name, compiler_params=_params("parallel"),
    )(x, g)


def _rms_bwd(x, g, dn, dres, *, name):
    S, Dm = x.shape

    def body(x_ref, g_ref, dn_ref, dres_ref, dx_ref, dxb_ref, dg_ref):
        i = pl.program_id(0)
        xf = x_ref[...]
        r = lax.rsqrt(jnp.mean(xf * xf, axis=-1, keepdims=True) + RMS_EPS)
        xh = xf * r
        dnf = dn_ref[...]
        dyg = dnf * g_ref[...]
        dx = dres_ref[...] + r * (dyg - xh * jnp.mean(dyg * xh, axis=-1, keepdims=True))
        dx_ref[...] = dx
        dxb_ref[...] = dx.astype(BF)

        @pl.when(i == 0)
        def _():
            dg_ref[...] = jnp.zeros_like(dg_ref)

        dg_ref[...] += jnp.sum(dnf * xh, axis=0, keepdims=True)

    row = pl.BlockSpec((ROW_TILE, Dm), lambda i: (i, 0))
    vec = pl.BlockSpec((1, Dm), lambda i: (0, 0))
    return pl.pallas_call(
        body, grid=(S // ROW_TILE,), in_specs=[row, vec, row, row], out_specs=[row, row, vec],
        out_shape=[jax.ShapeDtypeStruct((S, Dm), F32), jax.ShapeDtypeStruct((S, Dm), BF),
                   jax.ShapeDtypeStruct((1, Dm), F32)],
        name=name, compiler_params=_params("arbitrary"),
    )(x, g, dn, dres)


def _loss_head(h, g, tgt, *, name):
    S, Dm = h.shape

    def body(h_ref, g_ref, t_ref, loss_ref, dh_ref, dhb_ref, dg_ref):
        i = pl.program_id(0)
        xf = h_ref[...]
        r = lax.rsqrt(jnp.mean(xf * xf, axis=-1, keepdims=True) + RMS_EPS)
        xh = xf * r
        gv = g_ref[...]
        err = xh * gv - t_ref[...]
        dy = err * (1.0 / Dm)
        dyg = dy * gv
        dh = r * (dyg - xh * jnp.mean(dyg * xh, axis=-1, keepdims=True))
        dh_ref[...] = dh
        dhb_ref[...] = dh.astype(BF)

        @pl.when(i == 0)
        def _():
            dg_ref[...] = jnp.zeros_like(dg_ref)
            loss_ref[...] = jnp.zeros_like(loss_ref)

        dg_ref[...] += jnp.sum(dy * xh, axis=0, keepdims=True)
        part = 0.5 * jnp.sum(jnp.mean(err * err, axis=-1, keepdims=True), axis=0, keepdims=True)
        loss_ref[...] += jnp.broadcast_to(part, loss_ref.shape)

    row = pl.BlockSpec((ROW_TILE, Dm), lambda i: (i, 0))
    vec = pl.BlockSpec((1, Dm), lambda i: (0, 0))
    return pl.pallas_call(
        body, grid=(S // ROW_TILE,), in_specs=[row, vec, row],
        out_specs=[pl.BlockSpec((1, 128), lambda i: (0, 0)), row, row, vec],
        out_shape=[jax.ShapeDtypeStruct((1, 128), F32), jax.ShapeDtypeStruct((S, Dm), F32),
                   jax.ShapeDtypeStruct((S, Dm), BF), jax.ShapeDtypeStruct((1, Dm), F32)],
        name=name, compiler_params=_params("arbitrary"),
    )(h, g, tgt)


SWIGLU_ROWS = 512


def _swiglu_fwd(gu, *, name):
    S = gu.shape[0]

    def body(g_ref, u_ref, o_ref):
        g = g_ref[...].astype(F32)
        sig = 1.0 / (1.0 + jnp.exp(-g))
        o_ref[...] = (g * sig * u_ref[...].astype(F32)).astype(BF)

    return pl.pallas_call(
        body, grid=(S // SWIGLU_ROWS,),
        in_specs=[pl.BlockSpec((SWIGLU_ROWS, D_FF), lambda i: (i, 0)), pl.BlockSpec((SWIGLU_ROWS, D_FF), lambda i: (i, 1))],
        out_specs=pl.BlockSpec((SWIGLU_ROWS, D_FF), lambda i: (i, 0)),
        out_shape=jax.ShapeDtypeStruct((S, D_FF), BF), name=name, compiler_params=_params("parallel"),
    )(gu, gu)


def _swiglu_bwd(gu, dact, *, name):
    S = gu.shape[0]

    def body(g_ref, u_ref, d_ref, o_ref):
        g = g_ref[...].astype(F32)
        u = u_ref[...].astype(F32)
        d = d_ref[...].astype(F32)
        sig = 1.0 / (1.0 + jnp.exp(-g))
        o_ref[:, :D_FF] = (d * u * sig * (1.0 + g * (1.0 - sig))).astype(BF)
        o_ref[:, D_FF:] = (d * g * sig).astype(BF)

    return pl.pallas_call(
        body, grid=(S // SWIGLU_ROWS,),
        in_specs=[pl.BlockSpec((SWIGLU_ROWS, D_FF), lambda i: (i, 0)), pl.BlockSpec((SWIGLU_ROWS, D_FF), lambda i: (i, 1)),
                  pl.BlockSpec((SWIGLU_ROWS, D_FF), lambda i: (i, 0))],
        out_specs=pl.BlockSpec((SWIGLU_ROWS, 2 * D_FF), lambda i: (i, 0)),
        out_shape=jax.ShapeDtypeStruct((S, 2 * D_FF), BF), name=name, compiler_params=_params("parallel"),
    )(gu, gu, dact)


def _band_masks(T, n):
    row = lax.broadcasted_iota(jnp.int32, (T, T), 0)
    col = lax.broadcasted_iota(jnp.int32, (T, T), 1)
    return jnp.logical_and(col >= row, n > 0), col <= row


def _band_fwd(qa, ka, va, qcb, kcb, vcb, *, dil, T, window, name, guest=None):
    L = qa.shape[0]
    nq = L // T
    assert window == T
    nt = (((1,), (1,)), ((), ()))

    def body(q_ref, kp_ref, kc_ref, vp_ref, vc_ref, o_ref, lse_ref):
        valid_prev, valid_cur = _band_masks(T, pl.program_id(1))
        lane = lax.broadcasted_iota(jnp.int32, (T, 128), 1)
        low = lane < HEAD_DIM
        ones = jnp.ones((T, 128), BF)
        lse = jnp.zeros((T, 128), F32)
        def scores(h):
            ps = slice((h // 2) * 128, (h // 2 + 1) * 128)
            qp = q_ref[:, ps] * jnp.asarray(ATTN_SCALE, BF)
            qm = jnp.where(low if h % 2 == 0 else jnp.logical_not(low), qp, jnp.zeros_like(qp))
            s0 = jnp.where(valid_prev, lax.dot_general(qm, kp_ref[:, ps], nt, preferred_element_type=F32), NEG_INF)
            s1 = jnp.where(valid_cur, lax.dot_general(qm, kc_ref[:, ps], nt, preferred_element_type=F32), NEG_INF)
            return s0, s1

        def softmax(s0, s1):
            m = jnp.maximum(jnp.max(s0, axis=1, keepdims=True), jnp.max(s1, axis=1, keepdims=True))
            return m, jnp.exp(s0 - m).astype(BF), jnp.exp(s1 - m).astype(BF)

        def weighted(h, p0, p1):
            ps = slice((h // 2) * 128, (h // 2 + 1) * 128)
            l = jnp.dot(p0, ones, preferred_element_type=F32) + jnp.dot(p1, ones, preferred_element_type=F32)
            acc = jnp.dot(p0, vp_ref[:, ps], preferred_element_type=F32) + jnp.dot(p1, vc_ref[:, ps], preferred_element_type=F32)
            return l, acc

        sc, pr, even = {}, {}, None
        for t in range(N_HEADS + 2):
            if t < N_HEADS:
                sc[t] = scores(t)
            done = None
            if t >= 2:
                m, p0, p1 = pr.pop(t - 2)
                done = (m,) + weighted(t - 2, p0, p1)
            if 1 <= t <= N_HEADS:
                pr[t - 1] = softmax(*sc.pop(t - 1))
            if done is not None:
                h = t - 2
                m, l, acc = done
                lse = jnp.where(lane == h, m + jnp.log(l), lse)
                if h % 2 == 0:
                    even = acc / l
                else:
                    o_ref[:, (h // 2) * 128:(h // 2 + 1) * 128] = jnp.where(low, even, acc / l)
        lse_ref[...] = lse

    def prev(n):
        return jnp.maximum(n - 1, 0)

    blk = lambda f, cb: pl.BlockSpec((T, 1024), lambda r, n: (f(n), cb(r)))
    same = lambda n: n
    outs = _hosted_call(
        body, grid=(dil, nq),
        in_specs=[blk(same, qcb), blk(prev, kcb), blk(same, kcb), blk(prev, vcb), blk(same, vcb)],
        out_specs=[pl.BlockSpec((T, 1024), lambda r, n: (n, r)), pl.BlockSpec((T, 128), lambda r, n: (n, r))],
        out_shape=[jax.ShapeDtypeStruct((L, dil * 1024), F32), jax.ShapeDtypeStruct((L, dil * 128), F32)],
        scratch_shapes=[], args=(qa, ka, ka, va, va), name=name, guest=guest)
    return outs if guest is None else (outs[:2], outs[2:])


def _band_bwd(qa, ka, va, qcb, kcb, vcb, doa, oa, lsea, *, dil, T, window, name, guest=None):
    L = qa.shape[0]
    nq = L // T
    assert window == T
    nt = (((1,), (1,)), ((), ()))
    tn = (((0,), (0,)), ((), ()))

    def body(q_ref, kp_ref, kc_ref, vp_ref, vc_ref, do_ref, o_ref, lse_ref, dq_ref, dk_ref, dv_ref, ck_sc, cv_sc):
        n = pl.program_id(1)

        @pl.when(n == 0)
        def _():
            ck_sc[...] = jnp.zeros_like(ck_sc)
            cv_sc[...] = jnp.zeros_like(cv_sc)

        @pl.when(n < nq)
        def _():
            valid_prev, valid_cur = _band_masks(T, n)
            low = lax.broadcasted_iota(jnp.int32, (T, 128), 1) < HEAD_DIM
            dot = functools.partial(lax.dot_general, preferred_element_type=F32)

            def pair(h):
                return slice((h // 2) * 128, (h // 2 + 1) * 128)

            def products(h):
                ps = pair(h)
                mask = low if h % 2 == 0 else jnp.logical_not(low)
                qp = q_ref[:, ps] * jnp.asarray(ATTN_SCALE, BF)
                dop = do_ref[:, ps]
                qm = jnp.where(mask, qp, jnp.zeros_like(qp))
                dom = jnp.where(mask, dop, jnp.zeros_like(dop))
                s0 = jnp.where(valid_prev, dot(qm, kp_ref[:, ps], nt), NEG_INF)
                s1 = jnp.where(valid_cur, dot(qm, kc_ref[:, ps], nt), NEG_INF)
                return qm, dom, s0, s1, dot(dom, vp_ref[:, ps], nt), dot(dom, vc_ref[:, ps], nt)

            def pointwise(h, qm, dom, s0, s1, dp0, dp1):
                ps = pair(h)
                mask = low if h % 2 == 0 else jnp.logical_not(low)
                prod = do_ref[:, ps].astype(F32) * o_ref[:, ps].astype(F32)
                delta = jnp.sum(jnp.where(mask, prod, 0.0), axis=1, keepdims=True)
                lse = lse_ref[:, h:h + 1]
                p0 = jnp.exp(s0 - lse)
                p1 = jnp.exp(s1 - lse)
                ds0 = (p0 * (dp0 - delta)).astype(BF)
                ds1 = (p1 * (dp1 - delta)).astype(BF)
                return qm, dom, p0.astype(BF), p1.astype(BF), ds0, ds1

            def gradients(h, qm, dom, p0, p1, ds0, ds1):
                ps = pair(h)
                dq = dot(ds0, kp_ref[:, ps], (((1,), (0,)), ((), ()))) + dot(ds1, kc_ref[:, ps], (((1,), (0,)), ((), ())))
                return dq, dot(ds0, qm, tn), dot(p0, dom, tn), dot(ds1, qm, tn), dot(p1, dom, tn)

            st1, st2, even = {}, {}, None
            for t in range(N_HEADS + 2):
                if t < N_HEADS:
                    st1[t] = products(t)
                done = gradients(t - 2, *st2.pop(t - 2)) if t >= 2 else None
                if 1 <= t <= N_HEADS:
                    st2[t - 1] = pointwise(t - 1, *st1.pop(t - 1))
                if done is not None:
                    h = t - 2
                    if h % 2 == 0:
                        even = done
                    else:
                        ps = pair(h)
                        dq_ref[:, ps] = (jnp.where(low, even[0], done[0]) * ATTN_SCALE).astype(BF)
                        dk_ref[:, ps] = (ck_sc[:, ps] + even[1] + done[1]).astype(BF)
                        dv_ref[:, ps] = (cv_sc[:, ps] + even[2] + done[2]).astype(BF)
                        ck_sc[:, ps] = even[3] + done[3]
                        cv_sc[:, ps] = even[4] + done[4]

        @pl.when(n == nq)
        def _():
            dk_ref[...] = ck_sc[...].astype(BF)
            dv_ref[...] = cv_sc[...].astype(BF)

    def cur(n):
        return jnp.minimum(n, nq - 1)

    def prev(n):
        return jnp.maximum(cur(n) - 1, 0)

    blk = lambda f, cb: pl.BlockSpec((T, 1024), lambda r, n: (f(n), cb(r)))
    own = lambda r: r
    outs = _hosted_call(
        body, grid=(dil, nq + 1),
        in_specs=[blk(cur, qcb), blk(prev, kcb), blk(cur, kcb), blk(prev, vcb), blk(cur, vcb), blk(cur, own), blk(cur, own),
                  pl.BlockSpec((T, 128), lambda r, n: (cur(n), r))],
        out_specs=[blk(cur, own), blk(lambda n: jnp.maximum(n - 1, 0), own), blk(lambda n: jnp.maximum(n - 1, 0), own)],
        out_shape=[jax.ShapeDtypeStruct((L, dil * 1024), BF)] * 3,
        scratch_shapes=[pltpu.VMEM((T, 1024), F32), pltpu.VMEM((T, 1024), F32)],
        args=(qa, ka, ka, va, va, doa, oa, lsea), name=name, guest=guest)
    return outs if guest is None else (outs[:3], outs[3:])


FOX_T = 512
FOX_TQ = 512
FOX_TK = 512
FOX_TQ_BWD = 512
FOX_ROWS = 256


def _fox_fwd(qkv, cT, *, name, guest=None):
    S = qkv.shape[0]
    T, TK, R = FOX_TQ, FOX_TK, FOX_ROWS
    nq = S // T
    nt = (((1,), (1,)), ((), ()))
    chains = [(h, rh) for h in range(N_HEADS) for rh in range(T // R)]
    pairs = [(n, j) for n in range(nq) for j in range((n * T + T - 1) // TK + 1)]
    schedule = [jnp.asarray([p[i] for p in pairs], jnp.int32) for i in range(2)]

    def body(n_tab, j_tab, q_ref, k_ref, v_ref, ct_ref, o_ref, lse_ref, m_sc, l_sc, acc_sc):
        n = n_tab[pl.program_id(0)]
        j = j_tab[pl.program_id(0)]
        last_j = (n * T + T - 1) // TK
        lane = lax.broadcasted_iota(jnp.int32, (R, 128), 1)
        low = lane < HEAD_DIM
        ones = jnp.ones((TK, 128), BF)

        @pl.when(j == 0)
        def _():
            m_sc[...] = jnp.full(m_sc.shape, NEG_INF, F32)
            l_sc[...] = jnp.zeros_like(l_sc)
            acc_sc[...] = jnp.zeros_like(acc_sc)

        def step(diagonal):
            def pair(h):
                return slice((h // 2) * 128, (h // 2 + 1) * 128)

            def rows(rh):
                return slice(rh * R, (rh + 1) * R)

            def scores(h, rh):
                qp = q_ref[rows(rh), pair(h)] * jnp.asarray(ATTN_SCALE, BF)
                qm = jnp.where(low if h % 2 == 0 else jnp.logical_not(low), qp, jnp.zeros_like(qp))
                s = lax.dot_general(qm, k_ref[:, pair(h)], nt, preferred_element_type=F32) - ct_ref[h:h + 1, :]
                if diagonal:
                    ahead = lax.broadcasted_iota(jnp.int32, (R, TK), 1) - lax.broadcasted_iota(jnp.int32, (R, TK), 0)
                    s = jnp.where(ahead <= n * T + rh * R - j * TK, s, NEG_INF)
                return s

            def softmax(h, rh, s):
                m_prev = m_sc[h, rows(rh), :]
                m_new = jnp.maximum(m_prev, jnp.max(s, axis=1, keepdims=True))
                p = jnp.exp(s - jnp.concatenate([m_new] * (TK // 128), axis=1)).astype(BF)
                return m_new, jnp.exp(m_prev - m_new), p

            def weighted(h, p):
                vx = jnp.concatenate([v_ref[:, pair(h)], ones], axis=1)
                return jnp.dot(p, vx, preferred_element_type=F32)

            sc, pr, even = {}, {}, {}
            nch = len(chains)
            for t in range(nch + 2):
                if t < nch:
                    sc[t] = scores(*chains[t])
                done = None
                if t >= 2:
                    m_new, alpha, p = pr.pop(t - 2)
                    done = (m_new, alpha, weighted(chains[t - 2][0], p))
                if 1 <= t <= nch:
                    pr[t - 1] = softmax(*chains[t - 1], sc.pop(t - 1))
                if done is not None:
                    h, rh = chains[t - 2]
                    m_new, alpha, pv = done
                    m_sc[h, rows(rh), :] = m_new
                    l_sc[h, rows(rh), :] = alpha * l_sc[h, rows(rh), :] + pv[:, 128:]
                    if h % 2 == 0:
                        even[rh] = (alpha, pv[:, :128])
                    else:
                        a0, pv0 = even.pop(rh)
                        acc = acc_sc[h // 2, rows(rh), :]
                        acc_sc[h // 2, rows(rh), :] = jnp.where(low, a0 * acc + pv0, alpha * acc + pv[:, :128])

        @pl.when(j < last_j)
        def _():
            step(False)

        @pl.when(j == last_j)
        def _():
            step(True)
            lane_t = lax.broadcasted_iota(jnp.int32, (T, 128), 1)
            low_t = lane_t < HEAD_DIM
            lse = jnp.zeros((T, 128), F32)
            for h in range(N_HEADS):
                lse = jnp.where(lane_t == h, m_sc[h] + jnp.log(l_sc[h]), lse)
            lse_ref[...] = lse
            for hp in range(N_HEADS // 2):
                inv = jnp.where(low_t, 1.0 / l_sc[2 * hp], 1.0 / l_sc[2 * hp + 1])
                o_ref[:, hp * 128:(hp + 1) * 128] = (acc_sc[hp] * inv).astype(BF)

    outs = _hosted_call(
        body, grid=(len(pairs),),
        in_specs=[pl.BlockSpec((T, 1024), lambda t, n, j: (n[t], 0)), pl.BlockSpec((TK, 1024), lambda t, n, j: (j[t], 1)),
                  pl.BlockSpec((TK, 1024), lambda t, n, j: (j[t], 2)), pl.BlockSpec((GATE_LANES, TK), lambda t, n, j: (0, j[t]))],
        out_specs=[pl.BlockSpec((T, 1024), lambda t, n, j: (n[t], 0)), pl.BlockSpec((T, 128), lambda t, n, j: (n[t], 0))],
        out_shape=[jax.ShapeDtypeStruct((S, 1024), BF), jax.ShapeDtypeStruct((S, 128), F32)],
        scratch_shapes=[pltpu.VMEM((N_HEADS, T, 128), F32), pltpu.VMEM((N_HEADS, T, 128), F32),
                        pltpu.VMEM((N_HEADS // 2, T, 128), F32)],
        args=(qkv, qkv, qkv, cT), name=name, guest=guest, schedule=schedule)
    return outs if guest is None else (outs[:2], outs[2:])


def _fox_bwd(qkv, cT, do, o, lse, *, name, guest=None):
    S = qkv.shape[0]
    T, TQ, R = FOX_T, FOX_TQ_BWD, FOX_ROWS
    nk, nq = S // T, S // TQ
    nt = (((1,), (1,)), ((), ()))
    tn = (((0,), (0,)), ((), ()))
    nn = (((1,), (0,)), ((), ()))
    chains = [(h, rh) for h in range(N_HEADS) for rh in range(TQ // R)]
    dot = functools.partial(lax.dot_general, preferred_element_type=F32)
    pairs = [(kb, qb) for kb in range(nk) for qb in range(kb * T // TQ, nq)]
    schedule = [jnp.asarray([p[i] for p in pairs], jnp.int32) for i in range(2)]

    def body(kb_tab, qb_tab, q_ref, k_ref, v_ref, ct_ref, do_ref, o_ref, lse_ref, dq_ref, dk_ref, dv_ref, dct_ref, dcq_ref,
             dq_sc, dk_sc, dv_sc, dc_sc, dcq_sc):
        kb = kb_tab[pl.program_id(0)]
        qb = qb_tab[pl.program_id(0)]
        jq = qb - kb * T // TQ
        lane = lax.broadcasted_iota(jnp.int32, (R, 128), 1)
        low = lane < HEAD_DIM
        ones_k = jnp.ones((T, 128), BF)
        ones_r = jnp.ones((8, R), BF)

        @pl.when(jnp.logical_and(kb == 0, jq == 0))
        def _():
            dq_sc[...] = jnp.zeros_like(dq_sc)
            dcq_sc[...] = jnp.zeros_like(dcq_sc)

        @pl.when(jq == 0)
        def _():
            dk_sc[...] = jnp.zeros_like(dk_sc)
            dv_sc[...] = jnp.zeros_like(dv_sc)
            dc_sc[...] = jnp.zeros_like(dc_sc)

        def step(diagonal):
            def pair(h):
                return slice((h // 2) * 128, (h // 2 + 1) * 128)

            def rows(rh):
                return slice(rh * R, (rh + 1) * R)

            def qrows(rh):
                return pl.ds(pl.multiple_of(qb * TQ + rh * R, R), R)

            def products(h, rh):
                mask = low if h % 2 == 0 else jnp.logical_not(low)
                qp = q_ref[rows(rh), pair(h)] * jnp.asarray(ATTN_SCALE, BF)
                dop = do_ref[rows(rh), pair(h)]
                qm = jnp.where(mask, qp, jnp.zeros_like(qp))
                dom = jnp.where(mask, dop, jnp.zeros_like(dop))
                s = dot(qm, k_ref[:, pair(h)], nt) - ct_ref[h:h + 1, :]
                if diagonal:
                    ahead = lax.broadcasted_iota(jnp.int32, (R, T), 1) - lax.broadcasted_iota(jnp.int32, (R, T), 0)
                    s = jnp.where(ahead <= qb * TQ + rh * R - kb * T, s, NEG_INF)
                return qm, dom, s, dot(dom, v_ref[:, pair(h)], nt)

            def pointwise(h, rh, qm, dom, s, dp):
                mask = low if h % 2 == 0 else jnp.logical_not(low)
                prod = do_ref[rows(rh), pair(h)].astype(F32) * o_ref[rows(rh), pair(h)].astype(F32)
                delta = jnp.sum(jnp.where(mask, prod, 0.0), axis=1, keepdims=True)
                p = jnp.exp(s - lse_ref[rows(rh), h:h + 1])
                ds = (p * (dp - delta)).astype(BF)
                return qm, dom, p.astype(BF), ds

            def gradients(h, qm, dom, p, ds):
                kx = jnp.concatenate([k_ref[:, pair(h)], ones_k], axis=1)
                return dot(ds, kx, nn), dot(qm, ds, tn), dot(dom, p, tn), dot(ones_r, ds, nn)

            st1, st2, even = {}, {}, {}
            dcq_tiles = [jnp.zeros((R, 128), F32) for _ in range(TQ // R)]
            nch = len(chains)
            for t in range(nch + 2):
                if t < nch:
                    st1[t] = products(*chains[t])
                done = gradients(chains[t - 2][0], *st2.pop(t - 2)) if t >= 2 else None
                if 1 <= t <= nch:
                    st2[t - 1] = pointwise(*chains[t - 1], *st1.pop(t - 1))
                if done is not None:
                    h, rh = chains[t - 2]
                    dq_rsum, dk, dv, csum = done
                    dq = dq_rsum[:, :128]
                    dcq_tiles[rh] = jnp.where(lane == h, dq_rsum[:, 128:], dcq_tiles[rh])
                    dc_sc[h:h + 1, :] -= csum[0:1, :]
                    if h % 2 == 0:
                        even[rh] = (dq, dk, dv)
                    else:
                        dq0, dk0, dv0 = even.pop(rh)
                        dq_sc[qrows(rh), pair(h)] += jnp.where(low, dq0, dq) * ATTN_SCALE
                        dk_sc[h // 2] += dk0 + dk
                        dv_sc[h // 2] += dv0 + dv
            for rh in range(TQ // R):
                dcq_sc[qrows(rh), :] += dcq_tiles[rh]

        @pl.when(jq > 0)
        def _():
            step(False)

        @pl.when(jq == 0)
        def _():
            step(True)

        @pl.when(qb == nq - 1)
        def _():
            for hp in range(N_HEADS // 2):
                dk_ref[:, hp * 128:(hp + 1) * 128] = dk_sc[hp].T.astype(BF)
                dv_ref[:, hp * 128:(hp + 1) * 128] = dv_sc[hp].T.astype(BF)
            dct_ref[...] = dc_sc[...]

        @pl.when(jnp.logical_and(kb == nk - 1, qb == nq - 1))
        def _():
            def put(i, carry):
                r = pl.ds(pl.multiple_of(i * T, T), T)
                dq_ref[r, :] = dq_sc[r, :].astype(BF)
                return carry
            lax.fori_loop(0, nk, put, 0)
            dcq_ref[...] = dcq_sc[...]

    qblk = lambda col: pl.BlockSpec((TQ, 1024), lambda t, kb, qb: (qb[t], col))
    kblk = lambda col: pl.BlockSpec((T, 1024), lambda t, kb, qb: (kb[t], col))
    whole = pl.BlockSpec((S, 1024), lambda t, kb, qb: (0, 0))
    outs = _hosted_call(
        body, grid=(len(pairs),),
        in_specs=[qblk(0), kblk(1), kblk(2), pl.BlockSpec((GATE_LANES, T), lambda t, kb, qb: (0, kb[t])), qblk(0), qblk(0),
                  pl.BlockSpec((TQ, 128), lambda t, kb, qb: (qb[t], 0))],
        out_specs=[whole, kblk(0), kblk(0), pl.BlockSpec((GATE_LANES, T), lambda t, kb, qb: (0, kb[t])),
                   pl.BlockSpec((S, GATE_LANES), lambda t, kb, qb: (0, 0))],
        out_shape=[jax.ShapeDtypeStruct((S, 1024), BF)] * 3 + [jax.ShapeDtypeStruct((GATE_LANES, S), F32),
                                                               jax.ShapeDtypeStruct((S, GATE_LANES), F32)],
        scratch_shapes=[pltpu.VMEM((S, 1024), F32), pltpu.VMEM((N_HEADS // 2, 128, T), F32), pltpu.VMEM((N_HEADS // 2, 128, T), F32),
                        pltpu.VMEM((GATE_LANES, T), F32), pltpu.VMEM((S, GATE_LANES), F32)],
        args=(qkv, qkv, qkv, cT, do, o, lse), name=name, guest=guest, schedule=schedule)
    return outs if guest is None else (outs[:5], outs[5:])


def _to_natural(src_ref, buf, d, width):
    rows = buf.shape[1]
    for r in range(d):
        for ch in range(width // 128):
            lanes = slice(r * width + ch * 128, r * width + (ch + 1) * 128)
            buf.at[ch][pl.ds(r, rows // d, stride=d), :] = src_ref[:, lanes].astype(F32)
    return jnp.concatenate([buf[ch] for ch in range(width // 128)], axis=1)


def _to_view(val, buf, dst_ref, d, width):
    rows = buf.shape[1]
    for ch in range(width // 128):
        buf[ch] = val[:, ch * 128:(ch + 1) * 128]
    for r in range(d):
        for ch in range(width // 128):
            lanes = slice(r * width + ch * 128, r * width + (ch + 1) * 128)
            dst_ref[:, lanes] = buf.at[ch][pl.ds(r, rows // d, stride=d), :].astype(dst_ref.dtype)


def _view_spec(rows, d, width):
    return pl.BlockSpec((rows // d, d * width), lambda i, *_: (i, 0))


def _combine_groups(os, lses, dils, *, name):
    ng = len(os)
    S = os[0].shape[0] * dils[0]
    tm = ROW_TILE
    views = sorted(set(dils))

    def body(*refs):
        o_refs, l_refs = refs[:ng], refs[ng:2 * ng]
        outs = refs[2 * ng:2 * ng + 2 * len(views)]
        wide, narrow = refs[-2], refs[-1]
        ls = [l_refs[g][...] if dils[g] == 1 else _to_natural(l_refs[g], narrow, dils[g], 128) for g in range(ng)]
        m = functools.reduce(jnp.maximum, ls)
        es = [jnp.exp(l - m) for l in ls]
        den = functools.reduce(jnp.add, es)
        ws = [e / den for e in es]
        lse = m + jnp.log(den)
        og = [o_refs[g][...] if dils[g] == 1 else _to_natural(o_refs[g], wide, dils[g], 1024) for g in range(ng)]
        cols = []
        for h in range(N_HEADS):
            hs = slice(h * HEAD_DIM, (h + 1) * HEAD_DIM)
            acc = ws[0][:, h:h + 1] * og[0][:, hs]
            for g in range(1, ng):
                acc = acc + ws[g][:, h:h + 1] * og[g][:, hs]
            cols.append(acc)
        o = jnp.concatenate(cols, axis=1)
        for k, d in enumerate(views):
            if d == 1:
                outs[2 * k][...] = o.astype(BF)
                outs[2 * k + 1][...] = lse
            else:
                _to_view(o, wide, outs[2 * k], d, 1024)
                _to_view(lse, narrow, outs[2 * k + 1], d, 128)

    out_specs, out_shape = [], []
    for d in views:
        out_specs += [_view_spec(tm, d, 1024), _view_spec(tm, d, 128)]
        out_shape += [jax.ShapeDtypeStruct((S // d, d * 1024), BF), jax.ShapeDtypeStruct((S // d, d * 128), F32)]
    res = pl.pallas_call(
        body, grid=(S // tm,), in_specs=[_view_spec(tm, d, 1024) for d in dils] + [_view_spec(tm, d, 128) for d in dils],
        out_specs=out_specs, out_shape=out_shape,
        scratch_shapes=[pltpu.VMEM((8, tm, 128), F32), pltpu.VMEM((1, tm, 128), F32)],
        name=name, compiler_params=_params("parallel"),
    )(*os, *lses)
    return {d: (res[2 * k], res[2 * k + 1]) for k, d in enumerate(views)}


def _assemble(parts, rope_flags, rope, dils, *, name):
    n = len(parts)
    S = parts[0].shape[0] * dils[0]
    use_rope = any(rope_flags)
    tm = ROW_TILE

    def body(*refs):
        out_ref, natural = refs[-2], refs[-1]
        for b in range(n):
            cols = slice(b * 1024, (b + 1) * 1024)
            d = dils[b]
            val = refs[b][...].astype(F32) if d == 1 else _to_natural(refs[b], natural, d, 1024)
            if rope_flags[b]:
                cos_ref, sa_ref, sb_ref = refs[n:n + 3]
                val = _rope_rotate(val, cos_ref[...], sa_ref[...], sb_ref[...])
            out_ref[:, cols] = val.astype(BF)

    in_specs = [_view_spec(tm, d, 1024) for d in dils]
    args = list(parts)
    if use_rope:
        in_specs += [pl.BlockSpec((tm, 128), lambda i: (i, 0))] * 3
        args += list(rope)
    return pl.pallas_call(
        body, grid=(S // tm,), in_specs=in_specs, out_specs=pl.BlockSpec((tm, n * 1024), lambda i: (i, 0)),
        out_shape=jax.ShapeDtypeStruct((S, n * 1024), BF), scratch_shapes=[pltpu.VMEM((8, tm, 128), F32)],
        name=name, compiler_params=_params("parallel"),
    )(*args)


GATE_ROWS = 512


def _gate_fwd(z, bf, *, name):
    S = z.shape[0]

    def body(z_ref, b_ref, ct_ref, carry):
        i = pl.program_id(0)

        @pl.when(i == 0)
        def _():
            carry[...] = jnp.zeros_like(carry)

        zz = z_ref[...] + b_ref[...]
        logf = jnp.minimum(zz, 0.0) - jnp.log(1.0 + jnp.exp(-jnp.abs(zz)))
        tri = (lax.broadcasted_iota(jnp.int32, (GATE_ROWS, GATE_ROWS), 0)
               >= lax.broadcasted_iota(jnp.int32, (GATE_ROWS, GATE_ROWS), 1)).astype(F32)
        cs = jnp.dot(tri, logf, precision=lax.Precision.HIGHEST, preferred_element_type=F32) + carry[...]
        ct_ref[...] = cs.T
        carry[...] = cs[GATE_ROWS - 1:GATE_ROWS, :]

    return pl.pallas_call(
        body, grid=(S // GATE_ROWS,),
        in_specs=[pl.BlockSpec((GATE_ROWS, GATE_LANES), lambda i: (i, 0)), pl.BlockSpec((1, GATE_LANES), lambda i: (0, 0))],
        out_specs=pl.BlockSpec((GATE_LANES, GATE_ROWS), lambda i: (0, i)),
        out_shape=jax.ShapeDtypeStruct((GATE_LANES, S), F32),
        scratch_shapes=[pltpu.VMEM((1, GATE_LANES), F32)], name=name, compiler_params=_params("arbitrary"),
    )(z, bf)


def _gate_bwd(z, bf, dcT, dcq, *, name):
    S = z.shape[0]
    nb = S // GATE_ROWS

    def body(z_ref, b_ref, dct_ref, dcq_ref, dz_ref, db_ref, carry):
        i = pl.program_id(0)

        @pl.when(i == 0)
        def _():
            carry[...] = jnp.zeros_like(carry)
            db_ref[...] = jnp.zeros_like(db_ref)

        dc = dct_ref[...].T + dcq_ref[...]
        tri = (lax.broadcasted_iota(jnp.int32, (GATE_ROWS, GATE_ROWS), 0)
               <= lax.broadcasted_iota(jnp.int32, (GATE_ROWS, GATE_ROWS), 1)).astype(F32)
        dl = jnp.dot(tri, dc, precision=lax.Precision.HIGHEST, preferred_element_type=F32) + carry[...]
        carry[...] = dl[0:1, :]
        zz = z_ref[...] + b_ref[...]
        dz = dl * (1.0 / (1.0 + jnp.exp(zz)))
        lane = lax.broadcasted_iota(jnp.int32, dz.shape, 1)
        dz = jnp.where(lane < N_HEADS, dz, 0.0)
        dz_ref[...] = dz.astype(BF)
        db_ref[...] += jnp.sum(dz, axis=0, keepdims=True)

    return pl.pallas_call(
        body, grid=(nb,),
        in_specs=[pl.BlockSpec((GATE_ROWS, GATE_LANES), lambda i: (nb - 1 - i, 0)), pl.BlockSpec((1, GATE_LANES), lambda i: (0, 0)),
                  pl.BlockSpec((GATE_LANES, GATE_ROWS), lambda i: (0, nb - 1 - i)),
                  pl.BlockSpec((GATE_ROWS, GATE_LANES), lambda i: (nb - 1 - i, 0))],
        out_specs=[pl.BlockSpec((GATE_ROWS, GATE_LANES), lambda i: (nb - 1 - i, 0)), pl.BlockSpec((1, GATE_LANES), lambda i: (0, 0))],
        out_shape=[jax.ShapeDtypeStruct((S, GATE_LANES), BF), jax.ShapeDtypeStruct((1, GATE_LANES), F32)],
        scratch_shapes=[pltpu.VMEM((1, GATE_LANES), F32)], name=name, compiler_params=_params("arbitrary"),
    )(z, bf, dcT, dcq)


def _rope_tables(S):
    half = ROT_DIM // 2
    inv_freq = ROPE_THETA ** (-jnp.arange(half, dtype=F32) * 2.0 / ROT_DIM)
    ang = jnp.arange(S, dtype=F32)[:, None] * inv_freq[None, :]
    cos, sin = jnp.cos(ang), jnp.sin(ang)
    zero = jnp.zeros((S, HEAD_DIM - ROT_DIM), F32)
    zh = jnp.zeros((S, half), F32)
    cos_h = jnp.concatenate([cos, cos, jnp.ones_like(zero)], axis=1)
    sa_h = jnp.concatenate([-sin, zh, zero], axis=1)
    sb_h = jnp.concatenate([zh, sin, zero], axis=1)
    two = lambda t: jnp.concatenate([t, t], axis=1)
    return two(cos_h), two(sa_h), two(sb_h)


def _ffn_fwd(h, norm, w_gu, w_down, tag):
    n = _rms_fwd(h, norm, name=f"ffn{tag}_norm")
    gu = _mm_nn(n, w_gu, tm=2 * MM_ROWS, tn=1408, out_dtype=BF, name=f"ffn{tag}_gu")
    act = _swiglu_fwd(gu, name=f"ffn{tag}_act")
    out = _mm_nn(act, w_down, tm=MM_ROWS // 2, tn=1024, out_dtype=F32, name=f"ffn{tag}_down", resid=h)
    return out, (h, n, gu, act)


def _ffn_bwd(dh, dhb, saved, norm, w_gu, w_down, tag, ride=None):
    h, n, gu, act = saved
    dact = _mm_nt(dhb, w_down, tm=2 * MM_ROWS, to=1408, tn=1024, out_dtype=BF, name=f"ffn{tag}_dact")
    dw_down = _mm_tn(act, dhb, tk=1408, tn=1024, tm=2 * MM_ROWS, out_dtype=BF, name=f"ffn{tag}_dwdown")
    dgu = _swiglu_bwd(gu, dact, name=f"ffn{tag}_dgu")
    dn_call = lambda guest: _mm_nt(dgu, w_gu, tm=2 * MM_ROWS, to=1024, tn=1408, out_dtype=F32, name=f"ffn{tag}_dn", guest=guest)
    dn = dn_call(None) if ride is None else ride(dn_call)
    dw_gu = _mm_tn(n, dgu, tk=1024, tn=1408, tm=2 * MM_ROWS, out_dtype=BF, name=f"ffn{tag}_dwgu")
    dx, dxb, dg = _rms_bwd(h, norm, dn, dh, name=f"ffn{tag}_dnorm")
    return dx, dxb, dg, dw_gu, dw_down


def _local_step(x, tgt, w, mats, fetch, exchange):
    S = x.shape[0]
    rope_f = _rope_tables(S)
    rope_b = (rope_f[0], -rope_f[1], -rope_f[2])
    g, partial, landed = {}, {}, {}
    w = dict(w, ffn_w_gu={}, ffn_w_down={})

    def bring(call, indices):
        bufs = [mats[wi] for wi in indices]
        if fetch is None:
            return call(None), bufs
        return call(fetch(indices, bufs))

    def ride(call, indices):
        guest = exchange(indices, [partial[wi] for wi in indices]) if indices else None
        res = call(guest)
        if guest is None:
            return res
        res, outs = res
        landed.update(zip(indices, outs))
        return res

    n0 = _rms_fwd(x, w["a_norm"], name="a_norm")
    dils = [d for _, d in DILATED_PATTERNS]
    projs, (w["ffn_w_gu"][0], w["ffn_w_down"][0]) = bring(
        lambda guest: _mm_nn(n0, w["a_w_in"], tm=MM_ROWS, tn=1024, out_dtype=BF, name="a_proj", rope=rope_f, guest=guest,
                             groups=dils), [4, 6])
    block = lambda t, dil: (lambda r: t * dil + r)
    o_parts, lse_parts = [], []
    for gi, (window, dil) in enumerate(DILATED_PATTERNS):
        pv = projs[gi]
        attend = lambda guest: _band_fwd(pv, pv, pv, block(0, dil), block(1, dil), block(2, dil), dil=dil, T=128,
                                         window=window // dil, name=f"a_attn{gi}", guest=guest)
        if gi == 0:
            (o_g, lse_g), (w["a_w_out"],) = bring(attend, [1])
        elif gi == 1:
            (o_g, lse_g), (b_in,) = bring(attend, [2])
        else:
            (o_g, lse_g), (w["b_w_out"],) = bring(attend, [3])
        o_parts.append(o_g)
        lse_parts.append(lse_g)
    b_in = b_in.transpose(1, 0, 2).reshape(D_MODEL, -1)
    w["b_w_qkv"] = b_in[:, :QKV_COLS]
    w["b_w_f"] = jnp.pad(b_in[:, QKV_COLS:], ((0, 0), (0, GATE_LANES + QKV_COLS - b_in.shape[1])))
    mixed = _combine_groups(o_parts, lse_parts, dils, name="a_combine")
    o_a = mixed[1][0]
    h1 = _mm_nn(o_a, w["a_w_out"], tm=MM_ROWS, tn=1024, out_dtype=F32, name="a_out", resid=x)
    h2, ffn0 = _ffn_fwd(h1, w["ffn_norm"][0:1], w["ffn_w_gu"][0], w["ffn_w_down"][0], 0)

    n2 = _rms_fwd(h2, w["b_norm"], name="b_norm")
    qkv = _mm_nn(n2, w["b_w_qkv"], tm=MM_ROWS, tn=1024, out_dtype=BF, name="b_proj")
    zf = _mm_nn(n2, w["b_w_f"], tm=MM_ROWS, tn=GATE_LANES, out_dtype=F32, name="b_gate_proj")
    cT = _gate_fwd(zf, w["b_f"], name="b_gate")
    (o_b, lse_b), (w["ffn_w_gu"][1], w["ffn_w_down"][1]) = bring(lambda guest: _fox_fwd(qkv, cT, name="b_attn", guest=guest), [5, 7])
    h3 = _mm_nn(o_b, w["b_w_out"], tm=MM_ROWS, tn=1024, out_dtype=F32, name="b_out", resid=h2)
    h4, ffn1 = _ffn_fwd(h3, w["ffn_norm"][1:2], w["ffn_w_gu"][1], w["ffn_w_down"][1], 1)

    loss, dh4, dh4b, g["final_norm"] = _loss_head(h4, w["final_norm"], tgt, name="loss_head")

    dh3, dh3b, dg_f1, partial[5], partial[7] = _ffn_bwd(dh4, dh4b, ffn1, w["ffn_norm"][1:2], w["ffn_w_gu"][1], w["ffn_w_down"][1], 1)

    do_b = _mm_nt(dh3b, w["b_w_out"], tm=MM_ROWS, to=1024, tn=1024, out_dtype=BF, name="b_do")
    partial[3] = _mm_tn(o_b, dh3b, tk=1024, tn=1024, tm=MM_ROWS, out_dtype=BF, name="b_dwout")
    dq, dk, dv, dcT, dcq = ride(lambda guest: _fox_bwd(qkv, cT, do_b, o_b, lse_b, name="b_attn_bwd", guest=guest), [5, 7, 3])
    dz, g["b_f"] = _gate_bwd(zf, w["b_f"], dcT, dcq, name="b_gate_bwd")
    dqkv = _assemble([dq, dk, dv], [False] * 3, None, [1] * 3, name="b_dproj")
    dn2 = _mm_nt(dz, w["b_w_f"], tm=MM_ROWS, to=1024, tn=GATE_LANES, out_dtype=F32, name="b_dn_gate")
    dn2 = _mm_nt(dqkv, w["b_w_qkv"], tm=MM_ROWS, to=1024, tn=1024, out_dtype=F32, name="b_dn", add=dn2)
    g_qkv = _mm_tn(n2, dqkv, tk=1024, tn=1024, tm=MM_ROWS, out_dtype=BF, name="b_dwqkv")
    g_f = _mm_tn(n2, dz, tk=1024, tn=GATE_LANES, tm=MM_ROWS, out_dtype=BF, name="b_dwf")
    g_b_in = jnp.concatenate([g_qkv, g_f[:, :N_HEADS]], axis=1)
    partial[2] = g_b_in.reshape(D_MODEL, N_CHIPS, -1).transpose(1, 0, 2)
    dh2, dh2b, g["b_norm"] = _rms_bwd(h2, w["b_norm"], dn2, dh3, name="b_dnorm")

    dh1, dh1b, dg_f0, partial[4], partial[6] = _ffn_bwd(dh2, dh2b, ffn0, w["ffn_norm"][0:1], w["ffn_w_gu"][0], w["ffn_w_down"][0], 0,
                                                      ride=lambda call: ride(call, [2]))
    g["ffn_norm"] = jnp.concatenate([dg_f0, dg_f1], axis=0)

    views = tuple(sorted(set(dils)))
    do_a = dict(zip(views, _mm_nt(dh1b, w["a_w_out"], tm=MM_ROWS, to=1024, tn=1024, out_dtype=BF, name="a_do", views=views)))
    partial[1] = _mm_tn(o_a, dh1b, tk=1024, tn=1024, tm=MM_ROWS, out_dtype=BF, name="a_dwout")
    riders = {0: [4], 1: [6, 1], 2: []}
    parts = []
    for gi, (window, dil) in enumerate(DILATED_PATTERNS):
        pv = projs[gi]
        res = ride(lambda guest: _band_bwd(pv, pv, pv, block(0, dil), block(1, dil), block(2, dil), do_a[dil],
                                           mixed[dil][0], mixed[dil][1], dil=dil, T=128,
                                           window=window // dil, name=f"a_attn_bwd{gi}", guest=guest), riders[gi])
        parts += list(res)
    dproj = _assemble(parts, [True, True, False] * 3, rope_b, [d for _, d in DILATED_PATTERNS for _ in range(3)], name="a_dproj")
    partial[0] = _mm_tn(n0, dproj, tk=1024, tn=1024, tm=2 * MM_ROWS, out_dtype=BF, name="a_dwin")
    dn0 = ride(lambda guest: _mm_nt(dproj, w["a_w_in"], tm=2 * MM_ROWS, to=1024, tn=1024, out_dtype=F32, name="a_dn", guest=guest), [0])
    dx, _, g["a_norm"] = _rms_bwd(x, w["a_norm"], dn0, dh1, name="a_dnorm")
    return loss, dx, g, partial, landed


ANY = pl.BlockSpec(memory_space=pl.ANY)


def _place():
    x, y, c = lax.axis_index("x"), lax.axis_index("y"), lax.axis_index("c")
    chips = [(1 - x, y), (x, 1 - y), (1 - x, 1 - y)]
    return x, y, c, chips


def _shard_slice(ref, kind, rows, cols, s, half):
    hr = rows // 2
    if kind == "col":
        return ref.at[pl.ds(half * hr, hr), pl.ds(pl.multiple_of(s * cols, 128), cols)]
    if kind == "row":
        return ref.at[pl.ds(pl.multiple_of(s * rows + half * hr, 16), hr), :]
    return ref.at[s, pl.ds(half * hr, hr), :]


def _whole_shape(kind, rows, cols):
    return {"col": (rows, N_CHIPS * cols), "row": (N_CHIPS * rows, cols), "stack": (N_CHIPS, rows, cols)}[kind]


def _own_block(kind, rows, tr, cols):
    per = rows // tr

    def spec(half_rows):
        off = (lambda p: 0) if half_rows is None else (lambda p: p[1] * (half_rows // tr))
        if kind == "col":
            return pl.BlockSpec((tr, cols), lambda i, p: (off(p) + i, p[0]))
        if kind == "row":
            return pl.BlockSpec((tr, cols), lambda i, p: (p[0] * per + off(p) + i, 0))
        return pl.BlockSpec((None, tr, cols), lambda i, p: (p[0], off(p) + i, 0))
    return spec


def _place_shard(shards, layer, kind, place, *, name):
    _, rows, cols = shards.shape
    tr = 256 if rows % 256 == 0 else rows // 2

    def body(p_ref, s_ref, o_ref):
        o_ref[...] = s_ref[...].astype(BF)

    return pl.pallas_call(
        body,
        grid_spec=pltpu.PrefetchScalarGridSpec(
            num_scalar_prefetch=1, grid=(rows // tr,),
            in_specs=[pl.BlockSpec((None, tr, cols), lambda i, p: (layer, i, 0))],
            out_specs=_own_block(kind, rows, tr, cols)(None)),
        out_shape=jax.ShapeDtypeStruct(_whole_shape(kind, rows, cols), BF),
        name=name, compiler_params=_params("arbitrary"),
    )(place, shards)


def _gather_weights(placed, kinds, dims):
    nw = len(placed)

    def body(*refs):
        dst = refs[nw:2 * nw]
        send_sems, recv_sems = refs[2 * nw:]
        x, y, c, chips = _place()
        me = 2 * x + y
        sibling = (x, y, 1 - c)

        def copy(wi, k, s, half, to):
            p = _shard_slice(dst[wi], kinds[wi], dims[wi][0], dims[wi][1], s, half)
            return pltpu.make_async_remote_copy(src_ref=p, dst_ref=p, send_sem=send_sems.at[wi * 6 + k],
                                                recv_sem=recv_sems.at[wi * 6 + k], device_id=to, device_id_type=MESH)

        first, passed = [], []
        for wi in range(nw):
            for j, chip in enumerate(chips):
                cp = copy(wi, j, me, c, (*chip, c))
                cp.start()
                first.append(cp)
        for wi in range(nw):
            for j, chip in enumerate(chips):
                s = 2 * chip[0] + chip[1]
                copy(wi, j, s, c, (x, y, c)).wait_recv()
                cp = copy(wi, 3 + j, s, c, sibling)
                cp.start()
                passed.append(cp)
        for wi in range(nw):
            for j, chip in enumerate(chips):
                s = 2 * chip[0] + chip[1]
                copy(wi, 3 + j, s, 1 - c, (x, y, c)).wait_recv()
        for cp in first + passed:
            cp.wait_send()

    return pl.pallas_call(
        body, in_specs=[ANY] * nw, out_specs=[ANY] * nw,
        out_shape=[jax.ShapeDtypeStruct(p.shape, p.dtype) for p in placed],
        input_output_aliases={wi: wi for wi in range(nw)},
        scratch_shapes=[pltpu.SemaphoreType.DMA((nw * 6,)), pltpu.SemaphoreType.DMA((nw * 6,))],
        name="gather_weights",
    )(*placed)


def _fetch_guest(placed, kinds, dims):
    nw = len(placed)

    def copies(dst, send_sems, recv_sems, incoming):
        x, y, c, chips = _place()
        out = []
        for wi in range(nw):
            for j, chip in enumerate(chips):
                s = 2 * chip[0] + chip[1] if incoming else 2 * x + y
                to = (x, y, c) if incoming else (*chip, c)
                for half in range(2):
                    p = _shard_slice(dst[wi], kinds[wi], dims[wi][0], dims[wi][1], s, half)
                    k = wi * 6 + 2 * j + half
                    out.append(pltpu.make_async_remote_copy(src_ref=p, dst_ref=p, send_sem=send_sems.at[k],
                                                            recv_sem=recv_sems.at[k], device_id=to, device_id_type=MESH))
        return out

    def start(src, dst, sems):
        for cp in copies(dst, sems[0], sems[1], False):
            cp.start()

    def finish(src, dst, sems):
        for cp in copies(dst, sems[0], sems[1], True):
            cp.wait_recv()
        for cp in copies(dst, sems[0], sems[1], False):
            cp.wait_send()

    return dict(args=list(placed), out_shape=[jax.ShapeDtypeStruct(p.shape, p.dtype) for p in placed],
                scratch=[pltpu.SemaphoreType.DMA((nw * 6,)), pltpu.SemaphoreType.DMA((nw * 6,))],
                start=start, finish=finish, in_place=True)


def _scatter_guest(partials, kinds, dims):
    nw = len(partials)

    def copies(src, send_sems, recv_sems, dst):
        x, y, c, chips = _place()
        me = 2 * x + y
        out = []
        for wi in range(nw):
            rows, cols = dims[wi]

            def part(s, half, wi=wi, rows=rows, cols=cols):
                return _shard_slice(src[wi], kinds[wi], rows, cols, s, half)

            for j, chip in enumerate(chips):
                s = 2 * chip[0] + chip[1]
                for half in range(2):
                    slot = 2 * j + (c if half == 0 else 1 - c)
                    out.append(pltpu.make_async_remote_copy(
                        src_ref=part(s, half), dst_ref=dst[wi].at[slot],
                        send_sem=send_sems.at[wi * 7 + 2 * j + half], recv_sem=recv_sems.at[wi * 7 + slot],
                        device_id=(*chip, half), device_id_type=MESH))
            out.append(pltpu.make_async_remote_copy(
                src_ref=part(me, 1 - c), dst_ref=dst[wi].at[6],
                send_sem=send_sems.at[wi * 7 + 6], recv_sem=recv_sems.at[wi * 7 + 6],
                device_id=(x, y, 1 - c), device_id_type=MESH))
        return out

    def start(src, dst, sems):
        for cp in copies(src, sems[0], sems[1], dst):
            cp.start()

    def finish(src, dst, sems):
        x, y, c, _ = _place()
        for wi in range(nw):
            for slot in range(7):
                pltpu.make_async_remote_copy(
                    src_ref=dst[wi].at[slot], dst_ref=dst[wi].at[slot],
                    send_sem=sems[0].at[wi * 7 + slot], recv_sem=sems[1].at[wi * 7 + slot],
                    device_id=(x, y, c), device_id_type=MESH).wait_recv()
        for cp in copies(src, sems[0], sems[1], dst):
            cp.wait_send()

    return dict(args=list(partials), out_shape=[jax.ShapeDtypeStruct((7, d[0] // 2, d[1]), BF) for d in dims],
                scratch=[pltpu.SemaphoreType.DMA((nw * 7,)), pltpu.SemaphoreType.DMA((nw * 7,))],
                start=start, finish=finish)


def _sum_slots(slots, partial, kind, dims, place, *, name, into=None, layer=None, n_layers=1):
    rows, cols = dims
    hr = rows // 2
    tr = hr if 8 * hr * cols * 2 <= 6 * 1024 * 1024 else 128
    assert hr % tr == 0

    def body(p_ref, b_ref, own_ref, *rest):
        o_ref = rest[-1]
        acc = own_ref[...].astype(F32)
        for k in range(7):
            acc = acc + b_ref[k].astype(F32)
        o_ref[...] = acc

    half = lambda p: p[1] * (hr // tr)
    if n_layers == 1:
        out_spec = pl.BlockSpec((tr, cols), lambda i, p: (half(p) + i, 0))
        out_shape = jax.ShapeDtypeStruct((rows, cols), F32)
    else:
        out_spec = pl.BlockSpec((None, tr, cols), lambda i, p: (layer, half(p) + i, 0))
        out_shape = jax.ShapeDtypeStruct((n_layers, rows, cols), F32)
    in_specs = [pl.BlockSpec((7, tr, cols), lambda i, p: (0, i, 0)), _own_block(kind, rows, tr, cols)(hr)]
    args = [place, slots, partial]
    aliases = {}
    if into is not None:
        in_specs.append(ANY)
        args.append(into)
        aliases = {3: 0}
    return pl.pallas_call(
        body,
        grid_spec=pltpu.PrefetchScalarGridSpec(num_scalar_prefetch=1, grid=(hr // tr,), in_specs=in_specs, out_specs=out_spec),
        out_shape=out_shape, input_output_aliases=aliases, name=name, compiler_params=_params("arbitrary"),
    )(*args)


def _pair_exchange(bufs, members):
    nw = len(members)

    def body(*refs):
        dst = refs[len(bufs):2 * len(bufs)]
        send_sems, recv_sems = refs[2 * len(bufs):]
        x, y, c, _ = _place()

        def rows_of(wi, half):
            bi, l = members[wi]
            ref = dst[bi] if l is None else dst[bi].at[l]
            hr = ref.shape[0] // 2
            return ref.at[pl.ds(pl.multiple_of(half * hr, 8), hr), :]

        def copy(wi, half, to):
            p = rows_of(wi, half)
            return pltpu.make_async_remote_copy(src_ref=p, dst_ref=p, send_sem=send_sems.at[wi], recv_sem=recv_sems.at[wi],
                                                device_id=to, device_id_type=MESH)

        sent = []
        for wi in range(nw):
            cp = copy(wi, c, (x, y, 1 - c))
            cp.start()
            sent.append(cp)
        for wi in range(nw):
            copy(wi, 1 - c, (x, y, c)).wait_recv()
        for cp in sent:
            cp.wait_send()

    return pl.pallas_call(
        body, in_specs=[ANY] * len(bufs), out_specs=[ANY] * len(bufs),
        out_shape=[jax.ShapeDtypeStruct(b.shape, b.dtype) for b in bufs],
        input_output_aliases={i: i for i in range(len(bufs))},
        scratch_shapes=[pltpu.SemaphoreType.DMA((nw,)), pltpu.SemaphoreType.DMA((nw,))],
        name="pair_exchange",
    )(*bufs)


SMALL_ROWS = 8


def _allreduce_small(v, *, name):
    assert v.shape == (SMALL_ROWS, D_MODEL)

    def body(v_ref, o_ref, buf, send_sems, recv_sems):
        x, y, c, _ = _place()
        me = 4 * x + 2 * y + c
        buf[me] = v_ref[...]
        sent = []
        for k in range(1, 8):
            bx, by, bc = (k >> 2) & 1, (k >> 1) & 1, k & 1
            peer = (1 - x if bx else x, 1 - y if by else y, 1 - c if bc else c)
            cp = pltpu.make_async_remote_copy(src_ref=v_ref, dst_ref=buf.at[me], send_sem=send_sems.at[k - 1],
                                              recv_sem=recv_sems.at[k - 1], device_id=peer, device_id_type=MESH)
            cp.start()
            sent.append(cp)
        for k in range(1, 8):
            bx, by, bc = (k >> 2) & 1, (k >> 1) & 1, k & 1
            peer = 4 * (1 - x if bx else x) + 2 * (1 - y if by else y) + (1 - c if bc else c)
            pltpu.make_async_remote_copy(src_ref=v_ref, dst_ref=buf.at[peer], send_sem=send_sems.at[k - 1],
                                         recv_sem=recv_sems.at[k - 1], device_id=(x, y, c), device_id_type=MESH).wait_recv()
        for cp in sent:
            cp.wait_send()
        acc = buf[0]
        for d in range(1, 8):
            acc = acc + buf[d]
        o_ref[...] = acc

    vmem = pl.BlockSpec(memory_space=pltpu.VMEM)
    return pl.pallas_call(
        body, in_specs=[vmem], out_specs=vmem, out_shape=jax.ShapeDtypeStruct(v.shape, F32),
        scratch_shapes=[pltpu.VMEM((8,) + v.shape, F32), pltpu.SemaphoreType.DMA((7,)), pltpu.SemaphoreType.DMA((7,))],
        name=name,
    )(v)


def _adamw(w, g, m, v, *, name):
    R, C = w.shape
    tr = R
    if R * C * 4 > 1024 * 1024:
        tr = max(t for t in range(8, R, 8) if R % t == 0 and t * C * 4 <= 1024 * 1024)

    def body(w_ref, g_ref, m_ref, v_ref, d_ref, m2_ref, v2_ref):
        gg = g_ref[...]
        m2 = ADAM_B1 * m_ref[...] + (1.0 - ADAM_B1) * gg
        v2 = ADAM_B2 * v_ref[...] + (1.0 - ADAM_B2) * jnp.square(gg)
        m_hat = m2 / (1.0 - ADAM_B1 ** ADAM_STEP)
        v_hat = v2 / (1.0 - ADAM_B2 ** ADAM_STEP)
        d_ref[...] = -ADAM_LR * (m_hat / (jnp.sqrt(v_hat) + ADAM_EPS) + ADAM_WD * w_ref[...])
        m2_ref[...] = m2
        v2_ref[...] = v2

    blk = pl.BlockSpec((tr, C), lambda i: (i, 0))
    out = jax.ShapeDtypeStruct((R, C), F32)
    return pl.pallas_call(
        body, grid=(R // tr,), in_specs=[blk] * 4, out_specs=[blk] * 3, out_shape=[out] * 3,
        name=name, compiler_params=_params("parallel"),
    )(w, g, m, v)


WEIGHT_ORDER = ("a_norm", "a_w_in", "a_w_out", "b_norm", "b_w_in", "b_f", "b_w_out", "ffn_norm", "ffn_w_gu",
                "ffn_w_down", "final_norm")
MATRICES = (("a_w_in", 0, "col"), ("a_w_out", 0, "row"), ("b_w_in", 0, "stack"), ("b_w_out", 0, "row"),
            ("ffn_w_gu", 0, "col"), ("ffn_w_gu", 1, "col"), ("ffn_w_down", 0, "row"), ("ffn_w_down", 1, "row"))
MATRIX_GROUPS = ([0], [1], [2], [3], [4, 5], [6, 7])
GROUP_NAMES = ("a_w_in", "a_w_out", "b_w_in", "b_w_out", "ffn_w_gu", "ffn_w_down")
QKV_COLS = 3 * N_HEADS * HEAD_DIM


def kernel(x, a_norm, a_w_in, a_w_out, b_norm, b_w_in, b_f, b_w_out, ffn_norm, ffn_w_gu, ffn_w_down, final_norm, loss_target, m_a_norm, m_a_w_in, m_a_w_out, m_b_norm, m_b_w_in, m_b_f, m_b_w_out, m_ffn_norm, m_ffn_w_gu, m_ffn_w_down, m_final_norm, v_a_norm, v_a_w_in, v_a_w_out, v_b_norm, v_b_w_in, v_b_f, v_b_w_out, v_ffn_norm, v_ffn_w_gu, v_ffn_w_down, v_final_norm):
    given = dict(a_norm=a_norm, a_w_in=a_w_in, a_w_out=a_w_out, b_norm=b_norm, b_w_in=b_w_in, b_f=b_f, b_w_out=b_w_out,
                 ffn_norm=ffn_norm, ffn_w_gu=ffn_w_gu, ffn_w_down=ffn_w_down, final_norm=final_norm)
    mom_m = dict(a_norm=m_a_norm, a_w_in=m_a_w_in, a_w_out=m_a_w_out, b_norm=m_b_norm, b_w_in=m_b_w_in, b_f=m_b_f,
                 b_w_out=m_b_w_out, ffn_norm=m_ffn_norm, ffn_w_gu=m_ffn_w_gu, ffn_w_down=m_ffn_w_down, final_norm=m_final_norm)
    mom_v = dict(a_norm=v_a_norm, a_w_in=v_a_w_in, a_w_out=v_a_w_out, b_norm=v_b_norm, b_w_in=v_b_w_in, b_f=v_b_f,
                 b_w_out=v_b_w_out, ffn_norm=v_ffn_norm, ffn_w_gu=v_ffn_w_gu, ffn_w_down=v_ffn_w_down, final_norm=v_final_norm)
    chip = 2 * lax.axis_index("x") + lax.axis_index("y")
    core = lax.axis_index("c")
    bn_cols = b_norm.shape[1]

    placed = lax.dynamic_update_slice(jnp.zeros((SMALL_ROWS, D_MODEL), F32), b_norm, (0, chip * bn_cols))
    placed = placed * (core == 0).astype(F32)
    b_norm_full = _allreduce_small(placed, name="gather_b_norm")[0:1]

    place = jnp.stack([chip, core]).astype(jnp.int32)
    kinds = [k for _, _, k in MATRICES]
    dims = [given[n].shape[1:] for n, _, _ in MATRICES]
    placed = [_place_shard(given[n], l, k, place, name=f"place_{n}{l}") for n, l, k in MATRICES]
    first = _gather_weights(placed[:1], kinds[:1], dims[:1])
    mats = dict(enumerate(list(first) + placed[1:]))
    gate_cols = b_f.shape[1]
    w = dict(a_norm=a_norm, a_w_in=mats[0], b_norm=b_norm_full,
             b_f=jnp.pad(b_f, ((0, 0), (0, GATE_LANES - gate_cols))), ffn_norm=ffn_norm,
             final_norm=final_norm.reshape(1, D_MODEL))

    def fetch(indices, bufs):
        return _fetch_guest(bufs, [kinds[i] for i in indices], [dims[i] for i in indices])

    def exchange(indices, parts):
        return _scatter_guest(parts, [kinds[i] for i in indices], [dims[i] for i in indices])

    loss, dx, g, partials, slots = _local_step(x[0], loss_target[0], w, mats, fetch, exchange)
    bufs, members = [], []
    for group in MATRIX_GROUPS:
        buf = None
        for l, wi in enumerate(group):
            n = MATRICES[wi][0]
            buf = _sum_slots(slots[wi], partials[wi], kinds[wi], dims[wi], place, name=f"sum_{n}{l}", into=buf,
                             layer=l, n_layers=len(group))
            members.append((len(bufs), l if len(group) > 1 else None))
        bufs.append(buf)
    reduced = dict(zip(GROUP_NAMES, _pair_exchange(bufs, members)))

    small = jnp.concatenate([g["a_norm"], g["b_norm"], g["ffn_norm"], g["final_norm"],
                             jnp.pad(g["b_f"], ((0, 0), (0, D_MODEL - GATE_LANES))),
                             jnp.zeros((SMALL_ROWS - 6, D_MODEL), F32)], axis=0)
    small = _allreduce_small(small, name="allreduce_small")
    grads = dict(reduced)
    grads["a_norm"] = small[0:1]
    grads["b_norm"] = lax.dynamic_slice(small, (1, chip * bn_cols), (1, bn_cols))
    grads["ffn_norm"] = small[2:4]
    grads["final_norm"] = small[4]
    grads["b_f"] = small[5:6, :gate_cols]

    out_g, out_d, out_m, out_v = [], [], [], []
    for n in WEIGHT_ORDER:
        shape = given[n].shape
        two_d = (1, shape[0]) if len(shape) == 1 else (-1, shape[-1])
        d, m2, v2 = _adamw(given[n].reshape(two_d), grads[n].reshape(two_d), mom_m[n].reshape(two_d),
                           mom_v[n].reshape(two_d), name=f"adamw_{n}")
        out_g.append(grads[n].reshape(shape))
        out_d.append(d.reshape(shape))
        out_m.append(m2.reshape(shape))
        out_v.append(v2.reshape(shape))

    total = lax.psum(loss[0, 0], MESH_AXES)
    return (total, dx[None], *out_g, *out_d, *out_m, *out_v)
```

```python
import functools

import jax
import jax.numpy as jnp
from jax import lax
from jax.experimental import pallas as pl
from jax.experimental.pallas import tpu as pltpu

F32 = jnp.float32
BF = jnp.bfloat16

D_MODEL = 1024
N_HEADS = 16
HEAD_DIM = 64
D_FF = 2816
DILATED_PATTERNS = ((128, 1), (512, 4), (2048, 16))
ROT_DIM = 16
ROPE_THETA = 500000.0
RMS_EPS = 1e-6
NEG_INF = -1e30
ATTN_SCALE = HEAD_DIM ** -0.5
GATE_LANES = 128
N_CHIPS = 4
MESH_AXES = ("x", "y", "c")
MESH = pl.DeviceIdType.MESH

ADAM_LR = 0.001
ADAM_B1 = 0.9
ADAM_B2 = 0.999
ADAM_EPS = 1e-08
ADAM_WD = 0.01
ADAM_STEP = 10

VMEM_LIMIT_BYTES = 56 * 1024 * 1024


def _params(*sem):
    return pltpu.CompilerParams(dimension_semantics=sem, vmem_limit_bytes=VMEM_LIMIT_BYTES)


def _hosted_call(body, *, grid, in_specs, out_specs, out_shape, scratch_shapes, args, name, guest=None, schedule=()):
    params = _params(*(["arbitrary"] * len(grid)))
    ns = len(schedule)

    def call(kernel, in_specs, out_specs, out_shape, scratch_shapes, aliases, args):
        spec = pltpu.PrefetchScalarGridSpec(num_scalar_prefetch=ns, grid=grid, in_specs=in_specs, out_specs=out_specs,
                                            scratch_shapes=scratch_shapes)
        return pl.pallas_call(kernel, grid_spec=spec, out_shape=out_shape, input_output_aliases=aliases, name=name,
                              compiler_params=params)(*schedule, *args)

    if guest is None:
        return call(body, in_specs, out_specs, out_shape, scratch_shapes, {}, args)
    n_in, n_out, n_scr = ns + len(in_specs), len(out_specs), len(scratch_shapes)
    g_in, g_out = len(guest["args"]), len(guest["out_shape"])
    any_spec = pl.BlockSpec(memory_space=pl.ANY)

    def wrapped(*refs):
        i1 = n_in + g_in
        o1 = i1 + n_out
        o2 = o1 + g_out
        s1 = o2 + n_scr
        guest_refs = (refs[n_in:i1], refs[o1:o2], refs[s1:])
        ids = [pl.program_id(d) for d in range(len(grid))]
        first = functools.reduce(jnp.logical_and, [i == 0 for i in ids])
        last = functools.reduce(jnp.logical_and, [i == g - 1 for i, g in zip(ids, grid)])

        @pl.when(first)
        def _():
            guest["start"](*guest_refs)

        body(*refs[:n_in], *refs[i1:o1], *refs[o2:s1])

        @pl.when(last)
        def _():
            guest["finish"](*guest_refs)

    aliases = {n_in + k: n_out + k for k in range(g_in)} if guest.get("in_place") else {}
    return call(wrapped, list(in_specs) + [any_spec] * g_in, list(out_specs) + [any_spec] * g_out,
                list(out_shape) + list(guest["out_shape"]), list(scratch_shapes) + list(guest["scratch"]), aliases,
                list(args) + list(guest["args"]))


def _rope_rotate(t, cos, sin_a, sin_b):
    outs = []
    for cidx in range(t.shape[1] // 128):
        tc = t[:, cidx * 128:(cidx + 1) * 128]
        outs.append(tc * cos + pltpu.roll(tc, 120, 1) * sin_a + pltpu.roll(tc, 8, 1) * sin_b)
    return jnp.concatenate(outs, axis=1)


def _mm_nn(a, b, *, tm, tn, out_dtype, name, resid=None, rope=None, guest=None, groups=None):
    M, K = a.shape
    N = b.shape[1]
    assert M % tm == 0 and N % tn == 0 and b.shape[0] == K
    n_in = 2 + (resid is not None) + (3 if rope is not None else 0)
    if groups is not None:
        assert rope is not None and N == 3 * tn * len(groups)

    def body(*refs):
        a_ref, b_ref = refs[0], refs[1]
        o_ref = refs[n_in]
        acc = jnp.dot(a_ref[...], b_ref[...], preferred_element_type=F32)
        if resid is not None:
            acc = acc + refs[2][...]
        if groups is not None:
            cos_ref, sa_ref, sb_ref = refs[n_in - 3:n_in]
            j = pl.program_id(1)
            for g, d in enumerate(groups):
                for is_v in (False, True):
                    @pl.when(jnp.logical_and(j // 3 == g, (j % 3 == 2) == is_v))
                    def _(g=g, d=d, is_v=is_v):
                        val = acc if is_v else _rope_rotate(acc, cos_ref[...], sa_ref[...], sb_ref[...])
                        if d == 1:
                            refs[n_in + g][...] = val.astype(out_dtype)
                        else:
                            _to_view(val, refs[-1], refs[n_in + g], d, tn)
        elif rope is not None:
            cos_ref, sa_ref, sb_ref = refs[n_in - 3:n_in]
            j = pl.program_id(1)

            @pl.when(j % 3 != 2)
            def _():
                o_ref[...] = _rope_rotate(acc, cos_ref[...], sa_ref[...], sb_ref[...]).astype(out_dtype)

            @pl.when(j % 3 == 2)
            def _():
                o_ref[...] = acc.astype(out_dtype)
        else:
            o_ref[...] = acc.astype(out_dtype)

    in_specs = [pl.BlockSpec((tm, K), lambda i, j: (i, 0)), pl.BlockSpec((K, tn), lambda i, j: (0, j))]
    args = [a, b]
    if resid is not None:
        in_specs.append(pl.BlockSpec((tm, tn), lambda i, j: (i, j)))
        args.append(resid)
    if rope is not None:
        assert tn == 1024
        for t in rope:
            in_specs.append(pl.BlockSpec((tm, 128), lambda i, j: (i, 0)))
            args.append(t)
    if groups is None:
        out_specs = [pl.BlockSpec((tm, tn), lambda i, j: (i, j))]
        out_shape = [jax.ShapeDtypeStruct((M, N), out_dtype)]
        scratch = []
    else:
        out_specs = [pl.BlockSpec((tm // d, d * tn), lambda i, j, g=g: (i, jnp.clip(j - 3 * g, 0, 2)))
                     for g, d in enumerate(groups)]
        out_shape = [jax.ShapeDtypeStruct((M // d, d * 3 * tn), out_dtype) for d in groups]
        scratch = [pltpu.VMEM((tn // 128, tm, 128), F32)]
    outs = _hosted_call(body, grid=(M // tm, N // tn), in_specs=in_specs, out_specs=out_specs, out_shape=out_shape,
                        scratch_shapes=scratch, args=args, name=name, guest=guest)
    nout = len(out_shape)
    res = outs[0] if groups is None else list(outs[:nout])
    return res if guest is None else (res, outs[nout:])


def _mm_nt(a, b, *, tm, to, tn, out_dtype, name, add=None, guest=None, views=(1,)):
    M, N = a.shape
    O = b.shape[0]
    assert M % tm == 0 and O % to == 0 and N % tn == 0 and b.shape[1] == N
    nk = N // tn

    def body(*refs):
        a_ref, b_ref = refs[0], refs[1]
        n_in = 2 + (add is not None)
        o_refs = refs[n_in:n_in + len(views)]
        acc_ref = refs[n_in + len(views)]
        k = pl.program_id(2)

        @pl.when(k == 0)
        def _():
            if add is not None:
                acc_ref[...] = refs[2][...]
            else:
                acc_ref[...] = jnp.zeros_like(acc_ref)

        acc_ref[...] += lax.dot_general(a_ref[...], b_ref[...], (((1,), (1,)), ((), ())),
                                        preferred_element_type=F32)

        @pl.when(k == nk - 1)
        def _():
            for o_ref, d in zip(o_refs, views):
                if d == 1:
                    o_ref[...] = acc_ref[...].astype(out_dtype)
                else:
                    _to_view(acc_ref[...], refs[-1], o_ref, d, to)

    in_specs = [pl.BlockSpec((tm, tn), lambda i, j, k: (i, k)), pl.BlockSpec((to, tn), lambda i, j, k: (j, k))]
    args = [a, b]
    if add is not None:
        in_specs.append(pl.BlockSpec((tm, to), lambda i, j, k: (i, j)))
        args.append(add)
    assert views == (1,) or (to == O and to % 128 == 0)
    scratch = [pltpu.VMEM((tm, to), F32)] + ([pltpu.VMEM((to // 128, tm, 128), F32)] if views != (1,) else [])
    outs = _hosted_call(
        body, grid=(M // tm, O // to, nk), in_specs=in_specs,
        out_specs=[pl.BlockSpec((tm, to), lambda i, j, k: (i, j)) if d == 1 else _view_spec(tm, d, to) for d in views],
        out_shape=[jax.ShapeDtypeStruct((M // d, d * O), out_dtype) for d in views],
        scratch_shapes=scratch, args=args, name=name, guest=guest)
    nv = len(views)
    res = outs[0] if nv == 1 else list(outs[:nv])
    return res if guest is None else (res, outs[nv:])


def _mm_tn(a, b, *, tk, tn, tm, out_dtype, name):
    M, K = a.shape
    N = b.shape[1]
    assert M % tm == 0 and K % tk == 0 and N % tn == 0 and b.shape[0] == M
    nm = M // tm

    def body(a_ref, b_ref, o_ref, acc_ref):
        m = pl.program_id(2)

        @pl.when(m == 0)
        def _():
            acc_ref[...] = jnp.zeros_like(acc_ref)

        acc_ref[...] += lax.dot_general(a_ref[...], b_ref[...], (((0,), (0,)), ((), ())),
                                        preferred_element_type=F32)

        @pl.when(m == nm - 1)
        def _():
            o_ref[...] = acc_ref[...].astype(out_dtype)

    return pl.pallas_call(
        body, grid=(K // tk, N // tn, nm),
        in_specs=[pl.BlockSpec((tm, tk), lambda i, j, m: (m, i)), pl.BlockSpec((tm, tn), lambda i, j, m: (m, j))],
        out_specs=pl.BlockSpec((tk, tn), lambda i, j, m: (i, j)),
        out_shape=jax.ShapeDtypeStruct((K, N), out_dtype),
        scratch_shapes=[pltpu.VMEM((tk, tn), F32)], name=name,
        compiler_params=_params("parallel", "parallel", "arbitrary"),
    )(a, b)


ROW_TILE = 512
MM_ROWS = 1024


def _rms_fwd(x, g, *, name):
    S, Dm = x.shape

    def body(x_ref, g_ref, o_ref):
        xf = x_ref[...]
        r = lax.rsqrt(jnp.mean(xf * xf, axis=-1, keepdims=True) + RMS_EPS)
        o_ref[...] = (xf * r * g_ref[...]).astype(BF)

    return pl.pallas_call(
        body, grid=(S // ROW_TILE,),
        in_specs=[pl.BlockSpec((ROW_TILE, Dm), lambda i: (i, 0)), pl.BlockSpec((1, Dm), lambda i: (0, 0))],
        out_specs=pl.BlockSpec((ROW_TILE, Dm), lambda i: (i, 0)),
        out_shape=jax.ShapeDtypeStruct((S, Dm), BF), name=name, compiler_params=_params("parallel"),
    )(x, g)


def _rms_bwd(x, g, dn, dres, *, name):
    S, Dm = x.shape

    def body(x_ref, g_ref, dn_ref, dres_ref, dx_ref, dxb_ref, dg_ref):
        i = pl.program_id(0)
        xf = x_ref[...]
        r = lax.rsqrt(jnp.mean(xf * xf, axis=-1, keepdims=True) + RMS_EPS)
        xh = xf * r
        dnf = dn_ref[...]
        dyg = dnf * g_ref[...]
        dx = dres_ref[...] + r * (dyg - xh * jnp.mean(dyg * xh, axis=-1, keepdims=True))
        dx_ref[...] = dx
        dxb_ref[...] = dx.astype(BF)

        @pl.when(i == 0)
        def _():
            dg_ref[...] = jnp.zeros_like(dg_ref)

        dg_ref[...] += jnp.sum(dnf * xh, axis=0, keepdims=True)

    row = pl.BlockSpec((ROW_TILE, Dm), lambda i: (i, 0))
    vec = pl.BlockSpec((1, Dm), lambda i: (0, 0))
    return pl.pallas_call(
        body, grid=(S // ROW_TILE,), in_specs=[row, vec, row, row], out_specs=[row, row, vec],
        out_shape=[jax.ShapeDtypeStruct((S, Dm), F32), jax.ShapeDtypeStruct((S, Dm), BF),
                   jax.ShapeDtypeStruct((1, Dm), F32)],
        name=name, compiler_params=_params("arbitrary"),
    )(x, g, dn, dres)


def _loss_head(h, g, tgt, *, name):
    S, Dm = h.shape

    def body(h_ref, g_ref, t_ref, loss_ref, dh_ref, dhb_ref, dg_ref):
        i = pl.program_id(0)
        xf = h_ref[...]
        r = lax.rsqrt(jnp.mean(xf * xf, axis=-1, keepdims=True) + RMS_EPS)
        xh = xf * r
        gv = g_ref[...]
        err = xh * gv - t_ref[...]
        dy = err * (1.0 / Dm)
        dyg = dy * gv
        dh = r * (dyg - xh * jnp.mean(dyg * xh, axis=-1, keepdims=True))
        dh_ref[...] = dh
        dhb_ref[...] = dh.astype(BF)

        @pl.when(i == 0)
        def _():
            dg_ref[...] = jnp.zeros_like(dg_ref)
            loss_ref[...] = jnp.zeros_like(loss_ref)

        dg_ref[...] += jnp.sum(dy * xh, axis=0, keepdims=True)
        part = 0.5 * jnp.sum(jnp.mean(err * err, axis=-1, keepdims=True), axis=0, keepdims=True)
        loss_ref[...] += jnp.broadcast_to(part, loss_ref.shape)

    row = pl.BlockSpec((ROW_TILE, Dm), lambda i: (i, 0))
    vec = pl.BlockSpec((1, Dm), lambda i: (0, 0))
    return pl.pallas_call(
        body, grid=(S // ROW_TILE,), in_specs=[row, vec, row],
        out_specs=[pl.BlockSpec((1, 128), lambda i: (0, 0)), row, row, vec],
        out_shape=[jax.ShapeDtypeStruct((1, 128), F32), jax.ShapeDtypeStruct((S, Dm), F32),
                   jax.ShapeDtypeStruct((S, Dm), BF), jax.ShapeDtypeStruct((1, Dm), F32)],
        name=name, compiler_params=_params("arbitrary"),
    )(h, g, tgt)


SWIGLU_ROWS = 512


def _swiglu_fwd(gu, *, name):
    S = gu.shape[0]

    def body(g_ref, u_ref, o_ref):
        g = g_ref[...].astype(F32)
        sig = 1.0 / (1.0 + jnp.exp(-g))
        o_ref[...] = (g * sig * u_ref[...].astype(F32)).astype(BF)

    return pl.pallas_call(
        body, grid=(S // SWIGLU_ROWS,),
        in_specs=[pl.BlockSpec((SWIGLU_ROWS, D_FF), lambda i: (i, 0)), pl.BlockSpec((SWIGLU_ROWS, D_FF), lambda i: (i, 1))],
        out_specs=pl.BlockSpec((SWIGLU_ROWS, D_FF), lambda i: (i, 0)),
        out_shape=jax.ShapeDtypeStruct((S, D_FF), BF), name=name, compiler_params=_params("parallel"),
    )(gu, gu)


def _swiglu_bwd(gu, dact, *, name):
    S = gu.shape[0]

    def body(g_ref, u_ref, d_ref, o_ref):
        g = g_ref[...].astype(F32)
        u = u_ref[...].astype(F32)
        d = d_ref[...].astype(F32)
        sig = 1.0 / (1.0 + jnp.exp(-g))
        o_ref[:, :D_FF] = (d * u * sig * (1.0 + g * (1.0 - sig))).astype(BF)
        o_ref[:, D_FF:] = (d * g * sig).astype(BF)

    return pl.pallas_call(
        body, grid=(S // SWIGLU_ROWS,),
        in_specs=[pl.BlockSpec((SWIGLU_ROWS, D_FF), lambda i: (i, 0)), pl.BlockSpec((SWIGLU_ROWS, D_FF), lambda i: (i, 1)),
                  pl.BlockSpec((SWIGLU_ROWS, D_FF), lambda i: (i, 0))],
        out_specs=pl.BlockSpec((SWIGLU_ROWS, 2 * D_FF), lambda i: (i, 0)),
        out_shape=jax.ShapeDtypeStruct((S, 2 * D_FF), BF), name=name, compiler_params=_params("parallel"),
    )(gu, gu, dact)


def _band_masks(T, n):
    row = lax.broadcasted_iota(jnp.int32, (T, T), 0)
    col = lax.broadcasted_iota(jnp.int32, (T, T), 1)
    return jnp.logical_and(col >= row, n > 0), col <= row


def _band_fwd(qa, ka, va, qcb, kcb, vcb, *, dil, T, window, name, guest=None):
    L = qa.shape[0]
    nq = L // T
    assert window == T
    nt = (((1,), (1,)), ((), ()))

    def body(q_ref, kp_ref, kc_ref, vp_ref, vc_ref, o_ref, lse_ref):
        valid_prev, valid_cur = _band_masks(T, pl.program_id(1))
        lane = lax.broadcasted_iota(jnp.int32, (T, 128), 1)
        low = lane < HEAD_DIM
        ones = jnp.ones((T, 128), BF)
        lse = jnp.zeros((T, 128), F32)
        def scores(h):
            ps = slice((h // 2) * 128, (h // 2 + 1) * 128)
            qp = q_ref[:, ps] * jnp.asarray(ATTN_SCALE, BF)
            qm = jnp.where(low if h % 2 == 0 else jnp.logical_not(low), qp, jnp.zeros_like(qp))
            s0 = jnp.where(valid_prev, lax.dot_general(qm, kp_ref[:, ps], nt, preferred_element_type=F32), NEG_INF)
            s1 = jnp.where(valid_cur, lax.dot_general(qm, kc_ref[:, ps], nt, preferred_element_type=F32), NEG_INF)
            return s0, s1

        def softmax(s0, s1):
            m = jnp.maximum(jnp.max(s0, axis=1, keepdims=True), jnp.max(s1, axis=1, keepdims=True))
            return m, jnp.exp(s0 - m).astype(BF), jnp.exp(s1 - m).astype(BF)

        def weighted(h, p0, p1):
            ps = slice((h // 2) * 128, (h // 2 + 1) * 128)
            pv = (jnp.dot(p0, jnp.concatenate([vp_ref[:, ps], ones], axis=1), preferred_element_type=F32)
                  + jnp.dot(p1, jnp.concatenate([vc_ref[:, ps], ones], axis=1), preferred_element_type=F32))
            return pv[:, 128:], pv[:, :128]

        sc, pr, even = {}, {}, None
        for t in range(N_HEADS + 2):
            if t < N_HEADS:
                sc[t] = scores(t)
            done = None
            if t >= 2:
                m, p0, p1 = pr.pop(t - 2)
                done = (m,) + weighted(t - 2, p0, p1)
            if 1 <= t <= N_HEADS:
                pr[t - 1] = softmax(*sc.pop(t - 1))
            if done is not None:
                h = t - 2
                m, l, acc = done
                lse = jnp.where(lane == h, m + jnp.log(l), lse)
                if h % 2 == 0:
                    even = acc / l
                else:
                    o_ref[:, (h // 2) * 128:(h // 2 + 1) * 128] = jnp.where(low, even, acc / l)
        lse_ref[...] = lse

    def prev(n):
        return jnp.maximum(n - 1, 0)

    blk = lambda f, cb: pl.BlockSpec((T, 1024), lambda r, n: (f(n), cb(r)))
    same = lambda n: n
    outs = _hosted_call(
        body, grid=(dil, nq),
        in_specs=[blk(same, qcb), blk(prev, kcb), blk(same, kcb), blk(prev, vcb), blk(same, vcb)],
        out_specs=[pl.BlockSpec((T, 1024), lambda r, n: (n, r)), pl.BlockSpec((T, 128), lambda r, n: (n, r))],
        out_shape=[jax.ShapeDtypeStruct((L, dil * 1024), F32), jax.ShapeDtypeStruct((L, dil * 128), F32)],
        scratch_shapes=[], args=(qa, ka, ka, va, va), name=name, guest=guest)
    return outs if guest is None else (outs[:2], outs[2:])


def _band_bwd(qa, ka, va, qcb, kcb, vcb, doa, oa, lsea, *, dil, T, window, name, guest=None):
    L = qa.shape[0]
    nq = L // T
    assert window == T
    nt = (((1,), (1,)), ((), ()))
    tn = (((0,), (0,)), ((), ()))

    def body(q_ref, kp_ref, kc_ref, vp_ref, vc_ref, do_ref, o_ref, lse_ref, dq_ref, dk_ref, dv_ref, ck_sc, cv_sc):
        n = pl.program_id(1)

        @pl.when(n == 0)
        def _():
            ck_sc[...] = jnp.zeros_like(ck_sc)
            cv_sc[...] = jnp.zeros_like(cv_sc)

        @pl.when(n < nq)
        def _():
            valid_prev, valid_cur = _band_masks(T, n)
            low = lax.broadcasted_iota(jnp.int32, (T, 128), 1) < HEAD_DIM
            dot = functools.partial(lax.dot_general, preferred_element_type=F32)

            def pair(h):
                return slice((h // 2) * 128, (h // 2 + 1) * 128)

            def products(h):
                ps = pair(h)
                mask = low if h % 2 == 0 else jnp.logical_not(low)
                qp = q_ref[:, ps] * jnp.asarray(ATTN_SCALE, BF)
                dop = do_ref[:, ps]
                qm = jnp.where(mask, qp, jnp.zeros_like(qp))
                dom = jnp.where(mask, dop, jnp.zeros_like(dop))
                s0 = jnp.where(valid_prev, dot(qm, kp_ref[:, ps], nt), NEG_INF)
                s1 = jnp.where(valid_cur, dot(qm, kc_ref[:, ps], nt), NEG_INF)
                return qm, dom, s0, s1, dot(dom, vp_ref[:, ps], nt), dot(dom, vc_ref[:, ps], nt)

            def pointwise(h, qm, dom, s0, s1, dp0, dp1):
                ps = pair(h)
                mask = low if h % 2 == 0 else jnp.logical_not(low)
                prod = do_ref[:, ps].astype(F32) * o_ref[:, ps].astype(F32)
                delta = jnp.sum(jnp.where(mask, prod, 0.0), axis=1, keepdims=True)
                lse = lse_ref[:, h:h + 1]
                p0 = jnp.exp(s0 - lse)
                p1 = jnp.exp(s1 - lse)
                ds0 = (p0 * (dp0 - delta)).astype(BF)
                ds1 = (p1 * (dp1 - delta)).astype(BF)
                return qm, dom, p0.astype(BF), p1.astype(BF), ds0, ds1

            def gradients(h, qm, dom, p0, p1, ds0, ds1):
                ps = pair(h)
                dq = dot(ds0, kp_ref[:, ps], (((1,), (0,)), ((), ()))) + dot(ds1, kc_ref[:, ps], (((1,), (0,)), ((), ())))
                return dq, dot(ds0, qm, tn), dot(p0, dom, tn), dot(ds1, qm, tn), dot(p1, dom, tn)

            st1, st2, even = {}, {}, None
            for t in range(N_HEADS + 2):
                if t < N_HEADS:
                    st1[t] = products(t)
                done = gradients(t - 2, *st2.pop(t - 2)) if t >= 2 else None
                if 1 <= t <= N_HEADS:
                    st2[t - 1] = pointwise(t - 1, *st1.pop(t - 1))
                if done is not None:
                    h = t - 2
                    if h % 2 == 0:
                        even = done
                    else:
                        ps = pair(h)
                        dq_ref[:, ps] = (jnp.where(low, even[0], done[0]) * ATTN_SCALE).astype(BF)
                        dk_ref[:, ps] = (ck_sc[:, ps] + even[1] + done[1]).astype(BF)
                        dv_ref[:, ps] = (cv_sc[:, ps] + even[2] + done[2]).astype(BF)
                        ck_sc[:, ps] = even[3] + done[3]
                        cv_sc[:, ps] = even[4] + done[4]

        @pl.when(n == nq)
        def _():
            dk_ref[...] = ck_sc[...].astype(BF)
            dv_ref[...] = cv_sc[...].astype(BF)

    def cur(n):
        return jnp.minimum(n, nq - 1)

    def prev(n):
        return jnp.maximum(cur(n) - 1, 0)

    blk = lambda f, cb: pl.BlockSpec((T, 1024), lambda r, n: (f(n), cb(r)))
    own = lambda r: r
    outs = _hosted_call(
        body, grid=(dil, nq + 1),
        in_specs=[blk(cur, qcb), blk(prev, kcb), blk(cur, kcb), blk(prev, vcb), blk(cur, vcb), blk(cur, own), blk(cur, own),
                  pl.BlockSpec((T, 128), lambda r, n: (cur(n), r))],
        out_specs=[blk(cur, own), blk(lambda n: jnp.maximum(n - 1, 0), own), blk(lambda n: jnp.maximum(n - 1, 0), own)],
        out_shape=[jax.ShapeDtypeStruct((L, dil * 1024), BF)] * 3,
        scratch_shapes=[pltpu.VMEM((T, 1024), F32), pltpu.VMEM((T, 1024), F32)],
        args=(qa, ka, ka, va, va, doa, oa, lsea), name=name, guest=guest)
    return outs if guest is None else (outs[:3], outs[3:])


FOX_T = 512
FOX_TQ = 512
FOX_TK = 512
FOX_TQ_BWD = 512
FOX_ROWS = 256


def _fox_fwd(qkv, cT, *, name, guest=None):
    S = qkv.shape[0]
    T, TK, R = FOX_TQ, FOX_TK, FOX_ROWS
    nq = S // T
    nt = (((1,), (1,)), ((), ()))
    chains = [(h, rh) for h in range(N_HEADS) for rh in range(T // R)]
    pairs = [(n, j) for n in range(nq) for j in range((n * T + T - 1) // TK + 1)]
    schedule = [jnp.asarray([p[i] for p in pairs], jnp.int32) for i in range(2)]

    def body(n_tab, j_tab, q_ref, k_ref, v_ref, ct_ref, o_ref, lse_ref, m_sc, l_sc, acc_sc):
        n = n_tab[pl.program_id(0)]
        j = j_tab[pl.program_id(0)]
        last_j = (n * T + T - 1) // TK
        lane = lax.broadcasted_iota(jnp.int32, (R, 128), 1)
        low = lane < HEAD_DIM
        ones = jnp.ones((TK, 128), BF)

        @pl.when(j == 0)
        def _():
            m_sc[...] = jnp.full(m_sc.shape, NEG_INF, F32)
            l_sc[...] = jnp.zeros_like(l_sc)
            acc_sc[...] = jnp.zeros_like(acc_sc)

        def step(diagonal):
            def pair(h):
                return slice((h // 2) * 128, (h // 2 + 1) * 128)

            def rows(rh):
                return slice(rh * R, (rh + 1) * R)

            def scores(h, rh):
                qp = q_ref[rows(rh), pair(h)] * jnp.asarray(ATTN_SCALE, BF)
                qm = jnp.where(low if h % 2 == 0 else jnp.logical_not(low), qp, jnp.zeros_like(qp))
                s = lax.dot_general(qm, k_ref[:, pair(h)], nt, preferred_element_type=F32) - ct_ref[h:h + 1, :]
                if diagonal:
                    ahead = lax.broadcasted_iota(jnp.int32, (R, TK), 1) - lax.broadcasted_iota(jnp.int32, (R, TK), 0)
                    s = jnp.where(ahead <= n * T + rh * R - j * TK, s, NEG_INF)
                return s

            def softmax(h, rh, s):
                m_prev = m_sc[h, rows(rh), :]
                m_new = jnp.maximum(m_prev, jnp.max(s, axis=1, keepdims=True))
                p = jnp.exp(s - jnp.concatenate([m_new] * (TK // 128), axis=1)).astype(BF)
                return m_new, jnp.exp(m_prev - m_new), p

            def weighted(h, p):
                vx = jnp.concatenate([v_ref[:, pair(h)], ones], axis=1)
                return jnp.dot(p, vx, preferred_element_type=F32)

            sc, pr, even = {}, {}, {}
            nch = len(chains)
            for t in range(nch + 2):
                if t < nch:
                    sc[t] = scores(*chains[t])
                done = None
                if t >= 2:
                    m_new, alpha, p = pr.pop(t - 2)
                    done = (m_new, alpha, weighted(chains[t - 2][0], p))
                if 1 <= t <= nch:
                    pr[t - 1] = softmax(*chains[t - 1], sc.pop(t - 1))
                if done is not None:
                    h, rh = chains[t - 2]
                    m_new, alpha, pv = done
                    m_sc[h, rows(rh), :] = m_new
                    l_sc[h, rows(rh), :] = alpha * l_sc[h, rows(rh), :] + pv[:, 128:]
                    if h % 2 == 0:
                        even[rh] = (alpha, pv[:, :128])
                    else:
                        a0, pv0 = even.pop(rh)
                        acc = acc_sc[h // 2, rows(rh), :]
                        acc_sc[h // 2, rows(rh), :] = jnp.where(low, a0 * acc + pv0, alpha * acc + pv[:, :128])

        @pl.when(j < last_j)
        def _():
            step(False)

        @pl.when(j == last_j)
        def _():
            step(True)
            lane_t = lax.broadcasted_iota(jnp.int32, (T, 128), 1)
            low_t = lane_t < HEAD_DIM
            lse = jnp.zeros((T, 128), F32)
            for h in range(N_HEADS):
                lse = jnp.where(lane_t == h, m_sc[h] + jnp.log(l_sc[h]), lse)
            lse_ref[...] = lse
            for hp in range(N_HEADS // 2):
                inv = jnp.where(low_t, 1.0 / l_sc[2 * hp], 1.0 / l_sc[2 * hp + 1])
                o_ref[:, hp * 128:(hp + 1) * 128] = (acc_sc[hp] * inv).astype(BF)

    outs = _hosted_call(
        body, grid=(len(pairs),),
        in_specs=[pl.BlockSpec((T, 1024), lambda t, n, j: (n[t], 0)), pl.BlockSpec((TK, 1024), lambda t, n, j: (j[t], 1)),
                  pl.BlockSpec((TK, 1024), lambda t, n, j: (j[t], 2)), pl.BlockSpec((GATE_LANES, TK), lambda t, n, j: (0, j[t]))],
        out_specs=[pl.BlockSpec((T, 1024), lambda t, n, j: (n[t], 0)), pl.BlockSpec((T, 128), lambda t, n, j: (n[t], 0))],
        out_shape=[jax.ShapeDtypeStruct((S, 1024), BF), jax.ShapeDtypeStruct((S, 128), F32)],
        scratch_shapes=[pltpu.VMEM((N_HEADS, T, 128), F32), pltpu.VMEM((N_HEADS, T, 128), F32),
                        pltpu.VMEM((N_HEADS // 2, T, 128), F32)],
        args=(qkv, qkv, qkv, cT), name=name, guest=guest, schedule=schedule)
    return outs if guest is None else (outs[:2], outs[2:])


def _fox_bwd(qkv, cT, do, o, lse, *, name, guest=None):
    S = qkv.shape[0]
    T, TQ, R = FOX_T, FOX_TQ_BWD, FOX_ROWS
    nk, nq = S // T, S // TQ
    nt = (((1,), (1,)), ((), ()))
    tn = (((0,), (0,)), ((), ()))
    nn = (((1,), (0,)), ((), ()))
    chains = [(h, rh) for h in range(N_HEADS) for rh in range(TQ // R)]
    dot = functools.partial(lax.dot_general, preferred_element_type=F32)
    pairs = [(kb, qb) for kb in range(nk) for qb in range(kb * T // TQ, nq)]
    schedule = [jnp.asarray([p[i] for p in pairs], jnp.int32) for i in range(2)]

    def body(kb_tab, qb_tab, q_ref, k_ref, v_ref, ct_ref, do_ref, o_ref, lse_ref, dq_ref, dk_ref, dv_ref, dct_ref, dcq_ref,
             dq_sc, dk_sc, dv_sc, dc_sc, dcq_sc):
        kb = kb_tab[pl.program_id(0)]
        qb = qb_tab[pl.program_id(0)]
        jq = qb - kb * T // TQ
        lane = lax.broadcasted_iota(jnp.int32, (R, 128), 1)
        low = lane < HEAD_DIM
        ones_k = jnp.ones((T, 128), BF)
        ones_r = jnp.ones((8, R), BF)

        @pl.when(jnp.logical_and(kb == 0, jq == 0))
        def _():
            dq_sc[...] = jnp.zeros_like(dq_sc)
            dcq_sc[...] = jnp.zeros_like(dcq_sc)

        @pl.when(jq == 0)
        def _():
            dk_sc[...] = jnp.zeros_like(dk_sc)
            dv_sc[...] = jnp.zeros_like(dv_sc)
            dc_sc[...] = jnp.zeros_like(dc_sc)

        def step(diagonal):
            def pair(h):
                return slice((h // 2) * 128, (h // 2 + 1) * 128)

            def rows(rh):
                return slice(rh * R, (rh + 1) * R)

            def qrows(rh):
                return pl.ds(pl.multiple_of(qb * TQ + rh * R, R), R)

            def products(h, rh):
                mask = low if h % 2 == 0 else jnp.logical_not(low)
                qp = q_ref[rows(rh), pair(h)] * jnp.asarray(ATTN_SCALE, BF)
                dop = do_ref[rows(rh), pair(h)]
                qm = jnp.where(mask, qp, jnp.zeros_like(qp))
                dom = jnp.where(mask, dop, jnp.zeros_like(dop))
                s = dot(qm, k_ref[:, pair(h)], nt) - ct_ref[h:h + 1, :]
                if diagonal:
                    ahead = lax.broadcasted_iota(jnp.int32, (R, T), 1) - lax.broadcasted_iota(jnp.int32, (R, T), 0)
                    s = jnp.where(ahead <= qb * TQ + rh * R - kb * T, s, NEG_INF)
                return qm, dom, s, dot(dom, v_ref[:, pair(h)], nt)

            def pointwise(h, rh, qm, dom, s, dp):
                mask = low if h % 2 == 0 else jnp.logical_not(low)
                prod = do_ref[rows(rh), pair(h)].astype(F32) * o_ref[rows(rh), pair(h)].astype(F32)
                delta = jnp.sum(jnp.where(mask, prod, 0.0), axis=1, keepdims=True)
                p = jnp.exp(s - lse_ref[rows(rh), h:h + 1])
                ds = (p * (dp - delta)).astype(BF)
                return qm, dom, p.astype(BF), ds

            def gradients(h, qm, dom, p, ds):
                kx = jnp.concatenate([k_ref[:, pair(h)], ones_k], axis=1)
                return dot(ds, kx, nn), dot(qm, ds, tn), dot(dom, p, tn), dot(ones_r, ds, nn)

            st1, st2, even = {}, {}, {}
            dcq_tiles = [jnp.zeros((R, 128), F32) for _ in range(TQ // R)]
            nch = len(chains)
            for t in range(nch + 2):
                if t < nch:
                    st1[t] = products(*chains[t])
                done = gradients(chains[t - 2][0], *st2.pop(t - 2)) if t >= 2 else None
                if 1 <= t <= nch:
                    st2[t - 1] = pointwise(*chains[t - 1], *st1.pop(t - 1))
                if done is not None:
                    h, rh = chains[t - 2]
                    dq_rsum, dk, dv, csum = done
                    dq = dq_rsum[:, :128]
                    dcq_tiles[rh] = jnp.where(lane == h, dq_rsum[:, 128:], dcq_tiles[rh])
                    dc_sc[h:h + 1, :] -= csum[0:1, :]
                    if h % 2 == 0:
                        even[rh] = (dq, dk, dv)
                    else:
                        dq0, dk0, dv0 = even.pop(rh)
                        dq_sc[qrows(rh), pair(h)] += jnp.where(low, dq0, dq) * ATTN_SCALE
                        dk_sc[h // 2] += dk0 + dk
                        dv_sc[h // 2] += dv0 + dv
            for rh in range(TQ // R):
                dcq_sc[qrows(rh), :] += dcq_tiles[rh]

        @pl.when(jq > 0)
        def _():
            step(False)

        @pl.when(jq == 0)
        def _():
            step(True)

        @pl.when(qb == nq - 1)
        def _():
            for hp in range(N_HEADS // 2):
                dk_ref[:, hp * 128:(hp + 1) * 128] = dk_sc[hp].T.astype(BF)
                dv_ref[:, hp * 128:(hp + 1) * 128] = dv_sc[hp].T.astype(BF)
            dct_ref[...] = dc_sc[...]

        @pl.when(jnp.logical_and(kb == nk - 1, qb == nq - 1))
        def _():
            def put(i, carry):
                r = pl.ds(pl.multiple_of(i * T, T), T)
                dq_ref[r, :] = dq_sc[r, :].astype(BF)
                return carry
            lax.fori_loop(0, nk, put, 0)
            dcq_ref[...] = dcq_sc[...]

    qblk = lambda col: pl.BlockSpec((TQ, 1024), lambda t, kb, qb: (qb[t], col))
    kblk = lambda col: pl.BlockSpec((T, 1024), lambda t, kb, qb: (kb[t], col))
    whole = pl.BlockSpec((S, 1024), lambda t, kb, qb: (0, 0))
    outs = _hosted_call(
        body, grid=(len(pairs),),
        in_specs=[qblk(0), kblk(1), kblk(2), pl.BlockSpec((GATE_LANES, T), lambda t, kb, qb: (0, kb[t])), qblk(0), qblk(0),
                  pl.BlockSpec((TQ, 128), lambda t, kb, qb: (qb[t], 0))],
        out_specs=[whole, kblk(0), kblk(0), pl.BlockSpec((GATE_LANES, T), lambda t, kb, qb: (0, kb[t])),
                   pl.BlockSpec((S, GATE_LANES), lambda t, kb, qb: (0, 0))],
        out_shape=[jax.ShapeDtypeStruct((S, 1024), BF)] * 3 + [jax.ShapeDtypeStruct((GATE_LANES, S), F32),
                                                               jax.ShapeDtypeStruct((S, GATE_LANES), F32)],
        scratch_shapes=[pltpu.VMEM((S, 1024), F32), pltpu.VMEM((N_HEADS // 2, 128, T), F32), pltpu.VMEM((N_HEADS // 2, 128, T), F32),
                        pltpu.VMEM((GATE_LANES, T), F32), pltpu.VMEM((S, GATE_LANES), F32)],
        args=(qkv, qkv, qkv, cT, do, o, lse), name=name, guest=guest, schedule=schedule)
    return outs if guest is None else (outs[:5], outs[5:])


def _to_natural(src_ref, buf, d, width):
    rows = buf.shape[1]
    for r in range(d):
        for ch in range(width // 128):
            lanes = slice(r * width + ch * 128, r * width + (ch + 1) * 128)
            buf.at[ch][pl.ds(r, rows // d, stride=d), :] = src_ref[:, lanes].astype(F32)
    return jnp.concatenate([buf[ch] for ch in range(width // 128)], axis=1)


def _to_view(val, buf, dst_ref, d, width):
    rows = buf.shape[1]
    for ch in range(width // 128):
        buf[ch] = val[:, ch * 128:(ch + 1) * 128]
    for r in range(d):
        for ch in range(width // 128):
            lanes = slice(r * width + ch * 128, r * width + (ch + 1) * 128)
            dst_ref[:, lanes] = buf.at[ch][pl.ds(r, rows // d, stride=d), :].astype(dst_ref.dtype)


def _view_spec(rows, d, width):
    return pl.BlockSpec((rows // d, d * width), lambda i, *_: (i, 0))


def _combine_groups(os, lses, dils, *, name):
    ng = len(os)
    S = os[0].shape[0] * dils[0]
    tm = ROW_TILE
    views = sorted(set(dils))

    def body(*refs):
        o_refs, l_refs = refs[:ng], refs[ng:2 * ng]
        outs = refs[2 * ng:2 * ng + 2 * len(views)]
        wide, narrow = refs[-2], refs[-1]
        ls = [l_refs[g][...] if dils[g] == 1 else _to_natural(l_refs[g], narrow, dils[g], 128) for g in range(ng)]
        m = functools.reduce(jnp.maximum, ls)
        es = [jnp.exp(l - m) for l in ls]
        den = functools.reduce(jnp.add, es)
        ws = [e / den for e in es]
        lse = m + jnp.log(den)
        og = [o_refs[g][...] if dils[g] == 1 else _to_natural(o_refs[g], wide, dils[g], 1024) for g in range(ng)]
        cols = []
        for h in range(N_HEADS):
            hs = slice(h * HEAD_DIM, (h + 1) * HEAD_DIM)
            acc = ws[0][:, h:h + 1] * og[0][:, hs]
            for g in range(1, ng):
                acc = acc + ws[g][:, h:h + 1] * og[g][:, hs]
            cols.append(acc)
        o = jnp.concatenate(cols, axis=1)
        for k, d in enumerate(views):
            if d == 1:
                outs[2 * k][...] = o.astype(BF)
                outs[2 * k + 1][...] = lse
            else:
                _to_view(o, wide, outs[2 * k], d, 1024)
                _to_view(lse, narrow, outs[2 * k + 1], d, 128)

    out_specs, out_shape = [], []
    for d in views:
        out_specs += [_view_spec(tm, d, 1024), _view_spec(tm, d, 128)]
        out_shape += [jax.ShapeDtypeStruct((S // d, d * 1024), BF), jax.ShapeDtypeStruct((S // d, d * 128), F32)]
    res = pl.pallas_call(
        body, grid=(S // tm,), in_specs=[_view_spec(tm, d, 1024) for d in dils] + [_view_spec(tm, d, 128) for d in dils],
        out_specs=out_specs, out_shape=out_shape,
        scratch_shapes=[pltpu.VMEM((8, tm, 128), F32), pltpu.VMEM((1, tm, 128), F32)],
        name=name, compiler_params=_params("parallel"),
    )(*os, *lses)
    return {d: (res[2 * k], res[2 * k + 1]) for k, d in enumerate(views)}


def _assemble(parts, rope_flags, rope, dils, *, name):
    n = len(parts)
    S = parts[0].shape[0] * dils[0]
    use_rope = any(rope_flags)
    tm = ROW_TILE

    def body(*refs):
        out_ref, natural = refs[-2], refs[-1]
        for b in range(n):
            cols = slice(b * 1024, (b + 1) * 1024)
            d = dils[b]
            val = refs[b][...].astype(F32) if d == 1 else _to_natural(refs[b], natural, d, 1024)
            if rope_flags[b]:
                cos_ref, sa_ref, sb_ref = refs[n:n + 3]
                val = _rope_rotate(val, cos_ref[...], sa_ref[...], sb_ref[...])
            out_ref[:, cols] = val.astype(BF)

    in_specs = [_view_spec(tm, d, 1024) for d in dils]
    args = list(parts)
    if use_rope:
        in_specs += [pl.BlockSpec((tm, 128), lambda i: (i, 0))] * 3
        args += list(rope)
    return pl.pallas_call(
        body, grid=(S // tm,), in_specs=in_specs, out_specs=pl.BlockSpec((tm, n * 1024), lambda i: (i, 0)),
        out_shape=jax.ShapeDtypeStruct((S, n * 1024), BF), scratch_shapes=[pltpu.VMEM((8, tm, 128), F32)],
        name=name, compiler_params=_params("parallel"),
    )(*args)


GATE_ROWS = 512


def _gate_fwd(z, bf, *, name):
    S = z.shape[0]

    def body(z_ref, b_ref, ct_ref, carry):
        i = pl.program_id(0)

        @pl.when(i == 0)
        def _():
            carry[...] = jnp.zeros_like(carry)

        zz = z_ref[...] + b_ref[...]
        logf = jnp.minimum(zz, 0.0) - jnp.log(1.0 + jnp.exp(-jnp.abs(zz)))
        tri = (lax.broadcasted_iota(jnp.int32, (GATE_ROWS, GATE_ROWS), 0)
               >= lax.broadcasted_iota(jnp.int32, (GATE_ROWS, GATE_ROWS), 1)).astype(F32)
        cs = jnp.dot(tri, logf, precision=lax.Precision.HIGHEST, preferred_element_type=F32) + carry[...]
        ct_ref[...] = cs.T
        carry[...] = cs[GATE_ROWS - 1:GATE_ROWS, :]

    return pl.pallas_call(
        body, grid=(S // GATE_ROWS,),
        in_specs=[pl.BlockSpec((GATE_ROWS, GATE_LANES), lambda i: (i, 0)), pl.BlockSpec((1, GATE_LANES), lambda i: (0, 0))],
        out_specs=pl.BlockSpec((GATE_LANES, GATE_ROWS), lambda i: (0, i)),
        out_shape=jax.ShapeDtypeStruct((GATE_LANES, S), F32),
        scratch_shapes=[pltpu.VMEM((1, GATE_LANES), F32)], name=name, compiler_params=_params("arbitrary"),
    )(z, bf)


def _gate_bwd(z, bf, dcT, dcq, *, name):
    S = z.shape[0]
    nb = S // GATE_ROWS

    def body(z_ref, b_ref, dct_ref, dcq_ref, dz_ref, db_ref, carry):
        i = pl.program_id(0)

        @pl.when(i == 0)
        def _():
            carry[...] = jnp.zeros_like(carry)
            db_ref[...] = jnp.zeros_like(db_ref)

        dc = dct_ref[...].T + dcq_ref[...]
        tri = (lax.broadcasted_iota(jnp.int32, (GATE_ROWS, GATE_ROWS), 0)
               <= lax.broadcasted_iota(jnp.int32, (GATE_ROWS, GATE_ROWS), 1)).astype(F32)
        dl = jnp.dot(tri, dc, precision=lax.Precision.HIGHEST, preferred_element_type=F32) + carry[...]
        carry[...] = dl[0:1, :]
        zz = z_ref[...] + b_ref[...]
        dz = dl * (1.0 / (1.0 + jnp.exp(zz)))
        lane = lax.broadcasted_iota(jnp.int32, dz.shape, 1)
        dz = jnp.where(lane < N_HEADS, dz, 0.0)
        dz_ref[...] = dz.astype(BF)
        db_ref[...] += jnp.sum(dz, axis=0, keepdims=True)

    return pl.pallas_call(
        body, grid=(nb,),
        in_specs=[pl.BlockSpec((GATE_ROWS, GATE_LANES), lambda i: (nb - 1 - i, 0)), pl.BlockSpec((1, GATE_LANES), lambda i: (0, 0)),
                  pl.BlockSpec((GATE_LANES, GATE_ROWS), lambda i: (0, nb - 1 - i)),
                  pl.BlockSpec((GATE_ROWS, GATE_LANES), lambda i: (nb - 1 - i, 0))],
        out_specs=[pl.BlockSpec((GATE_ROWS, GATE_LANES), lambda i: (nb - 1 - i, 0)), pl.BlockSpec((1, GATE_LANES), lambda i: (0, 0))],
        out_shape=[jax.ShapeDtypeStruct((S, GATE_LANES), BF), jax.ShapeDtypeStruct((1, GATE_LANES), F32)],
        scratch_shapes=[pltpu.VMEM((1, GATE_LANES), F32)], name=name, compiler_params=_params("arbitrary"),
    )(z, bf, dcT, dcq)


def _rope_tables(S):
    half = ROT_DIM // 2
    inv_freq = ROPE_THETA ** (-jnp.arange(half, dtype=F32) * 2.0 / ROT_DIM)
    ang = jnp.arange(S, dtype=F32)[:, None] * inv_freq[None, :]
    cos, sin = jnp.cos(ang), jnp.sin(ang)
    zero = jnp.zeros((S, HEAD_DIM - ROT_DIM), F32)
    zh = jnp.zeros((S, half), F32)
    cos_h = jnp.concatenate([cos, cos, jnp.ones_like(zero)], axis=1)
    sa_h = jnp.concatenate([-sin, zh, zero], axis=1)
    sb_h = jnp.concatenate([zh, sin, zero], axis=1)
    two = lambda t: jnp.concatenate([t, t], axis=1)
    return two(cos_h), two(sa_h), two(sb_h)


def _ffn_fwd(h, norm, w_gu, w_down, tag):
    n = _rms_fwd(h, norm, name=f"ffn{tag}_norm")
    gu = _mm_nn(n, w_gu, tm=2 * MM_ROWS, tn=1408, out_dtype=BF, name=f"ffn{tag}_gu")
    act = _swiglu_fwd(gu, name=f"ffn{tag}_act")
    out = _mm_nn(act, w_down, tm=MM_ROWS // 2, tn=1024, out_dtype=F32, name=f"ffn{tag}_down", resid=h)
    return out, (h, n, gu, act)


def _ffn_bwd(dh, dhb, saved, norm, w_gu, w_down, tag, ride=None):
    h, n, gu, act = saved
    dact = _mm_nt(dhb, w_down, tm=2 * MM_ROWS, to=1408, tn=1024, out_dtype=BF, name=f"ffn{tag}_dact")
    dw_down = _mm_tn(act, dhb, tk=1408, tn=1024, tm=2 * MM_ROWS, out_dtype=BF, name=f"ffn{tag}_dwdown")
    dgu = _swiglu_bwd(gu, dact, name=f"ffn{tag}_dgu")
    dn_call = lambda guest: _mm_nt(dgu, w_gu, tm=2 * MM_ROWS, to=1024, tn=1408, out_dtype=F32, name=f"ffn{tag}_dn", guest=guest)
    dn = dn_call(None) if ride is None else ride(dn_call)
    dw_gu = _mm_tn(n, dgu, tk=1024, tn=1408, tm=2 * MM_ROWS, out_dtype=BF, name=f"ffn{tag}_dwgu")
    dx, dxb, dg = _rms_bwd(h, norm, dn, dh, name=f"ffn{tag}_dnorm")
    return dx, dxb, dg, dw_gu, dw_down


def _local_step(x, tgt, w, mats, fetch, exchange):
    S = x.shape[0]
    rope_f = _rope_tables(S)
    rope_b = (rope_f[0], -rope_f[1], -rope_f[2])
    g, partial, landed = {}, {}, {}
    w = dict(w, ffn_w_gu={}, ffn_w_down={})

    def bring(call, indices):
        bufs = [mats[wi] for wi in indices]
        if fetch is None:
            return call(None), bufs
        return call(fetch(indices, bufs))

    def ride(call, indices):
        guest = exchange(indices, [partial[wi] for wi in indices]) if indices else None
        res = call(guest)
        if guest is None:
            return res
        res, outs = res
        landed.update(zip(indices, outs))
        return res

    n0 = _rms_fwd(x, w["a_norm"], name="a_norm")
    dils = [d for _, d in DILATED_PATTERNS]
    projs, (w["ffn_w_gu"][0], w["ffn_w_down"][0]) = bring(
        lambda guest: _mm_nn(n0, w["a_w_in"], tm=MM_ROWS, tn=1024, out_dtype=BF, name="a_proj", rope=rope_f, guest=guest,
                             groups=dils), [4, 6])
    block = lambda t, dil: (lambda r: t * dil + r)
    o_parts, lse_parts = [], []
    for gi, (window, dil) in enumerate(DILATED_PATTERNS):
        pv = projs[gi]
        attend = lambda guest: _band_fwd(pv, pv, pv, block(0, dil), block(1, dil), block(2, dil), dil=dil, T=128,
                                         window=window // dil, name=f"a_attn{gi}", guest=guest)
        if gi == 0:
            (o_g, lse_g), (w["a_w_out"],) = bring(attend, [1])
        elif gi == 1:
            (o_g, lse_g), (b_in,) = bring(attend, [2])
        else:
            (o_g, lse_g), (w["b_w_out"],) = bring(attend, [3])
        o_parts.append(o_g)
        lse_parts.append(lse_g)
    b_in = b_in.transpose(1, 0, 2).reshape(D_MODEL, -1)
    w["b_w_qkv"] = b_in[:, :QKV_COLS]
    w["b_w_f"] = jnp.pad(b_in[:, QKV_COLS:], ((0, 0), (0, GATE_LANES + QKV_COLS - b_in.shape[1])))
    mixed = _combine_groups(o_parts, lse_parts, dils, name="a_combine")
    o_a = mixed[1][0]
    h1 = _mm_nn(o_a, w["a_w_out"], tm=MM_ROWS, tn=1024, out_dtype=F32, name="a_out", resid=x)
    h2, ffn0 = _ffn_fwd(h1, w["ffn_norm"][0:1], w["ffn_w_gu"][0], w["ffn_w_down"][0], 0)

    n2 = _rms_fwd(h2, w["b_norm"], name="b_norm")
    qkv = _mm_nn(n2, w["b_w_qkv"], tm=MM_ROWS, tn=1024, out_dtype=BF, name="b_proj")
    zf = _mm_nn(n2, w["b_w_f"], tm=MM_ROWS, tn=GATE_LANES, out_dtype=F32, name="b_gate_proj")
    cT = _gate_fwd(zf, w["b_f"], name="b_gate")
    (o_b, lse_b), (w["ffn_w_gu"][1], w["ffn_w_down"][1]) = bring(lambda guest: _fox_fwd(qkv, cT, name="b_attn", guest=guest), [5, 7])
    h3 = _mm_nn(o_b, w["b_w_out"], tm=MM_ROWS, tn=1024, out_dtype=F32, name="b_out", resid=h2)
    h4, ffn1 = _ffn_fwd(h3, w["ffn_norm"][1:2], w["ffn_w_gu"][1], w["ffn_w_down"][1], 1)

    loss, dh4, dh4b, g["final_norm"] = _loss_head(h4, w["final_norm"], tgt, name="loss_head")

    dh3, dh3b, dg_f1, partial[5], partial[7] = _ffn_bwd(dh4, dh4b, ffn1, w["ffn_norm"][1:2], w["ffn_w_gu"][1], w["ffn_w_down"][1], 1)

    do_b = _mm_nt(dh3b, w["b_w_out"], tm=MM_ROWS, to=1024, tn=1024, out_dtype=BF, name="b_do")
    partial[3] = _mm_tn(o_b, dh3b, tk=1024, tn=1024, tm=MM_ROWS, out_dtype=BF, name="b_dwout")
    dq, dk, dv, dcT, dcq = ride(lambda guest: _fox_bwd(qkv, cT, do_b, o_b, lse_b, name="b_attn_bwd", guest=guest), [5, 7, 3])
    dz, g["b_f"] = _gate_bwd(zf, w["b_f"], dcT, dcq, name="b_gate_bwd")
    dqkv = _assemble([dq, dk, dv], [False] * 3, None, [1] * 3, name="b_dproj")
    dn2 = _mm_nt(dz, w["b_w_f"], tm=MM_ROWS, to=1024, tn=GATE_LANES, out_dtype=F32, name="b_dn_gate")
    dn2 = _mm_nt(dqkv, w["b_w_qkv"], tm=MM_ROWS, to=1024, tn=1024, out_dtype=F32, name="b_dn", add=dn2)
    g_qkv = _mm_tn(n2, dqkv, tk=1024, tn=1024, tm=MM_ROWS, out_dtype=BF, name="b_dwqkv")
    g_f = _mm_tn(n2, dz, tk=1024, tn=GATE_LANES, tm=MM_ROWS, out_dtype=BF, name="b_dwf")
    g_b_in = jnp.concatenate([g_qkv, g_f[:, :N_HEADS]], axis=1)
    partial[2] = g_b_in.reshape(D_MODEL, N_CHIPS, -1).transpose(1, 0, 2)
    dh2, dh2b, g["b_norm"] = _rms_bwd(h2, w["b_norm"], dn2, dh3, name="b_dnorm")

    dh1, dh1b, dg_f0, partial[4], partial[6] = _ffn_bwd(dh2, dh2b, ffn0, w["ffn_norm"][0:1], w["ffn_w_gu"][0], w["ffn_w_down"][0], 0,
                                                      ride=lambda call: ride(call, [2]))
    g["ffn_norm"] = jnp.concatenate([dg_f0, dg_f1], axis=0)

    views = tuple(sorted(set(dils)))
    do_a = dict(zip(views, _mm_nt(dh1b, w["a_w_out"], tm=MM_ROWS, to=1024, tn=1024, out_dtype=BF, name="a_do", views=views)))
    partial[1] = _mm_tn(o_a, dh1b, tk=1024, tn=1024, tm=MM_ROWS, out_dtype=BF, name="a_dwout")
    riders = {0: [4], 1: [6, 1], 2: []}
    parts = []
    for gi, (window, dil) in enumerate(DILATED_PATTERNS):
        pv = projs[gi]
        res = ride(lambda guest: _band_bwd(pv, pv, pv, block(0, dil), block(1, dil), block(2, dil), do_a[dil],
                                           mixed[dil][0], mixed[dil][1], dil=dil, T=128,
                                           window=window // dil, name=f"a_attn_bwd{gi}", guest=guest), riders[gi])
        parts += list(res)
    dproj = _assemble(parts, [True, True, False] * 3, rope_b, [d for _, d in DILATED_PATTERNS for _ in range(3)], name="a_dproj")
    partial[0] = _mm_tn(n0, dproj, tk=1024, tn=1024, tm=2 * MM_ROWS, out_dtype=BF, name="a_dwin")
    dn0 = ride(lambda guest: _mm_nt(dproj, w["a_w_in"], tm=2 * MM_ROWS, to=1024, tn=1024, out_dtype=F32, name="a_dn", guest=guest), [0])
    dx, _, g["a_norm"] = _rms_bwd(x, w["a_norm"], dn0, dh1, name="a_dnorm")
    return loss, dx, g, partial, landed


ANY = pl.BlockSpec(memory_space=pl.ANY)


def _place():
    x, y, c = lax.axis_index("x"), lax.axis_index("y"), lax.axis_index("c")
    chips = [(1 - x, y), (x, 1 - y), (1 - x, 1 - y)]
    return x, y, c, chips


def _shard_slice(ref, kind, rows, cols, s, half):
    hr = rows // 2
    if kind == "col":
        return ref.at[pl.ds(half * hr, hr), pl.ds(pl.multiple_of(s * cols, 128), cols)]
    if kind == "row":
        return ref.at[pl.ds(pl.multiple_of(s * rows + half * hr, 16), hr), :]
    return ref.at[s, pl.ds(half * hr, hr), :]


def _whole_shape(kind, rows, cols):
    return {"col": (rows, N_CHIPS * cols), "row": (N_CHIPS * rows, cols), "stack": (N_CHIPS, rows, cols)}[kind]


def _own_block(kind, rows, tr, cols):
    per = rows // tr

    def spec(half_rows):
        off = (lambda p: 0) if half_rows is None else (lambda p: p[1] * (half_rows // tr))
        if kind == "col":
            return pl.BlockSpec((tr, cols), lambda i, p: (off(p) + i, p[0]))
        if kind == "row":
            return pl.BlockSpec((tr, cols), lambda i, p: (p[0] * per + off(p) + i, 0))
        return pl.BlockSpec((None, tr, cols), lambda i, p: (p[0], off(p) + i, 0))
    return spec


def _place_shard(shards, layer, kind, place, *, name):
    _, rows, cols = shards.shape
    tr = 256 if rows % 256 == 0 else rows // 2

    def body(p_ref, s_ref, o_ref):
        o_ref[...] = s_ref[...].astype(BF)

    return pl.pallas_call(
        body,
        grid_spec=pltpu.PrefetchScalarGridSpec(
            num_scalar_prefetch=1, grid=(rows // tr,),
            in_specs=[pl.BlockSpec((None, tr, cols), lambda i, p: (layer, i, 0))],
            out_specs=_own_block(kind, rows, tr, cols)(None)),
        out_shape=jax.ShapeDtypeStruct(_whole_shape(kind, rows, cols), BF),
        name=name, compiler_params=_params("arbitrary"),
    )(place, shards)


def _gather_weights(placed, kinds, dims):
    nw = len(placed)

    def body(*refs):
        dst = refs[nw:2 * nw]
        send_sems, recv_sems = refs[2 * nw:]
        x, y, c, chips = _place()
        me = 2 * x + y
        sibling = (x, y, 1 - c)

        def copy(wi, k, s, half, to):
            p = _shard_slice(dst[wi], kinds[wi], dims[wi][0], dims[wi][1], s, half)
            return pltpu.make_async_remote_copy(src_ref=p, dst_ref=p, send_sem=send_sems.at[wi * 6 + k],
                                                recv_sem=recv_sems.at[wi * 6 + k], device_id=to, device_id_type=MESH)

        first, passed = [], []
        for wi in range(nw):
            for j, chip in enumerate(chips):
                cp = copy(wi, j, me, c, (*chip, c))
                cp.start()
                first.append(cp)
        for wi in range(nw):
            for j, chip in enumerate(chips):
                s = 2 * chip[0] + chip[1]
                copy(wi, j, s, c, (x, y, c)).wait_recv()
                cp = copy(wi, 3 + j, s, c, sibling)
                cp.start()
                passed.append(cp)
        for wi in range(nw):
            for j, chip in enumerate(chips):
                s = 2 * chip[0] + chip[1]
                copy(wi, 3 + j, s, 1 - c, (x, y, c)).wait_recv()
        for cp in first + passed:
            cp.wait_send()

    return pl.pallas_call(
        body, in_specs=[ANY] * nw, out_specs=[ANY] * nw,
        out_shape=[jax.ShapeDtypeStruct(p.shape, p.dtype) for p in placed],
        input_output_aliases={wi: wi for wi in range(nw)},
        scratch_shapes=[pltpu.SemaphoreType.DMA((nw * 6,)), pltpu.SemaphoreType.DMA((nw * 6,))],
        name="gather_weights",
    )(*placed)


def _fetch_guest(placed, kinds, dims):
    nw = len(placed)

    def copies(dst, send_sems, recv_sems, incoming):
        x, y, c, chips = _place()
        out = []
        for wi in range(nw):
            for j, chip in enumerate(chips):
                s = 2 * chip[0] + chip[1] if incoming else 2 * x + y
                to = (x, y, c) if incoming else (*chip, c)
                for half in range(2):
                    p = _shard_slice(dst[wi], kinds[wi], dims[wi][0], dims[wi][1], s, half)
                    k = wi * 6 + 2 * j + half
                    out.append(pltpu.make_async_remote_copy(src_ref=p, dst_ref=p, send_sem=send_sems.at[k],
                                                            recv_sem=recv_sems.at[k], device_id=to, device_id_type=MESH))
        return out

    def start(src, dst, sems):
        for cp in copies(dst, sems[0], sems[1], False):
            cp.start()

    def finish(src, dst, sems):
        for cp in copies(dst, sems[0], sems[1], True):
            cp.wait_recv()
        for cp in copies(dst, sems[0], sems[1], False):
            cp.wait_send()

    return dict(args=list(placed), out_shape=[jax.ShapeDtypeStruct(p.shape, p.dtype) for p in placed],
                scratch=[pltpu.SemaphoreType.DMA((nw * 6,)), pltpu.SemaphoreType.DMA((nw * 6,))],
                start=start, finish=finish, in_place=True)


def _scatter_guest(partials, kinds, dims):
    nw = len(partials)

    def copies(src, send_sems, recv_sems, dst):
        x, y, c, chips = _place()
        me = 2 * x + y
        out = []
        for wi in range(nw):
            rows, cols = dims[wi]

            def part(s, half, wi=wi, rows=rows, cols=cols):
                return _shard_slice(src[wi], kinds[wi], rows, cols, s, half)

            for j, chip in enumerate(chips):
                s = 2 * chip[0] + chip[1]
                for half in range(2):
                    slot = 2 * j + (c if half == 0 else 1 - c)
                    out.append(pltpu.make_async_remote_copy(
                        src_ref=part(s, half), dst_ref=dst[wi].at[slot],
                        send_sem=send_sems.at[wi * 7 + 2 * j + half], recv_sem=recv_sems.at[wi * 7 + slot],
                        device_id=(*chip, half), device_id_type=MESH))
            out.append(pltpu.make_async_remote_copy(
                src_ref=part(me, 1 - c), dst_ref=dst[wi].at[6],
                send_sem=send_sems.at[wi * 7 + 6], recv_sem=recv_sems.at[wi * 7 + 6],
                device_id=(x, y, 1 - c), device_id_type=MESH))
        return out

    def start(src, dst, sems):
        for cp in copies(src, sems[0], sems[1], dst):
            cp.start()

    def finish(src, dst, sems):
        x, y, c, _ = _place()
        for wi in range(nw):
            for slot in range(7):
                pltpu.make_async_remote_copy(
                    src_ref=dst[wi].at[slot], dst_ref=dst[wi].at[slot],
                    send_sem=sems[0].at[wi * 7 + slot], recv_sem=sems[1].at[wi * 7 + slot],
                    device_id=(x, y, c), device_id_type=MESH).wait_recv()
        for cp in copies(src, sems[0], sems[1], dst):
            cp.wait_send()

    return dict(args=list(partials), out_shape=[jax.ShapeDtypeStruct((7, d[0] // 2, d[1]), BF) for d in dims],
                scratch=[pltpu.SemaphoreType.DMA((nw * 7,)), pltpu.SemaphoreType.DMA((nw * 7,))],
                start=start, finish=finish)


def _sum_slots(slots, partial, kind, dims, place, *, name, into=None, layer=None, n_layers=1):
    rows, cols = dims
    hr = rows // 2
    tr = hr if 8 * hr * cols * 2 <= 6 * 1024 * 1024 else 128
    assert hr % tr == 0

    def body(p_ref, b_ref, own_ref, *rest):
        o_ref = rest[-1]
        acc = own_ref[...].astype(F32)
        for k in range(7):
            acc = acc + b_ref[k].astype(F32)
        o_ref[...] = acc

    half = lambda p: p[1] * (hr // tr)
    if n_layers == 1:
        out_spec = pl.BlockSpec((tr, cols), lambda i, p: (half(p) + i, 0))
        out_shape = jax.ShapeDtypeStruct((rows, cols), F32)
    else:
        out_spec = pl.BlockSpec((None, tr, cols), lambda i, p: (layer, half(p) + i, 0))
        out_shape = jax.ShapeDtypeStruct((n_layers, rows, cols), F32)
    in_specs = [pl.BlockSpec((7, tr, cols), lambda i, p: (0, i, 0)), _own_block(kind, rows, tr, cols)(hr)]
    args = [place, slots, partial]
    aliases = {}
    if into is not None:
        in_specs.append(ANY)
        args.append(into)
        aliases = {3: 0}
    return pl.pallas_call(
        body,
        grid_spec=pltpu.PrefetchScalarGridSpec(num_scalar_prefetch=1, grid=(hr // tr,), in_specs=in_specs, out_specs=out_spec),
        out_shape=out_shape, input_output_aliases=aliases, name=name, compiler_params=_params("arbitrary"),
    )(*args)


def _pair_exchange(bufs, members):
    nw = len(members)

    def body(*refs):
        dst = refs[len(bufs):2 * len(bufs)]
        send_sems, recv_sems = refs[2 * len(bufs):]
        x, y, c, _ = _place()

        def rows_of(wi, half):
            bi, l = members[wi]
            ref = dst[bi] if l is None else dst[bi].at[l]
            hr = ref.shape[0] // 2
            return ref.at[pl.ds(pl.multiple_of(half * hr, 8), hr), :]

        def copy(wi, half, to):
            p = rows_of(wi, half)
            return pltpu.make_async_remote_copy(src_ref=p, dst_ref=p, send_sem=send_sems.at[wi], recv_sem=recv_sems.at[wi],
                                                device_id=to, device_id_type=MESH)

        sent = []
        for wi in range(nw):
            cp = copy(wi, c, (x, y, 1 - c))
            cp.start()
            sent.append(cp)
        for wi in range(nw):
            copy(wi, 1 - c, (x, y, c)).wait_recv()
        for cp in sent:
            cp.wait_send()

    return pl.pallas_call(
        body, in_specs=[ANY] * len(bufs), out_specs=[ANY] * len(bufs),
        out_shape=[jax.ShapeDtypeStruct(b.shape, b.dtype) for b in bufs],
        input_output_aliases={i: i for i in range(len(bufs))},
        scratch_shapes=[pltpu.SemaphoreType.DMA((nw,)), pltpu.SemaphoreType.DMA((nw,))],
        name="pair_exchange",
    )(*bufs)


SMALL_ROWS = 8


def _allreduce_small(v, *, name):
    assert v.shape == (SMALL_ROWS, D_MODEL)

    def body(v_ref, o_ref, buf, send_sems, recv_sems):
        x, y, c, _ = _place()
        me = 4 * x + 2 * y + c
        buf[me] = v_ref[...]
        sent = []
        for k in range(1, 8):
            bx, by, bc = (k >> 2) & 1, (k >> 1) & 1, k & 1
            peer = (1 - x if bx else x, 1 - y if by else y, 1 - c if bc else c)
            cp = pltpu.make_async_remote_copy(src_ref=v_ref, dst_ref=buf.at[me], send_sem=send_sems.at[k - 1],
                                              recv_sem=recv_sems.at[k - 1], device_id=peer, device_id_type=MESH)
            cp.start()
            sent.append(cp)
        for k in range(1, 8):
            bx, by, bc = (k >> 2) & 1, (k >> 1) & 1, k & 1
            peer = 4 * (1 - x if bx else x) + 2 * (1 - y if by else y) + (1 - c if bc else c)
            pltpu.make_async_remote_copy(src_ref=v_ref, dst_ref=buf.at[peer], send_sem=send_sems.at[k - 1],
                                         recv_sem=recv_sems.at[k - 1], device_id=(x, y, c), device_id_type=MESH).wait_recv()
        for cp in sent:
            cp.wait_send()
        acc = buf[0]
        for d in range(1, 8):
            acc = acc + buf[d]
        o_ref[...] = acc

    vmem = pl.BlockSpec(memory_space=pltpu.VMEM)
    return pl.pallas_call(
        body, in_specs=[vmem], out_specs=vmem, out_shape=jax.ShapeDtypeStruct(v.shape, F32),
        scratch_shapes=[pltpu.VMEM((8,) + v.shape, F32), pltpu.SemaphoreType.DMA((7,)), pltpu.SemaphoreType.DMA((7,))],
        name=name,
    )(v)


def _adamw(w, g, m, v, *, name):
    R, C = w.shape
    tr = R
    if R * C * 4 > 1024 * 1024:
        tr = max(t for t in range(8, R, 8) if R % t == 0 and t * C * 4 <= 1024 * 1024)

    def body(w_ref, g_ref, m_ref, v_ref, d_ref, m2_ref, v2_ref):
        gg = g_ref[...]
        m2 = ADAM_B1 * m_ref[...] + (1.0 - ADAM_B1) * gg
        v2 = ADAM_B2 * v_ref[...] + (1.0 - ADAM_B2) * jnp.square(gg)
        m_hat = m2 / (1.0 - ADAM_B1 ** ADAM_STEP)
        v_hat = v2 / (1.0 - ADAM_B2 ** ADAM_STEP)
        d_ref[...] = -ADAM_LR * (m_hat / (jnp.sqrt(v_hat) + ADAM_EPS) + ADAM_WD * w_ref[...])
        m2_ref[...] = m2
        v2_ref[...] = v2

    blk = pl.BlockSpec((tr, C), lambda i: (i, 0))
    out = jax.ShapeDtypeStruct((R, C), F32)
    return pl.pallas_call(
        body, grid=(R // tr,), in_specs=[blk] * 4, out_specs=[blk] * 3, out_shape=[out] * 3,
        name=name, compiler_params=_params("parallel"),
    )(w, g, m, v)


WEIGHT_ORDER = ("a_norm", "a_w_in", "a_w_out", "b_norm", "b_w_in", "b_f", "b_w_out", "ffn_norm", "ffn_w_gu",
                "ffn_w_down", "final_norm")
MATRICES = (("a_w_in", 0, "col"), ("a_w_out", 0, "row"), ("b_w_in", 0, "stack"), ("b_w_out", 0, "row"),
            ("ffn_w_gu", 0, "col"), ("ffn_w_gu", 1, "col"), ("ffn_w_down", 0, "row"), ("ffn_w_down", 1, "row"))
MATRIX_GROUPS = ([0], [1], [2], [3], [4, 5], [6, 7])
GROUP_NAMES = ("a_w_in", "a_w_out", "b_w_in", "b_w_out", "ffn_w_gu", "ffn_w_down")
QKV_COLS = 3 * N_HEADS * HEAD_DIM


def kernel(x, a_norm, a_w_in, a_w_out, b_norm, b_w_in, b_f, b_w_out, ffn_norm, ffn_w_gu, ffn_w_down, final_norm, loss_target, m_a_norm, m_a_w_in, m_a_w_out, m_b_norm, m_b_w_in, m_b_f, m_b_w_out, m_ffn_norm, m_ffn_w_gu, m_ffn_w_down, m_final_norm, v_a_norm, v_a_w_in, v_a_w_out, v_b_norm, v_b_w_in, v_b_f, v_b_w_out, v_ffn_norm, v_ffn_w_gu, v_ffn_w_down, v_final_norm):
    given = dict(a_norm=a_norm, a_w_in=a_w_in, a_w_out=a_w_out, b_norm=b_norm, b_w_in=b_w_in, b_f=b_f, b_w_out=b_w_out,
                 ffn_norm=ffn_norm, ffn_w_gu=ffn_w_gu, ffn_w_down=ffn_w_down, final_norm=final_norm)
    mom_m = dict(a_norm=m_a_norm, a_w_in=m_a_w_in, a_w_out=m_a_w_out, b_norm=m_b_norm, b_w_in=m_b_w_in, b_f=m_b_f,
                 b_w_out=m_b_w_out, ffn_norm=m_ffn_norm, ffn_w_gu=m_ffn_w_gu, ffn_w_down=m_ffn_w_down, final_norm=m_final_norm)
    mom_v = dict(a_norm=v_a_norm, a_w_in=v_a_w_in, a_w_out=v_a_w_out, b_norm=v_b_norm, b_w_in=v_b_w_in, b_f=v_b_f,
                 b_w_out=v_b_w_out, ffn_norm=v_ffn_norm, ffn_w_gu=v_ffn_w_gu, ffn_w_down=v_ffn_w_down, final_norm=v_final_norm)
    chip = 2 * lax.axis_index("x") + lax.axis_index("y")
    core = lax.axis_index("c")
    bn_cols = b_norm.shape[1]

    placed = lax.dynamic_update_slice(jnp.zeros((SMALL_ROWS, D_MODEL), F32), b_norm, (0, chip * bn_cols))
    placed = placed * (core == 0).astype(F32)
    b_norm_full = _allreduce_small(placed, name="gather_b_norm")[0:1]

    place = jnp.stack([chip, core]).astype(jnp.int32)
    kinds = [k for _, _, k in MATRICES]
    dims = [given[n].shape[1:] for n, _, _ in MATRICES]
    placed = [_place_shard(given[n], l, k, place, name=f"place_{n}{l}") for n, l, k in MATRICES]
    first = _gather_weights(placed[:1], kinds[:1], dims[:1])
    mats = dict(enumerate(list(first) + placed[1:]))
    gate_cols = b_f.shape[1]
    w = dict(a_norm=a_norm, a_w_in=mats[0], b_norm=b_norm_full,
             b_f=jnp.pad(b_f, ((0, 0), (0, GATE_LANES - gate_cols))), ffn_norm=ffn_norm,
             final_norm=final_norm.reshape(1, D_MODEL))

    def fetch(indices, bufs):
        return _fetch_guest(bufs, [kinds[i] for i in indices], [dims[i] for i in indices])

    def exchange(indices, parts):
        return _scatter_guest(parts, [kinds[i] for i in indices], [dims[i] for i in indices])

    loss, dx, g, partials, slots = _local_step(x[0], loss_target[0], w, mats, fetch, exchange)
    bufs, members = [], []
    for group in MATRIX_GROUPS:
        buf = None
        for l, wi in enumerate(group):
            n = MATRICES[wi][0]
            buf = _sum_slots(slots[wi], partials[wi], kinds[wi], dims[wi], place, name=f"sum_{n}{l}", into=buf,
                             layer=l, n_layers=len(group))
            members.append((len(bufs), l if len(group) > 1 else None))
        bufs.append(buf)
    reduced = dict(zip(GROUP_NAMES, _pair_exchange(bufs, members)))

    small = jnp.concatenate([g["a_norm"], g["b_norm"], g["ffn_norm"], g["final_norm"],
                             jnp.pad(g["b_f"], ((0, 0), (0, D_MODEL - GATE_LANES))),
                             jnp.zeros((SMALL_ROWS - 6, D_MODEL), F32)], axis=0)
    small = _allreduce_small(small, name="allreduce_small")
    grads = dict(reduced)
    grads["a_norm"] = small[0:1]
    grads["b_norm"] = lax.dynamic_slice(small, (1, chip * bn_cols), (1, bn_cols))
    grads["ffn_norm"] = small[2:4]
    grads["final_norm"] = small[4]
    grads["b_f"] = small[5:6, :gate_cols]

    out_g, out_d, out_m, out_v = [], [], [], []
    for n in WEIGHT_ORDER:
        shape = given[n].shape
        two_d = (1, shape[0]) if len(shape) == 1 else (-1, shape[-1])
        d, m2, v2 = _adamw(given[n].reshape(two_d), grads[n].reshape(two_d), mom_m[n].reshape(two_d),
                           mom_v[n].reshape(two_d), name=f"adamw_{n}")
        out_g.append(grads[n].reshape(shape))
        out_d.append(d.reshape(shape))
        out_m.append(m2.reshape(shape))
        out_v.append(v2.reshape(shape))

    total = lax.psum(loss[0, 0], MESH_AXES)
    return (total, dx[None], *out_g, *out_d, *out_m, *out_v)
```

```python
import functools

import jax
import jax.numpy as jnp
from jax import lax
from jax.experimental import pallas as pl
from jax.experimental.pallas import tpu as pltpu

F32 = jnp.float32
BF = jnp.bfloat16

D_MODEL = 1024
N_HEADS = 16
HEAD_DIM = 64
D_FF = 2816
DILATED_PATTERNS = ((128, 1), (512, 4), (2048, 16))
ROT_DIM = 16
ROPE_THETA = 500000.0
RMS_EPS = 1e-6
NEG_INF = -1e30
ATTN_SCALE = HEAD_DIM ** -0.5
GATE_LANES = 128
N_CHIPS = 4
MESH_AXES = ("x", "y", "c")
MESH = pl.DeviceIdType.MESH

ADAM_LR = 0.001
ADAM_B1 = 0.9
ADAM_B2 = 0.999
ADAM_EPS = 1e-08
ADAM_WD = 0.01
ADAM_STEP = 10

VMEM_LIMIT_BYTES = 56 * 1024 * 1024


def _params(*sem):
    return pltpu.CompilerParams(dimension_semantics=sem, vmem_limit_bytes=VMEM_LIMIT_BYTES)


def _hosted_call(body, *, grid, in_specs, out_specs, out_shape, scratch_shapes, args, name, guest=None, schedule=()):
    params = _params(*(["arbitrary"] * len(grid)))
    ns = len(schedule)

    def call(kernel, in_specs, out_specs, out_shape, scratch_shapes, aliases, args):
        spec = pltpu.PrefetchScalarGridSpec(num_scalar_prefetch=ns, grid=grid, in_specs=in_specs, out_specs=out_specs,
                                            scratch_shapes=scratch_shapes)
        return pl.pallas_call(kernel, grid_spec=spec, out_shape=out_shape, input_output_aliases=aliases, name=name,
                              compiler_params=params)(*schedule, *args)

    if guest is None:
        return call(body, in_specs, out_specs, out_shape, scratch_shapes, {}, args)
    n_in, n_out, n_scr = ns + len(in_specs), len(out_specs), len(scratch_shapes)
    g_in, g_out = len(guest["args"]), len(guest["out_shape"])
    any_spec = pl.BlockSpec(memory_space=pl.ANY)

    def wrapped(*refs):
        i1 = n_in + g_in
        o1 = i1 + n_out
        o2 = o1 + g_out
        s1 = o2 + n_scr
        guest_refs = (refs[n_in:i1], refs[o1:o2], refs[s1:])
        ids = [pl.program_id(d) for d in range(len(grid))]
        first = functools.reduce(jnp.logical_and, [i == 0 for i in ids])
        last = functools.reduce(jnp.logical_and, [i == g - 1 for i, g in zip(ids, grid)])

        @pl.when(first)
        def _():
            guest["start"](*guest_refs)

        body(*refs[:n_in], *refs[i1:o1], *refs[o2:s1])

        @pl.when(last)
        def _():
            guest["finish"](*guest_refs)

    aliases = {n_in + k: n_out + k for k in range(g_in)} if guest.get("in_place") else {}
    return call(wrapped, list(in_specs) + [any_spec] * g_in, list(out_specs) + [any_spec] * g_out,
                list(out_shape) + list(guest["out_shape"]), list(scratch_shapes) + list(guest["scratch"]), aliases,
                list(args) + list(guest["args"]))


def _rope_rotate(t, cos, sin_a, sin_b):
    outs = []
    for cidx in range(t.shape[1] // 128):
        tc = t[:, cidx * 128:(cidx + 1) * 128]
        outs.append(tc * cos + pltpu.roll(tc, 120, 1) * sin_a + pltpu.roll(tc, 8, 1) * sin_b)
    return jnp.concatenate(outs, axis=1)


def _mm_nn(a, b, *, tm, tn, out_dtype, name, resid=None, rope=None, guest=None, groups=None):
    M, K = a.shape
    N = b.shape[1]
    assert M % tm == 0 and N % tn == 0 and b.shape[0] == K
    n_in = 2 + (resid is not None) + (3 if rope is not None else 0)
    if groups is not None:
        assert rope is not None and N == 3 * tn * len(groups)

    def body(*refs):
        a_ref, b_ref = refs[0], refs[1]
        o_ref = refs[n_in]
        acc = jnp.dot(a_ref[...], b_ref[...], preferred_element_type=F32)
        if resid is not None:
            acc = acc + refs[2][...]
        if groups is not None:
            cos_ref, sa_ref, sb_ref = refs[n_in - 3:n_in]
            j = pl.program_id(1)
            for g, d in enumerate(groups):
                for is_v in (False, True):
                    @pl.when(jnp.logical_and(j // 3 == g, (j % 3 == 2) == is_v))
                    def _(g=g, d=d, is_v=is_v):
                        val = acc if is_v else _rope_rotate(acc, cos_ref[...], sa_ref[...], sb_ref[...])
                        if d == 1:
                            refs[n_in + g][...] = val.astype(out_dtype)
                        else:
                            _to_view(val, refs[-1], refs[n_in + g], d, tn)
        elif rope is not None:
            cos_ref, sa_ref, sb_ref = refs[n_in - 3:n_in]
            j = pl.program_id(1)

            @pl.when(j % 3 != 2)
            def _():
                o_ref[...] = _rope_rotate(acc, cos_ref[...], sa_ref[...], sb_ref[...]).astype(out_dtype)

            @pl.when(j % 3 == 2)
            def _():
                o_ref[...] = acc.astype(out_dtype)
        else:
            o_ref[...] = acc.astype(out_dtype)

    in_specs = [pl.BlockSpec((tm, K), lambda i, j: (i, 0)), pl.BlockSpec((K, tn), lambda i, j: (0, j))]
    args = [a, b]
    if resid is not None:
        in_specs.append(pl.BlockSpec((tm, tn), lambda i, j: (i, j)))
        args.append(resid)
    if rope is not None:
        assert tn == 1024
        for t in rope:
            in_specs.append(pl.BlockSpec((tm, 128), lambda i, j: (i, 0)))
            args.append(t)
    if groups is None:
        out_specs = [pl.BlockSpec((tm, tn), lambda i, j: (i, j))]
        out_shape = [jax.ShapeDtypeStruct((M, N), out_dtype)]
        scratch = []
    else:
        out_specs = [pl.BlockSpec((tm // d, d * tn), lambda i, j, g=g: (i, jnp.clip(j - 3 * g, 0, 2)))
                     for g, d in enumerate(groups)]
        out_shape = [jax.ShapeDtypeStruct((M // d, d * 3 * tn), out_dtype) for d in groups]
        scratch = [pltpu.VMEM((tn // 128, tm, 128), F32)]
    outs = _hosted_call(body, grid=(M // tm, N // tn), in_specs=in_specs, out_specs=out_specs, out_shape=out_shape,
                        scratch_shapes=scratch, args=args, name=name, guest=guest)
    nout = len(out_shape)
    res = outs[0] if groups is None else list(outs[:nout])
    return res if guest is None else (res, outs[nout:])


def _mm_nt(a, b, *, tm, to, tn, out_dtype, name, add=None, guest=None, views=(1,)):
    M, N = a.shape
    O = b.shape[0]
    assert M % tm == 0 and O % to == 0 and N % tn == 0 and b.shape[1] == N
    nk = N // tn

    def body(*refs):
        a_ref, b_ref = refs[0], refs[1]
        n_in = 2 + (add is not None)
        o_refs = refs[n_in:n_in + len(views)]
        acc_ref = refs[n_in + len(views)]
        k = pl.program_id(2)

        @pl.when(k == 0)
        def _():
            if add is not None:
                acc_ref[...] = refs[2][...]
            else:
                acc_ref[...] = jnp.zeros_like(acc_ref)

        acc_ref[...] += lax.dot_general(a_ref[...], b_ref[...], (((1,), (1,)), ((), ())),
                                        preferred_element_type=F32)

        @pl.when(k == nk - 1)
        def _():
            for o_ref, d in zip(o_refs, views):
                if d == 1:
                    o_ref[...] = acc_ref[...].astype(out_dtype)
                else:
                    _to_view(acc_ref[...], refs[-1], o_ref, d, to)

    in_specs = [pl.BlockSpec((tm, tn), lambda i, j, k: (i, k)), pl.BlockSpec((to, tn), lambda i, j, k: (j, k))]
    args = [a, b]
    if add is not None:
        in_specs.append(pl.BlockSpec((tm, to), lambda i, j, k: (i, j)))
        args.append(add)
    assert views == (1,) or (to == O and to % 128 == 0)
    scratch = [pltpu.VMEM((tm, to), F32)] + ([pltpu.VMEM((to // 128, tm, 128), F32)] if views != (1,) else [])
    outs = _hosted_call(
        body, grid=(M // tm, O // to, nk), in_specs=in_specs,
        out_specs=[pl.BlockSpec((tm, to), lambda i, j, k: (i, j)) if d == 1 else _view_spec(tm, d, to) for d in views],
        out_shape=[jax.ShapeDtypeStruct((M // d, d * O), out_dtype) for d in views],
        scratch_shapes=scratch, args=args, name=name, guest=guest)
    nv = len(views)
    res = outs[0] if nv == 1 else list(outs[:nv])
    return res if guest is None else (res, outs[nv:])


def _mm_tn(a, b, *, tk, tn, tm, out_dtype, name):
    M, K = a.shape
    N = b.shape[1]
    assert M % tm == 0 and K % tk == 0 and N % tn == 0 and b.shape[0] == M
    nm = M // tm

    def body(a_ref, b_ref, o_ref, acc_ref):
        m = pl.program_id(2)

        @pl.when(m == 0)
        def _():
            acc_ref[...] = jnp.zeros_like(acc_ref)

        acc_ref[...] += lax.dot_general(a_ref[...], b_ref[...], (((0,), (0,)), ((), ())),
                                        preferred_element_type=F32)

        @pl.when(m == nm - 1)
        def _():
            o_ref[...] = acc_ref[...].astype(out_dtype)

    return pl.pallas_call(
        body, grid=(K // tk, N // tn, nm),
        in_specs=[pl.BlockSpec((tm, tk), lambda i, j, m: (m, i)), pl.BlockSpec((tm, tn), lambda i, j, m: (m, j))],
        out_specs=pl.BlockSpec((tk, tn), lambda i, j, m: (i, j)),
        out_shape=jax.ShapeDtypeStruct((K, N), out_dtype),
        scratch_shapes=[pltpu.VMEM((tk, tn), F32)], name=name,
        compiler_params=_params("parallel", "parallel", "arbitrary"),
    )(a, b)


ROW_TILE = 512
MM_ROWS = 1024


def _rms_fwd(x, g, *, name):
    S, Dm = x.shape

    def body(x_ref, g_ref, o_ref):
        xf = x_ref[...]
        r = lax.rsqrt(jnp.mean(xf * xf, axis=-1, keepdims=True) + RMS_EPS)
        o_ref[...] = (xf * r * g_ref[...]).astype(BF)

    return pl.pallas_call(
        body, grid=(S // ROW_TILE,),
        in_specs=[pl.BlockSpec((ROW_TILE, Dm), lambda i: (i, 0)), pl.BlockSpec((1, Dm), lambda i: (0, 0))],
        out_specs=pl.BlockSpec((ROW_TILE, Dm), lambda i: (i, 0)),
        out_shape=jax.ShapeDtypeStruct((S, Dm), BF), name=name, compiler_params=_params("parallel"),
    )(x, g)


def _rms_bwd(x, g, dn, dres, *, name):
    S, Dm = x.shape

    def body(x_ref, g_ref, dn_ref, dres_ref, dx_ref, dxb_ref, dg_ref):
        i = pl.program_id(0)
        xf = x_ref[...]
        r = lax.rsqrt(jnp.mean(xf * xf, axis=-1, keepdims=True) + RMS_EPS)
        xh = xf * r
        dnf = dn_ref[...]
        dyg = dnf * g_ref[...]
        dx = dres_ref[...] + r * (dyg - xh * jnp.mean(dyg * xh, axis=-1, keepdims=True))
        dx_ref[...] = dx
        dxb_ref[...] = dx.astype(BF)

        @pl.when(i == 0)
        def _():
            dg_ref[...] = jnp.zeros_like(dg_ref)

        dg_ref[...] += jnp.sum(dnf * xh, axis=0, keepdims=True)

    row = pl.BlockSpec((ROW_TILE, Dm), lambda i: (i, 0))
    vec = pl.BlockSpec((1, Dm), lambda i: (0, 0))
    return pl.pallas_call(
        body, grid=(S // ROW_TILE,), in_specs=[row, vec, row, row], out_specs=[row, row, vec],
        out_shape=[jax.ShapeDtypeStruct((S, Dm), F32), jax.ShapeDtypeStruct((S, Dm), BF),
                   jax.ShapeDtypeStruct((1, Dm), F32)],
        name=name, compiler_params=_params("arbitrary"),
    )(x, g, dn, dres)


def _loss_head(h, g, tgt, *, name):
    S, Dm = h.shape

    def body(h_ref, g_ref, t_ref, loss_ref, dh_ref, dhb_ref, dg_ref):
        i = pl.program_id(0)
        xf = h_ref[...]
        r = lax.rsqrt(jnp.mean(xf * xf, axis=-1, keepdims=True) + RMS_EPS)
        xh = xf * r
        gv = g_ref[...]
        err = xh * gv - t_ref[...]
        dy = err * (1.0 / Dm)
        dyg = dy * gv
        dh = r * (dyg - xh * jnp.mean(dyg * xh, axis=-1, keepdims=True))
        dh_ref[...] = dh
        dhb_ref[...] = dh.astype(BF)

        @pl.when(i == 0)
        def _():
            dg_ref[...] = jnp.zeros_like(dg_ref)
            loss_ref[...] = jnp.zeros_like(loss_ref)

        dg_ref[...] += jnp.sum(dy * xh, axis=0, keepdims=True)
        part = 0.5 * jnp.sum(jnp.mean(err * err, axis=-1, keepdims=True), axis=0, keepdims=True)
        loss_ref[...] += jnp.broadcast_to(part, loss_ref.shape)

    row = pl.BlockSpec((ROW_TILE, Dm), lambda i: (i, 0))
    vec = pl.BlockSpec((1, Dm), lambda i: (0, 0))
    return pl.pallas_call(
        body, grid=(S // ROW_TILE,), in_specs=[row, vec, row],
        out_specs=[pl.BlockSpec((1, 128), lambda i: (0, 0)), row, row, vec],
        out_shape=[jax.ShapeDtypeStruct((1, 128), F32), jax.ShapeDtypeStruct((S, Dm), F32),
                   jax.ShapeDtypeStruct((S, Dm), BF), jax.ShapeDtypeStruct((1, Dm), F32)],
        name=name, compiler_params=_params("arbitrary"),
    )(h, g, tgt)


SWIGLU_ROWS = 512


def _swiglu_fwd(gu, *, name):
    S = gu.shape[0]

    def body(g_ref, u_ref, o_ref):
        g = g_ref[...].astype(F32)
        sig = 1.0 / (1.0 + jnp.exp(-g))
        o_ref[...] = (g * sig * u_ref[...].astype(F32)).astype(BF)

    return pl.pallas_call(
        body, grid=(S // SWIGLU_ROWS,),
        in_specs=[pl.BlockSpec((SWIGLU_ROWS, D_FF), lambda i: (i, 0)), pl.BlockSpec((SWIGLU_ROWS, D_FF), lambda i: (i, 1))],
        out_specs=pl.BlockSpec((SWIGLU_ROWS, D_FF), lambda i: (i, 0)),
        out_shape=jax.ShapeDtypeStruct((S, D_FF), BF), name=name, compiler_params=_params("parallel"),
    )(gu, gu)


def _swiglu_bwd(gu, dact, *, name):
    S = gu.shape[0]

    def body(g_ref, u_ref, d_ref, o_ref):
        g = g_ref[...].astype(F32)
        u = u_ref[...].astype(F32)
        d = d_ref[...].astype(F32)
        sig = 1.0 / (1.0 + jnp.exp(-g))
        o_ref[:, :D_FF] = (d * u * sig * (1.0 + g * (1.0 - sig))).astype(BF)
        o_ref[:, D_FF:] = (d * g * sig).astype(BF)

    return pl.pallas_call(
        body, grid=(S // SWIGLU_ROWS,),
        in_specs=[pl.BlockSpec((SWIGLU_ROWS, D_FF), lambda i: (i, 0)), pl.BlockSpec((SWIGLU_ROWS, D_FF), lambda i: (i, 1)),
                  pl.BlockSpec((SWIGLU_ROWS, D_FF), lambda i: (i, 0))],
        out_specs=pl.BlockSpec((SWIGLU_ROWS, 2 * D_FF), lambda i: (i, 0)),
        out_shape=jax.ShapeDtypeStruct((S, 2 * D_FF), BF), name=name, compiler_params=_params("parallel"),
    )(gu, gu, dact)


def _band_masks(T, n):
    row = lax.broadcasted_iota(jnp.int32, (T, T), 0)
    col = lax.broadcasted_iota(jnp.int32, (T, T), 1)
    return jnp.logical_and(col >= row, n > 0), col <= row


def _band_fwd(qa, ka, va, qcb, kcb, vcb, *, dil, T, window, name, guest=None):
    L = qa.shape[0]
    nq = L // T
    assert window == T
    nt = (((1,), (1,)), ((), ()))

    def body(q_ref, kp_ref, kc_ref, vp_ref, vc_ref, o_ref, lse_ref):
        valid = jnp.concatenate(_band_masks(T, pl.program_id(1)), axis=1)
        lane = lax.broadcasted_iota(jnp.int32, (T, 128), 1)
        low = lane < HEAD_DIM
        ones = jnp.ones((2 * T, 128), BF)
        lse = jnp.zeros((T, 128), F32)
        def scores(h):
            ps = slice((h // 2) * 128, (h // 2 + 1) * 128)
            qp = q_ref[:, ps] * jnp.asarray(ATTN_SCALE, BF)
            qm = jnp.where(low if h % 2 == 0 else jnp.logical_not(low), qp, jnp.zeros_like(qp))
            keys = jnp.concatenate([kp_ref[:, ps], kc_ref[:, ps]], axis=0)
            return jnp.where(valid, lax.dot_general(qm, keys, nt, preferred_element_type=F32), NEG_INF)

        def softmax(s):
            m = jnp.max(s, axis=1, keepdims=True)
            return m, jnp.exp(s - m).astype(BF)

        def weighted(h, p):
            ps = slice((h // 2) * 128, (h // 2 + 1) * 128)
            values = jnp.concatenate([jnp.concatenate([vp_ref[:, ps], vc_ref[:, ps]], axis=0), ones], axis=1)
            pv = jnp.dot(p, values, preferred_element_type=F32)
            return pv[:, 128:], pv[:, :128]

        sc, pr, even = {}, {}, None
        for t in range(N_HEADS + 2):
            if t < N_HEADS:
                sc[t] = scores(t)
            done = None
            if t >= 2:
                m, p = pr.pop(t - 2)
                done = (m,) + weighted(t - 2, p)
            if 1 <= t <= N_HEADS:
                pr[t - 1] = softmax(sc.pop(t - 1))
            if done is not None:
                h = t - 2
                m, l, acc = done
                lse = jnp.where(lane == h, m + jnp.log(l), lse)
                if h % 2 == 0:
                    even = acc / l
                else:
                    o_ref[:, (h // 2) * 128:(h // 2 + 1) * 128] = jnp.where(low, even, acc / l)
        lse_ref[...] = lse

    def prev(n):
        return jnp.maximum(n - 1, 0)

    blk = lambda f, cb: pl.BlockSpec((T, 1024), lambda r, n: (f(n), cb(r)))
    same = lambda n: n
    outs = _hosted_call(
        body, grid=(dil, nq),
        in_specs=[blk(same, qcb), blk(prev, kcb), blk(same, kcb), blk(prev, vcb), blk(same, vcb)],
        out_specs=[pl.BlockSpec((T, 1024), lambda r, n: (n, r)), pl.BlockSpec((T, 128), lambda r, n: (n, r))],
        out_shape=[jax.ShapeDtypeStruct((L, dil * 1024), F32), jax.ShapeDtypeStruct((L, dil * 128), F32)],
        scratch_shapes=[], args=(qa, ka, ka, va, va), name=name, guest=guest)
    return outs if guest is None else (outs[:2], outs[2:])


def _band_bwd(qa, ka, va, qcb, kcb, vcb, doa, oa, lsea, *, dil, T, window, name, guest=None):
    L = qa.shape[0]
    nq = L // T
    assert window == T
    nt = (((1,), (1,)), ((), ()))
    tn = (((0,), (0,)), ((), ()))

    def body(q_ref, kp_ref, kc_ref, vp_ref, vc_ref, do_ref, o_ref, lse_ref, dq_ref, dk_ref, dv_ref, ck_sc, cv_sc):
        n = pl.program_id(1)

        @pl.when(n == 0)
        def _():
            ck_sc[...] = jnp.zeros_like(ck_sc)
            cv_sc[...] = jnp.zeros_like(cv_sc)

        @pl.when(n < nq)
        def _():
            valid = jnp.concatenate(_band_masks(T, n), axis=1)
            low = lax.broadcasted_iota(jnp.int32, (T, 128), 1) < HEAD_DIM
            dot = functools.partial(lax.dot_general, preferred_element_type=F32)

            def pair(h):
                return slice((h // 2) * 128, (h // 2 + 1) * 128)

            def products(h):
                ps = pair(h)
                mask = low if h % 2 == 0 else jnp.logical_not(low)
                qp = q_ref[:, ps] * jnp.asarray(ATTN_SCALE, BF)
                dop = do_ref[:, ps]
                qm = jnp.where(mask, qp, jnp.zeros_like(qp))
                dom = jnp.where(mask, dop, jnp.zeros_like(dop))
                keys = jnp.concatenate([kp_ref[:, ps], kc_ref[:, ps]], axis=0)
                values = jnp.concatenate([vp_ref[:, ps], vc_ref[:, ps]], axis=0)
                return qm, dom, jnp.where(valid, dot(qm, keys, nt), NEG_INF), dot(dom, values, nt)

            def pointwise(h, qm, dom, s, dp):
                ps = pair(h)
                mask = low if h % 2 == 0 else jnp.logical_not(low)
                prod = do_ref[:, ps].astype(F32) * o_ref[:, ps].astype(F32)
                delta = jnp.sum(jnp.where(mask, prod, 0.0), axis=1, keepdims=True)
                p = jnp.exp(s - lse_ref[:, h:h + 1])
                ds = (p * (dp - delta)).astype(BF)
                return qm, dom, p.astype(BF), ds

            def gradients(h, qm, dom, p, ds):
                ps = pair(h)
                keys = jnp.concatenate([kp_ref[:, ps], kc_ref[:, ps]], axis=0)
                dq = dot(ds, keys, (((1,), (0,)), ((), ())))
                dk, dv = dot(ds, qm, tn), dot(p, dom, tn)
                return dq, dk[:T], dv[:T], dk[T:], dv[T:]

            st1, st2, even = {}, {}, None
            for t in range(N_HEADS + 2):
                if t < N_HEADS:
                    st1[t] = products(t)
                done = gradients(t - 2, *st2.pop(t - 2)) if t >= 2 else None
                if 1 <= t <= N_HEADS:
                    st2[t - 1] = pointwise(t - 1, *st1.pop(t - 1))
                if done is not None:
                    h = t - 2
                    if h % 2 == 0:
                        even = done
                    else:
                        ps = pair(h)
                        dq_ref[:, ps] = (jnp.where(low, even[0], done[0]) * ATTN_SCALE).astype(BF)
                        dk_ref[:, ps] = (ck_sc[:, ps] + even[1] + done[1]).astype(BF)
                        dv_ref[:, ps] = (cv_sc[:, ps] + even[2] + done[2]).astype(BF)
                        ck_sc[:, ps] = even[3] + done[3]
                        cv_sc[:, ps] = even[4] + done[4]

        @pl.when(n == nq)
        def _():
            dk_ref[...] = ck_sc[...].astype(BF)
            dv_ref[...] = cv_sc[...].astype(BF)

    def cur(n):
        return jnp.minimum(n, nq - 1)

    def prev(n):
        return jnp.maximum(cur(n) - 1, 0)

    blk = lambda f, cb: pl.BlockSpec((T, 1024), lambda r, n: (f(n), cb(r)))
    own = lambda r: r
    outs = _hosted_call(
        body, grid=(dil, nq + 1),
        in_specs=[blk(cur, qcb), blk(prev, kcb), blk(cur, kcb), blk(prev, vcb), blk(cur, vcb), blk(cur, own), blk(cur, own),
                  pl.BlockSpec((T, 128), lambda r, n: (cur(n), r))],
        out_specs=[blk(cur, own), blk(lambda n: jnp.maximum(n - 1, 0), own), blk(lambda n: jnp.maximum(n - 1, 0), own)],
        out_shape=[jax.ShapeDtypeStruct((L, dil * 1024), BF)] * 3,
        scratch_shapes=[pltpu.VMEM((T, 1024), F32), pltpu.VMEM((T, 1024), F32)],
        args=(qa, ka, ka, va, va, doa, oa, lsea), name=name, guest=guest)
    return outs if guest is None else (outs[:3], outs[3:])


FOX_T = 512
FOX_TQ = 512
FOX_TK = 512
FOX_TQ_BWD = 512
FOX_ROWS = 256


def _fox_fwd(qkv, cT, *, name, guest=None):
    S = qkv.shape[0]
    T, TK, R = FOX_TQ, FOX_TK, FOX_ROWS
    nq = S // T
    nt = (((1,), (1,)), ((), ()))
    chains = [(h, rh) for h in range(N_HEADS) for rh in range(T // R)]
    pairs = [(n, j) for n in range(nq) for j in range((n * T + T - 1) // TK + 1)]
    schedule = [jnp.asarray([p[i] for p in pairs], jnp.int32) for i in range(2)]

    def body(n_tab, j_tab, q_ref, k_ref, v_ref, ct_ref, o_ref, lse_ref, m_sc, l_sc, acc_sc):
        n = n_tab[pl.program_id(0)]
        j = j_tab[pl.program_id(0)]
        last_j = (n * T + T - 1) // TK
        lane = lax.broadcasted_iota(jnp.int32, (R, 128), 1)
        low = lane < HEAD_DIM
        ones = jnp.ones((TK, 128), BF)

        @pl.when(j == 0)
        def _():
            m_sc[...] = jnp.full(m_sc.shape, NEG_INF, F32)
            l_sc[...] = jnp.zeros_like(l_sc)
            acc_sc[...] = jnp.zeros_like(acc_sc)

        def step(diagonal):
            def pair(h):
                return slice((h // 2) * 128, (h // 2 + 1) * 128)

            def rows(rh):
                return slice(rh * R, (rh + 1) * R)

            def scores(h, rh):
                qp = q_ref[rows(rh), pair(h)] * jnp.asarray(ATTN_SCALE, BF)
                qm = jnp.where(low if h % 2 == 0 else jnp.logical_not(low), qp, jnp.zeros_like(qp))
                s = lax.dot_general(qm, k_ref[:, pair(h)], nt, preferred_element_type=F32) - ct_ref[h:h + 1, :]
                if diagonal:
                    ahead = lax.broadcasted_iota(jnp.int32, (R, TK), 1) - lax.broadcasted_iota(jnp.int32, (R, TK), 0)
                    s = jnp.where(ahead <= n * T + rh * R - j * TK, s, NEG_INF)
                return s

            def softmax(h, rh, s):
                m_prev = m_sc[h, rows(rh), :]
                m_new = jnp.maximum(m_prev, jnp.max(s, axis=1, keepdims=True))
                p = jnp.exp(s - jnp.concatenate([m_new] * (TK // 128), axis=1)).astype(BF)
                return m_new, jnp.exp(m_prev - m_new), p

            def weighted(h, p):
                vx = jnp.concatenate([v_ref[:, pair(h)], ones], axis=1)
                return jnp.dot(p, vx, preferred_element_type=F32)

            sc, pr, even = {}, {}, {}
            nch = len(chains)
            for t in range(nch + 2):
                if t < nch:
                    sc[t] = scores(*chains[t])
                done = None
                if t >= 2:
                    m_new, alpha, p = pr.pop(t - 2)
                    done = (m_new, alpha, weighted(chains[t - 2][0], p))
                if 1 <= t <= nch:
                    pr[t - 1] = softmax(*chains[t - 1], sc.pop(t - 1))
                if done is not None:
                    h, rh = chains[t - 2]
                    m_new, alpha, pv = done
                    m_sc[h, rows(rh), :] = m_new
                    l_sc[h, rows(rh), :] = alpha * l_sc[h, rows(rh), :] + pv[:, 128:]
                    if h % 2 == 0:
                        even[rh] = (alpha, pv[:, :128])
                    else:
                        a0, pv0 = even.pop(rh)
                        acc = acc_sc[h // 2, rows(rh), :]
                        acc_sc[h // 2, rows(rh), :] = jnp.where(low, a0 * acc + pv0, alpha * acc + pv[:, :128])

        @pl.when(j < last_j)
        def _():
            step(False)

        @pl.when(j == last_j)
        def _():
            step(True)
            lane_t = lax.broadcasted_iota(jnp.int32, (T, 128), 1)
            low_t = lane_t < HEAD_DIM
            lse = jnp.zeros((T, 128), F32)
            for h in range(N_HEADS):
                lse = jnp.where(lane_t == h, m_sc[h] + jnp.log(l_sc[h]), lse)
            lse_ref[...] = lse
            for hp in range(N_HEADS // 2):
                inv = jnp.where(low_t, 1.0 / l_sc[2 * hp], 1.0 / l_sc[2 * hp + 1])
                o_ref[:, hp * 128:(hp + 1) * 128] = (acc_sc[hp] * inv).astype(BF)

    outs = _hosted_call(
        body, grid=(len(pairs),),
        in_specs=[pl.BlockSpec((T, 1024), lambda t, n, j: (n[t], 0)), pl.BlockSpec((TK, 1024), lambda t, n, j: (j[t], 1)),
                  pl.BlockSpec((TK, 1024), lambda t, n, j: (j[t], 2)), pl.BlockSpec((GATE_LANES, TK), lambda t, n, j: (0, j[t]))],
        out_specs=[pl.BlockSpec((T, 1024), lambda t, n, j: (n[t], 0)), pl.BlockSpec((T, 128), lambda t, n, j: (n[t], 0))],
        out_shape=[jax.ShapeDtypeStruct((S, 1024), BF), jax.ShapeDtypeStruct((S, 128), F32)],
        scratch_shapes=[pltpu.VMEM((N_HEADS, T, 128), F32), pltpu.VMEM((N_HEADS, T, 128), F32),
                        pltpu.VMEM((N_HEADS // 2, T, 128), F32)],
        args=(qkv, qkv, qkv, cT), name=name, guest=guest, schedule=schedule)
    return outs if guest is None else (outs[:2], outs[2:])


def _fox_bwd(qkv, cT, do, o, lse, *, name, guest=None):
    S = qkv.shape[0]
    T, TQ, R = FOX_T, FOX_TQ_BWD, FOX_ROWS
    nk, nq = S // T, S // TQ
    nt = (((1,), (1,)), ((), ()))
    tn = (((0,), (0,)), ((), ()))
    nn = (((1,), (0,)), ((), ()))
    chains = [(h, rh) for h in range(N_HEADS) for rh in range(TQ // R)]
    dot = functools.partial(lax.dot_general, preferred_element_type=F32)
    pairs = [(kb, qb) for kb in range(nk) for qb in range(kb * T // TQ, nq)]
    schedule = [jnp.asarray([p[i] for p in pairs], jnp.int32) for i in range(2)]

    def body(kb_tab, qb_tab, q_ref, k_ref, v_ref, ct_ref, do_ref, o_ref, lse_ref, dq_ref, dk_ref, dv_ref, dct_ref, dcq_ref,
             dq_sc, dk_sc, dv_sc, dc_sc, dcq_sc):
        kb = kb_tab[pl.program_id(0)]
        qb = qb_tab[pl.program_id(0)]
        jq = qb - kb * T // TQ
        lane = lax.broadcasted_iota(jnp.int32, (R, 128), 1)
        low = lane < HEAD_DIM
        ones_k = jnp.ones((T, 128), BF)
        ones_r = jnp.ones((8, R), BF)

        @pl.when(jnp.logical_and(kb == 0, jq == 0))
        def _():
            dq_sc[...] = jnp.zeros_like(dq_sc)
            dcq_sc[...] = jnp.zeros_like(dcq_sc)

        @pl.when(jq == 0)
        def _():
            dk_sc[...] = jnp.zeros_like(dk_sc)
            dv_sc[...] = jnp.zeros_like(dv_sc)
            dc_sc[...] = jnp.zeros_like(dc_sc)

        def step(diagonal):
            def pair(h):
                return slice((h // 2) * 128, (h // 2 + 1) * 128)

            def rows(rh):
                return slice(rh * R, (rh + 1) * R)

            def qrows(rh):
                return pl.ds(pl.multiple_of(qb * TQ + rh * R, R), R)

            def products(h, rh):
                mask = low if h % 2 == 0 else jnp.logical_not(low)
                qp = q_ref[rows(rh), pair(h)] * jnp.asarray(ATTN_SCALE, BF)
                dop = do_ref[rows(rh), pair(h)]
                qm = jnp.where(mask, qp, jnp.zeros_like(qp))
                dom = jnp.where(mask, dop, jnp.zeros_like(dop))
                s = dot(qm, k_ref[:, pair(h)], nt) - ct_ref[h:h + 1, :]
                if diagonal:
                    ahead = lax.broadcasted_iota(jnp.int32, (R, T), 1) - lax.broadcasted_iota(jnp.int32, (R, T), 0)
                    s = jnp.where(ahead <= qb * TQ + rh * R - kb * T, s, NEG_INF)
                return qm, dom, s, dot(dom, v_ref[:, pair(h)], nt)

            def pointwise(h, rh, qm, dom, s, dp):
                mask = low if h % 2 == 0 else jnp.logical_not(low)
                prod = do_ref[rows(rh), pair(h)].astype(F32) * o_ref[rows(rh), pair(h)].astype(F32)
                delta = jnp.sum(jnp.where(mask, prod, 0.0), axis=1, keepdims=True)
                p = jnp.exp(s - lse_ref[rows(rh), h:h + 1])
                ds = (p * (dp - delta)).astype(BF)
                return qm, dom, p.astype(BF), ds

            def gradients(h, qm, dom, p, ds):
                kx = jnp.concatenate([k_ref[:, pair(h)], ones_k], axis=1)
                return dot(ds, kx, nn), dot(qm, ds, tn), dot(dom, p, tn), dot(ones_r, ds, nn)

            st1, st2, even = {}, {}, {}
            dcq_tiles = [jnp.zeros((R, 128), F32) for _ in range(TQ // R)]
            nch = len(chains)
            for t in range(nch + 2):
                if t < nch:
                    st1[t] = products(*chains[t])
                done = gradients(chains[t - 2][0], *st2.pop(t - 2)) if t >= 2 else None
                if 1 <= t <= nch:
                    st2[t - 1] = pointwise(*chains[t - 1], *st1.pop(t - 1))
                if done is not None:
                    h, rh = chains[t - 2]
                    dq_rsum, dk, dv, csum = done
                    dq = dq_rsum[:, :128]
                    dcq_tiles[rh] = jnp.where(lane == h, dq_rsum[:, 128:], dcq_tiles[rh])
                    dc_sc[h:h + 1, :] -= csum[0:1, :]
                    if h % 2 == 0:
                        even[rh] = (dq, dk, dv)
                    else:
                        dq0, dk0, dv0 = even.pop(rh)
                        dq_sc[qrows(rh), pair(h)] += jnp.where(low, dq0, dq) * ATTN_SCALE
                        dk_sc[h // 2] += dk0 + dk
                        dv_sc[h // 2] += dv0 + dv
            for rh in range(TQ // R):
                dcq_sc[qrows(rh), :] += dcq_tiles[rh]

        @pl.when(jq > 0)
        def _():
            step(False)

        @pl.when(jq == 0)
        def _():
            step(True)

        @pl.when(qb == nq - 1)
        def _():
            for hp in range(N_HEADS // 2):
                dk_ref[:, hp * 128:(hp + 1) * 128] = dk_sc[hp].T.astype(BF)
                dv_ref[:, hp * 128:(hp + 1) * 128] = dv_sc[hp].T.astype(BF)
            dct_ref[...] = dc_sc[...]

        @pl.when(jnp.logical_and(kb == nk - 1, qb == nq - 1))
        def _():
            def put(i, carry):
                r = pl.ds(pl.multiple_of(i * T, T), T)
                dq_ref[r, :] = dq_sc[r, :].astype(BF)
                return carry
            lax.fori_loop(0, nk, put, 0)
            dcq_ref[...] = dcq_sc[...]

    qblk = lambda col: pl.BlockSpec((TQ, 1024), lambda t, kb, qb: (qb[t], col))
    kblk = lambda col: pl.BlockSpec((T, 1024), lambda t, kb, qb: (kb[t], col))
    whole = pl.BlockSpec((S, 1024), lambda t, kb, qb: (0, 0))
    outs = _hosted_call(
        body, grid=(len(pairs),),
        in_specs=[qblk(0), kblk(1), kblk(2), pl.BlockSpec((GATE_LANES, T), lambda t, kb, qb: (0, kb[t])), qblk(0), qblk(0),
                  pl.BlockSpec((TQ, 128), lambda t, kb, qb: (qb[t], 0))],
        out_specs=[whole, kblk(0), kblk(0), pl.BlockSpec((GATE_LANES, T), lambda t, kb, qb: (0, kb[t])),
                   pl.BlockSpec((S, GATE_LANES), lambda t, kb, qb: (0, 0))],
        out_shape=[jax.ShapeDtypeStruct((S, 1024), BF)] * 3 + [jax.ShapeDtypeStruct((GATE_LANES, S), F32),
                                                               jax.ShapeDtypeStruct((S, GATE_LANES), F32)],
        scratch_shapes=[pltpu.VMEM((S, 1024), F32), pltpu.VMEM((N_HEADS // 2, 128, T), F32), pltpu.VMEM((N_HEADS // 2, 128, T), F32),
                        pltpu.VMEM((GATE_LANES, T), F32), pltpu.VMEM((S, GATE_LANES), F32)],
        args=(qkv, qkv, qkv, cT, do, o, lse), name=name, guest=guest, schedule=schedule)
    return outs if guest is None else (outs[:5], outs[5:])


def _to_natural(src_ref, buf, d, width):
    rows = buf.shape[1]
    for r in range(d):
        for ch in range(width // 128):
            lanes = slice(r * width + ch * 128, r * width + (ch + 1) * 128)
            buf.at[ch][pl.ds(r, rows // d, stride=d), :] = src_ref[:, lanes].astype(F32)
    return jnp.concatenate([buf[ch] for ch in range(width // 128)], axis=1)


def _to_view(val, buf, dst_ref, d, width):
    rows = buf.shape[1]
    for ch in range(width // 128):
        buf[ch] = val[:, ch * 128:(ch + 1) * 128]
    for r in range(d):
        for ch in range(width // 128):
            lanes = slice(r * width + ch * 128, r * width + (ch + 1) * 128)
            dst_ref[:, lanes] = buf.at[ch][pl.ds(r, rows // d, stride=d), :].astype(dst_ref.dtype)


def _view_spec(rows, d, width):
    return pl.BlockSpec((rows // d, d * width), lambda i, *_: (i, 0))


def _combine_groups(os, lses, dils, *, name):
    ng = len(os)
    S = os[0].shape[0] * dils[0]
    tm = ROW_TILE
    views = sorted(set(dils))

    def body(*refs):
        o_refs, l_refs = refs[:ng], refs[ng:2 * ng]
        outs = refs[2 * ng:2 * ng + 2 * len(views)]
        wide, narrow = refs[-2], refs[-1]
        ls = [l_refs[g][...] if dils[g] == 1 else _to_natural(l_refs[g], narrow, dils[g], 128) for g in range(ng)]
        m = functools.reduce(jnp.maximum, ls)
        es = [jnp.exp(l - m) for l in ls]
        den = functools.reduce(jnp.add, es)
        ws = [e / den for e in es]
        lse = m + jnp.log(den)
        og = [o_refs[g][...] if dils[g] == 1 else _to_natural(o_refs[g], wide, dils[g], 1024) for g in range(ng)]
        cols = []
        for h in range(N_HEADS):
            hs = slice(h * HEAD_DIM, (h + 1) * HEAD_DIM)
            acc = ws[0][:, h:h + 1] * og[0][:, hs]
            for g in range(1, ng):
                acc = acc + ws[g][:, h:h + 1] * og[g][:, hs]
            cols.append(acc)
        o = jnp.concatenate(cols, axis=1)
        for k, d in enumerate(views):
            if d == 1:
                outs[2 * k][...] = o.astype(BF)
                outs[2 * k + 1][...] = lse
            else:
                _to_view(o, wide, outs[2 * k], d, 1024)
                _to_view(lse, narrow, outs[2 * k + 1], d, 128)

    out_specs, out_shape = [], []
    for d in views:
        out_specs += [_view_spec(tm, d, 1024), _view_spec(tm, d, 128)]
        out_shape += [jax.ShapeDtypeStruct((S // d, d * 1024), BF), jax.ShapeDtypeStruct((S // d, d * 128), F32)]
    res = pl.pallas_call(
        body, grid=(S // tm,), in_specs=[_view_spec(tm, d, 1024) for d in dils] + [_view_spec(tm, d, 128) for d in dils],
        out_specs=out_specs, out_shape=out_shape,
        scratch_shapes=[pltpu.VMEM((8, tm, 128), F32), pltpu.VMEM((1, tm, 128), F32)],
        name=name, compiler_params=_params("parallel"),
    )(*os, *lses)
    return {d: (res[2 * k], res[2 * k + 1]) for k, d in enumerate(views)}


def _assemble(parts, rope_flags, rope, dils, *, name):
    n = len(parts)
    S = parts[0].shape[0] * dils[0]
    use_rope = any(rope_flags)
    tm = ROW_TILE

    def body(*refs):
        out_ref, natural = refs[-2], refs[-1]
        for b in range(n):
            cols = slice(b * 1024, (b + 1) * 1024)
            d = dils[b]
            val = refs[b][...].astype(F32) if d == 1 else _to_natural(refs[b], natural, d, 1024)
            if rope_flags[b]:
                cos_ref, sa_ref, sb_ref = refs[n:n + 3]
                val = _rope_rotate(val, cos_ref[...], sa_ref[...], sb_ref[...])
            out_ref[:, cols] = val.astype(BF)

    in_specs = [_view_spec(tm, d, 1024) for d in dils]
    args = list(parts)
    if use_rope:
        in_specs += [pl.BlockSpec((tm, 128), lambda i: (i, 0))] * 3
        args += list(rope)
    return pl.pallas_call(
        body, grid=(S // tm,), in_specs=in_specs, out_specs=pl.BlockSpec((tm, n * 1024), lambda i: (i, 0)),
        out_shape=jax.ShapeDtypeStruct((S, n * 1024), BF), scratch_shapes=[pltpu.VMEM((8, tm, 128), F32)],
        name=name, compiler_params=_params("parallel"),
    )(*args)


GATE_ROWS = 512


def _gate_fwd(z, bf, *, name):
    S = z.shape[0]

    def body(z_ref, b_ref, ct_ref, carry):
        i = pl.program_id(0)

        @pl.when(i == 0)
        def _():
            carry[...] = jnp.zeros_like(carry)

        zz = z_ref[...] + b_ref[...]
        logf = jnp.minimum(zz, 0.0) - jnp.log(1.0 + jnp.exp(-jnp.abs(zz)))
        tri = (lax.broadcasted_iota(jnp.int32, (GATE_ROWS, GATE_ROWS), 0)
               >= lax.broadcasted_iota(jnp.int32, (GATE_ROWS, GATE_ROWS), 1)).astype(F32)
        cs = jnp.dot(tri, logf, precision=lax.Precision.HIGHEST, preferred_element_type=F32) + carry[...]
        ct_ref[...] = cs.T
        carry[...] = cs[GATE_ROWS - 1:GATE_ROWS, :]

    return pl.pallas_call(
        body, grid=(S // GATE_ROWS,),
        in_specs=[pl.BlockSpec((GATE_ROWS, GATE_LANES), lambda i: (i, 0)), pl.BlockSpec((1, GATE_LANES), lambda i: (0, 0))],
        out_specs=pl.BlockSpec((GATE_LANES, GATE_ROWS), lambda i: (0, i)),
        out_shape=jax.ShapeDtypeStruct((GATE_LANES, S), F32),
        scratch_shapes=[pltpu.VMEM((1, GATE_LANES), F32)], name=name, compiler_params=_params("arbitrary"),
    )(z, bf)


def _gate_bwd(z, bf, dcT, dcq, *, name):
    S = z.shape[0]
    nb = S // GATE_ROWS

    def body(z_ref, b_ref, dct_ref, dcq_ref, dz_ref, db_ref, carry):
        i = pl.program_id(0)

        @pl.when(i == 0)
        def _():
            carry[...] = jnp.zeros_like(carry)
            db_ref[...] = jnp.zeros_like(db_ref)

        dc = dct_ref[...].T + dcq_ref[...]
        tri = (lax.broadcasted_iota(jnp.int32, (GATE_ROWS, GATE_ROWS), 0)
               <= lax.broadcasted_iota(jnp.int32, (GATE_ROWS, GATE_ROWS), 1)).astype(F32)
        dl = jnp.dot(tri, dc, precision=lax.Precision.HIGHEST, preferred_element_type=F32) + carry[...]
        carry[...] = dl[0:1, :]
        zz = z_ref[...] + b_ref[...]
        dz = dl * (1.0 / (1.0 + jnp.exp(zz)))
        lane = lax.broadcasted_iota(jnp.int32, dz.shape, 1)
        dz = jnp.where(lane < N_HEADS, dz, 0.0)
        dz_ref[...] = dz.astype(BF)
        db_ref[...] += jnp.sum(dz, axis=0, keepdims=True)

    return pl.pallas_call(
        body, grid=(nb,),
        in_specs=[pl.BlockSpec((GATE_ROWS, GATE_LANES), lambda i: (nb - 1 - i, 0)), pl.BlockSpec((1, GATE_LANES), lambda i: (0, 0)),
                  pl.BlockSpec((GATE_LANES, GATE_ROWS), lambda i: (0, nb - 1 - i)),
                  pl.BlockSpec((GATE_ROWS, GATE_LANES), lambda i: (nb - 1 - i, 0))],
        out_specs=[pl.BlockSpec((GATE_ROWS, GATE_LANES), lambda i: (nb - 1 - i, 0)), pl.BlockSpec((1, GATE_LANES), lambda i: (0, 0))],
        out_shape=[jax.ShapeDtypeStruct((S, GATE_LANES), BF), jax.ShapeDtypeStruct((1, GATE_LANES), F32)],
        scratch_shapes=[pltpu.VMEM((1, GATE_LANES), F32)], name=name, compiler_params=_params("arbitrary"),
    )(z, bf, dcT, dcq)


def _rope_tables(S):
    half = ROT_DIM // 2
    inv_freq = ROPE_THETA ** (-jnp.arange(half, dtype=F32) * 2.0 / ROT_DIM)
    ang = jnp.arange(S, dtype=F32)[:, None] * inv_freq[None, :]
    cos, sin = jnp.cos(ang), jnp.sin(ang)
    zero = jnp.zeros((S, HEAD_DIM - ROT_DIM), F32)
    zh = jnp.zeros((S, half), F32)
    cos_h = jnp.concatenate([cos, cos, jnp.ones_like(zero)], axis=1)
    sa_h = jnp.concatenate([-sin, zh, zero], axis=1)
    sb_h = jnp.concatenate([zh, sin, zero], axis=1)
    two = lambda t: jnp.concatenate([t, t], axis=1)
    return two(cos_h), two(sa_h), two(sb_h)


def _ffn_fwd(h, norm, w_gu, w_down, tag):
    n = _rms_fwd(h, norm, name=f"ffn{tag}_norm")
    gu = _mm_nn(n, w_gu, tm=2 * MM_ROWS, tn=1408, out_dtype=BF, name=f"ffn{tag}_gu")
    act = _swiglu_fwd(gu, name=f"ffn{tag}_act")
    out = _mm_nn(act, w_down, tm=MM_ROWS // 2, tn=1024, out_dtype=F32, name=f"ffn{tag}_down", resid=h)
    return out, (h, n, gu, act)


def _ffn_bwd(dh, dhb, saved, norm, w_gu, w_down, tag, ride=None):
    h, n, gu, act = saved
    dact = _mm_nt(dhb, w_down, tm=2 * MM_ROWS, to=1408, tn=1024, out_dtype=BF, name=f"ffn{tag}_dact")
    dw_down = _mm_tn(act, dhb, tk=1408, tn=1024, tm=2 * MM_ROWS, out_dtype=BF, name=f"ffn{tag}_dwdown")
    dgu = _swiglu_bwd(gu, dact, name=f"ffn{tag}_dgu")
    dn_call = lambda guest: _mm_nt(dgu, w_gu, tm=2 * MM_ROWS, to=1024, tn=1408, out_dtype=F32, name=f"ffn{tag}_dn", guest=guest)
    dn = dn_call(None) if ride is None else ride(dn_call)
    dw_gu = _mm_tn(n, dgu, tk=1024, tn=1408, tm=2 * MM_ROWS, out_dtype=BF, name=f"ffn{tag}_dwgu")
    dx, dxb, dg = _rms_bwd(h, norm, dn, dh, name=f"ffn{tag}_dnorm")
    return dx, dxb, dg, dw_gu, dw_down


def _local_step(x, tgt, w, mats, fetch, exchange):
    S = x.shape[0]
    rope_f = _rope_tables(S)
    rope_b = (rope_f[0], -rope_f[1], -rope_f[2])
    g, partial, landed = {}, {}, {}
    w = dict(w, ffn_w_gu={}, ffn_w_down={})

    def bring(call, indices):
        bufs = [mats[wi] for wi in indices]
        if fetch is None:
            return call(None), bufs
        return call(fetch(indices, bufs))

    def ride(call, indices):
        guest = exchange(indices, [partial[wi] for wi in indices]) if indices else None
        res = call(guest)
        if guest is None:
            return res
        res, outs = res
        landed.update(zip(indices, outs))
        return res

    n0 = _rms_fwd(x, w["a_norm"], name="a_norm")
    dils = [d for _, d in DILATED_PATTERNS]
    projs, (w["ffn_w_gu"][0], w["ffn_w_down"][0]) = bring(
        lambda guest: _mm_nn(n0, w["a_w_in"], tm=MM_ROWS, tn=1024, out_dtype=BF, name="a_proj", rope=rope_f, guest=guest,
                             groups=dils), [4, 6])
    block = lambda t, dil: (lambda r: t * dil + r)
    o_parts, lse_parts = [], []
    for gi, (window, dil) in enumerate(DILATED_PATTERNS):
        pv = projs[gi]
        attend = lambda guest: _band_fwd(pv, pv, pv, block(0, dil), block(1, dil), block(2, dil), dil=dil, T=128,
                                         window=window // dil, name=f"a_attn{gi}", guest=guest)
        if gi == 0:
            (o_g, lse_g), (w["a_w_out"],) = bring(attend, [1])
        elif gi == 1:
            (o_g, lse_g), (b_in,) = bring(attend, [2])
        else:
            (o_g, lse_g), (w["b_w_out"],) = bring(attend, [3])
        o_parts.append(o_g)
        lse_parts.append(lse_g)
    b_in = b_in.transpose(1, 0, 2).reshape(D_MODEL, -1)
    w["b_w_qkv"] = b_in[:, :QKV_COLS]
    w["b_w_f"] = jnp.pad(b_in[:, QKV_COLS:], ((0, 0), (0, GATE_LANES + QKV_COLS - b_in.shape[1])))
    mixed = _combine_groups(o_parts, lse_parts, dils, name="a_combine")
    o_a = mixed[1][0]
    h1 = _mm_nn(o_a, w["a_w_out"], tm=MM_ROWS, tn=1024, out_dtype=F32, name="a_out", resid=x)
    h2, ffn0 = _ffn_fwd(h1, w["ffn_norm"][0:1], w["ffn_w_gu"][0], w["ffn_w_down"][0], 0)

    n2 = _rms_fwd(h2, w["b_norm"], name="b_norm")
    qkv = _mm_nn(n2, w["b_w_qkv"], tm=MM_ROWS, tn=1024, out_dtype=BF, name="b_proj")
    zf = _mm_nn(n2, w["b_w_f"], tm=MM_ROWS, tn=GATE_LANES, out_dtype=F32, name="b_gate_proj")
    cT = _gate_fwd(zf, w["b_f"], name="b_gate")
    (o_b, lse_b), (w["ffn_w_gu"][1], w["ffn_w_down"][1]) = bring(lambda guest: _fox_fwd(qkv, cT, name="b_attn", guest=guest), [5, 7])
    h3 = _mm_nn(o_b, w["b_w_out"], tm=MM_ROWS, tn=1024, out_dtype=F32, name="b_out", resid=h2)
    h4, ffn1 = _ffn_fwd(h3, w["ffn_norm"][1:2], w["ffn_w_gu"][1], w["ffn_w_down"][1], 1)

    loss, dh4, dh4b, g["final_norm"] = _loss_head(h4, w["final_norm"], tgt, name="loss_head")

    dh3, dh3b, dg_f1, partial[5], partial[7] = _ffn_bwd(dh4, dh4b, ffn1, w["ffn_norm"][1:2], w["ffn_w_gu"][1], w["ffn_w_down"][1], 1)

    do_b = _mm_nt(dh3b, w["b_w_out"], tm=MM_ROWS, to=1024, tn=1024, out_dtype=BF, name="b_do")
    partial[3] = _mm_tn(o_b, dh3b, tk=1024, tn=1024, tm=MM_ROWS, out_dtype=BF, name="b_dwout")
    dq, dk, dv, dcT, dcq = ride(lambda guest: _fox_bwd(qkv, cT, do_b, o_b, lse_b, name="b_attn_bwd", guest=guest), [5, 7, 3])
    dz, g["b_f"] = _gate_bwd(zf, w["b_f"], dcT, dcq, name="b_gate_bwd")
    dqkv = _assemble([dq, dk, dv], [False] * 3, None, [1] * 3, name="b_dproj")
    dn2 = _mm_nt(dz, w["b_w_f"], tm=MM_ROWS, to=1024, tn=GATE_LANES, out_dtype=F32, name="b_dn_gate")
    dn2 = _mm_nt(dqkv, w["b_w_qkv"], tm=MM_ROWS, to=1024, tn=1024, out_dtype=F32, name="b_dn", add=dn2)
    g_qkv = _mm_tn(n2, dqkv, tk=1024, tn=1024, tm=MM_ROWS, out_dtype=BF, name="b_dwqkv")
    g_f = _mm_tn(n2, dz, tk=1024, tn=GATE_LANES, tm=MM_ROWS, out_dtype=BF, name="b_dwf")
    g_b_in = jnp.concatenate([g_qkv, g_f[:, :N_HEADS]], axis=1)
    partial[2] = g_b_in.reshape(D_MODEL, N_CHIPS, -1).transpose(1, 0, 2)
    dh2, dh2b, g["b_norm"] = _rms_bwd(h2, w["b_norm"], dn2, dh3, name="b_dnorm")

    dh1, dh1b, dg_f0, partial[4], partial[6] = _ffn_bwd(dh2, dh2b, ffn0, w["ffn_norm"][0:1], w["ffn_w_gu"][0], w["ffn_w_down"][0], 0,
                                                      ride=lambda call: ride(call, [2]))
    g["ffn_norm"] = jnp.concatenate([dg_f0, dg_f1], axis=0)

    views = tuple(sorted(set(dils)))
    do_a = dict(zip(views, _mm_nt(dh1b, w["a_w_out"], tm=MM_ROWS, to=1024, tn=1024, out_dtype=BF, name="a_do", views=views)))
    partial[1] = _mm_tn(o_a, dh1b, tk=1024, tn=1024, tm=MM_ROWS, out_dtype=BF, name="a_dwout")
    riders = {0: [4], 1: [6, 1], 2: []}
    parts = []
    for gi, (window, dil) in enumerate(DILATED_PATTERNS):
        pv = projs[gi]
        res = ride(lambda guest: _band_bwd(pv, pv, pv, block(0, dil), block(1, dil), block(2, dil), do_a[dil],
                                           mixed[dil][0], mixed[dil][1], dil=dil, T=128,
                                           window=window // dil, name=f"a_attn_bwd{gi}", guest=guest), riders[gi])
        parts += list(res)
    dproj = _assemble(parts, [True, True, False] * 3, rope_b, [d for _, d in DILATED_PATTERNS for _ in range(3)], name="a_dproj")
    partial[0] = _mm_tn(n0, dproj, tk=1024, tn=1024, tm=2 * MM_ROWS, out_dtype=BF, name="a_dwin")
    dn0 = ride(lambda guest: _mm_nt(dproj, w["a_w_in"], tm=2 * MM_ROWS, to=1024, tn=1024, out_dtype=F32, name="a_dn", guest=guest), [0])
    dx, _, g["a_norm"] = _rms_bwd(x, w["a_norm"], dn0, dh1, name="a_dnorm")
    return loss, dx, g, partial, landed


ANY = pl.BlockSpec(memory_space=pl.ANY)


def _place():
    x, y, c = lax.axis_index("x"), lax.axis_index("y"), lax.axis_index("c")
    chips = [(1 - x, y), (x, 1 - y), (1 - x, 1 - y)]
    return x, y, c, chips


def _shard_slice(ref, kind, rows, cols, s, half):
    hr = rows // 2
    if kind == "col":
        return ref.at[pl.ds(half * hr, hr), pl.ds(pl.multiple_of(s * cols, 128), cols)]
    if kind == "row":
        return ref.at[pl.ds(pl.multiple_of(s * rows + half * hr, 16), hr), :]
    return ref.at[s, pl.ds(half * hr, hr), :]


def _whole_shape(kind, rows, cols):
    return {"col": (rows, N_CHIPS * cols), "row": (N_CHIPS * rows, cols), "stack": (N_CHIPS, rows, cols)}[kind]


def _own_block(kind, rows, tr, cols):
    per = rows // tr

    def spec(half_rows):
        off = (lambda p: 0) if half_rows is None else (lambda p: p[1] * (half_rows // tr))
        if kind == "col":
            return pl.BlockSpec((tr, cols), lambda i, p: (off(p) + i, p[0]))
        if kind == "row":
            return pl.BlockSpec((tr, cols), lambda i, p: (p[0] * per + off(p) + i, 0))
        return pl.BlockSpec((None, tr, cols), lambda i, p: (p[0], off(p) + i, 0))
    return spec


def _place_shard(shards, layer, kind, place, *, name):
    _, rows, cols = shards.shape
    tr = 256 if rows % 256 == 0 else rows // 2

    def body(p_ref, s_ref, o_ref):
        o_ref[...] = s_ref[...].astype(BF)

    return pl.pallas_call(
        body,
        grid_spec=pltpu.PrefetchScalarGridSpec(
            num_scalar_prefetch=1, grid=(rows // tr,),
            in_specs=[pl.BlockSpec((None, tr, cols), lambda i, p: (layer, i, 0))],
            out_specs=_own_block(kind, rows, tr, cols)(None)),
        out_shape=jax.ShapeDtypeStruct(_whole_shape(kind, rows, cols), BF),
        name=name, compiler_params=_params("arbitrary"),
    )(place, shards)


def _gather_weights(placed, kinds, dims):
    nw = len(placed)

    def body(*refs):
        dst = refs[nw:2 * nw]
        send_sems, recv_sems = refs[2 * nw:]
        x, y, c, chips = _place()
        me = 2 * x + y
        sibling = (x, y, 1 - c)

        def copy(wi, k, s, half, to):
            p = _shard_slice(dst[wi], kinds[wi], dims[wi][0], dims[wi][1], s, half)
            return pltpu.make_async_remote_copy(src_ref=p, dst_ref=p, send_sem=send_sems.at[wi * 6 + k],
                                                recv_sem=recv_sems.at[wi * 6 + k], device_id=to, device_id_type=MESH)

        first, passed = [], []
        for wi in range(nw):
            for j, chip in enumerate(chips):
                cp = copy(wi, j, me, c, (*chip, c))
                cp.start()
                first.append(cp)
        for wi in range(nw):
            for j, chip in enumerate(chips):
                s = 2 * chip[0] + chip[1]
                copy(wi, j, s, c, (x, y, c)).wait_recv()
                cp = copy(wi, 3 + j, s, c, sibling)
                cp.start()
                passed.append(cp)
        for wi in range(nw):
            for j, chip in enumerate(chips):
                s = 2 * chip[0] + chip[1]
                copy(wi, 3 + j, s, 1 - c, (x, y, c)).wait_recv()
        for cp in first + passed:
            cp.wait_send()

    return pl.pallas_call(
        body, in_specs=[ANY] * nw, out_specs=[ANY] * nw,
        out_shape=[jax.ShapeDtypeStruct(p.shape, p.dtype) for p in placed],
        input_output_aliases={wi: wi for wi in range(nw)},
        scratch_shapes=[pltpu.SemaphoreType.DMA((nw * 6,)), pltpu.SemaphoreType.DMA((nw * 6,))],
        name="gather_weights",
    )(*placed)


def _fetch_guest(placed, kinds, dims):
    nw = len(placed)

    def copies(dst, send_sems, recv_sems, incoming):
        x, y, c, chips = _place()
        out = []
        for wi in range(nw):
            for j, chip in enumerate(chips):
                s = 2 * chip[0] + chip[1] if incoming else 2 * x + y
                to = (x, y, c) if incoming else (*chip, c)
                for half in range(2):
                    p = _shard_slice(dst[wi], kinds[wi], dims[wi][0], dims[wi][1], s, half)
                    k = wi * 6 + 2 * j + half
                    out.append(pltpu.make_async_remote_copy(src_ref=p, dst_ref=p, send_sem=send_sems.at[k],
                                                            recv_sem=recv_sems.at[k], device_id=to, device_id_type=MESH))
        return out

    def start(src, dst, sems):
        for cp in copies(dst, sems[0], sems[1], False):
            cp.start()

    def finish(src, dst, sems):
        for cp in copies(dst, sems[0], sems[1], True):
            cp.wait_recv()
        for cp in copies(dst, sems[0], sems[1], False):
            cp.wait_send()

    return dict(args=list(placed), out_shape=[jax.ShapeDtypeStruct(p.shape, p.dtype) for p in placed],
                scratch=[pltpu.SemaphoreType.DMA((nw * 6,)), pltpu.SemaphoreType.DMA((nw * 6,))],
                start=start, finish=finish, in_place=True)


def _scatter_guest(partials, kinds, dims):
    nw = len(partials)

    def copies(src, send_sems, recv_sems, dst):
        x, y, c, chips = _place()
        me = 2 * x + y
        out = []
        for wi in range(nw):
            rows, cols = dims[wi]

            def part(s, half, wi=wi, rows=rows, cols=cols):
                return _shard_slice(src[wi], kinds[wi], rows, cols, s, half)

            for j, chip in enumerate(chips):
                s = 2 * chip[0] + chip[1]
                for half in range(2):
                    slot = 2 * j + (c if half == 0 else 1 - c)
                    out.append(pltpu.make_async_remote_copy(
                        src_ref=part(s, half), dst_ref=dst[wi].at[slot],
                        send_sem=send_sems.at[wi * 7 + 2 * j + half], recv_sem=recv_sems.at[wi * 7 + slot],
                        device_id=(*chip, half), device_id_type=MESH))
            out.append(pltpu.make_async_remote_copy(
                src_ref=part(me, 1 - c), dst_ref=dst[wi].at[6],
                send_sem=send_sems.at[wi * 7 + 6], recv_sem=recv_sems.at[wi * 7 + 6],
                device_id=(x, y, 1 - c), device_id_type=MESH))
        return out

    def start(src, dst, sems):
        for cp in copies(src, sems[0], sems[1], dst):
            cp.start()

    def finish(src, dst, sems):
        x, y, c, _ = _place()
        for wi in range(nw):
            for slot in range(7):
                pltpu.make_async_remote_copy(
                    src_ref=dst[wi].at[slot], dst_ref=dst[wi].at[slot],
                    send_sem=sems[0].at[wi * 7 + slot], recv_sem=sems[1].at[wi * 7 + slot],
                    device_id=(x, y, c), device_id_type=MESH).wait_recv()
        for cp in copies(src, sems[0], sems[1], dst):
            cp.wait_send()

    return dict(args=list(partials), out_shape=[jax.ShapeDtypeStruct((7, d[0] // 2, d[1]), BF) for d in dims],
                scratch=[pltpu.SemaphoreType.DMA((nw * 7,)), pltpu.SemaphoreType.DMA((nw * 7,))],
                start=start, finish=finish)


def _sum_slots(slots, partial, kind, dims, place, *, name, into=None, layer=None, n_layers=1):
    rows, cols = dims
    hr = rows // 2
    tr = hr if 8 * hr * cols * 2 <= 6 * 1024 * 1024 else 128
    assert hr % tr == 0

    def body(p_ref, b_ref, own_ref, *rest):
        o_ref = rest[-1]
        acc = own_ref[...].astype(F32)
        for k in range(7):
            acc = acc + b_ref[k].astype(F32)
        o_ref[...] = acc

    half = lambda p: p[1] * (hr // tr)
    if n_layers == 1:
        out_spec = pl.BlockSpec((tr, cols), lambda i, p: (half(p) + i, 0))
        out_shape = jax.ShapeDtypeStruct((rows, cols), F32)
    else:
        out_spec = pl.BlockSpec((None, tr, cols), lambda i, p: (layer, half(p) + i, 0))
        out_shape = jax.ShapeDtypeStruct((n_layers, rows, cols), F32)
    in_specs = [pl.BlockSpec((7, tr, cols), lambda i, p: (0, i, 0)), _own_block(kind, rows, tr, cols)(hr)]
    args = [place, slots, partial]
    aliases = {}
    if into is not None:
        in_specs.append(ANY)
        args.append(into)
        aliases = {3: 0}
    return pl.pallas_call(
        body,
        grid_spec=pltpu.PrefetchScalarGridSpec(num_scalar_prefetch=1, grid=(hr // tr,), in_specs=in_specs, out_specs=out_spec),
        out_shape=out_shape, input_output_aliases=aliases, name=name, compiler_params=_params("arbitrary"),
    )(*args)


def _pair_exchange(bufs, members):
    nw = len(members)

    def body(*refs):
        dst = refs[len(bufs):2 * len(bufs)]
        send_sems, recv_sems = refs[2 * len(bufs):]
        x, y, c, _ = _place()

        def rows_of(wi, half):
            bi, l = members[wi]
            ref = dst[bi] if l is None else dst[bi].at[l]
            hr = ref.shape[0] // 2
            return ref.at[pl.ds(pl.multiple_of(half * hr, 8), hr), :]

        def copy(wi, half, to):
            p = rows_of(wi, half)
            return pltpu.make_async_remote_copy(src_ref=p, dst_ref=p, send_sem=send_sems.at[wi], recv_sem=recv_sems.at[wi],
                                                device_id=to, device_id_type=MESH)

        sent = []
        for wi in range(nw):
            cp = copy(wi, c, (x, y, 1 - c))
            cp.start()
            sent.append(cp)
        for wi in range(nw):
            copy(wi, 1 - c, (x, y, c)).wait_recv()
        for cp in sent:
            cp.wait_send()

    return pl.pallas_call(
        body, in_specs=[ANY] * len(bufs), out_specs=[ANY] * len(bufs),
        out_shape=[jax.ShapeDtypeStruct(b.shape, b.dtype) for b in bufs],
        input_output_aliases={i: i for i in range(len(bufs))},
        scratch_shapes=[pltpu.SemaphoreType.DMA((nw,)), pltpu.SemaphoreType.DMA((nw,))],
        name="pair_exchange",
    )(*bufs)


SMALL_ROWS = 8


def _allreduce_small(v, *, name):
    assert v.shape == (SMALL_ROWS, D_MODEL)

    def body(v_ref, o_ref, buf, send_sems, recv_sems):
        x, y, c, _ = _place()
        me = 4 * x + 2 * y + c
        buf[me] = v_ref[...]
        sent = []
        for k in range(1, 8):
            bx, by, bc = (k >> 2) & 1, (k >> 1) & 1, k & 1
            peer = (1 - x if bx else x, 1 - y if by else y, 1 - c if bc else c)
            cp = pltpu.make_async_remote_copy(src_ref=v_ref, dst_ref=buf.at[me], send_sem=send_sems.at[k - 1],
                                              recv_sem=recv_sems.at[k - 1], device_id=peer, device_id_type=MESH)
            cp.start()
            sent.append(cp)
        for k in range(1, 8):
            bx, by, bc = (k >> 2) & 1, (k >> 1) & 1, k & 1
            peer = 4 * (1 - x if bx else x) + 2 * (1 - y if by else y) + (1 - c if bc else c)
            pltpu.make_async_remote_copy(src_ref=v_ref, dst_ref=buf.at[peer], send_sem=send_sems.at[k - 1],
                                         recv_sem=recv_sems.at[k - 1], device_id=(x, y, c), device_id_type=MESH).wait_recv()
        for cp in sent:
            cp.wait_send()
        acc = buf[0]
        for d in range(1, 8):
            acc = acc + buf[d]
        o_ref[...] = acc

    vmem = pl.BlockSpec(memory_space=pltpu.VMEM)
    return pl.pallas_call(
        body, in_specs=[vmem], out_specs=vmem, out_shape=jax.ShapeDtypeStruct(v.shape, F32),
        scratch_shapes=[pltpu.VMEM((8,) + v.shape, F32), pltpu.SemaphoreType.DMA((7,)), pltpu.SemaphoreType.DMA((7,))],
        name=name,
    )(v)


def _adamw(w, g, m, v, *, name):
    R, C = w.shape
    tr = R
    if R * C * 4 > 1024 * 1024:
        tr = max(t for t in range(8, R, 8) if R % t == 0 and t * C * 4 <= 1024 * 1024)

    def body(w_ref, g_ref, m_ref, v_ref, d_ref, m2_ref, v2_ref):
        gg = g_ref[...]
        m2 = ADAM_B1 * m_ref[...] + (1.0 - ADAM_B1) * gg
        v2 = ADAM_B2 * v_ref[...] + (1.0 - ADAM_B2) * jnp.square(gg)
        m_hat = m2 / (1.0 - ADAM_B1 ** ADAM_STEP)
        v_hat = v2 / (1.0 - ADAM_B2 ** ADAM_STEP)
        d_ref[...] = -ADAM_LR * (m_hat / (jnp.sqrt(v_hat) + ADAM_EPS) + ADAM_WD * w_ref[...])
        m2_ref[...] = m2
        v2_ref[...] = v2

    blk = pl.BlockSpec((tr, C), lambda i: (i, 0))
    out = jax.ShapeDtypeStruct((R, C), F32)
    return pl.pallas_call(
        body, grid=(R // tr,), in_specs=[blk] * 4, out_specs=[blk] * 3, out_shape=[out] * 3,
        name=name, compiler_params=_params("parallel"),
    )(w, g, m, v)


WEIGHT_ORDER = ("a_norm", "a_w_in", "a_w_out", "b_norm", "b_w_in", "b_f", "b_w_out", "ffn_norm", "ffn_w_gu",
                "ffn_w_down", "final_norm")
MATRICES = (("a_w_in", 0, "col"), ("a_w_out", 0, "row"), ("b_w_in", 0, "stack"), ("b_w_out", 0, "row"),
            ("ffn_w_gu", 0, "col"), ("ffn_w_gu", 1, "col"), ("ffn_w_down", 0, "row"), ("ffn_w_down", 1, "row"))
MATRIX_GROUPS = ([0], [1], [2], [3], [4, 5], [6, 7])
GROUP_NAMES = ("a_w_in", "a_w_out", "b_w_in", "b_w_out", "ffn_w_gu", "ffn_w_down")
QKV_COLS = 3 * N_HEADS * HEAD_DIM


def kernel(x, a_norm, a_w_in, a_w_out, b_norm, b_w_in, b_f, b_w_out, ffn_norm, ffn_w_gu, ffn_w_down, final_norm, loss_target, m_a_norm, m_a_w_in, m_a_w_out, m_b_norm, m_b_w_in, m_b_f, m_b_w_out, m_ffn_norm, m_ffn_w_gu, m_ffn_w_down, m_final_norm, v_a_norm, v_a_w_in, v_a_w_out, v_b_norm, v_b_w_in, v_b_f, v_b_w_out, v_ffn_norm, v_ffn_w_gu, v_ffn_w_down, v_final_norm):
    given = dict(a_norm=a_norm, a_w_in=a_w_in, a_w_out=a_w_out, b_norm=b_norm, b_w_in=b_w_in, b_f=b_f, b_w_out=b_w_out,
                 ffn_norm=ffn_norm, ffn_w_gu=ffn_w_gu, ffn_w_down=ffn_w_down, final_norm=final_norm)
    mom_m = dict(a_norm=m_a_norm, a_w_in=m_a_w_in, a_w_out=m_a_w_out, b_norm=m_b_norm, b_w_in=m_b_w_in, b_f=m_b_f,
                 b_w_out=m_b_w_out, ffn_norm=m_ffn_norm, ffn_w_gu=m_ffn_w_gu, ffn_w_down=m_ffn_w_down, final_norm=m_final_norm)
    mom_v = dict(a_norm=v_a_norm, a_w_in=v_a_w_in, a_w_out=v_a_w_out, b_norm=v_b_norm, b_w_in=v_b_w_in, b_f=v_b_f,
                 b_w_out=v_b_w_out, ffn_norm=v_ffn_norm, ffn_w_gu=v_ffn_w_gu, ffn_w_down=v_ffn_w_down, final_norm=v_final_norm)
    chip = 2 * lax.axis_index("x") + lax.axis_index("y")
    core = lax.axis_index("c")
    bn_cols = b_norm.shape[1]

    placed = lax.dynamic_update_slice(jnp.zeros((SMALL_ROWS, D_MODEL), F32), b_norm, (0, chip * bn_cols))
    placed = placed * (core == 0).astype(F32)
    b_norm_full = _allreduce_small(placed, name="gather_b_norm")[0:1]

    place = jnp.stack([chip, core]).astype(jnp.int32)
    kinds = [k for _, _, k in MATRICES]
    dims = [given[n].shape[1:] for n, _, _ in MATRICES]
    placed = [_place_shard(given[n], l, k, place, name=f"place_{n}{l}") for n, l, k in MATRICES]
    first = _gather_weights(placed[:1], kinds[:1], dims[:1])
    mats = dict(enumerate(list(first) + placed[1:]))
    gate_cols = b_f.shape[1]
    w = dict(a_norm=a_norm, a_w_in=mats[0], b_norm=b_norm_full,
             b_f=jnp.pad(b_f, ((0, 0), (0, GATE_LANES - gate_cols))), ffn_norm=ffn_norm,
             final_norm=final_norm.reshape(1, D_MODEL))

    def fetch(indices, bufs):
        return _fetch_guest(bufs, [kinds[i] for i in indices], [dims[i] for i in indices])

    def exchange(indices, parts):
        return _scatter_guest(parts, [kinds[i] for i in indices], [dims[i] for i in indices])

    loss, dx, g, partials, slots = _local_step(x[0], loss_target[0], w, mats, fetch, exchange)
    bufs, members = [], []
    for group in MATRIX_GROUPS:
        buf = None
        for l, wi in enumerate(group):
            n = MATRICES[wi][0]
            buf = _sum_slots(slots[wi], partials[wi], kinds[wi], dims[wi], place, name=f"sum_{n}{l}", into=buf,
                             layer=l, n_layers=len(group))
            members.append((len(bufs), l if len(group) > 1 else None))
        bufs.append(buf)
    reduced = dict(zip(GROUP_NAMES, _pair_exchange(bufs, members)))

    small = jnp.concatenate([g["a_norm"], g["b_norm"], g["ffn_norm"], g["final_norm"],
                             jnp.pad(g["b_f"], ((0, 0), (0, D_MODEL - GATE_LANES))),
                             jnp.zeros((SMALL_ROWS - 6, D_MODEL), F32)], axis=0)
    small = _allreduce_small(small, name="allreduce_small")
    grads = dict(reduced)
    grads["a_norm"] = small[0:1]
    grads["b_norm"] = lax.dynamic_slice(small, (1, chip * bn_cols), (1, bn_cols))
    grads["ffn_norm"] = small[2:4]
    grads["final_norm"] = small[4]
    grads["b_f"] = small[5:6, :gate_cols]

    out_g, out_d, out_m, out_v = [], [], [], []
    for n in WEIGHT_ORDER:
        shape = given[n].shape
        two_d = (1, shape[0]) if len(shape) == 1 else (-1, shape[-1])
        d, m2, v2 = _adamw(given[n].reshape(two_d), grads[n].reshape(two_d), mom_m[n].reshape(two_d),
                           mom_v[n].reshape(two_d), name=f"adamw_{n}")
        out_g.append(grads[n].reshape(shape))
        out_d.append(d.reshape(shape))
        out_m.append(m2.reshape(shape))
        out_v.append(v2.reshape(shape))

    total = lax.psum(loss[0, 0], MESH_AXES)
    return (total, dx[None], *out_g, *out_d, *out_m, *out_v)
```

```python
import functools

import jax
import jax.numpy as jnp
from jax import lax
from jax.experimental import pallas as pl
from jax.experimental.pallas import tpu as pltpu

F32 = jnp.float32
BF = jnp.bfloat16

D_MODEL = 1024
N_HEADS = 16
HEAD_DIM = 64
D_FF = 2816
DILATED_PATTERNS = ((128, 1), (512, 4), (2048, 16))
ROT_DIM = 16
ROPE_THETA = 500000.0
RMS_EPS = 1e-6
NEG_INF = -1e30
ATTN_SCALE = HEAD_DIM ** -0.5
GATE_LANES = 128
N_CHIPS = 4
MESH_AXES = ("x", "y", "c")
MESH = pl.DeviceIdType.MESH

ADAM_LR = 0.001
ADAM_B1 = 0.9
ADAM_B2 = 0.999
ADAM_EPS = 1e-08
ADAM_WD = 0.01
ADAM_STEP = 10

VMEM_LIMIT_BYTES = 56 * 1024 * 1024


def _params(*sem):
    return pltpu.CompilerParams(dimension_semantics=sem, vmem_limit_bytes=VMEM_LIMIT_BYTES)


def _hosted_call(body, *, grid, in_specs, out_specs, out_shape, scratch_shapes, args, name, guest=None, schedule=()):
    params = _params(*(["arbitrary"] * len(grid)))
    ns = len(schedule)

    def call(kernel, in_specs, out_specs, out_shape, scratch_shapes, aliases, args):
        spec = pltpu.PrefetchScalarGridSpec(num_scalar_prefetch=ns, grid=grid, in_specs=in_specs, out_specs=out_specs,
                                            scratch_shapes=scratch_shapes)
        return pl.pallas_call(kernel, grid_spec=spec, out_shape=out_shape, input_output_aliases=aliases, name=name,
                              compiler_params=params)(*schedule, *args)

    if guest is None:
        return call(body, in_specs, out_specs, out_shape, scratch_shapes, {}, args)
    n_in, n_out, n_scr = ns + len(in_specs), len(out_specs), len(scratch_shapes)
    g_in, g_out = len(guest["args"]), len(guest["out_shape"])
    any_spec = pl.BlockSpec(memory_space=pl.ANY)

    def wrapped(*refs):
        i1 = n_in + g_in
        o1 = i1 + n_out
        o2 = o1 + g_out
        s1 = o2 + n_scr
        guest_refs = (refs[n_in:i1], refs[o1:o2], refs[s1:])
        ids = [pl.program_id(d) for d in range(len(grid))]
        first = functools.reduce(jnp.logical_and, [i == 0 for i in ids])
        last = functools.reduce(jnp.logical_and, [i == g - 1 for i, g in zip(ids, grid)])

        @pl.when(first)
        def _():
            guest["start"](*guest_refs)

        body(*refs[:n_in], *refs[i1:o1], *refs[o2:s1])

        @pl.when(last)
        def _():
            guest["finish"](*guest_refs)

    aliases = {n_in + k: n_out + k for k in range(g_in)} if guest.get("in_place") else {}
    return call(wrapped, list(in_specs) + [any_spec] * g_in, list(out_specs) + [any_spec] * g_out,
                list(out_shape) + list(guest["out_shape"]), list(scratch_shapes) + list(guest["scratch"]), aliases,
                list(args) + list(guest["args"]))


def _rope_rotate(t, cos, sin_a, sin_b):
    outs = []
    for cidx in range(t.shape[1] // 128):
        tc = t[:, cidx * 128:(cidx + 1) * 128]
        outs.append(tc * cos + pltpu.roll(tc, 120, 1) * sin_a + pltpu.roll(tc, 8, 1) * sin_b)
    return jnp.concatenate(outs, axis=1)


def _mm_nn(a, b, *, tm, tn, out_dtype, name, resid=None, rope=None, guest=None, groups=None):
    M, K = a.shape
    N = b.shape[1]
    assert M % tm == 0 and N % tn == 0 and b.shape[0] == K
    n_in = 2 + (resid is not None) + (3 if rope is not None else 0)
    if groups is not None:
        assert rope is not None and N == 3 * tn * len(groups)

    def body(*refs):
        a_ref, b_ref = refs[0], refs[1]
        o_ref = refs[n_in]
        acc = jnp.dot(a_ref[...], b_ref[...], preferred_element_type=F32)
        if resid is not None:
            acc = acc + refs[2][...]
        if groups is not None:
            cos_ref, sa_ref, sb_ref = refs[n_in - 3:n_in]
            j = pl.program_id(1)
            for g, d in enumerate(groups):
                for is_v in (False, True):
                    @pl.when(jnp.logical_and(j // 3 == g, (j % 3 == 2) == is_v))
                    def _(g=g, d=d, is_v=is_v):
                        val = acc if is_v else _rope_rotate(acc, cos_ref[...], sa_ref[...], sb_ref[...])
                        if d == 1:
                            refs[n_in + g][...] = val.astype(out_dtype)
                        else:
                            _to_view(val, refs[-1], refs[n_in + g], d, tn)
        elif rope is not None:
            cos_ref, sa_ref, sb_ref = refs[n_in - 3:n_in]
            j = pl.program_id(1)

            @pl.when(j % 3 != 2)
            def _():
                o_ref[...] = _rope_rotate(acc, cos_ref[...], sa_ref[...], sb_ref[...]).astype(out_dtype)

            @pl.when(j % 3 == 2)
            def _():
                o_ref[...] = acc.astype(out_dtype)
        else:
            o_ref[...] = acc.astype(out_dtype)

    in_specs = [pl.BlockSpec((tm, K), lambda i, j: (i, 0)), pl.BlockSpec((K, tn), lambda i, j: (0, j))]
    args = [a, b]
    if resid is not None:
        in_specs.append(pl.BlockSpec((tm, tn), lambda i, j: (i, j)))
        args.append(resid)
    if rope is not None:
        assert tn == 1024
        for t in rope:
            in_specs.append(pl.BlockSpec((tm, 128), lambda i, j: (i, 0)))
            args.append(t)
    if groups is None:
        out_specs = [pl.BlockSpec((tm, tn), lambda i, j: (i, j))]
        out_shape = [jax.ShapeDtypeStruct((M, N), out_dtype)]
        scratch = []
    else:
        out_specs = [pl.BlockSpec((tm // d, d * tn), lambda i, j, g=g: (i, jnp.clip(j - 3 * g, 0, 2)))
                     for g, d in enumerate(groups)]
        out_shape = [jax.ShapeDtypeStruct((M // d, d * 3 * tn), out_dtype) for d in groups]
        scratch = [pltpu.VMEM((tn // 128, tm, 128), F32)]
    outs = _hosted_call(body, grid=(M // tm, N // tn), in_specs=in_specs, out_specs=out_specs, out_shape=out_shape,
                        scratch_shapes=scratch, args=args, name=name, guest=guest)
    nout = len(out_shape)
    res = outs[0] if groups is None else list(outs[:nout])
    return res if guest is None else (res, outs[nout:])


def _mm_nt(a, b, *, tm, to, tn, out_dtype, name, add=None, guest=None, views=(1,)):
    M, N = a.shape
    O = b.shape[0]
    assert M % tm == 0 and O % to == 0 and N % tn == 0 and b.shape[1] == N
    nk = N // tn

    def body(*refs):
        a_ref, b_ref = refs[0], refs[1]
        n_in = 2 + (add is not None)
        o_refs = refs[n_in:n_in + len(views)]
        acc_ref = refs[n_in + len(views)]
        k = pl.program_id(2)

        @pl.when(k == 0)
        def _():
            if add is not None:
                acc_ref[...] = refs[2][...]
            else:
                acc_ref[...] = jnp.zeros_like(acc_ref)

        acc_ref[...] += lax.dot_general(a_ref[...], b_ref[...], (((1,), (1,)), ((), ())),
                                        preferred_element_type=F32)

        @pl.when(k == nk - 1)
        def _():
            for o_ref, d in zip(o_refs, views):
                if d == 1:
                    o_ref[...] = acc_ref[...].astype(out_dtype)
                else:
                    _to_view(acc_ref[...], refs[-1], o_ref, d, to)

    in_specs = [pl.BlockSpec((tm, tn), lambda i, j, k: (i, k)), pl.BlockSpec((to, tn), lambda i, j, k: (j, k))]
    args = [a, b]
    if add is not None:
        in_specs.append(pl.BlockSpec((tm, to), lambda i, j, k: (i, j)))
        args.append(add)
    assert views == (1,) or (to == O and to % 128 == 0)
    scratch = [pltpu.VMEM((tm, to), F32)] + ([pltpu.VMEM((to // 128, tm, 128), F32)] if views != (1,) else [])
    outs = _hosted_call(
        body, grid=(M // tm, O // to, nk), in_specs=in_specs,
        out_specs=[pl.BlockSpec((tm, to), lambda i, j, k: (i, j)) if d == 1 else _view_spec(tm, d, to) for d in views],
        out_shape=[jax.ShapeDtypeStruct((M // d, d * O), out_dtype) for d in views],
        scratch_shapes=scratch, args=args, name=name, guest=guest)
    nv = len(views)
    res = outs[0] if nv == 1 else list(outs[:nv])
    return res if guest is None else (res, outs[nv:])


def _mm_tn(a, b, *, tk, tn, tm, out_dtype, name):
    M, K = a.shape
    N = b.shape[1]
    assert M % tm == 0 and K % tk == 0 and N % tn == 0 and b.shape[0] == M
    nm = M // tm

    def body(a_ref, b_ref, o_ref, acc_ref):
        m = pl.program_id(2)

        @pl.when(m == 0)
        def _():
            acc_ref[...] = jnp.zeros_like(acc_ref)

        acc_ref[...] += lax.dot_general(a_ref[...], b_ref[...], (((0,), (0,)), ((), ())),
                                        preferred_element_type=F32)

        @pl.when(m == nm - 1)
        def _():
            o_ref[...] = acc_ref[...].astype(out_dtype)

    return pl.pallas_call(
        body, grid=(K // tk, N // tn, nm),
        in_specs=[pl.BlockSpec((tm, tk), lambda i, j, m: (m, i)), pl.BlockSpec((tm, tn), lambda i, j, m: (m, j))],
        out_specs=pl.BlockSpec((tk, tn), lambda i, j, m: (i, j)),
        out_shape=jax.ShapeDtypeStruct((K, N), out_dtype),
        scratch_shapes=[pltpu.VMEM((tk, tn), F32)], name=name,
        compiler_params=_params("parallel", "parallel", "arbitrary"),
    )(a, b)


ROW_TILE = 512
MM_ROWS = 1024


def _rms_fwd(x, g, *, name):
    S, Dm = x.shape

    def body(x_ref, g_ref, o_ref):
        xf = x_ref[...]
        r = lax.rsqrt(jnp.mean(xf * xf, axis=-1, keepdims=True) + RMS_EPS)
        o_ref[...] = (xf * r * g_ref[...]).astype(BF)

    return pl.pallas_call(
        body, grid=(S // ROW_TILE,),
        in_specs=[pl.BlockSpec((ROW_TILE, Dm), lambda i: (i, 0)), pl.BlockSpec((1, Dm), lambda i: (0, 0))],
        out_specs=pl.BlockSpec((ROW_TILE, Dm), lambda i: (i, 0)),
        out_shape=jax.ShapeDtypeStruct((S, Dm), BF), name=name, compiler_params=_params("parallel"),
    )(x, g)


def _rms_bwd(x, g, dn, dres, *, name):
    S, Dm = x.shape

    def body(x_ref, g_ref, dn_ref, dres_ref, dx_ref, dxb_ref, dg_ref):
        i = pl.program_id(0)
        xf = x_ref[...]
        r = lax.rsqrt(jnp.mean(xf * xf, axis=-1, keepdims=True) + RMS_EPS)
        xh = xf * r
        dnf = dn_ref[...]
        dyg = dnf * g_ref[...]
        dx = dres_ref[...] + r * (dyg - xh * jnp.mean(dyg * xh, axis=-1, keepdims=True))
        dx_ref[...] = dx
        dxb_ref[...] = dx.astype(BF)

        @pl.when(i == 0)
        def _():
            dg_ref[...] = jnp.zeros_like(dg_ref)

        dg_ref[...] += jnp.sum(dnf * xh, axis=0, keepdims=True)

    row = pl.BlockSpec((ROW_TILE, Dm), lambda i: (i, 0))
    vec = pl.BlockSpec((1, Dm), lambda i: (0, 0))
    return pl.pallas_call(
        body, grid=(S // ROW_TILE,), in_specs=[row, vec, row, row], out_specs=[row, row, vec],
        out_shape=[jax.ShapeDtypeStruct((S, Dm), F32), jax.ShapeDtypeStruct((S, Dm), BF),
                   jax.ShapeDtypeStruct((1, Dm), F32)],
        name=name, compiler_params=_params("arbitrary"),
    )(x, g, dn, dres)


def _loss_head(h, g, tgt, *, name):
    S, Dm = h.shape

    def body(h_ref, g_ref, t_ref, loss_ref, dh_ref, dhb_ref, dg_ref):
        i = pl.program_id(0)
        xf = h_ref[...]
        r = lax.rsqrt(jnp.mean(xf * xf, axis=-1, keepdims=True) + RMS_EPS)
        xh = xf * r
        gv = g_ref[...]
        err = xh * gv - t_ref[...]
        dy = err * (1.0 / Dm)
        dyg = dy * gv
        dh = r * (dyg - xh * jnp.mean(dyg * xh, axis=-1, keepdims=True))
        dh_ref[...] = dh
        dhb_ref[...] = dh.astype(BF)

        @pl.when(i == 0)
        def _():
            dg_ref[...] = jnp.zeros_like(dg_ref)
            loss_ref[...] = jnp.zeros_like(loss_ref)

        dg_ref[...] += jnp.sum(dy * xh, axis=0, keepdims=True)
        part = 0.5 * jnp.sum(jnp.mean(err * err, axis=-1, keepdims=True), axis=0, keepdims=True)
        loss_ref[...] += jnp.broadcast_to(part, loss_ref.shape)

    row = pl.BlockSpec((ROW_TILE, Dm), lambda i: (i, 0))
    vec = pl.BlockSpec((1, Dm), lambda i: (0, 0))
    return pl.pallas_call(
        body, grid=(S // ROW_TILE,), in_specs=[row, vec, row],
        out_specs=[pl.BlockSpec((1, 128), lambda i: (0, 0)), row, row, vec],
        out_shape=[jax.ShapeDtypeStruct((1, 128), F32), jax.ShapeDtypeStruct((S, Dm), F32),
                   jax.ShapeDtypeStruct((S, Dm), BF), jax.ShapeDtypeStruct((1, Dm), F32)],
        name=name, compiler_params=_params("arbitrary"),
    )(h, g, tgt)


SWIGLU_ROWS = 512


def _swiglu_fwd(gu, *, name):
    S = gu.shape[0]

    def body(g_ref, u_ref, o_ref):
        g = g_ref[...].astype(F32)
        sig = 1.0 / (1.0 + jnp.exp(-g))
        o_ref[...] = (g * sig * u_ref[...].astype(F32)).astype(BF)

    return pl.pallas_call(
        body, grid=(S // SWIGLU_ROWS,),
        in_specs=[pl.BlockSpec((SWIGLU_ROWS, D_FF), lambda i: (i, 0)), pl.BlockSpec((SWIGLU_ROWS, D_FF), lambda i: (i, 1))],
        out_specs=pl.BlockSpec((SWIGLU_ROWS, D_FF), lambda i: (i, 0)),
        out_shape=jax.ShapeDtypeStruct((S, D_FF), BF), name=name, compiler_params=_params("parallel"),
    )(gu, gu)


def _swiglu_bwd(gu, dact, *, name):
    S = gu.shape[0]

    def body(g_ref, u_ref, d_ref, o_ref):
        g = g_ref[...].astype(F32)
        u = u_ref[...].astype(F32)
        d = d_ref[...].astype(F32)
        sig = 1.0 / (1.0 + jnp.exp(-g))
        o_ref[:, :D_FF] = (d * u * sig * (1.0 + g * (1.0 - sig))).astype(BF)
        o_ref[:, D_FF:] = (d * g * sig).astype(BF)

    return pl.pallas_call(
        body, grid=(S // SWIGLU_ROWS,),
        in_specs=[pl.BlockSpec((SWIGLU_ROWS, D_FF), lambda i: (i, 0)), pl.BlockSpec((SWIGLU_ROWS, D_FF), lambda i: (i, 1)),
                  pl.BlockSpec((SWIGLU_ROWS, D_FF), lambda i: (i, 0))],
        out_specs=pl.BlockSpec((SWIGLU_ROWS, 2 * D_FF), lambda i: (i, 0)),
        out_shape=jax.ShapeDtypeStruct((S, 2 * D_FF), BF), name=name, compiler_params=_params("parallel"),
    )(gu, gu, dact)


class _Columns:
    def __init__(self, ref, offset):
        self.ref, self.offset = ref, offset

    def __getitem__(self, idx):
        rows, cols = idx
        return self.ref[rows, slice(cols.start + self.offset, cols.stop + self.offset)]


def _joined_qkv_specs(T, cur, prev):
    return [pl.BlockSpec((T, 3072), lambda r, n: (cur(n), 0)), pl.BlockSpec((T, 3072), lambda r, n: (prev(n), 0))]


def _joined_qkv(body):
    def kernel(cur_ref, prev_ref, *rest):
        body(_Columns(cur_ref, 0), _Columns(prev_ref, 1024), _Columns(cur_ref, 1024), _Columns(prev_ref, 2048),
             _Columns(cur_ref, 2048), *rest)
    return kernel


def _band_masks(T, n):
    row = lax.broadcasted_iota(jnp.int32, (T, T), 0)
    col = lax.broadcasted_iota(jnp.int32, (T, T), 1)
    return jnp.logical_and(col >= row, n > 0), col <= row


def _band_fwd(qa, ka, va, qcb, kcb, vcb, *, dil, T, window, name, guest=None):
    L = qa.shape[0]
    nq = L // T
    assert window == T
    nt = (((1,), (1,)), ((), ()))

    def body(q_ref, kp_ref, kc_ref, vp_ref, vc_ref, o_ref, lse_ref):
        valid = jnp.concatenate(_band_masks(T, pl.program_id(1)), axis=1)
        lane = lax.broadcasted_iota(jnp.int32, (T, 128), 1)
        low = lane < HEAD_DIM
        ones = jnp.ones((2 * T, 128), BF)
        lse = jnp.zeros((T, 128), F32)
        def scores(h):
            ps = slice((h // 2) * 128, (h // 2 + 1) * 128)
            qp = q_ref[:, ps] * jnp.asarray(ATTN_SCALE, BF)
            qm = jnp.where(low if h % 2 == 0 else jnp.logical_not(low), qp, jnp.zeros_like(qp))
            keys = jnp.concatenate([kp_ref[:, ps], kc_ref[:, ps]], axis=0)
            return jnp.where(valid, lax.dot_general(qm, keys, nt, preferred_element_type=F32), NEG_INF)

        def softmax(s):
            m = jnp.max(s, axis=1, keepdims=True)
            return m, jnp.exp(s - m).astype(BF)

        def weighted(h, p):
            ps = slice((h // 2) * 128, (h // 2 + 1) * 128)
            values = jnp.concatenate([jnp.concatenate([vp_ref[:, ps], vc_ref[:, ps]], axis=0), ones], axis=1)
            pv = jnp.dot(p, values, preferred_element_type=F32)
            return pv[:, 128:], pv[:, :128]

        sc, pr, even = {}, {}, None
        for t in range(N_HEADS + 2):
            if t < N_HEADS:
                sc[t] = scores(t)
            done = None
            if t >= 2:
                m, p = pr.pop(t - 2)
                done = (m,) + weighted(t - 2, p)
            if 1 <= t <= N_HEADS:
                pr[t - 1] = softmax(sc.pop(t - 1))
            if done is not None:
                h = t - 2
                m, l, acc = done
                lse = jnp.where(lane == h, m + jnp.log(l), lse)
                if h % 2 == 0:
                    even = acc / l
                else:
                    o_ref[:, (h // 2) * 128:(h // 2 + 1) * 128] = jnp.where(low, even, acc / l)
        lse_ref[...] = lse

    def prev(n):
        return jnp.maximum(n - 1, 0)

    blk = lambda f, cb: pl.BlockSpec((T, 1024), lambda r, n: (f(n), cb(r)))
    same = lambda n: n
    in_specs, args, kernel = [blk(same, qcb), blk(prev, kcb), blk(same, kcb), blk(prev, vcb), blk(same, vcb)], (qa, ka, ka, va, va), body
    if dil == 1 and qa is ka and ka is va:
        in_specs, args, kernel = _joined_qkv_specs(T, same, prev), (qa, qa), _joined_qkv(body)
    outs = _hosted_call(
        kernel, grid=(dil, nq), in_specs=in_specs,
        out_specs=[pl.BlockSpec((T, 1024), lambda r, n: (n, r)), pl.BlockSpec((T, 128), lambda r, n: (n, r))],
        out_shape=[jax.ShapeDtypeStruct((L, dil * 1024), F32), jax.ShapeDtypeStruct((L, dil * 128), F32)],
        scratch_shapes=[], args=args, name=name, guest=guest)
    return outs if guest is None else (outs[:2], outs[2:])


def _band_bwd(qa, ka, va, qcb, kcb, vcb, doa, oa, lsea, *, dil, T, window, name, guest=None):
    L = qa.shape[0]
    nq = L // T
    assert window == T
    nt = (((1,), (1,)), ((), ()))
    tn = (((0,), (0,)), ((), ()))

    def body(q_ref, kp_ref, kc_ref, vp_ref, vc_ref, do_ref, o_ref, lse_ref, dq_ref, dk_ref, dv_ref, ck_sc, cv_sc):
        n = pl.program_id(1)

        @pl.when(n == 0)
        def _():
            ck_sc[...] = jnp.zeros_like(ck_sc)
            cv_sc[...] = jnp.zeros_like(cv_sc)

        @pl.when(n < nq)
        def _():
            valid = jnp.concatenate(_band_masks(T, n), axis=1)
            low = lax.broadcasted_iota(jnp.int32, (T, 128), 1) < HEAD_DIM
            dot = functools.partial(lax.dot_general, preferred_element_type=F32)

            def pair(h):
                return slice((h // 2) * 128, (h // 2 + 1) * 128)

            def products(h):
                ps = pair(h)
                mask = low if h % 2 == 0 else jnp.logical_not(low)
                qp = q_ref[:, ps] * jnp.asarray(ATTN_SCALE, BF)
                dop = do_ref[:, ps]
                qm = jnp.where(mask, qp, jnp.zeros_like(qp))
                dom = jnp.where(mask, dop, jnp.zeros_like(dop))
                keys = jnp.concatenate([kp_ref[:, ps], kc_ref[:, ps]], axis=0)
                values = jnp.concatenate([vp_ref[:, ps], vc_ref[:, ps]], axis=0)
                return qm, dom, jnp.where(valid, dot(qm, keys, nt), NEG_INF), dot(dom, values, nt)

            def pointwise(h, qm, dom, s, dp):
                ps = pair(h)
                mask = low if h % 2 == 0 else jnp.logical_not(low)
                prod = do_ref[:, ps].astype(F32) * o_ref[:, ps].astype(F32)
                delta = jnp.sum(jnp.where(mask, prod, 0.0), axis=1, keepdims=True)
                p = jnp.exp(s - lse_ref[:, h:h + 1])
                ds = (p * (dp - delta)).astype(BF)
                return qm, dom, p.astype(BF), ds

            def gradients(h, qm, dom, p, ds):
                ps = pair(h)
                keys = jnp.concatenate([kp_ref[:, ps], kc_ref[:, ps]], axis=0)
                dq = dot(ds, keys, (((1,), (0,)), ((), ())))
                dk, dv = dot(ds, qm, tn), dot(p, dom, tn)
                return dq, dk[:T], dv[:T], dk[T:], dv[T:]

            st1, st2, even = {}, {}, None
            for t in range(N_HEADS + 2):
                if t < N_HEADS:
                    st1[t] = products(t)
                done = gradients(t - 2, *st2.pop(t - 2)) if t >= 2 else None
                if 1 <= t <= N_HEADS:
                    st2[t - 1] = pointwise(t - 1, *st1.pop(t - 1))
                if done is not None:
                    h = t - 2
                    if h % 2 == 0:
                        even = done
                    else:
                        ps = pair(h)
                        dq_ref[:, ps] = (jnp.where(low, even[0], done[0]) * ATTN_SCALE).astype(BF)
                        dk_ref[:, ps] = (ck_sc[:, ps] + even[1] + done[1]).astype(BF)
                        dv_ref[:, ps] = (cv_sc[:, ps] + even[2] + done[2]).astype(BF)
                        ck_sc[:, ps] = even[3] + done[3]
                        cv_sc[:, ps] = even[4] + done[4]

        @pl.when(n == nq)
        def _():
            dk_ref[...] = ck_sc[...].astype(BF)
            dv_ref[...] = cv_sc[...].astype(BF)

    def cur(n):
        return jnp.minimum(n, nq - 1)

    def prev(n):
        return jnp.maximum(cur(n) - 1, 0)

    blk = lambda f, cb: pl.BlockSpec((T, 1024), lambda r, n: (f(n), cb(r)))
    own = lambda r: r
    in_specs, args, kernel = [blk(cur, qcb), blk(prev, kcb), blk(cur, kcb), blk(prev, vcb), blk(cur, vcb)], (qa, ka, ka, va, va), body
    if dil == 1 and qa is ka and ka is va:
        in_specs, args, kernel = _joined_qkv_specs(T, cur, prev), (qa, qa), _joined_qkv(body)
    outs = _hosted_call(
        kernel, grid=(dil, nq + 1),
        in_specs=in_specs + [blk(cur, own), blk(cur, own), pl.BlockSpec((T, 128), lambda r, n: (cur(n), r))],
        out_specs=[blk(cur, own), blk(lambda n: jnp.maximum(n - 1, 0), own), blk(lambda n: jnp.maximum(n - 1, 0), own)],
        out_shape=[jax.ShapeDtypeStruct((L, dil * 1024), BF)] * 3,
        scratch_shapes=[pltpu.VMEM((T, 1024), F32), pltpu.VMEM((T, 1024), F32)],
        args=args + (doa, oa, lsea), name=name, guest=guest)
    return outs if guest is None else (outs[:3], outs[3:])


FOX_T = 512
FOX_TQ = 512
FOX_TK = 512
FOX_TQ_BWD = 512
FOX_ROWS = 256


def _fox_fwd(qkv, cT, *, name, guest=None):
    S = qkv.shape[0]
    T, TK, R = FOX_TQ, FOX_TK, FOX_ROWS
    nq = S // T
    nt = (((1,), (1,)), ((), ()))
    chains = [(h, rh) for h in range(N_HEADS) for rh in range(T // R)]
    pairs = [(n, j) for n in range(nq) for j in range((n * T + T - 1) // TK + 1)]
    schedule = [jnp.asarray([p[i] for p in pairs], jnp.int32) for i in range(2)]

    def body(n_tab, j_tab, q_ref, k_ref, v_ref, ct_ref, o_ref, lse_ref, m_sc, l_sc, acc_sc):
        n = n_tab[pl.program_id(0)]
        j = j_tab[pl.program_id(0)]
        last_j = (n * T + T - 1) // TK
        lane = lax.broadcasted_iota(jnp.int32, (R, 128), 1)
        low = lane < HEAD_DIM
        ones = jnp.ones((TK, 128), BF)

        @pl.when(j == 0)
        def _():
            m_sc[...] = jnp.full(m_sc.shape, NEG_INF, F32)
            l_sc[...] = jnp.zeros_like(l_sc)
            acc_sc[...] = jnp.zeros_like(acc_sc)

        def step(diagonal):
            def pair(h):
                return slice((h // 2) * 128, (h // 2 + 1) * 128)

            def rows(rh):
                return slice(rh * R, (rh + 1) * R)

            def scores(h, rh):
                qp = q_ref[rows(rh), pair(h)] * jnp.asarray(ATTN_SCALE, BF)
                qm = jnp.where(low if h % 2 == 0 else jnp.logical_not(low), qp, jnp.zeros_like(qp))
                s = lax.dot_general(qm, k_ref[:, pair(h)], nt, preferred_element_type=F32) - ct_ref[h:h + 1, :]
                if diagonal:
                    ahead = lax.broadcasted_iota(jnp.int32, (R, TK), 1) - lax.broadcasted_iota(jnp.int32, (R, TK), 0)
                    s = jnp.where(ahead <= n * T + rh * R - j * TK, s, NEG_INF)
                return s

            def softmax(h, rh, s):
                m_prev = m_sc[h, rows(rh), :]
                m_new = jnp.maximum(m_prev, jnp.max(s, axis=1, keepdims=True))
                p = jnp.exp(s - jnp.concatenate([m_new] * (TK // 128), axis=1)).astype(BF)
                return m_new, jnp.exp(m_prev - m_new), p

            def weighted(h, p):
                vx = jnp.concatenate([v_ref[:, pair(h)], ones], axis=1)
                return jnp.dot(p, vx, preferred_element_type=F32)

            sc, pr, even = {}, {}, {}
            nch = len(chains)
            for t in range(nch + 2):
                if t < nch:
                    sc[t] = scores(*chains[t])
                done = None
                if t >= 2:
                    m_new, alpha, p = pr.pop(t - 2)
                    done = (m_new, alpha, weighted(chains[t - 2][0], p))
                if 1 <= t <= nch:
                    pr[t - 1] = softmax(*chains[t - 1], sc.pop(t - 1))
                if done is not None:
                    h, rh = chains[t - 2]
                    m_new, alpha, pv = done
                    m_sc[h, rows(rh), :] = m_new
                    l_sc[h, rows(rh), :] = alpha * l_sc[h, rows(rh), :] + pv[:, 128:]
                    if h % 2 == 0:
                        even[rh] = (alpha, pv[:, :128])
                    else:
                        a0, pv0 = even.pop(rh)
                        acc = acc_sc[h // 2, rows(rh), :]
                        acc_sc[h // 2, rows(rh), :] = jnp.where(low, a0 * acc + pv0, alpha * acc + pv[:, :128])

        @pl.when(j < last_j)
        def _():
            step(False)

        @pl.when(j == last_j)
        def _():
            step(True)
            lane_t = lax.broadcasted_iota(jnp.int32, (T, 128), 1)
            low_t = lane_t < HEAD_DIM
            lse = jnp.zeros((T, 128), F32)
            for h in range(N_HEADS):
                lse = jnp.where(lane_t == h, m_sc[h] + jnp.log(l_sc[h]), lse)
            lse_ref[...] = lse
            for hp in range(N_HEADS // 2):
                inv = jnp.where(low_t, 1.0 / l_sc[2 * hp], 1.0 / l_sc[2 * hp + 1])
                o_ref[:, hp * 128:(hp + 1) * 128] = (acc_sc[hp] * inv).astype(BF)

    outs = _hosted_call(
        body, grid=(len(pairs),),
        in_specs=[pl.BlockSpec((T, 1024), lambda t, n, j: (n[t], 0)), pl.BlockSpec((TK, 1024), lambda t, n, j: (j[t], 1)),
                  pl.BlockSpec((TK, 1024), lambda t, n, j: (j[t], 2)), pl.BlockSpec((GATE_LANES, TK), lambda t, n, j: (0, j[t]))],
        out_specs=[pl.BlockSpec((T, 1024), lambda t, n, j: (n[t], 0)), pl.BlockSpec((T, 128), lambda t, n, j: (n[t], 0))],
        out_shape=[jax.ShapeDtypeStruct((S, 1024), BF), jax.ShapeDtypeStruct((S, 128), F32)],
        scratch_shapes=[pltpu.VMEM((N_HEADS, T, 128), F32), pltpu.VMEM((N_HEADS, T, 128), F32),
                        pltpu.VMEM((N_HEADS // 2, T, 128), F32)],
        args=(qkv, qkv, qkv, cT), name=name, guest=guest, schedule=schedule)
    return outs if guest is None else (outs[:2], outs[2:])


def _fox_bwd(qkv, cT, do, o, lse, *, name, guest=None):
    S = qkv.shape[0]
    T, TQ, R = FOX_T, FOX_TQ_BWD, FOX_ROWS
    nk, nq = S // T, S // TQ
    nt = (((1,), (1,)), ((), ()))
    tn = (((0,), (0,)), ((), ()))
    nn = (((1,), (0,)), ((), ()))
    chains = [(h, rh) for h in range(N_HEADS) for rh in range(TQ // R)]
    dot = functools.partial(lax.dot_general, preferred_element_type=F32)
    pairs = [(kb, qb) for kb in range(nk) for qb in range(kb * T // TQ, nq)]
    schedule = [jnp.asarray([p[i] for p in pairs], jnp.int32) for i in range(2)]

    def body(kb_tab, qb_tab, q_ref, k_ref, v_ref, ct_ref, do_ref, o_ref, lse_ref, dq_ref, dk_ref, dv_ref, dct_ref, dcq_ref,
             dq_sc, dk_sc, dv_sc, dc_sc, dcq_sc):
        kb = kb_tab[pl.program_id(0)]
        qb = qb_tab[pl.program_id(0)]
        jq = qb - kb * T // TQ
        lane = lax.broadcasted_iota(jnp.int32, (R, 128), 1)
        low = lane < HEAD_DIM
        ones_k = jnp.ones((T, 128), BF)
        ones_r = jnp.ones((8, R), BF)

        @pl.when(jnp.logical_and(kb == 0, jq == 0))
        def _():
            dq_sc[...] = jnp.zeros_like(dq_sc)
            dcq_sc[...] = jnp.zeros_like(dcq_sc)

        @pl.when(jq == 0)
        def _():
            dk_sc[...] = jnp.zeros_like(dk_sc)
            dv_sc[...] = jnp.zeros_like(dv_sc)
            dc_sc[...] = jnp.zeros_like(dc_sc)

        def step(diagonal):
            def pair(h):
                return slice((h // 2) * 128, (h // 2 + 1) * 128)

            def rows(rh):
                return slice(rh * R, (rh + 1) * R)

            def qrows(rh):
                return pl.ds(pl.multiple_of(qb * TQ + rh * R, R), R)

            def products(h, rh):
                mask = low if h % 2 == 0 else jnp.logical_not(low)
                qp = q_ref[rows(rh), pair(h)] * jnp.asarray(ATTN_SCALE, BF)
                dop = do_ref[rows(rh), pair(h)]
                qm = jnp.where(mask, qp, jnp.zeros_like(qp))
                dom = jnp.where(mask, dop, jnp.zeros_like(dop))
                s = dot(qm, k_ref[:, pair(h)], nt) - ct_ref[h:h + 1, :]
                if diagonal:
                    ahead = lax.broadcasted_iota(jnp.int32, (R, T), 1) - lax.broadcasted_iota(jnp.int32, (R, T), 0)
                    s = jnp.where(ahead <= qb * TQ + rh * R - kb * T, s, NEG_INF)
                return qm, dom, s, dot(dom, v_ref[:, pair(h)], nt)

            def pointwise(h, rh, qm, dom, s, dp):
                mask = low if h % 2 == 0 else jnp.logical_not(low)
                prod = do_ref[rows(rh), pair(h)].astype(F32) * o_ref[rows(rh), pair(h)].astype(F32)
                delta = jnp.sum(jnp.where(mask, prod, 0.0), axis=1, keepdims=True)
                p = jnp.exp(s - lse_ref[rows(rh), h:h + 1])
                ds = (p * (dp - delta)).astype(BF)
                return qm, dom, p.astype(BF), ds

            def gradients(h, qm, dom, p, ds):
                kx = jnp.concatenate([k_ref[:, pair(h)], ones_k], axis=1)
                return dot(ds, kx, nn), dot(qm, ds, tn), dot(dom, p, tn), dot(ones_r, ds, nn)

            st1, st2, even = {}, {}, {}
            dcq_tiles = [jnp.zeros((R, 128), F32) for _ in range(TQ // R)]
            nch = len(chains)
            for t in range(nch + 2):
                if t < nch:
                    st1[t] = products(*chains[t])
                done = gradients(chains[t - 2][0], *st2.pop(t - 2)) if t >= 2 else None
                if 1 <= t <= nch:
                    st2[t - 1] = pointwise(*chains[t - 1], *st1.pop(t - 1))
                if done is not None:
                    h, rh = chains[t - 2]
                    dq_rsum, dk, dv, csum = done
                    dq = dq_rsum[:, :128]
                    dcq_tiles[rh] = jnp.where(lane == h, dq_rsum[:, 128:], dcq_tiles[rh])
                    dc_sc[h:h + 1, :] -= csum[0:1, :]
                    if h % 2 == 0:
                        even[rh] = (dq, dk, dv)
                    else:
                        dq0, dk0, dv0 = even.pop(rh)
                        dq_sc[qrows(rh), pair(h)] += jnp.where(low, dq0, dq) * ATTN_SCALE
                        dk_sc[h // 2] += dk0 + dk
                        dv_sc[h // 2] += dv0 + dv
            for rh in range(TQ // R):
                dcq_sc[qrows(rh), :] += dcq_tiles[rh]

        @pl.when(jq > 0)
        def _():
            step(False)

        @pl.when(jq == 0)
        def _():
            step(True)

        @pl.when(qb == nq - 1)
        def _():
            for hp in range(N_HEADS // 2):
                dk_ref[:, hp * 128:(hp + 1) * 128] = dk_sc[hp].T.astype(BF)
                dv_ref[:, hp * 128:(hp + 1) * 128] = dv_sc[hp].T.astype(BF)
            dct_ref[...] = dc_sc[...]

        @pl.when(jnp.logical_and(kb == nk - 1, qb == nq - 1))
        def _():
            def put(i, carry):
                r = pl.ds(pl.multiple_of(i * T, T), T)
                dq_ref[r, :] = dq_sc[r, :].astype(BF)
                return carry
            lax.fori_loop(0, nk, put, 0)
            dcq_ref[...] = dcq_sc[...]

    qblk = lambda col: pl.BlockSpec((TQ, 1024), lambda t, kb, qb: (qb[t], col))
    kblk = lambda col: pl.BlockSpec((T, 1024), lambda t, kb, qb: (kb[t], col))
    whole = pl.BlockSpec((S, 1024), lambda t, kb, qb: (0, 0))
    outs = _hosted_call(
        body, grid=(len(pairs),),
        in_specs=[qblk(0), kblk(1), kblk(2), pl.BlockSpec((GATE_LANES, T), lambda t, kb, qb: (0, kb[t])), qblk(0), qblk(0),
                  pl.BlockSpec((TQ, 128), lambda t, kb, qb: (qb[t], 0))],
        out_specs=[whole, kblk(0), kblk(0), pl.BlockSpec((GATE_LANES, T), lambda t, kb, qb: (0, kb[t])),
                   pl.BlockSpec((S, GATE_LANES), lambda t, kb, qb: (0, 0))],
        out_shape=[jax.ShapeDtypeStruct((S, 1024), BF)] * 3 + [jax.ShapeDtypeStruct((GATE_LANES, S), F32),
                                                               jax.ShapeDtypeStruct((S, GATE_LANES), F32)],
        scratch_shapes=[pltpu.VMEM((S, 1024), F32), pltpu.VMEM((N_HEADS // 2, 128, T), F32), pltpu.VMEM((N_HEADS // 2, 128, T), F32),
                        pltpu.VMEM((GATE_LANES, T), F32), pltpu.VMEM((S, GATE_LANES), F32)],
        args=(qkv, qkv, qkv, cT, do, o, lse), name=name, guest=guest, schedule=schedule)
    return outs if guest is None else (outs[:5], outs[5:])


def _to_natural(src_ref, buf, d, width):
    rows = buf.shape[1]
    for r in range(d):
        for ch in range(width // 128):
            lanes = slice(r * width + ch * 128, r * width + (ch + 1) * 128)
            buf.at[ch][pl.ds(r, rows // d, stride=d), :] = src_ref[:, lanes].astype(F32)
    return jnp.concatenate([buf[ch] for ch in range(width // 128)], axis=1)


def _to_view(val, buf, dst_ref, d, width):
    rows = buf.shape[1]
    for ch in range(width // 128):
        buf[ch] = val[:, ch * 128:(ch + 1) * 128]
    for r in range(d):
        for ch in range(width // 128):
            lanes = slice(r * width + ch * 128, r * width + (ch + 1) * 128)
            dst_ref[:, lanes] = buf.at[ch][pl.ds(r, rows // d, stride=d), :].astype(dst_ref.dtype)


def _view_spec(rows, d, width):
    return pl.BlockSpec((rows // d, d * width), lambda i, *_: (i, 0))


def _combine_groups(os, lses, dils, *, name):
    ng = len(os)
    S = os[0].shape[0] * dils[0]
    tm = ROW_TILE
    views = sorted(set(dils))

    def body(*refs):
        o_refs, l_refs = refs[:ng], refs[ng:2 * ng]
        outs = refs[2 * ng:2 * ng + 2 * len(views)]
        wide, narrow = refs[-2], refs[-1]
        ls = [l_refs[g][...] if dils[g] == 1 else _to_natural(l_refs[g], narrow, dils[g], 128) for g in range(ng)]
        m = functools.reduce(jnp.maximum, ls)
        es = [jnp.exp(l - m) for l in ls]
        den = functools.reduce(jnp.add, es)
        ws = [e / den for e in es]
        lse = m + jnp.log(den)
        og = [o_refs[g][...] if dils[g] == 1 else _to_natural(o_refs[g], wide, dils[g], 1024) for g in range(ng)]
        cols = []
        for h in range(N_HEADS):
            hs = slice(h * HEAD_DIM, (h + 1) * HEAD_DIM)
            acc = ws[0][:, h:h + 1] * og[0][:, hs]
            for g in range(1, ng):
                acc = acc + ws[g][:, h:h + 1] * og[g][:, hs]
            cols.append(acc)
        o = jnp.concatenate(cols, axis=1)
        for k, d in enumerate(views):
            if d == 1:
                outs[2 * k][...] = o.astype(BF)
                outs[2 * k + 1][...] = lse
            else:
                _to_view(o, wide, outs[2 * k], d, 1024)
                _to_view(lse, narrow, outs[2 * k + 1], d, 128)

    out_specs, out_shape = [], []
    for d in views:
        out_specs += [_view_spec(tm, d, 1024), _view_spec(tm, d, 128)]
        out_shape += [jax.ShapeDtypeStruct((S // d, d * 1024), BF), jax.ShapeDtypeStruct((S // d, d * 128), F32)]
    res = pl.pallas_call(
        body, grid=(S // tm,), in_specs=[_view_spec(tm, d, 1024) for d in dils] + [_view_spec(tm, d, 128) for d in dils],
        out_specs=out_specs, out_shape=out_shape,
        scratch_shapes=[pltpu.VMEM((8, tm, 128), F32), pltpu.VMEM((1, tm, 128), F32)],
        name=name, compiler_params=_params("parallel"),
    )(*os, *lses)
    return {d: (res[2 * k], res[2 * k + 1]) for k, d in enumerate(views)}


def _assemble(parts, rope_flags, rope, dils, *, name):
    n = len(parts)
    S = parts[0].shape[0] * dils[0]
    use_rope = any(rope_flags)
    tm = ROW_TILE

    def body(*refs):
        out_ref, natural = refs[-2], refs[-1]
        for b in range(n):
            cols = slice(b * 1024, (b + 1) * 1024)
            d = dils[b]
            val = refs[b][...].astype(F32) if d == 1 else _to_natural(refs[b], natural, d, 1024)
            if rope_flags[b]:
                cos_ref, sa_ref, sb_ref = refs[n:n + 3]
                val = _rope_rotate(val, cos_ref[...], sa_ref[...], sb_ref[...])
            out_ref[:, cols] = val.astype(BF)

    in_specs = [_view_spec(tm, d, 1024) for d in dils]
    args = list(parts)
    if use_rope:
        in_specs += [pl.BlockSpec((tm, 128), lambda i: (i, 0))] * 3
        args += list(rope)
    return pl.pallas_call(
        body, grid=(S // tm,), in_specs=in_specs, out_specs=pl.BlockSpec((tm, n * 1024), lambda i: (i, 0)),
        out_shape=jax.ShapeDtypeStruct((S, n * 1024), BF), scratch_shapes=[pltpu.VMEM((8, tm, 128), F32)],
        name=name, compiler_params=_params("parallel"),
    )(*args)


GATE_ROWS = 512


def _gate_fwd(z, bf, *, name):
    S = z.shape[0]

    def body(z_ref, b_ref, ct_ref, carry):
        i = pl.program_id(0)

        @pl.when(i == 0)
        def _():
            carry[...] = jnp.zeros_like(carry)

        zz = z_ref[...] + b_ref[...]
        logf = jnp.minimum(zz, 0.0) - jnp.log(1.0 + jnp.exp(-jnp.abs(zz)))
        tri = (lax.broadcasted_iota(jnp.int32, (GATE_ROWS, GATE_ROWS), 0)
               >= lax.broadcasted_iota(jnp.int32, (GATE_ROWS, GATE_ROWS), 1)).astype(F32)
        cs = jnp.dot(tri, logf, precision=lax.Precision.HIGHEST, preferred_element_type=F32) + carry[...]
        ct_ref[...] = cs.T
        carry[...] = cs[GATE_ROWS - 1:GATE_ROWS, :]

    return pl.pallas_call(
        body, grid=(S // GATE_ROWS,),
        in_specs=[pl.BlockSpec((GATE_ROWS, GATE_LANES), lambda i: (i, 0)), pl.BlockSpec((1, GATE_LANES), lambda i: (0, 0))],
        out_specs=pl.BlockSpec((GATE_LANES, GATE_ROWS), lambda i: (0, i)),
        out_shape=jax.ShapeDtypeStruct((GATE_LANES, S), F32),
        scratch_shapes=[pltpu.VMEM((1, GATE_LANES), F32)], name=name, compiler_params=_params("arbitrary"),
    )(z, bf)


def _gate_bwd(z, bf, dcT, dcq, *, name):
    S = z.shape[0]
    nb = S // GATE_ROWS

    def body(z_ref, b_ref, dct_ref, dcq_ref, dz_ref, db_ref, carry):
        i = pl.program_id(0)

        @pl.when(i == 0)
        def _():
            carry[...] = jnp.zeros_like(carry)
            db_ref[...] = jnp.zeros_like(db_ref)

        dc = dct_ref[...].T + dcq_ref[...]
        tri = (lax.broadcasted_iota(jnp.int32, (GATE_ROWS, GATE_ROWS), 0)
               <= lax.broadcasted_iota(jnp.int32, (GATE_ROWS, GATE_ROWS), 1)).astype(F32)
        dl = jnp.dot(tri, dc, precision=lax.Precision.HIGHEST, preferred_element_type=F32) + carry[...]
        carry[...] = dl[0:1, :]
        zz = z_ref[...] + b_ref[...]
        dz = dl * (1.0 / (1.0 + jnp.exp(zz)))
        lane = lax.broadcasted_iota(jnp.int32, dz.shape, 1)
        dz = jnp.where(lane < N_HEADS, dz, 0.0)
        dz_ref[...] = dz.astype(BF)
        db_ref[...] += jnp.sum(dz, axis=0, keepdims=True)

    return pl.pallas_call(
        body, grid=(nb,),
        in_specs=[pl.BlockSpec((GATE_ROWS, GATE_LANES), lambda i: (nb - 1 - i, 0)), pl.BlockSpec((1, GATE_LANES), lambda i: (0, 0)),
                  pl.BlockSpec((GATE_LANES, GATE_ROWS), lambda i: (0, nb - 1 - i)),
                  pl.BlockSpec((GATE_ROWS, GATE_LANES), lambda i: (nb - 1 - i, 0))],
        out_specs=[pl.BlockSpec((GATE_ROWS, GATE_LANES), lambda i: (nb - 1 - i, 0)), pl.BlockSpec((1, GATE_LANES), lambda i: (0, 0))],
        out_shape=[jax.ShapeDtypeStruct((S, GATE_LANES), BF), jax.ShapeDtypeStruct((1, GATE_LANES), F32)],
        scratch_shapes=[pltpu.VMEM((1, GATE_LANES), F32)], name=name, compiler_params=_params("arbitrary"),
    )(z, bf, dcT, dcq)


def _rope_tables(S):
    half = ROT_DIM // 2
    inv_freq = ROPE_THETA ** (-jnp.arange(half, dtype=F32) * 2.0 / ROT_DIM)
    ang = jnp.arange(S, dtype=F32)[:, None] * inv_freq[None, :]
    cos, sin = jnp.cos(ang), jnp.sin(ang)
    zero = jnp.zeros((S, HEAD_DIM - ROT_DIM), F32)
    zh = jnp.zeros((S, half), F32)
    cos_h = jnp.concatenate([cos, cos, jnp.ones_like(zero)], axis=1)
    sa_h = jnp.concatenate([-sin, zh, zero], axis=1)
    sb_h = jnp.concatenate([zh, sin, zero], axis=1)
    two = lambda t: jnp.concatenate([t, t], axis=1)
    return two(cos_h), two(sa_h), two(sb_h)


def _ffn_fwd(h, norm, w_gu, w_down, tag):
    n = _rms_fwd(h, norm, name=f"ffn{tag}_norm")
    gu = _mm_nn(n, w_gu, tm=2 * MM_ROWS, tn=1408, out_dtype=BF, name=f"ffn{tag}_gu")
    act = _swiglu_fwd(gu, name=f"ffn{tag}_act")
    out = _mm_nn(act, w_down, tm=MM_ROWS // 2, tn=1024, out_dtype=F32, name=f"ffn{tag}_down", resid=h)
    return out, (h, n, gu, act)


def _ffn_bwd(dh, dhb, saved, norm, w_gu, w_down, tag, ride=None):
    h, n, gu, act = saved
    dact = _mm_nt(dhb, w_down, tm=2 * MM_ROWS, to=1408, tn=1024, out_dtype=BF, name=f"ffn{tag}_dact")
    dw_down = _mm_tn(act, dhb, tk=1408, tn=1024, tm=2 * MM_ROWS, out_dtype=BF, name=f"ffn{tag}_dwdown")
    dgu = _swiglu_bwd(gu, dact, name=f"ffn{tag}_dgu")
    dn_call = lambda guest: _mm_nt(dgu, w_gu, tm=2 * MM_ROWS, to=1024, tn=1408, out_dtype=F32, name=f"ffn{tag}_dn", guest=guest)
    dn = dn_call(None) if ride is None else ride(dn_call)
    dw_gu = _mm_tn(n, dgu, tk=1024, tn=1408, tm=2 * MM_ROWS, out_dtype=BF, name=f"ffn{tag}_dwgu")
    dx, dxb, dg = _rms_bwd(h, norm, dn, dh, name=f"ffn{tag}_dnorm")
    return dx, dxb, dg, dw_gu, dw_down


def _local_step(x, tgt, w, mats, fetch, exchange):
    S = x.shape[0]
    rope_f = _rope_tables(S)
    rope_b = (rope_f[0], -rope_f[1], -rope_f[2])
    g, partial, landed = {}, {}, {}
    w = dict(w, ffn_w_gu={}, ffn_w_down={})

    def bring(call, indices):
        bufs = [mats[wi] for wi in indices]
        if fetch is None:
            return call(None), bufs
        return call(fetch(indices, bufs))

    def ride(call, indices):
        guest = exchange(indices, [partial[wi] for wi in indices]) if indices else None
        res = call(guest)
        if guest is None:
            return res
        res, outs = res
        landed.update(zip(indices, outs))
        return res

    n0 = _rms_fwd(x, w["a_norm"], name="a_norm")
    dils = [d for _, d in DILATED_PATTERNS]
    projs, (w["ffn_w_gu"][0], w["ffn_w_down"][0]) = bring(
        lambda guest: _mm_nn(n0, w["a_w_in"], tm=MM_ROWS, tn=1024, out_dtype=BF, name="a_proj", rope=rope_f, guest=guest,
                             groups=dils), [4, 6])
    block = lambda t, dil: (lambda r: t * dil + r)
    o_parts, lse_parts = [], []
    for gi, (window, dil) in enumerate(DILATED_PATTERNS):
        pv = projs[gi]
        attend = lambda guest: _band_fwd(pv, pv, pv, block(0, dil), block(1, dil), block(2, dil), dil=dil, T=128,
                                         window=window // dil, name=f"a_attn{gi}", guest=guest)
        if gi == 0:
            (o_g, lse_g), (w["a_w_out"],) = bring(attend, [1])
        elif gi == 1:
            (o_g, lse_g), (b_in,) = bring(attend, [2])
        else:
            (o_g, lse_g), (w["b_w_out"],) = bring(attend, [3])
        o_parts.append(o_g)
        lse_parts.append(lse_g)
    b_in = b_in.transpose(1, 0, 2).reshape(D_MODEL, -1)
    w["b_w_qkv"] = b_in[:, :QKV_COLS]
    w["b_w_f"] = jnp.pad(b_in[:, QKV_COLS:], ((0, 0), (0, GATE_LANES + QKV_COLS - b_in.shape[1])))
    mixed = _combine_groups(o_parts, lse_parts, dils, name="a_combine")
    o_a = mixed[1][0]
    h1 = _mm_nn(o_a, w["a_w_out"], tm=MM_ROWS, tn=1024, out_dtype=F32, name="a_out", resid=x)
    h2, ffn0 = _ffn_fwd(h1, w["ffn_norm"][0:1], w["ffn_w_gu"][0], w["ffn_w_down"][0], 0)

    n2 = _rms_fwd(h2, w["b_norm"], name="b_norm")
    qkv = _mm_nn(n2, w["b_w_qkv"], tm=MM_ROWS, tn=1024, out_dtype=BF, name="b_proj")
    zf = _mm_nn(n2, w["b_w_f"], tm=MM_ROWS, tn=GATE_LANES, out_dtype=F32, name="b_gate_proj")
    cT = _gate_fwd(zf, w["b_f"], name="b_gate")
    (o_b, lse_b), (w["ffn_w_gu"][1], w["ffn_w_down"][1]) = bring(lambda guest: _fox_fwd(qkv, cT, name="b_attn", guest=guest), [5, 7])
    h3 = _mm_nn(o_b, w["b_w_out"], tm=MM_ROWS, tn=1024, out_dtype=F32, name="b_out", resid=h2)
    h4, ffn1 = _ffn_fwd(h3, w["ffn_norm"][1:2], w["ffn_w_gu"][1], w["ffn_w_down"][1], 1)

    loss, dh4, dh4b, g["final_norm"] = _loss_head(h4, w["final_norm"], tgt, name="loss_head")

    dh3, dh3b, dg_f1, partial[5], partial[7] = _ffn_bwd(dh4, dh4b, ffn1, w["ffn_norm"][1:2], w["ffn_w_gu"][1], w["ffn_w_down"][1], 1)

    do_b = _mm_nt(dh3b, w["b_w_out"], tm=MM_ROWS, to=1024, tn=1024, out_dtype=BF, name="b_do")
    partial[3] = _mm_tn(o_b, dh3b, tk=1024, tn=1024, tm=MM_ROWS, out_dtype=BF, name="b_dwout")
    dq, dk, dv, dcT, dcq = ride(lambda guest: _fox_bwd(qkv, cT, do_b, o_b, lse_b, name="b_attn_bwd", guest=guest), [5, 7, 3])
    dz, g["b_f"] = _gate_bwd(zf, w["b_f"], dcT, dcq, name="b_gate_bwd")
    dqkv = _assemble([dq, dk, dv], [False] * 3, None, [1] * 3, name="b_dproj")
    dn2 = _mm_nt(dz, w["b_w_f"], tm=MM_ROWS, to=1024, tn=GATE_LANES, out_dtype=F32, name="b_dn_gate")
    dn2 = _mm_nt(dqkv, w["b_w_qkv"], tm=MM_ROWS, to=1024, tn=1024, out_dtype=F32, name="b_dn", add=dn2)
    g_qkv = _mm_tn(n2, dqkv, tk=1024, tn=1024, tm=MM_ROWS, out_dtype=BF, name="b_dwqkv")
    g_f = _mm_tn(n2, dz, tk=1024, tn=GATE_LANES, tm=MM_ROWS, out_dtype=BF, name="b_dwf")
    g_b_in = jnp.concatenate([g_qkv, g_f[:, :N_HEADS]], axis=1)
    partial[2] = g_b_in.reshape(D_MODEL, N_CHIPS, -1).transpose(1, 0, 2)
    dh2, dh2b, g["b_norm"] = _rms_bwd(h2, w["b_norm"], dn2, dh3, name="b_dnorm")

    dh1, dh1b, dg_f0, partial[4], partial[6] = _ffn_bwd(dh2, dh2b, ffn0, w["ffn_norm"][0:1], w["ffn_w_gu"][0], w["ffn_w_down"][0], 0,
                                                      ride=lambda call: ride(call, [2]))
    g["ffn_norm"] = jnp.concatenate([dg_f0, dg_f1], axis=0)

    views = tuple(sorted(set(dils)))
    do_a = dict(zip(views, _mm_nt(dh1b, w["a_w_out"], tm=MM_ROWS, to=1024, tn=1024, out_dtype=BF, name="a_do", views=views)))
    partial[1] = _mm_tn(o_a, dh1b, tk=1024, tn=1024, tm=MM_ROWS, out_dtype=BF, name="a_dwout")
    riders = {0: [4], 1: [6, 1], 2: []}
    parts = []
    for gi, (window, dil) in enumerate(DILATED_PATTERNS):
        pv = projs[gi]
        res = ride(lambda guest: _band_bwd(pv, pv, pv, block(0, dil), block(1, dil), block(2, dil), do_a[dil],
                                           mixed[dil][0], mixed[dil][1], dil=dil, T=128,
                                           window=window // dil, name=f"a_attn_bwd{gi}", guest=guest), riders[gi])
        parts += list(res)
    dproj = _assemble(parts, [True, True, False] * 3, rope_b, [d for _, d in DILATED_PATTERNS for _ in range(3)], name="a_dproj")
    partial[0] = _mm_tn(n0, dproj, tk=1024, tn=1024, tm=2 * MM_ROWS, out_dtype=BF, name="a_dwin")
    dn0 = ride(lambda guest: _mm_nt(dproj, w["a_w_in"], tm=2 * MM_ROWS, to=1024, tn=1024, out_dtype=F32, name="a_dn", guest=guest), [0])
    dx, _, g["a_norm"] = _rms_bwd(x, w["a_norm"], dn0, dh1, name="a_dnorm")
    return loss, dx, g, partial, landed


ANY = pl.BlockSpec(memory_space=pl.ANY)


def _place():
    x, y, c = lax.axis_index("x"), lax.axis_index("y"), lax.axis_index("c")
    chips = [(1 - x, y), (x, 1 - y), (1 - x, 1 - y)]
    return x, y, c, chips


def _shard_slice(ref, kind, rows, cols, s, half):
    hr = rows // 2
    if kind == "col":
        return ref.at[pl.ds(half * hr, hr), pl.ds(pl.multiple_of(s * cols, 128), cols)]
    if kind == "row":
        return ref.at[pl.ds(pl.multiple_of(s * rows + half * hr, 16), hr), :]
    return ref.at[s, pl.ds(half * hr, hr), :]


def _whole_shape(kind, rows, cols):
    return {"col": (rows, N_CHIPS * cols), "row": (N_CHIPS * rows, cols), "stack": (N_CHIPS, rows, cols)}[kind]


def _own_block(kind, rows, tr, cols):
    per = rows // tr

    def spec(half_rows):
        off = (lambda p: 0) if half_rows is None else (lambda p: p[1] * (half_rows // tr))
        if kind == "col":
            return pl.BlockSpec((tr, cols), lambda i, p: (off(p) + i, p[0]))
        if kind == "row":
            return pl.BlockSpec((tr, cols), lambda i, p: (p[0] * per + off(p) + i, 0))
        return pl.BlockSpec((None, tr, cols), lambda i, p: (p[0], off(p) + i, 0))
    return spec


def _place_shard(shards, layer, kind, place, *, name):
    _, rows, cols = shards.shape
    tr = 256 if rows % 256 == 0 else rows // 2

    def body(p_ref, s_ref, o_ref):
        o_ref[...] = s_ref[...].astype(BF)

    return pl.pallas_call(
        body,
        grid_spec=pltpu.PrefetchScalarGridSpec(
            num_scalar_prefetch=1, grid=(rows // tr,),
            in_specs=[pl.BlockSpec((None, tr, cols), lambda i, p: (layer, i, 0))],
            out_specs=_own_block(kind, rows, tr, cols)(None)),
        out_shape=jax.ShapeDtypeStruct(_whole_shape(kind, rows, cols), BF),
        name=name, compiler_params=_params("arbitrary"),
    )(place, shards)


def _gather_weights(placed, kinds, dims):
    nw = len(placed)

    def body(*refs):
        dst = refs[nw:2 * nw]
        send_sems, recv_sems = refs[2 * nw:]
        x, y, c, chips = _place()
        me = 2 * x + y
        sibling = (x, y, 1 - c)

        def copy(wi, k, s, half, to):
            p = _shard_slice(dst[wi], kinds[wi], dims[wi][0], dims[wi][1], s, half)
            return pltpu.make_async_remote_copy(src_ref=p, dst_ref=p, send_sem=send_sems.at[wi * 6 + k],
                                                recv_sem=recv_sems.at[wi * 6 + k], device_id=to, device_id_type=MESH)

        first, passed = [], []
        for wi in range(nw):
            for j, chip in enumerate(chips):
                cp = copy(wi, j, me, c, (*chip, c))
                cp.start()
                first.append(cp)
        for wi in range(nw):
            for j, chip in enumerate(chips):
                s = 2 * chip[0] + chip[1]
                copy(wi, j, s, c, (x, y, c)).wait_recv()
                cp = copy(wi, 3 + j, s, c, sibling)
                cp.start()
                passed.append(cp)
        for wi in range(nw):
            for j, chip in enumerate(chips):
                s = 2 * chip[0] + chip[1]
                copy(wi, 3 + j, s, 1 - c, (x, y, c)).wait_recv()
        for cp in first + passed:
            cp.wait_send()

    return pl.pallas_call(
        body, in_specs=[ANY] * nw, out_specs=[ANY] * nw,
        out_shape=[jax.ShapeDtypeStruct(p.shape, p.dtype) for p in placed],
        input_output_aliases={wi: wi for wi in range(nw)},
        scratch_shapes=[pltpu.SemaphoreType.DMA((nw * 6,)), pltpu.SemaphoreType.DMA((nw * 6,))],
        name="gather_weights",
    )(*placed)


def _fetch_guest(placed, kinds, dims):
    nw = len(placed)

    def copies(dst, send_sems, recv_sems, incoming):
        x, y, c, chips = _place()
        out = []
        for wi in range(nw):
            for j, chip in enumerate(chips):
                s = 2 * chip[0] + chip[1] if incoming else 2 * x + y
                to = (x, y, c) if incoming else (*chip, c)
                for half in range(2):
                    p = _shard_slice(dst[wi], kinds[wi], dims[wi][0], dims[wi][1], s, half)
                    k = wi * 6 + 2 * j + half
                    out.append(pltpu.make_async_remote_copy(src_ref=p, dst_ref=p, send_sem=send_sems.at[k],
                                                            recv_sem=recv_sems.at[k], device_id=to, device_id_type=MESH))
        return out

    def start(src, dst, sems):
        for cp in copies(dst, sems[0], sems[1], False):
            cp.start()

    def finish(src, dst, sems):
        for cp in copies(dst, sems[0], sems[1], True):
            cp.wait_recv()
        for cp in copies(dst, sems[0], sems[1], False):
            cp.wait_send()

    return dict(args=list(placed), out_shape=[jax.ShapeDtypeStruct(p.shape, p.dtype) for p in placed],
                scratch=[pltpu.SemaphoreType.DMA((nw * 6,)), pltpu.SemaphoreType.DMA((nw * 6,))],
                start=start, finish=finish, in_place=True)


def _scatter_guest(partials, kinds, dims):
    nw = len(partials)

    def copies(src, send_sems, recv_sems, dst):
        x, y, c, chips = _place()
        me = 2 * x + y
        out = []
        for wi in range(nw):
            rows, cols = dims[wi]

            def part(s, half, wi=wi, rows=rows, cols=cols):
                return _shard_slice(src[wi], kinds[wi], rows, cols, s, half)

            for j, chip in enumerate(chips):
                s = 2 * chip[0] + chip[1]
                for half in range(2):
                    slot = 2 * j + (c if half == 0 else 1 - c)
                    out.append(pltpu.make_async_remote_copy(
                        src_ref=part(s, half), dst_ref=dst[wi].at[slot],
                        send_sem=send_sems.at[wi * 7 + 2 * j + half], recv_sem=recv_sems.at[wi * 7 + slot],
                        device_id=(*chip, half), device_id_type=MESH))
            out.append(pltpu.make_async_remote_copy(
                src_ref=part(me, 1 - c), dst_ref=dst[wi].at[6],
                send_sem=send_sems.at[wi * 7 + 6], recv_sem=recv_sems.at[wi * 7 + 6],
                device_id=(x, y, 1 - c), device_id_type=MESH))
        return out

    def start(src, dst, sems):
        for cp in copies(src, sems[0], sems[1], dst):
            cp.start()

    def finish(src, dst, sems):
        x, y, c, _ = _place()
        for wi in range(nw):
            for slot in range(7):
                pltpu.make_async_remote_copy(
                    src_ref=dst[wi].at[slot], dst_ref=dst[wi].at[slot],
                    send_sem=sems[0].at[wi * 7 + slot], recv_sem=sems[1].at[wi * 7 + slot],
                    device_id=(x, y, c), device_id_type=MESH).wait_recv()
        for cp in copies(src, sems[0], sems[1], dst):
            cp.wait_send()

    return dict(args=list(partials), out_shape=[jax.ShapeDtypeStruct((7, d[0] // 2, d[1]), BF) for d in dims],
                scratch=[pltpu.SemaphoreType.DMA((nw * 7,)), pltpu.SemaphoreType.DMA((nw * 7,))],
                start=start, finish=finish)


def _sum_slots(slots, partial, kind, dims, place, *, name, into=None, layer=None, n_layers=1):
    rows, cols = dims
    hr = rows // 2
    tr = hr if 8 * hr * cols * 2 <= 6 * 1024 * 1024 else 128
    assert hr % tr == 0

    def body(p_ref, b_ref, own_ref, *rest):
        o_ref = rest[-1]
        acc = own_ref[...].astype(F32)
        for k in range(7):
            acc = acc + b_ref[k].astype(F32)
        o_ref[...] = acc

    half = lambda p: p[1] * (hr // tr)
    if n_layers == 1:
        out_spec = pl.BlockSpec((tr, cols), lambda i, p: (half(p) + i, 0))
        out_shape = jax.ShapeDtypeStruct((rows, cols), F32)
    else:
        out_spec = pl.BlockSpec((None, tr, cols), lambda i, p: (layer, half(p) + i, 0))
        out_shape = jax.ShapeDtypeStruct((n_layers, rows, cols), F32)
    in_specs = [pl.BlockSpec((7, tr, cols), lambda i, p: (0, i, 0)), _own_block(kind, rows, tr, cols)(hr)]
    args = [place, slots, partial]
    aliases = {}
    if into is not None:
        in_specs.append(ANY)
        args.append(into)
        aliases = {3: 0}
    return pl.pallas_call(
        body,
        grid_spec=pltpu.PrefetchScalarGridSpec(num_scalar_prefetch=1, grid=(hr // tr,), in_specs=in_specs, out_specs=out_spec),
        out_shape=out_shape, input_output_aliases=aliases, name=name, compiler_params=_params("arbitrary"),
    )(*args)


def _pair_exchange(bufs, members):
    nw = len(members)

    def body(*refs):
        dst = refs[len(bufs):2 * len(bufs)]
        send_sems, recv_sems = refs[2 * len(bufs):]
        x, y, c, _ = _place()

        def rows_of(wi, half):
            bi, l = members[wi]
            ref = dst[bi] if l is None else dst[bi].at[l]
            hr = ref.shape[0] // 2
            return ref.at[pl.ds(pl.multiple_of(half * hr, 8), hr), :]

        def copy(wi, half, to):
            p = rows_of(wi, half)
            return pltpu.make_async_remote_copy(src_ref=p, dst_ref=p, send_sem=send_sems.at[wi], recv_sem=recv_sems.at[wi],
                                                device_id=to, device_id_type=MESH)

        sent = []
        for wi in range(nw):
            cp = copy(wi, c, (x, y, 1 - c))
            cp.start()
            sent.append(cp)
        for wi in range(nw):
            copy(wi, 1 - c, (x, y, c)).wait_recv()
        for cp in sent:
            cp.wait_send()

    return pl.pallas_call(
        body, in_specs=[ANY] * len(bufs), out_specs=[ANY] * len(bufs),
        out_shape=[jax.ShapeDtypeStruct(b.shape, b.dtype) for b in bufs],
        input_output_aliases={i: i for i in range(len(bufs))},
        scratch_shapes=[pltpu.SemaphoreType.DMA((nw,)), pltpu.SemaphoreType.DMA((nw,))],
        name="pair_exchange",
    )(*bufs)


SMALL_ROWS = 8


def _allreduce_small(v, *, name):
    assert v.shape == (SMALL_ROWS, D_MODEL)

    def body(v_ref, o_ref, buf, send_sems, recv_sems):
        x, y, c, _ = _place()
        me = 4 * x + 2 * y + c
        buf[me] = v_ref[...]
        sent = []
        for k in range(1, 8):
            bx, by, bc = (k >> 2) & 1, (k >> 1) & 1, k & 1
            peer = (1 - x if bx else x, 1 - y if by else y, 1 - c if bc else c)
            cp = pltpu.make_async_remote_copy(src_ref=v_ref, dst_ref=buf.at[me], send_sem=send_sems.at[k - 1],
                                              recv_sem=recv_sems.at[k - 1], device_id=peer, device_id_type=MESH)
            cp.start()
            sent.append(cp)
        for k in range(1, 8):
            bx, by, bc = (k >> 2) & 1, (k >> 1) & 1, k & 1
            peer = 4 * (1 - x if bx else x) + 2 * (1 - y if by else y) + (1 - c if bc else c)
            pltpu.make_async_remote_copy(src_ref=v_ref, dst_ref=buf.at[peer], send_sem=send_sems.at[k - 1],
                                         recv_sem=recv_sems.at[k - 1], device_id=(x, y, c), device_id_type=MESH).wait_recv()
        for cp in sent:
            cp.wait_send()
        acc = buf[0]
        for d in range(1, 8):
            acc = acc + buf[d]
        o_ref[...] = acc

    vmem = pl.BlockSpec(memory_space=pltpu.VMEM)
    return pl.pallas_call(
        body, in_specs=[vmem], out_specs=vmem, out_shape=jax.ShapeDtypeStruct(v.shape, F32),
        scratch_shapes=[pltpu.VMEM((8,) + v.shape, F32), pltpu.SemaphoreType.DMA((7,)), pltpu.SemaphoreType.DMA((7,))],
        name=name,
    )(v)


def _adamw(w, g, m, v, *, name):
    R, C = w.shape
    tr = R
    if R * C * 4 > 1024 * 1024:
        tr = max(t for t in range(8, R, 8) if R % t == 0 and t * C * 4 <= 1024 * 1024)

    def body(w_ref, g_ref, m_ref, v_ref, d_ref, m2_ref, v2_ref):
        gg = g_ref[...]
        m2 = ADAM_B1 * m_ref[...] + (1.0 - ADAM_B1) * gg
        v2 = ADAM_B2 * v_ref[...] + (1.0 - ADAM_B2) * jnp.square(gg)
        m_hat = m2 / (1.0 - ADAM_B1 ** ADAM_STEP)
        v_hat = v2 / (1.0 - ADAM_B2 ** ADAM_STEP)
        d_ref[...] = -ADAM_LR * (m_hat / (jnp.sqrt(v_hat) + ADAM_EPS) + ADAM_WD * w_ref[...])
        m2_ref[...] = m2
        v2_ref[...] = v2

    blk = pl.BlockSpec((tr, C), lambda i: (i, 0))
    out = jax.ShapeDtypeStruct((R, C), F32)
    return pl.pallas_call(
        body, grid=(R // tr,), in_specs=[blk] * 4, out_specs=[blk] * 3, out_shape=[out] * 3,
        name=name, compiler_params=_params("parallel"),
    )(w, g, m, v)


WEIGHT_ORDER = ("a_norm", "a_w_in", "a_w_out", "b_norm", "b_w_in", "b_f", "b_w_out", "ffn_norm", "ffn_w_gu",
                "ffn_w_down", "final_norm")
MATRICES = (("a_w_in", 0, "col"), ("a_w_out", 0, "row"), ("b_w_in", 0, "stack"), ("b_w_out", 0, "row"),
            ("ffn_w_gu", 0, "col"), ("ffn_w_gu", 1, "col"), ("ffn_w_down", 0, "row"), ("ffn_w_down", 1, "row"))
MATRIX_GROUPS = ([0], [1], [2], [3], [4, 5], [6, 7])
GROUP_NAMES = ("a_w_in", "a_w_out", "b_w_in", "b_w_out", "ffn_w_gu", "ffn_w_down")
QKV_COLS = 3 * N_HEADS * HEAD_DIM


def kernel(x, a_norm, a_w_in, a_w_out, b_norm, b_w_in, b_f, b_w_out, ffn_norm, ffn_w_gu, ffn_w_down, final_norm, loss_target, m_a_norm, m_a_w_in, m_a_w_out, m_b_norm, m_b_w_in, m_b_f, m_b_w_out, m_ffn_norm, m_ffn_w_gu, m_ffn_w_down, m_final_norm, v_a_norm, v_a_w_in, v_a_w_out, v_b_norm, v_b_w_in, v_b_f, v_b_w_out, v_ffn_norm, v_ffn_w_gu, v_ffn_w_down, v_final_norm):
    given = dict(a_norm=a_norm, a_w_in=a_w_in, a_w_out=a_w_out, b_norm=b_norm, b_w_in=b_w_in, b_f=b_f, b_w_out=b_w_out,
                 ffn_norm=ffn_norm, ffn_w_gu=ffn_w_gu, ffn_w_down=ffn_w_down, final_norm=final_norm)
    mom_m = dict(a_norm=m_a_norm, a_w_in=m_a_w_in, a_w_out=m_a_w_out, b_norm=m_b_norm, b_w_in=m_b_w_in, b_f=m_b_f,
                 b_w_out=m_b_w_out, ffn_norm=m_ffn_norm, ffn_w_gu=m_ffn_w_gu, ffn_w_down=m_ffn_w_down, final_norm=m_final_norm)
    mom_v = dict(a_norm=v_a_norm, a_w_in=v_a_w_in, a_w_out=v_a_w_out, b_norm=v_b_norm, b_w_in=v_b_w_in, b_f=v_b_f,
                 b_w_out=v_b_w_out, ffn_norm=v_ffn_norm, ffn_w_gu=v_ffn_w_gu, ffn_w_down=v_ffn_w_down, final_norm=v_final_norm)
    chip = 2 * lax.axis_index("x") + lax.axis_index("y")
    core = lax.axis_index("c")
    bn_cols = b_norm.shape[1]

    placed = lax.dynamic_update_slice(jnp.zeros((SMALL_ROWS, D_MODEL), F32), b_norm, (0, chip * bn_cols))
    placed = placed * (core == 0).astype(F32)
    b_norm_full = _allreduce_small(placed, name="gather_b_norm")[0:1]

    place = jnp.stack([chip, core]).astype(jnp.int32)
    kinds = [k for _, _, k in MATRICES]
    dims = [given[n].shape[1:] for n, _, _ in MATRICES]
    placed = [_place_shard(given[n], l, k, place, name=f"place_{n}{l}") for n, l, k in MATRICES]
    first = _gather_weights(placed[:1], kinds[:1], dims[:1])
    mats = dict(enumerate(list(first) + placed[1:]))
    gate_cols = b_f.shape[1]
    w = dict(a_norm=a_norm, a_w_in=mats[0], b_norm=b_norm_full,
             b_f=jnp.pad(b_f, ((0, 0), (0, GATE_LANES - gate_cols))), ffn_norm=ffn_norm,
             final_norm=final_norm.reshape(1, D_MODEL))

    def fetch(indices, bufs):
        return _fetch_guest(bufs, [kinds[i] for i in indices], [dims[i] for i in indices])

    def exchange(indices, parts):
        return _scatter_guest(parts, [kinds[i] for i in indices], [dims[i] for i in indices])

    loss, dx, g, partials, slots = _local_step(x[0], loss_target[0], w, mats, fetch, exchange)
    bufs, members = [], []
    for group in MATRIX_GROUPS:
        buf = None
        for l, wi in enumerate(group):
            n = MATRICES[wi][0]
            buf = _sum_slots(slots[wi], partials[wi], kinds[wi], dims[wi], place, name=f"sum_{n}{l}", into=buf,
                             layer=l, n_layers=len(group))
            members.append((len(bufs), l if len(group) > 1 else None))
        bufs.append(buf)
    reduced = dict(zip(GROUP_NAMES, _pair_exchange(bufs, members)))

    small = jnp.concatenate([g["a_norm"], g["b_norm"], g["ffn_norm"], g["final_norm"],
                             jnp.pad(g["b_f"], ((0, 0), (0, D_MODEL - GATE_LANES))),
                             jnp.zeros((SMALL_ROWS - 6, D_MODEL), F32)], axis=0)
    small = _allreduce_small(small, name="allreduce_small")
    grads = dict(reduced)
    grads["a_norm"] = small[0:1]
    grads["b_norm"] = lax.dynamic_slice(small, (1, chip * bn_cols), (1, bn_cols))
    grads["ffn_norm"] = small[2:4]
    grads["final_norm"] = small[4]
    grads["b_f"] = small[5:6, :gate_cols]

    out_g, out_d, out_m, out_v = [], [], [], []
    for n in WEIGHT_ORDER:
        shape = given[n].shape
        two_d = (1, shape[0]) if len(shape) == 1 else (-1, shape[-1])
        d, m2, v2 = _adamw(given[n].reshape(two_d), grads[n].reshape(two_d), mom_m[n].reshape(two_d),
                           mom_v[n].reshape(two_d), name=f"adamw_{n}")
        out_g.append(grads[n].reshape(shape))
        out_d.append(d.reshape(shape))
        out_m.append(m2.reshape(shape))
        out_v.append(v2.reshape(shape))

    total = lax.psum(loss[0, 0], MESH_AXES)
    return (total, dx[None], *out_g, *out_d, *out_m, *out_v)
```
